```python
import jax
import jax.numpy as jnp
from jax import lax
import numpy as np

D_MODEL = 1024
BATCH = 8
SEQ = 2048
DEPTH = 2

N_A_LAYERS = DEPTH // 2
N_B_LAYERS = DEPTH - N_A_LAYERS

A_HEAD_DIM = 128
A_HEADS = D_MODEL // A_HEAD_DIM
A_WIDTH = A_HEADS * A_HEAD_DIM
A_CONV = 4
A_CHUNK = 64
A_PROJ = 4 * A_WIDTH + 2 * A_HEADS

B_HEAD_DIM = 64
B_Q_HEADS = D_MODEL // B_HEAD_DIM
B_KV_HEADS = max(1, B_Q_HEADS // 8)
B_GROUP = B_Q_HEADS // B_KV_HEADS
WINDOW = 128
ROPE_DIM = B_HEAD_DIM // 4
ROPE_THETA = 500000.0

N_EXPERTS = 32
TOP_K = 4
D_FF = D_MODEL
SWIGLU_LIMIT = 7.0
SWIGLU_ALPHA = 1.702
MOE_BLOCK = 128

EPS = 1e-6
F32 = jnp.float32

kernel_name = "yoco_gdn_swa_sink_moe_adaln"


def rms_norm(x, gain):
    xf = x.astype(F32)
    y = xf * lax.rsqrt(jnp.mean(xf * xf, axis=-1, keepdims=True) + EPS)
    return (y * gain.astype(F32)).astype(x.dtype)


def ada_norm(x, gain, shift, scale):
    return rms_norm(x, gain) * (1.0 + scale[:, None, :]) + shift[:, None, :]


def l2_norm(x):
    xf = x.astype(F32)
    return (xf * lax.rsqrt(jnp.sum(xf * xf, axis=-1, keepdims=True) + EPS)).astype(x.dtype)


def causal_depthwise_conv(x, w):
    k_len = w.shape[0]
    s_len = x.shape[1]
    xp = jnp.pad(x, ((0, 0), (k_len - 1, 0), (0, 0)))
    return sum(xp[:, i:i + s_len] * w[i] for i in range(k_len))


def partial_rope(x, positions):
    inv_freq = ROPE_THETA ** (-jnp.arange(0, ROPE_DIM, 2, dtype=F32) / ROPE_DIM)
    ang = positions.astype(F32)[..., None] * inv_freq
    cos = jnp.cos(ang)[:, :, None, :]
    sin = jnp.sin(ang)[:, :, None, :]
    x1, x2 = jnp.split(x[..., :ROPE_DIM].astype(F32), 2, axis=-1)
    rot = jnp.concatenate([x1 * cos - x2 * sin, x2 * cos + x1 * sin], axis=-1)
    return jnp.concatenate([rot.astype(x.dtype), x[..., ROPE_DIM:]], axis=-1)


def chunk_gated_delta_rule(q, k, v, g, beta):
    b_sz, s_len, n_h, d_k = q.shape
    d_v = v.shape[-1]
    n_c = s_len // A_CHUNK

    def chunks(t):
        t = t.astype(F32).reshape(b_sz, n_c, A_CHUNK, n_h, *t.shape[3:])
        return jnp.moveaxis(t, (1, 3), (0, 2))

    q, k, v, g, beta = map(chunks, (q, k, v, g, beta))
    gc = jnp.cumsum(g, axis=-1)
    idx = jnp.arange(A_CHUNK)
    incl = idx[:, None] >= idx[None, :]
    strict = idx[:, None] > idx[None, :]
    decay = jnp.exp(jnp.where(incl, gc[..., :, None] - gc[..., None, :], -jnp.inf))
    kb = k * beta[..., None]
    a_mat = jnp.where(strict, jnp.einsum('nbhid,nbhjd->nbhij', kb, k) * decay, 0.0)
    rhs = jnp.concatenate([v * beta[..., None], kb * jnp.exp(gc)[..., None]], axis=-1)
    sol = lax.linalg.triangular_solve(a_mat, rhs, left_side=True, lower=True, unit_diagonal=True)
    u, w = sol[..., :d_v], sol[..., d_v:]
    qk = jnp.einsum('nbhid,nbhjd->nbhij', q, k) * decay
    q_dec = q * jnp.exp(gc)[..., None]
    k_tail = k * jnp.exp(gc[..., -1:] - gc)[..., None]
    g_tot = jnp.exp(gc[..., -1])

    def step(state, xs):
        q_c, kt_c, u_c, w_c, qk_c, gt_c = xs
        v_new = u_c - jnp.einsum('bhcd,bhde->bhce', w_c, state)
        o_c = jnp.einsum('bhcd,bhde->bhce', q_c, state) + jnp.einsum('bhij,bhje->bhie', qk_c, v_new)
        state = state * gt_c[..., None, None] + jnp.einsum('bhcd,bhce->bhde', kt_c, v_new)
        return state, o_c

    s0 = jnp.zeros((b_sz, n_h, d_k, d_v), F32)
    _, o = lax.scan(step, s0, (q_dec, k_tail, u, w, qk, g_tot))
    return jnp.moveaxis(o, (0, 2), (1, 3)).reshape(b_sz, s_len, n_h, d_v)


def gated_deltanet(h, w_in, conv_w, a_log, dt_bias, out_gain, w_out):
    b_sz, s_len, _ = h.shape
    proj = h @ w_in
    qkv, z, a, b = jnp.split(proj, [3 * A_WIDTH, 4 * A_WIDTH, 4 * A_WIDTH + A_HEADS], axis=-1)
    qkv = jax.nn.silu(causal_depthwise_conv(qkv, conv_w))
    q, k, v = [t.reshape(b_sz, s_len, A_HEADS, A_HEAD_DIM) for t in jnp.split(qkv, 3, axis=-1)]
    q = l2_norm(q) * (A_HEAD_DIM ** -0.5)
    k = l2_norm(k)
    beta = jax.nn.sigmoid(b.astype(F32))
    g = -jnp.exp(a_log.astype(F32)) * jax.nn.softplus(a.astype(F32) + dt_bias.astype(F32))
    o = chunk_gated_delta_rule(q, k, v, g, beta).astype(h.dtype)
    o = rms_norm(o, out_gain) * jax.nn.silu(z.reshape(b_sz, s_len, A_HEADS, A_HEAD_DIM))
    return o.reshape(b_sz, s_len, A_WIDTH) @ w_out


def shared_kv(x, gain, shift, scale, w_kv, k_gain, positions):
    b_sz, s_len, _ = x.shape
    h = ada_norm(x, gain, shift, scale)
    k, v = jnp.split(h @ w_kv, 2, axis=-1)
    k = k.reshape(b_sz, s_len, B_KV_HEADS, B_HEAD_DIM)
    v = v.reshape(b_sz, s_len, B_KV_HEADS, B_HEAD_DIM)
    k = partial_rope(rms_norm(k, k_gain), positions)
    return k, v


def swa_sink_attention(h, k, v, w_q, q_gain, sinks, w_out, positions):
    b_sz, s_len, _ = h.shape
    n_b = s_len // WINDOW
    q = (h @ w_q).reshape(b_sz, s_len, B_Q_HEADS, B_HEAD_DIM)
    q = partial_rope(rms_norm(q, q_gain), positions)
    qb = q.reshape(b_sz, n_b, WINDOW, B_KV_HEADS, B_GROUP, B_HEAD_DIM)
    kb = k.reshape(b_sz, n_b, WINDOW, B_KV_HEADS, B_HEAD_DIM)
    vb = v.reshape(b_sz, n_b, WINDOW, B_KV_HEADS, B_HEAD_DIM)

    def with_prev(t):
        prev = jnp.concatenate([jnp.zeros_like(t[:, :1]), t[:, :-1]], axis=1)
        return jnp.concatenate([prev, t], axis=2)

    k_band, v_band = with_prev(kb), with_prev(vb)
    s = jnp.einsum('bnqhgd,bnkhd->bnhgqk', qb, k_band, preferred_element_type=F32) * (B_HEAD_DIM ** -0.5)
    qi = jnp.arange(WINDOW)[:, None] + WINDOW
    kj = jnp.arange(2 * WINDOW)[None, :]
    band = (kj <= qi) & (qi - kj < WINDOW)
    has_prev = (jnp.arange(n_b) > 0)[:, None, None] | (kj >= WINDOW)[None]
    mask = band[None] & has_prev
    s = jnp.where(mask[None, :, None, None], s, -jnp.inf)
    sink = sinks.astype(F32).reshape(B_KV_HEADS, B_GROUP)[None, None, :, :, None, None]
    m = jnp.maximum(jnp.max(s, axis=-1, keepdims=True), sink)
    p = jnp.exp(s - m)
    p = p / (jnp.sum(p, axis=-1, keepdims=True) + jnp.exp(sink - m))
    o = jnp.einsum('bnhgqk,bnkhd->bnqhgd', p.astype(v.dtype), v_band)
    return o.reshape(b_sz, s_len, B_Q_HEADS * B_HEAD_DIM) @ w_out


def moe_ffn(h, router_w, router_b, up_w, up_b, down_w, down_b):
    b_sz, s_len, d = h.shape
    n_tok = b_sz * s_len
    xt = h.reshape(n_tok, d)
    logits = jnp.dot(xt, router_w, preferred_element_type=F32) + router_b.astype(F32)
    top_logit, top_idx = lax.top_k(logits, TOP_K)
    gates = jax.nn.softmax(top_logit, axis=-1)
    n_as = n_tok * TOP_K
    flat_e = top_idx.reshape(-1).astype(jnp.int32)
    order = jnp.argsort(flat_e)
    sorted_e = flat_e[order]
    counts = jnp.bincount(flat_e, length=N_EXPERTS).astype(jnp.int32)
    starts = jnp.cumsum(counts) - counts
    padded = (counts + MOE_BLOCK - 1) // MOE_BLOCK * MOE_BLOCK
    p_ends = jnp.cumsum(padded)
    p_starts = p_ends - padded
    rank = jnp.arange(n_as, dtype=jnp.int32) - starts[sorted_e]
    dest_sorted = p_starts[sorted_e] + rank
    n_rows = n_as + N_EXPERTS * MOE_BLOCK
    n_blk = n_rows // MOE_BLOCK
    row_token = jnp.full((n_rows,), n_tok, jnp.int32).at[dest_sorted].set((order // TOP_K).astype(jnp.int32))
    x_pad = jnp.concatenate([xt, jnp.zeros((1, d), xt.dtype)], axis=0)
    xb = x_pad[row_token].reshape(n_blk, MOE_BLOCK, d)
    blk_start = jnp.arange(n_blk, dtype=jnp.int32) * MOE_BLOCK
    blk_expert = jnp.clip(jnp.searchsorted(p_ends, blk_start, side='right'), 0, N_EXPERTS - 1)

    def expert_block(args):
        xblk, e = args
        gu = xblk @ up_w[e] + up_b[e]
        gate, lin = jnp.split(gu, 2, axis=-1)
        gate = jnp.minimum(gate, SWIGLU_LIMIT)
        lin = jnp.clip(lin, -SWIGLU_LIMIT, SWIGLU_LIMIT)
        act = gate * jax.nn.sigmoid(SWIGLU_ALPHA * gate) * (lin + 1.0)
        return act @ down_w[e] + down_b[e]

    y_rows = lax.map(expert_block, (xb, blk_expert)).reshape(n_rows, d)
    dest = jnp.zeros((n_as,), jnp.int32).at[order].set(dest_sorted)
    y = jnp.einsum('tkd,tk->td', y_rows[dest].reshape(n_tok, TOP_K, d), gates.astype(h.dtype))
    return y.reshape(b_sz, s_len, d)


def _normal(k, shape, scale):
    return jax.random.normal(k, shape, F32) * scale


def setup_inputs(seed: int = 0) -> dict:
    key = jax.random.key(seed)
    ks = jax.random.split(key, 32)
    d = D_MODEL
    x = _normal(ks[0], (BATCH, SEQ, d), 1.0)
    c = _normal(ks[1], (BATCH, d), 1.0)
    offset = jax.random.randint(ks[2], (BATCH, 1), 0, 4096, dtype=jnp.int32)
    positions = offset + jnp.arange(SEQ, dtype=jnp.int32)[None, :]

    ada_w = _normal(ks[3], (DEPTH, d, 6 * d), 0.5 * d ** -0.5)
    ada_b = _normal(ks[4], (DEPTH, 6 * d), 0.02)
    norm_gain = 1.0 + _normal(ks[5], (DEPTH, 2, d), 0.02)

    a_w_in = _normal(ks[6], (N_A_LAYERS, d, A_PROJ), d ** -0.5)
    a_conv = _normal(ks[7], (N_A_LAYERS, A_CONV, 3 * A_WIDTH), A_CONV ** -0.5)
    a_log = jnp.log(jax.random.uniform(ks[8], (N_A_LAYERS, A_HEADS), F32, 1.0, 16.0))
    dt = jnp.exp(jax.random.uniform(ks[9], (N_A_LAYERS, A_HEADS), F32, np.log(1e-3), np.log(1e-1)))
    a_dt_bias = dt + jnp.log(-jnp.expm1(-dt))
    a_out_gain = 1.0 + _normal(ks[10], (N_A_LAYERS, A_HEAD_DIM), 0.02)
    a_w_out = _normal(ks[11], (N_A_LAYERS, A_WIDTH, d), A_WIDTH ** -0.5)

    kv_ada_w = _normal(ks[12], (d, 2 * d), 0.5 * d ** -0.5)
    kv_ada_b = _normal(ks[13], (2 * d,), 0.02)
    kv_norm_gain = 1.0 + _normal(ks[14], (d,), 0.02)
    kv_w = _normal(ks[15], (d, 2 * B_KV_HEADS * B_HEAD_DIM), d ** -0.5)
    k_norm_gain = 1.0 + _normal(ks[16], (B_HEAD_DIM,), 0.02)

    b_w_q = _normal(ks[17], (N_B_LAYERS, d, B_Q_HEADS * B_HEAD_DIM), d ** -0.5)
    q_norm_gain = 1.0 + _normal(ks[18], (N_B_LAYERS, B_HEAD_DIM), 0.02)
    b_sinks = _normal(ks[19], (N_B_LAYERS, B_Q_HEADS), 0.5)
    b_w_out = _normal(ks[20], (N_B_LAYERS, B_Q_HEADS * B_HEAD_DIM, d), (B_Q_HEADS * B_HEAD_DIM) ** -0.5)

    router_w = _normal(ks[21], (DEPTH, d, N_EXPERTS), d ** -0.5)
    router_b = _normal(ks[22], (DEPTH, N_EXPERTS), 0.01)
    up_w = _normal(ks[23], (DEPTH, N_EXPERTS, d, 2 * D_FF), d ** -0.5)
    up_b = _normal(ks[24], (DEPTH, N_EXPERTS, 2 * D_FF), 0.01)
    down_w = _normal(ks[25], (DEPTH, N_EXPERTS, D_FF, d), D_FF ** -0.5)
    down_b = _normal(ks[26], (DEPTH, N_EXPERTS, d), 0.01)
    return {
        "x": x, "c": c, "positions": positions,
        "ada_w": ada_w, "ada_b": ada_b, "norm_gain": norm_gain,
        "a_w_in": a_w_in, "a_conv": a_conv, "a_log": a_log, "a_dt_bias": a_dt_bias,
        "a_out_gain": a_out_gain, "a_w_out": a_w_out,
        "kv_ada_w": kv_ada_w, "kv_ada_b": kv_ada_b, "kv_norm_gain": kv_norm_gain,
        "kv_w": kv_w, "k_norm_gain": k_norm_gain,
        "b_w_q": b_w_q, "q_norm_gain": q_norm_gain, "b_sinks": b_sinks, "b_w_out": b_w_out,
        "router_w": router_w, "router_b": router_b, "up_w": up_w, "up_b": up_b,
        "down_w": down_w, "down_b": down_b,
    }


def reference(x, c, positions, ada_w, ada_b, norm_gain, a_w_in, a_conv, a_log, a_dt_bias,
              a_out_gain, a_w_out, kv_ada_w, kv_ada_b, kv_norm_gain, kv_w, k_norm_gain,
              b_w_q, q_norm_gain, b_sinks, b_w_out, router_w, router_b, up_w, up_b,
              down_w, down_b):
    c_act = jax.nn.silu(c)
    k_sh, v_sh = None, None
    for layer in range(DEPTH):
        if layer == N_A_LAYERS:
            kv_shift, kv_scale = jnp.split(c_act @ kv_ada_w + kv_ada_b, 2, axis=-1)
            k_sh, v_sh = shared_kv(x, kv_norm_gain, kv_shift, kv_scale, kv_w, k_norm_gain, positions)
        mod = c_act @ ada_w[layer] + ada_b[layer]
        sh1, sc1, g1, sh2, sc2, g2 = jnp.split(mod, 6, axis=-1)
        h = ada_norm(x, norm_gain[layer, 0], sh1, sc1)
        if layer < N_A_LAYERS:
            y = gated_deltanet(h, a_w_in[layer], a_conv[layer], a_log[layer], a_dt_bias[layer],
                               a_out_gain[layer], a_w_out[layer])
        else:
            j = layer - N_A_LAYERS
            y = swa_sink_attention(h, k_sh, v_sh, b_w_q[j], q_norm_gain[j], b_sinks[j],
                                   b_w_out[j], positions)
        x = x + g1[:, None, :] * y
        h = ada_norm(x, norm_gain[layer, 1], sh2, sc2)
        y = moe_ffn(h, router_w[layer], router_b[layer], up_w[layer], up_b[layer],
                    down_w[layer], down_b[layer])
        x = x + g2[:, None, :] * y
    return x
```

```python
import functools

import jax
import jax.numpy as jnp
import numpy as np
from jax import lax
from jax.experimental import pallas as pl
from jax.experimental.pallas import tpu as pltpu

F32 = jnp.float32
BF16 = jnp.bfloat16
HIGHEST = lax.Precision.HIGHEST

D_MODEL = 1024
N_A_LAYERS = 1

A_HEAD_DIM = 128
A_HEADS = 8
A_WIDTH = 1024
A_CONV = 4
A_CHUNK = 64
A_PROJ_PAD = 4608

B_HEAD_DIM = 64
B_Q_HEADS = 16
B_KV_HEADS = 2
B_GROUP = 8
WINDOW = 128
ROPE_DIM = 16
ROPE_THETA = 500000.0

N_EXPERTS = 32
TOP_K = 4
D_FF = 1024
SWIGLU_LIMIT = 7.0
SWIGLU_ALPHA = 1.702
MOE_ROWS = 256

EPS = 1e-6
LANES = 128
NEG_INF = float("-inf")

VMEM_LIMIT = 56 * 1024 * 1024


def _cparams(sem, vmem=VMEM_LIMIT):
    return pltpu.CompilerParams(dimension_semantics=sem, vmem_limit_bytes=vmem)


def _silu(x):
    return x * jax.nn.sigmoid(x)


def _ada_norm(x, gain, shift, scale):
    y = x * lax.rsqrt(jnp.mean(x * x, axis=-1, keepdims=True) + EPS)
    return (y * gain) * (1.0 + scale) + shift


def _mod_body(c_ref, w_ref, b_ref, o_ref):
    o_ref[...] = jnp.dot(_silu(c_ref[...]), w_ref[...], preferred_element_type=F32,
                         precision=HIGHEST) + b_ref[...]


def _modulation(c, w3, b3, layer):
    bsz, d = c.shape
    n = w3.shape[-1]
    tn = 1024
    return pl.pallas_call(
        _mod_body,
        out_shape=jax.ShapeDtypeStruct((bsz, n), F32),
        grid=(n // tn,),
        in_specs=[pl.BlockSpec((bsz, d), lambda j: (0, 0)),
                  pl.BlockSpec((None, d, tn), lambda j: (layer, 0, j)),
                  pl.BlockSpec((None, 1, tn), lambda j: (layer, 0, j))],
        out_specs=pl.BlockSpec((bsz, tn), lambda j: (0, j)),
        compiler_params=_cparams(("arbitrary",)),
        name="adaln_mod",
    )(c, w3, b3)


def _inproj_body(x_ref, gain_ref, sh_ref, sc_ref, w_ref, o_ref):
    h = _ada_norm(x_ref[...], gain_ref[...], sh_ref[...], sc_ref[...]).astype(BF16)
    n = o_ref.shape[-1]
    for n0 in range(0, n, 512):
        o_ref[:, n0:n0 + 512] = jnp.dot(h, w_ref[:, n0:n0 + 512], preferred_element_type=F32)


def _in_proj(x2, gain, mod4, w_bf, seq):
    t, d = x2.shape
    n = w_bf.shape[-1]
    tm = 256
    tps = seq // tm
    return pl.pallas_call(
        _inproj_body,
        out_shape=jax.ShapeDtypeStruct((t, n), F32),
        grid=(t // tm,),
        in_specs=[pl.BlockSpec((tm, d), lambda i: (i, 0)),
                  pl.BlockSpec((1, d), lambda i: (0, 0)),
                  pl.BlockSpec((None, None, 1, d), lambda i: (i // tps, 0, 0, 0)),
                  pl.BlockSpec((None, None, 1, d), lambda i: (i // tps, 1, 0, 0)),
                  pl.BlockSpec((d, n), lambda i: (0, 0))],
        out_specs=pl.BlockSpec((tm, n), lambda i: (i, 0)),
        compiler_params=_cparams(("parallel",)),
        name="gdn_in_proj",
    )(x2, gain, mod4, mod4, w_bf)


GDN_HG = 2
GDN_RB = 256


def _gdn_body(alog_ref, dtb_ref, q_ref, k_ref, v_ref, z_ref, ab_ref, cq_ref, ck_ref, cv_ref, og_ref,
              o_ref, qs, ks, vs, ws, bs, gcs, qks, st):
    seq = q_ref.shape[0]
    hd = A_HEAD_DIM
    ch = A_CHUNK
    n_chunks = seq // ch
    g_idx = pl.program_id(1)

    lane = lax.broadcasted_iota(jnp.int32, (GDN_RB, hd), 1)
    row = lax.broadcasted_iota(jnp.int32, (GDN_RB, hd), 0)
    row_in_chunk = row % ch

    def conv_silu(x_ref, c_ref, col0, r, start):
        main = x_ref[pl.ds(start, GDN_RB), col0:col0 + hd]
        pstart = pl.multiple_of(jnp.maximum(start - 8, 0), 8)
        prev = x_ref[pl.ds(pstart, 8), col0:col0 + hd] * jnp.where(r > 0, 1.0, 0.0)
        ext = jnp.concatenate([prev, main], axis=0)
        acc = main * c_ref[A_CONV - 1:A_CONV, col0:col0 + hd]
        for s in range(1, A_CONV):
            shifted = pltpu.roll(ext, s, 0)[8:8 + GDN_RB]
            acc = acc + shifted * c_ref[A_CONV - 1 - s:A_CONV - s, col0:col0 + hd]
        return _silu(acc)

    def l2n(x):
        return x * lax.rsqrt(jnp.sum(x * x, axis=-1, keepdims=True) + EPS)

    def prep(r, carry):
        start = pl.multiple_of(r * GDN_RB, GDN_RB)
        ab = ab_ref[pl.ds(start, GDN_RB), :]
        for hh in range(GDN_HG):
            col0 = hh * hd
            head = g_idx * GDN_HG + hh
            qs[hh, pl.ds(start, GDN_RB), :] = l2n(conv_silu(q_ref, cq_ref, col0, r, start)) * (hd ** -0.5)
            ks[hh, pl.ds(start, GDN_RB), :] = l2n(conv_silu(k_ref, ck_ref, col0, r, start))
            vs[hh, pl.ds(start, GDN_RB), :] = conv_silu(v_ref, cv_ref, col0, r, start)
            a_col = jnp.sum(jnp.where(lane == head, ab, 0.0), axis=-1, keepdims=True)
            b_col = jnp.sum(jnp.where(lane == head + A_HEADS, ab, 0.0), axis=-1, keepdims=True)
            bs[hh, pl.ds(start, GDN_RB), :] = jnp.broadcast_to(jax.nn.sigmoid(b_col), (GDN_RB, hd))
            xa = a_col + dtb_ref[0, head]
            softplus = jnp.maximum(xa, 0.0) + jnp.log1p(jnp.exp(-jnp.abs(xa)))
            g = jnp.broadcast_to(-jnp.exp(alog_ref[0, head]) * softplus, (GDN_RB, hd))
            s = 1
            while s < ch:
                g = g + jnp.where(row_in_chunk >= s, pltpu.roll(g, s, 0), 0.0)
                s *= 2
            gcs[hh, pl.ds(start, GDN_RB), :] = g
        return carry

    lax.fori_loop(0, seq // GDN_RB, prep, 0)

    ci = lax.broadcasted_iota(jnp.int32, (ch, ch), 0)
    cj = lax.broadcasted_iota(jnp.int32, (ch, ch), 1)
    incl = ci >= cj
    strict = ci > cj
    eye = (ci == cj).astype(F32)

    def local(c, carry):
        r0 = pl.multiple_of(c * ch, ch)
        for hh in range(GDN_HG):
            rows = pl.ds(r0, ch)
            q = qs[hh, rows, :]
            k = ks[hh, rows, :]
            v = vs[hh, rows, :]
            beta = bs[hh, rows, :]
            gc = gcs[hh, rows, :]
            gc_sq = gc[:, :ch]
            diff = gc_sq - gc_sq.T
            decay = jnp.where(incl, jnp.exp(jnp.minimum(diff, 0.0)), 0.0)
            kb = k * beta
            kk = lax.dot_general(kb, k, (((1,), (1,)), ((), ())), precision=HIGHEST,
                                 preferred_element_type=F32)
            neg_a = jnp.where(strict, -(kk * decay), 0.0)
            inv = eye + neg_a
            pw = neg_a
            for _ in range(5):
                pw = jnp.dot(pw, pw, precision=HIGHEST, preferred_element_type=F32)
                inv = inv + jnp.dot(inv, pw, precision=HIGHEST, preferred_element_type=F32)
            e_gc = jnp.exp(gc)
            rhs = jnp.concatenate([v * beta, kb * e_gc], axis=1)
            sol = jnp.dot(inv, rhs, precision=HIGHEST, preferred_element_type=F32)
            qk = lax.dot_general(q.astype(BF16), k.astype(BF16), (((1,), (1,)), ((), ())),
                                 preferred_element_type=F32)
            gl = gcs[hh, pl.ds(r0 + ch - 1, 1), :]
            vs[hh, rows, :] = sol[:, :hd]
            ws[hh, rows, :] = sol[:, hd:]
            qks[hh, rows, :] = qk * decay
            qs[hh, rows, :] = q * e_gc
            ks[hh, rows, :] = k * jnp.exp(gl - gc)
        return carry

    lax.fori_loop(0, n_chunks, local, 0)

    st[...] = jnp.zeros_like(st)

    def scan(c, carry):
        r0 = pl.multiple_of(c * ch, ch)
        for hh in range(GDN_HG):
            rows = pl.ds(r0, ch)
            col0 = hh * hd
            state = st[hh]
            s_bf = state.astype(BF16)
            u = vs[hh, rows, :]
            w = ws[hh, rows, :]
            qd = qs[hh, rows, :]
            kt = ks[hh, rows, :]
            qk = qks[hh, rows, :]
            gl = gcs[hh, pl.ds(r0 + ch - 1, 1), :]
            v_new = u - jnp.dot(w.astype(BF16), s_bf, preferred_element_type=F32)
            vn_bf = v_new.astype(BF16)
            o = (jnp.dot(qd.astype(BF16), s_bf, preferred_element_type=F32)
                 + jnp.dot(qk.astype(BF16), vn_bf, preferred_element_type=F32))
            st[hh] = state * jnp.exp(gl) + lax.dot_general(
                kt.astype(BF16), vn_bf, (((0,), (0,)), ((), ())), preferred_element_type=F32)
            on = o * lax.rsqrt(jnp.mean(o * o, axis=-1, keepdims=True) + EPS) * og_ref[...]
            z = z_ref[rows, col0:col0 + hd]
            o_ref[rows, col0:col0 + hd] = (on * _silu(z)).astype(BF16)
        return carry

    lax.fori_loop(0, n_chunks, scan, 0)


def _gdn(proj, conv_w, a_log, dt_bias, out_gain, bsz, seq):
    t = proj.shape[0]
    hd = A_HEAD_DIM
    wb = hd * GDN_HG
    ng = A_HEADS // GDN_HG
    per = A_WIDTH // wb
    smem = pl.BlockSpec(memory_space=pltpu.SMEM)
    seq_spec = lambda off: pl.BlockSpec((seq, wb), lambda b, g, off=off: (b, off + g))
    conv_spec = lambda off: pl.BlockSpec((A_CONV, wb), lambda b, g, off=off: (0, off + g))
    sc = lambda lanes=hd: pltpu.VMEM((GDN_HG, seq, lanes), F32)
    return pl.pallas_call(
        _gdn_body,
        out_shape=jax.ShapeDtypeStruct((t, A_WIDTH), BF16),
        grid=(bsz, ng),
        in_specs=[smem, smem,
                  seq_spec(0), seq_spec(per), seq_spec(2 * per), seq_spec(3 * per),
                  pl.BlockSpec((seq, LANES), lambda b, g: (b, 4 * A_WIDTH // LANES)),
                  conv_spec(0), conv_spec(per), conv_spec(2 * per),
                  pl.BlockSpec((1, hd), lambda b, g: (0, 0))],
        out_specs=pl.BlockSpec((seq, wb), lambda b, g: (b, g)),
        scratch_shapes=[sc(), sc(), sc(), sc(), sc(), sc(), sc(A_CHUNK),
                        pltpu.VMEM((GDN_HG, hd, hd), F32)],
        compiler_params=_cparams(("parallel", "parallel")),
        name="gdn_core",
    )(a_log, dt_bias, proj, proj, proj, proj, proj, conv_w, conv_w, conv_w, out_gain)


def _oproj_body(a_ref, w_ref, x_ref, g_ref, o_ref):
    y = jnp.dot(a_ref[...], w_ref[...], preferred_element_type=F32)
    o_ref[...] = x_ref[...] + g_ref[...] * y


def _out_proj_residual(a_bf, w_bf, x2, mod4, gate_slot, seq):
    t, d = x2.shape
    kdim = a_bf.shape[-1]
    tm = 512
    tps = seq // tm
    return pl.pallas_call(
        _oproj_body,
        out_shape=jax.ShapeDtypeStruct((t, d), F32),
        grid=(t // tm,),
        in_specs=[pl.BlockSpec((tm, kdim), lambda i: (i, 0)),
                  pl.BlockSpec((kdim, d), lambda i: (0, 0)),
                  pl.BlockSpec((tm, d), lambda i: (i, 0)),
                  pl.BlockSpec((None, None, 1, d), lambda i: (i // tps, gate_slot, 0, 0))],
        out_specs=pl.BlockSpec((tm, d), lambda i: (i, 0)),
        compiler_params=_cparams(("parallel",)),
        name="out_proj_residual",
    )(a_bf, w_bf, x2, mod4)


ROUTER_TM = 512


def _router_body(x_ref, gain_ref, sh_ref, sc_ref, rw_ref, rb_ref, h_ref, idx_ref, gate_ref, rank_ref,
                 cnt_ref, carry):
    i = pl.program_id(0)

    @pl.when(i == 0)
    def _():
        carry[...] = jnp.zeros_like(carry)

    tm = x_ref.shape[0]
    h = _ada_norm(x_ref[...], gain_ref[...], sh_ref[...], sc_ref[...])
    h_ref[...] = h
    lane = lax.broadcasted_iota(jnp.int32, (tm, LANES), 1)
    logits = jnp.dot(h, rw_ref[...], preferred_element_type=F32, precision=HIGHEST) + rb_ref[...]
    logits = jnp.where(lane < N_EXPERTS, logits, NEG_INF)
    tops, sels = [], []
    work = logits
    for _ in range(TOP_K):
        m = jnp.max(work, axis=-1, keepdims=True)
        sel = jnp.min(jnp.where(work == m, lane, LANES), axis=-1, keepdims=True)
        work = jnp.where(lane == sel, NEG_INF, work)
        tops.append(m)
        sels.append(sel)
    exps = [jnp.exp(m - tops[0]) for m in tops]
    denom = exps[0] + exps[1] + exps[2] + exps[3]
    onehot = jnp.zeros((tm, LANES), F32)
    for sel in sels:
        onehot = onehot + (lane == sel).astype(F32)
    ti = lax.broadcasted_iota(jnp.int32, (tm, tm), 0)
    tj = lax.broadcasted_iota(jnp.int32, (tm, tm), 1)
    tri = (ti > tj).astype(BF16)
    before = jnp.dot(tri, onehot.astype(BF16), preferred_element_type=F32) + carry[...]
    idx_out = jnp.zeros((tm, LANES), jnp.int32)
    gate_out = jnp.zeros((tm, LANES), F32)
    rank_out = jnp.zeros((tm, LANES), jnp.int32)
    for k in range(TOP_K):
        rank_k = jnp.sum(jnp.where(lane == sels[k], before, 0.0), axis=-1, keepdims=True)
        idx_out = jnp.where(lane == k, sels[k], idx_out)
        gate_out = jnp.where(lane == k, exps[k] / denom, gate_out)
        rank_out = jnp.where(lane == k, rank_k.astype(jnp.int32), rank_out)
    idx_ref[...] = idx_out
    gate_ref[...] = gate_out
    rank_ref[...] = rank_out
    carry[...] = carry[...] + jnp.sum(onehot, axis=0, keepdims=True)
    cnt_ref[...] = carry[...]


def _router(x2, gain, mod4, sh_slot, sc_slot, rw_pad, rb_pad, seq):
    t, d = x2.shape
    tm = ROUTER_TM
    tps = seq // tm
    tok = lambda dt: jax.ShapeDtypeStruct((t, LANES), dt)
    tok_spec = pl.BlockSpec((tm, LANES), lambda i: (i, 0))
    return pl.pallas_call(
        _router_body,
        out_shape=(jax.ShapeDtypeStruct((t, d), F32), tok(jnp.int32), tok(F32), tok(jnp.int32),
                   jax.ShapeDtypeStruct((1, LANES), F32)),
        grid=(t // tm,),
        in_specs=[pl.BlockSpec((tm, d), lambda i: (i, 0)),
                  pl.BlockSpec((1, d), lambda i: (0, 0)),
                  pl.BlockSpec((None, None, 1, d), lambda i: (i // tps, sh_slot, 0, 0)),
                  pl.BlockSpec((None, None, 1, d), lambda i: (i // tps, sc_slot, 0, 0)),
                  pl.BlockSpec((d, LANES), lambda i: (0, 0)),
                  pl.BlockSpec((1, LANES), lambda i: (0, 0))],
        out_specs=(pl.BlockSpec((tm, d), lambda i: (i, 0)), tok_spec, tok_spec, tok_spec,
                   pl.BlockSpec((1, LANES), lambda i: (0, 0))),
        scratch_shapes=[pltpu.VMEM((1, LANES), F32)],
        compiler_params=_cparams(("arbitrary",)),
        name="moe_router",
    )(x2, gain, mod4, mod4, rw_pad, rb_pad)


DISPATCH_TM = 256


def _row_copy(src_ref, src_row, dst_ref, dst_row, sem):
    return pltpu.make_async_copy(src_ref.at[pl.ds(src_row, 1)], dst_ref.at[pl.ds(dst_row, 1)], sem)


def _dispatch_body(dest_ref, h_ref, xb_in, xb_out, sem):
    del xb_in
    tm = h_ref.shape[0]

    def issue(r, carry):
        for k in range(TOP_K):
            _row_copy(h_ref, r, xb_out, dest_ref[0, r * TOP_K + k], sem).start()
        return carry

    lax.fori_loop(0, tm, issue, 0)

    def drain(r, carry):
        for k in range(TOP_K):
            _row_copy(h_ref, 0, xb_out, 0, sem).wait()
        return carry

    lax.fori_loop(0, tm, drain, 0)


def _dispatch(h2, dest3, n_rows):
    t, d = h2.shape
    tm = DISPATCH_TM
    xb0 = jnp.zeros((n_rows, d), h2.dtype)
    return pl.pallas_call(
        _dispatch_body,
        out_shape=jax.ShapeDtypeStruct((n_rows, d), h2.dtype),
        grid=(t // tm,),
        in_specs=[pl.BlockSpec((None, 1, tm * TOP_K), lambda i: (i, 0, 0), memory_space=pltpu.SMEM),
                  pl.BlockSpec((tm, d), lambda i: (i, 0)),
                  pl.BlockSpec(memory_space=pl.ANY)],
        out_specs=pl.BlockSpec(memory_space=pl.ANY),
        scratch_shapes=[pltpu.SemaphoreType.DMA(())],
        input_output_aliases={2: 0},
        compiler_params=_cparams(("arbitrary",)),
        name="moe_dispatch",
    )(dest3, h2, xb0)


def _ffn_body(be_ref, nu_ref, x_ref, uw_ref, ub_ref, dw_ref, db_ref, y_ref, uw_bf, dw_bf):
    i = pl.program_id(0)
    changed = jnp.logical_or(i == 0, be_ref[i] != be_ref[jnp.maximum(i - 1, 0)])
    active = i < nu_ref[0]

    @pl.when(jnp.logical_and(changed, active))
    def _():
        rows = 128
        for r0 in range(0, uw_ref.shape[0], rows):
            uw_bf[r0:r0 + rows, :] = uw_ref[r0:r0 + rows, :].astype(BF16)
        for r0 in range(0, dw_ref.shape[0], rows):
            dw_bf[r0:r0 + rows, :] = dw_ref[r0:r0 + rows, :].astype(BF16)

    @pl.when(active)
    def _():
        x = x_ref[...].astype(BF16)
        gu = jnp.dot(x, uw_bf[...], preferred_element_type=F32) + ub_ref[...]
        gate = jnp.minimum(gu[:, :D_FF], SWIGLU_LIMIT)
        lin = jnp.clip(gu[:, D_FF:], -SWIGLU_LIMIT, SWIGLU_LIMIT)
        act = gate * jax.nn.sigmoid(SWIGLU_ALPHA * gate) * (lin + 1.0)
        y_ref[...] = jnp.dot(act.astype(BF16), dw_bf[...], preferred_element_type=F32) + db_ref[...]

    @pl.when(jnp.logical_not(active))
    def _():
        y_ref[...] = jnp.zeros_like(y_ref)


def _expert_ffn(xb, blk_expert, n_used, up_w, up_b4, down_w, down_b4, layer):
    n_rows, d = xb.shape
    bm = MOE_ROWS
    n_blk = n_rows // bm
    f2 = up_w.shape[-1]
    grid_spec = pltpu.PrefetchScalarGridSpec(
        num_scalar_prefetch=2,
        grid=(n_blk,),
        in_specs=[pl.BlockSpec((bm, d), lambda i, be, nu: (i, 0)),
                  pl.BlockSpec((None, None, d, f2), lambda i, be, nu: (layer, be[i], 0, 0)),
                  pl.BlockSpec((None, None, 1, f2), lambda i, be, nu: (layer, be[i], 0, 0)),
                  pl.BlockSpec((None, None, f2 // 2, d), lambda i, be, nu: (layer, be[i], 0, 0)),
                  pl.BlockSpec((None, None, 1, d), lambda i, be, nu: (layer, be[i], 0, 0))],
        out_specs=pl.BlockSpec((bm, d), lambda i, be, nu: (i, 0)),
        scratch_shapes=[pltpu.VMEM((d, f2), BF16), pltpu.VMEM((f2 // 2, d), BF16)],
    )
    return pl.pallas_call(
        _ffn_body,
        out_shape=jax.ShapeDtypeStruct((n_rows, d), F32),
        grid_spec=grid_spec,
        compiler_params=_cparams(("arbitrary",)),
        name="moe_expert_ffn",
    )(blk_expert, n_used, xb, up_w, up_b4, down_w, down_b4)


COMBINE_TM = 128


def _combine_body(dest_ref, gate_ref, x_ref, g_ref, yrows, o_ref, buf, sem):
    tm = x_ref.shape[0]

    def issue(r, carry):
        for k in range(TOP_K):
            _row_copy(yrows, dest_ref[0, r * TOP_K + k], buf, k * tm + r, sem).start()
        return carry

    lax.fori_loop(0, tm, issue, 0)

    def drain(r, carry):
        for k in range(TOP_K):
            _row_copy(yrows, 0, buf, 0, sem).wait()
        return carry

    lax.fori_loop(0, tm, drain, 0)

    gates = gate_ref[...]
    y = gates[:, 0:1] * buf[0:tm, :]
    for k in range(1, TOP_K):
        y = y + gates[:, k:k + 1] * buf[k * tm:(k + 1) * tm, :]
    o_ref[...] = x_ref[...] + g_ref[...] * y


def _combine(y_rows, dest3, gates, x2, mod4, gate_slot, seq):
    t, d = x2.shape
    tm = COMBINE_TM
    tps = seq // tm
    return pl.pallas_call(
        _combine_body,
        out_shape=jax.ShapeDtypeStruct((t, d), F32),
        grid=(t // tm,),
        in_specs=[pl.BlockSpec((None, 1, tm * TOP_K), lambda i: (i, 0, 0), memory_space=pltpu.SMEM),
                  pl.BlockSpec((tm, LANES), lambda i: (i, 0)),
                  pl.BlockSpec((tm, d), lambda i: (i, 0)),
                  pl.BlockSpec((None, None, 1, d), lambda i: (i // tps, gate_slot, 0, 0)),
                  pl.BlockSpec(memory_space=pl.ANY)],
        out_specs=pl.BlockSpec((tm, d), lambda i: (i, 0)),
        scratch_shapes=[pltpu.VMEM((TOP_K * tm, d), y_rows.dtype), pltpu.SemaphoreType.DMA(())],
        compiler_params=_cparams(("arbitrary",)),
        name="moe_combine",
    )(dest3, gates, x2, mod4, y_rows)


def _moe_layer(x2, gain, mod4, router_w, router_b, up_w, up_b, down_w, down_b, layer, seq):
    t, d = x2.shape
    rw_pad = jnp.pad(router_w[layer], ((0, 0), (0, LANES - N_EXPERTS)))
    rb_pad = jnp.pad(router_b[layer], (0, LANES - N_EXPERTS)).reshape(1, LANES)
    h2, idx, gates, rank, counts = _router(x2, gain, mod4, 3, 4, rw_pad, rb_pad, seq)

    bm = MOE_ROWS
    counts = counts[0, :N_EXPERTS].astype(jnp.int32)
    padded = (counts + bm - 1) // bm * bm
    p_ends = jnp.cumsum(padded)
    p_starts = p_ends - padded
    n_rows = t * TOP_K + N_EXPERTS * bm
    n_blk = n_rows // bm
    top_idx = idx[:, :TOP_K]
    onehot = top_idx[:, :, None] == jnp.arange(N_EXPERTS, dtype=jnp.int32)[None, None, :]
    dest = jnp.sum(jnp.where(onehot, p_starts[None, None, :], 0), axis=-1) + rank[:, :TOP_K]
    blk_start = jnp.arange(n_blk, dtype=jnp.int32) * bm
    blk_expert = jnp.sum((blk_start[:, None] >= p_ends[None, :]).astype(jnp.int32), axis=-1)
    blk_expert = jnp.minimum(blk_expert, N_EXPERTS - 1).astype(jnp.int32)
    n_used = (p_ends[-1:] // bm).astype(jnp.int32)

    xb = _dispatch(h2, dest.reshape(t // DISPATCH_TM, 1, DISPATCH_TM * TOP_K), n_rows)
    f2 = up_w.shape[-1]
    y_rows = _expert_ffn(xb, blk_expert, n_used, up_w, up_b.reshape(up_b.shape[0], N_EXPERTS, 1, f2),
                         down_w, down_b.reshape(down_b.shape[0], N_EXPERTS, 1, d), layer)
    return _combine(y_rows, dest.reshape(t // COMBINE_TM, 1, COMBINE_TM * TOP_K), gates, x2, mod4, 5, seq)


QKV_TM = 256


def _rope_tables(pos_col, invf_row):
    tm = pos_col.shape[0]
    ang = pos_col * invf_row
    c = jnp.cos(ang)
    s = jnp.sin(ang)
    d = lax.broadcasted_iota(jnp.int32, (tm, LANES), 1) % B_HEAD_DIM
    half = ROPE_DIM // 2
    cos_t = jnp.where(d < ROPE_DIM, c, 1.0)
    sin_lo = jnp.where(d < half, -s, 0.0)
    sin_hi = jnp.where(jnp.logical_and(d >= half, d < ROPE_DIM), s, 0.0)
    return cos_t, sin_lo, sin_hi


def _head_norm_rope(x, gain_row, tables):
    cos_t, sin_lo, sin_hi = tables
    lane = lax.broadcasted_iota(jnp.int32, x.shape, 1)
    sq = x * x
    first = lane < B_HEAD_DIM
    ss0 = jnp.sum(jnp.where(first, sq, 0.0), axis=-1, keepdims=True)
    ss1 = jnp.sum(jnp.where(first, 0.0, sq), axis=-1, keepdims=True)
    ms = jnp.where(first, ss0, ss1) * (1.0 / B_HEAD_DIM)
    xn = x * lax.rsqrt(ms + EPS) * gain_row
    half = ROPE_DIM // 2
    return (xn * cos_t + pltpu.roll(xn, LANES - half, 1) * sin_lo + pltpu.roll(xn, half, 1) * sin_hi)


def _qkv_body(x_ref, pos_ref, gq_ref, shq_ref, scq_ref, gkv_ref, shkv_ref, sckv_ref, wq_ref, wkv_ref,
              qg_ref, kg_ref, invf_ref, q_ref, k_ref, v_ref):
    x = x_ref[...]
    y = x * lax.rsqrt(jnp.mean(x * x, axis=-1, keepdims=True) + EPS)
    hq = ((y * gq_ref[...]) * (1.0 + scq_ref[...]) + shq_ref[...]).astype(BF16)
    hkv = ((y * gkv_ref[...]) * (1.0 + sckv_ref[...]) + shkv_ref[...]).astype(BF16)
    tables = _rope_tables(pos_ref[...].astype(F32), invf_ref[...])
    kv = jnp.dot(hkv, wkv_ref[...], preferred_element_type=F32)
    k_ref[...] = _head_norm_rope(kv[:, :LANES], kg_ref[...], tables).astype(BF16)
    v_ref[...] = kv[:, LANES:].astype(BF16)
    q = jnp.dot(hq, wq_ref[...], preferred_element_type=F32)
    scale = B_HEAD_DIM ** -0.5
    for p in range(q.shape[-1] // LANES):
        qp = _head_norm_rope(q[:, p * LANES:(p + 1) * LANES], qg_ref[...], tables)
        q_ref[:, p * LANES:(p + 1) * LANES] = (qp * scale).astype(BF16)


def _qkv(x2, pos2, gq, mod4, gkv, kvmod4, wq_bf, wkv_bf, qg2, kg2, invf, seq):
    t, d = x2.shape
    tm = QKV_TM
    tps = seq // tm
    nq = wq_bf.shape[-1]
    nkv = wkv_bf.shape[-1]
    row = lambda n: pl.BlockSpec((1, n), lambda i: (0, 0))
    modspec = lambda slot: pl.BlockSpec((None, None, 1, d), lambda i, slot=slot: (i // tps, slot, 0, 0))
    return pl.pallas_call(
        _qkv_body,
        out_shape=(jax.ShapeDtypeStruct((t, nq), BF16), jax.ShapeDtypeStruct((t, LANES), BF16),
                   jax.ShapeDtypeStruct((t, LANES), BF16)),
        grid=(t // tm,),
        in_specs=[pl.BlockSpec((tm, d), lambda i: (i, 0)),
                  pl.BlockSpec((tm, 1), lambda i: (i, 0)),
                  row(d), modspec(0), modspec(1),
                  row(d), modspec(0), modspec(1),
                  pl.BlockSpec((d, nq), lambda i: (0, 0)),
                  pl.BlockSpec((d, nkv), lambda i: (0, 0)),
                  row(LANES), row(LANES), row(LANES)],
        out_specs=(pl.BlockSpec((tm, nq), lambda i: (i, 0)),
                   pl.BlockSpec((tm, LANES), lambda i: (i, 0)),
                   pl.BlockSpec((tm, LANES), lambda i: (i, 0))),
        compiler_params=_cparams(("parallel",)),
        name="swa_qkv_proj",
    )(x2, pos2, gq, mod4, mod4, gkv, kvmod4, kvmod4, wq_bf, wkv_bf, qg2, kg2, invf)


def _attn_body(sink_ref, q_ref, kc_ref, kp_ref, vc_ref, vp_ref, o_ref):
    n = pl.program_id(1)
    w = WINDOW
    qi = lax.broadcasted_iota(jnp.int32, (w, 2 * w), 0) + w
    kj = lax.broadcasted_iota(jnp.int32, (w, 2 * w), 1)
    band = jnp.logical_and(kj <= qi, qi - kj < w)
    mask = jnp.logical_and(band, jnp.logical_or(n > 0, kj >= w))
    hd = B_HEAD_DIM
    for g in range(B_KV_HEADS):
        kb = jnp.concatenate([kp_ref[:, g * hd:(g + 1) * hd], kc_ref[:, g * hd:(g + 1) * hd]], axis=0)
        vb = jnp.concatenate([vp_ref[:, g * hd:(g + 1) * hd], vc_ref[:, g * hd:(g + 1) * hd]], axis=0)
        for hh in range(B_GROUP):
            head = g * B_GROUP + hh
            qh = q_ref[:, head * hd:(head + 1) * hd]
            s = lax.dot_general(qh, kb, (((1,), (1,)), ((), ())), preferred_element_type=F32)
            s = jnp.where(mask, s, NEG_INF)
            sink = sink_ref[0, head]
            m = jnp.maximum(jnp.max(s, axis=-1, keepdims=True), sink)
            p = jnp.exp(s - m)
            denom = jnp.sum(p, axis=-1, keepdims=True) + jnp.exp(sink - m)
            o = jnp.dot(p.astype(BF16), vb, preferred_element_type=F32) / denom
            o_ref[:, head * hd:(head + 1) * hd] = o.astype(BF16)


def _attention(q, k, v, sinks2, bsz, seq):
    t, nq = q.shape
    w = WINDOW
    nb = seq // w
    cur = lambda b, n: (b * nb + n, 0)
    prev = lambda b, n: (b * nb + jnp.maximum(n - 1, 0), 0)
    return pl.pallas_call(
        _attn_body,
        out_shape=jax.ShapeDtypeStruct((t, nq), BF16),
        grid=(bsz, nb),
        in_specs=[pl.BlockSpec(memory_space=pltpu.SMEM),
                  pl.BlockSpec((w, nq), cur),
                  pl.BlockSpec((w, LANES), cur), pl.BlockSpec((w, LANES), prev),
                  pl.BlockSpec((w, LANES), cur), pl.BlockSpec((w, LANES), prev)],
        out_specs=pl.BlockSpec((w, nq), cur),
        compiler_params=_cparams(("parallel", "parallel")),
        name="swa_sink_attention",
    )(sinks2, q, k, k, v, v)


def kernel(x, c, positions, ada_w, ada_b, norm_gain, a_w_in, a_conv, a_log, a_dt_bias, a_out_gain, a_w_out,
           kv_ada_w, kv_ada_b, kv_norm_gain, kv_w, k_norm_gain, b_w_q, q_norm_gain, b_sinks, b_w_out,
           router_w, router_b, up_w, up_b, down_w, down_b):
    bsz, seq, d = x.shape
    t = bsz * seq
    depth = ada_w.shape[0]
    x2 = x.reshape(t, d)
    ada_b3 = ada_b.reshape(depth, 1, 6 * d)

    for layer in range(depth):
        mod4 = _modulation(c, ada_w, ada_b3, layer).reshape(bsz, 6, 1, d)
        gain1 = norm_gain[layer, 0].reshape(1, d)
        gain2 = norm_gain[layer, 1].reshape(1, d)
        if layer < N_A_LAYERS:
            w_in = jnp.pad(a_w_in[layer].astype(BF16), ((0, 0), (0, A_PROJ_PAD - a_w_in.shape[-1])))
            proj = _in_proj(x2, gain1, mod4, w_in, seq)
            o = _gdn(proj, a_conv[layer], a_log[layer].reshape(1, A_HEADS),
                     a_dt_bias[layer].reshape(1, A_HEADS), a_out_gain[layer].reshape(1, A_HEAD_DIM),
                     bsz, seq)
            x2 = _out_proj_residual(o, a_w_out[layer].astype(BF16), x2, mod4, 2, seq)
        else:
            j = layer - N_A_LAYERS
            kvmod4 = _modulation(c, kv_ada_w.reshape(1, d, 2 * d), kv_ada_b.reshape(1, 1, 2 * d), 0)
            kvmod4 = kvmod4.reshape(bsz, 2, 1, d)
            half = ROPE_DIM // 2
            inv_freq = ROPE_THETA ** (-np.arange(0, ROPE_DIM, 2, dtype=np.float32) / ROPE_DIM)
            invf = jnp.asarray(np.tile(inv_freq.astype(np.float32), LANES // half).reshape(1, LANES))
            q, k, v = _qkv(x2, positions.reshape(t, 1), gain1, mod4, kv_norm_gain.reshape(1, d), kvmod4,
                           b_w_q[j].astype(BF16), kv_w.astype(BF16),
                           jnp.tile(q_norm_gain[j], 2).reshape(1, LANES),
                           jnp.tile(k_norm_gain, 2).reshape(1, LANES), invf, seq)
            o = _attention(q, k, v, b_sinks[j].reshape(1, B_Q_HEADS), bsz, seq)
            x2 = _out_proj_residual(o, b_w_out[j].astype(BF16), x2, mod4, 2, seq)
        x2 = _moe_layer(x2, gain2, mod4, router_w, router_b, up_w, up_b, down_w, down_b, layer, seq)
    return x2.reshape(bsz, seq, d)
```

```python
import functools

import jax
import jax.numpy as jnp
import numpy as np
from jax import lax
from jax.experimental import pallas as pl
from jax.experimental.pallas import tpu as pltpu

F32 = jnp.float32
BF16 = jnp.bfloat16
HIGHEST = lax.Precision.HIGHEST

D_MODEL = 1024
N_A_LAYERS = 1

A_HEAD_DIM = 128
A_HEADS = 8
A_WIDTH = 1024
A_CONV = 4
A_CHUNK = 64
A_PROJ_PAD = 4608

B_HEAD_DIM = 64
B_Q_HEADS = 16
B_KV_HEADS = 2
B_GROUP = 8
WINDOW = 128
ROPE_DIM = 16
ROPE_THETA = 500000.0

N_EXPERTS = 32
TOP_K = 4
D_FF = 1024
SWIGLU_LIMIT = 7.0
SWIGLU_ALPHA = 1.702
MOE_ROWS = 256

EPS = 1e-6
LANES = 128
NEG_INF = float("-inf")

VMEM_LIMIT = 56 * 1024 * 1024


def _cparams(sem, vmem=VMEM_LIMIT):
    return pltpu.CompilerParams(dimension_semantics=sem, vmem_limit_bytes=vmem)


def _silu(x):
    return x * jax.nn.sigmoid(x)


def _ada_norm(x, gain, shift, scale):
    y = x * lax.rsqrt(jnp.mean(x * x, axis=-1, keepdims=True) + EPS)
    return (y * gain) * (1.0 + scale) + shift


def _mod_body(c_ref, w_ref, b_ref, o_ref):
    o_ref[...] = jnp.dot(_silu(c_ref[...]), w_ref[...], preferred_element_type=F32,
                         precision=HIGHEST) + b_ref[...]


def _modulation(c, w3, b3, layer):
    bsz, d = c.shape
    n = w3.shape[-1]
    tn = 1024
    return pl.pallas_call(
        _mod_body,
        out_shape=jax.ShapeDtypeStruct((bsz, n), F32),
        grid=(n // tn,),
        in_specs=[pl.BlockSpec((bsz, d), lambda j: (0, 0)),
                  pl.BlockSpec((None, d, tn), lambda j: (layer, 0, j)),
                  pl.BlockSpec((None, 1, tn), lambda j: (layer, 0, j))],
        out_specs=pl.BlockSpec((bsz, tn), lambda j: (0, j)),
        compiler_params=_cparams(("arbitrary",)),
        name="adaln_mod",
    )(c, w3, b3)


def _inproj_body(x_ref, gain_ref, sh_ref, sc_ref, w_ref, o_ref):
    h = _ada_norm(x_ref[...], gain_ref[...], sh_ref[...], sc_ref[...]).astype(BF16)
    n = o_ref.shape[-1]
    for n0 in range(0, n, 512):
        o_ref[:, n0:n0 + 512] = jnp.dot(h, w_ref[:, n0:n0 + 512], preferred_element_type=F32)


def _in_proj(x2, gain, mod4, w_bf, seq):
    t, d = x2.shape
    n = w_bf.shape[-1]
    tm = 256
    tps = seq // tm
    return pl.pallas_call(
        _inproj_body,
        out_shape=jax.ShapeDtypeStruct((t, n), F32),
        grid=(t // tm,),
        in_specs=[pl.BlockSpec((tm, d), lambda i: (i, 0)),
                  pl.BlockSpec((1, d), lambda i: (0, 0)),
                  pl.BlockSpec((None, None, 1, d), lambda i: (i // tps, 0, 0, 0)),
                  pl.BlockSpec((None, None, 1, d), lambda i: (i // tps, 1, 0, 0)),
                  pl.BlockSpec((d, n), lambda i: (0, 0))],
        out_specs=pl.BlockSpec((tm, n), lambda i: (i, 0)),
        compiler_params=_cparams(("parallel",)),
        name="gdn_in_proj",
    )(x2, gain, mod4, mod4, w_bf)


GDN_HG = 2
GDN_RB = 256
GDN_SC = 128
GDN_SUB = 4


def _split_bf16(x):
    hi = x.astype(BF16)
    return hi, (x - hi.astype(F32)).astype(BF16)


def _mm3(a_hi, a_lo, b_hi, b_lo, dims=(((1,), (0,)), ((), ()))):
    dg = functools.partial(lax.dot_general, dimension_numbers=dims, preferred_element_type=F32)
    return dg(a_hi, b_hi) + (dg(a_lo, b_hi) + dg(a_hi, b_lo))


def _dot3(a, b, dims):
    return _mm3(*_split_bf16(a), *_split_bf16(b), dims)


def _gdn_body(alog_ref, dtb_ref, q_ref, k_ref, v_ref, z_ref, ab_ref, cq_ref, ck_ref, cv_ref, og_ref,
              o_ref, qs, ks, vs, bs, gcs, mp, bc, qp, op):
    seq = q_ref.shape[0]
    hd = A_HEAD_DIM
    ch = A_CHUNK
    n_chunks = seq // ch
    g_idx = pl.program_id(1)

    lane = lax.broadcasted_iota(jnp.int32, (GDN_RB, hd), 1)
    row = lax.broadcasted_iota(jnp.int32, (GDN_RB, hd), 0)
    row_in_chunk = row % ch

    def conv_silu(x_ref, c_ref, col0, r, start):
        main = x_ref[pl.ds(start, GDN_RB), col0:col0 + hd]
        pstart = pl.multiple_of(jnp.maximum(start - 8, 0), 8)
        prev = x_ref[pl.ds(pstart, 8), col0:col0 + hd] * jnp.where(r > 0, 1.0, 0.0)
        ext = jnp.concatenate([prev, main], axis=0)
        acc = main * c_ref[A_CONV - 1:A_CONV, col0:col0 + hd]
        for s in range(1, A_CONV):
            shifted = pltpu.roll(ext, s, 0)[8:8 + GDN_RB]
            acc = acc + shifted * c_ref[A_CONV - 1 - s:A_CONV - s, col0:col0 + hd]
        return _silu(acc)

    def l2n(x):
        return x * lax.rsqrt(jnp.sum(x * x, axis=-1, keepdims=True) + EPS)

    def prep(r, carry):
        start = pl.multiple_of(r * GDN_RB, GDN_RB)
        ab = ab_ref[pl.ds(start, GDN_RB), :]
        for hh in range(GDN_HG):
            col0 = hh * hd
            head = g_idx * GDN_HG + hh
            qs[hh, pl.ds(start, GDN_RB), :] = l2n(conv_silu(q_ref, cq_ref, col0, r, start)) * (hd ** -0.5)
            ks[hh, pl.ds(start, GDN_RB), :] = l2n(conv_silu(k_ref, ck_ref, col0, r, start))
            vs[hh, pl.ds(start, GDN_RB), :] = conv_silu(v_ref, cv_ref, col0, r, start)
            a_col = jnp.sum(jnp.where(lane == head, ab, 0.0), axis=-1, keepdims=True)
            b_col = jnp.sum(jnp.where(lane == head + A_HEADS, ab, 0.0), axis=-1, keepdims=True)
            bs[hh, pl.ds(start, GDN_RB), :] = jnp.broadcast_to(jax.nn.sigmoid(b_col), (GDN_RB, hd))
            xa = a_col + dtb_ref[0, head]
            softplus = jnp.maximum(xa, 0.0) + jnp.log1p(jnp.exp(-jnp.abs(xa)))
            g = jnp.broadcast_to(-jnp.exp(alog_ref[0, head]) * softplus, (GDN_RB, hd))
            s = 1
            while s < ch:
                g = g + jnp.where(row_in_chunk >= s, pltpu.roll(g, s, 0), 0.0)
                s *= 2
            gcs[hh, pl.ds(start, GDN_RB), :] = g
        return carry

    lax.fori_loop(0, seq // GDN_RB, prep, 0)

    sc_rows = GDN_SC
    per_sc = sc_rows // ch
    ri = lax.broadcasted_iota(jnp.int32, (sc_rows, sc_rows), 0)
    cj = lax.broadcasted_iota(jnp.int32, (sc_rows, sc_rows), 1)
    same_chunk = (ri // ch) == (cj // ch)
    incl = jnp.logical_and(same_chunk, ri >= cj)
    strict = jnp.logical_and(same_chunk, ri > cj)
    eye = jnp.where(ri == cj, 1.0, 0.0)
    chunk_of_col = lax.broadcasted_iota(jnp.int32, (hd, sc_rows), 1) // ch
    lanes_nt = (((1,), (1,)), ((), ()))
    plain = (((1,), (0,)), ((), ()))

    def local(i, carry):
        chains = [(i * GDN_SUB + sub, hh) for sub in range(GDN_SUB) for hh in range(GDN_HG)]
        rows_of = lambda blk: pl.ds(pl.multiple_of(blk * sc_rows, sc_rows), sc_rows)

        def start(blk, hh):
            rows = rows_of(blk)
            k = ks[hh, rows, :]
            gc = gcs[hh, rows, :]
            decay = jnp.where(incl, jnp.exp(jnp.minimum(gc - gc.T[0:1, :], 0.0)), 0.0)
            k_bf = k.astype(BF16)
            kk = lax.dot_general((k * bs[hh, rows, :]).astype(BF16), k_bf, lanes_nt, preferred_element_type=F32)
            qk = lax.dot_general(qs[hh, rows, :].astype(BF16), k_bf, lanes_nt, preferred_element_type=F32)
            return jnp.where(strict, -(kk * decay), 0.0), (qk * decay).astype(BF16)

        started = [start(*c) for c in chains]
        qks = [s[1] for s in started]
        invs = [eye + s[0] for s in started]
        pws = [_dot3(s[0], s[0], plain) for s in started]
        for step in range(1, 6):
            for n in range(len(chains)):
                pw_hi, pw_lo = _split_bf16(pws[n])
                inv_hi, inv_lo = _split_bf16(invs[n])
                if step < 5:
                    prod = _mm3(jnp.concatenate([pw_hi, inv_hi], axis=0),
                                jnp.concatenate([pw_lo, inv_lo], axis=0), pw_hi, pw_lo)
                    pws[n] = prod[:sc_rows]
                    invs[n] = invs[n] + prod[sc_rows:]
                else:
                    invs[n] = invs[n] + _mm3(inv_hi, inv_lo, pw_hi, pw_lo)

        def solve(n, blk, hh):
            rows = rows_of(blk)
            beta = bs[hh, rows, :]
            rhs = jnp.concatenate([vs[hh, rows, :] * beta, ks[hh, rows, :] * beta * jnp.exp(gcs[hh, rows, :])], axis=1)
            return _dot3(invs[n], rhs, plain).astype(BF16)

        uws = [solve(n, *c) for n, c in enumerate(chains)]

        def finish(n, blk, hh):
            rows = rows_of(blk)
            k = ks[hh, rows, :]
            gc = gcs[hh, rows, :]
            res = jnp.dot(qks[n], uws[n], preferred_element_type=F32)
            op[hh, rows, :] = res[:, :hd]
            qp[hh, rows, :] = (qs[hh, rows, :] * jnp.exp(gc) - res[:, hd:]).astype(BF16)
            gl = jnp.concatenate(
                [jnp.broadcast_to(gc[(j + 1) * ch - 1:(j + 1) * ch, :], (ch, hd)) for j in range(per_sc)], axis=0)
            kt_t = (k * jnp.exp(gl - gc)).T
            for j in range(per_sc):
                kt_j = jnp.where(chunk_of_col == j, kt_t, 0.0).astype(BF16)
                bm = jnp.dot(kt_j, uws[n], preferred_element_type=F32)
                bc[hh, blk * per_sc + j] = bm[:, :hd]
                mp[hh, blk * per_sc + j] = bm[:, hd:].astype(BF16)

        for n, c in enumerate(chains):
            finish(n, *c)
        return carry

    lax.fori_loop(0, seq // (sc_rows * GDN_SUB), local, 0)

    def scan(c, states):
        r0 = pl.multiple_of(c * ch, ch)
        rows = pl.ds(r0, ch)
        new_states = []
        for hh in range(GDN_HG):
            col0 = hh * hd
            state = states[hh]
            s_bf = state.astype(BF16)
            o = jnp.dot(qp[hh, rows, :], s_bf, preferred_element_type=F32) + op[hh, rows, :]
            g_tot = jnp.exp(gcs[hh, pl.ds(r0 + ch - 1, 1), :])
            new_states.append(state * g_tot - jnp.dot(mp[hh, c], s_bf, preferred_element_type=F32) + bc[hh, c])
            on = o * lax.rsqrt(jnp.mean(o * o, axis=-1, keepdims=True) + EPS) * og_ref[...]
            z = z_ref[rows, col0:col0 + hd]
            o_ref[rows, col0:col0 + hd] = (on * _silu(z)).astype(BF16)
        return tuple(new_states)

    lax.fori_loop(0, n_chunks, scan, tuple(jnp.zeros((hd, hd), F32) for _ in range(GDN_HG)))


def _gdn(proj, conv_w, a_log, dt_bias, out_gain, bsz, seq):
    t = proj.shape[0]
    hd = A_HEAD_DIM
    wb = hd * GDN_HG
    ng = A_HEADS // GDN_HG
    per = A_WIDTH // wb
    n_chunks = seq // A_CHUNK
    smem = pl.BlockSpec(memory_space=pltpu.SMEM)
    seq_spec = lambda off: pl.BlockSpec((seq, wb), lambda b, g, off=off: (b, off + g))
    conv_spec = lambda off: pl.BlockSpec((A_CONV, wb), lambda b, g, off=off: (0, off + g))
    sc = lambda dt=F32: pltpu.VMEM((GDN_HG, seq, hd), dt)
    return pl.pallas_call(
        _gdn_body,
        out_shape=jax.ShapeDtypeStruct((t, A_WIDTH), BF16),
        grid=(bsz, ng),
        in_specs=[smem, smem,
                  seq_spec(0), seq_spec(per), seq_spec(2 * per), seq_spec(3 * per),
                  pl.BlockSpec((seq, LANES), lambda b, g: (b, 4 * A_WIDTH // LANES)),
                  conv_spec(0), conv_spec(per), conv_spec(2 * per),
                  pl.BlockSpec((1, hd), lambda b, g: (0, 0))],
        out_specs=pl.BlockSpec((seq, wb), lambda b, g: (b, g)),
        scratch_shapes=[sc(), sc(), sc(), sc(), sc(),
                        pltpu.VMEM((GDN_HG, n_chunks, hd, hd), BF16),
                        pltpu.VMEM((GDN_HG, n_chunks, hd, hd), F32),
                        sc(BF16), sc()],
        compiler_params=_cparams(("parallel", "parallel")),
        name="gdn_core",
    )(a_log, dt_bias, proj, proj, proj, proj, proj, conv_w, conv_w, conv_w, out_gain)


def _oproj_body(a_ref, w_ref, x_ref, g_ref, o_ref):
    y = jnp.dot(a_ref[...], w_ref[...], preferred_element_type=F32)
    o_ref[...] = x_ref[...] + g_ref[...] * y


def _out_proj_residual(a_bf, w_bf, x2, mod4, gate_slot, seq):
    t, d = x2.shape
    kdim = a_bf.shape[-1]
    tm = 512
    tps = seq // tm
    return pl.pallas_call(
        _oproj_body,
        out_shape=jax.ShapeDtypeStruct((t, d), F32),
        grid=(t // tm,),
        in_specs=[pl.BlockSpec((tm, kdim), lambda i: (i, 0)),
                  pl.BlockSpec((kdim, d), lambda i: (0, 0)),
                  pl.BlockSpec((tm, d), lambda i: (i, 0)),
                  pl.BlockSpec((None, None, 1, d), lambda i: (i // tps, gate_slot, 0, 0))],
        out_specs=pl.BlockSpec((tm, d), lambda i: (i, 0)),
        compiler_params=_cparams(("parallel",)),
        name="out_proj_residual",
    )(a_bf, w_bf, x2, mod4)


ROUTER_TM = 512


def _router_body(x_ref, gain_ref, sh_ref, sc_ref, rw_ref, rb_ref, h_ref, idx_ref, gate_ref, rank_ref,
                 cnt_ref, carry):
    i = pl.program_id(0)

    @pl.when(i == 0)
    def _():
        carry[...] = jnp.zeros_like(carry)

    tm = x_ref.shape[0]
    h = _ada_norm(x_ref[...], gain_ref[...], sh_ref[...], sc_ref[...])
    h_ref[...] = h
    lane = lax.broadcasted_iota(jnp.int32, (tm, LANES), 1)
    logits = jnp.dot(h, rw_ref[...], preferred_element_type=F32, precision=HIGHEST) + rb_ref[...]
    logits = jnp.where(lane < N_EXPERTS, logits, NEG_INF)
    tops, sels = [], []
    work = logits
    for _ in range(TOP_K):
        m = jnp.max(work, axis=-1, keepdims=True)
        sel = jnp.min(jnp.where(work == m, lane, LANES), axis=-1, keepdims=True)
        work = jnp.where(lane == sel, NEG_INF, work)
        tops.append(m)
        sels.append(sel)
    exps = [jnp.exp(m - tops[0]) for m in tops]
    denom = exps[0] + exps[1] + exps[2] + exps[3]
    onehot = jnp.zeros((tm, LANES), F32)
    for sel in sels:
        onehot = onehot + (lane == sel).astype(F32)
    ti = lax.broadcasted_iota(jnp.int32, (tm, tm), 0)
    tj = lax.broadcasted_iota(jnp.int32, (tm, tm), 1)
    tri = (ti > tj).astype(BF16)
    before = jnp.dot(tri, onehot.astype(BF16), preferred_element_type=F32) + carry[...]
    idx_out = jnp.zeros((tm, LANES), jnp.int32)
    gate_out = jnp.zeros((tm, LANES), F32)
    rank_out = jnp.zeros((tm, LANES), jnp.int32)
    for k in range(TOP_K):
        rank_k = jnp.sum(jnp.where(lane == sels[k], before, 0.0), axis=-1, keepdims=True)
        idx_out = jnp.where(lane == k, sels[k], idx_out)
        gate_out = jnp.where(lane == k, exps[k] / denom, gate_out)
        rank_out = jnp.where(lane == k, rank_k.astype(jnp.int32), rank_out)
    idx_ref[...] = idx_out
    gate_ref[...] = gate_out
    rank_ref[...] = rank_out
    carry[...] = carry[...] + jnp.sum(onehot, axis=0, keepdims=True)
    cnt_ref[...] = carry[...]


def _router(x2, gain, mod4, sh_slot, sc_slot, rw_pad, rb_pad, seq):
    t, d = x2.shape
    tm = ROUTER_TM
    tps = seq // tm
    tok = lambda dt: jax.ShapeDtypeStruct((t, LANES), dt)
    tok_spec = pl.BlockSpec((tm, LANES), lambda i: (i, 0))
    return pl.pallas_call(
        _router_body,
        out_shape=(jax.ShapeDtypeStruct((t, d), F32), tok(jnp.int32), tok(F32), tok(jnp.int32),
                   jax.ShapeDtypeStruct((1, LANES), F32)),
        grid=(t // tm,),
        in_specs=[pl.BlockSpec((tm, d), lambda i: (i, 0)),
                  pl.BlockSpec((1, d), lambda i: (0, 0)),
                  pl.BlockSpec((None, None, 1, d), lambda i: (i // tps, sh_slot, 0, 0)),
                  pl.BlockSpec((None, None, 1, d), lambda i: (i // tps, sc_slot, 0, 0)),
                  pl.BlockSpec((d, LANES), lambda i: (0, 0)),
                  pl.BlockSpec((1, LANES), lambda i: (0, 0))],
        out_specs=(pl.BlockSpec((tm, d), lambda i: (i, 0)), tok_spec, tok_spec, tok_spec,
                   pl.BlockSpec((1, LANES), lambda i: (0, 0))),
        scratch_shapes=[pltpu.VMEM((1, LANES), F32)],
        compiler_params=_cparams(("arbitrary",)),
        name="moe_router",
    )(x2, gain, mod4, mod4, rw_pad, rb_pad)


DISPATCH_TM = 256


def _row_copy(src_ref, src_row, dst_ref, dst_row, sem):
    return pltpu.make_async_copy(src_ref.at[pl.ds(src_row, 1)], dst_ref.at[pl.ds(dst_row, 1)], sem)


def _dispatch_body(zb_ref, dest_ref, h_ref, xb_out, zbuf, sem, zsem):
    tm = h_ref.shape[0]
    bm = zbuf.shape[0]

    @pl.when(pl.program_id(0) == 0)
    def _():
        zbuf[...] = jnp.zeros_like(zbuf)

        def zero_copy(j):
            start = pl.multiple_of(jnp.maximum(zb_ref[j], 0) * bm, bm)
            return pltpu.make_async_copy(zbuf, xb_out.at[pl.ds(start, bm)], zsem)

        def zstart(j, carry):
            @pl.when(zb_ref[j] >= 0)
            def _():
                zero_copy(j).start()
            return carry

        def zwait(j, carry):
            @pl.when(zb_ref[j] >= 0)
            def _():
                zero_copy(j).wait()
            return carry

        lax.fori_loop(0, zb_ref.shape[0], zstart, 0)
        lax.fori_loop(0, zb_ref.shape[0], zwait, 0)

    def issue(r, carry):
        for k in range(TOP_K):
            _row_copy(h_ref, r, xb_out, dest_ref[0, r * TOP_K + k], sem).start()
        return carry

    lax.fori_loop(0, tm, issue, 0)

    def drain(r, carry):
        for k in range(TOP_K):
            _row_copy(h_ref, 0, xb_out, 0, sem).wait()
        return carry

    lax.fori_loop(0, tm, drain, 0)


def _dispatch(h2, dest3, zero_blocks, n_rows):
    t, d = h2.shape
    tm = DISPATCH_TM
    grid_spec = pltpu.PrefetchScalarGridSpec(
        num_scalar_prefetch=1,
        grid=(t // tm,),
        in_specs=[pl.BlockSpec((None, 1, tm * TOP_K), lambda i, zb: (i, 0, 0), memory_space=pltpu.SMEM),
                  pl.BlockSpec((tm, d), lambda i, zb: (i, 0))],
        out_specs=pl.BlockSpec(memory_space=pl.ANY),
        scratch_shapes=[pltpu.VMEM((MOE_ROWS, d), h2.dtype), pltpu.SemaphoreType.DMA(()),
                        pltpu.SemaphoreType.DMA(())],
    )
    return pl.pallas_call(
        _dispatch_body,
        out_shape=jax.ShapeDtypeStruct((n_rows, d), h2.dtype),
        grid_spec=grid_spec,
        compiler_params=_cparams(("arbitrary",)),
        name="moe_dispatch",
    )(zero_blocks, dest3, h2)


def _ffn_body(be_ref, nu_ref, x_ref, uw_ref, ub_ref, dw_ref, db_ref, y_ref, uw_bf, dw_bf):
    i = pl.program_id(0)
    changed = jnp.logical_or(i == 0, be_ref[i] != be_ref[jnp.maximum(i - 1, 0)])
    active = i < nu_ref[0]

    @pl.when(jnp.logical_and(changed, active))
    def _():
        rows = 128
        for r0 in range(0, uw_ref.shape[0], rows):
            uw_bf[r0:r0 + rows, :] = uw_ref[r0:r0 + rows, :].astype(BF16)
        for r0 in range(0, dw_ref.shape[0], rows):
            dw_bf[r0:r0 + rows, :] = dw_ref[r0:r0 + rows, :].astype(BF16)

    @pl.when(active)
    def _():
        x = x_ref[...].astype(BF16)
        gu = jnp.dot(x, uw_bf[...], preferred_element_type=F32) + ub_ref[...]
        gate = jnp.minimum(gu[:, :D_FF], SWIGLU_LIMIT)
        lin = jnp.clip(gu[:, D_FF:], -SWIGLU_LIMIT, SWIGLU_LIMIT)
        act = gate * jax.nn.sigmoid(SWIGLU_ALPHA * gate) * (lin + 1.0)
        y_ref[...] = jnp.dot(act.astype(BF16), dw_bf[...], preferred_element_type=F32) + db_ref[...]

    @pl.when(jnp.logical_not(active))
    def _():
        y_ref[...] = jnp.zeros_like(y_ref)


def _expert_ffn(xb, blk_expert, n_used, up_w, up_b4, down_w, down_b4, layer):
    n_rows, d = xb.shape
    bm = MOE_ROWS
    n_blk = n_rows // bm
    f2 = up_w.shape[-1]
    grid_spec = pltpu.PrefetchScalarGridSpec(
        num_scalar_prefetch=2,
        grid=(n_blk,),
        in_specs=[pl.BlockSpec((bm, d), lambda i, be, nu: (i, 0)),
                  pl.BlockSpec((None, None, d, f2), lambda i, be, nu: (layer, be[i], 0, 0)),
                  pl.BlockSpec((None, None, 1, f2), lambda i, be, nu: (layer, be[i], 0, 0)),
                  pl.BlockSpec((None, None, f2 // 2, d), lambda i, be, nu: (layer, be[i], 0, 0)),
                  pl.BlockSpec((None, None, 1, d), lambda i, be, nu: (layer, be[i], 0, 0))],
        out_specs=pl.BlockSpec((bm, d), lambda i, be, nu: (i, 0)),
        scratch_shapes=[pltpu.VMEM((d, f2), BF16), pltpu.VMEM((f2 // 2, d), BF16)],
    )
    return pl.pallas_call(
        _ffn_body,
        out_shape=jax.ShapeDtypeStruct((n_rows, d), F32),
        grid_spec=grid_spec,
        compiler_params=_cparams(("arbitrary",)),
        name="moe_expert_ffn",
    )(blk_expert, n_used, xb, up_w, up_b4, down_w, down_b4)


COMBINE_TM = 128


def _combine_body(dest_ref, gate_ref, x_ref, g_ref, yrows, o_ref, buf, sem):
    tm = x_ref.shape[0]

    def issue(r, carry):
        for k in range(TOP_K):
            _row_copy(yrows, dest_ref[0, r * TOP_K + k], buf, k * tm + r, sem).start()
        return carry

    lax.fori_loop(0, tm, issue, 0)

    def drain(r, carry):
        for k in range(TOP_K):
            _row_copy(yrows, 0, buf, 0, sem).wait()
        return carry

    lax.fori_loop(0, tm, drain, 0)

    gates = gate_ref[...]
    y = gates[:, 0:1] * buf[0:tm, :]
    for k in range(1, TOP_K):
        y = y + gates[:, k:k + 1] * buf[k * tm:(k + 1) * tm, :]
    o_ref[...] = x_ref[...] + g_ref[...] * y


def _combine(y_rows, dest3, gates, x2, mod4, gate_slot, seq):
    t, d = x2.shape
    tm = COMBINE_TM
    tps = seq // tm
    return pl.pallas_call(
        _combine_body,
        out_shape=jax.ShapeDtypeStruct((t, d), F32),
        grid=(t // tm,),
        in_specs=[pl.BlockSpec((None, 1, tm * TOP_K), lambda i: (i, 0, 0), memory_space=pltpu.SMEM),
                  pl.BlockSpec((tm, LANES), lambda i: (i, 0)),
                  pl.BlockSpec((tm, d), lambda i: (i, 0)),
                  pl.BlockSpec((None, None, 1, d), lambda i: (i // tps, gate_slot, 0, 0)),
                  pl.BlockSpec(memory_space=pl.ANY)],
        out_specs=pl.BlockSpec((tm, d), lambda i: (i, 0)),
        scratch_shapes=[pltpu.VMEM((TOP_K * tm, d), y_rows.dtype), pltpu.SemaphoreType.DMA(())],
        compiler_params=_cparams(("arbitrary",)),
        name="moe_combine",
    )(dest3, gates, x2, mod4, y_rows)


def _moe_layer(x2, gain, mod4, router_w, router_b, up_w, up_b, down_w, down_b, layer, seq):
    t, d = x2.shape
    rw_pad = jnp.pad(router_w[layer], ((0, 0), (0, LANES - N_EXPERTS)))
    rb_pad = jnp.pad(router_b[layer], (0, LANES - N_EXPERTS)).reshape(1, LANES)
    h2, idx, gates, rank, counts = _router(x2, gain, mod4, 3, 4, rw_pad, rb_pad, seq)

    bm = MOE_ROWS
    counts = counts[0, :N_EXPERTS].astype(jnp.int32)
    padded = (counts + bm - 1) // bm * bm
    p_ends = jnp.cumsum(padded)
    p_starts = p_ends - padded
    n_rows = t * TOP_K + N_EXPERTS * bm
    n_blk = n_rows // bm
    top_idx = idx[:, :TOP_K]
    onehot = top_idx[:, :, None] == jnp.arange(N_EXPERTS, dtype=jnp.int32)[None, None, :]
    dest = jnp.sum(jnp.where(onehot, p_starts[None, None, :], 0), axis=-1) + rank[:, :TOP_K]
    blk_start = jnp.arange(n_blk, dtype=jnp.int32) * bm
    blk_expert = jnp.sum((blk_start[:, None] >= p_ends[None, :]).astype(jnp.int32), axis=-1)
    blk_expert = jnp.minimum(blk_expert, N_EXPERTS - 1).astype(jnp.int32)
    n_used = (p_ends[-1:] // bm).astype(jnp.int32)
    last_blk = jnp.where(padded > 0, p_ends // bm - 1, -1)
    tail_blk = n_used + jnp.arange(N_EXPERTS, dtype=jnp.int32)
    zero_blocks = jnp.concatenate([last_blk, jnp.where(tail_blk < n_blk, tail_blk, -1)]).astype(jnp.int32)

    xb = _dispatch(h2, dest.reshape(t // DISPATCH_TM, 1, DISPATCH_TM * TOP_K), zero_blocks, n_rows)
    f2 = up_w.shape[-1]
    y_rows = _expert_ffn(xb, blk_expert, n_used, up_w, up_b.reshape(up_b.shape[0], N_EXPERTS, 1, f2),
                         down_w, down_b.reshape(down_b.shape[0], N_EXPERTS, 1, d), layer)
    return _combine(y_rows, dest.reshape(t // COMBINE_TM, 1, COMBINE_TM * TOP_K), gates, x2, mod4, 5, seq)


QKV_TM = 256


def _rope_tables(pos_col, invf_row):
    tm = pos_col.shape[0]
    ang = pos_col * invf_row
    c = jnp.cos(ang)
    s = jnp.sin(ang)
    d = lax.broadcasted_iota(jnp.int32, (tm, LANES), 1) % B_HEAD_DIM
    half = ROPE_DIM // 2
    cos_t = jnp.where(d < ROPE_DIM, c, 1.0)
    sin_lo = jnp.where(d < half, -s, 0.0)
    sin_hi = jnp.where(jnp.logical_and(d >= half, d < ROPE_DIM), s, 0.0)
    return cos_t, sin_lo, sin_hi


def _head_norm_rope(x, gain_row, tables):
    cos_t, sin_lo, sin_hi = tables
    lane = lax.broadcasted_iota(jnp.int32, x.shape, 1)
    sq = x * x
    first = lane < B_HEAD_DIM
    ss0 = jnp.sum(jnp.where(first, sq, 0.0), axis=-1, keepdims=True)
    ss1 = jnp.sum(jnp.where(first, 0.0, sq), axis=-1, keepdims=True)
    ms = jnp.where(first, ss0, ss1) * (1.0 / B_HEAD_DIM)
    xn = x * lax.rsqrt(ms + EPS) * gain_row
    half = ROPE_DIM // 2
    return (xn * cos_t + pltpu.roll(xn, LANES - half, 1) * sin_lo + pltpu.roll(xn, half, 1) * sin_hi)


def _qkv_body(x_ref, pos_ref, gq_ref, shq_ref, scq_ref, gkv_ref, shkv_ref, sckv_ref, wq_ref, wkv_ref,
              qg_ref, kg_ref, invf_ref, q_ref, k_ref, v_ref):
    x = x_ref[...]
    y = x * lax.rsqrt(jnp.mean(x * x, axis=-1, keepdims=True) + EPS)
    hq = ((y * gq_ref[...]) * (1.0 + scq_ref[...]) + shq_ref[...]).astype(BF16)
    hkv = ((y * gkv_ref[...]) * (1.0 + sckv_ref[...]) + shkv_ref[...]).astype(BF16)
    tables = _rope_tables(pos_ref[...].astype(F32), invf_ref[...])
    kv = jnp.dot(hkv, wkv_ref[...], preferred_element_type=F32)
    k_ref[...] = _head_norm_rope(kv[:, :LANES], kg_ref[...], tables).astype(BF16)
    v_ref[...] = kv[:, LANES:].astype(BF16)
    q = jnp.dot(hq, wq_ref[...], preferred_element_type=F32)
    scale = B_HEAD_DIM ** -0.5
    for p in range(q.shape[-1] // LANES):
        qp = _head_norm_rope(q[:, p * LANES:(p + 1) * LANES], qg_ref[...], tables)
        q_ref[:, p * LANES:(p + 1) * LANES] = (qp * scale).astype(BF16)


def _qkv(x2, pos2, gq, mod4, gkv, kvmod4, wq_bf, wkv_bf, qg2, kg2, invf, seq):
    t, d = x2.shape
    tm = QKV_TM
    tps = seq // tm
    nq = wq_bf.shape[-1]
    nkv = wkv_bf.shape[-1]
    row = lambda n: pl.BlockSpec((1, n), lambda i: (0, 0))
    modspec = lambda slot: pl.BlockSpec((None, None, 1, d), lambda i, slot=slot: (i // tps, slot, 0, 0))
    return pl.pallas_call(
        _qkv_body,
        out_shape=(jax.ShapeDtypeStruct((t, nq), BF16), jax.ShapeDtypeStruct((t, LANES), BF16),
                   jax.ShapeDtypeStruct((t, LANES), BF16)),
        grid=(t // tm,),
        in_specs=[pl.BlockSpec((tm, d), lambda i: (i, 0)),
                  pl.BlockSpec((tm, 1), lambda i: (i, 0)),
                  row(d), modspec(0), modspec(1),
                  row(d), modspec(0), modspec(1),
                  pl.BlockSpec((d, nq), lambda i: (0, 0)),
                  pl.BlockSpec((d, nkv), lambda i: (0, 0)),
                  row(LANES), row(LANES), row(LANES)],
        out_specs=(pl.BlockSpec((tm, nq), lambda i: (i, 0)),
                   pl.BlockSpec((tm, LANES), lambda i: (i, 0)),
                   pl.BlockSpec((tm, LANES), lambda i: (i, 0))),
        compiler_params=_cparams(("parallel",)),
        name="swa_qkv_proj",
    )(x2, pos2, gq, mod4, mod4, gkv, kvmod4, kvmod4, wq_bf, wkv_bf, qg2, kg2, invf)


def _attn_body(sink_ref, q_ref, kc_ref, kp_ref, vc_ref, vp_ref, o_ref):
    n = pl.program_id(1)
    w = WINDOW
    qi = lax.broadcasted_iota(jnp.int32, (w, 2 * w), 0) + w
    kj = lax.broadcasted_iota(jnp.int32, (w, 2 * w), 1)
    band = jnp.logical_and(kj <= qi, qi - kj < w)
    mask = jnp.logical_and(band, jnp.logical_or(n > 0, kj >= w))
    hd = B_HEAD_DIM
    for g in range(B_KV_HEADS):
        kb = jnp.concatenate([kp_ref[:, g * hd:(g + 1) * hd], kc_ref[:, g * hd:(g + 1) * hd]], axis=0)
        vb = jnp.concatenate([vp_ref[:, g * hd:(g + 1) * hd], vc_ref[:, g * hd:(g + 1) * hd]], axis=0)
        for hh in range(B_GROUP):
            head = g * B_GROUP + hh
            qh = q_ref[:, head * hd:(head + 1) * hd]
            s = lax.dot_general(qh, kb, (((1,), (1,)), ((), ())), preferred_element_type=F32)
            s = jnp.where(mask, s, NEG_INF)
            sink = sink_ref[0, head]
            m = jnp.maximum(jnp.max(s, axis=-1, keepdims=True), sink)
            p = jnp.exp(s - m)
            denom = jnp.sum(p, axis=-1, keepdims=True) + jnp.exp(sink - m)
            o = jnp.dot(p.astype(BF16), vb, preferred_element_type=F32) / denom
            o_ref[:, head * hd:(head + 1) * hd] = o.astype(BF16)


def _attention(q, k, v, sinks2, bsz, seq):
    t, nq = q.shape
    w = WINDOW
    nb = seq // w
    cur = lambda b, n: (b * nb + n, 0)
    prev = lambda b, n: (b * nb + jnp.maximum(n - 1, 0), 0)
    return pl.pallas_call(
        _attn_body,
        out_shape=jax.ShapeDtypeStruct((t, nq), BF16),
        grid=(bsz, nb),
        in_specs=[pl.BlockSpec(memory_space=pltpu.SMEM),
                  pl.BlockSpec((w, nq), cur),
                  pl.BlockSpec((w, LANES), cur), pl.BlockSpec((w, LANES), prev),
                  pl.BlockSpec((w, LANES), cur), pl.BlockSpec((w, LANES), prev)],
        out_specs=pl.BlockSpec((w, nq), cur),
        compiler_params=_cparams(("parallel", "parallel")),
        name="swa_sink_attention",
    )(sinks2, q, k, k, v, v)


def kernel(x, c, positions, ada_w, ada_b, norm_gain, a_w_in, a_conv, a_log, a_dt_bias, a_out_gain, a_w_out,
           kv_ada_w, kv_ada_b, kv_norm_gain, kv_w, k_norm_gain, b_w_q, q_norm_gain, b_sinks, b_w_out,
           router_w, router_b, up_w, up_b, down_w, down_b):
    bsz, seq, d = x.shape
    t = bsz * seq
    depth = ada_w.shape[0]
    x2 = x.reshape(t, d)
    ada_b3 = ada_b.reshape(depth, 1, 6 * d)

    for layer in range(depth):
        mod4 = _modulation(c, ada_w, ada_b3, layer).reshape(bsz, 6, 1, d)
        gain1 = norm_gain[layer, 0].reshape(1, d)
        gain2 = norm_gain[layer, 1].reshape(1, d)
        if layer < N_A_LAYERS:
            w_in = jnp.pad(a_w_in[layer].astype(BF16), ((0, 0), (0, A_PROJ_PAD - a_w_in.shape[-1])))
            proj = _in_proj(x2, gain1, mod4, w_in, seq)
            o = _gdn(proj, a_conv[layer], a_log[layer].reshape(1, A_HEADS),
                     a_dt_bias[layer].reshape(1, A_HEADS), a_out_gain[layer].reshape(1, A_HEAD_DIM),
                     bsz, seq)
            x2 = _out_proj_residual(o, a_w_out[layer].astype(BF16), x2, mod4, 2, seq)
        else:
            j = layer - N_A_LAYERS
            kvmod4 = _modulation(c, kv_ada_w.reshape(1, d, 2 * d), kv_ada_b.reshape(1, 1, 2 * d), 0)
            kvmod4 = kvmod4.reshape(bsz, 2, 1, d)
            half = ROPE_DIM // 2
            inv_freq = ROPE_THETA ** (-np.arange(0, ROPE_DIM, 2, dtype=np.float32) / ROPE_DIM)
            invf = jnp.asarray(np.tile(inv_freq.astype(np.float32), LANES // half).reshape(1, LANES))
            q, k, v = _qkv(x2, positions.reshape(t, 1), gain1, mod4, kv_norm_gain.reshape(1, d), kvmod4,
                           b_w_q[j].astype(BF16), kv_w.astype(BF16),
                           jnp.tile(q_norm_gain[j], 2).reshape(1, LANES),
                           jnp.tile(k_norm_gain, 2).reshape(1, LANES), invf, seq)
            o = _attention(q, k, v, b_sinks[j].reshape(1, B_Q_HEADS), bsz, seq)
            x2 = _out_proj_residual(o, b_w_out[j].astype(BF16), x2, mod4, 2, seq)
        x2 = _moe_layer(x2, gain2, mod4, router_w, router_b, up_w, up_b, down_w, down_b, layer, seq)
    return x2.reshape(bsz, seq, d)
```

```python
import functools

import jax
import jax.numpy as jnp
import numpy as np
from jax import lax
from jax.experimental import pallas as pl
from jax.experimental.pallas import tpu as pltpu
from jax.experimental.pallas import tpu_sc as plsc

F32 = jnp.float32
BF16 = jnp.bfloat16
HIGHEST = lax.Precision.HIGHEST

D_MODEL = 1024
N_A_LAYERS = 1

A_HEAD_DIM = 128
A_HEADS = 8
A_WIDTH = 1024
A_CONV = 4
A_CHUNK = 64
A_PROJ_PAD = 4608

B_HEAD_DIM = 64
B_Q_HEADS = 16
B_KV_HEADS = 2
B_GROUP = 8
WINDOW = 128
ROPE_DIM = 16
ROPE_THETA = 500000.0

N_EXPERTS = 32
TOP_K = 4
D_FF = 1024
SWIGLU_LIMIT = 7.0
SWIGLU_ALPHA = 1.702
MOE_ROWS = 256

EPS = 1e-6
LANES = 128
NEG_INF = float("-inf")

VMEM_LIMIT = 56 * 1024 * 1024


def _cparams(sem, vmem=VMEM_LIMIT):
    return pltpu.CompilerParams(dimension_semantics=sem, vmem_limit_bytes=vmem)


def _silu(x):
    return x * jax.nn.sigmoid(x)


def _ada_norm(x, gain, shift, scale):
    y = x * lax.rsqrt(jnp.mean(x * x, axis=-1, keepdims=True) + EPS)
    return (y * gain) * (1.0 + scale) + shift


def _mod_body(c_ref, w_ref, b_ref, o_ref):
    o_ref[...] = jnp.dot(_silu(c_ref[...]), w_ref[...], preferred_element_type=F32,
                         precision=HIGHEST) + b_ref[...]


def _modulation(c, w3, b3, layer):
    bsz, d = c.shape
    n = w3.shape[-1]
    tn = 1024
    return pl.pallas_call(
        _mod_body,
        out_shape=jax.ShapeDtypeStruct((bsz, n), F32),
        grid=(n // tn,),
        in_specs=[pl.BlockSpec((bsz, d), lambda j: (0, 0)),
                  pl.BlockSpec((None, d, tn), lambda j: (layer, 0, j)),
                  pl.BlockSpec((None, 1, tn), lambda j: (layer, 0, j))],
        out_specs=pl.BlockSpec((bsz, tn), lambda j: (0, j)),
        compiler_params=_cparams(("arbitrary",)),
        name="adaln_mod",
    )(c, w3, b3)


def _inproj_body(x_ref, gain_ref, sh_ref, sc_ref, w_ref, o_ref):
    h = _ada_norm(x_ref[...], gain_ref[...], sh_ref[...], sc_ref[...]).astype(BF16)
    n = o_ref.shape[-1]
    for n0 in range(0, n, 512):
        o_ref[:, n0:n0 + 512] = jnp.dot(h, w_ref[:, n0:n0 + 512], preferred_element_type=F32)


def _in_proj(x2, gain, mod4, w_bf, seq):
    t, d = x2.shape
    n = w_bf.shape[-1]
    tm = 256
    tps = seq // tm
    return pl.pallas_call(
        _inproj_body,
        out_shape=jax.ShapeDtypeStruct((t, n), F32),
        grid=(t // tm,),
        in_specs=[pl.BlockSpec((tm, d), lambda i: (i, 0)),
                  pl.BlockSpec((1, d), lambda i: (0, 0)),
                  pl.BlockSpec((None, None, 1, d), lambda i: (i // tps, 0, 0, 0)),
                  pl.BlockSpec((None, None, 1, d), lambda i: (i // tps, 1, 0, 0)),
                  pl.BlockSpec((d, n), lambda i: (0, 0))],
        out_specs=pl.BlockSpec((tm, n), lambda i: (i, 0)),
        compiler_params=_cparams(("parallel",)),
        name="gdn_in_proj",
    )(x2, gain, mod4, mod4, w_bf)


GDN_HG = 2
GDN_RB = 256
GDN_SC = 128
GDN_SUB = 4


def _split_bf16(x):
    hi = x.astype(BF16)
    return hi, (x - hi.astype(F32)).astype(BF16)


def _mm3(a_hi, a_lo, b_hi, b_lo, dims=(((1,), (0,)), ((), ()))):
    dg = functools.partial(lax.dot_general, dimension_numbers=dims, preferred_element_type=F32)
    return dg(a_hi, b_hi) + (dg(a_lo, b_hi) + dg(a_hi, b_lo))


def _dot3(a, b, dims):
    return _mm3(*_split_bf16(a), *_split_bf16(b), dims)


def _gdn_body(alog_ref, dtb_ref, q_ref, k_ref, v_ref, z_ref, ab_ref, cq_ref, ck_ref, cv_ref, og_ref,
              o_ref, qs, ks, vs, bs, gcs, mp, bc, qp, op):
    seq = q_ref.shape[0]
    hd = A_HEAD_DIM
    ch = A_CHUNK
    n_chunks = seq // ch
    g_idx = pl.program_id(1)

    lane = lax.broadcasted_iota(jnp.int32, (GDN_RB, hd), 1)
    row = lax.broadcasted_iota(jnp.int32, (GDN_RB, hd), 0)
    row_in_chunk = row % ch

    def conv_silu(x_ref, c_ref, col0, r, start):
        main = x_ref[pl.ds(start, GDN_RB), col0:col0 + hd]
        pstart = pl.multiple_of(jnp.maximum(start - 8, 0), 8)
        prev = x_ref[pl.ds(pstart, 8), col0:col0 + hd] * jnp.where(r > 0, 1.0, 0.0)
        ext = jnp.concatenate([prev, main], axis=0)
        acc = main * c_ref[A_CONV - 1:A_CONV, col0:col0 + hd]
        for s in range(1, A_CONV):
            shifted = pltpu.roll(ext, s, 0)[8:8 + GDN_RB]
            acc = acc + shifted * c_ref[A_CONV - 1 - s:A_CONV - s, col0:col0 + hd]
        return _silu(acc)

    def l2n(x):
        return x * lax.rsqrt(jnp.sum(x * x, axis=-1, keepdims=True) + EPS)

    def prep(r, carry):
        start = pl.multiple_of(r * GDN_RB, GDN_RB)
        ab = ab_ref[pl.ds(start, GDN_RB), :]
        for hh in range(GDN_HG):
            col0 = hh * hd
            head = g_idx * GDN_HG + hh
            qs[hh, pl.ds(start, GDN_RB), :] = l2n(conv_silu(q_ref, cq_ref, col0, r, start)) * (hd ** -0.5)
            ks[hh, pl.ds(start, GDN_RB), :] = l2n(conv_silu(k_ref, ck_ref, col0, r, start))
            vs[hh, pl.ds(start, GDN_RB), :] = conv_silu(v_ref, cv_ref, col0, r, start)
            a_col = jnp.sum(jnp.where(lane == head, ab, 0.0), axis=-1, keepdims=True)
            b_col = jnp.sum(jnp.where(lane == head + A_HEADS, ab, 0.0), axis=-1, keepdims=True)
            bs[hh, pl.ds(start, GDN_RB), :] = jnp.broadcast_to(jax.nn.sigmoid(b_col), (GDN_RB, hd))
            xa = a_col + dtb_ref[0, head]
            softplus = jnp.maximum(xa, 0.0) + jnp.log1p(jnp.exp(-jnp.abs(xa)))
            g = jnp.broadcast_to(-jnp.exp(alog_ref[0, head]) * softplus, (GDN_RB, hd))
            s = 1
            while s < ch:
                g = g + jnp.where(row_in_chunk >= s, pltpu.roll(g, s, 0), 0.0)
                s *= 2
            gcs[hh, pl.ds(start, GDN_RB), :] = g
        return carry

    lax.fori_loop(0, seq // GDN_RB, prep, 0)

    sc_rows = GDN_SC
    per_sc = sc_rows // ch
    ri = lax.broadcasted_iota(jnp.int32, (sc_rows, sc_rows), 0)
    cj = lax.broadcasted_iota(jnp.int32, (sc_rows, sc_rows), 1)
    same_chunk = (ri // ch) == (cj // ch)
    incl = jnp.logical_and(same_chunk, ri >= cj)
    strict = jnp.logical_and(same_chunk, ri > cj)
    eye = jnp.where(ri == cj, 1.0, 0.0)
    chunk_of_col = lax.broadcasted_iota(jnp.int32, (hd, sc_rows), 1) // ch
    lanes_nt = (((1,), (1,)), ((), ()))
    plain = (((1,), (0,)), ((), ()))

    def local(i, carry):
        chains = [(i * GDN_SUB + sub, hh) for sub in range(GDN_SUB) for hh in range(GDN_HG)]
        rows_of = lambda blk: pl.ds(pl.multiple_of(blk * sc_rows, sc_rows), sc_rows)

        def start(blk, hh):
            rows = rows_of(blk)
            k = ks[hh, rows, :]
            gc = gcs[hh, rows, :]
            decay = jnp.where(incl, jnp.exp(jnp.minimum(gc - gc.T[0:1, :], 0.0)), 0.0)
            k_bf = k.astype(BF16)
            kk = lax.dot_general((k * bs[hh, rows, :]).astype(BF16), k_bf, lanes_nt, preferred_element_type=F32)
            qk = lax.dot_general(qs[hh, rows, :].astype(BF16), k_bf, lanes_nt, preferred_element_type=F32)
            return jnp.where(strict, -(kk * decay), 0.0), (qk * decay).astype(BF16)

        started = [start(*c) for c in chains]
        qks = [s[1] for s in started]
        invs = [eye + s[0] for s in started]
        pws = [_dot3(s[0], s[0], plain) for s in started]
        for step in range(1, 6):
            for n in range(len(chains)):
                pw_hi, pw_lo = _split_bf16(pws[n])
                inv_hi, inv_lo = _split_bf16(invs[n])
                if step < 5:
                    prod = _mm3(jnp.concatenate([pw_hi, inv_hi], axis=0),
                                jnp.concatenate([pw_lo, inv_lo], axis=0), pw_hi, pw_lo)
                    pws[n] = prod[:sc_rows]
                    invs[n] = invs[n] + prod[sc_rows:]
                else:
                    invs[n] = invs[n] + _mm3(inv_hi, inv_lo, pw_hi, pw_lo)

        def solve(n, blk, hh):
            rows = rows_of(blk)
            beta = bs[hh, rows, :]
            rhs = jnp.concatenate([vs[hh, rows, :] * beta, ks[hh, rows, :] * beta * jnp.exp(gcs[hh, rows, :])], axis=1)
            return _dot3(invs[n], rhs, plain).astype(BF16)

        uws = [solve(n, *c) for n, c in enumerate(chains)]

        def finish(n, blk, hh):
            rows = rows_of(blk)
            k = ks[hh, rows, :]
            gc = gcs[hh, rows, :]
            res = jnp.dot(qks[n], uws[n], preferred_element_type=F32)
            op[hh, rows, :] = res[:, :hd]
            qp[hh, rows, :] = (qs[hh, rows, :] * jnp.exp(gc) - res[:, hd:]).astype(BF16)
            gl = jnp.concatenate(
                [jnp.broadcast_to(gc[(j + 1) * ch - 1:(j + 1) * ch, :], (ch, hd)) for j in range(per_sc)], axis=0)
            kt_t = (k * jnp.exp(gl - gc)).T
            for j in range(per_sc):
                kt_j = jnp.where(chunk_of_col == j, kt_t, 0.0).astype(BF16)
                bm = jnp.dot(kt_j, uws[n], preferred_element_type=F32)
                bc[hh, blk * per_sc + j] = bm[:, :hd]
                mp[hh, blk * per_sc + j] = bm[:, hd:].astype(BF16)

        for n, c in enumerate(chains):
            finish(n, *c)
        return carry

    lax.fori_loop(0, seq // (sc_rows * GDN_SUB), local, 0)

    def scan(c, states):
        r0 = pl.multiple_of(c * ch, ch)
        rows = pl.ds(r0, ch)
        new_states = []
        for hh in range(GDN_HG):
            col0 = hh * hd
            state = states[hh]
            s_bf = state.astype(BF16)
            o = jnp.dot(qp[hh, rows, :], s_bf, preferred_element_type=F32) + op[hh, rows, :]
            g_tot = jnp.exp(gcs[hh, pl.ds(r0 + ch - 1, 1), :])
            new_states.append(state * g_tot - jnp.dot(mp[hh, c], s_bf, preferred_element_type=F32) + bc[hh, c])
            on = o * lax.rsqrt(jnp.mean(o * o, axis=-1, keepdims=True) + EPS) * og_ref[...]
            z = z_ref[rows, col0:col0 + hd]
            o_ref[rows, col0:col0 + hd] = (on * _silu(z)).astype(BF16)
        return tuple(new_states)

    lax.fori_loop(0, n_chunks, scan, tuple(jnp.zeros((hd, hd), F32) for _ in range(GDN_HG)))


def _gdn(proj, conv_w, a_log, dt_bias, out_gain, bsz, seq):
    t = proj.shape[0]
    hd = A_HEAD_DIM
    wb = hd * GDN_HG
    ng = A_HEADS // GDN_HG
    per = A_WIDTH // wb
    n_chunks = seq // A_CHUNK
    smem = pl.BlockSpec(memory_space=pltpu.SMEM)
    seq_spec = lambda off: pl.BlockSpec((seq, wb), lambda b, g, off=off: (b, off + g))
    conv_spec = lambda off: pl.BlockSpec((A_CONV, wb), lambda b, g, off=off: (0, off + g))
    sc = lambda dt=F32: pltpu.VMEM((GDN_HG, seq, hd), dt)
    return pl.pallas_call(
        _gdn_body,
        out_shape=jax.ShapeDtypeStruct((t, A_WIDTH), BF16),
        grid=(bsz, ng),
        in_specs=[smem, smem,
                  seq_spec(0), seq_spec(per), seq_spec(2 * per), seq_spec(3 * per),
                  pl.BlockSpec((seq, LANES), lambda b, g: (b, 4 * A_WIDTH // LANES)),
                  conv_spec(0), conv_spec(per), conv_spec(2 * per),
                  pl.BlockSpec((1, hd), lambda b, g: (0, 0))],
        out_specs=pl.BlockSpec((seq, wb), lambda b, g: (b, g)),
        scratch_shapes=[sc(), sc(), sc(), sc(), sc(),
                        pltpu.VMEM((GDN_HG, n_chunks, hd, hd), BF16),
                        pltpu.VMEM((GDN_HG, n_chunks, hd, hd), F32),
                        sc(BF16), sc()],
        compiler_params=_cparams(("parallel", "parallel")),
        name="gdn_core",
    )(a_log, dt_bias, proj, proj, proj, proj, proj, conv_w, conv_w, conv_w, out_gain)


def _oproj_body(a_ref, w_ref, x_ref, g_ref, o_ref):
    y = jnp.dot(a_ref[...], w_ref[...], preferred_element_type=F32)
    o_ref[...] = x_ref[...] + g_ref[...] * y


def _out_proj_residual(a_bf, w_bf, x2, mod4, gate_slot, seq):
    t, d = x2.shape
    kdim = a_bf.shape[-1]
    tm = 512
    tps = seq // tm
    return pl.pallas_call(
        _oproj_body,
        out_shape=jax.ShapeDtypeStruct((t, d), F32),
        grid=(t // tm,),
        in_specs=[pl.BlockSpec((tm, kdim), lambda i: (i, 0)),
                  pl.BlockSpec((kdim, d), lambda i: (0, 0)),
                  pl.BlockSpec((tm, d), lambda i: (i, 0)),
                  pl.BlockSpec((None, None, 1, d), lambda i: (i // tps, gate_slot, 0, 0))],
        out_specs=pl.BlockSpec((tm, d), lambda i: (i, 0)),
        compiler_params=_cparams(("parallel",)),
        name="out_proj_residual",
    )(a_bf, w_bf, x2, mod4)


ROUTER_TM = 512


def _router_body(x_ref, gain_ref, sh_ref, sc_ref, rw_ref, rb_ref, h_ref, idx_ref, gate_ref, rank_ref,
                 cnt_ref, carry):
    i = pl.program_id(0)

    @pl.when(i == 0)
    def _():
        carry[...] = jnp.zeros_like(carry)

    tm = x_ref.shape[0]
    h = _ada_norm(x_ref[...], gain_ref[...], sh_ref[...], sc_ref[...])
    h_ref[...] = h
    lane = lax.broadcasted_iota(jnp.int32, (tm, LANES), 1)
    logits = jnp.dot(h, rw_ref[...], preferred_element_type=F32, precision=HIGHEST) + rb_ref[...]
    logits = jnp.where(lane < N_EXPERTS, logits, NEG_INF)
    tops, sels = [], []
    work = logits
    for _ in range(TOP_K):
        m = jnp.max(work, axis=-1, keepdims=True)
        sel = jnp.min(jnp.where(work == m, lane, LANES), axis=-1, keepdims=True)
        work = jnp.where(lane == sel, NEG_INF, work)
        tops.append(m)
        sels.append(sel)
    exps = [jnp.exp(m - tops[0]) for m in tops]
    denom = exps[0] + exps[1] + exps[2] + exps[3]
    onehot = jnp.zeros((tm, LANES), F32)
    for sel in sels:
        onehot = onehot + (lane == sel).astype(F32)
    ti = lax.broadcasted_iota(jnp.int32, (tm, tm), 0)
    tj = lax.broadcasted_iota(jnp.int32, (tm, tm), 1)
    tri = (ti > tj).astype(BF16)
    before = jnp.dot(tri, onehot.astype(BF16), preferred_element_type=F32) + carry[...]
    idx_out = jnp.zeros((tm, LANES), jnp.int32)
    gate_out = jnp.zeros((tm, LANES), F32)
    rank_out = jnp.zeros((tm, LANES), jnp.int32)
    for k in range(TOP_K):
        rank_k = jnp.sum(jnp.where(lane == sels[k], before, 0.0), axis=-1, keepdims=True)
        idx_out = jnp.where(lane == k, sels[k], idx_out)
        gate_out = jnp.where(lane == k, exps[k] / denom, gate_out)
        rank_out = jnp.where(lane == k, rank_k.astype(jnp.int32), rank_out)
    idx_ref[...] = idx_out
    gate_ref[...] = gate_out
    rank_ref[...] = rank_out
    carry[...] = carry[...] + jnp.sum(onehot, axis=0, keepdims=True)
    cnt_ref[...] = carry[...]


def _router(x2, gain, mod4, sh_slot, sc_slot, rw_pad, rb_pad, seq):
    t, d = x2.shape
    tm = ROUTER_TM
    tps = seq // tm
    tok = lambda dt: jax.ShapeDtypeStruct((t, LANES), dt)
    tok_spec = pl.BlockSpec((tm, LANES), lambda i: (i, 0))
    return pl.pallas_call(
        _router_body,
        out_shape=(jax.ShapeDtypeStruct((t, d), F32), tok(jnp.int32), tok(F32), tok(jnp.int32),
                   jax.ShapeDtypeStruct((1, LANES), F32)),
        grid=(t // tm,),
        in_specs=[pl.BlockSpec((tm, d), lambda i: (i, 0)),
                  pl.BlockSpec((1, d), lambda i: (0, 0)),
                  pl.BlockSpec((None, None, 1, d), lambda i: (i // tps, sh_slot, 0, 0)),
                  pl.BlockSpec((None, None, 1, d), lambda i: (i // tps, sc_slot, 0, 0)),
                  pl.BlockSpec((d, LANES), lambda i: (0, 0)),
                  pl.BlockSpec((1, LANES), lambda i: (0, 0))],
        out_specs=(pl.BlockSpec((tm, d), lambda i: (i, 0)), tok_spec, tok_spec, tok_spec,
                   pl.BlockSpec((1, LANES), lambda i: (0, 0))),
        scratch_shapes=[pltpu.VMEM((1, LANES), F32)],
        compiler_params=_cparams(("arbitrary",)),
        name="moe_router",
    )(x2, gain, mod4, mod4, rw_pad, rb_pad)


DISPATCH_TM = 256


def _row_copy(src_ref, src_row, dst_ref, dst_row, sem):
    return pltpu.make_async_copy(src_ref.at[pl.ds(src_row, 1)], dst_ref.at[pl.ds(dst_row, 1)], sem)


def _dispatch_body(zb_ref, dest_ref, h_ref, xb_out, zbuf, sem, zsem):
    tm = h_ref.shape[0]
    bm = zbuf.shape[0]

    @pl.when(pl.program_id(0) == 0)
    def _():
        zbuf[...] = jnp.zeros_like(zbuf)

        def zero_copy(j):
            start = pl.multiple_of(jnp.maximum(zb_ref[j], 0) * bm, bm)
            return pltpu.make_async_copy(zbuf, xb_out.at[pl.ds(start, bm)], zsem)

        def zstart(j, carry):
            @pl.when(zb_ref[j] >= 0)
            def _():
                zero_copy(j).start()
            return carry

        def zwait(j, carry):
            @pl.when(zb_ref[j] >= 0)
            def _():
                zero_copy(j).wait()
            return carry

        lax.fori_loop(0, zb_ref.shape[0], zstart, 0)
        lax.fori_loop(0, zb_ref.shape[0], zwait, 0)

    def issue(r, carry):
        for k in range(TOP_K):
            _row_copy(h_ref, r, xb_out, dest_ref[0, r * TOP_K + k], sem).start()
        return carry

    lax.fori_loop(0, tm, issue, 0)

    def drain(r, carry):
        for k in range(TOP_K):
            _row_copy(h_ref, 0, xb_out, 0, sem).wait()
        return carry

    lax.fori_loop(0, tm, drain, 0)


def _dispatch(h2, dest3, zero_blocks, n_rows):
    t, d = h2.shape
    tm = DISPATCH_TM
    grid_spec = pltpu.PrefetchScalarGridSpec(
        num_scalar_prefetch=1,
        grid=(t // tm,),
        in_specs=[pl.BlockSpec((None, 1, tm * TOP_K), lambda i, zb: (i, 0, 0), memory_space=pltpu.SMEM),
                  pl.BlockSpec((tm, d), lambda i, zb: (i, 0))],
        out_specs=pl.BlockSpec(memory_space=pl.ANY),
        scratch_shapes=[pltpu.VMEM((MOE_ROWS, d), h2.dtype), pltpu.SemaphoreType.DMA(()),
                        pltpu.SemaphoreType.DMA(())],
    )
    return pl.pallas_call(
        _dispatch_body,
        out_shape=jax.ShapeDtypeStruct((n_rows, d), h2.dtype),
        grid_spec=grid_spec,
        compiler_params=_cparams(("arbitrary",)),
        name="moe_dispatch",
    )(zero_blocks, dest3, h2)


def _ffn_body(be_ref, nu_ref, x_ref, uw_ref, ub_ref, dw_ref, db_ref, y_ref, uw_bf, dw_bf):
    i = pl.program_id(0)
    changed = jnp.logical_or(i == 0, be_ref[i] != be_ref[jnp.maximum(i - 1, 0)])
    active = i < nu_ref[0]

    @pl.when(jnp.logical_and(changed, active))
    def _():
        rows = 128
        for r0 in range(0, uw_ref.shape[0], rows):
            uw_bf[r0:r0 + rows, :] = uw_ref[r0:r0 + rows, :].astype(BF16)
        for r0 in range(0, dw_ref.shape[0], rows):
            dw_bf[r0:r0 + rows, :] = dw_ref[r0:r0 + rows, :].astype(BF16)

    @pl.when(active)
    def _():
        x = x_ref[...].astype(BF16)
        gu = jnp.dot(x, uw_bf[...], preferred_element_type=F32) + ub_ref[...]
        gate = jnp.minimum(gu[:, :D_FF], SWIGLU_LIMIT)
        lin = jnp.clip(gu[:, D_FF:], -SWIGLU_LIMIT, SWIGLU_LIMIT)
        act = gate * jax.nn.sigmoid(SWIGLU_ALPHA * gate) * (lin + 1.0)
        y_ref[...] = jnp.dot(act.astype(BF16), dw_bf[...], preferred_element_type=F32) + db_ref[...]

    @pl.when(jnp.logical_not(active))
    def _():
        y_ref[...] = jnp.zeros_like(y_ref)


def _expert_ffn(xb, blk_expert, n_used, up_w, up_b4, down_w, down_b4, layer):
    n_rows, d = xb.shape
    bm = MOE_ROWS
    n_blk = n_rows // bm
    f2 = up_w.shape[-1]
    grid_spec = pltpu.PrefetchScalarGridSpec(
        num_scalar_prefetch=2,
        grid=(n_blk,),
        in_specs=[pl.BlockSpec((bm, d), lambda i, be, nu: (i, 0)),
                  pl.BlockSpec((None, None, d, f2), lambda i, be, nu: (layer, be[i], 0, 0)),
                  pl.BlockSpec((None, None, 1, f2), lambda i, be, nu: (layer, be[i], 0, 0)),
                  pl.BlockSpec((None, None, f2 // 2, d), lambda i, be, nu: (layer, be[i], 0, 0)),
                  pl.BlockSpec((None, None, 1, d), lambda i, be, nu: (layer, be[i], 0, 0))],
        out_specs=pl.BlockSpec((bm, d), lambda i, be, nu: (i, 0)),
        scratch_shapes=[pltpu.VMEM((d, f2), BF16), pltpu.VMEM((f2 // 2, d), BF16)],
    )
    return pl.pallas_call(
        _ffn_body,
        out_shape=jax.ShapeDtypeStruct((n_rows, d), F32),
        grid_spec=grid_spec,
        compiler_params=_cparams(("arbitrary",)),
        name="moe_expert_ffn",
    )(blk_expert, n_used, xb, up_w, up_b4, down_w, down_b4)


SC_GATHER_ROWS = 32


def _sc_gather_rows(table, idx):
    n, d = idx.shape[0], table.shape[1]
    info = plsc.get_sparse_core_info()
    n_cores, n_sub = info.num_cores, info.num_subcores
    n_workers = n_cores * n_sub
    chunk = SC_GATHER_ROWS
    per_worker = n // n_workers
    n_chunks = per_worker // chunk
    assert n_chunks * chunk * n_workers == n and n_chunks % 2 == 0
    mesh = plsc.VectorSubcoreMesh(core_axis_name="c", subcore_axis_name="s")

    @functools.partial(
        pl.kernel, mesh=mesh,
        out_type=jax.ShapeDtypeStruct((n, d), table.dtype),
        scratch_types=[pltpu.VMEM((n_chunks, chunk), jnp.int32), pltpu.VMEM((2, chunk, d), table.dtype),
                       pltpu.SemaphoreType.DMA((2,))],
    )
    def gather_kernel(table_hbm, idx_hbm, out_hbm, idx_v, rows_v, sems):
        wid = lax.axis_index("s") * n_cores + lax.axis_index("c")
        base = wid * per_worker
        pltpu.sync_copy(idx_hbm.at[wid], idx_v)

        def gather(ci, slot):
            return pltpu.make_async_copy(table_hbm.at[idx_v.at[ci]], rows_v.at[slot], sems.at[slot])

        gather(0, 0).start()

        @pl.loop(0, n_chunks, step=2)
        def _(c0):
            for slot in range(2):
                ci = c0 + slot
                gather(ci, slot).wait()

                @pl.when(ci + 1 < n_chunks)
                def _():
                    gather(ci + 1, 1 - slot).start()

                pltpu.sync_copy(rows_v.at[slot], out_hbm.at[pl.ds(base + ci * chunk, chunk)])

    return gather_kernel(table, idx.reshape(n_workers, n_chunks, chunk))


COMBINE_TM = 256


def _combine_body(gate_ref, x_ref, g_ref, y_ref, o_ref):
    gates = gate_ref[...]
    y = gates[:, 0:1] * y_ref[0]
    for k in range(1, TOP_K):
        y = y + gates[:, k:k + 1] * y_ref[k]
    o_ref[...] = x_ref[...] + g_ref[...] * y


def _combine(y_kt, gates, x2, mod4, gate_slot, seq):
    t, d = x2.shape
    tm = COMBINE_TM
    tps = seq // tm
    return pl.pallas_call(
        _combine_body,
        out_shape=jax.ShapeDtypeStruct((t, d), F32),
        grid=(t // tm,),
        in_specs=[pl.BlockSpec((tm, LANES), lambda i: (i, 0)),
                  pl.BlockSpec((tm, d), lambda i: (i, 0)),
                  pl.BlockSpec((None, None, 1, d), lambda i: (i // tps, gate_slot, 0, 0)),
                  pl.BlockSpec((TOP_K, tm, d), lambda i: (0, i, 0))],
        out_specs=pl.BlockSpec((tm, d), lambda i: (i, 0)),
        compiler_params=_cparams(("parallel",)),
        name="moe_combine",
    )(gates, x2, mod4, y_kt)


def _moe_layer(x2, gain, mod4, router_w, router_b, up_w, up_b, down_w, down_b, layer, seq):
    t, d = x2.shape
    rw_pad = jnp.pad(router_w[layer], ((0, 0), (0, LANES - N_EXPERTS)))
    rb_pad = jnp.pad(router_b[layer], (0, LANES - N_EXPERTS)).reshape(1, LANES)
    h2, idx, gates, rank, counts = _router(x2, gain, mod4, 3, 4, rw_pad, rb_pad, seq)

    bm = MOE_ROWS
    counts = counts[0, :N_EXPERTS].astype(jnp.int32)
    padded = (counts + bm - 1) // bm * bm
    p_ends = jnp.cumsum(padded)
    p_starts = p_ends - padded
    n_rows = t * TOP_K + N_EXPERTS * bm
    n_blk = n_rows // bm
    top_idx = idx[:, :TOP_K]
    onehot = top_idx[:, :, None] == jnp.arange(N_EXPERTS, dtype=jnp.int32)[None, None, :]
    dest = jnp.sum(jnp.where(onehot, p_starts[None, None, :], 0), axis=-1) + rank[:, :TOP_K]
    blk_start = jnp.arange(n_blk, dtype=jnp.int32) * bm
    blk_expert = jnp.sum((blk_start[:, None] >= p_ends[None, :]).astype(jnp.int32), axis=-1)
    blk_expert = jnp.minimum(blk_expert, N_EXPERTS - 1).astype(jnp.int32)
    n_used = (p_ends[-1:] // bm).astype(jnp.int32)
    last_blk = jnp.where(padded > 0, p_ends // bm - 1, -1)
    tail_blk = n_used + jnp.arange(N_EXPERTS, dtype=jnp.int32)
    zero_blocks = jnp.concatenate([last_blk, jnp.where(tail_blk < n_blk, tail_blk, -1)]).astype(jnp.int32)

    xb = _dispatch(h2, dest.reshape(t // DISPATCH_TM, 1, DISPATCH_TM * TOP_K), zero_blocks, n_rows)
    f2 = up_w.shape[-1]
    y_rows = _expert_ffn(xb, blk_expert, n_used, up_w, up_b.reshape(up_b.shape[0], N_EXPERTS, 1, f2),
                         down_w, down_b.reshape(down_b.shape[0], N_EXPERTS, 1, d), layer)
    y_kt = _sc_gather_rows(y_rows, dest.T.reshape(TOP_K * t)).reshape(TOP_K, t, d)
    return _combine(y_kt, gates, x2, mod4, 5, seq)


QKV_TM = 256


def _rope_tables(pos_col, invf_row):
    tm = pos_col.shape[0]
    ang = pos_col * invf_row
    c = jnp.cos(ang)
    s = jnp.sin(ang)
    d = lax.broadcasted_iota(jnp.int32, (tm, LANES), 1) % B_HEAD_DIM
    half = ROPE_DIM // 2
    cos_t = jnp.where(d < ROPE_DIM, c, 1.0)
    sin_lo = jnp.where(d < half, -s, 0.0)
    sin_hi = jnp.where(jnp.logical_and(d >= half, d < ROPE_DIM), s, 0.0)
    return cos_t, sin_lo, sin_hi


def _head_norm_rope(x, gain_row, tables):
    cos_t, sin_lo, sin_hi = tables
    lane = lax.broadcasted_iota(jnp.int32, x.shape, 1)
    sq = x * x
    first = lane < B_HEAD_DIM
    ss0 = jnp.sum(jnp.where(first, sq, 0.0), axis=-1, keepdims=True)
    ss1 = jnp.sum(jnp.where(first, 0.0, sq), axis=-1, keepdims=True)
    ms = jnp.where(first, ss0, ss1) * (1.0 / B_HEAD_DIM)
    xn = x * lax.rsqrt(ms + EPS) * gain_row
    half = ROPE_DIM // 2
    return (xn * cos_t + pltpu.roll(xn, LANES - half, 1) * sin_lo + pltpu.roll(xn, half, 1) * sin_hi)


def _qkv_body(x_ref, pos_ref, gq_ref, shq_ref, scq_ref, gkv_ref, shkv_ref, sckv_ref, wq_ref, wkv_ref,
              qg_ref, kg_ref, invf_ref, q_ref, k_ref, v_ref):
    x = x_ref[...]
    y = x * lax.rsqrt(jnp.mean(x * x, axis=-1, keepdims=True) + EPS)
    hq = ((y * gq_ref[...]) * (1.0 + scq_ref[...]) + shq_ref[...]).astype(BF16)
    hkv = ((y * gkv_ref[...]) * (1.0 + sckv_ref[...]) + shkv_ref[...]).astype(BF16)
    tables = _rope_tables(pos_ref[...].astype(F32), invf_ref[...])
    kv = jnp.dot(hkv, wkv_ref[...], preferred_element_type=F32)
    k_ref[...] = _head_norm_rope(kv[:, :LANES], kg_ref[...], tables).astype(BF16)
    v_ref[...] = kv[:, LANES:].astype(BF16)
    q = jnp.dot(hq, wq_ref[...], preferred_element_type=F32)
    scale = B_HEAD_DIM ** -0.5
    for p in range(q.shape[-1] // LANES):
        qp = _head_norm_rope(q[:, p * LANES:(p + 1) * LANES], qg_ref[...], tables)
        q_ref[:, p * LANES:(p + 1) * LANES] = (qp * scale).astype(BF16)


def _qkv(x2, pos2, gq, mod4, gkv, kvmod4, wq_bf, wkv_bf, qg2, kg2, invf, seq):
    t, d = x2.shape
    tm = QKV_TM
    tps = seq // tm
    nq = wq_bf.shape[-1]
    nkv = wkv_bf.shape[-1]
    row = lambda n: pl.BlockSpec((1, n), lambda i: (0, 0))
    modspec = lambda slot: pl.BlockSpec((None, None, 1, d), lambda i, slot=slot: (i // tps, slot, 0, 0))
    return pl.pallas_call(
        _qkv_body,
        out_shape=(jax.ShapeDtypeStruct((t, nq), BF16), jax.ShapeDtypeStruct((t, LANES), BF16),
                   jax.ShapeDtypeStruct((t, LANES), BF16)),
        grid=(t // tm,),
        in_specs=[pl.BlockSpec((tm, d), lambda i: (i, 0)),
                  pl.BlockSpec((tm, 1), lambda i: (i, 0)),
                  row(d), modspec(0), modspec(1),
                  row(d), modspec(0), modspec(1),
                  pl.BlockSpec((d, nq), lambda i: (0, 0)),
                  pl.BlockSpec((d, nkv), lambda i: (0, 0)),
                  row(LANES), row(LANES), row(LANES)],
        out_specs=(pl.BlockSpec((tm, nq), lambda i: (i, 0)),
                   pl.BlockSpec((tm, LANES), lambda i: (i, 0)),
                   pl.BlockSpec((tm, LANES), lambda i: (i, 0))),
        compiler_params=_cparams(("parallel",)),
        name="swa_qkv_proj",
    )(x2, pos2, gq, mod4, mod4, gkv, kvmod4, kvmod4, wq_bf, wkv_bf, qg2, kg2, invf)


def _attn_body(sink_ref, q_ref, kc_ref, kp_ref, vc_ref, vp_ref, o_ref):
    n = pl.program_id(1)
    w = WINDOW
    qi = lax.broadcasted_iota(jnp.int32, (w, 2 * w), 0) + w
    kj = lax.broadcasted_iota(jnp.int32, (w, 2 * w), 1)
    band = jnp.logical_and(kj <= qi, qi - kj < w)
    mask = jnp.logical_and(band, jnp.logical_or(n > 0, kj >= w))
    hd = B_HEAD_DIM
    for g in range(B_KV_HEADS):
        kb = jnp.concatenate([kp_ref[:, g * hd:(g + 1) * hd], kc_ref[:, g * hd:(g + 1) * hd]], axis=0)
        vb = jnp.concatenate([vp_ref[:, g * hd:(g + 1) * hd], vc_ref[:, g * hd:(g + 1) * hd]], axis=0)
        for hh in range(B_GROUP):
            head = g * B_GROUP + hh
            qh = q_ref[:, head * hd:(head + 1) * hd]
            s = lax.dot_general(qh, kb, (((1,), (1,)), ((), ())), preferred_element_type=F32)
            s = jnp.where(mask, s, NEG_INF)
            sink = sink_ref[0, head]
            m = jnp.maximum(jnp.max(s, axis=-1, keepdims=True), sink)
            p = jnp.exp(s - m)
            denom = jnp.sum(p, axis=-1, keepdims=True) + jnp.exp(sink - m)
            o = jnp.dot(p.astype(BF16), vb, preferred_element_type=F32) / denom
            o_ref[:, head * hd:(head + 1) * hd] = o.astype(BF16)


def _attention(q, k, v, sinks2, bsz, seq):
    t, nq = q.shape
    w = WINDOW
    nb = seq // w
    cur = lambda b, n: (b * nb + n, 0)
    prev = lambda b, n: (b * nb + jnp.maximum(n - 1, 0), 0)
    return pl.pallas_call(
        _attn_body,
        out_shape=jax.ShapeDtypeStruct((t, nq), BF16),
        grid=(bsz, nb),
        in_specs=[pl.BlockSpec(memory_space=pltpu.SMEM),
                  pl.BlockSpec((w, nq), cur),
                  pl.BlockSpec((w, LANES), cur), pl.BlockSpec((w, LANES), prev),
                  pl.BlockSpec((w, LANES), cur), pl.BlockSpec((w, LANES), prev)],
        out_specs=pl.BlockSpec((w, nq), cur),
        compiler_params=_cparams(("parallel", "parallel")),
        name="swa_sink_attention",
    )(sinks2, q, k, k, v, v)


def kernel(x, c, positions, ada_w, ada_b, norm_gain, a_w_in, a_conv, a_log, a_dt_bias, a_out_gain, a_w_out,
           kv_ada_w, kv_ada_b, kv_norm_gain, kv_w, k_norm_gain, b_w_q, q_norm_gain, b_sinks, b_w_out,
           router_w, router_b, up_w, up_b, down_w, down_b):
    bsz, seq, d = x.shape
    t = bsz * seq
    depth = ada_w.shape[0]
    x2 = x.reshape(t, d)
    ada_b3 = ada_b.reshape(depth, 1, 6 * d)

    for layer in range(depth):
        mod4 = _modulation(c, ada_w, ada_b3, layer).reshape(bsz, 6, 1, d)
        gain1 = norm_gain[layer, 0].reshape(1, d)
        gain2 = norm_gain[layer, 1].reshape(1, d)
        if layer < N_A_LAYERS:
            w_in = jnp.pad(a_w_in[layer].astype(BF16), ((0, 0), (0, A_PROJ_PAD - a_w_in.shape[-1])))
            proj = _in_proj(x2, gain1, mod4, w_in, seq)
            o = _gdn(proj, a_conv[layer], a_log[layer].reshape(1, A_HEADS),
                     a_dt_bias[layer].reshape(1, A_HEADS), a_out_gain[layer].reshape(1, A_HEAD_DIM),
                     bsz, seq)
            x2 = _out_proj_residual(o, a_w_out[layer].astype(BF16), x2, mod4, 2, seq)
        else:
            j = layer - N_A_LAYERS
            kvmod4 = _modulation(c, kv_ada_w.reshape(1, d, 2 * d), kv_ada_b.reshape(1, 1, 2 * d), 0)
            kvmod4 = kvmod4.reshape(bsz, 2, 1, d)
            half = ROPE_DIM // 2
            inv_freq = ROPE_THETA ** (-np.arange(0, ROPE_DIM, 2, dtype=np.float32) / ROPE_DIM)
            invf = jnp.asarray(np.tile(inv_freq.astype(np.float32), LANES // half).reshape(1, LANES))
            q, k, v = _qkv(x2, positions.reshape(t, 1), gain1, mod4, kv_norm_gain.reshape(1, d), kvmod4,
                           b_w_q[j].astype(BF16), kv_w.astype(BF16),
                           jnp.tile(q_norm_gain[j], 2).reshape(1, LANES),
                           jnp.tile(k_norm_gain, 2).reshape(1, LANES), invf, seq)
            o = _attention(q, k, v, b_sinks[j].reshape(1, B_Q_HEADS), bsz, seq)
            x2 = _out_proj_residual(o, b_w_out[j].astype(BF16), x2, mod4, 2, seq)
        x2 = _moe_layer(x2, gain2, mod4, router_w, router_b, up_w, up_b, down_w, down_b, layer, seq)
    return x2.reshape(bsz, seq, d)
```

```python
import dataclasses
import functools

import jax
import jax.numpy as jnp
import numpy as np
from jax import lax
from jax.experimental import pallas as pl
from jax.experimental.pallas import tpu as pltpu
from jax.experimental.pallas import tpu_sc as plsc

F32 = jnp.float32
BF16 = jnp.bfloat16
HIGHEST = lax.Precision.HIGHEST

D_MODEL = 1024
N_A_LAYERS = 1

A_HEAD_DIM = 128
A_HEADS = 8
A_WIDTH = 1024
A_CONV = 4
A_CHUNK = 64
A_PROJ_PAD = 4608

B_HEAD_DIM = 64
B_Q_HEADS = 16
B_KV_HEADS = 2
B_GROUP = 8
WINDOW = 128
ROPE_DIM = 16
ROPE_THETA = 500000.0

N_EXPERTS = 32
TOP_K = 4
D_FF = 1024
SWIGLU_LIMIT = 7.0
SWIGLU_ALPHA = 1.702
MOE_ROWS = 256

EPS = 1e-6
LANES = 128
NEG_INF = float("-inf")

VMEM_LIMIT = 56 * 1024 * 1024


def _cparams(sem, vmem=VMEM_LIMIT):
    return pltpu.CompilerParams(dimension_semantics=sem, vmem_limit_bytes=vmem)


def _silu(x):
    return x * jax.nn.sigmoid(x)


def _ada_norm(x, gain, shift, scale):
    y = x * lax.rsqrt(jnp.mean(x * x, axis=-1, keepdims=True) + EPS)
    return (y * gain) * (1.0 + scale) + shift


def _mod_body(c_ref, w_ref, b_ref, o_ref):
    o_ref[...] = jnp.dot(_silu(c_ref[...]), w_ref[...], preferred_element_type=F32,
                         precision=HIGHEST) + b_ref[...]


def _modulation(c, w3, b3, layer):
    bsz, d = c.shape
    n = w3.shape[-1]
    tn = 1024
    return pl.pallas_call(
        _mod_body,
        out_shape=jax.ShapeDtypeStruct((bsz, n), F32),
        grid=(n // tn,),
        in_specs=[pl.BlockSpec((bsz, d), lambda j: (0, 0)),
                  pl.BlockSpec((None, d, tn), lambda j: (layer, 0, j)),
                  pl.BlockSpec((None, 1, tn), lambda j: (layer, 0, j))],
        out_specs=pl.BlockSpec((bsz, tn), lambda j: (0, j)),
        compiler_params=_cparams(("arbitrary",)),
        name="adaln_mod",
    )(c, w3, b3)


def _inproj_body(x_ref, gain_ref, sh_ref, sc_ref, w_ref, o_ref):
    h = _ada_norm(x_ref[...], gain_ref[...], sh_ref[...], sc_ref[...]).astype(BF16)
    n = o_ref.shape[-1]
    for n0 in range(0, n, 512):
        o_ref[:, n0:n0 + 512] = jnp.dot(h, w_ref[:, n0:n0 + 512], preferred_element_type=F32)


def _in_proj(x2, gain, mod4, w_bf, seq):
    t, d = x2.shape
    n = w_bf.shape[-1]
    tm = 256
    tps = seq // tm
    return pl.pallas_call(
        _inproj_body,
        out_shape=jax.ShapeDtypeStruct((t, n), F32),
        grid=(t // tm,),
        in_specs=[pl.BlockSpec((tm, d), lambda i: (i, 0)),
                  pl.BlockSpec((1, d), lambda i: (0, 0)),
                  pl.BlockSpec((None, None, 1, d), lambda i: (i // tps, 0, 0, 0)),
                  pl.BlockSpec((None, None, 1, d), lambda i: (i // tps, 1, 0, 0)),
                  pl.BlockSpec((d, n), lambda i: (0, 0))],
        out_specs=pl.BlockSpec((tm, n), lambda i: (i, 0)),
        compiler_params=_cparams(("parallel",)),
        name="gdn_in_proj",
    )(x2, gain, mod4, mod4, w_bf)


GDN_HG = 2
GDN_RB = 256
GDN_SC = 128
GDN_SUB = 4


def _split_bf16(x):
    hi = x.astype(BF16)
    return hi, (x - hi.astype(F32)).astype(BF16)


def _mm3(a_hi, a_lo, b_hi, b_lo, dims=(((1,), (0,)), ((), ()))):
    dg = functools.partial(lax.dot_general, dimension_numbers=dims, preferred_element_type=F32)
    return dg(a_hi, b_hi) + (dg(a_lo, b_hi) + dg(a_hi, b_lo))


def _dot3(a, b, dims):
    return _mm3(*_split_bf16(a), *_split_bf16(b), dims)


def _gdn_body(alog_ref, dtb_ref, q_ref, k_ref, v_ref, z_ref, ab_ref, cq_ref, ck_ref, cv_ref, og_ref,
              o_ref, qs, ks, vs, bs, gcs, mp, bc, qp, op):
    seq = q_ref.shape[0]
    hd = A_HEAD_DIM
    ch = A_CHUNK
    n_chunks = seq // ch
    g_idx = pl.program_id(1)

    lane = lax.broadcasted_iota(jnp.int32, (GDN_RB, hd), 1)
    row = lax.broadcasted_iota(jnp.int32, (GDN_RB, hd), 0)
    row_in_chunk = row % ch

    def conv_silu(x_ref, c_ref, col0, r, start):
        main = x_ref[pl.ds(start, GDN_RB), col0:col0 + hd]
        pstart = pl.multiple_of(jnp.maximum(start - 8, 0), 8)
        prev = x_ref[pl.ds(pstart, 8), col0:col0 + hd] * jnp.where(r > 0, 1.0, 0.0)
        ext = jnp.concatenate([prev, main], axis=0)
        acc = main * c_ref[A_CONV - 1:A_CONV, col0:col0 + hd]
        for s in range(1, A_CONV):
            shifted = pltpu.roll(ext, s, 0)[8:8 + GDN_RB]
            acc = acc + shifted * c_ref[A_CONV - 1 - s:A_CONV - s, col0:col0 + hd]
        return _silu(acc)

    def l2n(x):
        return x * lax.rsqrt(jnp.sum(x * x, axis=-1, keepdims=True) + EPS)

    def prep(r, carry):
        start = pl.multiple_of(r * GDN_RB, GDN_RB)
        ab = ab_ref[pl.ds(start, GDN_RB), :]
        for hh in range(GDN_HG):
            col0 = hh * hd
            head = g_idx * GDN_HG + hh
            qs[hh, pl.ds(start, GDN_RB), :] = l2n(conv_silu(q_ref, cq_ref, col0, r, start)) * (hd ** -0.5)
            ks[hh, pl.ds(start, GDN_RB), :] = l2n(conv_silu(k_ref, ck_ref, col0, r, start))
            vs[hh, pl.ds(start, GDN_RB), :] = conv_silu(v_ref, cv_ref, col0, r, start)
            a_col = jnp.sum(jnp.where(lane == head, ab, 0.0), axis=-1, keepdims=True)
            b_col = jnp.sum(jnp.where(lane == head + A_HEADS, ab, 0.0), axis=-1, keepdims=True)
            bs[hh, pl.ds(start, GDN_RB), :] = jnp.broadcast_to(jax.nn.sigmoid(b_col), (GDN_RB, hd))
            xa = a_col + dtb_ref[0, head]
            softplus = jnp.maximum(xa, 0.0) + jnp.log1p(jnp.exp(-jnp.abs(xa)))
            g = jnp.broadcast_to(-jnp.exp(alog_ref[0, head]) * softplus, (GDN_RB, hd))
            s = 1
            while s < ch:
                g = g + jnp.where(row_in_chunk >= s, pltpu.roll(g, s, 0), 0.0)
                s *= 2
            gcs[hh, pl.ds(start, GDN_RB), :] = g
        return carry

    lax.fori_loop(0, seq // GDN_RB, prep, 0)

    sc_rows = GDN_SC
    per_sc = sc_rows // ch
    ri = lax.broadcasted_iota(jnp.int32, (sc_rows, sc_rows), 0)
    cj = lax.broadcasted_iota(jnp.int32, (sc_rows, sc_rows), 1)
    same_chunk = (ri // ch) == (cj // ch)
    incl = jnp.logical_and(same_chunk, ri >= cj)
    strict = jnp.logical_and(same_chunk, ri > cj)
    eye = jnp.where(ri == cj, 1.0, 0.0)
    chunk_of_col = lax.broadcasted_iota(jnp.int32, (hd, sc_rows), 1) // ch
    lanes_nt = (((1,), (1,)), ((), ()))
    plain = (((1,), (0,)), ((), ()))

    def local(i, carry):
        chains = [(i * GDN_SUB + sub, hh) for sub in range(GDN_SUB) for hh in range(GDN_HG)]
        rows_of = lambda blk: pl.ds(pl.multiple_of(blk * sc_rows, sc_rows), sc_rows)

        def start(blk, hh):
            rows = rows_of(blk)
            k = ks[hh, rows, :]
            gc = gcs[hh, rows, :]
            decay = jnp.where(incl, jnp.exp(jnp.minimum(gc - gc.T[0:1, :], 0.0)), 0.0)
            k_bf = k.astype(BF16)
            kk = lax.dot_general((k * bs[hh, rows, :]).astype(BF16), k_bf, lanes_nt, preferred_element_type=F32)
            qk = lax.dot_general(qs[hh, rows, :].astype(BF16), k_bf, lanes_nt, preferred_element_type=F32)
            return jnp.where(strict, -(kk * decay), 0.0), (qk * decay).astype(BF16)

        started = [start(*c) for c in chains]
        qks = [s[1] for s in started]
        invs = [eye + s[0] for s in started]
        pws = [_dot3(s[0], s[0], plain) for s in started]
        for step in range(1, 6):
            for n in range(len(chains)):
                pw_hi, pw_lo = _split_bf16(pws[n])
                inv_hi, inv_lo = _split_bf16(invs[n])
                if step < 5:
                    prod = _mm3(jnp.concatenate([pw_hi, inv_hi], axis=0),
                                jnp.concatenate([pw_lo, inv_lo], axis=0), pw_hi, pw_lo)
                    pws[n] = prod[:sc_rows]
                    invs[n] = invs[n] + prod[sc_rows:]
                else:
                    invs[n] = invs[n] + _mm3(inv_hi, inv_lo, pw_hi, pw_lo)

        def solve(n, blk, hh):
            rows = rows_of(blk)
            beta = bs[hh, rows, :]
            rhs = jnp.concatenate([vs[hh, rows, :] * beta, ks[hh, rows, :] * beta * jnp.exp(gcs[hh, rows, :])], axis=1)
            return _dot3(invs[n], rhs, plain).astype(BF16)

        uws = [solve(n, *c) for n, c in enumerate(chains)]

        def finish(n, blk, hh):
            rows = rows_of(blk)
            k = ks[hh, rows, :]
            gc = gcs[hh, rows, :]
            res = jnp.dot(qks[n], uws[n], preferred_element_type=F32)
            op[hh, rows, :] = res[:, :hd]
            qp[hh, rows, :] = (qs[hh, rows, :] * jnp.exp(gc) - res[:, hd:]).astype(BF16)
            gl = jnp.concatenate(
                [jnp.broadcast_to(gc[(j + 1) * ch - 1:(j + 1) * ch, :], (ch, hd)) for j in range(per_sc)], axis=0)
            kt_t = (k * jnp.exp(gl - gc)).T
            for j in range(per_sc):
                kt_j = jnp.where(chunk_of_col == j, kt_t, 0.0).astype(BF16)
                bm = jnp.dot(kt_j, uws[n], preferred_element_type=F32)
                bc[hh, blk * per_sc + j] = bm[:, :hd]
                mp[hh, blk * per_sc + j] = bm[:, hd:].astype(BF16)

        for n, c in enumerate(chains):
            finish(n, *c)
        return carry

    lax.fori_loop(0, seq // (sc_rows * GDN_SUB), local, 0)

    def scan(c, states):
        r0 = pl.multiple_of(c * ch, ch)
        rows = pl.ds(r0, ch)
        new_states = []
        for hh in range(GDN_HG):
            col0 = hh * hd
            state = states[hh]
            s_bf = state.astype(BF16)
            o = jnp.dot(qp[hh, rows, :], s_bf, preferred_element_type=F32) + op[hh, rows, :]
            g_tot = jnp.exp(gcs[hh, pl.ds(r0 + ch - 1, 1), :])
            new_states.append(state * g_tot - jnp.dot(mp[hh, c], s_bf, preferred_element_type=F32) + bc[hh, c])
            on = o * lax.rsqrt(jnp.mean(o * o, axis=-1, keepdims=True) + EPS) * og_ref[...]
            z = z_ref[rows, col0:col0 + hd]
            o_ref[rows, col0:col0 + hd] = (on * _silu(z)).astype(BF16)
        return tuple(new_states)

    lax.fori_loop(0, n_chunks, scan, tuple(jnp.zeros((hd, hd), F32) for _ in range(GDN_HG)))


def _gdn(proj, conv_w, a_log, dt_bias, out_gain, bsz, seq):
    t = proj.shape[0]
    hd = A_HEAD_DIM
    wb = hd * GDN_HG
    ng = A_HEADS // GDN_HG
    per = A_WIDTH // wb
    n_chunks = seq // A_CHUNK
    smem = pl.BlockSpec(memory_space=pltpu.SMEM)
    seq_spec = lambda off: pl.BlockSpec((seq, wb), lambda b, g, off=off: (b, off + g))
    conv_spec = lambda off: pl.BlockSpec((A_CONV, wb), lambda b, g, off=off: (0, off + g))
    sc = lambda dt=F32: pltpu.VMEM((GDN_HG, seq, hd), dt)
    return pl.pallas_call(
        _gdn_body,
        out_shape=jax.ShapeDtypeStruct((t, A_WIDTH), BF16),
        grid=(bsz, ng),
        in_specs=[smem, smem,
                  seq_spec(0), seq_spec(per), seq_spec(2 * per), seq_spec(3 * per),
                  pl.BlockSpec((seq, LANES), lambda b, g: (b, 4 * A_WIDTH // LANES)),
                  conv_spec(0), conv_spec(per), conv_spec(2 * per),
                  pl.BlockSpec((1, hd), lambda b, g: (0, 0))],
        out_specs=pl.BlockSpec((seq, wb), lambda b, g: (b, g)),
        scratch_shapes=[sc(), sc(), sc(), sc(), sc(),
                        pltpu.VMEM((GDN_HG, n_chunks, hd, hd), BF16),
                        pltpu.VMEM((GDN_HG, n_chunks, hd, hd), F32),
                        sc(BF16), sc()],
        compiler_params=_cparams(("parallel", "parallel")),
        name="gdn_core",
    )(a_log, dt_bias, proj, proj, proj, proj, proj, conv_w, conv_w, conv_w, out_gain)


def _oproj_body(a_ref, w_ref, x_ref, g_ref, o_ref):
    y = jnp.dot(a_ref[...], w_ref[...], preferred_element_type=F32)
    o_ref[...] = x_ref[...] + g_ref[...] * y


def _out_proj_residual(a_bf, w_bf, x2, mod4, gate_slot, seq):
    t, d = x2.shape
    kdim = a_bf.shape[-1]
    tm = 512
    tps = seq // tm
    return pl.pallas_call(
        _oproj_body,
        out_shape=jax.ShapeDtypeStruct((t, d), F32),
        grid=(t // tm,),
        in_specs=[pl.BlockSpec((tm, kdim), lambda i: (i, 0)),
                  pl.BlockSpec((kdim, d), lambda i: (0, 0)),
                  pl.BlockSpec((tm, d), lambda i: (i, 0)),
                  pl.BlockSpec((None, None, 1, d), lambda i: (i // tps, gate_slot, 0, 0))],
        out_specs=pl.BlockSpec((tm, d), lambda i: (i, 0)),
        compiler_params=_cparams(("parallel",)),
        name="out_proj_residual",
    )(a_bf, w_bf, x2, mod4)


ROUTER_TM = 512


def _router_body(x_ref, gain_ref, sh_ref, sc_ref, rw_ref, rb_ref, h_ref, idx_ref, gate_ref, rank_ref,
                 cnt_ref, carry):
    i = pl.program_id(0)

    @pl.when(i == 0)
    def _():
        carry[...] = jnp.zeros_like(carry)

    tm = x_ref.shape[0]
    h = _ada_norm(x_ref[...], gain_ref[...], sh_ref[...], sc_ref[...])
    h_ref[...] = h
    lane = lax.broadcasted_iota(jnp.int32, (tm, LANES), 1)
    logits = jnp.dot(h, rw_ref[...], preferred_element_type=F32, precision=HIGHEST) + rb_ref[...]
    logits = jnp.where(lane < N_EXPERTS, logits, NEG_INF)
    tops, sels = [], []
    work = logits
    for _ in range(TOP_K):
        m = jnp.max(work, axis=-1, keepdims=True)
        sel = jnp.min(jnp.where(work == m, lane, LANES), axis=-1, keepdims=True)
        work = jnp.where(lane == sel, NEG_INF, work)
        tops.append(m)
        sels.append(sel)
    exps = [jnp.exp(m - tops[0]) for m in tops]
    denom = exps[0] + exps[1] + exps[2] + exps[3]
    onehot = jnp.zeros((tm, LANES), F32)
    for sel in sels:
        onehot = onehot + (lane == sel).astype(F32)
    ti = lax.broadcasted_iota(jnp.int32, (tm, tm), 0)
    tj = lax.broadcasted_iota(jnp.int32, (tm, tm), 1)
    tri = (ti > tj).astype(BF16)
    before = jnp.dot(tri, onehot.astype(BF16), preferred_element_type=F32) + carry[...]
    idx_out = jnp.zeros((tm, LANES), jnp.int32)
    gate_out = jnp.zeros((tm, LANES), F32)
    rank_out = jnp.zeros((tm, LANES), jnp.int32)
    for k in range(TOP_K):
        rank_k = jnp.sum(jnp.where(lane == sels[k], before, 0.0), axis=-1, keepdims=True)
        idx_out = jnp.where(lane == k, sels[k], idx_out)
        gate_out = jnp.where(lane == k, exps[k] / denom, gate_out)
        rank_out = jnp.where(lane == k, rank_k.astype(jnp.int32), rank_out)
    idx_ref[...] = idx_out
    gate_ref[...] = gate_out
    rank_ref[...] = rank_out
    carry[...] = carry[...] + jnp.sum(onehot, axis=0, keepdims=True)
    cnt_ref[...] = carry[...]


def _router(x2, gain, mod4, sh_slot, sc_slot, rw_pad, rb_pad, seq):
    t, d = x2.shape
    tm = ROUTER_TM
    tps = seq // tm
    tok = lambda dt: jax.ShapeDtypeStruct((t, LANES), dt)
    tok_spec = pl.BlockSpec((tm, LANES), lambda i: (i, 0))
    return pl.pallas_call(
        _router_body,
        out_shape=(jax.ShapeDtypeStruct((t, d), F32), tok(jnp.int32), tok(F32), tok(jnp.int32),
                   jax.ShapeDtypeStruct((1, LANES), F32)),
        grid=(t // tm,),
        in_specs=[pl.BlockSpec((tm, d), lambda i: (i, 0)),
                  pl.BlockSpec((1, d), lambda i: (0, 0)),
                  pl.BlockSpec((None, None, 1, d), lambda i: (i // tps, sh_slot, 0, 0)),
                  pl.BlockSpec((None, None, 1, d), lambda i: (i // tps, sc_slot, 0, 0)),
                  pl.BlockSpec((d, LANES), lambda i: (0, 0)),
                  pl.BlockSpec((1, LANES), lambda i: (0, 0))],
        out_specs=(pl.BlockSpec((tm, d), lambda i: (i, 0)), tok_spec, tok_spec, tok_spec,
                   pl.BlockSpec((1, LANES), lambda i: (0, 0))),
        scratch_shapes=[pltpu.VMEM((1, LANES), F32)],
        compiler_params=_cparams(("arbitrary",)),
        name="moe_router",
    )(x2, gain, mod4, mod4, rw_pad, rb_pad)


SC_LANES = 16
SC_INDEX_CHUNK = 8192


def _sc_row_tokens(dest_flat, n_rows, n_tok):
    n_assign = dest_flat.shape[0]
    assert n_rows % SC_LANES == 0 and n_assign % SC_INDEX_CHUNK == 0
    n_cores = plsc.get_sparse_core_info().num_cores
    mesh = plsc.VectorSubcoreMesh(core_axis_name="c", subcore_axis_name="s")

    @functools.partial(
        pl.kernel, mesh=mesh,
        out_type=jax.ShapeDtypeStruct((n_rows,), jnp.int32),
        scratch_types=[pltpu.VMEM((n_rows,), jnp.int32), pltpu.VMEM((SC_INDEX_CHUNK,), jnp.int32)],
        compiler_params=dataclasses.replace(pltpu.CompilerParams(), needs_layout_passes=False),
    )
    def row_token_kernel(dest_hbm, out_hbm, rt_v, d_v):
        wid = lax.axis_index("s") * n_cores + lax.axis_index("c")

        @pl.when(wid == 0)
        def _():
            lanes = lax.iota(jnp.int32, SC_LANES)

            @pl.loop(0, n_rows // SC_LANES)
            def _(i):
                rt_v[pl.ds(i * SC_LANES, SC_LANES)] = lax.rem(lanes + i * SC_LANES, n_tok)

            @pl.loop(0, n_assign // SC_INDEX_CHUNK)
            def _(c):
                pltpu.sync_copy(dest_hbm.at[pl.ds(c * SC_INDEX_CHUNK, SC_INDEX_CHUNK)], d_v)

                @pl.loop(0, SC_INDEX_CHUNK // SC_LANES)
                def _(i):
                    idx = d_v[pl.ds(i * SC_LANES, SC_LANES)]
                    tok = lax.div(lanes + (c * SC_INDEX_CHUNK + i * SC_LANES), TOP_K)
                    plsc.store_scatter(rt_v, [idx], tok)

            pltpu.sync_copy(rt_v, out_hbm)

    return row_token_kernel(dest_flat)


def _ffn_body(be_ref, nu_ref, x_ref, uw_ref, ub_ref, dw_ref, db_ref, y_ref, uw_bf, dw_bf):
    i = pl.program_id(0)
    changed = jnp.logical_or(i == 0, be_ref[i] != be_ref[jnp.maximum(i - 1, 0)])
    active = i < nu_ref[0]

    @pl.when(jnp.logical_and(changed, active))
    def _():
        rows = 128
        for r0 in range(0, uw_ref.shape[0], rows):
            uw_bf[r0:r0 + rows, :] = uw_ref[r0:r0 + rows, :].astype(BF16)
        for r0 in range(0, dw_ref.shape[0], rows):
            dw_bf[r0:r0 + rows, :] = dw_ref[r0:r0 + rows, :].astype(BF16)

    @pl.when(active)
    def _():
        x = x_ref[...].astype(BF16)
        gu = jnp.dot(x, uw_bf[...], preferred_element_type=F32) + ub_ref[...]
        gate = jnp.minimum(gu[:, :D_FF], SWIGLU_LIMIT)
        lin = jnp.clip(gu[:, D_FF:], -SWIGLU_LIMIT, SWIGLU_LIMIT)
        act = gate * jax.nn.sigmoid(SWIGLU_ALPHA * gate) * (lin + 1.0)
        y_ref[...] = jnp.dot(act.astype(BF16), dw_bf[...], preferred_element_type=F32) + db_ref[...]

    @pl.when(jnp.logical_not(active))
    def _():
        y_ref[...] = jnp.zeros_like(y_ref)


def _expert_ffn(xb, blk_expert, n_used, up_w, up_b4, down_w, down_b4, layer):
    n_rows, d = xb.shape
    bm = MOE_ROWS
    n_blk = n_rows // bm
    f2 = up_w.shape[-1]
    grid_spec = pltpu.PrefetchScalarGridSpec(
        num_scalar_prefetch=2,
        grid=(n_blk,),
        in_specs=[pl.BlockSpec((bm, d), lambda i, be, nu: (i, 0)),
                  pl.BlockSpec((None, None, d, f2), lambda i, be, nu: (layer, be[i], 0, 0)),
                  pl.BlockSpec((None, None, 1, f2), lambda i, be, nu: (layer, be[i], 0, 0)),
                  pl.BlockSpec((None, None, f2 // 2, d), lambda i, be, nu: (layer, be[i], 0, 0)),
                  pl.BlockSpec((None, None, 1, d), lambda i, be, nu: (layer, be[i], 0, 0))],
        out_specs=pl.BlockSpec((bm, d), lambda i, be, nu: (i, 0)),
        scratch_shapes=[pltpu.VMEM((d, f2), BF16), pltpu.VMEM((f2 // 2, d), BF16)],
    )
    return pl.pallas_call(
        _ffn_body,
        out_shape=jax.ShapeDtypeStruct((n_rows, d), F32),
        grid_spec=grid_spec,
        compiler_params=_cparams(("arbitrary",)),
        name="moe_expert_ffn",
    )(blk_expert, n_used, xb, up_w, up_b4, down_w, down_b4)


SC_GATHER_ROWS = 32


def _sc_gather_rows(table, idx):
    n, d = idx.shape[0], table.shape[1]
    info = plsc.get_sparse_core_info()
    n_cores, n_sub = info.num_cores, info.num_subcores
    n_workers = n_cores * n_sub
    chunk = SC_GATHER_ROWS
    per_worker = n // n_workers
    n_chunks = per_worker // chunk
    assert n_chunks * chunk * n_workers == n and n_chunks % 2 == 0
    mesh = plsc.VectorSubcoreMesh(core_axis_name="c", subcore_axis_name="s")

    @functools.partial(
        pl.kernel, mesh=mesh,
        out_type=jax.ShapeDtypeStruct((n, d), table.dtype),
        scratch_types=[pltpu.VMEM((n_chunks, chunk), jnp.int32), pltpu.VMEM((2, chunk, d), table.dtype),
                       pltpu.SemaphoreType.DMA((2,))],
    )
    def gather_kernel(table_hbm, idx_hbm, out_hbm, idx_v, rows_v, sems):
        wid = lax.axis_index("s") * n_cores + lax.axis_index("c")
        base = wid * per_worker
        pltpu.sync_copy(idx_hbm.at[wid], idx_v)

        def gather(ci, slot):
            return pltpu.make_async_copy(table_hbm.at[idx_v.at[ci]], rows_v.at[slot], sems.at[slot])

        gather(0, 0).start()

        @pl.loop(0, n_chunks, step=2)
        def _(c0):
            for slot in range(2):
                ci = c0 + slot
                gather(ci, slot).wait()

                @pl.when(ci + 1 < n_chunks)
                def _():
                    gather(ci + 1, 1 - slot).start()

                pltpu.sync_copy(rows_v.at[slot], out_hbm.at[pl.ds(base + ci * chunk, chunk)])

    return gather_kernel(table, idx.reshape(n_workers, n_chunks, chunk))


COMBINE_TM = 256


def _combine_body(gate_ref, x_ref, g_ref, y_ref, o_ref):
    gates = gate_ref[...]
    y = gates[:, 0:1] * y_ref[0]
    for k in range(1, TOP_K):
        y = y + gates[:, k:k + 1] * y_ref[k]
    o_ref[...] = x_ref[...] + g_ref[...] * y


def _combine(y_kt, gates, x2, mod4, gate_slot, seq):
    t, d = x2.shape
    tm = COMBINE_TM
    tps = seq // tm
    return pl.pallas_call(
        _combine_body,
        out_shape=jax.ShapeDtypeStruct((t, d), F32),
        grid=(t // tm,),
        in_specs=[pl.BlockSpec((tm, LANES), lambda i: (i, 0)),
                  pl.BlockSpec((tm, d), lambda i: (i, 0)),
                  pl.BlockSpec((None, None, 1, d), lambda i: (i // tps, gate_slot, 0, 0)),
                  pl.BlockSpec((TOP_K, tm, d), lambda i: (0, i, 0))],
        out_specs=pl.BlockSpec((tm, d), lambda i: (i, 0)),
        compiler_params=_cparams(("parallel",)),
        name="moe_combine",
    )(gates, x2, mod4, y_kt)


def _moe_layer(x2, gain, mod4, router_w, router_b, up_w, up_b, down_w, down_b, layer, seq):
    t, d = x2.shape
    rw_pad = jnp.pad(router_w[layer], ((0, 0), (0, LANES - N_EXPERTS)))
    rb_pad = jnp.pad(router_b[layer], (0, LANES - N_EXPERTS)).reshape(1, LANES)
    h2, idx, gates, rank, counts = _router(x2, gain, mod4, 3, 4, rw_pad, rb_pad, seq)

    bm = MOE_ROWS
    counts = counts[0, :N_EXPERTS].astype(jnp.int32)
    padded = (counts + bm - 1) // bm * bm
    p_ends = jnp.cumsum(padded)
    p_starts = p_ends - padded
    n_rows = t * TOP_K + N_EXPERTS * bm
    n_blk = n_rows // bm
    top_idx = idx[:, :TOP_K]
    onehot = top_idx[:, :, None] == jnp.arange(N_EXPERTS, dtype=jnp.int32)[None, None, :]
    dest = jnp.sum(jnp.where(onehot, p_starts[None, None, :], 0), axis=-1) + rank[:, :TOP_K]
    blk_start = jnp.arange(n_blk, dtype=jnp.int32) * bm
    blk_expert = jnp.sum((blk_start[:, None] >= p_ends[None, :]).astype(jnp.int32), axis=-1)
    blk_expert = jnp.minimum(blk_expert, N_EXPERTS - 1).astype(jnp.int32)
    n_used = (p_ends[-1:] // bm).astype(jnp.int32)

    xb = _sc_gather_rows(h2, _sc_row_tokens(dest.reshape(t * TOP_K), n_rows, t))
    f2 = up_w.shape[-1]
    y_rows = _expert_ffn(xb, blk_expert, n_used, up_w, up_b.reshape(up_b.shape[0], N_EXPERTS, 1, f2),
                         down_w, down_b.reshape(down_b.shape[0], N_EXPERTS, 1, d), layer)
    y_kt = _sc_gather_rows(y_rows, dest.T.reshape(TOP_K * t)).reshape(TOP_K, t, d)
    return _combine(y_kt, gates, x2, mod4, 5, seq)


QKV_TM = 256


def _rope_tables(pos_col, invf_row):
    tm = pos_col.shape[0]
    ang = pos_col * invf_row
    c = jnp.cos(ang)
    s = jnp.sin(ang)
    d = lax.broadcasted_iota(jnp.int32, (tm, LANES), 1) % B_HEAD_DIM
    half = ROPE_DIM // 2
    cos_t = jnp.where(d < ROPE_DIM, c, 1.0)
    sin_lo = jnp.where(d < half, -s, 0.0)
    sin_hi = jnp.where(jnp.logical_and(d >= half, d < ROPE_DIM), s, 0.0)
    return cos_t, sin_lo, sin_hi


def _head_norm_rope(x, gain_row, tables):
    cos_t, sin_lo, sin_hi = tables
    lane = lax.broadcasted_iota(jnp.int32, x.shape, 1)
    sq = x * x
    first = lane < B_HEAD_DIM
    ss0 = jnp.sum(jnp.where(first, sq, 0.0), axis=-1, keepdims=True)
    ss1 = jnp.sum(jnp.where(first, 0.0, sq), axis=-1, keepdims=True)
    ms = jnp.where(first, ss0, ss1) * (1.0 / B_HEAD_DIM)
    xn = x * lax.rsqrt(ms + EPS) * gain_row
    half = ROPE_DIM // 2
    return (xn * cos_t + pltpu.roll(xn, LANES - half, 1) * sin_lo + pltpu.roll(xn, half, 1) * sin_hi)


def _qkv_body(x_ref, pos_ref, gq_ref, shq_ref, scq_ref, gkv_ref, shkv_ref, sckv_ref, wq_ref, wkv_ref,
              qg_ref, kg_ref, invf_ref, q_ref, k_ref, v_ref):
    x = x_ref[...]
    y = x * lax.rsqrt(jnp.mean(x * x, axis=-1, keepdims=True) + EPS)
    hq = ((y * gq_ref[...]) * (1.0 + scq_ref[...]) + shq_ref[...]).astype(BF16)
    hkv = ((y * gkv_ref[...]) * (1.0 + sckv_ref[...]) + shkv_ref[...]).astype(BF16)
    tables = _rope_tables(pos_ref[...].astype(F32), invf_ref[...])
    kv = jnp.dot(hkv, wkv_ref[...], preferred_element_type=F32)
    k_ref[...] = _head_norm_rope(kv[:, :LANES], kg_ref[...], tables).astype(BF16)
    v_ref[...] = kv[:, LANES:].astype(BF16)
    q = jnp.dot(hq, wq_ref[...], preferred_element_type=F32)
    scale = B_HEAD_DIM ** -0.5
    for p in range(q.shape[-1] // LANES):
        qp = _head_norm_rope(q[:, p * LANES:(p + 1) * LANES], qg_ref[...], tables)
        q_ref[:, p * LANES:(p + 1) * LANES] = (qp * scale).astype(BF16)


def _qkv(x2, pos2, gq, mod4, gkv, kvmod4, wq_bf, wkv_bf, qg2, kg2, invf, seq):
    t, d = x2.shape
    tm = QKV_TM
    tps = seq // tm
    nq = wq_bf.shape[-1]
    nkv = wkv_bf.shape[-1]
    row = lambda n: pl.BlockSpec((1, n), lambda i: (0, 0))
    modspec = lambda slot: pl.BlockSpec((None, None, 1, d), lambda i, slot=slot: (i // tps, slot, 0, 0))
    return pl.pallas_call(
        _qkv_body,
        out_shape=(jax.ShapeDtypeStruct((t, nq), BF16), jax.ShapeDtypeStruct((t, LANES), BF16),
                   jax.ShapeDtypeStruct((t, LANES), BF16)),
        grid=(t // tm,),
        in_specs=[pl.BlockSpec((tm, d), lambda i: (i, 0)),
                  pl.BlockSpec((tm, 1), lambda i: (i, 0)),
                  row(d), modspec(0), modspec(1),
                  row(d), modspec(0), modspec(1),
                  pl.BlockSpec((d, nq), lambda i: (0, 0)),
                  pl.BlockSpec((d, nkv), lambda i: (0, 0)),
                  row(LANES), row(LANES), row(LANES)],
        out_specs=(pl.BlockSpec((tm, nq), lambda i: (i, 0)),
                   pl.BlockSpec((tm, LANES), lambda i: (i, 0)),
                   pl.BlockSpec((tm, LANES), lambda i: (i, 0))),
        compiler_params=_cparams(("parallel",)),
        name="swa_qkv_proj",
    )(x2, pos2, gq, mod4, mod4, gkv, kvmod4, kvmod4, wq_bf, wkv_bf, qg2, kg2, invf)


def _attn_body(sink_ref, q_ref, kc_ref, kp_ref, vc_ref, vp_ref, o_ref):
    n = pl.program_id(1)
    w = WINDOW
    qi = lax.broadcasted_iota(jnp.int32, (w, 2 * w), 0) + w
    kj = lax.broadcasted_iota(jnp.int32, (w, 2 * w), 1)
    band = jnp.logical_and(kj <= qi, qi - kj < w)
    mask = jnp.logical_and(band, jnp.logical_or(n > 0, kj >= w))
    hd = B_HEAD_DIM
    for g in range(B_KV_HEADS):
        kb = jnp.concatenate([kp_ref[:, g * hd:(g + 1) * hd], kc_ref[:, g * hd:(g + 1) * hd]], axis=0)
        vb = jnp.concatenate([vp_ref[:, g * hd:(g + 1) * hd], vc_ref[:, g * hd:(g + 1) * hd]], axis=0)
        for hh in range(B_GROUP):
            head = g * B_GROUP + hh
            qh = q_ref[:, head * hd:(head + 1) * hd]
            s = lax.dot_general(qh, kb, (((1,), (1,)), ((), ())), preferred_element_type=F32)
            s = jnp.where(mask, s, NEG_INF)
            sink = sink_ref[0, head]
            m = jnp.maximum(jnp.max(s, axis=-1, keepdims=True), sink)
            p = jnp.exp(s - m)
            denom = jnp.sum(p, axis=-1, keepdims=True) + jnp.exp(sink - m)
            o = jnp.dot(p.astype(BF16), vb, preferred_element_type=F32) / denom
            o_ref[:, head * hd:(head + 1) * hd] = o.astype(BF16)


def _attention(q, k, v, sinks2, bsz, seq):
    t, nq = q.shape
    w = WINDOW
    nb = seq // w
    cur = lambda b, n: (b * nb + n, 0)
    prev = lambda b, n: (b * nb + jnp.maximum(n - 1, 0), 0)
    return pl.pallas_call(
        _attn_body,
        out_shape=jax.ShapeDtypeStruct((t, nq), BF16),
        grid=(bsz, nb),
        in_specs=[pl.BlockSpec(memory_space=pltpu.SMEM),
                  pl.BlockSpec((w, nq), cur),
                  pl.BlockSpec((w, LANES), cur), pl.BlockSpec((w, LANES), prev),
                  pl.BlockSpec((w, LANES), cur), pl.BlockSpec((w, LANES), prev)],
        out_specs=pl.BlockSpec((w, nq), cur),
        compiler_params=_cparams(("parallel", "parallel")),
        name="swa_sink_attention",
    )(sinks2, q, k, k, v, v)


def kernel(x, c, positions, ada_w, ada_b, norm_gain, a_w_in, a_conv, a_log, a_dt_bias, a_out_gain, a_w_out,
           kv_ada_w, kv_ada_b, kv_norm_gain, kv_w, k_norm_gain, b_w_q, q_norm_gain, b_sinks, b_w_out,
           router_w, router_b, up_w, up_b, down_w, down_b):
    bsz, seq, d = x.shape
    t = bsz * seq
    depth = ada_w.shape[0]
    x2 = x.reshape(t, d)
    ada_b3 = ada_b.reshape(depth, 1, 6 * d)

    for layer in range(depth):
        mod4 = _modulation(c, ada_w, ada_b3, layer).reshape(bsz, 6, 1, d)
        gain1 = norm_gain[layer, 0].reshape(1, d)
        gain2 = norm_gain[layer, 1].reshape(1, d)
        if layer < N_A_LAYERS:
            w_in = jnp.pad(a_w_in[layer].astype(BF16), ((0, 0), (0, A_PROJ_PAD - a_w_in.shape[-1])))
            proj = _in_proj(x2, gain1, mod4, w_in, seq)
            o = _gdn(proj, a_conv[layer], a_log[layer].reshape(1, A_HEADS),
                     a_dt_bias[layer].reshape(1, A_HEADS), a_out_gain[layer].reshape(1, A_HEAD_DIM),
                     bsz, seq)
            x2 = _out_proj_residual(o, a_w_out[layer].astype(BF16), x2, mod4, 2, seq)
        else:
            j = layer - N_A_LAYERS
            kvmod4 = _modulation(c, kv_ada_w.reshape(1, d, 2 * d), kv_ada_b.reshape(1, 1, 2 * d), 0)
            kvmod4 = kvmod4.reshape(bsz, 2, 1, d)
            half = ROPE_DIM // 2
            inv_freq = ROPE_THETA ** (-np.arange(0, ROPE_DIM, 2, dtype=np.float32) / ROPE_DIM)
            invf = jnp.asarray(np.tile(inv_freq.astype(np.float32), LANES // half).reshape(1, LANES))
            q, k, v = _qkv(x2, positions.reshape(t, 1), gain1, mod4, kv_norm_gain.reshape(1, d), kvmod4,
                           b_w_q[j].astype(BF16), kv_w.astype(BF16),
                           jnp.tile(q_norm_gain[j], 2).reshape(1, LANES),
                           jnp.tile(k_norm_gain, 2).reshape(1, LANES), invf, seq)
            o = _attention(q, k, v, b_sinks[j].reshape(1, B_Q_HEADS), bsz, seq)
            x2 = _out_proj_residual(o, b_w_out[j].astype(BF16), x2, mod4, 2, seq)
        x2 = _moe_layer(x2, gain2, mod4, router_w, router_b, up_w, up_b, down_w, down_b, layer, seq)
    return x2.reshape(bsz, seq, d)
```

```python
import dataclasses
import functools

import jax
import jax.numpy as jnp
import numpy as np
from jax import lax
from jax.experimental import pallas as pl
from jax.experimental.pallas import tpu as pltpu
from jax.experimental.pallas import tpu_sc as plsc

F32 = jnp.float32
BF16 = jnp.bfloat16
HIGHEST = lax.Precision.HIGHEST

D_MODEL = 1024
N_A_LAYERS = 1

A_HEAD_DIM = 128
A_HEADS = 8
A_WIDTH = 1024
A_CONV = 4
A_CHUNK = 64
A_PROJ_PAD = 4608

B_HEAD_DIM = 64
B_Q_HEADS = 16
B_KV_HEADS = 2
B_GROUP = 8
WINDOW = 128
ROPE_DIM = 16
ROPE_THETA = 500000.0

N_EXPERTS = 32
TOP_K = 4
D_FF = 1024
SWIGLU_LIMIT = 7.0
SWIGLU_ALPHA = 1.702
MOE_ROWS = 256

EPS = 1e-6
LANES = 128
NEG_INF = float("-inf")

VMEM_LIMIT = 56 * 1024 * 1024


def _cparams(sem, vmem=VMEM_LIMIT):
    return pltpu.CompilerParams(dimension_semantics=sem, vmem_limit_bytes=vmem)


def _silu(x):
    return x * jax.nn.sigmoid(x)


HI_HALF = -65536


def _pack_bf16_pairs(x):
    n = x.shape[-1] // 2
    bits = lax.bitcast_convert_type(x.astype(BF16).astype(F32), jnp.int32)
    return jnp.bitwise_or(lax.shift_right_logical(bits[:, :n], 16), jnp.bitwise_and(bits[:, n:], HI_HALF))


def _unpack_bf16_pairs(w):
    lo = lax.bitcast_convert_type(lax.shift_left(w, 16), F32)
    hi = lax.bitcast_convert_type(jnp.bitwise_and(w, HI_HALF), F32)
    return lo, hi


def _ada_norm(x, gain, shift, scale):
    y = x * lax.rsqrt(jnp.mean(x * x, axis=-1, keepdims=True) + EPS)
    return (y * gain) * (1.0 + scale) + shift


def _mod_body(c_ref, w_ref, b_ref, o_ref):
    o_ref[...] = jnp.dot(_silu(c_ref[...]), w_ref[...], preferred_element_type=F32,
                         precision=HIGHEST) + b_ref[...]


def _modulation(c, w3, b3, layer):
    bsz, d = c.shape
    n = w3.shape[-1]
    tn = 1024
    return pl.pallas_call(
        _mod_body,
        out_shape=jax.ShapeDtypeStruct((bsz, n), F32),
        grid=(n // tn,),
        in_specs=[pl.BlockSpec((bsz, d), lambda j: (0, 0)),
                  pl.BlockSpec((None, d, tn), lambda j: (layer, 0, j)),
                  pl.BlockSpec((None, 1, tn), lambda j: (layer, 0, j))],
        out_specs=pl.BlockSpec((bsz, tn), lambda j: (0, j)),
        compiler_params=_cparams(("arbitrary",)),
        name="adaln_mod",
    )(c, w3, b3)


def _inproj_body(x_ref, gain_ref, sh_ref, sc_ref, w_ref, o_ref):
    h = _ada_norm(x_ref[...], gain_ref[...], sh_ref[...], sc_ref[...]).astype(BF16)
    n = o_ref.shape[-1]
    for n0 in range(0, n, 512):
        o_ref[:, n0:n0 + 512] = jnp.dot(h, w_ref[:, n0:n0 + 512], preferred_element_type=F32)


def _in_proj(x2, gain, mod4, w_bf, seq):
    t, d = x2.shape
    n = w_bf.shape[-1]
    tm = 256
    tps = seq // tm
    return pl.pallas_call(
        _inproj_body,
        out_shape=jax.ShapeDtypeStruct((t, n), F32),
        grid=(t // tm,),
        in_specs=[pl.BlockSpec((tm, d), lambda i: (i, 0)),
                  pl.BlockSpec((1, d), lambda i: (0, 0)),
                  pl.BlockSpec((None, None, 1, d), lambda i: (i // tps, 0, 0, 0)),
                  pl.BlockSpec((None, None, 1, d), lambda i: (i // tps, 1, 0, 0)),
                  pl.BlockSpec((d, n), lambda i: (0, 0))],
        out_specs=pl.BlockSpec((tm, n), lambda i: (i, 0)),
        compiler_params=_cparams(("parallel",)),
        name="gdn_in_proj",
    )(x2, gain, mod4, mod4, w_bf)


GDN_HG = 2
GDN_RB = 256
GDN_SC = 128
GDN_SUB = 4


def _split_bf16(x):
    hi = x.astype(BF16)
    return hi, (x - hi.astype(F32)).astype(BF16)


def _mm3(a_hi, a_lo, b_hi, b_lo, dims=(((1,), (0,)), ((), ()))):
    dg = functools.partial(lax.dot_general, dimension_numbers=dims, preferred_element_type=F32)
    return dg(a_hi, b_hi) + (dg(a_lo, b_hi) + dg(a_hi, b_lo))


def _dot3(a, b, dims):
    return _mm3(*_split_bf16(a), *_split_bf16(b), dims)


def _gdn_body(alog_ref, dtb_ref, q_ref, k_ref, v_ref, z_ref, ab_ref, cq_ref, ck_ref, cv_ref, og_ref,
              o_ref, qs, ks, vs, bs, gcs, mp, bc, qp, op):
    seq = q_ref.shape[0]
    hd = A_HEAD_DIM
    ch = A_CHUNK
    n_chunks = seq // ch
    g_idx = pl.program_id(1)

    lane = lax.broadcasted_iota(jnp.int32, (GDN_RB, hd), 1)
    row = lax.broadcasted_iota(jnp.int32, (GDN_RB, hd), 0)
    row_in_chunk = row % ch

    def conv_silu(x_ref, c_ref, col0, r, start):
        main = x_ref[pl.ds(start, GDN_RB), col0:col0 + hd]
        pstart = pl.multiple_of(jnp.maximum(start - 8, 0), 8)
        prev = x_ref[pl.ds(pstart, 8), col0:col0 + hd] * jnp.where(r > 0, 1.0, 0.0)
        ext = jnp.concatenate([prev, main], axis=0)
        acc = main * c_ref[A_CONV - 1:A_CONV, col0:col0 + hd]
        for s in range(1, A_CONV):
            shifted = pltpu.roll(ext, s, 0)[8:8 + GDN_RB]
            acc = acc + shifted * c_ref[A_CONV - 1 - s:A_CONV - s, col0:col0 + hd]
        return _silu(acc)

    def l2n(x):
        return x * lax.rsqrt(jnp.sum(x * x, axis=-1, keepdims=True) + EPS)

    def prep(r, carry):
        start = pl.multiple_of(r * GDN_RB, GDN_RB)
        ab = ab_ref[pl.ds(start, GDN_RB), :]
        for hh in range(GDN_HG):
            col0 = hh * hd
            head = g_idx * GDN_HG + hh
            qs[hh, pl.ds(start, GDN_RB), :] = l2n(conv_silu(q_ref, cq_ref, col0, r, start)) * (hd ** -0.5)
            ks[hh, pl.ds(start, GDN_RB), :] = l2n(conv_silu(k_ref, ck_ref, col0, r, start))
            vs[hh, pl.ds(start, GDN_RB), :] = conv_silu(v_ref, cv_ref, col0, r, start)
            a_col = jnp.sum(jnp.where(lane == head, ab, 0.0), axis=-1, keepdims=True)
            b_col = jnp.sum(jnp.where(lane == head + A_HEADS, ab, 0.0), axis=-1, keepdims=True)
            bs[hh, pl.ds(start, GDN_RB), :] = jnp.broadcast_to(jax.nn.sigmoid(b_col), (GDN_RB, hd))
            xa = a_col + dtb_ref[0, head]
            softplus = jnp.maximum(xa, 0.0) + jnp.log1p(jnp.exp(-jnp.abs(xa)))
            g = jnp.broadcast_to(-jnp.exp(alog_ref[0, head]) * softplus, (GDN_RB, hd))
            s = 1
            while s < ch:
                g = g + jnp.where(row_in_chunk >= s, pltpu.roll(g, s, 0), 0.0)
                s *= 2
            gcs[hh, pl.ds(start, GDN_RB), :] = g
        return carry

    lax.fori_loop(0, seq // GDN_RB, prep, 0)

    sc_rows = GDN_SC
    per_sc = sc_rows // ch
    ri = lax.broadcasted_iota(jnp.int32, (sc_rows, sc_rows), 0)
    cj = lax.broadcasted_iota(jnp.int32, (sc_rows, sc_rows), 1)
    same_chunk = (ri // ch) == (cj // ch)
    incl = jnp.logical_and(same_chunk, ri >= cj)
    strict = jnp.logical_and(same_chunk, ri > cj)
    eye = jnp.where(ri == cj, 1.0, 0.0)
    chunk_of_col = lax.broadcasted_iota(jnp.int32, (hd, sc_rows), 1) // ch
    lanes_nt = (((1,), (1,)), ((), ()))
    plain = (((1,), (0,)), ((), ()))

    def local(i, carry):
        chains = [(i * GDN_SUB + sub, hh) for sub in range(GDN_SUB) for hh in range(GDN_HG)]
        rows_of = lambda blk: pl.ds(pl.multiple_of(blk * sc_rows, sc_rows), sc_rows)

        def start(blk, hh):
            rows = rows_of(blk)
            k = ks[hh, rows, :]
            gc = gcs[hh, rows, :]
            decay = jnp.where(incl, jnp.exp(jnp.minimum(gc - gc.T[0:1, :], 0.0)), 0.0)
            k_bf = k.astype(BF16)
            kk = lax.dot_general((k * bs[hh, rows, :]).astype(BF16), k_bf, lanes_nt, preferred_element_type=F32)
            qk = lax.dot_general(qs[hh, rows, :].astype(BF16), k_bf, lanes_nt, preferred_element_type=F32)
            return jnp.where(strict, -(kk * decay), 0.0), (qk * decay).astype(BF16)

        started = [start(*c) for c in chains]
        qks = [s[1] for s in started]
        invs = [eye + s[0] for s in started]
        pws = [_dot3(s[0], s[0], plain) for s in started]
        for step in range(1, 6):
            for n in range(len(chains)):
                pw_hi, pw_lo = _split_bf16(pws[n])
                inv_hi, inv_lo = _split_bf16(invs[n])
                if step < 5:
                    prod = _mm3(jnp.concatenate([pw_hi, inv_hi], axis=0),
                                jnp.concatenate([pw_lo, inv_lo], axis=0), pw_hi, pw_lo)
                    pws[n] = prod[:sc_rows]
                    invs[n] = invs[n] + prod[sc_rows:]
                else:
                    invs[n] = invs[n] + _mm3(inv_hi, inv_lo, pw_hi, pw_lo)

        def solve(n, blk, hh):
            rows = rows_of(blk)
            beta = bs[hh, rows, :]
            rhs = jnp.concatenate([vs[hh, rows, :] * beta, ks[hh, rows, :] * beta * jnp.exp(gcs[hh, rows, :])], axis=1)
            return _dot3(invs[n], rhs, plain).astype(BF16)

        uws = [solve(n, *c) for n, c in enumerate(chains)]

        def finish(n, blk, hh):
            rows = rows_of(blk)
            k = ks[hh, rows, :]
            gc = gcs[hh, rows, :]
            res = jnp.dot(qks[n], uws[n], preferred_element_type=F32)
            op[hh, rows, :] = res[:, :hd]
            qp[hh, rows, :] = (qs[hh, rows, :] * jnp.exp(gc) - res[:, hd:]).astype(BF16)
            gl = jnp.concatenate(
                [jnp.broadcast_to(gc[(j + 1) * ch - 1:(j + 1) * ch, :], (ch, hd)) for j in range(per_sc)], axis=0)
            kt_t = (k * jnp.exp(gl - gc)).T
            for j in range(per_sc):
                kt_j = jnp.where(chunk_of_col == j, kt_t, 0.0).astype(BF16)
                bm = jnp.dot(kt_j, uws[n], preferred_element_type=F32)
                bc[hh, blk * per_sc + j] = bm[:, :hd]
                mp[hh, blk * per_sc + j] = bm[:, hd:].astype(BF16)

        for n, c in enumerate(chains):
            finish(n, *c)
        return carry

    lax.fori_loop(0, seq // (sc_rows * GDN_SUB), local, 0)

    def scan(c, states):
        r0 = pl.multiple_of(c * ch, ch)
        rows = pl.ds(r0, ch)
        new_states = []
        for hh in range(GDN_HG):
            col0 = hh * hd
            state = states[hh]
            s_bf = state.astype(BF16)
            o = jnp.dot(qp[hh, rows, :], s_bf, preferred_element_type=F32) + op[hh, rows, :]
            g_tot = jnp.exp(gcs[hh, pl.ds(r0 + ch - 1, 1), :])
            new_states.append(state * g_tot - jnp.dot(mp[hh, c], s_bf, preferred_element_type=F32) + bc[hh, c])
            on = o * lax.rsqrt(jnp.mean(o * o, axis=-1, keepdims=True) + EPS) * og_ref[...]
            z = z_ref[rows, col0:col0 + hd]
            o_ref[rows, col0:col0 + hd] = (on * _silu(z)).astype(BF16)
        return tuple(new_states)

    lax.fori_loop(0, n_chunks, scan, tuple(jnp.zeros((hd, hd), F32) for _ in range(GDN_HG)))


def _gdn(proj, conv_w, a_log, dt_bias, out_gain, bsz, seq):
    t = proj.shape[0]
    hd = A_HEAD_DIM
    wb = hd * GDN_HG
    ng = A_HEADS // GDN_HG
    per = A_WIDTH // wb
    n_chunks = seq // A_CHUNK
    smem = pl.BlockSpec(memory_space=pltpu.SMEM)
    seq_spec = lambda off: pl.BlockSpec((seq, wb), lambda b, g, off=off: (b, off + g))
    conv_spec = lambda off: pl.BlockSpec((A_CONV, wb), lambda b, g, off=off: (0, off + g))
    sc = lambda dt=F32: pltpu.VMEM((GDN_HG, seq, hd), dt)
    return pl.pallas_call(
        _gdn_body,
        out_shape=jax.ShapeDtypeStruct((t, A_WIDTH), BF16),
        grid=(bsz, ng),
        in_specs=[smem, smem,
                  seq_spec(0), seq_spec(per), seq_spec(2 * per), seq_spec(3 * per),
                  pl.BlockSpec((seq, LANES), lambda b, g: (b, 4 * A_WIDTH // LANES)),
                  conv_spec(0), conv_spec(per), conv_spec(2 * per),
                  pl.BlockSpec((1, hd), lambda b, g: (0, 0))],
        out_specs=pl.BlockSpec((seq, wb), lambda b, g: (b, g)),
        scratch_shapes=[sc(), sc(), sc(), sc(), sc(),
                        pltpu.VMEM((GDN_HG, n_chunks, hd, hd), BF16),
                        pltpu.VMEM((GDN_HG, n_chunks, hd, hd), F32),
                        sc(BF16), sc()],
        compiler_params=_cparams(("parallel", "parallel")),
        name="gdn_core",
    )(a_log, dt_bias, proj, proj, proj, proj, proj, conv_w, conv_w, conv_w, out_gain)


def _oproj_body(a_ref, w_ref, x_ref, g_ref, o_ref):
    y = jnp.dot(a_ref[...], w_ref[...], preferred_element_type=F32)
    o_ref[...] = x_ref[...] + g_ref[...] * y


def _out_proj_residual(a_bf, w_bf, x2, mod4, gate_slot, seq):
    t, d = x2.shape
    kdim = a_bf.shape[-1]
    tm = 512
    tps = seq // tm
    return pl.pallas_call(
        _oproj_body,
        out_shape=jax.ShapeDtypeStruct((t, d), F32),
        grid=(t // tm,),
        in_specs=[pl.BlockSpec((tm, kdim), lambda i: (i, 0)),
                  pl.BlockSpec((kdim, d), lambda i: (0, 0)),
                  pl.BlockSpec((tm, d), lambda i: (i, 0)),
                  pl.BlockSpec((None, None, 1, d), lambda i: (i // tps, gate_slot, 0, 0))],
        out_specs=pl.BlockSpec((tm, d), lambda i: (i, 0)),
        compiler_params=_cparams(("parallel",)),
        name="out_proj_residual",
    )(a_bf, w_bf, x2, mod4)


ROUTER_TM = 512


def _router_body(x_ref, gain_ref, sh_ref, sc_ref, rw_ref, rb_ref, h_ref, idx_ref, gate_ref, rank_ref,
                 cnt_ref, carry):
    i = pl.program_id(0)

    @pl.when(i == 0)
    def _():
        carry[...] = jnp.zeros_like(carry)

    tm = x_ref.shape[0]
    h = _ada_norm(x_ref[...], gain_ref[...], sh_ref[...], sc_ref[...])
    h_ref[...] = _pack_bf16_pairs(h)
    lane = lax.broadcasted_iota(jnp.int32, (tm, LANES), 1)
    logits = _dot3(h, rw_ref[...], (((1,), (0,)), ((), ()))) + rb_ref[...]
    logits = jnp.where(lane < N_EXPERTS, logits, NEG_INF)
    tops, sels = [], []
    work = logits
    for _ in range(TOP_K):
        m = jnp.max(work, axis=-1, keepdims=True)
        sel = jnp.min(jnp.where(work == m, lane, LANES), axis=-1, keepdims=True)
        work = jnp.where(lane == sel, NEG_INF, work)
        tops.append(m)
        sels.append(sel)
    exps = [jnp.exp(m - tops[0]) for m in tops]
    denom = exps[0] + exps[1] + exps[2] + exps[3]
    onehot = jnp.zeros((tm, LANES), F32)
    for sel in sels:
        onehot = onehot + (lane == sel).astype(F32)
    ti = lax.broadcasted_iota(jnp.int32, (tm, tm), 0)
    tj = lax.broadcasted_iota(jnp.int32, (tm, tm), 1)
    tri = (ti > tj).astype(BF16)
    before = jnp.dot(tri, onehot.astype(BF16), preferred_element_type=F32) + carry[...]
    idx_out = jnp.zeros((tm, LANES), jnp.int32)
    gate_out = jnp.zeros((tm, LANES), F32)
    rank_out = jnp.zeros((tm, LANES), jnp.int32)
    for k in range(TOP_K):
        rank_k = jnp.sum(jnp.where(lane == sels[k], before, 0.0), axis=-1, keepdims=True)
        idx_out = jnp.where(lane == k, sels[k], idx_out)
        gate_out = jnp.where(lane == k, exps[k] / denom, gate_out)
        rank_out = jnp.where(lane == k, rank_k.astype(jnp.int32), rank_out)
    idx_ref[...] = idx_out
    gate_ref[...] = gate_out
    rank_ref[...] = rank_out
    carry[...] = carry[...] + jnp.sum(onehot, axis=0, keepdims=True)
    cnt_ref[...] = carry[...]


def _router(x2, gain, mod4, sh_slot, sc_slot, rw_pad, rb_pad, seq):
    t, d = x2.shape
    tm = ROUTER_TM
    tps = seq // tm
    tok = lambda dt: jax.ShapeDtypeStruct((t, LANES), dt)
    tok_spec = pl.BlockSpec((tm, LANES), lambda i: (i, 0))
    return pl.pallas_call(
        _router_body,
        out_shape=(jax.ShapeDtypeStruct((t, d // 2), jnp.int32), tok(jnp.int32), tok(F32), tok(jnp.int32),
                   jax.ShapeDtypeStruct((1, LANES), F32)),
        grid=(t // tm,),
        in_specs=[pl.BlockSpec((tm, d), lambda i: (i, 0)),
                  pl.BlockSpec((1, d), lambda i: (0, 0)),
                  pl.BlockSpec((None, None, 1, d), lambda i: (i // tps, sh_slot, 0, 0)),
                  pl.BlockSpec((None, None, 1, d), lambda i: (i // tps, sc_slot, 0, 0)),
                  pl.BlockSpec((d, LANES), lambda i: (0, 0)),
                  pl.BlockSpec((1, LANES), lambda i: (0, 0))],
        out_specs=(pl.BlockSpec((tm, d // 2), lambda i: (i, 0)), tok_spec, tok_spec, tok_spec,
                   pl.BlockSpec((1, LANES), lambda i: (0, 0))),
        scratch_shapes=[pltpu.VMEM((1, LANES), F32)],
        compiler_params=_cparams(("arbitrary",)),
        name="moe_router",
    )(x2, gain, mod4, mod4, rw_pad, rb_pad)


SC_LANES = 16
SC_INDEX_CHUNK = 8192


def _sc_row_tokens(dest_flat, n_rows, n_tok):
    n_assign = dest_flat.shape[0]
    assert n_rows % SC_LANES == 0 and n_assign % SC_INDEX_CHUNK == 0
    n_cores = plsc.get_sparse_core_info().num_cores
    mesh = plsc.VectorSubcoreMesh(core_axis_name="c", subcore_axis_name="s")

    @functools.partial(
        pl.kernel, mesh=mesh,
        out_type=jax.ShapeDtypeStruct((n_rows,), jnp.int32),
        scratch_types=[pltpu.VMEM((n_rows,), jnp.int32), pltpu.VMEM((SC_INDEX_CHUNK,), jnp.int32)],
        compiler_params=dataclasses.replace(pltpu.CompilerParams(), needs_layout_passes=False),
    )
    def row_token_kernel(dest_hbm, out_hbm, rt_v, d_v):
        wid = lax.axis_index("s") * n_cores + lax.axis_index("c")

        @pl.when(wid == 0)
        def _():
            lanes = lax.iota(jnp.int32, SC_LANES)

            @pl.loop(0, n_rows // SC_LANES)
            def _(i):
                rt_v[pl.ds(i * SC_LANES, SC_LANES)] = lax.rem(lanes + i * SC_LANES, n_tok)

            @pl.loop(0, n_assign // SC_INDEX_CHUNK)
            def _(c):
                pltpu.sync_copy(dest_hbm.at[pl.ds(c * SC_INDEX_CHUNK, SC_INDEX_CHUNK)], d_v)

                @pl.loop(0, SC_INDEX_CHUNK // SC_LANES)
                def _(i):
                    idx = d_v[pl.ds(i * SC_LANES, SC_LANES)]
                    tok = lax.div(lanes + (c * SC_INDEX_CHUNK + i * SC_LANES), TOP_K)
                    plsc.store_scatter(rt_v, [idx], tok)

            pltpu.sync_copy(rt_v, out_hbm)

    return row_token_kernel(dest_flat)


def _ffn_body(be_ref, nu_ref, x_ref, uw_ref, ub_ref, dw_ref, db_ref, y_ref, uw_bf, dw_bf):
    i = pl.program_id(0)
    changed = jnp.logical_or(i == 0, be_ref[i] != be_ref[jnp.maximum(i - 1, 0)])
    active = i < nu_ref[0]

    @pl.when(jnp.logical_and(changed, active))
    def _():
        rows = 128
        for r0 in range(0, uw_ref.shape[0], rows):
            uw_bf[r0:r0 + rows, :] = uw_ref[r0:r0 + rows, :].astype(BF16)
        for r0 in range(0, dw_ref.shape[0], rows):
            dw_bf[r0:r0 + rows, :] = dw_ref[r0:r0 + rows, :].astype(BF16)

    @pl.when(active)
    def _():
        x = jnp.concatenate(_unpack_bf16_pairs(x_ref[...]), axis=1).astype(BF16)
        gu = jnp.dot(x, uw_bf[...], preferred_element_type=F32) + ub_ref[...]
        gate = jnp.minimum(gu[:, :D_FF], SWIGLU_LIMIT)
        lin = jnp.clip(gu[:, D_FF:], -SWIGLU_LIMIT, SWIGLU_LIMIT)
        act = gate * jax.nn.sigmoid(SWIGLU_ALPHA * gate) * (lin + 1.0)
        y = jnp.dot(act.astype(BF16), dw_bf[...], preferred_element_type=F32) + db_ref[...]
        y_ref[...] = _pack_bf16_pairs(y)

    @pl.when(jnp.logical_not(active))
    def _():
        y_ref[...] = jnp.zeros_like(y_ref)


def _expert_ffn(xb, blk_expert, n_used, up_w, up_b4, down_w, down_b4, layer):
    n_rows, dp = xb.shape
    d = 2 * dp
    bm = MOE_ROWS
    n_blk = n_rows // bm
    f2 = up_w.shape[-1]
    grid_spec = pltpu.PrefetchScalarGridSpec(
        num_scalar_prefetch=2,
        grid=(n_blk,),
        in_specs=[pl.BlockSpec((bm, dp), lambda i, be, nu: (i, 0)),
                  pl.BlockSpec((None, None, d, f2), lambda i, be, nu: (layer, be[i], 0, 0)),
                  pl.BlockSpec((None, None, 1, f2), lambda i, be, nu: (layer, be[i], 0, 0)),
                  pl.BlockSpec((None, None, f2 // 2, d), lambda i, be, nu: (layer, be[i], 0, 0)),
                  pl.BlockSpec((None, None, 1, d), lambda i, be, nu: (layer, be[i], 0, 0))],
        out_specs=pl.BlockSpec((bm, dp), lambda i, be, nu: (i, 0)),
        scratch_shapes=[pltpu.VMEM((d, f2), BF16), pltpu.VMEM((f2 // 2, d), BF16)],
    )
    return pl.pallas_call(
        _ffn_body,
        out_shape=jax.ShapeDtypeStruct((n_rows, dp), jnp.int32),
        grid_spec=grid_spec,
        compiler_params=_cparams(("arbitrary",)),
        name="moe_expert_ffn",
    )(blk_expert, n_used, xb, up_w, up_b4, down_w, down_b4)


SC_GATHER_ROWS = 64


def _sc_gather_rows(table, idx):
    n, d = idx.shape[0], table.shape[1]
    info = plsc.get_sparse_core_info()
    n_cores, n_sub = info.num_cores, info.num_subcores
    n_workers = n_cores * n_sub
    chunk = SC_GATHER_ROWS
    per_worker = n // n_workers
    n_chunks = per_worker // chunk
    assert n_chunks * chunk * n_workers == n and n_chunks % 2 == 0
    mesh = plsc.VectorSubcoreMesh(core_axis_name="c", subcore_axis_name="s")

    @functools.partial(
        pl.kernel, mesh=mesh,
        out_type=jax.ShapeDtypeStruct((n, d), table.dtype),
        scratch_types=[pltpu.VMEM((n_chunks, chunk), jnp.int32), pltpu.VMEM((2, chunk, d), table.dtype),
                       pltpu.SemaphoreType.DMA((2,))],
    )
    def gather_kernel(table_hbm, idx_hbm, out_hbm, idx_v, rows_v, sems):
        wid = lax.axis_index("s") * n_cores + lax.axis_index("c")
        base = wid * per_worker
        pltpu.sync_copy(idx_hbm.at[wid], idx_v)

        def gather(ci, slot):
            return pltpu.make_async_copy(table_hbm.at[idx_v.at[ci]], rows_v.at[slot], sems.at[slot])

        gather(0, 0).start()

        @pl.loop(0, n_chunks, step=2)
        def _(c0):
            for slot in range(2):
                ci = c0 + slot
                gather(ci, slot).wait()

                @pl.when(ci + 1 < n_chunks)
                def _():
                    gather(ci + 1, 1 - slot).start()

                pltpu.sync_copy(rows_v.at[slot], out_hbm.at[pl.ds(base + ci * chunk, chunk)])

    return gather_kernel(table, idx.reshape(n_workers, n_chunks, chunk))


COMBINE_TM = 256


def _combine_body(gate_ref, x_ref, g_ref, y_ref, o_ref):
    gates = gate_ref[...]
    acc_lo = acc_hi = None
    for k in range(TOP_K):
        lo, hi = _unpack_bf16_pairs(y_ref[k])
        gk = gates[:, k:k + 1]
        acc_lo = gk * lo if acc_lo is None else acc_lo + gk * lo
        acc_hi = gk * hi if acc_hi is None else acc_hi + gk * hi
    o_ref[...] = x_ref[...] + g_ref[...] * jnp.concatenate([acc_lo, acc_hi], axis=1)


def _combine(y_kt, gates, x2, mod4, gate_slot, seq):
    t, d = x2.shape
    tm = COMBINE_TM
    tps = seq // tm
    return pl.pallas_call(
        _combine_body,
        out_shape=jax.ShapeDtypeStruct((t, d), F32),
        grid=(t // tm,),
        in_specs=[pl.BlockSpec((tm, LANES), lambda i: (i, 0)),
                  pl.BlockSpec((tm, d), lambda i: (i, 0)),
                  pl.BlockSpec((None, None, 1, d), lambda i: (i // tps, gate_slot, 0, 0)),
                  pl.BlockSpec((TOP_K, tm, d // 2), lambda i: (0, i, 0))],
        out_specs=pl.BlockSpec((tm, d), lambda i: (i, 0)),
        compiler_params=_cparams(("parallel",)),
        name="moe_combine",
    )(gates, x2, mod4, y_kt)


def _moe_layer(x2, gain, mod4, router_w, router_b, up_w, up_b, down_w, down_b, layer, seq):
    t, d = x2.shape
    rw_pad = jnp.pad(router_w[layer], ((0, 0), (0, LANES - N_EXPERTS)))
    rb_pad = jnp.pad(router_b[layer], (0, LANES - N_EXPERTS)).reshape(1, LANES)
    h2, idx, gates, rank, counts = _router(x2, gain, mod4, 3, 4, rw_pad, rb_pad, seq)

    bm = MOE_ROWS
    counts = counts[0, :N_EXPERTS].astype(jnp.int32)
    padded = (counts + bm - 1) // bm * bm
    p_ends = jnp.cumsum(padded)
    p_starts = p_ends - padded
    n_rows = t * TOP_K + N_EXPERTS * bm
    n_blk = n_rows // bm
    top_idx = idx[:, :TOP_K]
    onehot = top_idx[:, :, None] == jnp.arange(N_EXPERTS, dtype=jnp.int32)[None, None, :]
    dest = jnp.sum(jnp.where(onehot, p_starts[None, None, :], 0), axis=-1) + rank[:, :TOP_K]
    blk_start = jnp.arange(n_blk, dtype=jnp.int32) * bm
    blk_expert = jnp.sum((blk_start[:, None] >= p_ends[None, :]).astype(jnp.int32), axis=-1)
    blk_expert = jnp.minimum(blk_expert, N_EXPERTS - 1).astype(jnp.int32)
    n_used = (p_ends[-1:] // bm).astype(jnp.int32)

    xb = _sc_gather_rows(h2, _sc_row_tokens(dest.reshape(t * TOP_K), n_rows, t))
    f2 = up_w.shape[-1]
    y_rows = _expert_ffn(xb, blk_expert, n_used, up_w, up_b.reshape(up_b.shape[0], N_EXPERTS, 1, f2),
                         down_w, down_b.reshape(down_b.shape[0], N_EXPERTS, 1, d), layer)
    y_kt = _sc_gather_rows(y_rows, dest.T.reshape(TOP_K * t)).reshape(TOP_K, t, d // 2)
    return _combine(y_kt, gates, x2, mod4, 5, seq)


QKV_TM = 256


def _rope_tables(pos_col, invf_row):
    tm = pos_col.shape[0]
    ang = pos_col * invf_row
    c = jnp.cos(ang)
    s = jnp.sin(ang)
    d = lax.broadcasted_iota(jnp.int32, (tm, LANES), 1) % B_HEAD_DIM
    half = ROPE_DIM // 2
    cos_t = jnp.where(d < ROPE_DIM, c, 1.0)
    sin_lo = jnp.where(d < half, -s, 0.0)
    sin_hi = jnp.where(jnp.logical_and(d >= half, d < ROPE_DIM), s, 0.0)
    return cos_t, sin_lo, sin_hi


def _head_norm_rope(x, gain_row, tables):
    cos_t, sin_lo, sin_hi = tables
    lane = lax.broadcasted_iota(jnp.int32, x.shape, 1)
    sq = x * x
    first = lane < B_HEAD_DIM
    ss0 = jnp.sum(jnp.where(first, sq, 0.0), axis=-1, keepdims=True)
    ss1 = jnp.sum(jnp.where(first, 0.0, sq), axis=-1, keepdims=True)
    ms = jnp.where(first, ss0, ss1) * (1.0 / B_HEAD_DIM)
    xn = x * lax.rsqrt(ms + EPS) * gain_row
    half = ROPE_DIM // 2
    return (xn * cos_t + pltpu.roll(xn, LANES - half, 1) * sin_lo + pltpu.roll(xn, half, 1) * sin_hi)


def _qkv_body(x_ref, pos_ref, gq_ref, shq_ref, scq_ref, gkv_ref, shkv_ref, sckv_ref, wq_ref, wkv_ref,
              qg_ref, kg_ref, invf_ref, q_ref, k_ref, v_ref):
    x = x_ref[...]
    y = x * lax.rsqrt(jnp.mean(x * x, axis=-1, keepdims=True) + EPS)
    hq = ((y * gq_ref[...]) * (1.0 + scq_ref[...]) + shq_ref[...]).astype(BF16)
    hkv = ((y * gkv_ref[...]) * (1.0 + sckv_ref[...]) + shkv_ref[...]).astype(BF16)
    tables = _rope_tables(pos_ref[...].astype(F32), invf_ref[...])
    kv = jnp.dot(hkv, wkv_ref[...], preferred_element_type=F32)
    k_ref[...] = _head_norm_rope(kv[:, :LANES], kg_ref[...], tables).astype(BF16)
    v_ref[...] = kv[:, LANES:].astype(BF16)
    q = jnp.dot(hq, wq_ref[...], preferred_element_type=F32)
    scale = B_HEAD_DIM ** -0.5
    for p in range(q.shape[-1] // LANES):
        qp = _head_norm_rope(q[:, p * LANES:(p + 1) * LANES], qg_ref[...], tables)
        q_ref[:, p * LANES:(p + 1) * LANES] = (qp * scale).astype(BF16)


def _qkv(x2, pos2, gq, mod4, gkv, kvmod4, wq_bf, wkv_bf, qg2, kg2, invf, seq):
    t, d = x2.shape
    tm = QKV_TM
    tps = seq // tm
    nq = wq_bf.shape[-1]
    nkv = wkv_bf.shape[-1]
    row = lambda n: pl.BlockSpec((1, n), lambda i: (0, 0))
    modspec = lambda slot: pl.BlockSpec((None, None, 1, d), lambda i, slot=slot: (i // tps, slot, 0, 0))
    return pl.pallas_call(
        _qkv_body,
        out_shape=(jax.ShapeDtypeStruct((t, nq), BF16), jax.ShapeDtypeStruct((t, LANES), BF16),
                   jax.ShapeDtypeStruct((t, LANES), BF16)),
        grid=(t // tm,),
        in_specs=[pl.BlockSpec((tm, d), lambda i: (i, 0)),
                  pl.BlockSpec((tm, 1), lambda i: (i, 0)),
                  row(d), modspec(0), modspec(1),
                  row(d), modspec(0), modspec(1),
                  pl.BlockSpec((d, nq), lambda i: (0, 0)),
                  pl.BlockSpec((d, nkv), lambda i: (0, 0)),
                  row(LANES), row(LANES), row(LANES)],
        out_specs=(pl.BlockSpec((tm, nq), lambda i: (i, 0)),
                   pl.BlockSpec((tm, LANES), lambda i: (i, 0)),
                   pl.BlockSpec((tm, LANES), lambda i: (i, 0))),
        compiler_params=_cparams(("parallel",)),
        name="swa_qkv_proj",
    )(x2, pos2, gq, mod4, mod4, gkv, kvmod4, kvmod4, wq_bf, wkv_bf, qg2, kg2, invf)


def _attn_body(sink_ref, q_ref, kc_ref, kp_ref, vc_ref, vp_ref, o_ref):
    n = pl.program_id(1)
    w = WINDOW
    qi = lax.broadcasted_iota(jnp.int32, (w, 2 * w), 0) + w
    kj = lax.broadcasted_iota(jnp.int32, (w, 2 * w), 1)
    band = jnp.logical_and(kj <= qi, qi - kj < w)
    mask = jnp.logical_and(band, jnp.logical_or(n > 0, kj >= w))
    hd = B_HEAD_DIM
    lane = lax.broadcasted_iota(jnp.int32, (2 * w, LANES), 1)
    first = lane < hd
    kfull = jnp.concatenate([kp_ref[...], kc_ref[...]], axis=0).astype(F32)
    vfull = jnp.concatenate([vp_ref[...], vc_ref[...]], axis=0).astype(F32)
    placed = {}
    for g in range(B_KV_HEADS):
        for name, full in (("k", kfull), ("v", vfull)):
            swapped = pltpu.roll(full, hd, 1)
            own_first = full if g == 0 else swapped
            own_second = swapped if g == 0 else full
            placed[name, g, 0] = jnp.where(first, own_first, 0.0).astype(BF16)
            placed[name, g, 1] = jnp.where(first, 0.0, own_second).astype(BF16)
    pairs = B_Q_HEADS // 2
    heads = [(p, side) for p in range(pairs) for side in range(2)]
    scores = []
    for p, side in heads:
        g = (2 * p) // B_GROUP
        q_pair = q_ref[:, p * LANES:(p + 1) * LANES]
        s = lax.dot_general(q_pair, placed["k", g, side], (((1,), (1,)), ((), ())), preferred_element_type=F32)
        scores.append(jnp.where(mask, s, NEG_INF))
    probs, denoms = [], []
    for (p, side), s in zip(heads, scores):
        sink = sink_ref[0, 2 * p + side]
        m = jnp.maximum(jnp.max(jnp.maximum(s[:, :w], s[:, w:]), axis=-1, keepdims=True), sink)
        e = jnp.exp(s - m)
        denoms.append(jnp.sum(e[:, :w] + e[:, w:], axis=-1, keepdims=True) + jnp.exp(sink - m))
        probs.append(e.astype(BF16))
    outs = []
    for (p, side), pr, den in zip(heads, probs, denoms):
        g = (2 * p) // B_GROUP
        outs.append(jnp.dot(pr, placed["v", g, side], preferred_element_type=F32) / den)
    for p in range(pairs):
        o_ref[:, p * LANES:(p + 1) * LANES] = (outs[2 * p] + outs[2 * p + 1]).astype(BF16)


def _attention(q, k, v, sinks2, bsz, seq):
    t, nq = q.shape
    w = WINDOW
    nb = seq // w
    cur = lambda b, n: (b * nb + n, 0)
    prev = lambda b, n: (b * nb + jnp.maximum(n - 1, 0), 0)
    return pl.pallas_call(
        _attn_body,
        out_shape=jax.ShapeDtypeStruct((t, nq), BF16),
        grid=(bsz, nb),
        in_specs=[pl.BlockSpec(memory_space=pltpu.SMEM),
                  pl.BlockSpec((w, nq), cur),
                  pl.BlockSpec((w, LANES), cur), pl.BlockSpec((w, LANES), prev),
                  pl.BlockSpec((w, LANES), cur), pl.BlockSpec((w, LANES), prev)],
        out_specs=pl.BlockSpec((w, nq), cur),
        compiler_params=_cparams(("parallel", "parallel")),
        name="swa_sink_attention",
    )(sinks2, q, k, k, v, v)


def kernel(x, c, positions, ada_w, ada_b, norm_gain, a_w_in, a_conv, a_log, a_dt_bias, a_out_gain, a_w_out,
           kv_ada_w, kv_ada_b, kv_norm_gain, kv_w, k_norm_gain, b_w_q, q_norm_gain, b_sinks, b_w_out,
           router_w, router_b, up_w, up_b, down_w, down_b):
    bsz, seq, d = x.shape
    t = bsz * seq
    depth = ada_w.shape[0]
    x2 = x.reshape(t, d)
    ada_b3 = ada_b.reshape(depth, 1, 6 * d)

    for layer in range(depth):
        mod4 = _modulation(c, ada_w, ada_b3, layer).reshape(bsz, 6, 1, d)
        gain1 = norm_gain[layer, 0].reshape(1, d)
        gain2 = norm_gain[layer, 1].reshape(1, d)
        if layer < N_A_LAYERS:
            w_in = jnp.pad(a_w_in[layer].astype(BF16), ((0, 0), (0, A_PROJ_PAD - a_w_in.shape[-1])))
            proj = _in_proj(x2, gain1, mod4, w_in, seq)
            o = _gdn(proj, a_conv[layer], a_log[layer].reshape(1, A_HEADS),
                     a_dt_bias[layer].reshape(1, A_HEADS), a_out_gain[layer].reshape(1, A_HEAD_DIM),
                     bsz, seq)
            x2 = _out_proj_residual(o, a_w_out[layer].astype(BF16), x2, mod4, 2, seq)
        else:
            j = layer - N_A_LAYERS
            kvmod4 = _modulation(c, kv_ada_w.reshape(1, d, 2 * d), kv_ada_b.reshape(1, 1, 2 * d), 0)
            kvmod4 = kvmod4.reshape(bsz, 2, 1, d)
            half = ROPE_DIM // 2
            inv_freq = ROPE_THETA ** (-np.arange(0, ROPE_DIM, 2, dtype=np.float32) / ROPE_DIM)
            invf = jnp.asarray(np.tile(inv_freq.astype(np.float32), LANES // half).reshape(1, LANES))
            q, k, v = _qkv(x2, positions.reshape(t, 1), gain1, mod4, kv_norm_gain.reshape(1, d), kvmod4,
                           b_w_q[j].astype(BF16), kv_w.astype(BF16),
                           jnp.tile(q_norm_gain[j], 2).reshape(1, LANES),
                           jnp.tile(k_norm_gain, 2).reshape(1, LANES), invf, seq)
            o = _attention(q, k, v, b_sinks[j].reshape(1, B_Q_HEADS), bsz, seq)
            x2 = _out_proj_residual(o, b_w_out[j].astype(BF16), x2, mod4, 2, seq)
        x2 = _moe_layer(x2, gain2, mod4, router_w, router_b, up_w, up_b, down_w, down_b, layer, seq)
    return x2.reshape(bsz, seq, d)
```

```python
import dataclasses
import functools

import jax
import jax.numpy as jnp
import numpy as np
from jax import lax
from jax.experimental import pallas as pl
from jax.experimental.pallas import tpu as pltpu
from jax.experimental.pallas import tpu_sc as plsc

F32 = jnp.float32
BF16 = jnp.bfloat16
HIGHEST = lax.Precision.HIGHEST

D_MODEL = 1024
N_A_LAYERS = 1

A_HEAD_DIM = 128
A_HEADS = 8
A_WIDTH = 1024
A_CONV = 4
A_CHUNK = 64
A_PROJ_PAD = 4608

B_HEAD_DIM = 64
B_Q_HEADS = 16
B_KV_HEADS = 2
B_GROUP = 8
WINDOW = 128
ROPE_DIM = 16
ROPE_THETA = 500000.0

N_EXPERTS = 32
TOP_K = 4
D_FF = 1024
SWIGLU_LIMIT = 7.0
SWIGLU_ALPHA = 1.702
MOE_ROWS = 512

EPS = 1e-6
LANES = 128
NEG_INF = float("-inf")

VMEM_LIMIT = 56 * 1024 * 1024


def _cparams(sem, vmem=VMEM_LIMIT):
    return pltpu.CompilerParams(dimension_semantics=sem, vmem_limit_bytes=vmem)


def _silu(x):
    return x * jax.nn.sigmoid(x)


HI_HALF = -65536


def _pack_bf16_pairs(x):
    n = x.shape[-1] // 2
    bits = lax.bitcast_convert_type(x.astype(BF16).astype(F32), jnp.int32)
    return jnp.bitwise_or(lax.shift_right_logical(bits[:, :n], 16), jnp.bitwise_and(bits[:, n:], HI_HALF))


def _unpack_bf16_pairs(w):
    lo = lax.bitcast_convert_type(lax.shift_left(w, 16), F32)
    hi = lax.bitcast_convert_type(jnp.bitwise_and(w, HI_HALF), F32)
    return lo, hi


def _ada_norm(x, gain, shift, scale):
    y = x * lax.rsqrt(jnp.mean(x * x, axis=-1, keepdims=True) + EPS)
    return (y * gain) * (1.0 + scale) + shift


def _mod_body(c_ref, w_ref, b_ref, o_ref):
    o_ref[...] = jnp.dot(_silu(c_ref[...]), w_ref[...], preferred_element_type=F32,
                         precision=HIGHEST) + b_ref[...]


def _modulation(c, w3, b3, layer):
    bsz, d = c.shape
    n = w3.shape[-1]
    tn = 1024
    return pl.pallas_call(
        _mod_body,
        out_shape=jax.ShapeDtypeStruct((bsz, n), F32),
        grid=(n // tn,),
        in_specs=[pl.BlockSpec((bsz, d), lambda j: (0, 0)),
                  pl.BlockSpec((None, d, tn), lambda j: (layer, 0, j)),
                  pl.BlockSpec((None, 1, tn), lambda j: (layer, 0, j))],
        out_specs=pl.BlockSpec((bsz, tn), lambda j: (0, j)),
        compiler_params=_cparams(("arbitrary",)),
        name="adaln_mod",
    )(c, w3, b3)


def _inproj_body(x_ref, gain_ref, sh_ref, sc_ref, w_ref, o_ref):
    h = _ada_norm(x_ref[...], gain_ref[...], sh_ref[...], sc_ref[...]).astype(BF16)
    n = o_ref.shape[-1]
    for n0 in range(0, n, 512):
        o_ref[:, n0:n0 + 512] = jnp.dot(h, w_ref[:, n0:n0 + 512], preferred_element_type=F32)


def _in_proj(x2, gain, mod4, w_bf, seq):
    t, d = x2.shape
    n = w_bf.shape[-1]
    tm = 256
    tps = seq // tm
    return pl.pallas_call(
        _inproj_body,
        out_shape=jax.ShapeDtypeStruct((t, n), F32),
        grid=(t // tm,),
        in_specs=[pl.BlockSpec((tm, d), lambda i: (i, 0)),
                  pl.BlockSpec((1, d), lambda i: (0, 0)),
                  pl.BlockSpec((None, None, 1, d), lambda i: (i // tps, 0, 0, 0)),
                  pl.BlockSpec((None, None, 1, d), lambda i: (i // tps, 1, 0, 0)),
                  pl.BlockSpec((d, n), lambda i: (0, 0))],
        out_specs=pl.BlockSpec((tm, n), lambda i: (i, 0)),
        compiler_params=_cparams(("parallel",)),
        name="gdn_in_proj",
    )(x2, gain, mod4, mod4, w_bf)


GDN_HG = 2
GDN_RB = 256
GDN_SC = 128
GDN_SUB = 4


def _split_bf16(x):
    hi = x.astype(BF16)
    return hi, (x - hi.astype(F32)).astype(BF16)


def _mm3(a_hi, a_lo, b_hi, b_lo, dims=(((1,), (0,)), ((), ()))):
    dg = functools.partial(lax.dot_general, dimension_numbers=dims, preferred_element_type=F32)
    return dg(a_hi, b_hi) + (dg(a_lo, b_hi) + dg(a_hi, b_lo))


def _dot3(a, b, dims):
    return _mm3(*_split_bf16(a), *_split_bf16(b), dims)


def _gdn_body(alog_ref, dtb_ref, q_ref, k_ref, v_ref, z_ref, ab_ref, cq_ref, ck_ref, cv_ref, og_ref,
              o_ref, qs, ks, vs, bs, gcs, mp, bc, qp, op):
    seq = q_ref.shape[0]
    hd = A_HEAD_DIM
    ch = A_CHUNK
    n_chunks = seq // ch
    g_idx = pl.program_id(1)

    lane = lax.broadcasted_iota(jnp.int32, (GDN_RB, hd), 1)
    row = lax.broadcasted_iota(jnp.int32, (GDN_RB, hd), 0)
    row_in_chunk = row % ch

    def conv_silu(x_ref, c_ref, col0, r, start):
        main = x_ref[pl.ds(start, GDN_RB), col0:col0 + hd]
        pstart = pl.multiple_of(jnp.maximum(start - 8, 0), 8)
        prev = x_ref[pl.ds(pstart, 8), col0:col0 + hd] * jnp.where(r > 0, 1.0, 0.0)
        ext = jnp.concatenate([prev, main], axis=0)
        acc = main * c_ref[A_CONV - 1:A_CONV, col0:col0 + hd]
        for s in range(1, A_CONV):
            shifted = pltpu.roll(ext, s, 0)[8:8 + GDN_RB]
            acc = acc + shifted * c_ref[A_CONV - 1 - s:A_CONV - s, col0:col0 + hd]
        return _silu(acc)

    def l2n(x):
        return x * lax.rsqrt(jnp.sum(x * x, axis=-1, keepdims=True) + EPS)

    def prep(r, carry):
        start = pl.multiple_of(r * GDN_RB, GDN_RB)
        ab = ab_ref[pl.ds(start, GDN_RB), :]
        for hh in range(GDN_HG):
            col0 = hh * hd
            head = g_idx * GDN_HG + hh
            qs[hh, pl.ds(start, GDN_RB), :] = l2n(conv_silu(q_ref, cq_ref, col0, r, start)) * (hd ** -0.5)
            ks[hh, pl.ds(start, GDN_RB), :] = l2n(conv_silu(k_ref, ck_ref, col0, r, start))
            vs[hh, pl.ds(start, GDN_RB), :] = conv_silu(v_ref, cv_ref, col0, r, start)
            a_col = jnp.sum(jnp.where(lane == head, ab, 0.0), axis=-1, keepdims=True)
            b_col = jnp.sum(jnp.where(lane == head + A_HEADS, ab, 0.0), axis=-1, keepdims=True)
            bs[hh, pl.ds(start, GDN_RB), :] = jnp.broadcast_to(jax.nn.sigmoid(b_col), (GDN_RB, hd))
            xa = a_col + dtb_ref[0, head]
            softplus = jnp.maximum(xa, 0.0) + jnp.log1p(jnp.exp(-jnp.abs(xa)))
            g = jnp.broadcast_to(-jnp.exp(alog_ref[0, head]) * softplus, (GDN_RB, hd))
            s = 1
            while s < ch:
                g = g + jnp.where(row_in_chunk >= s, pltpu.roll(g, s, 0), 0.0)
                s *= 2
            gcs[hh, pl.ds(start, GDN_RB), :] = g
        return carry

    lax.fori_loop(0, seq // GDN_RB, prep, 0)

    sc_rows = GDN_SC
    per_sc = sc_rows // ch
    ri = lax.broadcasted_iota(jnp.int32, (sc_rows, sc_rows), 0)
    cj = lax.broadcasted_iota(jnp.int32, (sc_rows, sc_rows), 1)
    same_chunk = (ri // ch) == (cj // ch)
    incl = jnp.logical_and(same_chunk, ri >= cj)
    strict = jnp.logical_and(same_chunk, ri > cj)
    eye = jnp.where(ri == cj, 1.0, 0.0)
    chunk_of_col = lax.broadcasted_iota(jnp.int32, (hd, sc_rows), 1) // ch
    lanes_nt = (((1,), (1,)), ((), ()))
    plain = (((1,), (0,)), ((), ()))

    def local(i, carry):
        chains = [(i * GDN_SUB + sub, hh) for sub in range(GDN_SUB) for hh in range(GDN_HG)]
        rows_of = lambda blk: pl.ds(pl.multiple_of(blk * sc_rows, sc_rows), sc_rows)

        def start(blk, hh):
            rows = rows_of(blk)
            k = ks[hh, rows, :]
            gc = gcs[hh, rows, :]
            decay = jnp.where(incl, jnp.exp(jnp.minimum(gc - gc.T[0:1, :], 0.0)), 0.0)
            k_bf = k.astype(BF16)
            kk = lax.dot_general((k * bs[hh, rows, :]).astype(BF16), k_bf, lanes_nt, preferred_element_type=F32)
            qk = lax.dot_general(qs[hh, rows, :].astype(BF16), k_bf, lanes_nt, preferred_element_type=F32)
            return jnp.where(strict, -(kk * decay), 0.0), (qk * decay).astype(BF16)

        started = [start(*c) for c in chains]
        qks = [s[1] for s in started]
        invs = [eye + s[0] for s in started]
        pws = [_dot3(s[0], s[0], plain) for s in started]
        for step in range(1, 6):
            for n in range(len(chains)):
                pw_hi, pw_lo = _split_bf16(pws[n])
                inv_hi, inv_lo = _split_bf16(invs[n])
                if step < 5:
                    prod = _mm3(jnp.concatenate([pw_hi, inv_hi], axis=0),
                                jnp.concatenate([pw_lo, inv_lo], axis=0), pw_hi, pw_lo)
                    pws[n] = prod[:sc_rows]
                    invs[n] = invs[n] + prod[sc_rows:]
                else:
                    invs[n] = invs[n] + _mm3(inv_hi, inv_lo, pw_hi, pw_lo)

        def solve(n, blk, hh):
            rows = rows_of(blk)
            beta = bs[hh, rows, :]
            rhs = jnp.concatenate([vs[hh, rows, :] * beta, ks[hh, rows, :] * beta * jnp.exp(gcs[hh, rows, :])], axis=1)
            return _dot3(invs[n], rhs, plain).astype(BF16)

        uws = [solve(n, *c) for n, c in enumerate(chains)]

        def finish(n, blk, hh):
            rows = rows_of(blk)
            k = ks[hh, rows, :]
            gc = gcs[hh, rows, :]
            res = jnp.dot(qks[n], uws[n], preferred_element_type=F32)
            op[hh, rows, :] = res[:, :hd]
            qp[hh, rows, :] = (qs[hh, rows, :] * jnp.exp(gc) - res[:, hd:]).astype(BF16)
            gl = jnp.concatenate(
                [jnp.broadcast_to(gc[(j + 1) * ch - 1:(j + 1) * ch, :], (ch, hd)) for j in range(per_sc)], axis=0)
            kt_t = (k * jnp.exp(gl - gc)).T
            for j in range(per_sc):
                kt_j = jnp.where(chunk_of_col == j, kt_t, 0.0).astype(BF16)
                bm = jnp.dot(kt_j, uws[n], preferred_element_type=F32)
                bc[hh, blk * per_sc + j] = bm[:, :hd]
                mp[hh, blk * per_sc + j] = bm[:, hd:].astype(BF16)

        for n, c in enumerate(chains):
            finish(n, *c)
        return carry

    lax.fori_loop(0, seq // (sc_rows * GDN_SUB), local, 0)

    def scan(c, states):
        r0 = pl.multiple_of(c * ch, ch)
        rows = pl.ds(r0, ch)
        new_states = []
        for hh in range(GDN_HG):
            col0 = hh * hd
            state = states[hh]
            s_bf = state.astype(BF16)
            o = jnp.dot(qp[hh, rows, :], s_bf, preferred_element_type=F32) + op[hh, rows, :]
            g_tot = jnp.exp(gcs[hh, pl.ds(r0 + ch - 1, 1), :])
            new_states.append(state * g_tot - jnp.dot(mp[hh, c], s_bf, preferred_element_type=F32) + bc[hh, c])
            on = o * lax.rsqrt(jnp.mean(o * o, axis=-1, keepdims=True) + EPS) * og_ref[...]
            z = z_ref[rows, col0:col0 + hd]
            o_ref[rows, col0:col0 + hd] = (on * _silu(z)).astype(BF16)
        return tuple(new_states)

    lax.fori_loop(0, n_chunks, scan, tuple(jnp.zeros((hd, hd), F32) for _ in range(GDN_HG)))


def _gdn(proj, conv_w, a_log, dt_bias, out_gain, bsz, seq):
    t = proj.shape[0]
    hd = A_HEAD_DIM
    wb = hd * GDN_HG
    ng = A_HEADS // GDN_HG
    per = A_WIDTH // wb
    n_chunks = seq // A_CHUNK
    smem = pl.BlockSpec(memory_space=pltpu.SMEM)
    seq_spec = lambda off: pl.BlockSpec((seq, wb), lambda b, g, off=off: (b, off + g))
    conv_spec = lambda off: pl.BlockSpec((A_CONV, wb), lambda b, g, off=off: (0, off + g))
    sc = lambda dt=F32: pltpu.VMEM((GDN_HG, seq, hd), dt)
    return pl.pallas_call(
        _gdn_body,
        out_shape=jax.ShapeDtypeStruct((t, A_WIDTH), BF16),
        grid=(bsz, ng),
        in_specs=[smem, smem,
                  seq_spec(0), seq_spec(per), seq_spec(2 * per), seq_spec(3 * per),
                  pl.BlockSpec((seq, LANES), lambda b, g: (b, 4 * A_WIDTH // LANES)),
                  conv_spec(0), conv_spec(per), conv_spec(2 * per),
                  pl.BlockSpec((1, hd), lambda b, g: (0, 0))],
        out_specs=pl.BlockSpec((seq, wb), lambda b, g: (b, g)),
        scratch_shapes=[sc(), sc(), sc(), sc(), sc(),
                        pltpu.VMEM((GDN_HG, n_chunks, hd, hd), BF16),
                        pltpu.VMEM((GDN_HG, n_chunks, hd, hd), F32),
                        sc(BF16), sc()],
        compiler_params=_cparams(("parallel", "parallel")),
        name="gdn_core",
    )(a_log, dt_bias, proj, proj, proj, proj, proj, conv_w, conv_w, conv_w, out_gain)


def _oproj_body(a_ref, w_ref, x_ref, g_ref, o_ref):
    y = jnp.dot(a_ref[...], w_ref[...], preferred_element_type=F32)
    o_ref[...] = x_ref[...] + g_ref[...] * y


def _out_proj_residual(a_bf, w_bf, x2, mod4, gate_slot, seq):
    t, d = x2.shape
    kdim = a_bf.shape[-1]
    tm = 512
    tps = seq // tm
    return pl.pallas_call(
        _oproj_body,
        out_shape=jax.ShapeDtypeStruct((t, d), F32),
        grid=(t // tm,),
        in_specs=[pl.BlockSpec((tm, kdim), lambda i: (i, 0)),
                  pl.BlockSpec((kdim, d), lambda i: (0, 0)),
                  pl.BlockSpec((tm, d), lambda i: (i, 0)),
                  pl.BlockSpec((None, None, 1, d), lambda i: (i // tps, gate_slot, 0, 0))],
        out_specs=pl.BlockSpec((tm, d), lambda i: (i, 0)),
        compiler_params=_cparams(("parallel",)),
        name="out_proj_residual",
    )(a_bf, w_bf, x2, mod4)


ROUTER_TM = 512


def _router_body(x_ref, gain_ref, sh_ref, sc_ref, rw_ref, rb_ref, h_ref, idx_ref, gate_ref, rank_ref,
                 cnt_ref, carry):
    i = pl.program_id(0)

    @pl.when(i == 0)
    def _():
        carry[...] = jnp.zeros_like(carry)

    tm = x_ref.shape[0]
    h = _ada_norm(x_ref[...], gain_ref[...], sh_ref[...], sc_ref[...])
    h_ref[...] = _pack_bf16_pairs(h)
    lane = lax.broadcasted_iota(jnp.int32, (tm, LANES), 1)
    logits = _dot3(h, rw_ref[...], (((1,), (0,)), ((), ()))) + rb_ref[...]
    logits = jnp.where(lane < N_EXPERTS, logits, NEG_INF)
    tops, sels = [], []
    work = logits
    for _ in range(TOP_K):
        m = jnp.max(work, axis=-1, keepdims=True)
        sel = jnp.min(jnp.where(work == m, lane, LANES), axis=-1, keepdims=True)
        work = jnp.where(lane == sel, NEG_INF, work)
        tops.append(m)
        sels.append(sel)
    exps = [jnp.exp(m - tops[0]) for m in tops]
    denom = exps[0] + exps[1] + exps[2] + exps[3]
    onehot = jnp.zeros((tm, LANES), F32)
    for sel in sels:
        onehot = onehot + (lane == sel).astype(F32)
    ti = lax.broadcasted_iota(jnp.int32, (tm, tm), 0)
    tj = lax.broadcasted_iota(jnp.int32, (tm, tm), 1)
    tri = (ti > tj).astype(BF16)
    before = jnp.dot(tri, onehot.astype(BF16), preferred_element_type=F32) + carry[...]
    idx_out = jnp.zeros((tm, LANES), jnp.int32)
    gate_out = jnp.zeros((tm, LANES), F32)
    rank_out = jnp.zeros((tm, LANES), jnp.int32)
    for k in range(TOP_K):
        rank_k = jnp.sum(jnp.where(lane == sels[k], before, 0.0), axis=-1, keepdims=True)
        idx_out = jnp.where(lane == k, sels[k], idx_out)
        gate_out = jnp.where(lane == k, exps[k] / denom, gate_out)
        rank_out = jnp.where(lane == k, rank_k.astype(jnp.int32), rank_out)
    idx_ref[...] = idx_out
    gate_ref[...] = gate_out
    rank_ref[...] = rank_out
    carry[...] = carry[...] + jnp.sum(onehot, axis=0, keepdims=True)
    cnt_ref[...] = carry[...]


def _router(x2, gain, mod4, sh_slot, sc_slot, rw_pad, rb_pad, seq):
    t, d = x2.shape
    tm = ROUTER_TM
    tps = seq // tm
    tok = lambda dt: jax.ShapeDtypeStruct((t, LANES), dt)
    tok_spec = pl.BlockSpec((tm, LANES), lambda i: (i, 0))
    return pl.pallas_call(
        _router_body,
        out_shape=(jax.ShapeDtypeStruct((t, d // 2), jnp.int32), tok(jnp.int32), tok(F32), tok(jnp.int32),
                   jax.ShapeDtypeStruct((1, LANES), F32)),
        grid=(t // tm,),
        in_specs=[pl.BlockSpec((tm, d), lambda i: (i, 0)),
                  pl.BlockSpec((1, d), lambda i: (0, 0)),
                  pl.BlockSpec((None, None, 1, d), lambda i: (i // tps, sh_slot, 0, 0)),
                  pl.BlockSpec((None, None, 1, d), lambda i: (i // tps, sc_slot, 0, 0)),
                  pl.BlockSpec((d, LANES), lambda i: (0, 0)),
                  pl.BlockSpec((1, LANES), lambda i: (0, 0))],
        out_specs=(pl.BlockSpec((tm, d // 2), lambda i: (i, 0)), tok_spec, tok_spec, tok_spec,
                   pl.BlockSpec((1, LANES), lambda i: (0, 0))),
        scratch_shapes=[pltpu.VMEM((1, LANES), F32)],
        compiler_params=_cparams(("arbitrary",)),
        name="moe_router",
    )(x2, gain, mod4, mod4, rw_pad, rb_pad)


SC_LANES = 16
SC_INDEX_CHUNK = 8192


def _sc_row_tokens(dest_flat, n_rows, n_tok):
    n_assign = dest_flat.shape[0]
    assert n_rows % SC_LANES == 0 and n_assign % SC_INDEX_CHUNK == 0
    n_cores = plsc.get_sparse_core_info().num_cores
    mesh = plsc.VectorSubcoreMesh(core_axis_name="c", subcore_axis_name="s")

    @functools.partial(
        pl.kernel, mesh=mesh,
        out_type=jax.ShapeDtypeStruct((n_rows,), jnp.int32),
        scratch_types=[pltpu.VMEM((n_rows,), jnp.int32), pltpu.VMEM((SC_INDEX_CHUNK,), jnp.int32)],
        compiler_params=dataclasses.replace(pltpu.CompilerParams(), needs_layout_passes=False),
    )
    def row_token_kernel(dest_hbm, out_hbm, rt_v, d_v):
        wid = lax.axis_index("s") * n_cores + lax.axis_index("c")

        @pl.when(wid == 0)
        def _():
            lanes = lax.iota(jnp.int32, SC_LANES)

            @pl.loop(0, n_rows // SC_LANES)
            def _(i):
                rt_v[pl.ds(i * SC_LANES, SC_LANES)] = lax.rem(lanes + i * SC_LANES, n_tok)

            @pl.loop(0, n_assign // SC_INDEX_CHUNK)
            def _(c):
                pltpu.sync_copy(dest_hbm.at[pl.ds(c * SC_INDEX_CHUNK, SC_INDEX_CHUNK)], d_v)

                @pl.loop(0, SC_INDEX_CHUNK // SC_LANES)
                def _(i):
                    idx = d_v[pl.ds(i * SC_LANES, SC_LANES)]
                    tok = lax.div(lanes + (c * SC_INDEX_CHUNK + i * SC_LANES), TOP_K)
                    plsc.store_scatter(rt_v, [idx], tok)

            pltpu.sync_copy(rt_v, out_hbm)

    return row_token_kernel(dest_flat)


def _ffn_body(be_ref, nu_ref, x_ref, uw_ref, ub_ref, dw_ref, db_ref, y_ref, uw_bf, dw_bf):
    i = pl.program_id(0)
    changed = jnp.logical_or(i == 0, be_ref[i] != be_ref[jnp.maximum(i - 1, 0)])
    active = i < nu_ref[0]

    @pl.when(jnp.logical_and(changed, active))
    def _():
        rows = 128
        for r0 in range(0, uw_ref.shape[0], rows):
            uw_bf[r0:r0 + rows, :] = uw_ref[r0:r0 + rows, :].astype(BF16)
        for r0 in range(0, dw_ref.shape[0], rows):
            dw_bf[r0:r0 + rows, :] = dw_ref[r0:r0 + rows, :].astype(BF16)

    @pl.when(active)
    def _():
        x = jnp.concatenate(_unpack_bf16_pairs(x_ref[...]), axis=1).astype(BF16)
        gu = jnp.dot(x, uw_bf[...], preferred_element_type=F32) + ub_ref[...]
        gate = jnp.minimum(gu[:, :D_FF], SWIGLU_LIMIT)
        lin = jnp.clip(gu[:, D_FF:], -SWIGLU_LIMIT, SWIGLU_LIMIT)
        act = gate * jax.nn.sigmoid(SWIGLU_ALPHA * gate) * (lin + 1.0)
        y = jnp.dot(act.astype(BF16), dw_bf[...], preferred_element_type=F32) + db_ref[...]
        y_ref[...] = _pack_bf16_pairs(y)

    @pl.when(jnp.logical_not(active))
    def _():
        y_ref[...] = jnp.zeros_like(y_ref)


def _expert_ffn(xb, blk_expert, n_used, up_w, up_b4, down_w, down_b4, layer):
    n_rows, dp = xb.shape
    d = 2 * dp
    bm = MOE_ROWS
    n_blk = n_rows // bm
    f2 = up_w.shape[-1]
    grid_spec = pltpu.PrefetchScalarGridSpec(
        num_scalar_prefetch=2,
        grid=(n_blk,),
        in_specs=[pl.BlockSpec((bm, dp), lambda i, be, nu: (i, 0)),
                  pl.BlockSpec((None, None, d, f2), lambda i, be, nu: (layer, be[i], 0, 0)),
                  pl.BlockSpec((None, None, 1, f2), lambda i, be, nu: (layer, be[i], 0, 0)),
                  pl.BlockSpec((None, None, f2 // 2, d), lambda i, be, nu: (layer, be[i], 0, 0)),
                  pl.BlockSpec((None, None, 1, d), lambda i, be, nu: (layer, be[i], 0, 0))],
        out_specs=pl.BlockSpec((bm, dp), lambda i, be, nu: (i, 0)),
        scratch_shapes=[pltpu.VMEM((d, f2), BF16), pltpu.VMEM((f2 // 2, d), BF16)],
    )
    return pl.pallas_call(
        _ffn_body,
        out_shape=jax.ShapeDtypeStruct((n_rows, dp), jnp.int32),
        grid_spec=grid_spec,
        compiler_params=_cparams(("arbitrary",)),
        name="moe_expert_ffn",
    )(blk_expert, n_used, xb, up_w, up_b4, down_w, down_b4)


SC_GATHER_ROWS = 64


def _sc_gather_rows(table, idx):
    n, d = idx.shape[0], table.shape[1]
    info = plsc.get_sparse_core_info()
    n_cores, n_sub = info.num_cores, info.num_subcores
    n_workers = n_cores * n_sub
    chunk = SC_GATHER_ROWS
    per_worker = n // n_workers
    n_chunks = per_worker // chunk
    assert n_chunks * chunk * n_workers == n and n_chunks % 2 == 0
    mesh = plsc.VectorSubcoreMesh(core_axis_name="c", subcore_axis_name="s")

    @functools.partial(
        pl.kernel, mesh=mesh,
        out_type=jax.ShapeDtypeStruct((n, d), table.dtype),
        scratch_types=[pltpu.VMEM((n_chunks, chunk), jnp.int32), pltpu.VMEM((2, chunk, d), table.dtype),
                       pltpu.SemaphoreType.DMA((2,))],
    )
    def gather_kernel(table_hbm, idx_hbm, out_hbm, idx_v, rows_v, sems):
        wid = lax.axis_index("s") * n_cores + lax.axis_index("c")
        base = wid * per_worker
        pltpu.sync_copy(idx_hbm.at[wid], idx_v)

        def gather(ci, slot):
            return pltpu.make_async_copy(table_hbm.at[idx_v.at[ci]], rows_v.at[slot], sems.at[slot])

        gather(0, 0).start()

        @pl.loop(0, n_chunks, step=2)
        def _(c0):
            for slot in range(2):
                ci = c0 + slot
                gather(ci, slot).wait()

                @pl.when(ci + 1 < n_chunks)
                def _():
                    gather(ci + 1, 1 - slot).start()

                pltpu.sync_copy(rows_v.at[slot], out_hbm.at[pl.ds(base + ci * chunk, chunk)])

    return gather_kernel(table, idx.reshape(n_workers, n_chunks, chunk))


COMBINE_TM = 256


def _combine_body(gate_ref, x_ref, g_ref, y_ref, o_ref):
    gates = gate_ref[...]
    acc_lo = acc_hi = None
    for k in range(TOP_K):
        lo, hi = _unpack_bf16_pairs(y_ref[k])
        gk = gates[:, k:k + 1]
        acc_lo = gk * lo if acc_lo is None else acc_lo + gk * lo
        acc_hi = gk * hi if acc_hi is None else acc_hi + gk * hi
    o_ref[...] = x_ref[...] + g_ref[...] * jnp.concatenate([acc_lo, acc_hi], axis=1)


def _combine(y_kt, gates, x2, mod4, gate_slot, seq):
    t, d = x2.shape
    tm = COMBINE_TM
    tps = seq // tm
    return pl.pallas_call(
        _combine_body,
        out_shape=jax.ShapeDtypeStruct((t, d), F32),
        grid=(t // tm,),
        in_specs=[pl.BlockSpec((tm, LANES), lambda i: (i, 0)),
                  pl.BlockSpec((tm, d), lambda i: (i, 0)),
                  pl.BlockSpec((None, None, 1, d), lambda i: (i // tps, gate_slot, 0, 0)),
                  pl.BlockSpec((TOP_K, tm, d // 2), lambda i: (0, i, 0))],
        out_specs=pl.BlockSpec((tm, d), lambda i: (i, 0)),
        compiler_params=_cparams(("parallel",)),
        name="moe_combine",
    )(gates, x2, mod4, y_kt)


def _moe_layer(x2, gain, mod4, router_w, router_b, up_w, up_b, down_w, down_b, layer, seq):
    t, d = x2.shape
    rw_pad = jnp.pad(router_w[layer], ((0, 0), (0, LANES - N_EXPERTS)))
    rb_pad = jnp.pad(router_b[layer], (0, LANES - N_EXPERTS)).reshape(1, LANES)
    h2, idx, gates, rank, counts = _router(x2, gain, mod4, 3, 4, rw_pad, rb_pad, seq)

    bm = MOE_ROWS
    counts = counts[0, :N_EXPERTS].astype(jnp.int32)
    padded = (counts + bm - 1) // bm * bm
    p_ends = jnp.cumsum(padded)
    p_starts = p_ends - padded
    n_rows = t * TOP_K + N_EXPERTS * bm
    n_blk = n_rows // bm
    top_idx = idx[:, :TOP_K]
    onehot = top_idx[:, :, None] == jnp.arange(N_EXPERTS, dtype=jnp.int32)[None, None, :]
    dest = jnp.sum(jnp.where(onehot, p_starts[None, None, :], 0), axis=-1) + rank[:, :TOP_K]
    blk_start = jnp.arange(n_blk, dtype=jnp.int32) * bm
    blk_expert = jnp.sum((blk_start[:, None] >= p_ends[None, :]).astype(jnp.int32), axis=-1)
    blk_expert = jnp.minimum(blk_expert, N_EXPERTS - 1).astype(jnp.int32)
    n_used = (p_ends[-1:] // bm).astype(jnp.int32)

    xb = _sc_gather_rows(h2, _sc_row_tokens(dest.reshape(t * TOP_K), n_rows, t))
    f2 = up_w.shape[-1]
    y_rows = _expert_ffn(xb, blk_expert, n_used, up_w, up_b.reshape(up_b.shape[0], N_EXPERTS, 1, f2),
                         down_w, down_b.reshape(down_b.shape[0], N_EXPERTS, 1, d), layer)
    y_kt = _sc_gather_rows(y_rows, dest.T.reshape(TOP_K * t)).reshape(TOP_K, t, d // 2)
    return _combine(y_kt, gates, x2, mod4, 5, seq)


QKV_TM = 256


def _rope_tables(pos_col, invf_row):
    tm = pos_col.shape[0]
    ang = pos_col * invf_row
    c = jnp.cos(ang)
    s = jnp.sin(ang)
    d = lax.broadcasted_iota(jnp.int32, (tm, LANES), 1) % B_HEAD_DIM
    half = ROPE_DIM // 2
    cos_t = jnp.where(d < ROPE_DIM, c, 1.0)
    sin_lo = jnp.where(d < half, -s, 0.0)
    sin_hi = jnp.where(jnp.logical_and(d >= half, d < ROPE_DIM), s, 0.0)
    return cos_t, sin_lo, sin_hi


def _head_norm_rope(x, gain_row, tables):
    cos_t, sin_lo, sin_hi = tables
    lane = lax.broadcasted_iota(jnp.int32, x.shape, 1)
    sq = x * x
    first = lane < B_HEAD_DIM
    ss0 = jnp.sum(jnp.where(first, sq, 0.0), axis=-1, keepdims=True)
    ss1 = jnp.sum(jnp.where(first, 0.0, sq), axis=-1, keepdims=True)
    ms = jnp.where(first, ss0, ss1) * (1.0 / B_HEAD_DIM)
    xn = x * lax.rsqrt(ms + EPS) * gain_row
    half = ROPE_DIM // 2
    return (xn * cos_t + pltpu.roll(xn, LANES - half, 1) * sin_lo + pltpu.roll(xn, half, 1) * sin_hi)


def _qkv_body(x_ref, pos_ref, gq_ref, shq_ref, scq_ref, gkv_ref, shkv_ref, sckv_ref, wq_ref, wkv_ref,
              qg_ref, kg_ref, invf_ref, q_ref, k_ref, v_ref):
    x = x_ref[...]
    y = x * lax.rsqrt(jnp.mean(x * x, axis=-1, keepdims=True) + EPS)
    hq = ((y * gq_ref[...]) * (1.0 + scq_ref[...]) + shq_ref[...]).astype(BF16)
    hkv = ((y * gkv_ref[...]) * (1.0 + sckv_ref[...]) + shkv_ref[...]).astype(BF16)
    tables = _rope_tables(pos_ref[...].astype(F32), invf_ref[...])
    kv = jnp.dot(hkv, wkv_ref[...], preferred_element_type=F32)
    k_ref[...] = _head_norm_rope(kv[:, :LANES], kg_ref[...], tables).astype(BF16)
    v_ref[...] = kv[:, LANES:].astype(BF16)
    q = jnp.dot(hq, wq_ref[...], preferred_element_type=F32)
    scale = B_HEAD_DIM ** -0.5
    for p in range(q.shape[-1] // LANES):
        qp = _head_norm_rope(q[:, p * LANES:(p + 1) * LANES], qg_ref[...], tables)
        q_ref[:, p * LANES:(p + 1) * LANES] = (qp * scale).astype(BF16)


def _qkv(x2, pos2, gq, mod4, gkv, kvmod4, wq_bf, wkv_bf, qg2, kg2, invf, seq):
    t, d = x2.shape
    tm = QKV_TM
    tps = seq // tm
    nq = wq_bf.shape[-1]
    nkv = wkv_bf.shape[-1]
    row = lambda n: pl.BlockSpec((1, n), lambda i: (0, 0))
    modspec = lambda slot: pl.BlockSpec((None, None, 1, d), lambda i, slot=slot: (i // tps, slot, 0, 0))
    return pl.pallas_call(
        _qkv_body,
        out_shape=(jax.ShapeDtypeStruct((t, nq), BF16), jax.ShapeDtypeStruct((t, LANES), BF16),
                   jax.ShapeDtypeStruct((t, LANES), BF16)),
        grid=(t // tm,),
        in_specs=[pl.BlockSpec((tm, d), lambda i: (i, 0)),
                  pl.BlockSpec((tm, 1), lambda i: (i, 0)),
                  row(d), modspec(0), modspec(1),
                  row(d), modspec(0), modspec(1),
                  pl.BlockSpec((d, nq), lambda i: (0, 0)),
                  pl.BlockSpec((d, nkv), lambda i: (0, 0)),
                  row(LANES), row(LANES), row(LANES)],
        out_specs=(pl.BlockSpec((tm, nq), lambda i: (i, 0)),
                   pl.BlockSpec((tm, LANES), lambda i: (i, 0)),
                   pl.BlockSpec((tm, LANES), lambda i: (i, 0))),
        compiler_params=_cparams(("parallel",)),
        name="swa_qkv_proj",
    )(x2, pos2, gq, mod4, mod4, gkv, kvmod4, kvmod4, wq_bf, wkv_bf, qg2, kg2, invf)


def _attn_body(sink_ref, q_ref, kc_ref, kp_ref, vc_ref, vp_ref, o_ref):
    n = pl.program_id(1)
    w = WINDOW
    qi = lax.broadcasted_iota(jnp.int32, (w, 2 * w), 0) + w
    kj = lax.broadcasted_iota(jnp.int32, (w, 2 * w), 1)
    band = jnp.logical_and(kj <= qi, qi - kj < w)
    mask = jnp.logical_and(band, jnp.logical_or(n > 0, kj >= w))
    hd = B_HEAD_DIM
    lane = lax.broadcasted_iota(jnp.int32, (2 * w, LANES), 1)
    first = lane < hd
    kfull = jnp.concatenate([kp_ref[...], kc_ref[...]], axis=0).astype(F32)
    vfull = jnp.concatenate([vp_ref[...], vc_ref[...]], axis=0).astype(F32)
    placed = {}
    for g in range(B_KV_HEADS):
        for name, full in (("k", kfull), ("v", vfull)):
            swapped = pltpu.roll(full, hd, 1)
            own_first = full if g == 0 else swapped
            own_second = swapped if g == 0 else full
            placed[name, g, 0] = jnp.where(first, own_first, 0.0).astype(BF16)
            placed[name, g, 1] = jnp.where(first, 0.0, own_second).astype(BF16)
    pairs = B_Q_HEADS // 2
    heads = [(p, side) for p in range(pairs) for side in range(2)]
    scores = []
    for p, side in heads:
        g = (2 * p) // B_GROUP
        q_pair = q_ref[:, p * LANES:(p + 1) * LANES]
        s = lax.dot_general(q_pair, placed["k", g, side], (((1,), (1,)), ((), ())), preferred_element_type=F32)
        scores.append(jnp.where(mask, s, NEG_INF))
    probs, denoms = [], []
    for (p, side), s in zip(heads, scores):
        sink = sink_ref[0, 2 * p + side]
        m = jnp.maximum(jnp.max(jnp.maximum(s[:, :w], s[:, w:]), axis=-1, keepdims=True), sink)
        e = jnp.exp(s - m)
        denoms.append(jnp.sum(e[:, :w] + e[:, w:], axis=-1, keepdims=True) + jnp.exp(sink - m))
        probs.append(e.astype(BF16))
    outs = []
    for (p, side), pr, den in zip(heads, probs, denoms):
        g = (2 * p) // B_GROUP
        outs.append(jnp.dot(pr, placed["v", g, side], preferred_element_type=F32) / den)
    for p in range(pairs):
        o_ref[:, p * LANES:(p + 1) * LANES] = (outs[2 * p] + outs[2 * p + 1]).astype(BF16)


def _attention(q, k, v, sinks2, bsz, seq):
    t, nq = q.shape
    w = WINDOW
    nb = seq // w
    cur = lambda b, n: (b * nb + n, 0)
    prev = lambda b, n: (b * nb + jnp.maximum(n - 1, 0), 0)
    return pl.pallas_call(
        _attn_body,
        out_shape=jax.ShapeDtypeStruct((t, nq), BF16),
        grid=(bsz, nb),
        in_specs=[pl.BlockSpec(memory_space=pltpu.SMEM),
                  pl.BlockSpec((w, nq), cur),
                  pl.BlockSpec((w, LANES), cur), pl.BlockSpec((w, LANES), prev),
                  pl.BlockSpec((w, LANES), cur), pl.BlockSpec((w, LANES), prev)],
        out_specs=pl.BlockSpec((w, nq), cur),
        compiler_params=_cparams(("parallel", "parallel")),
        name="swa_sink_attention",
    )(sinks2, q, k, k, v, v)


def kernel(x, c, positions, ada_w, ada_b, norm_gain, a_w_in, a_conv, a_log, a_dt_bias, a_out_gain, a_w_out,
           kv_ada_w, kv_ada_b, kv_norm_gain, kv_w, k_norm_gain, b_w_q, q_norm_gain, b_sinks, b_w_out,
           router_w, router_b, up_w, up_b, down_w, down_b):
    bsz, seq, d = x.shape
    t = bsz * seq
    depth = ada_w.shape[0]
    x2 = x.reshape(t, d)
    ada_b3 = ada_b.reshape(depth, 1, 6 * d)

    for layer in range(depth):
        mod4 = _modulation(c, ada_w, ada_b3, layer).reshape(bsz, 6, 1, d)
        gain1 = norm_gain[layer, 0].reshape(1, d)
        gain2 = norm_gain[layer, 1].reshape(1, d)
        if layer < N_A_LAYERS:
            w_in = jnp.pad(a_w_in[layer].astype(BF16), ((0, 0), (0, A_PROJ_PAD - a_w_in.shape[-1])))
            proj = _in_proj(x2, gain1, mod4, w_in, seq)
            o = _gdn(proj, a_conv[layer], a_log[layer].reshape(1, A_HEADS),
                     a_dt_bias[layer].reshape(1, A_HEADS), a_out_gain[layer].reshape(1, A_HEAD_DIM),
                     bsz, seq)
            x2 = _out_proj_residual(o, a_w_out[layer].astype(BF16), x2, mod4, 2, seq)
        else:
            j = layer - N_A_LAYERS
            kvmod4 = _modulation(c, kv_ada_w.reshape(1, d, 2 * d), kv_ada_b.reshape(1, 1, 2 * d), 0)
            kvmod4 = kvmod4.reshape(bsz, 2, 1, d)
            half = ROPE_DIM // 2
            inv_freq = ROPE_THETA ** (-np.arange(0, ROPE_DIM, 2, dtype=np.float32) / ROPE_DIM)
            invf = jnp.asarray(np.tile(inv_freq.astype(np.float32), LANES // half).reshape(1, LANES))
            q, k, v = _qkv(x2, positions.reshape(t, 1), gain1, mod4, kv_norm_gain.reshape(1, d), kvmod4,
                           b_w_q[j].astype(BF16), kv_w.astype(BF16),
                           jnp.tile(q_norm_gain[j], 2).reshape(1, LANES),
                           jnp.tile(k_norm_gain, 2).reshape(1, LANES), invf, seq)
            o = _attention(q, k, v, b_sinks[j].reshape(1, B_Q_HEADS), bsz, seq)
            x2 = _out_proj_residual(o, b_w_out[j].astype(BF16), x2, mod4, 2, seq)
        x2 = _moe_layer(x2, gain2, mod4, router_w, router_b, up_w, up_b, down_w, down_b, layer, seq)
    return x2.reshape(bsz, seq, d)
```

```python
import dataclasses
import functools

import jax
import jax.numpy as jnp
import numpy as np
from jax import lax
from jax.experimental import pallas as pl
from jax.experimental.pallas import tpu as pltpu
from jax.experimental.pallas import tpu_sc as plsc

F32 = jnp.float32
BF16 = jnp.bfloat16
HIGHEST = lax.Precision.HIGHEST

D_MODEL = 1024
N_A_LAYERS = 1

A_HEAD_DIM = 128
A_HEADS = 8
A_WIDTH = 1024
A_CONV = 4
A_CHUNK = 64
A_PROJ_PAD = 4608

B_HEAD_DIM = 64
B_Q_HEADS = 16
B_KV_HEADS = 2
B_GROUP = 8
WINDOW = 128
ROPE_DIM = 16
ROPE_THETA = 500000.0

N_EXPERTS = 32
TOP_K = 4
D_FF = 1024
SWIGLU_LIMIT = 7.0
SWIGLU_ALPHA = 1.702
MOE_ROWS = 512

EPS = 1e-6
LANES = 128
NEG_INF = float("-inf")

VMEM_LIMIT = 56 * 1024 * 1024


def _cparams(sem, vmem=VMEM_LIMIT):
    return pltpu.CompilerParams(dimension_semantics=sem, vmem_limit_bytes=vmem)


def _silu(x):
    return x * jax.nn.sigmoid(x)


HI_HALF = -65536


def _pack_bf16_pairs(x):
    n = x.shape[-1] // 2
    bits = lax.bitcast_convert_type(x.astype(BF16).astype(F32), jnp.int32)
    return jnp.bitwise_or(lax.shift_right_logical(bits[:, :n], 16), jnp.bitwise_and(bits[:, n:], HI_HALF))


def _unpack_bf16_pairs(w):
    lo = lax.bitcast_convert_type(lax.shift_left(w, 16), F32)
    hi = lax.bitcast_convert_type(jnp.bitwise_and(w, HI_HALF), F32)
    return lo, hi


def _ada_norm(x, gain, shift, scale):
    y = x * lax.rsqrt(jnp.mean(x * x, axis=-1, keepdims=True) + EPS)
    return (y * gain) * (1.0 + scale) + shift


def _mod_body(c_ref, w_ref, b_ref, o_ref):
    o_ref[...] = jnp.dot(_silu(c_ref[...]), w_ref[...], preferred_element_type=F32,
                         precision=HIGHEST) + b_ref[...]


def _modulation(c, w3, b3, layer):
    bsz, d = c.shape
    n = w3.shape[-1]
    tn = 1024
    return pl.pallas_call(
        _mod_body,
        out_shape=jax.ShapeDtypeStruct((bsz, n), F32),
        grid=(n // tn,),
        in_specs=[pl.BlockSpec((bsz, d), lambda j: (0, 0)),
                  pl.BlockSpec((None, d, tn), lambda j: (layer, 0, j)),
                  pl.BlockSpec((None, 1, tn), lambda j: (layer, 0, j))],
        out_specs=pl.BlockSpec((bsz, tn), lambda j: (0, j)),
        compiler_params=_cparams(("arbitrary",)),
        name="adaln_mod",
    )(c, w3, b3)


def _inproj_body(x_ref, gain_ref, sh_ref, sc_ref, w_ref, o_ref):
    h = _ada_norm(x_ref[...], gain_ref[...], sh_ref[...], sc_ref[...]).astype(BF16)
    n = o_ref.shape[-1]
    for n0 in range(0, n, 512):
        o_ref[:, n0:n0 + 512] = jnp.dot(h, w_ref[:, n0:n0 + 512], preferred_element_type=F32)


def _in_proj(x2, gain, mod4, w_bf, seq):
    t, d = x2.shape
    n = w_bf.shape[-1]
    tm = 256
    tps = seq // tm
    return pl.pallas_call(
        _inproj_body,
        out_shape=jax.ShapeDtypeStruct((t, n), F32),
        grid=(t // tm,),
        in_specs=[pl.BlockSpec((tm, d), lambda i: (i, 0)),
                  pl.BlockSpec((1, d), lambda i: (0, 0)),
                  pl.BlockSpec((None, None, 1, d), lambda i: (i // tps, 0, 0, 0)),
                  pl.BlockSpec((None, None, 1, d), lambda i: (i // tps, 1, 0, 0)),
                  pl.BlockSpec((d, n), lambda i: (0, 0))],
        out_specs=pl.BlockSpec((tm, n), lambda i: (i, 0)),
        compiler_params=_cparams(("parallel",)),
        name="gdn_in_proj",
    )(x2, gain, mod4, mod4, w_bf)


GDN_HG = 2
GDN_RB = 256
GDN_SC = 128
GDN_SUB = 4


def _split_bf16(x):
    hi = x.astype(BF16)
    return hi, (x - hi.astype(F32)).astype(BF16)


def _mm3(a_hi, a_lo, b_hi, b_lo, dims=(((1,), (0,)), ((), ()))):
    dg = functools.partial(lax.dot_general, dimension_numbers=dims, preferred_element_type=F32)
    return dg(a_hi, b_hi) + (dg(a_lo, b_hi) + dg(a_hi, b_lo))


def _dot3(a, b, dims):
    return _mm3(*_split_bf16(a), *_split_bf16(b), dims)


def _gdn_body(alog_ref, dtb_ref, q_ref, k_ref, v_ref, z_ref, ab_ref, cq_ref, ck_ref, cv_ref, og_ref,
              o_ref, qs, ks, vs, bs, gcs, mp, bc, qp, op):
    seq = q_ref.shape[0]
    hd = A_HEAD_DIM
    ch = A_CHUNK
    n_chunks = seq // ch
    g_idx = pl.program_id(1)

    lane = lax.broadcasted_iota(jnp.int32, (GDN_RB, hd), 1)
    row = lax.broadcasted_iota(jnp.int32, (GDN_RB, hd), 0)
    row_in_chunk = row % ch

    def conv_silu(x_ref, c_ref, col0, r, start):
        main = x_ref[pl.ds(start, GDN_RB), col0:col0 + hd]
        pstart = pl.multiple_of(jnp.maximum(start - 8, 0), 8)
        prev = x_ref[pl.ds(pstart, 8), col0:col0 + hd] * jnp.where(r > 0, 1.0, 0.0)
        ext = jnp.concatenate([prev, main], axis=0)
        acc = main * c_ref[A_CONV - 1:A_CONV, col0:col0 + hd]
        for s in range(1, A_CONV):
            shifted = pltpu.roll(ext, s, 0)[8:8 + GDN_RB]
            acc = acc + shifted * c_ref[A_CONV - 1 - s:A_CONV - s, col0:col0 + hd]
        return _silu(acc)

    def l2n(x):
        return x * lax.rsqrt(jnp.sum(x * x, axis=-1, keepdims=True) + EPS)

    sc_rows = GDN_SC
    per_sc = sc_rows // ch
    rb_per_step = sc_rows * GDN_SUB // GDN_RB

    def prep_stages(j):
        for rbi in range(rb_per_step):
            r = j * rb_per_step + rbi
            start = pl.multiple_of(r * GDN_RB, GDN_RB)
            rows = pl.ds(start, GDN_RB)
            for hh in range(GDN_HG):
                col0 = hh * hd
                head = g_idx * GDN_HG + hh
                qs[hh, rows, :] = l2n(conv_silu(q_ref, cq_ref, col0, r, start)) * (hd ** -0.5)
                ks[hh, rows, :] = l2n(conv_silu(k_ref, ck_ref, col0, r, start))
                yield
                vs[hh, rows, :] = conv_silu(v_ref, cv_ref, col0, r, start)
                ab = ab_ref[rows, :]
                a_col = jnp.sum(jnp.where(lane == head, ab, 0.0), axis=-1, keepdims=True)
                b_col = jnp.sum(jnp.where(lane == head + A_HEADS, ab, 0.0), axis=-1, keepdims=True)
                bs[hh, rows, :] = jnp.broadcast_to(jax.nn.sigmoid(b_col), (GDN_RB, hd))
                xa = a_col + dtb_ref[0, head]
                softplus = jnp.maximum(xa, 0.0) + jnp.log1p(jnp.exp(-jnp.abs(xa)))
                g = jnp.broadcast_to(-jnp.exp(alog_ref[0, head]) * softplus, (GDN_RB, hd))
                s = 1
                while s < ch:
                    g = g + jnp.where(row_in_chunk >= s, pltpu.roll(g, s, 0), 0.0)
                    s *= 2
                gcs[hh, rows, :] = g
                yield
    ri = lax.broadcasted_iota(jnp.int32, (sc_rows, sc_rows), 0)
    cj = lax.broadcasted_iota(jnp.int32, (sc_rows, sc_rows), 1)
    same_chunk = (ri // ch) == (cj // ch)
    incl = jnp.logical_and(same_chunk, ri >= cj)
    strict = jnp.logical_and(same_chunk, ri > cj)
    eye = jnp.where(ri == cj, 1.0, 0.0)
    chunk_of_col = lax.broadcasted_iota(jnp.int32, (hd, sc_rows), 1) // ch
    lanes_nt = (((1,), (1,)), ((), ()))
    plain = (((1,), (0,)), ((), ()))

    chunks_per_step = GDN_SUB * per_sc
    n_steps = seq // (sc_rows * GDN_SUB)

    def local_stages(i):
        chains = [(i * GDN_SUB + sub, hh) for sub in range(GDN_SUB) for hh in range(GDN_HG)]
        rows_of = lambda blk: pl.ds(pl.multiple_of(blk * sc_rows, sc_rows), sc_rows)

        def start(blk, hh):
            rows = rows_of(blk)
            k = ks[hh, rows, :]
            gc = gcs[hh, rows, :]
            decay = jnp.where(incl, jnp.exp(jnp.minimum(gc - gc.T[0:1, :], 0.0)), 0.0)
            k_bf = k.astype(BF16)
            kk = lax.dot_general((k * bs[hh, rows, :]).astype(BF16), k_bf, lanes_nt, preferred_element_type=F32)
            qk = lax.dot_general(qs[hh, rows, :].astype(BF16), k_bf, lanes_nt, preferred_element_type=F32)
            return jnp.where(strict, -(kk * decay), 0.0), (qk * decay).astype(BF16)

        started = [start(*c) for c in chains]
        yield
        qks = [s[1] for s in started]
        invs = [eye + s[0] for s in started]
        pws = [_dot3(s[0], s[0], plain) for s in started]
        yield
        for step in range(1, 6):
            for n in range(len(chains)):
                pw_hi, pw_lo = _split_bf16(pws[n])
                inv_hi, inv_lo = _split_bf16(invs[n])
                if step < 5:
                    prod = _mm3(jnp.concatenate([pw_hi, inv_hi], axis=0),
                                jnp.concatenate([pw_lo, inv_lo], axis=0), pw_hi, pw_lo)
                    pws[n] = prod[:sc_rows]
                    invs[n] = invs[n] + prod[sc_rows:]
                else:
                    invs[n] = invs[n] + _mm3(inv_hi, inv_lo, pw_hi, pw_lo)
            yield

        def solve(n, blk, hh):
            rows = rows_of(blk)
            beta = bs[hh, rows, :]
            rhs = jnp.concatenate([vs[hh, rows, :] * beta, ks[hh, rows, :] * beta * jnp.exp(gcs[hh, rows, :])], axis=1)
            return _dot3(invs[n], rhs, plain).astype(BF16)

        uws = [solve(n, *c) for n, c in enumerate(chains)]
        yield

        def finish(n, blk, hh):
            rows = rows_of(blk)
            k = ks[hh, rows, :]
            gc = gcs[hh, rows, :]
            res = jnp.dot(qks[n], uws[n], preferred_element_type=F32)
            op[hh, rows, :] = res[:, :hd]
            qp[hh, rows, :] = (qs[hh, rows, :] * jnp.exp(gc) - res[:, hd:]).astype(BF16)
            gl = jnp.concatenate(
                [jnp.broadcast_to(gc[(j + 1) * ch - 1:(j + 1) * ch, :], (ch, hd)) for j in range(per_sc)], axis=0)
            kt_t = (k * jnp.exp(gl - gc)).T
            for j in range(per_sc):
                kt_j = jnp.where(chunk_of_col == j, kt_t, 0.0).astype(BF16)
                bm = jnp.dot(kt_j, uws[n], preferred_element_type=F32)
                bc[hh, blk * per_sc + j] = bm[:, :hd]
                mp[hh, blk * per_sc + j] = bm[:, hd:].astype(BF16)

        for n, c in enumerate(chains):
            finish(n, *c)

    def scan_stages(j, states):
        for cc in range(chunks_per_step):
            c = j * chunks_per_step + cc
            r0 = pl.multiple_of(c * ch, ch)
            rows = pl.ds(r0, ch)
            for hh in range(GDN_HG):
                col0 = hh * hd
                state = states[hh]
                s_bf = state.astype(BF16)
                o = jnp.dot(qp[hh, rows, :], s_bf, preferred_element_type=F32) + op[hh, rows, :]
                g_tot = jnp.exp(gcs[hh, pl.ds(r0 + ch - 1, 1), :])
                states[hh] = state * g_tot - jnp.dot(mp[hh, c], s_bf, preferred_element_type=F32) + bc[hh, c]
                on = o * lax.rsqrt(jnp.mean(o * o, axis=-1, keepdims=True) + EPS) * og_ref[...]
                z = z_ref[rows, col0:col0 + hd]
                o_ref[rows, col0:col0 + hd] = (on * _silu(z)).astype(BF16)
            yield

    def interleave(*gens):
        live = list(gens)
        while live:
            for g in list(live):
                try:
                    next(g)
                except StopIteration:
                    live.remove(g)

    assert n_steps >= 2
    interleave(prep_stages(0))
    interleave(local_stages(0), prep_stages(1))

    def pipelined(i, states):
        states = list(states)
        interleave(local_stages(i), scan_stages(i - 1, states), prep_stages(i + 1))
        return tuple(states)

    states = list(lax.fori_loop(1, n_steps - 1, pipelined,
                                tuple(jnp.zeros((hd, hd), F32) for _ in range(GDN_HG))))
    interleave(local_stages(n_steps - 1), scan_stages(n_steps - 2, states))
    interleave(scan_stages(n_steps - 1, states))


def _gdn(proj, conv_w, a_log, dt_bias, out_gain, bsz, seq):
    t = proj.shape[0]
    hd = A_HEAD_DIM
    wb = hd * GDN_HG
    ng = A_HEADS // GDN_HG
    per = A_WIDTH // wb
    n_chunks = seq // A_CHUNK
    smem = pl.BlockSpec(memory_space=pltpu.SMEM)
    seq_spec = lambda off: pl.BlockSpec((seq, wb), lambda b, g, off=off: (b, off + g))
    conv_spec = lambda off: pl.BlockSpec((A_CONV, wb), lambda b, g, off=off: (0, off + g))
    sc = lambda dt=F32: pltpu.VMEM((GDN_HG, seq, hd), dt)
    return pl.pallas_call(
        _gdn_body,
        out_shape=jax.ShapeDtypeStruct((t, A_WIDTH), BF16),
        grid=(bsz, ng),
        in_specs=[smem, smem,
                  seq_spec(0), seq_spec(per), seq_spec(2 * per), seq_spec(3 * per),
                  pl.BlockSpec((seq, LANES), lambda b, g: (b, 4 * A_WIDTH // LANES)),
                  conv_spec(0), conv_spec(per), conv_spec(2 * per),
                  pl.BlockSpec((1, hd), lambda b, g: (0, 0))],
        out_specs=pl.BlockSpec((seq, wb), lambda b, g: (b, g)),
        scratch_shapes=[sc(), sc(), sc(), sc(), sc(),
                        pltpu.VMEM((GDN_HG, n_chunks, hd, hd), BF16),
                        pltpu.VMEM((GDN_HG, n_chunks, hd, hd), F32),
                        sc(BF16), sc()],
        compiler_params=_cparams(("parallel", "parallel")),
        name="gdn_core",
    )(a_log, dt_bias, proj, proj, proj, proj, proj, conv_w, conv_w, conv_w, out_gain)


def _oproj_body(a_ref, w_ref, x_ref, g_ref, o_ref):
    y = jnp.dot(a_ref[...], w_ref[...], preferred_element_type=F32)
    o_ref[...] = x_ref[...] + g_ref[...] * y


def _out_proj_residual(a_bf, w_bf, x2, mod4, gate_slot, seq):
    t, d = x2.shape
    kdim = a_bf.shape[-1]
    tm = 512
    tps = seq // tm
    return pl.pallas_call(
        _oproj_body,
        out_shape=jax.ShapeDtypeStruct((t, d), F32),
        grid=(t // tm,),
        in_specs=[pl.BlockSpec((tm, kdim), lambda i: (i, 0)),
                  pl.BlockSpec((kdim, d), lambda i: (0, 0)),
                  pl.BlockSpec((tm, d), lambda i: (i, 0)),
                  pl.BlockSpec((None, None, 1, d), lambda i: (i // tps, gate_slot, 0, 0))],
        out_specs=pl.BlockSpec((tm, d), lambda i: (i, 0)),
        compiler_params=_cparams(("parallel",)),
        name="out_proj_residual",
    )(a_bf, w_bf, x2, mod4)


ROUTER_TM = 512


def _router_body(x_ref, gain_ref, sh_ref, sc_ref, rw_ref, rb_ref, h_ref, idx_ref, gate_ref, rank_ref,
                 cnt_ref, carry):
    i = pl.program_id(0)

    @pl.when(i == 0)
    def _():
        carry[...] = jnp.zeros_like(carry)

    tm = x_ref.shape[0]
    h = _ada_norm(x_ref[...], gain_ref[...], sh_ref[...], sc_ref[...])
    h_ref[...] = _pack_bf16_pairs(h)
    lane = lax.broadcasted_iota(jnp.int32, (tm, LANES), 1)
    logits = _dot3(h, rw_ref[...], (((1,), (0,)), ((), ()))) + rb_ref[...]
    logits = jnp.where(lane < N_EXPERTS, logits, NEG_INF)
    tops, sels = [], []
    work = logits
    for _ in range(TOP_K):
        m = jnp.max(work, axis=-1, keepdims=True)
        sel = jnp.min(jnp.where(work == m, lane, LANES), axis=-1, keepdims=True)
        work = jnp.where(lane == sel, NEG_INF, work)
        tops.append(m)
        sels.append(sel)
    exps = [jnp.exp(m - tops[0]) for m in tops]
    denom = exps[0] + exps[1] + exps[2] + exps[3]
    onehot = jnp.zeros((tm, LANES), F32)
    for sel in sels:
        onehot = onehot + (lane == sel).astype(F32)
    ti = lax.broadcasted_iota(jnp.int32, (tm, tm), 0)
    tj = lax.broadcasted_iota(jnp.int32, (tm, tm), 1)
    tri = (ti > tj).astype(BF16)
    before = jnp.dot(tri, onehot.astype(BF16), preferred_element_type=F32) + carry[...]
    idx_out = jnp.zeros((tm, LANES), jnp.int32)
    gate_out = jnp.zeros((tm, LANES), F32)
    rank_out = jnp.zeros((tm, LANES), jnp.int32)
    for k in range(TOP_K):
        rank_k = jnp.sum(jnp.where(lane == sels[k], before, 0.0), axis=-1, keepdims=True)
        idx_out = jnp.where(lane == k, sels[k], idx_out)
        gate_out = jnp.where(lane == k, exps[k] / denom, gate_out)
        rank_out = jnp.where(lane == k, rank_k.astype(jnp.int32), rank_out)
    idx_ref[...] = idx_out
    gate_ref[...] = gate_out
    rank_ref[...] = rank_out
    carry[...] = carry[...] + jnp.sum(onehot, axis=0, keepdims=True)
    cnt_ref[...] = carry[...]


def _router(x2, gain, mod4, sh_slot, sc_slot, rw_pad, rb_pad, seq):
    t, d = x2.shape
    tm = ROUTER_TM
    tps = seq // tm
    tok = lambda dt: jax.ShapeDtypeStruct((t, LANES), dt)
    tok_spec = pl.BlockSpec((tm, LANES), lambda i: (i, 0))
    return pl.pallas_call(
        _router_body,
        out_shape=(jax.ShapeDtypeStruct((t, d // 2), jnp.int32), tok(jnp.int32), tok(F32), tok(jnp.int32),
                   jax.ShapeDtypeStruct((1, LANES), F32)),
        grid=(t // tm,),
        in_specs=[pl.BlockSpec((tm, d), lambda i: (i, 0)),
                  pl.BlockSpec((1, d), lambda i: (0, 0)),
                  pl.BlockSpec((None, None, 1, d), lambda i: (i // tps, sh_slot, 0, 0)),
                  pl.BlockSpec((None, None, 1, d), lambda i: (i // tps, sc_slot, 0, 0)),
                  pl.BlockSpec((d, LANES), lambda i: (0, 0)),
                  pl.BlockSpec((1, LANES), lambda i: (0, 0))],
        out_specs=(pl.BlockSpec((tm, d // 2), lambda i: (i, 0)), tok_spec, tok_spec, tok_spec,
                   pl.BlockSpec((1, LANES), lambda i: (0, 0))),
        scratch_shapes=[pltpu.VMEM((1, LANES), F32)],
        compiler_params=_cparams(("arbitrary",)),
        name="moe_router",
    )(x2, gain, mod4, mod4, rw_pad, rb_pad)


SC_LANES = 16
SC_INDEX_CHUNK = 8192


def _sc_row_tokens(dest_flat, n_rows, n_tok):
    n_assign = dest_flat.shape[0]
    assert n_rows % SC_LANES == 0 and n_assign % SC_INDEX_CHUNK == 0
    n_cores = plsc.get_sparse_core_info().num_cores
    mesh = plsc.VectorSubcoreMesh(core_axis_name="c", subcore_axis_name="s")

    @functools.partial(
        pl.kernel, mesh=mesh,
        out_type=jax.ShapeDtypeStruct((n_rows,), jnp.int32),
        scratch_types=[pltpu.VMEM((n_rows,), jnp.int32), pltpu.VMEM((SC_INDEX_CHUNK,), jnp.int32)],
        compiler_params=dataclasses.replace(pltpu.CompilerParams(), needs_layout_passes=False),
    )
    def row_token_kernel(dest_hbm, out_hbm, rt_v, d_v):
        wid = lax.axis_index("s") * n_cores + lax.axis_index("c")

        @pl.when(wid == 0)
        def _():
            lanes = lax.iota(jnp.int32, SC_LANES)

            @pl.loop(0, n_rows // SC_LANES)
            def _(i):
                rt_v[pl.ds(i * SC_LANES, SC_LANES)] = lax.rem(lanes + i * SC_LANES, n_tok)

            @pl.loop(0, n_assign // SC_INDEX_CHUNK)
            def _(c):
                pltpu.sync_copy(dest_hbm.at[pl.ds(c * SC_INDEX_CHUNK, SC_INDEX_CHUNK)], d_v)

                @pl.loop(0, SC_INDEX_CHUNK // SC_LANES)
                def _(i):
                    idx = d_v[pl.ds(i * SC_LANES, SC_LANES)]
                    tok = lax.div(lanes + (c * SC_INDEX_CHUNK + i * SC_LANES), TOP_K)
                    plsc.store_scatter(rt_v, [idx], tok)

            pltpu.sync_copy(rt_v, out_hbm)

    return row_token_kernel(dest_flat)


def _ffn_body(be_ref, nu_ref, x_ref, uw_ref, ub_ref, dw_ref, db_ref, y_ref, uw_bf, dw_bf):
    i = pl.program_id(0)
    changed = jnp.logical_or(i == 0, be_ref[i] != be_ref[jnp.maximum(i - 1, 0)])
    active = i < nu_ref[0]

    @pl.when(jnp.logical_and(changed, active))
    def _():
        rows = 128
        for r0 in range(0, uw_ref.shape[0], rows):
            uw_bf[r0:r0 + rows, :] = uw_ref[r0:r0 + rows, :].astype(BF16)
        for r0 in range(0, dw_ref.shape[0], rows):
            dw_bf[r0:r0 + rows, :] = dw_ref[r0:r0 + rows, :].astype(BF16)

    @pl.when(active)
    def _():
        x = jnp.concatenate(_unpack_bf16_pairs(x_ref[...]), axis=1).astype(BF16)
        gu = jnp.dot(x, uw_bf[...], preferred_element_type=F32) + ub_ref[...]
        gate = jnp.minimum(gu[:, :D_FF], SWIGLU_LIMIT)
        lin = jnp.clip(gu[:, D_FF:], -SWIGLU_LIMIT, SWIGLU_LIMIT)
        act = gate * jax.nn.sigmoid(SWIGLU_ALPHA * gate) * (lin + 1.0)
        y = jnp.dot(act.astype(BF16), dw_bf[...], preferred_element_type=F32) + db_ref[...]
        y_ref[...] = _pack_bf16_pairs(y)

    @pl.when(jnp.logical_not(active))
    def _():
        y_ref[...] = jnp.zeros_like(y_ref)


def _expert_ffn(xb, blk_expert, n_used, up_w, up_b4, down_w, down_b4, layer):
    n_rows, dp = xb.shape
    d = 2 * dp
    bm = MOE_ROWS
    n_blk = n_rows // bm
    f2 = up_w.shape[-1]
    grid_spec = pltpu.PrefetchScalarGridSpec(
        num_scalar_prefetch=2,
        grid=(n_blk,),
        in_specs=[pl.BlockSpec((bm, dp), lambda i, be, nu: (i, 0)),
                  pl.BlockSpec((None, None, d, f2), lambda i, be, nu: (layer, be[i], 0, 0)),
                  pl.BlockSpec((None, None, 1, f2), lambda i, be, nu: (layer, be[i], 0, 0)),
                  pl.BlockSpec((None, None, f2 // 2, d), lambda i, be, nu: (layer, be[i], 0, 0)),
                  pl.BlockSpec((None, None, 1, d), lambda i, be, nu: (layer, be[i], 0, 0))],
        out_specs=pl.BlockSpec((bm, dp), lambda i, be, nu: (i, 0)),
        scratch_shapes=[pltpu.VMEM((d, f2), BF16), pltpu.VMEM((f2 // 2, d), BF16)],
    )
    return pl.pallas_call(
        _ffn_body,
        out_shape=jax.ShapeDtypeStruct((n_rows, dp), jnp.int32),
        grid_spec=grid_spec,
        compiler_params=_cparams(("arbitrary",)),
        name="moe_expert_ffn",
    )(blk_expert, n_used, xb, up_w, up_b4, down_w, down_b4)


SC_GATHER_ROWS = 64


def _sc_gather_rows(table, idx):
    n, d = idx.shape[0], table.shape[1]
    info = plsc.get_sparse_core_info()
    n_cores, n_sub = info.num_cores, info.num_subcores
    n_workers = n_cores * n_sub
    chunk = SC_GATHER_ROWS
    per_worker = n // n_workers
    n_chunks = per_worker // chunk
    assert n_chunks * chunk * n_workers == n and n_chunks % 2 == 0
    mesh = plsc.VectorSubcoreMesh(core_axis_name="c", subcore_axis_name="s")

    @functools.partial(
        pl.kernel, mesh=mesh,
        out_type=jax.ShapeDtypeStruct((n, d), table.dtype),
        scratch_types=[pltpu.VMEM((n_chunks, chunk), jnp.int32), pltpu.VMEM((2, chunk, d), table.dtype),
                       pltpu.SemaphoreType.DMA((2,))],
    )
    def gather_kernel(table_hbm, idx_hbm, out_hbm, idx_v, rows_v, sems):
        wid = lax.axis_index("s") * n_cores + lax.axis_index("c")
        base = wid * per_worker
        pltpu.sync_copy(idx_hbm.at[wid], idx_v)

        def gather(ci, slot):
            return pltpu.make_async_copy(table_hbm.at[idx_v.at[ci]], rows_v.at[slot], sems.at[slot])

        gather(0, 0).start()

        @pl.loop(0, n_chunks, step=2)
        def _(c0):
            for slot in range(2):
                ci = c0 + slot
                gather(ci, slot).wait()

                @pl.when(ci + 1 < n_chunks)
                def _():
                    gather(ci + 1, 1 - slot).start()

                pltpu.sync_copy(rows_v.at[slot], out_hbm.at[pl.ds(base + ci * chunk, chunk)])

    return gather_kernel(table, idx.reshape(n_workers, n_chunks, chunk))


COMBINE_TM = 256


def _combine_body(gate_ref, x_ref, g_ref, y_ref, o_ref):
    gates = gate_ref[...]
    acc_lo = acc_hi = None
    for k in range(TOP_K):
        lo, hi = _unpack_bf16_pairs(y_ref[k])
        gk = gates[:, k:k + 1]
        acc_lo = gk * lo if acc_lo is None else acc_lo + gk * lo
        acc_hi = gk * hi if acc_hi is None else acc_hi + gk * hi
    o_ref[...] = x_ref[...] + g_ref[...] * jnp.concatenate([acc_lo, acc_hi], axis=1)


def _combine(y_kt, gates, x2, mod4, gate_slot, seq):
    t, d = x2.shape
    tm = COMBINE_TM
    tps = seq // tm
    return pl.pallas_call(
        _combine_body,
        out_shape=jax.ShapeDtypeStruct((t, d), F32),
        grid=(t // tm,),
        in_specs=[pl.BlockSpec((tm, LANES), lambda i: (i, 0)),
                  pl.BlockSpec((tm, d), lambda i: (i, 0)),
                  pl.BlockSpec((None, None, 1, d), lambda i: (i // tps, gate_slot, 0, 0)),
                  pl.BlockSpec((TOP_K, tm, d // 2), lambda i: (0, i, 0))],
        out_specs=pl.BlockSpec((tm, d), lambda i: (i, 0)),
        compiler_params=_cparams(("parallel",)),
        name="moe_combine",
    )(gates, x2, mod4, y_kt)


def _moe_layer(x2, gain, mod4, router_w, router_b, up_w, up_b, down_w, down_b, layer, seq):
    t, d = x2.shape
    rw_pad = jnp.pad(router_w[layer], ((0, 0), (0, LANES - N_EXPERTS)))
    rb_pad = jnp.pad(router_b[layer], (0, LANES - N_EXPERTS)).reshape(1, LANES)
    h2, idx, gates, rank, counts = _router(x2, gain, mod4, 3, 4, rw_pad, rb_pad, seq)

    bm = MOE_ROWS
    counts = counts[0, :N_EXPERTS].astype(jnp.int32)
    padded = (counts + bm - 1) // bm * bm
    p_ends = jnp.cumsum(padded)
    p_starts = p_ends - padded
    n_rows = t * TOP_K + N_EXPERTS * bm
    n_blk = n_rows // bm
    top_idx = idx[:, :TOP_K]
    onehot = top_idx[:, :, None] == jnp.arange(N_EXPERTS, dtype=jnp.int32)[None, None, :]
    dest = jnp.sum(jnp.where(onehot, p_starts[None, None, :], 0), axis=-1) + rank[:, :TOP_K]
    blk_start = jnp.arange(n_blk, dtype=jnp.int32) * bm
    blk_expert = jnp.sum((blk_start[:, None] >= p_ends[None, :]).astype(jnp.int32), axis=-1)
    blk_expert = jnp.minimum(blk_expert, N_EXPERTS - 1).astype(jnp.int32)
    n_used = (p_ends[-1:] // bm).astype(jnp.int32)

    xb = _sc_gather_rows(h2, _sc_row_tokens(dest.reshape(t * TOP_K), n_rows, t))
    f2 = up_w.shape[-1]
    y_rows = _expert_ffn(xb, blk_expert, n_used, up_w, up_b.reshape(up_b.shape[0], N_EXPERTS, 1, f2),
                         down_w, down_b.reshape(down_b.shape[0], N_EXPERTS, 1, d), layer)
    y_kt = _sc_gather_rows(y_rows, dest.T.reshape(TOP_K * t)).reshape(TOP_K, t, d // 2)
    return _combine(y_kt, gates, x2, mod4, 5, seq)


QKV_TM = 256


def _rope_tables(pos_col, invf_row):
    tm = pos_col.shape[0]
    ang = pos_col * invf_row
    c = jnp.cos(ang)
    s = jnp.sin(ang)
    d = lax.broadcasted_iota(jnp.int32, (tm, LANES), 1) % B_HEAD_DIM
    half = ROPE_DIM // 2
    cos_t = jnp.where(d < ROPE_DIM, c, 1.0)
    sin_lo = jnp.where(d < half, -s, 0.0)
    sin_hi = jnp.where(jnp.logical_and(d >= half, d < ROPE_DIM), s, 0.0)
    return cos_t, sin_lo, sin_hi


def _head_norm_rope(x, gain_row, tables):
    cos_t, sin_lo, sin_hi = tables
    lane = lax.broadcasted_iota(jnp.int32, x.shape, 1)
    sq = x * x
    first = lane < B_HEAD_DIM
    ss0 = jnp.sum(jnp.where(first, sq, 0.0), axis=-1, keepdims=True)
    ss1 = jnp.sum(jnp.where(first, 0.0, sq), axis=-1, keepdims=True)
    ms = jnp.where(first, ss0, ss1) * (1.0 / B_HEAD_DIM)
    xn = x * lax.rsqrt(ms + EPS) * gain_row
    half = ROPE_DIM // 2
    return (xn * cos_t + pltpu.roll(xn, LANES - half, 1) * sin_lo + pltpu.roll(xn, half, 1) * sin_hi)


def _qkv_body(x_ref, pos_ref, gq_ref, shq_ref, scq_ref, gkv_ref, shkv_ref, sckv_ref, wq_ref, wkv_ref,
              qg_ref, kg_ref, invf_ref, q_ref, k_ref, v_ref):
    x = x_ref[...]
    y = x * lax.rsqrt(jnp.mean(x * x, axis=-1, keepdims=True) + EPS)
    hq = ((y * gq_ref[...]) * (1.0 + scq_ref[...]) + shq_ref[...]).astype(BF16)
    hkv = ((y * gkv_ref[...]) * (1.0 + sckv_ref[...]) + shkv_ref[...]).astype(BF16)
    tables = _rope_tables(pos_ref[...].astype(F32), invf_ref[...])
    kv = jnp.dot(hkv, wkv_ref[...], preferred_element_type=F32)
    k_ref[...] = _head_norm_rope(kv[:, :LANES], kg_ref[...], tables).astype(BF16)
    v_ref[...] = kv[:, LANES:].astype(BF16)
    q = jnp.dot(hq, wq_ref[...], preferred_element_type=F32)
    scale = B_HEAD_DIM ** -0.5
    for p in range(q.shape[-1] // LANES):
        qp = _head_norm_rope(q[:, p * LANES:(p + 1) * LANES], qg_ref[...], tables)
        q_ref[:, p * LANES:(p + 1) * LANES] = (qp * scale).astype(BF16)


def _qkv(x2, pos2, gq, mod4, gkv, kvmod4, wq_bf, wkv_bf, qg2, kg2, invf, seq):
    t, d = x2.shape
    tm = QKV_TM
    tps = seq // tm
    nq = wq_bf.shape[-1]
    nkv = wkv_bf.shape[-1]
    row = lambda n: pl.BlockSpec((1, n), lambda i: (0, 0))
    modspec = lambda slot: pl.BlockSpec((None, None, 1, d), lambda i, slot=slot: (i // tps, slot, 0, 0))
    return pl.pallas_call(
        _qkv_body,
        out_shape=(jax.ShapeDtypeStruct((t, nq), BF16), jax.ShapeDtypeStruct((t, LANES), BF16),
                   jax.ShapeDtypeStruct((t, LANES), BF16)),
        grid=(t // tm,),
        in_specs=[pl.BlockSpec((tm, d), lambda i: (i, 0)),
                  pl.BlockSpec((tm, 1), lambda i: (i, 0)),
                  row(d), modspec(0), modspec(1),
                  row(d), modspec(0), modspec(1),
                  pl.BlockSpec((d, nq), lambda i: (0, 0)),
                  pl.BlockSpec((d, nkv), lambda i: (0, 0)),
                  row(LANES), row(LANES), row(LANES)],
        out_specs=(pl.BlockSpec((tm, nq), lambda i: (i, 0)),
                   pl.BlockSpec((tm, LANES), lambda i: (i, 0)),
                   pl.BlockSpec((tm, LANES), lambda i: (i, 0))),
        compiler_params=_cparams(("parallel",)),
        name="swa_qkv_proj",
    )(x2, pos2, gq, mod4, mod4, gkv, kvmod4, kvmod4, wq_bf, wkv_bf, qg2, kg2, invf)


def _attn_body(sink_ref, q_ref, kc_ref, kp_ref, vc_ref, vp_ref, o_ref):
    n = pl.program_id(1)
    w = WINDOW
    qi = lax.broadcasted_iota(jnp.int32, (w, 2 * w), 0) + w
    kj = lax.broadcasted_iota(jnp.int32, (w, 2 * w), 1)
    band = jnp.logical_and(kj <= qi, qi - kj < w)
    mask = jnp.logical_and(band, jnp.logical_or(n > 0, kj >= w))
    hd = B_HEAD_DIM
    lane = lax.broadcasted_iota(jnp.int32, (2 * w, LANES), 1)
    first = lane < hd
    kfull = jnp.concatenate([kp_ref[...], kc_ref[...]], axis=0).astype(F32)
    vfull = jnp.concatenate([vp_ref[...], vc_ref[...]], axis=0).astype(F32)
    placed = {}
    for g in range(B_KV_HEADS):
        for name, full in (("k", kfull), ("v", vfull)):
            swapped = pltpu.roll(full, hd, 1)
            own_first = full if g == 0 else swapped
            own_second = swapped if g == 0 else full
            placed[name, g, 0] = jnp.where(first, own_first, 0.0).astype(BF16)
            placed[name, g, 1] = jnp.where(first, 0.0, own_second).astype(BF16)
    pairs = B_Q_HEADS // 2
    heads = [(p, side) for p in range(pairs) for side in range(2)]
    scores = []
    for p, side in heads:
        g = (2 * p) // B_GROUP
        q_pair = q_ref[:, p * LANES:(p + 1) * LANES]
        s = lax.dot_general(q_pair, placed["k", g, side], (((1,), (1,)), ((), ())), preferred_element_type=F32)
        scores.append(jnp.where(mask, s, NEG_INF))
    probs, denoms = [], []
    for (p, side), s in zip(heads, scores):
        sink = sink_ref[0, 2 * p + side]
        m = jnp.maximum(jnp.max(jnp.maximum(s[:, :w], s[:, w:]), axis=-1, keepdims=True), sink)
        e = jnp.exp(s - m)
        denoms.append(jnp.sum(e[:, :w] + e[:, w:], axis=-1, keepdims=True) + jnp.exp(sink - m))
        probs.append(e.astype(BF16))
    outs = []
    for (p, side), pr, den in zip(heads, probs, denoms):
        g = (2 * p) // B_GROUP
        outs.append(jnp.dot(pr, placed["v", g, side], preferred_element_type=F32) / den)
    for p in range(pairs):
        o_ref[:, p * LANES:(p + 1) * LANES] = (outs[2 * p] + outs[2 * p + 1]).astype(BF16)


def _attention(q, k, v, sinks2, bsz, seq):
    t, nq = q.shape
    w = WINDOW
    nb = seq // w
    cur = lambda b, n: (b * nb + n, 0)
    prev = lambda b, n: (b * nb + jnp.maximum(n - 1, 0), 0)
    return pl.pallas_call(
        _attn_body,
        out_shape=jax.ShapeDtypeStruct((t, nq), BF16),
        grid=(bsz, nb),
        in_specs=[pl.BlockSpec(memory_space=pltpu.SMEM),
                  pl.BlockSpec((w, nq), cur),
                  pl.BlockSpec((w, LANES), cur), pl.BlockSpec((w, LANES), prev),
                  pl.BlockSpec((w, LANES), cur), pl.BlockSpec((w, LANES), prev)],
        out_specs=pl.BlockSpec((w, nq), cur),
        compiler_params=_cparams(("parallel", "parallel")),
        name="swa_sink_attention",
    )(sinks2, q, k, k, v, v)


def kernel(x, c, positions, ada_w, ada_b, norm_gain, a_w_in, a_conv, a_log, a_dt_bias, a_out_gain, a_w_out,
           kv_ada_w, kv_ada_b, kv_norm_gain, kv_w, k_norm_gain, b_w_q, q_norm_gain, b_sinks, b_w_out,
           router_w, router_b, up_w, up_b, down_w, down_b):
    bsz, seq, d = x.shape
    t = bsz * seq
    depth = ada_w.shape[0]
    x2 = x.reshape(t, d)
    ada_b3 = ada_b.reshape(depth, 1, 6 * d)

    for layer in range(depth):
        mod4 = _modulation(c, ada_w, ada_b3, layer).reshape(bsz, 6, 1, d)
        gain1 = norm_gain[layer, 0].reshape(1, d)
        gain2 = norm_gain[layer, 1].reshape(1, d)
        if layer < N_A_LAYERS:
            w_in = jnp.pad(a_w_in[layer].astype(BF16), ((0, 0), (0, A_PROJ_PAD - a_w_in.shape[-1])))
            proj = _in_proj(x2, gain1, mod4, w_in, seq)
            o = _gdn(proj, a_conv[layer], a_log[layer].reshape(1, A_HEADS),
                     a_dt_bias[layer].reshape(1, A_HEADS), a_out_gain[layer].reshape(1, A_HEAD_DIM),
                     bsz, seq)
            x2 = _out_proj_residual(o, a_w_out[layer].astype(BF16), x2, mod4, 2, seq)
        else:
            j = layer - N_A_LAYERS
            kvmod4 = _modulation(c, kv_ada_w.reshape(1, d, 2 * d), kv_ada_b.reshape(1, 1, 2 * d), 0)
            kvmod4 = kvmod4.reshape(bsz, 2, 1, d)
            half = ROPE_DIM // 2
            inv_freq = ROPE_THETA ** (-np.arange(0, ROPE_DIM, 2, dtype=np.float32) / ROPE_DIM)
            invf = jnp.asarray(np.tile(inv_freq.astype(np.float32), LANES // half).reshape(1, LANES))
            q, k, v = _qkv(x2, positions.reshape(t, 1), gain1, mod4, kv_norm_gain.reshape(1, d), kvmod4,
                           b_w_q[j].astype(BF16), kv_w.astype(BF16),
                           jnp.tile(q_norm_gain[j], 2).reshape(1, LANES),
                           jnp.tile(k_norm_gain, 2).reshape(1, LANES), invf, seq)
            o = _attention(q, k, v, b_sinks[j].reshape(1, B_Q_HEADS), bsz, seq)
            x2 = _out_proj_residual(o, b_w_out[j].astype(BF16), x2, mod4, 2, seq)
        x2 = _moe_layer(x2, gain2, mod4, router_w, router_b, up_w, up_b, down_w, down_b, layer, seq)
    return x2.reshape(bsz, seq, d)
```

```python
import dataclasses
import functools

import jax
import jax.numpy as jnp
import numpy as np
from jax import lax
from jax.experimental import pallas as pl
from jax.experimental.pallas import tpu as pltpu
from jax.experimental.pallas import tpu_sc as plsc

F32 = jnp.float32
BF16 = jnp.bfloat16
HIGHEST = lax.Precision.HIGHEST

D_MODEL = 1024
N_A_LAYERS = 1

A_HEAD_DIM = 128
A_HEADS = 8
A_WIDTH = 1024
A_CONV = 4
A_CHUNK = 64
A_PROJ_PAD = 4608

B_HEAD_DIM = 64
B_Q_HEADS = 16
B_KV_HEADS = 2
B_GROUP = 8
WINDOW = 128
ROPE_DIM = 16
ROPE_THETA = 500000.0

N_EXPERTS = 32
TOP_K = 4
D_FF = 1024
SWIGLU_LIMIT = 7.0
SWIGLU_ALPHA = 1.702
MOE_ROWS = 512

EPS = 1e-6
LANES = 128
NEG_INF = float("-inf")

VMEM_LIMIT = 56 * 1024 * 1024


def _cparams(sem, vmem=VMEM_LIMIT):
    return pltpu.CompilerParams(dimension_semantics=sem, vmem_limit_bytes=vmem)


def _silu(x):
    return x * jax.nn.sigmoid(x)


HI_HALF = -65536


def _pack_bf16_pairs(x):
    n = x.shape[-1] // 2
    bits = lax.bitcast_convert_type(x.astype(BF16).astype(F32), jnp.int32)
    return jnp.bitwise_or(lax.shift_right_logical(bits[:, :n], 16), jnp.bitwise_and(bits[:, n:], HI_HALF))


def _unpack_bf16_pairs(w):
    lo = lax.bitcast_convert_type(lax.shift_left(w, 16), F32)
    hi = lax.bitcast_convert_type(jnp.bitwise_and(w, HI_HALF), F32)
    return lo, hi


def _ada_norm(x, gain, shift, scale):
    y = x * lax.rsqrt(jnp.mean(x * x, axis=-1, keepdims=True) + EPS)
    return (y * gain) * (1.0 + scale) + shift


def _mod_body(c_ref, w_ref, b_ref, o_ref):
    o_ref[...] = jnp.dot(_silu(c_ref[...]), w_ref[...], preferred_element_type=F32,
                         precision=HIGHEST) + b_ref[...]


def _modulation(c, w3, b3, layer):
    bsz, d = c.shape
    n = w3.shape[-1]
    tn = 1024
    return pl.pallas_call(
        _mod_body,
        out_shape=jax.ShapeDtypeStruct((bsz, n), F32),
        grid=(n // tn,),
        in_specs=[pl.BlockSpec((bsz, d), lambda j: (0, 0)),
                  pl.BlockSpec((None, d, tn), lambda j: (layer, 0, j)),
                  pl.BlockSpec((None, 1, tn), lambda j: (layer, 0, j))],
        out_specs=pl.BlockSpec((bsz, tn), lambda j: (0, j)),
        compiler_params=_cparams(("arbitrary",)),
        name="adaln_mod",
    )(c, w3, b3)


INPROJ_TM = 256
INPROJ_TN = 256
HALO = 8


def _inproj_body(x_ref, gain_ref, sh_ref, sc_ref, w_ref, conv_ref, o_ref, ext, *, tiles_per_seq):
    i = pl.program_id(0)
    tm = x_ref.shape[0]
    hd = A_HEAD_DIM
    n = o_ref.shape[-1]
    tn = INPROJ_TN
    h = _ada_norm(x_ref[...], gain_ref[...], sh_ref[...], sc_ref[...]).astype(BF16)

    @pl.when(i % tiles_per_seq == 0)
    def _():
        ext[0:HALO, :] = jnp.zeros((HALO, ext.shape[-1]), F32)

    def conv_silu_norm(raw, n0):
        cols = slice(n0, n0 + tn)
        ext[HALO:HALO + tm, cols] = raw
        acc = raw * conv_ref[A_CONV - 1:A_CONV, cols]
        for s in range(1, A_CONV):
            acc = acc + ext[HALO - s:HALO - s + tm, cols] * conv_ref[A_CONV - 1 - s:A_CONV - s, cols]
        ext[0:HALO, cols] = raw[tm - HALO:, :]
        y = _silu(acc)
        if n0 >= 2 * A_WIDTH:
            return y
        scale = hd ** -0.5 if n0 < A_WIDTH else 1.0
        heads = []
        for c0 in range(0, tn, hd):
            yh = y[:, c0:c0 + hd]
            heads.append(yh * (lax.rsqrt(jnp.sum(yh * yh, axis=-1, keepdims=True) + EPS) * scale))
        return jnp.concatenate(heads, axis=1)

    chunks = list(range(0, n, tn))
    raw = jnp.dot(h, w_ref[:, 0:tn], preferred_element_type=F32)
    for idx, n0 in enumerate(chunks):
        nxt = None
        if idx + 1 < len(chunks):
            n1 = chunks[idx + 1]
            nxt = jnp.dot(h, w_ref[:, n1:n1 + tn], preferred_element_type=F32)
        o_ref[:, n0:n0 + tn] = conv_silu_norm(raw, n0) if n0 < 3 * A_WIDTH else raw
        raw = nxt


def _in_proj(x2, gain, mod4, w_bf, conv_w, seq):
    t, d = x2.shape
    n = w_bf.shape[-1]
    tm = INPROJ_TM
    tps = seq // tm
    return pl.pallas_call(
        functools.partial(_inproj_body, tiles_per_seq=tps),
        out_shape=jax.ShapeDtypeStruct((t, n), F32),
        grid=(t // tm,),
        in_specs=[pl.BlockSpec((tm, d), lambda i: (i, 0)),
                  pl.BlockSpec((1, d), lambda i: (0, 0)),
                  pl.BlockSpec((None, None, 1, d), lambda i: (i // tps, 0, 0, 0)),
                  pl.BlockSpec((None, None, 1, d), lambda i: (i // tps, 1, 0, 0)),
                  pl.BlockSpec((d, n), lambda i: (0, 0)),
                  pl.BlockSpec(conv_w.shape, lambda i: (0, 0))],
        out_specs=pl.BlockSpec((tm, n), lambda i: (i, 0)),
        scratch_shapes=[pltpu.VMEM((HALO + tm, conv_w.shape[-1]), F32)],
        compiler_params=_cparams(("arbitrary",)),
        name="gdn_in_proj",
    )(x2, gain, mod4, mod4, w_bf, conv_w)


GDN_HG = 2
GDN_RB = 256
GDN_SC = 128
GDN_SUB = 4


def _split_bf16(x):
    hi = x.astype(BF16)
    return hi, (x - hi.astype(F32)).astype(BF16)


def _mm3(a_hi, a_lo, b_hi, b_lo, dims=(((1,), (0,)), ((), ()))):
    dg = functools.partial(lax.dot_general, dimension_numbers=dims, preferred_element_type=F32)
    return dg(a_hi, b_hi) + (dg(a_lo, b_hi) + dg(a_hi, b_lo))


def _dot3(a, b, dims):
    return _mm3(*_split_bf16(a), *_split_bf16(b), dims)


def _gdn_body(alog_ref, dtb_ref, q_ref, k_ref, v_ref, z_ref, ab_ref, og_ref,
              o_ref, bs, gcs, mp, bc, qp, op):
    seq = q_ref.shape[0]
    hd = A_HEAD_DIM
    ch = A_CHUNK
    g_idx = pl.program_id(1)

    lane = lax.broadcasted_iota(jnp.int32, (GDN_RB, hd), 1)
    row = lax.broadcasted_iota(jnp.int32, (GDN_RB, hd), 0)
    row_in_chunk = row % ch

    sc_rows = GDN_SC
    per_sc = sc_rows // ch
    rb_per_step = sc_rows * GDN_SUB // GDN_RB
    head_cols = lambda hh: slice(hh * hd, (hh + 1) * hd)

    def prep_stages(j):
        for rbi in range(rb_per_step):
            r = j * rb_per_step + rbi
            start = pl.multiple_of(r * GDN_RB, GDN_RB)
            rows = pl.ds(start, GDN_RB)
            for hh in range(GDN_HG):
                head = g_idx * GDN_HG + hh
                ab = ab_ref[rows, :]
                a_col = jnp.sum(jnp.where(lane == head, ab, 0.0), axis=-1, keepdims=True)
                b_col = jnp.sum(jnp.where(lane == head + A_HEADS, ab, 0.0), axis=-1, keepdims=True)
                bs[hh, rows, :] = jnp.broadcast_to(jax.nn.sigmoid(b_col), (GDN_RB, hd))
                xa = a_col + dtb_ref[0, head]
                softplus = jnp.maximum(xa, 0.0) + jnp.log1p(jnp.exp(-jnp.abs(xa)))
                g = jnp.broadcast_to(-jnp.exp(alog_ref[0, head]) * softplus, (GDN_RB, hd))
                s = 1
                while s < ch:
                    g = g + jnp.where(row_in_chunk >= s, pltpu.roll(g, s, 0), 0.0)
                    s *= 2
                gcs[hh, rows, :] = g
                yield

    ri = lax.broadcasted_iota(jnp.int32, (sc_rows, sc_rows), 0)
    cj = lax.broadcasted_iota(jnp.int32, (sc_rows, sc_rows), 1)
    same_chunk = (ri // ch) == (cj // ch)
    incl = jnp.logical_and(same_chunk, ri >= cj)
    strict = jnp.logical_and(same_chunk, ri > cj)
    eye = jnp.where(ri == cj, 1.0, 0.0)
    chunk_of_col = lax.broadcasted_iota(jnp.int32, (hd, sc_rows), 1) // ch
    lanes_nt = (((1,), (1,)), ((), ()))
    plain = (((1,), (0,)), ((), ()))

    chunks_per_step = GDN_SUB * per_sc
    n_steps = seq // (sc_rows * GDN_SUB)

    def local_stages(i):
        chains = [(i * GDN_SUB + sub, hh) for sub in range(GDN_SUB) for hh in range(GDN_HG)]
        rows_of = lambda blk: pl.ds(pl.multiple_of(blk * sc_rows, sc_rows), sc_rows)

        def start(blk, hh):
            rows = rows_of(blk)
            k = k_ref[rows, head_cols(hh)]
            gc = gcs[hh, rows, :]
            decay = jnp.where(incl, jnp.exp(jnp.minimum(gc - gc.T[0:1, :], 0.0)), 0.0)
            k_bf = k.astype(BF16)
            kk = lax.dot_general((k * bs[hh, rows, :]).astype(BF16), k_bf, lanes_nt, preferred_element_type=F32)
            qk = lax.dot_general(q_ref[rows, head_cols(hh)].astype(BF16), k_bf, lanes_nt, preferred_element_type=F32)
            return jnp.where(strict, -(kk * decay), 0.0), (qk * decay).astype(BF16)

        started = [start(*c) for c in chains]
        yield
        qks = [s[1] for s in started]
        invs = [eye + s[0] for s in started]
        pws = [_dot3(s[0], s[0], plain) for s in started]
        yield
        for step in range(1, 6):
            for n in range(len(chains)):
                pw_hi, pw_lo = _split_bf16(pws[n])
                inv_hi, inv_lo = _split_bf16(invs[n])
                if step < 5:
                    prod = _mm3(jnp.concatenate([pw_hi, inv_hi], axis=0),
                                jnp.concatenate([pw_lo, inv_lo], axis=0), pw_hi, pw_lo)
                    pws[n] = prod[:sc_rows]
                    invs[n] = invs[n] + prod[sc_rows:]
                else:
                    invs[n] = invs[n] + _mm3(inv_hi, inv_lo, pw_hi, pw_lo)
            yield

        def solve(n, blk, hh):
            rows = rows_of(blk)
            beta = bs[hh, rows, :]
            rhs = jnp.concatenate([v_ref[rows, head_cols(hh)] * beta, k_ref[rows, head_cols(hh)] * beta * jnp.exp(gcs[hh, rows, :])], axis=1)
            return _dot3(invs[n], rhs, plain).astype(BF16)

        uws = [solve(n, *c) for n, c in enumerate(chains)]
        yield

        def finish(n, blk, hh):
            rows = rows_of(blk)
            k = k_ref[rows, head_cols(hh)]
            gc = gcs[hh, rows, :]
            res = jnp.dot(qks[n], uws[n], preferred_element_type=F32)
            op[hh, rows, :] = res[:, :hd]
            qp[hh, rows, :] = (q_ref[rows, head_cols(hh)] * jnp.exp(gc) - res[:, hd:]).astype(BF16)
            gl = jnp.concatenate(
                [jnp.broadcast_to(gc[(j + 1) * ch - 1:(j + 1) * ch, :], (ch, hd)) for j in range(per_sc)], axis=0)
            kt_t = (k * jnp.exp(gl - gc)).T
            for j in range(per_sc):
                kt_j = jnp.where(chunk_of_col == j, kt_t, 0.0).astype(BF16)
                bm = jnp.dot(kt_j, uws[n], preferred_element_type=F32)
                bc[hh, blk * per_sc + j] = bm[:, :hd]
                mp[hh, blk * per_sc + j] = bm[:, hd:].astype(BF16)

        for n, c in enumerate(chains):
            finish(n, *c)

    def scan_stages(j, states):
        for cc in range(chunks_per_step):
            c = j * chunks_per_step + cc
            r0 = pl.multiple_of(c * ch, ch)
            rows = pl.ds(r0, ch)
            for hh in range(GDN_HG):
                col0 = hh * hd
                state = states[hh]
                s_bf = state.astype(BF16)
                o = jnp.dot(qp[hh, rows, :], s_bf, preferred_element_type=F32) + op[hh, rows, :]
                g_tot = jnp.exp(gcs[hh, pl.ds(r0 + ch - 1, 1), :])
                states[hh] = state * g_tot - jnp.dot(mp[hh, c], s_bf, preferred_element_type=F32) + bc[hh, c]
                on = o * lax.rsqrt(jnp.mean(o * o, axis=-1, keepdims=True) + EPS) * og_ref[...]
                z = z_ref[rows, col0:col0 + hd]
                o_ref[rows, col0:col0 + hd] = (on * _silu(z)).astype(BF16)
            yield

    def interleave(*gens):
        live = list(gens)
        while live:
            for g in list(live):
                try:
                    next(g)
                except StopIteration:
                    live.remove(g)

    assert n_steps >= 2
    interleave(prep_stages(0))
    interleave(local_stages(0), prep_stages(1))

    def pipelined(i, states):
        states = list(states)
        interleave(local_stages(i), scan_stages(i - 1, states), prep_stages(i + 1))
        return tuple(states)

    states = list(lax.fori_loop(1, n_steps - 1, pipelined,
                                tuple(jnp.zeros((hd, hd), F32) for _ in range(GDN_HG))))
    interleave(local_stages(n_steps - 1), scan_stages(n_steps - 2, states))
    interleave(scan_stages(n_steps - 1, states))


def _gdn(proj, a_log, dt_bias, out_gain, bsz, seq):
    t = proj.shape[0]
    hd = A_HEAD_DIM
    wb = hd * GDN_HG
    ng = A_HEADS // GDN_HG
    per = A_WIDTH // wb
    n_chunks = seq // A_CHUNK
    smem = pl.BlockSpec(memory_space=pltpu.SMEM)
    seq_spec = lambda off: pl.BlockSpec((seq, wb), lambda b, g, off=off: (b, off + g))
    sc = lambda dt=F32: pltpu.VMEM((GDN_HG, seq, hd), dt)
    return pl.pallas_call(
        _gdn_body,
        out_shape=jax.ShapeDtypeStruct((t, A_WIDTH), BF16),
        grid=(bsz, ng),
        in_specs=[smem, smem,
                  seq_spec(0), seq_spec(per), seq_spec(2 * per), seq_spec(3 * per),
                  pl.BlockSpec((seq, LANES), lambda b, g: (b, 4 * A_WIDTH // LANES)),
                  pl.BlockSpec((1, hd), lambda b, g: (0, 0))],
        out_specs=pl.BlockSpec((seq, wb), lambda b, g: (b, g)),
        scratch_shapes=[sc(), sc(),
                        pltpu.VMEM((GDN_HG, n_chunks, hd, hd), BF16),
                        pltpu.VMEM((GDN_HG, n_chunks, hd, hd), F32),
                        sc(BF16), sc()],
        compiler_params=_cparams(("parallel", "parallel")),
        name="gdn_core",
    )(a_log, dt_bias, proj, proj, proj, proj, proj, out_gain)


def _oproj_body(a_ref, w_ref, x_ref, g_ref, o_ref):
    y = jnp.dot(a_ref[...], w_ref[...], preferred_element_type=F32)
    o_ref[...] = x_ref[...] + g_ref[...] * y


def _out_proj_residual(a_bf, w_bf, x2, mod4, gate_slot, seq):
    t, d = x2.shape
    kdim = a_bf.shape[-1]
    tm = 512
    tps = seq // tm
    return pl.pallas_call(
        _oproj_body,
        out_shape=jax.ShapeDtypeStruct((t, d), F32),
        grid=(t // tm,),
        in_specs=[pl.BlockSpec((tm, kdim), lambda i: (i, 0)),
                  pl.BlockSpec((kdim, d), lambda i: (0, 0)),
                  pl.BlockSpec((tm, d), lambda i: (i, 0)),
                  pl.BlockSpec((None, None, 1, d), lambda i: (i // tps, gate_slot, 0, 0))],
        out_specs=pl.BlockSpec((tm, d), lambda i: (i, 0)),
        compiler_params=_cparams(("parallel",)),
        name="out_proj_residual",
    )(a_bf, w_bf, x2, mod4)


ROUTER_TM = 512


def _router_body(x_ref, gain_ref, sh_ref, sc_ref, rw_ref, rb_ref, h_ref, idx_ref, gate_ref, rank_ref,
                 cnt_ref, carry):
    i = pl.program_id(0)

    @pl.when(i == 0)
    def _():
        carry[...] = jnp.zeros_like(carry)

    tm = x_ref.shape[0]
    h = _ada_norm(x_ref[...], gain_ref[...], sh_ref[...], sc_ref[...])
    h_ref[...] = _pack_bf16_pairs(h)
    lane = lax.broadcasted_iota(jnp.int32, (tm, LANES), 1)
    logits = _dot3(h, rw_ref[...], (((1,), (0,)), ((), ()))) + rb_ref[...]
    logits = jnp.where(lane < N_EXPERTS, logits, NEG_INF)
    tops, sels = [], []
    work = logits
    for _ in range(TOP_K):
        m = jnp.max(work, axis=-1, keepdims=True)
        sel = jnp.min(jnp.where(work == m, lane, LANES), axis=-1, keepdims=True)
        work = jnp.where(lane == sel, NEG_INF, work)
        tops.append(m)
        sels.append(sel)
    exps = [jnp.exp(m - tops[0]) for m in tops]
    denom = exps[0] + exps[1] + exps[2] + exps[3]
    onehot = jnp.zeros((tm, LANES), F32)
    for sel in sels:
        onehot = onehot + (lane == sel).astype(F32)
    ti = lax.broadcasted_iota(jnp.int32, (tm, tm), 0)
    tj = lax.broadcasted_iota(jnp.int32, (tm, tm), 1)
    tri = (ti > tj).astype(BF16)
    before = jnp.dot(tri, onehot.astype(BF16), preferred_element_type=F32) + carry[...]
    idx_out = jnp.zeros((tm, LANES), jnp.int32)
    gate_out = jnp.zeros((tm, LANES), F32)
    rank_out = jnp.zeros((tm, LANES), jnp.int32)
    for k in range(TOP_K):
        rank_k = jnp.sum(jnp.where(lane == sels[k], before, 0.0), axis=-1, keepdims=True)
        idx_out = jnp.where(lane == k, sels[k], idx_out)
        gate_out = jnp.where(lane == k, exps[k] / denom, gate_out)
        rank_out = jnp.where(lane == k, rank_k.astype(jnp.int32), rank_out)
    idx_ref[...] = idx_out
    gate_ref[...] = gate_out
    rank_ref[...] = rank_out
    carry[...] = carry[...] + jnp.sum(onehot, axis=0, keepdims=True)
    cnt_ref[...] = carry[...]


def _router(x2, gain, mod4, sh_slot, sc_slot, rw_pad, rb_pad, seq):
    t, d = x2.shape
    tm = ROUTER_TM
    tps = seq // tm
    tok = lambda dt: jax.ShapeDtypeStruct((t, LANES), dt)
    tok_spec = pl.BlockSpec((tm, LANES), lambda i: (i, 0))
    return pl.pallas_call(
        _router_body,
        out_shape=(jax.ShapeDtypeStruct((t, d // 2), jnp.int32), tok(jnp.int32), tok(F32), tok(jnp.int32),
                   jax.ShapeDtypeStruct((1, LANES), F32)),
        grid=(t // tm,),
        in_specs=[pl.BlockSpec((tm, d), lambda i: (i, 0)),
                  pl.BlockSpec((1, d), lambda i: (0, 0)),
                  pl.BlockSpec((None, None, 1, d), lambda i: (i // tps, sh_slot, 0, 0)),
                  pl.BlockSpec((None, None, 1, d), lambda i: (i // tps, sc_slot, 0, 0)),
                  pl.BlockSpec((d, LANES), lambda i: (0, 0)),
                  pl.BlockSpec((1, LANES), lambda i: (0, 0))],
        out_specs=(pl.BlockSpec((tm, d // 2), lambda i: (i, 0)), tok_spec, tok_spec, tok_spec,
                   pl.BlockSpec((1, LANES), lambda i: (0, 0))),
        scratch_shapes=[pltpu.VMEM((1, LANES), F32)],
        compiler_params=_cparams(("arbitrary",)),
        name="moe_router",
    )(x2, gain, mod4, mod4, rw_pad, rb_pad)


SC_LANES = 16
SC_INDEX_CHUNK = 8192


def _sc_row_tokens(dest_flat, n_rows, n_tok):
    n_assign = dest_flat.shape[0]
    assert n_rows % SC_LANES == 0 and n_assign % SC_INDEX_CHUNK == 0
    n_cores = plsc.get_sparse_core_info().num_cores
    mesh = plsc.VectorSubcoreMesh(core_axis_name="c", subcore_axis_name="s")

    @functools.partial(
        pl.kernel, mesh=mesh,
        out_type=jax.ShapeDtypeStruct((n_rows,), jnp.int32),
        scratch_types=[pltpu.VMEM((n_rows,), jnp.int32), pltpu.VMEM((SC_INDEX_CHUNK,), jnp.int32)],
        compiler_params=dataclasses.replace(pltpu.CompilerParams(), needs_layout_passes=False),
    )
    def row_token_kernel(dest_hbm, out_hbm, rt_v, d_v):
        wid = lax.axis_index("s") * n_cores + lax.axis_index("c")

        @pl.when(wid == 0)
        def _():
            lanes = lax.iota(jnp.int32, SC_LANES)

            @pl.loop(0, n_rows // SC_LANES)
            def _(i):
                rt_v[pl.ds(i * SC_LANES, SC_LANES)] = lax.rem(lanes + i * SC_LANES, n_tok)

            @pl.loop(0, n_assign // SC_INDEX_CHUNK)
            def _(c):
                pltpu.sync_copy(dest_hbm.at[pl.ds(c * SC_INDEX_CHUNK, SC_INDEX_CHUNK)], d_v)

                @pl.loop(0, SC_INDEX_CHUNK // SC_LANES)
                def _(i):
                    idx = d_v[pl.ds(i * SC_LANES, SC_LANES)]
                    tok = lax.div(lanes + (c * SC_INDEX_CHUNK + i * SC_LANES), TOP_K)
                    plsc.store_scatter(rt_v, [idx], tok)

            pltpu.sync_copy(rt_v, out_hbm)

    return row_token_kernel(dest_flat)


def _ffn_body(be_ref, nu_ref, x_ref, uw_ref, ub_ref, dw_ref, db_ref, y_ref, uw_bf, dw_bf):
    i = pl.program_id(0)
    changed = jnp.logical_or(i == 0, be_ref[i] != be_ref[jnp.maximum(i - 1, 0)])
    active = i < nu_ref[0]

    @pl.when(jnp.logical_and(changed, active))
    def _():
        rows = 128
        for r0 in range(0, uw_ref.shape[0], rows):
            uw_bf[r0:r0 + rows, :] = uw_ref[r0:r0 + rows, :].astype(BF16)
        for r0 in range(0, dw_ref.shape[0], rows):
            dw_bf[r0:r0 + rows, :] = dw_ref[r0:r0 + rows, :].astype(BF16)

    @pl.when(active)
    def _():
        x = jnp.concatenate(_unpack_bf16_pairs(x_ref[...]), axis=1).astype(BF16)
        gu = jnp.dot(x, uw_bf[...], preferred_element_type=F32) + ub_ref[...]
        gate = jnp.minimum(gu[:, :D_FF], SWIGLU_LIMIT)
        lin = jnp.clip(gu[:, D_FF:], -SWIGLU_LIMIT, SWIGLU_LIMIT)
        act = gate * jax.nn.sigmoid(SWIGLU_ALPHA * gate) * (lin + 1.0)
        y = jnp.dot(act.astype(BF16), dw_bf[...], preferred_element_type=F32) + db_ref[...]
        y_ref[...] = _pack_bf16_pairs(y)

    @pl.when(jnp.logical_not(active))
    def _():
        y_ref[...] = jnp.zeros_like(y_ref)


def _expert_ffn(xb, blk_expert, n_used, up_w, up_b4, down_w, down_b4, layer):
    n_rows, dp = xb.shape
    d = 2 * dp
    bm = MOE_ROWS
    n_blk = n_rows // bm
    f2 = up_w.shape[-1]
    grid_spec = pltpu.PrefetchScalarGridSpec(
        num_scalar_prefetch=2,
        grid=(n_blk,),
        in_specs=[pl.BlockSpec((bm, dp), lambda i, be, nu: (i, 0)),
                  pl.BlockSpec((None, None, d, f2), lambda i, be, nu: (layer, be[i], 0, 0)),
                  pl.BlockSpec((None, None, 1, f2), lambda i, be, nu: (layer, be[i], 0, 0)),
                  pl.BlockSpec((None, None, f2 // 2, d), lambda i, be, nu: (layer, be[i], 0, 0)),
                  pl.BlockSpec((None, None, 1, d), lambda i, be, nu: (layer, be[i], 0, 0))],
        out_specs=pl.BlockSpec((bm, dp), lambda i, be, nu: (i, 0)),
        scratch_shapes=[pltpu.VMEM((d, f2), BF16), pltpu.VMEM((f2 // 2, d), BF16)],
    )
    return pl.pallas_call(
        _ffn_body,
        out_shape=jax.ShapeDtypeStruct((n_rows, dp), jnp.int32),
        grid_spec=grid_spec,
        compiler_params=_cparams(("arbitrary",)),
        name="moe_expert_ffn",
    )(blk_expert, n_used, xb, up_w, up_b4, down_w, down_b4)


SC_GATHER_ROWS = 64


def _sc_gather_rows(table, idx):
    n, d = idx.shape[0], table.shape[1]
    info = plsc.get_sparse_core_info()
    n_cores, n_sub = info.num_cores, info.num_subcores
    n_workers = n_cores * n_sub
    chunk = SC_GATHER_ROWS
    per_worker = n // n_workers
    n_chunks = per_worker // chunk
    assert n_chunks * chunk * n_workers == n and n_chunks % 2 == 0
    mesh = plsc.VectorSubcoreMesh(core_axis_name="c", subcore_axis_name="s")

    @functools.partial(
        pl.kernel, mesh=mesh,
        out_type=jax.ShapeDtypeStruct((n, d), table.dtype),
        scratch_types=[pltpu.VMEM((n_chunks, chunk), jnp.int32), pltpu.VMEM((2, chunk, d), table.dtype),
                       pltpu.SemaphoreType.DMA((2,))],
    )
    def gather_kernel(table_hbm, idx_hbm, out_hbm, idx_v, rows_v, sems):
        wid = lax.axis_index("s") * n_cores + lax.axis_index("c")
        base = wid * per_worker
        pltpu.sync_copy(idx_hbm.at[wid], idx_v)

        def gather(ci, slot):
            return pltpu.make_async_copy(table_hbm.at[idx_v.at[ci]], rows_v.at[slot], sems.at[slot])

        gather(0, 0).start()

        @pl.loop(0, n_chunks, step=2)
        def _(c0):
            for slot in range(2):
                ci = c0 + slot
                gather(ci, slot).wait()

                @pl.when(ci + 1 < n_chunks)
                def _():
                    gather(ci + 1, 1 - slot).start()

                pltpu.sync_copy(rows_v.at[slot], out_hbm.at[pl.ds(base + ci * chunk, chunk)])

    return gather_kernel(table, idx.reshape(n_workers, n_chunks, chunk))


COMBINE_TM = 256


def _combine_body(gate_ref, x_ref, g_ref, y_ref, o_ref):
    gates = gate_ref[...]
    acc_lo = acc_hi = None
    for k in range(TOP_K):
        lo, hi = _unpack_bf16_pairs(y_ref[k])
        gk = gates[:, k:k + 1]
        acc_lo = gk * lo if acc_lo is None else acc_lo + gk * lo
        acc_hi = gk * hi if acc_hi is None else acc_hi + gk * hi
    o_ref[...] = x_ref[...] + g_ref[...] * jnp.concatenate([acc_lo, acc_hi], axis=1)


def _combine(y_kt, gates, x2, mod4, gate_slot, seq):
    t, d = x2.shape
    tm = COMBINE_TM
    tps = seq // tm
    return pl.pallas_call(
        _combine_body,
        out_shape=jax.ShapeDtypeStruct((t, d), F32),
        grid=(t // tm,),
        in_specs=[pl.BlockSpec((tm, LANES), lambda i: (i, 0)),
                  pl.BlockSpec((tm, d), lambda i: (i, 0)),
                  pl.BlockSpec((None, None, 1, d), lambda i: (i // tps, gate_slot, 0, 0)),
                  pl.BlockSpec((TOP_K, tm, d // 2), lambda i: (0, i, 0))],
        out_specs=pl.BlockSpec((tm, d), lambda i: (i, 0)),
        compiler_params=_cparams(("parallel",)),
        name="moe_combine",
    )(gates, x2, mod4, y_kt)


def _moe_layer(x2, gain, mod4, router_w, router_b, up_w, up_b, down_w, down_b, layer, seq):
    t, d = x2.shape
    rw_pad = jnp.pad(router_w[layer], ((0, 0), (0, LANES - N_EXPERTS)))
    rb_pad = jnp.pad(router_b[layer], (0, LANES - N_EXPERTS)).reshape(1, LANES)
    h2, idx, gates, rank, counts = _router(x2, gain, mod4, 3, 4, rw_pad, rb_pad, seq)

    bm = MOE_ROWS
    counts = counts[0, :N_EXPERTS].astype(jnp.int32)
    padded = (counts + bm - 1) // bm * bm
    p_ends = jnp.cumsum(padded)
    p_starts = p_ends - padded
    n_rows = t * TOP_K + N_EXPERTS * bm
    n_blk = n_rows // bm
    top_idx = idx[:, :TOP_K]
    onehot = top_idx[:, :, None] == jnp.arange(N_EXPERTS, dtype=jnp.int32)[None, None, :]
    dest = jnp.sum(jnp.where(onehot, p_starts[None, None, :], 0), axis=-1) + rank[:, :TOP_K]
    blk_start = jnp.arange(n_blk, dtype=jnp.int32) * bm
    blk_expert = jnp.sum((blk_start[:, None] >= p_ends[None, :]).astype(jnp.int32), axis=-1)
    blk_expert = jnp.minimum(blk_expert, N_EXPERTS - 1).astype(jnp.int32)
    n_used = (p_ends[-1:] // bm).astype(jnp.int32)

    xb = _sc_gather_rows(h2, _sc_row_tokens(dest.reshape(t * TOP_K), n_rows, t))
    f2 = up_w.shape[-1]
    y_rows = _expert_ffn(xb, blk_expert, n_used, up_w, up_b.reshape(up_b.shape[0], N_EXPERTS, 1, f2),
                         down_w, down_b.reshape(down_b.shape[0], N_EXPERTS, 1, d), layer)
    y_kt = _sc_gather_rows(y_rows, dest.T.reshape(TOP_K * t)).reshape(TOP_K, t, d // 2)
    return _combine(y_kt, gates, x2, mod4, 5, seq)


QKV_TM = 256


def _rope_tables(pos_col, invf_row):
    tm = pos_col.shape[0]
    ang = pos_col * invf_row
    c = jnp.cos(ang)
    s = jnp.sin(ang)
    d = lax.broadcasted_iota(jnp.int32, (tm, LANES), 1) % B_HEAD_DIM
    half = ROPE_DIM // 2
    cos_t = jnp.where(d < ROPE_DIM, c, 1.0)
    sin_lo = jnp.where(d < half, -s, 0.0)
    sin_hi = jnp.where(jnp.logical_and(d >= half, d < ROPE_DIM), s, 0.0)
    return cos_t, sin_lo, sin_hi


def _head_norm_rope(x, gain_row, tables):
    cos_t, sin_lo, sin_hi = tables
    lane = lax.broadcasted_iota(jnp.int32, x.shape, 1)
    sq = x * x
    first = lane < B_HEAD_DIM
    ss0 = jnp.sum(jnp.where(first, sq, 0.0), axis=-1, keepdims=True)
    ss1 = jnp.sum(jnp.where(first, 0.0, sq), axis=-1, keepdims=True)
    ms = jnp.where(first, ss0, ss1) * (1.0 / B_HEAD_DIM)
    xn = x * lax.rsqrt(ms + EPS) * gain_row
    half = ROPE_DIM // 2
    return (xn * cos_t + pltpu.roll(xn, LANES - half, 1) * sin_lo + pltpu.roll(xn, half, 1) * sin_hi)


def _qkv_body(x_ref, pos_ref, gq_ref, shq_ref, scq_ref, gkv_ref, shkv_ref, sckv_ref, wq_ref, wkv_ref,
              qg_ref, kg_ref, invf_ref, q_ref, k_ref, v_ref):
    x = x_ref[...]
    y = x * lax.rsqrt(jnp.mean(x * x, axis=-1, keepdims=True) + EPS)
    hq = ((y * gq_ref[...]) * (1.0 + scq_ref[...]) + shq_ref[...]).astype(BF16)
    hkv = ((y * gkv_ref[...]) * (1.0 + sckv_ref[...]) + shkv_ref[...]).astype(BF16)
    tables = _rope_tables(pos_ref[...].astype(F32), invf_ref[...])
    kv = jnp.dot(hkv, wkv_ref[...], preferred_element_type=F32)
    k_ref[...] = _head_norm_rope(kv[:, :LANES], kg_ref[...], tables).astype(BF16)
    v_ref[...] = kv[:, LANES:].astype(BF16)
    q = jnp.dot(hq, wq_ref[...], preferred_element_type=F32)
    scale = B_HEAD_DIM ** -0.5
    for p in range(q.shape[-1] // LANES):
        qp = _head_norm_rope(q[:, p * LANES:(p + 1) * LANES], qg_ref[...], tables)
        q_ref[:, p * LANES:(p + 1) * LANES] = (qp * scale).astype(BF16)


def _qkv(x2, pos2, gq, mod4, gkv, kvmod4, wq_bf, wkv_bf, qg2, kg2, invf, seq):
    t, d = x2.shape
    tm = QKV_TM
    tps = seq // tm
    nq = wq_bf.shape[-1]
    nkv = wkv_bf.shape[-1]
    row = lambda n: pl.BlockSpec((1, n), lambda i: (0, 0))
    modspec = lambda slot: pl.BlockSpec((None, None, 1, d), lambda i, slot=slot: (i // tps, slot, 0, 0))
    return pl.pallas_call(
        _qkv_body,
        out_shape=(jax.ShapeDtypeStruct((t, nq), BF16), jax.ShapeDtypeStruct((t, LANES), BF16),
                   jax.ShapeDtypeStruct((t, LANES), BF16)),
        grid=(t // tm,),
        in_specs=[pl.BlockSpec((tm, d), lambda i: (i, 0)),
                  pl.BlockSpec((tm, 1), lambda i: (i, 0)),
                  row(d), modspec(0), modspec(1),
                  row(d), modspec(0), modspec(1),
                  pl.BlockSpec((d, nq), lambda i: (0, 0)),
                  pl.BlockSpec((d, nkv), lambda i: (0, 0)),
                  row(LANES), row(LANES), row(LANES)],
        out_specs=(pl.BlockSpec((tm, nq), lambda i: (i, 0)),
                   pl.BlockSpec((tm, LANES), lambda i: (i, 0)),
                   pl.BlockSpec((tm, LANES), lambda i: (i, 0))),
        compiler_params=_cparams(("parallel",)),
        name="swa_qkv_proj",
    )(x2, pos2, gq, mod4, mod4, gkv, kvmod4, kvmod4, wq_bf, wkv_bf, qg2, kg2, invf)


def _attn_body(sink_ref, q_ref, kc_ref, kp_ref, vc_ref, vp_ref, o_ref):
    n = pl.program_id(1)
    w = WINDOW
    qi = lax.broadcasted_iota(jnp.int32, (w, 2 * w), 0) + w
    kj = lax.broadcasted_iota(jnp.int32, (w, 2 * w), 1)
    band = jnp.logical_and(kj <= qi, qi - kj < w)
    mask = jnp.logical_and(band, jnp.logical_or(n > 0, kj >= w))
    hd = B_HEAD_DIM
    lane = lax.broadcasted_iota(jnp.int32, (2 * w, LANES), 1)
    first = lane < hd
    kfull = jnp.concatenate([kp_ref[...], kc_ref[...]], axis=0).astype(F32)
    vfull = jnp.concatenate([vp_ref[...], vc_ref[...]], axis=0).astype(F32)
    placed = {}
    for g in range(B_KV_HEADS):
        for name, full in (("k", kfull), ("v", vfull)):
            swapped = pltpu.roll(full, hd, 1)
            own_first = full if g == 0 else swapped
            own_second = swapped if g == 0 else full
            placed[name, g, 0] = jnp.where(first, own_first, 0.0).astype(BF16)
            placed[name, g, 1] = jnp.where(first, 0.0, own_second).astype(BF16)
    pairs = B_Q_HEADS // 2
    heads = [(p, side) for p in range(pairs) for side in range(2)]
    scores = []
    for p, side in heads:
        g = (2 * p) // B_GROUP
        q_pair = q_ref[:, p * LANES:(p + 1) * LANES]
        s = lax.dot_general(q_pair, placed["k", g, side], (((1,), (1,)), ((), ())), preferred_element_type=F32)
        scores.append(jnp.where(mask, s, NEG_INF))
    probs, denoms = [], []
    for (p, side), s in zip(heads, scores):
        sink = sink_ref[0, 2 * p + side]
        m = jnp.maximum(jnp.max(jnp.maximum(s[:, :w], s[:, w:]), axis=-1, keepdims=True), sink)
        e = jnp.exp(s - m)
        denoms.append(jnp.sum(e[:, :w] + e[:, w:], axis=-1, keepdims=True) + jnp.exp(sink - m))
        probs.append(e.astype(BF16))
    outs = []
    for (p, side), pr, den in zip(heads, probs, denoms):
        g = (2 * p) // B_GROUP
        outs.append(jnp.dot(pr, placed["v", g, side], preferred_element_type=F32) / den)
    for p in range(pairs):
        o_ref[:, p * LANES:(p + 1) * LANES] = (outs[2 * p] + outs[2 * p + 1]).astype(BF16)


def _attention(q, k, v, sinks2, bsz, seq):
    t, nq = q.shape
    w = WINDOW
    nb = seq // w
    cur = lambda b, n: (b * nb + n, 0)
    prev = lambda b, n: (b * nb + jnp.maximum(n - 1, 0), 0)
    return pl.pallas_call(
        _attn_body,
        out_shape=jax.ShapeDtypeStruct((t, nq), BF16),
        grid=(bsz, nb),
        in_specs=[pl.BlockSpec(memory_space=pltpu.SMEM),
                  pl.BlockSpec((w, nq), cur),
                  pl.BlockSpec((w, LANES), cur), pl.BlockSpec((w, LANES), prev),
                  pl.BlockSpec((w, LANES), cur), pl.BlockSpec((w, LANES), prev)],
        out_specs=pl.BlockSpec((w, nq), cur),
        compiler_params=_cparams(("parallel", "parallel")),
        name="swa_sink_attention",
    )(sinks2, q, k, k, v, v)


def kernel(x, c, positions, ada_w, ada_b, norm_gain, a_w_in, a_conv, a_log, a_dt_bias, a_out_gain, a_w_out,
           kv_ada_w, kv_ada_b, kv_norm_gain, kv_w, k_norm_gain, b_w_q, q_norm_gain, b_sinks, b_w_out,
           router_w, router_b, up_w, up_b, down_w, down_b):
    bsz, seq, d = x.shape
    t = bsz * seq
    depth = ada_w.shape[0]
    x2 = x.reshape(t, d)
    ada_b3 = ada_b.reshape(depth, 1, 6 * d)

    for layer in range(depth):
        mod4 = _modulation(c, ada_w, ada_b3, layer).reshape(bsz, 6, 1, d)
        gain1 = norm_gain[layer, 0].reshape(1, d)
        gain2 = norm_gain[layer, 1].reshape(1, d)
        if layer < N_A_LAYERS:
            w_in = jnp.pad(a_w_in[layer].astype(BF16), ((0, 0), (0, A_PROJ_PAD - a_w_in.shape[-1])))
            proj = _in_proj(x2, gain1, mod4, w_in, a_conv[layer], seq)
            o = _gdn(proj, a_log[layer].reshape(1, A_HEADS),
                     a_dt_bias[layer].reshape(1, A_HEADS), a_out_gain[layer].reshape(1, A_HEAD_DIM),
                     bsz, seq)
            x2 = _out_proj_residual(o, a_w_out[layer].astype(BF16), x2, mod4, 2, seq)
        else:
            j = layer - N_A_LAYERS
            kvmod4 = _modulation(c, kv_ada_w.reshape(1, d, 2 * d), kv_ada_b.reshape(1, 1, 2 * d), 0)
            kvmod4 = kvmod4.reshape(bsz, 2, 1, d)
            half = ROPE_DIM // 2
            inv_freq = ROPE_THETA ** (-np.arange(0, ROPE_DIM, 2, dtype=np.float32) / ROPE_DIM)
            invf = jnp.asarray(np.tile(inv_freq.astype(np.float32), LANES // half).reshape(1, LANES))
            q, k, v = _qkv(x2, positions.reshape(t, 1), gain1, mod4, kv_norm_gain.reshape(1, d), kvmod4,
                           b_w_q[j].astype(BF16), kv_w.astype(BF16),
                           jnp.tile(q_norm_gain[j], 2).reshape(1, LANES),
                           jnp.tile(k_norm_gain, 2).reshape(1, LANES), invf, seq)
            o = _attention(q, k, v, b_sinks[j].reshape(1, B_Q_HEADS), bsz, seq)
            x2 = _out_proj_residual(o, b_w_out[j].astype(BF16), x2, mod4, 2, seq)
        x2 = _moe_layer(x2, gain2, mod4, router_w, router_b, up_w, up_b, down_w, down_b, layer, seq)
    return x2.reshape(bsz, seq, d)
```

```python
import dataclasses
import functools

import jax
import jax.numpy as jnp
import numpy as np
from jax import lax
from jax.experimental import pallas as pl
from jax.experimental.pallas import tpu as pltpu
from jax.experimental.pallas import tpu_sc as plsc

F32 = jnp.float32
BF16 = jnp.bfloat16
HIGHEST = lax.Precision.HIGHEST

D_MODEL = 1024
N_A_LAYERS = 1

A_HEAD_DIM = 128
A_HEADS = 8
A_WIDTH = 1024
A_CONV = 4
A_CHUNK = 64
A_PROJ_PAD = 4608

B_HEAD_DIM = 64
B_Q_HEADS = 16
B_KV_HEADS = 2
B_GROUP = 8
WINDOW = 128
ROPE_DIM = 16
ROPE_THETA = 500000.0

N_EXPERTS = 32
TOP_K = 4
D_FF = 1024
SWIGLU_LIMIT = 7.0
SWIGLU_ALPHA = 1.702
MOE_ROWS = 512

EPS = 1e-6
LANES = 128
NEG_INF = float("-inf")

VMEM_LIMIT = 56 * 1024 * 1024


def _cparams(sem, vmem=VMEM_LIMIT):
    return pltpu.CompilerParams(dimension_semantics=sem, vmem_limit_bytes=vmem)


def _silu(x):
    return x * jax.nn.sigmoid(x)


HI_HALF = -65536


def _pack_bf16_pairs(x):
    n = x.shape[-1] // 2
    bits = lax.bitcast_convert_type(x.astype(BF16).astype(F32), jnp.int32)
    return jnp.bitwise_or(lax.shift_right_logical(bits[:, :n], 16), jnp.bitwise_and(bits[:, n:], HI_HALF))


def _unpack_bf16_pairs(w):
    lo = lax.bitcast_convert_type(lax.shift_left(w, 16), F32)
    hi = lax.bitcast_convert_type(jnp.bitwise_and(w, HI_HALF), F32)
    return lo, hi


def _ada_norm(x, gain, shift, scale):
    y = x * lax.rsqrt(jnp.mean(x * x, axis=-1, keepdims=True) + EPS)
    return (y * gain) * (1.0 + scale) + shift


def _mod_body(c_ref, w_ref, b_ref, o_ref):
    o_ref[...] = jnp.dot(_silu(c_ref[...]), w_ref[...], preferred_element_type=F32,
                         precision=HIGHEST) + b_ref[...]


def _modulation(c, w3, b3, layer):
    bsz, d = c.shape
    n = w3.shape[-1]
    tn = 1024
    return pl.pallas_call(
        _mod_body,
        out_shape=jax.ShapeDtypeStruct((bsz, n), F32),
        grid=(n // tn,),
        in_specs=[pl.BlockSpec((bsz, d), lambda j: (0, 0)),
                  pl.BlockSpec((None, d, tn), lambda j: (layer, 0, j)),
                  pl.BlockSpec((None, 1, tn), lambda j: (layer, 0, j))],
        out_specs=pl.BlockSpec((bsz, tn), lambda j: (0, j)),
        compiler_params=_cparams(("arbitrary",)),
        name="adaln_mod",
    )(c, w3, b3)


INPROJ_TM = 256
INPROJ_TN = 256
HALO = 8


def _inproj_body(x_ref, gain_ref, sh_ref, sc_ref, w_ref, conv_ref, o_ref, ext, *, tiles_per_seq):
    i = pl.program_id(0)
    tm = x_ref.shape[0]
    hd = A_HEAD_DIM
    n = o_ref.shape[-1]
    tn = INPROJ_TN
    h = _ada_norm(x_ref[...], gain_ref[...], sh_ref[...], sc_ref[...]).astype(BF16)

    @pl.when(i % tiles_per_seq == 0)
    def _():
        ext[0:HALO, :] = jnp.zeros((HALO, ext.shape[-1]), F32)

    def conv_silu_norm(raw, n0):
        cols = slice(n0, n0 + tn)
        ext[HALO:HALO + tm, cols] = raw
        acc = raw * conv_ref[A_CONV - 1:A_CONV, cols]
        for s in range(1, A_CONV):
            acc = acc + ext[HALO - s:HALO - s + tm, cols] * conv_ref[A_CONV - 1 - s:A_CONV - s, cols]
        ext[0:HALO, cols] = raw[tm - HALO:, :]
        y = _silu(acc)
        if n0 >= 2 * A_WIDTH:
            return y
        scale = hd ** -0.5 if n0 < A_WIDTH else 1.0
        heads = []
        for c0 in range(0, tn, hd):
            yh = y[:, c0:c0 + hd]
            heads.append(yh * (lax.rsqrt(jnp.sum(yh * yh, axis=-1, keepdims=True) + EPS) * scale))
        return jnp.concatenate(heads, axis=1)

    matmul = lambda n0: jnp.dot(h, w_ref[:, n0:n0 + tn], preferred_element_type=F32)
    conv_chunks = list(range(0, 3 * A_WIDTH, tn))
    plain_chunks = list(range(3 * A_WIDTH, n, tn))
    every = max(1, len(conv_chunks) // max(1, len(plain_chunks)))
    raw = matmul(conv_chunks[0])
    for idx, n0 in enumerate(conv_chunks):
        nxt = matmul(conv_chunks[idx + 1]) if idx + 1 < len(conv_chunks) else None
        if idx % every == every - 1 and plain_chunks:
            p0 = plain_chunks.pop(0)
            o_ref[:, p0:p0 + tn] = matmul(p0)
        o_ref[:, n0:n0 + tn] = conv_silu_norm(raw, n0)
        raw = nxt
    for p0 in plain_chunks:
        o_ref[:, p0:p0 + tn] = matmul(p0)


def _in_proj(x2, gain, mod4, w_bf, conv_w, seq):
    t, d = x2.shape
    n = w_bf.shape[-1]
    tm = INPROJ_TM
    tps = seq // tm
    return pl.pallas_call(
        functools.partial(_inproj_body, tiles_per_seq=tps),
        out_shape=jax.ShapeDtypeStruct((t, n), F32),
        grid=(t // tm,),
        in_specs=[pl.BlockSpec((tm, d), lambda i: (i, 0)),
                  pl.BlockSpec((1, d), lambda i: (0, 0)),
                  pl.BlockSpec((None, None, 1, d), lambda i: (i // tps, 0, 0, 0)),
                  pl.BlockSpec((None, None, 1, d), lambda i: (i // tps, 1, 0, 0)),
                  pl.BlockSpec((d, n), lambda i: (0, 0)),
                  pl.BlockSpec(conv_w.shape, lambda i: (0, 0))],
        out_specs=pl.BlockSpec((tm, n), lambda i: (i, 0)),
        scratch_shapes=[pltpu.VMEM((HALO + tm, conv_w.shape[-1]), F32)],
        compiler_params=_cparams(("arbitrary",)),
        name="gdn_in_proj",
    )(x2, gain, mod4, mod4, w_bf, conv_w)


GDN_HG = 2
GDN_RB = 256
GDN_SC = 128
GDN_SUB = 4


def _split_bf16(x):
    hi = x.astype(BF16)
    return hi, (x - hi.astype(F32)).astype(BF16)


def _mm3(a_hi, a_lo, b_hi, b_lo, dims=(((1,), (0,)), ((), ()))):
    dg = functools.partial(lax.dot_general, dimension_numbers=dims, preferred_element_type=F32)
    return dg(a_hi, b_hi) + (dg(a_lo, b_hi) + dg(a_hi, b_lo))


def _dot3(a, b, dims):
    return _mm3(*_split_bf16(a), *_split_bf16(b), dims)


def _gdn_body(alog_ref, dtb_ref, q_ref, k_ref, v_ref, z_ref, ab_ref, og_ref,
              o_ref, bs, gcs, mp, bc, qp, op):
    seq = q_ref.shape[0]
    hd = A_HEAD_DIM
    ch = A_CHUNK
    g_idx = pl.program_id(1)

    lane = lax.broadcasted_iota(jnp.int32, (GDN_RB, hd), 1)
    row = lax.broadcasted_iota(jnp.int32, (GDN_RB, hd), 0)
    row_in_chunk = row % ch

    sc_rows = GDN_SC
    per_sc = sc_rows // ch
    rb_per_step = sc_rows * GDN_SUB // GDN_RB
    head_cols = lambda hh: slice(hh * hd, (hh + 1) * hd)

    def prep_stages(j):
        for rbi in range(rb_per_step):
            r = j * rb_per_step + rbi
            start = pl.multiple_of(r * GDN_RB, GDN_RB)
            rows = pl.ds(start, GDN_RB)
            for hh in range(GDN_HG):
                head = g_idx * GDN_HG + hh
                ab = ab_ref[rows, :]
                a_col = jnp.sum(jnp.where(lane == head, ab, 0.0), axis=-1, keepdims=True)
                b_col = jnp.sum(jnp.where(lane == head + A_HEADS, ab, 0.0), axis=-1, keepdims=True)
                bs[hh, rows, :] = jnp.broadcast_to(jax.nn.sigmoid(b_col), (GDN_RB, hd))
                xa = a_col + dtb_ref[0, head]
                softplus = jnp.maximum(xa, 0.0) + jnp.log1p(jnp.exp(-jnp.abs(xa)))
                g = jnp.broadcast_to(-jnp.exp(alog_ref[0, head]) * softplus, (GDN_RB, hd))
                s = 1
                while s < ch:
                    g = g + jnp.where(row_in_chunk >= s, pltpu.roll(g, s, 0), 0.0)
                    s *= 2
                gcs[hh, rows, :] = g
                yield

    ri = lax.broadcasted_iota(jnp.int32, (sc_rows, sc_rows), 0)
    cj = lax.broadcasted_iota(jnp.int32, (sc_rows, sc_rows), 1)
    same_chunk = (ri // ch) == (cj // ch)
    incl = jnp.logical_and(same_chunk, ri >= cj)
    strict = jnp.logical_and(same_chunk, ri > cj)
    eye = jnp.where(ri == cj, 1.0, 0.0)
    chunk_of_col = lax.broadcasted_iota(jnp.int32, (hd, sc_rows), 1) // ch
    lanes_nt = (((1,), (1,)), ((), ()))
    plain = (((1,), (0,)), ((), ()))

    chunks_per_step = GDN_SUB * per_sc
    n_steps = seq // (sc_rows * GDN_SUB)

    def local_stages(i):
        chains = [(i * GDN_SUB + sub, hh) for sub in range(GDN_SUB) for hh in range(GDN_HG)]
        rows_of = lambda blk: pl.ds(pl.multiple_of(blk * sc_rows, sc_rows), sc_rows)

        def start(blk, hh):
            rows = rows_of(blk)
            k = k_ref[rows, head_cols(hh)]
            gc = gcs[hh, rows, :]
            decay = jnp.where(incl, jnp.exp(jnp.minimum(gc - gc.T[0:1, :], 0.0)), 0.0)
            k_bf = k.astype(BF16)
            kk = lax.dot_general((k * bs[hh, rows, :]).astype(BF16), k_bf, lanes_nt, preferred_element_type=F32)
            qk = lax.dot_general(q_ref[rows, head_cols(hh)].astype(BF16), k_bf, lanes_nt, preferred_element_type=F32)
            return jnp.where(strict, -(kk * decay), 0.0), (qk * decay).astype(BF16)

        started = [start(*c) for c in chains]
        yield
        qks = [s[1] for s in started]
        invs = [eye + s[0] for s in started]
        pws = [_dot3(s[0], s[0], plain) for s in started]
        yield
        for step in range(1, 6):
            for n in range(len(chains)):
                pw_hi, pw_lo = _split_bf16(pws[n])
                inv_hi, inv_lo = _split_bf16(invs[n])
                if step < 5:
                    prod = _mm3(jnp.concatenate([pw_hi, inv_hi], axis=0),
                                jnp.concatenate([pw_lo, inv_lo], axis=0), pw_hi, pw_lo)
                    pws[n] = prod[:sc_rows]
                    invs[n] = invs[n] + prod[sc_rows:]
                else:
                    invs[n] = invs[n] + _mm3(inv_hi, inv_lo, pw_hi, pw_lo)
            yield

        def solve(n, blk, hh):
            rows = rows_of(blk)
            beta = bs[hh, rows, :]
            rhs = jnp.concatenate([v_ref[rows, head_cols(hh)] * beta, k_ref[rows, head_cols(hh)] * beta * jnp.exp(gcs[hh, rows, :])], axis=1)
            return _dot3(invs[n], rhs, plain).astype(BF16)

        uws = [solve(n, *c) for n, c in enumerate(chains)]
        yield

        def finish(n, blk, hh):
            rows = rows_of(blk)
            k = k_ref[rows, head_cols(hh)]
            gc = gcs[hh, rows, :]
            res = jnp.dot(qks[n], uws[n], preferred_element_type=F32)
            op[hh, rows, :] = res[:, :hd]
            qp[hh, rows, :] = (q_ref[rows, head_cols(hh)] * jnp.exp(gc) - res[:, hd:]).astype(BF16)
            gl = jnp.concatenate(
                [jnp.broadcast_to(gc[(j + 1) * ch - 1:(j + 1) * ch, :], (ch, hd)) for j in range(per_sc)], axis=0)
            kt_t = (k * jnp.exp(gl - gc)).T
            for j in range(per_sc):
                kt_j = jnp.where(chunk_of_col == j, kt_t, 0.0).astype(BF16)
                bm = jnp.dot(kt_j, uws[n], preferred_element_type=F32)
                bc[hh, blk * per_sc + j] = bm[:, :hd]
                mp[hh, blk * per_sc + j] = bm[:, hd:].astype(BF16)

        for n, c in enumerate(chains):
            finish(n, *c)

    def scan_stages(j, states):
        for cc in range(chunks_per_step):
            c = j * chunks_per_step + cc
            r0 = pl.multiple_of(c * ch, ch)
            rows = pl.ds(r0, ch)
            for hh in range(GDN_HG):
                col0 = hh * hd
                state = states[hh]
                s_bf = state.astype(BF16)
                o = jnp.dot(qp[hh, rows, :], s_bf, preferred_element_type=F32) + op[hh, rows, :]
                g_tot = jnp.exp(gcs[hh, pl.ds(r0 + ch - 1, 1), :])
                states[hh] = state * g_tot - jnp.dot(mp[hh, c], s_bf, preferred_element_type=F32) + bc[hh, c]
                on = o * lax.rsqrt(jnp.mean(o * o, axis=-1, keepdims=True) + EPS) * og_ref[...]
                z = z_ref[rows, col0:col0 + hd]
                o_ref[rows, col0:col0 + hd] = (on * _silu(z)).astype(BF16)
            yield

    def interleave(*gens):
        live = list(gens)
        while live:
            for g in list(live):
                try:
                    next(g)
                except StopIteration:
                    live.remove(g)

    assert n_steps >= 2
    interleave(prep_stages(0))
    interleave(local_stages(0), prep_stages(1))

    def pipelined(i, states):
        states = list(states)
        interleave(local_stages(i), scan_stages(i - 1, states), prep_stages(i + 1))
        return tuple(states)

    states = list(lax.fori_loop(1, n_steps - 1, pipelined,
                                tuple(jnp.zeros((hd, hd), F32) for _ in range(GDN_HG))))
    interleave(local_stages(n_steps - 1), scan_stages(n_steps - 2, states))
    interleave(scan_stages(n_steps - 1, states))


def _gdn(proj, a_log, dt_bias, out_gain, bsz, seq):
    t = proj.shape[0]
    hd = A_HEAD_DIM
    wb = hd * GDN_HG
    ng = A_HEADS // GDN_HG
    per = A_WIDTH // wb
    n_chunks = seq // A_CHUNK
    smem = pl.BlockSpec(memory_space=pltpu.SMEM)
    seq_spec = lambda off: pl.BlockSpec((seq, wb), lambda b, g, off=off: (b, off + g))
    sc = lambda dt=F32: pltpu.VMEM((GDN_HG, seq, hd), dt)
    return pl.pallas_call(
        _gdn_body,
        out_shape=jax.ShapeDtypeStruct((t, A_WIDTH), BF16),
        grid=(bsz, ng),
        in_specs=[smem, smem,
                  seq_spec(0), seq_spec(per), seq_spec(2 * per), seq_spec(3 * per),
                  pl.BlockSpec((seq, LANES), lambda b, g: (b, 4 * A_WIDTH // LANES)),
                  pl.BlockSpec((1, hd), lambda b, g: (0, 0))],
        out_specs=pl.BlockSpec((seq, wb), lambda b, g: (b, g)),
        scratch_shapes=[sc(), sc(),
                        pltpu.VMEM((GDN_HG, n_chunks, hd, hd), BF16),
                        pltpu.VMEM((GDN_HG, n_chunks, hd, hd), F32),
                        sc(BF16), sc()],
        compiler_params=_cparams(("parallel", "parallel")),
        name="gdn_core",
    )(a_log, dt_bias, proj, proj, proj, proj, proj, out_gain)


def _oproj_body(a_ref, w_ref, x_ref, g_ref, o_ref):
    y = jnp.dot(a_ref[...], w_ref[...], preferred_element_type=F32)
    o_ref[...] = x_ref[...] + g_ref[...] * y


def _out_proj_residual(a_bf, w_bf, x2, mod4, gate_slot, seq):
    t, d = x2.shape
    kdim = a_bf.shape[-1]
    tm = 512
    tps = seq // tm
    return pl.pallas_call(
        _oproj_body,
        out_shape=jax.ShapeDtypeStruct((t, d), F32),
        grid=(t // tm,),
        in_specs=[pl.BlockSpec((tm, kdim), lambda i: (i, 0)),
                  pl.BlockSpec((kdim, d), lambda i: (0, 0)),
                  pl.BlockSpec((tm, d), lambda i: (i, 0)),
                  pl.BlockSpec((None, None, 1, d), lambda i: (i // tps, gate_slot, 0, 0))],
        out_specs=pl.BlockSpec((tm, d), lambda i: (i, 0)),
        compiler_params=_cparams(("parallel",)),
        name="out_proj_residual",
    )(a_bf, w_bf, x2, mod4)


ROUTER_TM = 512


def _router_body(x_ref, gain_ref, sh_ref, sc_ref, rw_ref, rb_ref, h_ref, idx_ref, gate_ref, rank_ref,
                 cnt_ref, carry):
    i = pl.program_id(0)

    @pl.when(i == 0)
    def _():
        carry[...] = jnp.zeros_like(carry)

    tm = x_ref.shape[0]
    h = _ada_norm(x_ref[...], gain_ref[...], sh_ref[...], sc_ref[...])
    h_ref[...] = _pack_bf16_pairs(h)
    lane = lax.broadcasted_iota(jnp.int32, (tm, LANES), 1)
    logits = _dot3(h, rw_ref[...], (((1,), (0,)), ((), ()))) + rb_ref[...]
    logits = jnp.where(lane < N_EXPERTS, logits, NEG_INF)
    tops, sels = [], []
    work = logits
    for _ in range(TOP_K):
        m = jnp.max(work, axis=-1, keepdims=True)
        sel = jnp.min(jnp.where(work == m, lane, LANES), axis=-1, keepdims=True)
        work = jnp.where(lane == sel, NEG_INF, work)
        tops.append(m)
        sels.append(sel)
    exps = [jnp.exp(m - tops[0]) for m in tops]
    denom = exps[0] + exps[1] + exps[2] + exps[3]
    onehot = jnp.zeros((tm, LANES), F32)
    for sel in sels:
        onehot = onehot + (lane == sel).astype(F32)
    ti = lax.broadcasted_iota(jnp.int32, (tm, tm), 0)
    tj = lax.broadcasted_iota(jnp.int32, (tm, tm), 1)
    tri = (ti > tj).astype(BF16)
    before = jnp.dot(tri, onehot.astype(BF16), preferred_element_type=F32) + carry[...]
    idx_out = jnp.zeros((tm, LANES), jnp.int32)
    gate_out = jnp.zeros((tm, LANES), F32)
    rank_out = jnp.zeros((tm, LANES), jnp.int32)
    for k in range(TOP_K):
        rank_k = jnp.sum(jnp.where(lane == sels[k], before, 0.0), axis=-1, keepdims=True)
        idx_out = jnp.where(lane == k, sels[k], idx_out)
        gate_out = jnp.where(lane == k, exps[k] / denom, gate_out)
        rank_out = jnp.where(lane == k, rank_k.astype(jnp.int32), rank_out)
    idx_ref[...] = idx_out
    gate_ref[...] = gate_out
    rank_ref[...] = rank_out
    carry[...] = carry[...] + jnp.sum(onehot, axis=0, keepdims=True)
    cnt_ref[...] = carry[...]


def _router(x2, gain, mod4, sh_slot, sc_slot, rw_pad, rb_pad, seq):
    t, d = x2.shape
    tm = ROUTER_TM
    tps = seq // tm
    tok = lambda dt: jax.ShapeDtypeStruct((t, LANES), dt)
    tok_spec = pl.BlockSpec((tm, LANES), lambda i: (i, 0))
    return pl.pallas_call(
        _router_body,
        out_shape=(jax.ShapeDtypeStruct((t, d // 2), jnp.int32), tok(jnp.int32), tok(F32), tok(jnp.int32),
                   jax.ShapeDtypeStruct((1, LANES), F32)),
        grid=(t // tm,),
        in_specs=[pl.BlockSpec((tm, d), lambda i: (i, 0)),
                  pl.BlockSpec((1, d), lambda i: (0, 0)),
                  pl.BlockSpec((None, None, 1, d), lambda i: (i // tps, sh_slot, 0, 0)),
                  pl.BlockSpec((None, None, 1, d), lambda i: (i // tps, sc_slot, 0, 0)),
                  pl.BlockSpec((d, LANES), lambda i: (0, 0)),
                  pl.BlockSpec((1, LANES), lambda i: (0, 0))],
        out_specs=(pl.BlockSpec((tm, d // 2), lambda i: (i, 0)), tok_spec, tok_spec, tok_spec,
                   pl.BlockSpec((1, LANES), lambda i: (0, 0))),
        scratch_shapes=[pltpu.VMEM((1, LANES), F32)],
        compiler_params=_cparams(("arbitrary",)),
        name="moe_router",
    )(x2, gain, mod4, mod4, rw_pad, rb_pad)


SC_LANES = 16
SC_INDEX_CHUNK = 8192


def _sc_row_tokens(dest_flat, n_rows, n_tok):
    n_assign = dest_flat.shape[0]
    assert n_rows % SC_LANES == 0 and n_assign % SC_INDEX_CHUNK == 0
    n_cores = plsc.get_sparse_core_info().num_cores
    mesh = plsc.VectorSubcoreMesh(core_axis_name="c", subcore_axis_name="s")

    @functools.partial(
        pl.kernel, mesh=mesh,
        out_type=jax.ShapeDtypeStruct((n_rows,), jnp.int32),
        scratch_types=[pltpu.VMEM((n_rows,), jnp.int32), pltpu.VMEM((SC_INDEX_CHUNK,), jnp.int32)],
        compiler_params=dataclasses.replace(pltpu.CompilerParams(), needs_layout_passes=False),
    )
    def row_token_kernel(dest_hbm, out_hbm, rt_v, d_v):
        wid = lax.axis_index("s") * n_cores + lax.axis_index("c")

        @pl.when(wid == 0)
        def _():
            lanes = lax.iota(jnp.int32, SC_LANES)

            @pl.loop(0, n_rows // SC_LANES)
            def _(i):
                rt_v[pl.ds(i * SC_LANES, SC_LANES)] = lax.rem(lanes + i * SC_LANES, n_tok)

            @pl.loop(0, n_assign // SC_INDEX_CHUNK)
            def _(c):
                pltpu.sync_copy(dest_hbm.at[pl.ds(c * SC_INDEX_CHUNK, SC_INDEX_CHUNK)], d_v)

                @pl.loop(0, SC_INDEX_CHUNK // SC_LANES)
                def _(i):
                    idx = d_v[pl.ds(i * SC_LANES, SC_LANES)]
                    tok = lax.div(lanes + (c * SC_INDEX_CHUNK + i * SC_LANES), TOP_K)
                    plsc.store_scatter(rt_v, [idx], tok)

            pltpu.sync_copy(rt_v, out_hbm)

    return row_token_kernel(dest_flat)


def _ffn_body(be_ref, nu_ref, x_ref, uw_ref, ub_ref, dw_ref, db_ref, y_ref, uw_bf, dw_bf):
    i = pl.program_id(0)
    changed = jnp.logical_or(i == 0, be_ref[i] != be_ref[jnp.maximum(i - 1, 0)])
    active = i < nu_ref[0]

    @pl.when(jnp.logical_and(changed, active))
    def _():
        rows = 128
        for r0 in range(0, uw_ref.shape[0], rows):
            uw_bf[r0:r0 + rows, :] = uw_ref[r0:r0 + rows, :].astype(BF16)
        for r0 in range(0, dw_ref.shape[0], rows):
            dw_bf[r0:r0 + rows, :] = dw_ref[r0:r0 + rows, :].astype(BF16)

    @pl.when(active)
    def _():
        x = jnp.concatenate(_unpack_bf16_pairs(x_ref[...]), axis=1).astype(BF16)
        gu = jnp.dot(x, uw_bf[...], preferred_element_type=F32) + ub_ref[...]
        gate = jnp.minimum(gu[:, :D_FF], SWIGLU_LIMIT)
        lin = jnp.clip(gu[:, D_FF:], -SWIGLU_LIMIT, SWIGLU_LIMIT)
        act = gate * jax.nn.sigmoid(SWIGLU_ALPHA * gate) * (lin + 1.0)
        y = jnp.dot(act.astype(BF16), dw_bf[...], preferred_element_type=F32) + db_ref[...]
        y_ref[...] = _pack_bf16_pairs(y)

    @pl.when(jnp.logical_not(active))
    def _():
        y_ref[...] = jnp.zeros_like(y_ref)


def _expert_ffn(xb, blk_expert, n_used, up_w, up_b4, down_w, down_b4, layer):
    n_rows, dp = xb.shape
    d = 2 * dp
    bm = MOE_ROWS
    n_blk = n_rows // bm
    f2 = up_w.shape[-1]
    grid_spec = pltpu.PrefetchScalarGridSpec(
        num_scalar_prefetch=2,
        grid=(n_blk,),
        in_specs=[pl.BlockSpec((bm, dp), lambda i, be, nu: (i, 0)),
                  pl.BlockSpec((None, None, d, f2), lambda i, be, nu: (layer, be[i], 0, 0)),
                  pl.BlockSpec((None, None, 1, f2), lambda i, be, nu: (layer, be[i], 0, 0)),
                  pl.BlockSpec((None, None, f2 // 2, d), lambda i, be, nu: (layer, be[i], 0, 0)),
                  pl.BlockSpec((None, None, 1, d), lambda i, be, nu: (layer, be[i], 0, 0))],
        out_specs=pl.BlockSpec((bm, dp), lambda i, be, nu: (i, 0)),
        scratch_shapes=[pltpu.VMEM((d, f2), BF16), pltpu.VMEM((f2 // 2, d), BF16)],
    )
    return pl.pallas_call(
        _ffn_body,
        out_shape=jax.ShapeDtypeStruct((n_rows, dp), jnp.int32),
        grid_spec=grid_spec,
        compiler_params=_cparams(("arbitrary",)),
        name="moe_expert_ffn",
    )(blk_expert, n_used, xb, up_w, up_b4, down_w, down_b4)


SC_GATHER_ROWS = 64


def _sc_gather_rows(table, idx):
    n, d = idx.shape[0], table.shape[1]
    info = plsc.get_sparse_core_info()
    n_cores, n_sub = info.num_cores, info.num_subcores
    n_workers = n_cores * n_sub
    chunk = SC_GATHER_ROWS
    per_worker = n // n_workers
    n_chunks = per_worker // chunk
    assert n_chunks * chunk * n_workers == n and n_chunks % 2 == 0
    mesh = plsc.VectorSubcoreMesh(core_axis_name="c", subcore_axis_name="s")

    @functools.partial(
        pl.kernel, mesh=mesh,
        out_type=jax.ShapeDtypeStruct((n, d), table.dtype),
        scratch_types=[pltpu.VMEM((n_chunks, chunk), jnp.int32), pltpu.VMEM((2, chunk, d), table.dtype),
                       pltpu.SemaphoreType.DMA((2,))],
    )
    def gather_kernel(table_hbm, idx_hbm, out_hbm, idx_v, rows_v, sems):
        wid = lax.axis_index("s") * n_cores + lax.axis_index("c")
        base = wid * per_worker
        pltpu.sync_copy(idx_hbm.at[wid], idx_v)

        def gather(ci, slot):
            return pltpu.make_async_copy(table_hbm.at[idx_v.at[ci]], rows_v.at[slot], sems.at[slot])

        gather(0, 0).start()

        @pl.loop(0, n_chunks, step=2)
        def _(c0):
            for slot in range(2):
                ci = c0 + slot
                gather(ci, slot).wait()

                @pl.when(ci + 1 < n_chunks)
                def _():
                    gather(ci + 1, 1 - slot).start()

                pltpu.sync_copy(rows_v.at[slot], out_hbm.at[pl.ds(base + ci * chunk, chunk)])

    return gather_kernel(table, idx.reshape(n_workers, n_chunks, chunk))


COMBINE_TM = 256


def _combine_body(gate_ref, x_ref, g_ref, y_ref, o_ref):
    gates = gate_ref[...]
    acc_lo = acc_hi = None
    for k in range(TOP_K):
        lo, hi = _unpack_bf16_pairs(y_ref[k])
        gk = gates[:, k:k + 1]
        acc_lo = gk * lo if acc_lo is None else acc_lo + gk * lo
        acc_hi = gk * hi if acc_hi is None else acc_hi + gk * hi
    o_ref[...] = x_ref[...] + g_ref[...] * jnp.concatenate([acc_lo, acc_hi], axis=1)


def _combine(y_kt, gates, x2, mod4, gate_slot, seq):
    t, d = x2.shape
    tm = COMBINE_TM
    tps = seq // tm
    return pl.pallas_call(
        _combine_body,
        out_shape=jax.ShapeDtypeStruct((t, d), F32),
        grid=(t // tm,),
        in_specs=[pl.BlockSpec((tm, LANES), lambda i: (i, 0)),
                  pl.BlockSpec((tm, d), lambda i: (i, 0)),
                  pl.BlockSpec((None, None, 1, d), lambda i: (i // tps, gate_slot, 0, 0)),
                  pl.BlockSpec((TOP_K, tm, d // 2), lambda i: (0, i, 0))],
        out_specs=pl.BlockSpec((tm, d), lambda i: (i, 0)),
        compiler_params=_cparams(("parallel",)),
        name="moe_combine",
    )(gates, x2, mod4, y_kt)


def _moe_layer(x2, gain, mod4, router_w, router_b, up_w, up_b, down_w, down_b, layer, seq):
    t, d = x2.shape
    rw_pad = jnp.pad(router_w[layer], ((0, 0), (0, LANES - N_EXPERTS)))
    rb_pad = jnp.pad(router_b[layer], (0, LANES - N_EXPERTS)).reshape(1, LANES)
    h2, idx, gates, rank, counts = _router(x2, gain, mod4, 3, 4, rw_pad, rb_pad, seq)

    bm = MOE_ROWS
    counts = counts[0, :N_EXPERTS].astype(jnp.int32)
    padded = (counts + bm - 1) // bm * bm
    p_ends = jnp.cumsum(padded)
    p_starts = p_ends - padded
    n_rows = t * TOP_K + N_EXPERTS * bm
    n_blk = n_rows // bm
    top_idx = idx[:, :TOP_K]
    onehot = top_idx[:, :, None] == jnp.arange(N_EXPERTS, dtype=jnp.int32)[None, None, :]
    dest = jnp.sum(jnp.where(onehot, p_starts[None, None, :], 0), axis=-1) + rank[:, :TOP_K]
    blk_start = jnp.arange(n_blk, dtype=jnp.int32) * bm
    blk_expert = jnp.sum((blk_start[:, None] >= p_ends[None, :]).astype(jnp.int32), axis=-1)
    blk_expert = jnp.minimum(blk_expert, N_EXPERTS - 1).astype(jnp.int32)
    n_used = (p_ends[-1:] // bm).astype(jnp.int32)

    xb = _sc_gather_rows(h2, _sc_row_tokens(dest.reshape(t * TOP_K), n_rows, t))
    f2 = up_w.shape[-1]
    y_rows = _expert_ffn(xb, blk_expert, n_used, up_w, up_b.reshape(up_b.shape[0], N_EXPERTS, 1, f2),
                         down_w, down_b.reshape(down_b.shape[0], N_EXPERTS, 1, d), layer)
    y_kt = _sc_gather_rows(y_rows, dest.T.reshape(TOP_K * t)).reshape(TOP_K, t, d // 2)
    return _combine(y_kt, gates, x2, mod4, 5, seq)


QKV_TM = 256


def _rope_tables(pos_col, invf_row):
    tm = pos_col.shape[0]
    ang = pos_col * invf_row
    c = jnp.cos(ang)
    s = jnp.sin(ang)
    d = lax.broadcasted_iota(jnp.int32, (tm, LANES), 1) % B_HEAD_DIM
    half = ROPE_DIM // 2
    cos_t = jnp.where(d < ROPE_DIM, c, 1.0)
    sin_lo = jnp.where(d < half, -s, 0.0)
    sin_hi = jnp.where(jnp.logical_and(d >= half, d < ROPE_DIM), s, 0.0)
    return cos_t, sin_lo, sin_hi


def _head_norm_rope(x, gain_row, tables):
    cos_t, sin_lo, sin_hi = tables
    lane = lax.broadcasted_iota(jnp.int32, x.shape, 1)
    sq = x * x
    first = lane < B_HEAD_DIM
    ss0 = jnp.sum(jnp.where(first, sq, 0.0), axis=-1, keepdims=True)
    ss1 = jnp.sum(jnp.where(first, 0.0, sq), axis=-1, keepdims=True)
    ms = jnp.where(first, ss0, ss1) * (1.0 / B_HEAD_DIM)
    xn = x * lax.rsqrt(ms + EPS) * gain_row
    half = ROPE_DIM // 2
    return (xn * cos_t + pltpu.roll(xn, LANES - half, 1) * sin_lo + pltpu.roll(xn, half, 1) * sin_hi)


def _qkv_body(x_ref, pos_ref, gq_ref, shq_ref, scq_ref, gkv_ref, shkv_ref, sckv_ref, wq_ref, wkv_ref,
              qg_ref, kg_ref, invf_ref, q_ref, k_ref, v_ref):
    x = x_ref[...]
    y = x * lax.rsqrt(jnp.mean(x * x, axis=-1, keepdims=True) + EPS)
    hq = ((y * gq_ref[...]) * (1.0 + scq_ref[...]) + shq_ref[...]).astype(BF16)
    hkv = ((y * gkv_ref[...]) * (1.0 + sckv_ref[...]) + shkv_ref[...]).astype(BF16)
    tables = _rope_tables(pos_ref[...].astype(F32), invf_ref[...])
    kv = jnp.dot(hkv, wkv_ref[...], preferred_element_type=F32)
    k_ref[...] = _head_norm_rope(kv[:, :LANES], kg_ref[...], tables).astype(BF16)
    v_ref[...] = kv[:, LANES:].astype(BF16)
    q = jnp.dot(hq, wq_ref[...], preferred_element_type=F32)
    scale = B_HEAD_DIM ** -0.5
    for p in range(q.shape[-1] // LANES):
        qp = _head_norm_rope(q[:, p * LANES:(p + 1) * LANES], qg_ref[...], tables)
        q_ref[:, p * LANES:(p + 1) * LANES] = (qp * scale).astype(BF16)


def _qkv(x2, pos2, gq, mod4, gkv, kvmod4, wq_bf, wkv_bf, qg2, kg2, invf, seq):
    t, d = x2.shape
    tm = QKV_TM
    tps = seq // tm
    nq = wq_bf.shape[-1]
    nkv = wkv_bf.shape[-1]
    row = lambda n: pl.BlockSpec((1, n), lambda i: (0, 0))
    modspec = lambda slot: pl.BlockSpec((None, None, 1, d), lambda i, slot=slot: (i // tps, slot, 0, 0))
    return pl.pallas_call(
        _qkv_body,
        out_shape=(jax.ShapeDtypeStruct((t, nq), BF16), jax.ShapeDtypeStruct((t, LANES), BF16),
                   jax.ShapeDtypeStruct((t, LANES), BF16)),
        grid=(t // tm,),
        in_specs=[pl.BlockSpec((tm, d), lambda i: (i, 0)),
                  pl.BlockSpec((tm, 1), lambda i: (i, 0)),
                  row(d), modspec(0), modspec(1),
                  row(d), modspec(0), modspec(1),
                  pl.BlockSpec((d, nq), lambda i: (0, 0)),
                  pl.BlockSpec((d, nkv), lambda i: (0, 0)),
                  row(LANES), row(LANES), row(LANES)],
        out_specs=(pl.BlockSpec((tm, nq), lambda i: (i, 0)),
                   pl.BlockSpec((tm, LANES), lambda i: (i, 0)),
                   pl.BlockSpec((tm, LANES), lambda i: (i, 0))),
        compiler_params=_cparams(("parallel",)),
        name="swa_qkv_proj",
    )(x2, pos2, gq, mod4, mod4, gkv, kvmod4, kvmod4, wq_bf, wkv_bf, qg2, kg2, invf)


def _attn_body(sink_ref, q_ref, kc_ref, kp_ref, vc_ref, vp_ref, o_ref):
    n = pl.program_id(1)
    w = WINDOW
    qi = lax.broadcasted_iota(jnp.int32, (w, 2 * w), 0) + w
    kj = lax.broadcasted_iota(jnp.int32, (w, 2 * w), 1)
    band = jnp.logical_and(kj <= qi, qi - kj < w)
    mask = jnp.logical_and(band, jnp.logical_or(n > 0, kj >= w))
    hd = B_HEAD_DIM
    lane = lax.broadcasted_iota(jnp.int32, (2 * w, LANES), 1)
    first = lane < hd
    kfull = jnp.concatenate([kp_ref[...], kc_ref[...]], axis=0).astype(F32)
    vfull = jnp.concatenate([vp_ref[...], vc_ref[...]], axis=0).astype(F32)
    placed = {}
    for g in range(B_KV_HEADS):
        for name, full in (("k", kfull), ("v", vfull)):
            swapped = pltpu.roll(full, hd, 1)
            own_first = full if g == 0 else swapped
            own_second = swapped if g == 0 else full
            placed[name, g, 0] = jnp.where(first, own_first, 0.0).astype(BF16)
            placed[name, g, 1] = jnp.where(first, 0.0, own_second).astype(BF16)
    pairs = B_Q_HEADS // 2
    heads = [(p, side) for p in range(pairs) for side in range(2)]
    scores = []
    for p, side in heads:
        g = (2 * p) // B_GROUP
        q_pair = q_ref[:, p * LANES:(p + 1) * LANES]
        s = lax.dot_general(q_pair, placed["k", g, side], (((1,), (1,)), ((), ())), preferred_element_type=F32)
        scores.append(jnp.where(mask, s, NEG_INF))
    probs, denoms = [], []
    for (p, side), s in zip(heads, scores):
        sink = sink_ref[0, 2 * p + side]
        m = jnp.maximum(jnp.max(jnp.maximum(s[:, :w], s[:, w:]), axis=-1, keepdims=True), sink)
        e = jnp.exp(s - m)
        denoms.append(jnp.sum(e[:, :w] + e[:, w:], axis=-1, keepdims=True) + jnp.exp(sink - m))
        probs.append(e.astype(BF16))
    outs = []
    for (p, side), pr, den in zip(heads, probs, denoms):
        g = (2 * p) // B_GROUP
        outs.append(jnp.dot(pr, placed["v", g, side], preferred_element_type=F32) / den)
    for p in range(pairs):
        o_ref[:, p * LANES:(p + 1) * LANES] = (outs[2 * p] + outs[2 * p + 1]).astype(BF16)


def _attention(q, k, v, sinks2, bsz, seq):
    t, nq = q.shape
    w = WINDOW
    nb = seq // w
    cur = lambda b, n: (b * nb + n, 0)
    prev = lambda b, n: (b * nb + jnp.maximum(n - 1, 0), 0)
    return pl.pallas_call(
        _attn_body,
        out_shape=jax.ShapeDtypeStruct((t, nq), BF16),
        grid=(bsz, nb),
        in_specs=[pl.BlockSpec(memory_space=pltpu.SMEM),
                  pl.BlockSpec((w, nq), cur),
                  pl.BlockSpec((w, LANES), cur), pl.BlockSpec((w, LANES), prev),
                  pl.BlockSpec((w, LANES), cur), pl.BlockSpec((w, LANES), prev)],
        out_specs=pl.BlockSpec((w, nq), cur),
        compiler_params=_cparams(("parallel", "parallel")),
        name="swa_sink_attention",
    )(sinks2, q, k, k, v, v)


def kernel(x, c, positions, ada_w, ada_b, norm_gain, a_w_in, a_conv, a_log, a_dt_bias, a_out_gain, a_w_out,
           kv_ada_w, kv_ada_b, kv_norm_gain, kv_w, k_norm_gain, b_w_q, q_norm_gain, b_sinks, b_w_out,
           router_w, router_b, up_w, up_b, down_w, down_b):
    bsz, seq, d = x.shape
    t = bsz * seq
    depth = ada_w.shape[0]
    x2 = x.reshape(t, d)
    ada_b3 = ada_b.reshape(depth, 1, 6 * d)

    for layer in range(depth):
        mod4 = _modulation(c, ada_w, ada_b3, layer).reshape(bsz, 6, 1, d)
        gain1 = norm_gain[layer, 0].reshape(1, d)
        gain2 = norm_gain[layer, 1].reshape(1, d)
        if layer < N_A_LAYERS:
            w_in = jnp.pad(a_w_in[layer].astype(BF16), ((0, 0), (0, A_PROJ_PAD - a_w_in.shape[-1])))
            proj = _in_proj(x2, gain1, mod4, w_in, a_conv[layer], seq)
            o = _gdn(proj, a_log[layer].reshape(1, A_HEADS),
                     a_dt_bias[layer].reshape(1, A_HEADS), a_out_gain[layer].reshape(1, A_HEAD_DIM),
                     bsz, seq)
            x2 = _out_proj_residual(o, a_w_out[layer].astype(BF16), x2, mod4, 2, seq)
        else:
            j = layer - N_A_LAYERS
            kvmod4 = _modulation(c, kv_ada_w.reshape(1, d, 2 * d), kv_ada_b.reshape(1, 1, 2 * d), 0)
            kvmod4 = kvmod4.reshape(bsz, 2, 1, d)
            half = ROPE_DIM // 2
            inv_freq = ROPE_THETA ** (-np.arange(0, ROPE_DIM, 2, dtype=np.float32) / ROPE_DIM)
            invf = jnp.asarray(np.tile(inv_freq.astype(np.float32), LANES // half).reshape(1, LANES))
            q, k, v = _qkv(x2, positions.reshape(t, 1), gain1, mod4, kv_norm_gain.reshape(1, d), kvmod4,
                           b_w_q[j].astype(BF16), kv_w.astype(BF16),
                           jnp.tile(q_norm_gain[j], 2).reshape(1, LANES),
                           jnp.tile(k_norm_gain, 2).reshape(1, LANES), invf, seq)
            o = _attention(q, k, v, b_sinks[j].reshape(1, B_Q_HEADS), bsz, seq)
            x2 = _out_proj_residual(o, b_w_out[j].astype(BF16), x2, mod4, 2, seq)
        x2 = _moe_layer(x2, gain2, mod4, router_w, router_b, up_w, up_b, down_w, down_b, layer, seq)
    return x2.reshape(bsz, seq, d)
```

```python
import dataclasses
import functools

import jax
import jax.numpy as jnp
import numpy as np
from jax import lax
from jax.experimental import pallas as pl
from jax.experimental.pallas import tpu as pltpu
from jax.experimental.pallas import tpu_sc as plsc

F32 = jnp.float32
BF16 = jnp.bfloat16
HIGHEST = lax.Precision.HIGHEST

D_MODEL = 1024
N_A_LAYERS = 1

A_HEAD_DIM = 128
A_HEADS = 8
A_WIDTH = 1024
A_CONV = 4
A_CHUNK = 64
A_PROJ_PAD = 4608

B_HEAD_DIM = 64
B_Q_HEADS = 16
B_KV_HEADS = 2
B_GROUP = 8
WINDOW = 128
ROPE_DIM = 16
ROPE_THETA = 500000.0

N_EXPERTS = 32
TOP_K = 4
D_FF = 1024
SWIGLU_LIMIT = 7.0
SWIGLU_ALPHA = 1.702
MOE_ROWS = 512

EPS = 1e-6
LANES = 128
NEG_INF = float("-inf")

VMEM_LIMIT = 56 * 1024 * 1024


def _cparams(sem, vmem=VMEM_LIMIT):
    return pltpu.CompilerParams(dimension_semantics=sem, vmem_limit_bytes=vmem)


def _silu(x):
    return x * jax.nn.sigmoid(x)


HI_HALF = -65536


def _pack_bf16_pairs(x):
    n = x.shape[-1] // 2
    bits = lax.bitcast_convert_type(x.astype(BF16).astype(F32), jnp.int32)
    return jnp.bitwise_or(lax.shift_right_logical(bits[:, :n], 16), jnp.bitwise_and(bits[:, n:], HI_HALF))


def _unpack_bf16_pairs(w):
    lo = lax.bitcast_convert_type(lax.shift_left(w, 16), F32)
    hi = lax.bitcast_convert_type(jnp.bitwise_and(w, HI_HALF), F32)
    return lo, hi


def _ada_norm(x, gain, shift, scale):
    y = x * lax.rsqrt(jnp.mean(x * x, axis=-1, keepdims=True) + EPS)
    return (y * gain) * (1.0 + scale) + shift


def _mod_body(c_ref, w_ref, b_ref, o_ref):
    o_ref[...] = jnp.dot(_silu(c_ref[...]), w_ref[...], preferred_element_type=F32,
                         precision=HIGHEST) + b_ref[...]


def _modulation(c, w3, b3, layer):
    bsz, d = c.shape
    n = w3.shape[-1]
    tn = 1024
    return pl.pallas_call(
        _mod_body,
        out_shape=jax.ShapeDtypeStruct((bsz, n), F32),
        grid=(n // tn,),
        in_specs=[pl.BlockSpec((bsz, d), lambda j: (0, 0)),
                  pl.BlockSpec((None, d, tn), lambda j: (layer, 0, j)),
                  pl.BlockSpec((None, 1, tn), lambda j: (layer, 0, j))],
        out_specs=pl.BlockSpec((bsz, tn), lambda j: (0, j)),
        compiler_params=_cparams(("arbitrary",)),
        name="adaln_mod",
    )(c, w3, b3)


INPROJ_TM = 256
INPROJ_TN = 256
HALO = 8


def _inproj_body(x_ref, gain_ref, sh_ref, sc_ref, w_ref, conv_ref, o_ref, ext, *, tiles_per_seq):
    i = pl.program_id(0)
    tm = x_ref.shape[0]
    hd = A_HEAD_DIM
    n = o_ref.shape[-1]
    tn = INPROJ_TN
    h = _ada_norm(x_ref[...], gain_ref[...], sh_ref[...], sc_ref[...]).astype(BF16)

    @pl.when(i % tiles_per_seq == 0)
    def _():
        ext[0:HALO, :] = jnp.zeros((HALO, ext.shape[-1]), F32)

    def conv_silu_norm(raw, n0):
        cols = slice(n0, n0 + tn)
        ext[HALO:HALO + tm, cols] = raw
        acc = raw * conv_ref[A_CONV - 1:A_CONV, cols]
        for s in range(1, A_CONV):
            acc = acc + ext[HALO - s:HALO - s + tm, cols] * conv_ref[A_CONV - 1 - s:A_CONV - s, cols]
        ext[0:HALO, cols] = raw[tm - HALO:, :]
        y = _silu(acc)
        if n0 >= 2 * A_WIDTH:
            return y
        scale = hd ** -0.5 if n0 < A_WIDTH else 1.0
        heads = []
        for c0 in range(0, tn, hd):
            yh = y[:, c0:c0 + hd]
            heads.append(yh * (lax.rsqrt(jnp.sum(yh * yh, axis=-1, keepdims=True) + EPS) * scale))
        return jnp.concatenate(heads, axis=1)

    matmul = lambda n0: jnp.dot(h, w_ref[:, n0:n0 + tn], preferred_element_type=F32)
    conv_chunks = list(range(0, 3 * A_WIDTH, tn))
    plain_chunks = list(range(3 * A_WIDTH, n, tn))
    every = max(1, len(conv_chunks) // max(1, len(plain_chunks)))
    raw = matmul(conv_chunks[0])
    for idx, n0 in enumerate(conv_chunks):
        nxt = matmul(conv_chunks[idx + 1]) if idx + 1 < len(conv_chunks) else None
        if idx % every == every - 1 and plain_chunks:
            p0 = plain_chunks.pop(0)
            o_ref[:, p0:p0 + tn] = matmul(p0)
        o_ref[:, n0:n0 + tn] = conv_silu_norm(raw, n0)
        raw = nxt
    for p0 in plain_chunks:
        o_ref[:, p0:p0 + tn] = matmul(p0)


def _in_proj(x2, gain, mod4, w_bf, conv_w, seq):
    t, d = x2.shape
    n = w_bf.shape[-1]
    tm = INPROJ_TM
    tps = seq // tm
    return pl.pallas_call(
        functools.partial(_inproj_body, tiles_per_seq=tps),
        out_shape=jax.ShapeDtypeStruct((t, n), F32),
        grid=(t // tm,),
        in_specs=[pl.BlockSpec((tm, d), lambda i: (i, 0)),
                  pl.BlockSpec((1, d), lambda i: (0, 0)),
                  pl.BlockSpec((None, None, 1, d), lambda i: (i // tps, 0, 0, 0)),
                  pl.BlockSpec((None, None, 1, d), lambda i: (i // tps, 1, 0, 0)),
                  pl.BlockSpec((d, n), lambda i: (0, 0)),
                  pl.BlockSpec(conv_w.shape, lambda i: (0, 0))],
        out_specs=pl.BlockSpec((tm, n), lambda i: (i, 0)),
        scratch_shapes=[pltpu.VMEM((HALO + tm, conv_w.shape[-1]), F32)],
        compiler_params=_cparams(("arbitrary",)),
        name="gdn_in_proj",
    )(x2, gain, mod4, mod4, w_bf, conv_w)


GDN_HG = 2
GDN_RB = 256
GDN_SC = 128
GDN_SUB = 4


def _split_bf16(x):
    hi = x.astype(BF16)
    return hi, (x - hi.astype(F32)).astype(BF16)


def _mm3(a_hi, a_lo, b_hi, b_lo, dims=(((1,), (0,)), ((), ()))):
    dg = functools.partial(lax.dot_general, dimension_numbers=dims, preferred_element_type=F32)
    return dg(a_hi, b_hi) + (dg(a_lo, b_hi) + dg(a_hi, b_lo))


def _dot3(a, b, dims):
    return _mm3(*_split_bf16(a), *_split_bf16(b), dims)


def _gdn_body(alog_ref, dtb_ref, q_ref, k_ref, v_ref, z_ref, ab_ref, og_ref,
              o_ref, bs, gcs, mp, bc, qp, op):
    seq = q_ref.shape[0]
    hd = A_HEAD_DIM
    ch = A_CHUNK
    g_idx = pl.program_id(1)

    lane = lax.broadcasted_iota(jnp.int32, (GDN_RB, hd), 1)
    row = lax.broadcasted_iota(jnp.int32, (GDN_RB, hd), 0)
    row_in_chunk = row % ch

    sc_rows = GDN_SC
    per_sc = sc_rows // ch
    rb_per_step = sc_rows * GDN_SUB // GDN_RB
    head_cols = lambda hh: slice(hh * hd, (hh + 1) * hd)

    def prep_stages(j):
        for rbi in range(rb_per_step):
            r = j * rb_per_step + rbi
            start = pl.multiple_of(r * GDN_RB, GDN_RB)
            rows = pl.ds(start, GDN_RB)
            for hh in range(GDN_HG):
                head = g_idx * GDN_HG + hh
                ab = ab_ref[rows, :]
                a_col = jnp.sum(jnp.where(lane == head, ab, 0.0), axis=-1, keepdims=True)
                b_col = jnp.sum(jnp.where(lane == head + A_HEADS, ab, 0.0), axis=-1, keepdims=True)
                bs[hh, rows, :] = jnp.broadcast_to(jax.nn.sigmoid(b_col), (GDN_RB, hd))
                xa = a_col + dtb_ref[0, head]
                softplus = jnp.maximum(xa, 0.0) + jnp.log1p(jnp.exp(-jnp.abs(xa)))
                g = jnp.broadcast_to(-jnp.exp(alog_ref[0, head]) * softplus, (GDN_RB, hd))
                s = 1
                while s < ch:
                    g = g + jnp.where(row_in_chunk >= s, pltpu.roll(g, s, 0), 0.0)
                    s *= 2
                gcs[hh, rows, :] = g
                yield

    ri = lax.broadcasted_iota(jnp.int32, (sc_rows, sc_rows), 0)
    cj = lax.broadcasted_iota(jnp.int32, (sc_rows, sc_rows), 1)
    same_chunk = (ri // ch) == (cj // ch)
    incl = jnp.logical_and(same_chunk, ri >= cj)
    strict = jnp.logical_and(same_chunk, ri > cj)
    eye = jnp.where(ri == cj, 1.0, 0.0)
    chunk_of_col = lax.broadcasted_iota(jnp.int32, (hd, sc_rows), 1) // ch
    lanes_nt = (((1,), (1,)), ((), ()))
    plain = (((1,), (0,)), ((), ()))

    chunks_per_step = GDN_SUB * per_sc
    n_steps = seq // (sc_rows * GDN_SUB)

    def local_stages(i):
        chains = [(i * GDN_SUB + sub, hh) for sub in range(GDN_SUB) for hh in range(GDN_HG)]
        rows_of = lambda blk: pl.ds(pl.multiple_of(blk * sc_rows, sc_rows), sc_rows)

        def start(blk, hh):
            rows = rows_of(blk)
            k = k_ref[rows, head_cols(hh)]
            gc = gcs[hh, rows, :]
            decay = jnp.where(incl, jnp.exp(jnp.minimum(gc - gc.T[0:1, :], 0.0)), 0.0)
            k_bf = k.astype(BF16)
            kk = lax.dot_general((k * bs[hh, rows, :]).astype(BF16), k_bf, lanes_nt, preferred_element_type=F32)
            qk = lax.dot_general(q_ref[rows, head_cols(hh)].astype(BF16), k_bf, lanes_nt, preferred_element_type=F32)
            return jnp.where(strict, -(kk * decay), 0.0), (qk * decay).astype(BF16)

        started = [start(*c) for c in chains]
        yield
        qks = [s[1] for s in started]
        invs = [eye + s[0] for s in started]
        pws = [_dot3(s[0], s[0], plain) for s in started]
        yield
        for step in range(1, 6):
            for n in range(len(chains)):
                pw_hi, pw_lo = _split_bf16(pws[n])
                inv_hi, inv_lo = _split_bf16(invs[n])
                if step < 5:
                    prod = _mm3(jnp.concatenate([pw_hi, inv_hi], axis=0),
                                jnp.concatenate([pw_lo, inv_lo], axis=0), pw_hi, pw_lo)
                    pws[n] = prod[:sc_rows]
                    invs[n] = invs[n] + prod[sc_rows:]
                else:
                    invs[n] = invs[n] + _mm3(inv_hi, inv_lo, pw_hi, pw_lo)
            yield

        def solve(n, blk, hh):
            rows = rows_of(blk)
            beta = bs[hh, rows, :]
            rhs = jnp.concatenate([v_ref[rows, head_cols(hh)] * beta, k_ref[rows, head_cols(hh)] * beta * jnp.exp(gcs[hh, rows, :])], axis=1)
            return _dot3(invs[n], rhs, plain).astype(BF16)

        uws = [solve(n, *c) for n, c in enumerate(chains)]
        yield

        def finish(n, blk, hh):
            rows = rows_of(blk)
            k = k_ref[rows, head_cols(hh)]
            gc = gcs[hh, rows, :]
            res = jnp.dot(qks[n], uws[n], preferred_element_type=F32)
            op[hh, rows, :] = res[:, :hd]
            qp[hh, rows, :] = (q_ref[rows, head_cols(hh)] * jnp.exp(gc) - res[:, hd:]).astype(BF16)
            gl = jnp.concatenate(
                [jnp.broadcast_to(gc[(j + 1) * ch - 1:(j + 1) * ch, :], (ch, hd)) for j in range(per_sc)], axis=0)
            kt_t = (k * jnp.exp(gl - gc)).T
            for j in range(per_sc):
                kt_j = jnp.where(chunk_of_col == j, kt_t, 0.0).astype(BF16)
                bm = jnp.dot(kt_j, uws[n], preferred_element_type=F32)
                bc[hh, blk * per_sc + j] = bm[:, :hd]
                mp[hh, blk * per_sc + j] = bm[:, hd:].astype(BF16)

        for n, c in enumerate(chains):
            finish(n, *c)

    def scan_stages(j, states):
        for cc in range(chunks_per_step):
            c = j * chunks_per_step + cc
            r0 = pl.multiple_of(c * ch, ch)
            rows = pl.ds(r0, ch)
            for hh in range(GDN_HG):
                col0 = hh * hd
                state = states[hh]
                s_bf = state.astype(BF16)
                o = jnp.dot(qp[hh, rows, :], s_bf, preferred_element_type=F32) + op[hh, rows, :]
                g_tot = jnp.exp(gcs[hh, pl.ds(r0 + ch - 1, 1), :])
                states[hh] = state * g_tot - jnp.dot(mp[hh, c], s_bf, preferred_element_type=F32) + bc[hh, c]
                on = o * lax.rsqrt(jnp.mean(o * o, axis=-1, keepdims=True) + EPS) * og_ref[...]
                z = z_ref[rows, col0:col0 + hd]
                o_ref[rows, col0:col0 + hd] = (on * _silu(z)).astype(BF16)
            yield

    def interleave(*gens):
        live = list(gens)
        while live:
            for g in list(live):
                try:
                    next(g)
                except StopIteration:
                    live.remove(g)

    assert n_steps >= 2
    interleave(prep_stages(0))
    interleave(local_stages(0), prep_stages(1))

    def pipelined(i, states):
        states = list(states)
        interleave(local_stages(i), scan_stages(i - 1, states), prep_stages(i + 1))
        return tuple(states)

    states = list(lax.fori_loop(1, n_steps - 1, pipelined,
                                tuple(jnp.zeros((hd, hd), F32) for _ in range(GDN_HG))))
    interleave(local_stages(n_steps - 1), scan_stages(n_steps - 2, states))
    interleave(scan_stages(n_steps - 1, states))


def _gdn(proj, a_log, dt_bias, out_gain, bsz, seq):
    t = proj.shape[0]
    hd = A_HEAD_DIM
    wb = hd * GDN_HG
    ng = A_HEADS // GDN_HG
    per = A_WIDTH // wb
    n_chunks = seq // A_CHUNK
    smem = pl.BlockSpec(memory_space=pltpu.SMEM)
    seq_spec = lambda off: pl.BlockSpec((seq, wb), lambda b, g, off=off: (b, off + g))
    sc = lambda dt=F32: pltpu.VMEM((GDN_HG, seq, hd), dt)
    return pl.pallas_call(
        _gdn_body,
        out_shape=jax.ShapeDtypeStruct((t, A_WIDTH), BF16),
        grid=(bsz, ng),
        in_specs=[smem, smem,
                  seq_spec(0), seq_spec(per), seq_spec(2 * per), seq_spec(3 * per),
                  pl.BlockSpec((seq, LANES), lambda b, g: (b, 4 * A_WIDTH // LANES)),
                  pl.BlockSpec((1, hd), lambda b, g: (0, 0))],
        out_specs=pl.BlockSpec((seq, wb), lambda b, g: (b, g)),
        scratch_shapes=[sc(), sc(),
                        pltpu.VMEM((GDN_HG, n_chunks, hd, hd), BF16),
                        pltpu.VMEM((GDN_HG, n_chunks, hd, hd), F32),
                        sc(BF16), sc()],
        compiler_params=_cparams(("parallel", "parallel")),
        name="gdn_core",
    )(a_log, dt_bias, proj, proj, proj, proj, proj, out_gain)


def _oproj_body(a_ref, w_ref, x_ref, g_ref, o_ref):
    y = jnp.dot(a_ref[...], w_ref[...], preferred_element_type=F32)
    o_ref[...] = x_ref[...] + g_ref[...] * y


def _out_proj_residual(a_bf, w_bf, x2, mod4, gate_slot, seq):
    t, d = x2.shape
    kdim = a_bf.shape[-1]
    tm = 512
    tps = seq // tm
    return pl.pallas_call(
        _oproj_body,
        out_shape=jax.ShapeDtypeStruct((t, d), F32),
        grid=(t // tm,),
        in_specs=[pl.BlockSpec((tm, kdim), lambda i: (i, 0)),
                  pl.BlockSpec((kdim, d), lambda i: (0, 0)),
                  pl.BlockSpec((tm, d), lambda i: (i, 0)),
                  pl.BlockSpec((None, None, 1, d), lambda i: (i // tps, gate_slot, 0, 0))],
        out_specs=pl.BlockSpec((tm, d), lambda i: (i, 0)),
        compiler_params=_cparams(("parallel",)),
        name="out_proj_residual",
    )(a_bf, w_bf, x2, mod4)


ROUTER_TM = 512


def _router_body(x_ref, gain_ref, sh_ref, sc_ref, rw_ref, rb_ref, h_ref, idx_ref, gate_ref, rank_ref,
                 cnt_ref, carry):
    i = pl.program_id(0)

    @pl.when(i == 0)
    def _():
        carry[...] = jnp.zeros_like(carry)

    tm = x_ref.shape[0]
    h = _ada_norm(x_ref[...], gain_ref[...], sh_ref[...], sc_ref[...])
    h_ref[...] = _pack_bf16_pairs(h)
    lane = lax.broadcasted_iota(jnp.int32, (tm, LANES), 1)
    logits = _dot3(h, rw_ref[...], (((1,), (0,)), ((), ()))) + rb_ref[...]
    logits = jnp.where(lane < N_EXPERTS, logits, NEG_INF)
    tops, sels = [], []
    work = logits
    for _ in range(TOP_K):
        m = jnp.max(work, axis=-1, keepdims=True)
        sel = jnp.min(jnp.where(work == m, lane, LANES), axis=-1, keepdims=True)
        work = jnp.where(lane == sel, NEG_INF, work)
        tops.append(m)
        sels.append(sel)
    exps = [jnp.exp(m - tops[0]) for m in tops]
    denom = exps[0] + exps[1] + exps[2] + exps[3]
    onehot = jnp.zeros((tm, LANES), F32)
    for sel in sels:
        onehot = onehot + (lane == sel).astype(F32)
    ti = lax.broadcasted_iota(jnp.int32, (tm, tm), 0)
    tj = lax.broadcasted_iota(jnp.int32, (tm, tm), 1)
    tri = (ti > tj).astype(BF16)
    before = jnp.dot(tri, onehot.astype(BF16), preferred_element_type=F32) + carry[...]
    idx_out = jnp.zeros((tm, LANES), jnp.int32)
    gate_out = jnp.zeros((tm, LANES), F32)
    rank_out = jnp.zeros((tm, LANES), jnp.int32)
    for k in range(TOP_K):
        rank_k = jnp.sum(jnp.where(lane == sels[k], before, 0.0), axis=-1, keepdims=True)
        idx_out = jnp.where(lane == k, sels[k], idx_out)
        gate_out = jnp.where(lane == k, exps[k] / denom, gate_out)
        rank_out = jnp.where(lane == k, rank_k.astype(jnp.int32), rank_out)
    idx_ref[...] = idx_out
    gate_ref[...] = gate_out
    rank_ref[...] = rank_out
    carry[...] = carry[...] + jnp.sum(onehot, axis=0, keepdims=True)
    cnt_ref[...] = carry[...]


def _router(x2, gain, mod4, sh_slot, sc_slot, rw_pad, rb_pad, seq):
    t, d = x2.shape
    tm = ROUTER_TM
    tps = seq // tm
    tok = lambda dt: jax.ShapeDtypeStruct((t, LANES), dt)
    tok_spec = pl.BlockSpec((tm, LANES), lambda i: (i, 0))
    return pl.pallas_call(
        _router_body,
        out_shape=(jax.ShapeDtypeStruct((t, d // 2), jnp.int32), tok(jnp.int32), tok(F32), tok(jnp.int32),
                   jax.ShapeDtypeStruct((1, LANES), F32)),
        grid=(t // tm,),
        in_specs=[pl.BlockSpec((tm, d), lambda i: (i, 0)),
                  pl.BlockSpec((1, d), lambda i: (0, 0)),
                  pl.BlockSpec((None, None, 1, d), lambda i: (i // tps, sh_slot, 0, 0)),
                  pl.BlockSpec((None, None, 1, d), lambda i: (i // tps, sc_slot, 0, 0)),
                  pl.BlockSpec((d, LANES), lambda i: (0, 0)),
                  pl.BlockSpec((1, LANES), lambda i: (0, 0))],
        out_specs=(pl.BlockSpec((tm, d // 2), lambda i: (i, 0)), tok_spec, tok_spec, tok_spec,
                   pl.BlockSpec((1, LANES), lambda i: (0, 0))),
        scratch_shapes=[pltpu.VMEM((1, LANES), F32)],
        compiler_params=_cparams(("arbitrary",)),
        name="moe_router",
    )(x2, gain, mod4, mod4, rw_pad, rb_pad)


SC_LANES = 16
SC_INDEX_CHUNK = 8192


def _sc_row_tokens(dest_flat, n_rows, n_tok):
    n_assign = dest_flat.shape[0]
    assert n_rows % SC_LANES == 0 and n_assign % SC_INDEX_CHUNK == 0
    n_cores = plsc.get_sparse_core_info().num_cores
    mesh = plsc.VectorSubcoreMesh(core_axis_name="c", subcore_axis_name="s")

    @functools.partial(
        pl.kernel, mesh=mesh,
        out_type=jax.ShapeDtypeStruct((n_rows,), jnp.int32),
        scratch_types=[pltpu.VMEM((n_rows,), jnp.int32), pltpu.VMEM((SC_INDEX_CHUNK,), jnp.int32)],
        compiler_params=dataclasses.replace(pltpu.CompilerParams(), needs_layout_passes=False),
    )
    def row_token_kernel(dest_hbm, out_hbm, rt_v, d_v):
        wid = lax.axis_index("s") * n_cores + lax.axis_index("c")

        @pl.when(wid == 0)
        def _():
            lanes = lax.iota(jnp.int32, SC_LANES)

            @pl.loop(0, n_rows // SC_LANES)
            def _(i):
                rt_v[pl.ds(i * SC_LANES, SC_LANES)] = lax.rem(lanes + i * SC_LANES, n_tok)

            @pl.loop(0, n_assign // SC_INDEX_CHUNK)
            def _(c):
                pltpu.sync_copy(dest_hbm.at[pl.ds(c * SC_INDEX_CHUNK, SC_INDEX_CHUNK)], d_v)

                @pl.loop(0, SC_INDEX_CHUNK // SC_LANES)
                def _(i):
                    idx = d_v[pl.ds(i * SC_LANES, SC_LANES)]
                    tok = lax.div(lanes + (c * SC_INDEX_CHUNK + i * SC_LANES), TOP_K)
                    plsc.store_scatter(rt_v, [idx], tok)

            pltpu.sync_copy(rt_v, out_hbm)

    return row_token_kernel(dest_flat)


def _ffn_body(be_ref, nu_ref, x_ref, uw_ref, ub_ref, dw_ref, db_ref, y_ref, uw_bf, dw_bf):
    i = pl.program_id(0)
    changed = jnp.logical_or(i == 0, be_ref[i] != be_ref[jnp.maximum(i - 1, 0)])
    active = i < nu_ref[0]

    @pl.when(jnp.logical_and(changed, active))
    def _():
        rows = 128
        for r0 in range(0, uw_ref.shape[0], rows):
            uw_bf[r0:r0 + rows, :] = uw_ref[r0:r0 + rows, :].astype(BF16)
        for r0 in range(0, dw_ref.shape[0], rows):
            dw_bf[r0:r0 + rows, :] = dw_ref[r0:r0 + rows, :].astype(BF16)

    @pl.when(active)
    def _():
        x = jnp.concatenate(_unpack_bf16_pairs(x_ref[...]), axis=1).astype(BF16)
        gu = jnp.dot(x, uw_bf[...], preferred_element_type=F32) + ub_ref[...]
        gate = jnp.minimum(gu[:, :D_FF], SWIGLU_LIMIT)
        lin = jnp.clip(gu[:, D_FF:], -SWIGLU_LIMIT, SWIGLU_LIMIT)
        act = gate * jax.nn.sigmoid(SWIGLU_ALPHA * gate) * (lin + 1.0)
        y = jnp.dot(act.astype(BF16), dw_bf[...], preferred_element_type=F32) + db_ref[...]
        y_ref[...] = _pack_bf16_pairs(y)

    @pl.when(jnp.logical_not(active))
    def _():
        y_ref[...] = jnp.zeros_like(y_ref)


def _expert_ffn(xb, blk_expert, n_used, up_w, up_b4, down_w, down_b4, layer):
    n_rows, dp = xb.shape
    d = 2 * dp
    bm = MOE_ROWS
    n_blk = n_rows // bm
    f2 = up_w.shape[-1]
    grid_spec = pltpu.PrefetchScalarGridSpec(
        num_scalar_prefetch=2,
        grid=(n_blk,),
        in_specs=[pl.BlockSpec((bm, dp), lambda i, be, nu: (i, 0)),
                  pl.BlockSpec((None, None, d, f2), lambda i, be, nu: (layer, be[i], 0, 0)),
                  pl.BlockSpec((None, None, 1, f2), lambda i, be, nu: (layer, be[i], 0, 0)),
                  pl.BlockSpec((None, None, f2 // 2, d), lambda i, be, nu: (layer, be[i], 0, 0)),
                  pl.BlockSpec((None, None, 1, d), lambda i, be, nu: (layer, be[i], 0, 0))],
        out_specs=pl.BlockSpec((bm, dp), lambda i, be, nu: (i, 0)),
        scratch_shapes=[pltpu.VMEM((d, f2), BF16), pltpu.VMEM((f2 // 2, d), BF16)],
    )
    return pl.pallas_call(
        _ffn_body,
        out_shape=jax.ShapeDtypeStruct((n_rows, dp), jnp.int32),
        grid_spec=grid_spec,
        compiler_params=_cparams(("arbitrary",)),
        name="moe_expert_ffn",
    )(blk_expert, n_used, xb, up_w, up_b4, down_w, down_b4)


SC_GATHER_ROWS = 64


def _sc_gather_rows(table, idx):
    n, d = idx.shape[0], table.shape[1]
    info = plsc.get_sparse_core_info()
    n_cores, n_sub = info.num_cores, info.num_subcores
    n_workers = n_cores * n_sub
    chunk = SC_GATHER_ROWS
    per_worker = n // n_workers
    n_chunks = per_worker // chunk
    assert n_chunks * chunk * n_workers == n and n_chunks % 2 == 0
    mesh = plsc.VectorSubcoreMesh(core_axis_name="c", subcore_axis_name="s")

    @functools.partial(
        pl.kernel, mesh=mesh,
        out_type=jax.ShapeDtypeStruct((n, d), table.dtype),
        scratch_types=[pltpu.VMEM((n_chunks, chunk), jnp.int32), pltpu.VMEM((2, chunk, d), table.dtype),
                       pltpu.SemaphoreType.DMA((2,))],
    )
    def gather_kernel(table_hbm, idx_hbm, out_hbm, idx_v, rows_v, sems):
        wid = lax.axis_index("s") * n_cores + lax.axis_index("c")
        base = wid * per_worker
        pltpu.sync_copy(idx_hbm.at[wid], idx_v)

        def gather(ci, slot):
            return pltpu.make_async_copy(table_hbm.at[idx_v.at[ci]], rows_v.at[slot], sems.at[slot])

        gather(0, 0).start()

        @pl.loop(0, n_chunks, step=2)
        def _(c0):
            for slot in range(2):
                ci = c0 + slot
                gather(ci, slot).wait()

                @pl.when(ci + 1 < n_chunks)
                def _():
                    gather(ci + 1, 1 - slot).start()

                pltpu.sync_copy(rows_v.at[slot], out_hbm.at[pl.ds(base + ci * chunk, chunk)])

    return gather_kernel(table, idx.reshape(n_workers, n_chunks, chunk))


COMBINE_TM = 256


def _combine_body(gate_ref, x_ref, g_ref, y_ref, o_ref):
    gates = gate_ref[...]
    acc_lo = acc_hi = None
    for k in range(TOP_K):
        lo, hi = _unpack_bf16_pairs(y_ref[k])
        gk = gates[:, k:k + 1]
        acc_lo = gk * lo if acc_lo is None else acc_lo + gk * lo
        acc_hi = gk * hi if acc_hi is None else acc_hi + gk * hi
    o_ref[...] = x_ref[...] + g_ref[...] * jnp.concatenate([acc_lo, acc_hi], axis=1)


def _combine(y_kt, gates, x2, mod4, gate_slot, seq):
    t, d = x2.shape
    tm = COMBINE_TM
    tps = seq // tm
    return pl.pallas_call(
        _combine_body,
        out_shape=jax.ShapeDtypeStruct((t, d), F32),
        grid=(t // tm,),
        in_specs=[pl.BlockSpec((tm, LANES), lambda i: (i, 0)),
                  pl.BlockSpec((tm, d), lambda i: (i, 0)),
                  pl.BlockSpec((None, None, 1, d), lambda i: (i // tps, gate_slot, 0, 0)),
                  pl.BlockSpec((TOP_K, tm, d // 2), lambda i: (0, i, 0))],
        out_specs=pl.BlockSpec((tm, d), lambda i: (i, 0)),
        compiler_params=_cparams(("parallel",)),
        name="moe_combine",
    )(gates, x2, mod4, y_kt)


def _moe_layer(x2, gain, mod4, router_w, router_b, up_w, up_b, down_w, down_b, layer, seq):
    t, d = x2.shape
    rw_pad = jnp.pad(router_w[layer], ((0, 0), (0, LANES - N_EXPERTS)))
    rb_pad = jnp.pad(router_b[layer], (0, LANES - N_EXPERTS)).reshape(1, LANES)
    h2, idx, gates, rank, counts = _router(x2, gain, mod4, 3, 4, rw_pad, rb_pad, seq)

    bm = MOE_ROWS
    counts = counts[0, :N_EXPERTS].astype(jnp.int32)
    padded = (counts + bm - 1) // bm * bm
    p_ends = jnp.cumsum(padded)
    p_starts = p_ends - padded
    n_rows = t * TOP_K + N_EXPERTS * bm
    n_blk = n_rows // bm
    top_idx = idx[:, :TOP_K]
    onehot = top_idx[:, :, None] == jnp.arange(N_EXPERTS, dtype=jnp.int32)[None, None, :]
    dest = jnp.sum(jnp.where(onehot, p_starts[None, None, :], 0), axis=-1) + rank[:, :TOP_K]
    blk_start = jnp.arange(n_blk, dtype=jnp.int32) * bm
    blk_expert = jnp.sum((blk_start[:, None] >= p_ends[None, :]).astype(jnp.int32), axis=-1)
    blk_expert = jnp.minimum(blk_expert, N_EXPERTS - 1).astype(jnp.int32)
    n_used = (p_ends[-1:] // bm).astype(jnp.int32)

    xb = _sc_gather_rows(h2, _sc_row_tokens(dest.reshape(t * TOP_K), n_rows, t))
    f2 = up_w.shape[-1]
    y_rows = _expert_ffn(xb, blk_expert, n_used, up_w, up_b.reshape(up_b.shape[0], N_EXPERTS, 1, f2),
                         down_w, down_b.reshape(down_b.shape[0], N_EXPERTS, 1, d), layer)
    y_kt = _sc_gather_rows(y_rows, dest.T.reshape(TOP_K * t)).reshape(TOP_K, t, d // 2)
    return _combine(y_kt, gates, x2, mod4, 5, seq)


QKV_TM = 256


def _rope_tables(pos_row, invf_col):
    tm = pos_row.shape[1]
    half = ROPE_DIM // 2
    ang = invf_col * pos_row
    expand = (lax.broadcasted_iota(jnp.int32, (half, LANES), 1) % half
              == lax.broadcasted_iota(jnp.int32, (half, LANES), 0)).astype(BF16)

    def to_lanes(x):
        hi = x.astype(BF16)
        rest = x - hi.astype(F32)
        mid = rest.astype(BF16)
        parts = (hi, mid, (rest - mid.astype(F32)).astype(BF16))
        return sum(lax.dot_general(p, expand, (((0,), (0,)), ((), ())), preferred_element_type=F32) for p in parts)

    c = to_lanes(jnp.cos(ang))
    s = to_lanes(jnp.sin(ang))
    d = lax.broadcasted_iota(jnp.int32, (tm, LANES), 1) % B_HEAD_DIM
    cos_t = jnp.where(d < ROPE_DIM, c, 1.0)
    sin_lo = jnp.where(d < half, -s, 0.0)
    sin_hi = jnp.where(jnp.logical_and(d >= half, d < ROPE_DIM), s, 0.0)
    return cos_t, sin_lo, sin_hi


def _head_norm_rope(x, gain_row, tables):
    cos_t, sin_lo, sin_hi = tables
    same_head = (lax.broadcasted_iota(jnp.int32, (LANES, LANES), 0) // B_HEAD_DIM
                 == lax.broadcasted_iota(jnp.int32, (LANES, LANES), 1) // B_HEAD_DIM).astype(BF16)
    sq_hi, sq_lo = _split_bf16(x * x)
    ss = (jnp.dot(sq_hi, same_head, preferred_element_type=F32)
          + jnp.dot(sq_lo, same_head, preferred_element_type=F32))
    xn = x * lax.rsqrt(ss * (1.0 / B_HEAD_DIM) + EPS) * gain_row
    half = ROPE_DIM // 2
    return (xn * cos_t + pltpu.roll(xn, LANES - half, 1) * sin_lo + pltpu.roll(xn, half, 1) * sin_hi)


def _qkv_body(x_ref, pos_ref, gq_ref, shq_ref, scq_ref, gkv_ref, shkv_ref, sckv_ref, wq_ref, wkv_ref,
              qg_ref, kg_ref, invf_ref, q_ref, k_ref, v_ref):
    x = x_ref[...]
    y = x * lax.rsqrt(jnp.mean(x * x, axis=-1, keepdims=True) + EPS)
    hq = ((y * gq_ref[...]) * (1.0 + scq_ref[...]) + shq_ref[...]).astype(BF16)
    hkv = ((y * gkv_ref[...]) * (1.0 + sckv_ref[...]) + shkv_ref[...]).astype(BF16)
    tables = _rope_tables(pos_ref[...].astype(F32), invf_ref[...])
    kv = jnp.dot(hkv, wkv_ref[...], preferred_element_type=F32)
    k_ref[...] = _head_norm_rope(kv[:, :LANES], kg_ref[...], tables).astype(BF16)
    v_ref[...] = kv[:, LANES:].astype(BF16)
    q = jnp.dot(hq, wq_ref[...], preferred_element_type=F32)
    scale = B_HEAD_DIM ** -0.5
    for p in range(q.shape[-1] // LANES):
        qp = _head_norm_rope(q[:, p * LANES:(p + 1) * LANES], qg_ref[...], tables)
        q_ref[:, p * LANES:(p + 1) * LANES] = (qp * scale).astype(BF16)


def _qkv(x2, pos2, gq, mod4, gkv, kvmod4, wq_bf, wkv_bf, qg2, kg2, invf, seq):
    t, d = x2.shape
    tm = QKV_TM
    tps = seq // tm
    nq = wq_bf.shape[-1]
    nkv = wkv_bf.shape[-1]
    row = lambda n: pl.BlockSpec((1, n), lambda i: (0, 0))
    modspec = lambda slot: pl.BlockSpec((None, None, 1, d), lambda i, slot=slot: (i // tps, slot, 0, 0))
    return pl.pallas_call(
        _qkv_body,
        out_shape=(jax.ShapeDtypeStruct((t, nq), BF16), jax.ShapeDtypeStruct((t, LANES), BF16),
                   jax.ShapeDtypeStruct((t, LANES), BF16)),
        grid=(t // tm,),
        in_specs=[pl.BlockSpec((tm, d), lambda i: (i, 0)),
                  pl.BlockSpec((None, 1, tm), lambda i: (i, 0, 0)),
                  row(d), modspec(0), modspec(1),
                  row(d), modspec(0), modspec(1),
                  pl.BlockSpec((d, nq), lambda i: (0, 0)),
                  pl.BlockSpec((d, nkv), lambda i: (0, 0)),
                  row(LANES), row(LANES), pl.BlockSpec(invf.shape, lambda i: (0, 0))],
        out_specs=(pl.BlockSpec((tm, nq), lambda i: (i, 0)),
                   pl.BlockSpec((tm, LANES), lambda i: (i, 0)),
                   pl.BlockSpec((tm, LANES), lambda i: (i, 0))),
        compiler_params=_cparams(("parallel",)),
        name="swa_qkv_proj",
    )(x2, pos2.reshape(t // tm, 1, tm), gq, mod4, mod4, gkv, kvmod4, kvmod4, wq_bf, wkv_bf, qg2, kg2, invf)


def _attn_body(sink_ref, q_ref, kc_ref, kp_ref, vc_ref, vp_ref, w_ref, x_ref, g_ref, o_ref):
    n = pl.program_id(1)
    w = WINDOW
    qi = lax.broadcasted_iota(jnp.int32, (w, 2 * w), 0) + w
    kj = lax.broadcasted_iota(jnp.int32, (w, 2 * w), 1)
    band = jnp.logical_and(kj <= qi, qi - kj < w)
    mask = jnp.logical_and(band, jnp.logical_or(n > 0, kj >= w))
    hd = B_HEAD_DIM
    lane = lax.broadcasted_iota(jnp.int32, (2 * w, LANES), 1)
    first = lane < hd
    kfull = jnp.concatenate([kp_ref[...], kc_ref[...]], axis=0).astype(F32)
    vfull = jnp.concatenate([vp_ref[...], vc_ref[...]], axis=0).astype(F32)
    placed = {}
    for g in range(B_KV_HEADS):
        for name, full in (("k", kfull), ("v", vfull)):
            swapped = pltpu.roll(full, hd, 1)
            own_first = full if g == 0 else swapped
            own_second = swapped if g == 0 else full
            placed[name, g, 0] = jnp.where(first, own_first, 0.0).astype(BF16)
            placed[name, g, 1] = jnp.where(first, 0.0, own_second).astype(BF16)
    pairs = B_Q_HEADS // 2
    heads = [(p, side) for p in range(pairs) for side in range(2)]
    scores = []
    for p, side in heads:
        g = (2 * p) // B_GROUP
        q_pair = q_ref[:, p * LANES:(p + 1) * LANES]
        s = lax.dot_general(q_pair, placed["k", g, side], (((1,), (1,)), ((), ())), preferred_element_type=F32)
        scores.append(jnp.where(mask, s, NEG_INF))
    probs, denoms = [], []
    for (p, side), s in zip(heads, scores):
        sink = sink_ref[0, 2 * p + side]
        m = jnp.maximum(jnp.max(jnp.maximum(s[:, :w], s[:, w:]), axis=-1, keepdims=True), sink)
        e = jnp.exp(s - m)
        denoms.append(jnp.sum(e[:, :w] + e[:, w:], axis=-1, keepdims=True) + jnp.exp(sink - m))
        probs.append(e.astype(BF16))
    outs = []
    for (p, side), pr, den in zip(heads, probs, denoms):
        g = (2 * p) // B_GROUP
        outs.append(jnp.dot(pr, placed["v", g, side], preferred_element_type=F32) / den)
    attn = jnp.concatenate([(outs[2 * p] + outs[2 * p + 1]).astype(BF16) for p in range(pairs)], axis=1)
    o_ref[...] = x_ref[...] + g_ref[...] * jnp.dot(attn, w_ref[...], preferred_element_type=F32)


def _attention_residual(q, k, v, sinks2, w_out_bf, x2, mod4, gate_slot, bsz, seq):
    t, nq = q.shape
    d = x2.shape[-1]
    w = WINDOW
    nb = seq // w
    cur = lambda b, n: (b * nb + n, 0)
    prev = lambda b, n: (b * nb + jnp.maximum(n - 1, 0), 0)
    return pl.pallas_call(
        _attn_body,
        out_shape=jax.ShapeDtypeStruct((t, d), F32),
        grid=(bsz, nb),
        in_specs=[pl.BlockSpec(memory_space=pltpu.SMEM),
                  pl.BlockSpec((w, nq), cur),
                  pl.BlockSpec((w, LANES), cur), pl.BlockSpec((w, LANES), prev),
                  pl.BlockSpec((w, LANES), cur), pl.BlockSpec((w, LANES), prev),
                  pl.BlockSpec((nq, d), lambda b, n: (0, 0)),
                  pl.BlockSpec((w, d), cur),
                  pl.BlockSpec((None, None, 1, d), lambda b, n: (b, gate_slot, 0, 0))],
        out_specs=pl.BlockSpec((w, d), cur),
        compiler_params=_cparams(("parallel", "parallel")),
        name="swa_sink_attention",
    )(sinks2, q, k, k, v, v, w_out_bf, x2, mod4)


def kernel(x, c, positions, ada_w, ada_b, norm_gain, a_w_in, a_conv, a_log, a_dt_bias, a_out_gain, a_w_out,
           kv_ada_w, kv_ada_b, kv_norm_gain, kv_w, k_norm_gain, b_w_q, q_norm_gain, b_sinks, b_w_out,
           router_w, router_b, up_w, up_b, down_w, down_b):
    bsz, seq, d = x.shape
    t = bsz * seq
    depth = ada_w.shape[0]
    x2 = x.reshape(t, d)
    ada_b3 = ada_b.reshape(depth, 1, 6 * d)

    for layer in range(depth):
        mod4 = _modulation(c, ada_w, ada_b3, layer).reshape(bsz, 6, 1, d)
        gain1 = norm_gain[layer, 0].reshape(1, d)
        gain2 = norm_gain[layer, 1].reshape(1, d)
        if layer < N_A_LAYERS:
            w_in = jnp.pad(a_w_in[layer].astype(BF16), ((0, 0), (0, A_PROJ_PAD - a_w_in.shape[-1])))
            proj = _in_proj(x2, gain1, mod4, w_in, a_conv[layer], seq)
            o = _gdn(proj, a_log[layer].reshape(1, A_HEADS),
                     a_dt_bias[layer].reshape(1, A_HEADS), a_out_gain[layer].reshape(1, A_HEAD_DIM),
                     bsz, seq)
            x2 = _out_proj_residual(o, a_w_out[layer].astype(BF16), x2, mod4, 2, seq)
        else:
            j = layer - N_A_LAYERS
            kvmod4 = _modulation(c, kv_ada_w.reshape(1, d, 2 * d), kv_ada_b.reshape(1, 1, 2 * d), 0)
            kvmod4 = kvmod4.reshape(bsz, 2, 1, d)
            inv_freq = ROPE_THETA ** (-np.arange(0, ROPE_DIM, 2, dtype=np.float32) / ROPE_DIM)
            invf = jnp.asarray(inv_freq.astype(np.float32).reshape(ROPE_DIM // 2, 1))
            q, k, v = _qkv(x2, positions.reshape(t), gain1, mod4, kv_norm_gain.reshape(1, d), kvmod4,
                           b_w_q[j].astype(BF16), kv_w.astype(BF16),
                           jnp.tile(q_norm_gain[j], 2).reshape(1, LANES),
                           jnp.tile(k_norm_gain, 2).reshape(1, LANES), invf, seq)
            x2 = _attention_residual(q, k, v, b_sinks[j].reshape(1, B_Q_HEADS), b_w_out[j].astype(BF16),
                                     x2, mod4, 2, bsz, seq)
        x2 = _moe_layer(x2, gain2, mod4, router_w, router_b, up_w, up_b, down_w, down_b, layer, seq)
    return x2.reshape(bsz, seq, d)
```

```python
import dataclasses
import functools

import jax
import jax.numpy as jnp
import numpy as np
from jax import lax
from jax.experimental import pallas as pl
from jax.experimental.pallas import tpu as pltpu
from jax.experimental.pallas import tpu_sc as plsc

F32 = jnp.float32
BF16 = jnp.bfloat16
HIGHEST = lax.Precision.HIGHEST

D_MODEL = 1024
N_A_LAYERS = 1

A_HEAD_DIM = 128
A_HEADS = 8
A_WIDTH = 1024
A_CONV = 4
A_CHUNK = 64
A_PROJ_PAD = 4608

B_HEAD_DIM = 64
B_Q_HEADS = 16
B_KV_HEADS = 2
B_GROUP = 8
WINDOW = 128
ROPE_DIM = 16
ROPE_THETA = 500000.0

N_EXPERTS = 32
TOP_K = 4
D_FF = 1024
SWIGLU_LIMIT = 7.0
SWIGLU_ALPHA = 1.702
MOE_ROWS = 512

EPS = 1e-6
LANES = 128
NEG_INF = float("-inf")

VMEM_LIMIT = 56 * 1024 * 1024


def _cparams(sem, vmem=VMEM_LIMIT):
    return pltpu.CompilerParams(dimension_semantics=sem, vmem_limit_bytes=vmem)


def _silu(x):
    return x * jax.nn.sigmoid(x)


HI_HALF = -65536


def _pack_bf16_pairs(x):
    n = x.shape[-1] // 2
    bits = lax.bitcast_convert_type(x.astype(BF16).astype(F32), jnp.int32)
    return jnp.bitwise_or(lax.shift_right_logical(bits[:, :n], 16), jnp.bitwise_and(bits[:, n:], HI_HALF))


def _unpack_bf16_pairs(w):
    lo = lax.bitcast_convert_type(lax.shift_left(w, 16), F32)
    hi = lax.bitcast_convert_type(jnp.bitwise_and(w, HI_HALF), F32)
    return lo, hi


def _ada_norm(x, gain, shift, scale):
    y = x * lax.rsqrt(jnp.mean(x * x, axis=-1, keepdims=True) + EPS)
    return (y * gain) * (1.0 + scale) + shift


def _mod_body(c_ref, w_ref, b_ref, o_ref):
    o_ref[...] = jnp.dot(_silu(c_ref[...]), w_ref[...], preferred_element_type=F32,
                         precision=HIGHEST) + b_ref[...]


def _modulation(c, w3, b3, layer):
    bsz, d = c.shape
    n = w3.shape[-1]
    tn = 1024
    return pl.pallas_call(
        _mod_body,
        out_shape=jax.ShapeDtypeStruct((bsz, n), F32),
        grid=(n // tn,),
        in_specs=[pl.BlockSpec((bsz, d), lambda j: (0, 0)),
                  pl.BlockSpec((None, d, tn), lambda j: (layer, 0, j)),
                  pl.BlockSpec((None, 1, tn), lambda j: (layer, 0, j))],
        out_specs=pl.BlockSpec((bsz, tn), lambda j: (0, j)),
        compiler_params=_cparams(("arbitrary",)),
        name="adaln_mod",
    )(c, w3, b3)


INPROJ_TM = 256
INPROJ_TN = 256
HALO = 8


def _inproj_body(x_ref, gain_ref, sh_ref, sc_ref, w_ref, conv_ref, o_ref, ext, *, tiles_per_seq):
    i = pl.program_id(0)
    tm = x_ref.shape[0]
    hd = A_HEAD_DIM
    n = o_ref.shape[-1]
    tn = INPROJ_TN
    h = _ada_norm(x_ref[...], gain_ref[...], sh_ref[...], sc_ref[...]).astype(BF16)

    @pl.when(i % tiles_per_seq == 0)
    def _():
        ext[0:HALO, :] = jnp.zeros((HALO, ext.shape[-1]), F32)

    def conv_silu_norm(raw, n0):
        cols = slice(n0, n0 + tn)
        ext[HALO:HALO + tm, cols] = raw
        acc = raw * conv_ref[A_CONV - 1:A_CONV, cols]
        for s in range(1, A_CONV):
            acc = acc + ext[HALO - s:HALO - s + tm, cols] * conv_ref[A_CONV - 1 - s:A_CONV - s, cols]
        ext[0:HALO, cols] = raw[tm - HALO:, :]
        y = _silu(acc)
        if n0 >= 2 * A_WIDTH:
            return y
        scale = hd ** -0.5 if n0 < A_WIDTH else 1.0
        heads = []
        for c0 in range(0, tn, hd):
            yh = y[:, c0:c0 + hd]
            heads.append(yh * (lax.rsqrt(jnp.sum(yh * yh, axis=-1, keepdims=True) + EPS) * scale))
        return jnp.concatenate(heads, axis=1)

    matmul = lambda n0: jnp.dot(h, w_ref[:, n0:n0 + tn], preferred_element_type=F32)
    conv_chunks = list(range(0, 3 * A_WIDTH, tn))
    plain_chunks = list(range(3 * A_WIDTH, n, tn))
    every = max(1, len(conv_chunks) // max(1, len(plain_chunks)))
    raw = matmul(conv_chunks[0])
    for idx, n0 in enumerate(conv_chunks):
        nxt = matmul(conv_chunks[idx + 1]) if idx + 1 < len(conv_chunks) else None
        if idx % every == every - 1 and plain_chunks:
            p0 = plain_chunks.pop(0)
            o_ref[:, p0:p0 + tn] = matmul(p0)
        o_ref[:, n0:n0 + tn] = conv_silu_norm(raw, n0)
        raw = nxt
    for p0 in plain_chunks:
        o_ref[:, p0:p0 + tn] = matmul(p0)


def _in_proj(x2, gain, mod4, w_bf, conv_w, seq):
    t, d = x2.shape
    n = w_bf.shape[-1]
    tm = INPROJ_TM
    tps = seq // tm
    return pl.pallas_call(
        functools.partial(_inproj_body, tiles_per_seq=tps),
        out_shape=jax.ShapeDtypeStruct((t, n), F32),
        grid=(t // tm,),
        in_specs=[pl.BlockSpec((tm, d), lambda i: (i, 0)),
                  pl.BlockSpec((1, d), lambda i: (0, 0)),
                  pl.BlockSpec((None, None, 1, d), lambda i: (i // tps, 0, 0, 0)),
                  pl.BlockSpec((None, None, 1, d), lambda i: (i // tps, 1, 0, 0)),
                  pl.BlockSpec((d, n), lambda i: (0, 0)),
                  pl.BlockSpec(conv_w.shape, lambda i: (0, 0))],
        out_specs=pl.BlockSpec((tm, n), lambda i: (i, 0)),
        scratch_shapes=[pltpu.VMEM((HALO + tm, conv_w.shape[-1]), F32)],
        compiler_params=_cparams(("arbitrary",)),
        name="gdn_in_proj",
    )(x2, gain, mod4, mod4, w_bf, conv_w)


GDN_HG = 2
GDN_RB = 256
GDN_SC = 128
GDN_SUB = 4


def _split_bf16(x):
    hi = x.astype(BF16)
    return hi, (x - hi.astype(F32)).astype(BF16)


def _mm3(a_hi, a_lo, b_hi, b_lo, dims=(((1,), (0,)), ((), ()))):
    dg = functools.partial(lax.dot_general, dimension_numbers=dims, preferred_element_type=F32)
    return dg(a_hi, b_hi) + (dg(a_lo, b_hi) + dg(a_hi, b_lo))


def _dot3(a, b, dims):
    return _mm3(*_split_bf16(a), *_split_bf16(b), dims)


def _gdn_body(alog_ref, dtb_ref, q_ref, k_ref, v_ref, z_ref, ab_ref, og_ref,
              o_ref, bs, gcs, mp, bc, qp, op):
    seq = q_ref.shape[0]
    hd = A_HEAD_DIM
    ch = A_CHUNK
    g_idx = pl.program_id(1)

    lane = lax.broadcasted_iota(jnp.int32, (GDN_RB, hd), 1)
    row = lax.broadcasted_iota(jnp.int32, (GDN_RB, hd), 0)
    row_in_chunk = row % ch

    sc_rows = GDN_SC
    per_sc = sc_rows // ch
    rb_per_step = sc_rows * GDN_SUB // GDN_RB
    head_cols = lambda hh: slice(hh * hd, (hh + 1) * hd)

    def prep_stages(j):
        for rbi in range(rb_per_step):
            r = j * rb_per_step + rbi
            start = pl.multiple_of(r * GDN_RB, GDN_RB)
            rows = pl.ds(start, GDN_RB)
            for hh in range(GDN_HG):
                head = g_idx * GDN_HG + hh
                ab = ab_ref[rows, :]
                a_col = jnp.sum(jnp.where(lane == head, ab, 0.0), axis=-1, keepdims=True)
                b_col = jnp.sum(jnp.where(lane == head + A_HEADS, ab, 0.0), axis=-1, keepdims=True)
                bs[hh, rows, :] = jnp.broadcast_to(jax.nn.sigmoid(b_col), (GDN_RB, hd))
                xa = a_col + dtb_ref[0, head]
                softplus = jnp.maximum(xa, 0.0) + jnp.log1p(jnp.exp(-jnp.abs(xa)))
                g = jnp.broadcast_to(-jnp.exp(alog_ref[0, head]) * softplus, (GDN_RB, hd))
                s = 1
                while s < ch:
                    g = g + jnp.where(row_in_chunk >= s, pltpu.roll(g, s, 0), 0.0)
                    s *= 2
                gcs[hh, rows, :] = g
                yield

    ri = lax.broadcasted_iota(jnp.int32, (sc_rows, sc_rows), 0)
    cj = lax.broadcasted_iota(jnp.int32, (sc_rows, sc_rows), 1)
    same_chunk = (ri // ch) == (cj // ch)
    incl = jnp.logical_and(same_chunk, ri >= cj)
    strict = jnp.logical_and(same_chunk, ri > cj)
    eye = jnp.where(ri == cj, 1.0, 0.0)
    chunk_of_col = lax.broadcasted_iota(jnp.int32, (hd, sc_rows), 1) // ch
    lanes_nt = (((1,), (1,)), ((), ()))
    plain = (((1,), (0,)), ((), ()))

    chunks_per_step = GDN_SUB * per_sc
    n_steps = seq // (sc_rows * GDN_SUB)

    def local_stages(i):
        chains = [(i * GDN_SUB + sub, hh) for sub in range(GDN_SUB) for hh in range(GDN_HG)]
        rows_of = lambda blk: pl.ds(pl.multiple_of(blk * sc_rows, sc_rows), sc_rows)

        def start(blk, hh):
            rows = rows_of(blk)
            k = k_ref[rows, head_cols(hh)]
            gc = gcs[hh, rows, :]
            decay = jnp.where(incl, jnp.exp(jnp.minimum(gc - gc.T[0:1, :], 0.0)), 0.0)
            k_bf = k.astype(BF16)
            kk = lax.dot_general((k * bs[hh, rows, :]).astype(BF16), k_bf, lanes_nt, preferred_element_type=F32)
            qk = lax.dot_general(q_ref[rows, head_cols(hh)].astype(BF16), k_bf, lanes_nt, preferred_element_type=F32)
            return jnp.where(strict, -(kk * decay), 0.0), (qk * decay).astype(BF16)

        started = [start(*c) for c in chains]
        yield
        qks = [s[1] for s in started]
        invs = [eye + s[0] for s in started]
        pws = [_dot3(s[0], s[0], plain) for s in started]
        yield
        for step in range(1, 6):
            for n in range(len(chains)):
                pw_hi, pw_lo = _split_bf16(pws[n])
                inv_hi, inv_lo = _split_bf16(invs[n])
                if step < 5:
                    prod = _mm3(jnp.concatenate([pw_hi, inv_hi], axis=0),
                                jnp.concatenate([pw_lo, inv_lo], axis=0), pw_hi, pw_lo)
                    pws[n] = prod[:sc_rows]
                    invs[n] = invs[n] + prod[sc_rows:]
                else:
                    invs[n] = invs[n] + _mm3(inv_hi, inv_lo, pw_hi, pw_lo)
            yield

        def solve(n, blk, hh):
            rows = rows_of(blk)
            beta = bs[hh, rows, :]
            rhs = jnp.concatenate([v_ref[rows, head_cols(hh)] * beta, k_ref[rows, head_cols(hh)] * beta * jnp.exp(gcs[hh, rows, :])], axis=1)
            return _dot3(invs[n], rhs, plain).astype(BF16)

        uws = [solve(n, *c) for n, c in enumerate(chains)]
        yield

        def finish(n, blk, hh):
            rows = rows_of(blk)
            k = k_ref[rows, head_cols(hh)]
            gc = gcs[hh, rows, :]
            res = jnp.dot(qks[n], uws[n], preferred_element_type=F32)
            op[hh, rows, :] = res[:, :hd]
            qp[hh, rows, :] = (q_ref[rows, head_cols(hh)] * jnp.exp(gc) - res[:, hd:]).astype(BF16)
            gl = jnp.concatenate(
                [jnp.broadcast_to(gc[(j + 1) * ch - 1:(j + 1) * ch, :], (ch, hd)) for j in range(per_sc)], axis=0)
            kt_t = (k * jnp.exp(gl - gc)).T
            for j in range(per_sc):
                kt_j = jnp.where(chunk_of_col == j, kt_t, 0.0).astype(BF16)
                bm = jnp.dot(kt_j, uws[n], preferred_element_type=F32)
                bc[hh, blk * per_sc + j] = bm[:, :hd]
                mp[hh, blk * per_sc + j] = bm[:, hd:].astype(BF16)

        for n, c in enumerate(chains):
            finish(n, *c)

    def scan_stages(j, states):
        for cc in range(chunks_per_step):
            c = j * chunks_per_step + cc
            r0 = pl.multiple_of(c * ch, ch)
            rows = pl.ds(r0, ch)
            for hh in range(GDN_HG):
                col0 = hh * hd
                state = states[hh]
                s_bf = state.astype(BF16)
                o = jnp.dot(qp[hh, rows, :], s_bf, preferred_element_type=F32) + op[hh, rows, :]
                g_tot = jnp.exp(gcs[hh, pl.ds(r0 + ch - 1, 1), :])
                states[hh] = state * g_tot - jnp.dot(mp[hh, c], s_bf, preferred_element_type=F32) + bc[hh, c]
                on = o * lax.rsqrt(jnp.mean(o * o, axis=-1, keepdims=True) + EPS) * og_ref[...]
                z = z_ref[rows, col0:col0 + hd]
                o_ref[rows, col0:col0 + hd] = (on * _silu(z)).astype(BF16)
            yield

    def interleave(*gens):
        live = list(gens)
        while live:
            for g in list(live):
                try:
                    next(g)
                except StopIteration:
                    live.remove(g)

    assert n_steps >= 2
    interleave(prep_stages(0))
    interleave(local_stages(0), prep_stages(1))

    def pipelined(i, states):
        states = list(states)
        interleave(local_stages(i), scan_stages(i - 1, states), prep_stages(i + 1))
        return tuple(states)

    states = list(lax.fori_loop(1, n_steps - 1, pipelined,
                                tuple(jnp.zeros((hd, hd), F32) for _ in range(GDN_HG))))
    interleave(local_stages(n_steps - 1), scan_stages(n_steps - 2, states))
    interleave(scan_stages(n_steps - 1, states))


def _gdn(proj, a_log, dt_bias, out_gain, bsz, seq):
    t = proj.shape[0]
    hd = A_HEAD_DIM
    wb = hd * GDN_HG
    ng = A_HEADS // GDN_HG
    per = A_WIDTH // wb
    n_chunks = seq // A_CHUNK
    smem = pl.BlockSpec(memory_space=pltpu.SMEM)
    seq_spec = lambda off: pl.BlockSpec((seq, wb), lambda b, g, off=off: (b, off + g))
    sc = lambda dt=F32: pltpu.VMEM((GDN_HG, seq, hd), dt)
    return pl.pallas_call(
        _gdn_body,
        out_shape=jax.ShapeDtypeStruct((t, A_WIDTH), BF16),
        grid=(bsz, ng),
        in_specs=[smem, smem,
                  seq_spec(0), seq_spec(per), seq_spec(2 * per), seq_spec(3 * per),
                  pl.BlockSpec((seq, LANES), lambda b, g: (b, 4 * A_WIDTH // LANES)),
                  pl.BlockSpec((1, hd), lambda b, g: (0, 0))],
        out_specs=pl.BlockSpec((seq, wb), lambda b, g: (b, g)),
        scratch_shapes=[sc(), sc(),
                        pltpu.VMEM((GDN_HG, n_chunks, hd, hd), BF16),
                        pltpu.VMEM((GDN_HG, n_chunks, hd, hd), F32),
                        sc(BF16), sc()],
        compiler_params=_cparams(("parallel", "parallel")),
        name="gdn_core",
    )(a_log, dt_bias, proj, proj, proj, proj, proj, out_gain)


def _oproj_body(a_ref, w_ref, x_ref, g_ref, o_ref):
    y = jnp.dot(a_ref[...], w_ref[...], preferred_element_type=F32)
    o_ref[...] = x_ref[...] + g_ref[...] * y


def _out_proj_residual(a_bf, w_bf, x2, mod4, gate_slot, seq):
    t, d = x2.shape
    kdim = a_bf.shape[-1]
    tm = 512
    tps = seq // tm
    return pl.pallas_call(
        _oproj_body,
        out_shape=jax.ShapeDtypeStruct((t, d), F32),
        grid=(t // tm,),
        in_specs=[pl.BlockSpec((tm, kdim), lambda i: (i, 0)),
                  pl.BlockSpec((kdim, d), lambda i: (0, 0)),
                  pl.BlockSpec((tm, d), lambda i: (i, 0)),
                  pl.BlockSpec((None, None, 1, d), lambda i: (i // tps, gate_slot, 0, 0))],
        out_specs=pl.BlockSpec((tm, d), lambda i: (i, 0)),
        compiler_params=_cparams(("parallel",)),
        name="out_proj_residual",
    )(a_bf, w_bf, x2, mod4)


ROUTER_TM = 512


K_ROWS = 8


def _router_body(x_ref, gain_ref, sh_ref, sc_ref, rw_ref, rb_ref, h_ref, idx_ref, gate_ref, rank_ref,
                 cnt_ref, carry):
    i = pl.program_id(0)

    @pl.when(i == 0)
    def _():
        carry[...] = jnp.zeros_like(carry)

    tm = x_ref.shape[0]
    h = _ada_norm(x_ref[...], gain_ref[...], sh_ref[...], sc_ref[...])
    h_ref[...] = _pack_bf16_pairs(h)
    logits = _dot3(h, rw_ref[...], (((1,), (0,)), ((), ()))) + rb_ref[...]
    work = logits.T[:N_EXPERTS, :]
    expert = lax.broadcasted_iota(jnp.int32, (N_EXPERTS, tm), 0)
    tops, sels = [], []
    for _ in range(TOP_K):
        m = jnp.max(work, axis=0, keepdims=True)
        sel = jnp.min(jnp.where(work == m, expert, N_EXPERTS), axis=0, keepdims=True)
        work = jnp.where(expert == sel, NEG_INF, work)
        tops.append(m)
        sels.append(sel)
    exps = [jnp.exp(m - tops[0]) for m in tops]
    denom = exps[0] + exps[1] + exps[2] + exps[3]
    onehot = jnp.zeros((N_EXPERTS, tm), F32)
    for sel in sels:
        onehot = onehot + jnp.where(expert == sel, 1.0, 0.0)
    ti = lax.broadcasted_iota(jnp.int32, (tm, tm), 0)
    tj = lax.broadcasted_iota(jnp.int32, (tm, tm), 1)
    earlier = (ti < tj).astype(BF16)
    before = jnp.dot(onehot.astype(BF16), earlier, preferred_element_type=F32) + carry[:, 0:1]
    ranks = [jnp.sum(jnp.where(expert == sel, before, 0.0), axis=0, keepdims=True) for sel in sels]
    pad_i = jnp.zeros((K_ROWS - TOP_K, tm), jnp.int32)
    idx_ref[...] = jnp.concatenate(sels + [pad_i], axis=0)
    gate_ref[...] = jnp.concatenate([e / denom for e in exps] + [pad_i.astype(F32)], axis=0)
    rank_ref[...] = jnp.concatenate([r.astype(jnp.int32) for r in ranks] + [pad_i], axis=0)
    carry[...] = carry[...] + jnp.sum(onehot, axis=1, keepdims=True)
    cnt_ref[...] = carry[...]


def _router(x2, gain, mod4, sh_slot, sc_slot, rw_pad, rb_pad, seq):
    t, d = x2.shape
    tm = ROUTER_TM
    tps = seq // tm
    tok = lambda dt: jax.ShapeDtypeStruct((K_ROWS, t), dt)
    tok_spec = pl.BlockSpec((K_ROWS, tm), lambda i: (0, i))
    cnt_spec = pl.BlockSpec((N_EXPERTS, LANES), lambda i: (0, 0))
    return pl.pallas_call(
        _router_body,
        out_shape=(jax.ShapeDtypeStruct((t, d // 2), jnp.int32), tok(jnp.int32), tok(F32), tok(jnp.int32),
                   jax.ShapeDtypeStruct((N_EXPERTS, LANES), F32)),
        grid=(t // tm,),
        in_specs=[pl.BlockSpec((tm, d), lambda i: (i, 0)),
                  pl.BlockSpec((1, d), lambda i: (0, 0)),
                  pl.BlockSpec((None, None, 1, d), lambda i: (i // tps, sh_slot, 0, 0)),
                  pl.BlockSpec((None, None, 1, d), lambda i: (i // tps, sc_slot, 0, 0)),
                  pl.BlockSpec((d, LANES), lambda i: (0, 0)),
                  pl.BlockSpec((1, LANES), lambda i: (0, 0))],
        out_specs=(pl.BlockSpec((tm, d // 2), lambda i: (i, 0)), tok_spec, tok_spec, tok_spec, cnt_spec),
        scratch_shapes=[pltpu.VMEM((N_EXPERTS, LANES), F32)],
        compiler_params=_cparams(("arbitrary",)),
        name="moe_router",
    )(x2, gain, mod4, mod4, rw_pad, rb_pad)


SC_LANES = 16
SC_INDEX_CHUNK = 8192


def _sc_row_tokens(dest_flat, n_rows, n_tok):
    n_assign = dest_flat.shape[0]
    assert n_rows % SC_LANES == 0 and n_assign % SC_INDEX_CHUNK == 0
    n_cores = plsc.get_sparse_core_info().num_cores
    mesh = plsc.VectorSubcoreMesh(core_axis_name="c", subcore_axis_name="s")

    @functools.partial(
        pl.kernel, mesh=mesh,
        out_type=jax.ShapeDtypeStruct((n_rows,), jnp.int32),
        scratch_types=[pltpu.VMEM((n_rows,), jnp.int32), pltpu.VMEM((SC_INDEX_CHUNK,), jnp.int32)],
        compiler_params=dataclasses.replace(pltpu.CompilerParams(), needs_layout_passes=False),
    )
    def row_token_kernel(dest_hbm, out_hbm, rt_v, d_v):
        wid = lax.axis_index("s") * n_cores + lax.axis_index("c")

        @pl.when(wid == 0)
        def _():
            lanes = lax.iota(jnp.int32, SC_LANES)

            @pl.loop(0, n_rows // SC_LANES)
            def _(i):
                rt_v[pl.ds(i * SC_LANES, SC_LANES)] = lax.rem(lanes + i * SC_LANES, n_tok)

            @pl.loop(0, n_assign // SC_INDEX_CHUNK)
            def _(c):
                pltpu.sync_copy(dest_hbm.at[pl.ds(c * SC_INDEX_CHUNK, SC_INDEX_CHUNK)], d_v)

                @pl.loop(0, SC_INDEX_CHUNK // SC_LANES)
                def _(i):
                    idx = d_v[pl.ds(i * SC_LANES, SC_LANES)]
                    tok = lax.rem(lanes + (c * SC_INDEX_CHUNK + i * SC_LANES), n_tok)
                    plsc.store_scatter(rt_v, [idx], tok)

            pltpu.sync_copy(rt_v, out_hbm)

    return row_token_kernel(dest_flat)


def _ffn_body(be_ref, nu_ref, x_ref, uw_ref, ub_ref, dw_ref, db_ref, y_ref, uw_bf, dw_bf):
    i = pl.program_id(0)
    changed = jnp.logical_or(i == 0, be_ref[i] != be_ref[jnp.maximum(i - 1, 0)])
    active = i < nu_ref[0]

    @pl.when(jnp.logical_and(changed, active))
    def _():
        rows = 128
        for r0 in range(0, uw_ref.shape[0], rows):
            uw_bf[r0:r0 + rows, :] = uw_ref[r0:r0 + rows, :].astype(BF16)
        for r0 in range(0, dw_ref.shape[0], rows):
            dw_bf[r0:r0 + rows, :] = dw_ref[r0:r0 + rows, :].astype(BF16)

    @pl.when(active)
    def _():
        x = jnp.concatenate(_unpack_bf16_pairs(x_ref[...]), axis=1).astype(BF16)
        gu = jnp.dot(x, uw_bf[...], preferred_element_type=F32) + ub_ref[...]
        gate = jnp.minimum(gu[:, :D_FF], SWIGLU_LIMIT)
        lin = jnp.clip(gu[:, D_FF:], -SWIGLU_LIMIT, SWIGLU_LIMIT)
        act = gate * jax.nn.sigmoid(SWIGLU_ALPHA * gate) * (lin + 1.0)
        y = jnp.dot(act.astype(BF16), dw_bf[...], preferred_element_type=F32) + db_ref[...]
        y_ref[...] = _pack_bf16_pairs(y)

    @pl.when(jnp.logical_not(active))
    def _():
        y_ref[...] = jnp.zeros_like(y_ref)


def _expert_ffn(xb, blk_expert, n_used, up_w, up_b4, down_w, down_b4, layer):
    n_rows, dp = xb.shape
    d = 2 * dp
    bm = MOE_ROWS
    n_blk = n_rows // bm
    f2 = up_w.shape[-1]
    grid_spec = pltpu.PrefetchScalarGridSpec(
        num_scalar_prefetch=2,
        grid=(n_blk,),
        in_specs=[pl.BlockSpec((bm, dp), lambda i, be, nu: (i, 0)),
                  pl.BlockSpec((None, None, d, f2), lambda i, be, nu: (layer, be[i], 0, 0)),
                  pl.BlockSpec((None, None, 1, f2), lambda i, be, nu: (layer, be[i], 0, 0)),
                  pl.BlockSpec((None, None, f2 // 2, d), lambda i, be, nu: (layer, be[i], 0, 0)),
                  pl.BlockSpec((None, None, 1, d), lambda i, be, nu: (layer, be[i], 0, 0))],
        out_specs=pl.BlockSpec((bm, dp), lambda i, be, nu: (i, 0)),
        scratch_shapes=[pltpu.VMEM((d, f2), BF16), pltpu.VMEM((f2 // 2, d), BF16)],
    )
    return pl.pallas_call(
        _ffn_body,
        out_shape=jax.ShapeDtypeStruct((n_rows, dp), jnp.int32),
        grid_spec=grid_spec,
        compiler_params=_cparams(("arbitrary",)),
        name="moe_expert_ffn",
    )(blk_expert, n_used, xb, up_w, up_b4, down_w, down_b4)


SC_GATHER_ROWS = 64


def _sc_gather_rows(table, idx):
    n, d = idx.shape[0], table.shape[1]
    info = plsc.get_sparse_core_info()
    n_cores, n_sub = info.num_cores, info.num_subcores
    n_workers = n_cores * n_sub
    chunk = SC_GATHER_ROWS
    per_worker = n // n_workers
    n_chunks = per_worker // chunk
    assert n_chunks * chunk * n_workers == n and n_chunks % 2 == 0
    mesh = plsc.VectorSubcoreMesh(core_axis_name="c", subcore_axis_name="s")

    @functools.partial(
        pl.kernel, mesh=mesh,
        out_type=jax.ShapeDtypeStruct((n, d), table.dtype),
        scratch_types=[pltpu.VMEM((n_chunks, chunk), jnp.int32), pltpu.VMEM((2, chunk, d), table.dtype),
                       pltpu.SemaphoreType.DMA((2,))],
    )
    def gather_kernel(table_hbm, idx_hbm, out_hbm, idx_v, rows_v, sems):
        wid = lax.axis_index("s") * n_cores + lax.axis_index("c")
        base = wid * per_worker
        pltpu.sync_copy(idx_hbm.at[wid], idx_v)

        def gather(ci, slot):
            return pltpu.make_async_copy(table_hbm.at[idx_v.at[ci]], rows_v.at[slot], sems.at[slot])

        gather(0, 0).start()

        @pl.loop(0, n_chunks, step=2)
        def _(c0):
            for slot in range(2):
                ci = c0 + slot
                gather(ci, slot).wait()

                @pl.when(ci + 1 < n_chunks)
                def _():
                    gather(ci + 1, 1 - slot).start()

                pltpu.sync_copy(rows_v.at[slot], out_hbm.at[pl.ds(base + ci * chunk, chunk)])

    return gather_kernel(table, idx.reshape(n_workers, n_chunks, chunk))


COMBINE_TM = 256


def _combine_body(gate_ref, x_ref, g_ref, y_ref, o_ref):
    gates = gate_ref[...]
    acc_lo = acc_hi = None
    for k in range(TOP_K):
        lo, hi = _unpack_bf16_pairs(y_ref[k])
        gk = gates[:, k:k + 1]
        acc_lo = gk * lo if acc_lo is None else acc_lo + gk * lo
        acc_hi = gk * hi if acc_hi is None else acc_hi + gk * hi
    o_ref[...] = x_ref[...] + g_ref[...] * jnp.concatenate([acc_lo, acc_hi], axis=1)


def _combine(y_kt, gates, x2, mod4, gate_slot, seq):
    t, d = x2.shape
    tm = COMBINE_TM
    tps = seq // tm
    return pl.pallas_call(
        _combine_body,
        out_shape=jax.ShapeDtypeStruct((t, d), F32),
        grid=(t // tm,),
        in_specs=[pl.BlockSpec((tm, gates.shape[1]), lambda i: (i, 0)),
                  pl.BlockSpec((tm, d), lambda i: (i, 0)),
                  pl.BlockSpec((None, None, 1, d), lambda i: (i // tps, gate_slot, 0, 0)),
                  pl.BlockSpec((TOP_K, tm, d // 2), lambda i: (0, i, 0))],
        out_specs=pl.BlockSpec((tm, d), lambda i: (i, 0)),
        compiler_params=_cparams(("parallel",)),
        name="moe_combine",
    )(gates, x2, mod4, y_kt)


def _moe_layer(x2, gain, mod4, router_w, router_b, up_w, up_b, down_w, down_b, layer, seq):
    t, d = x2.shape
    rw_pad = jnp.pad(router_w[layer], ((0, 0), (0, LANES - N_EXPERTS)))
    rb_pad = jnp.pad(router_b[layer], (0, LANES - N_EXPERTS)).reshape(1, LANES)
    h2, idx, gates, rank, counts = _router(x2, gain, mod4, 3, 4, rw_pad, rb_pad, seq)

    bm = MOE_ROWS
    counts = counts[:, 0].astype(jnp.int32)
    padded = (counts + bm - 1) // bm * bm
    p_ends = jnp.cumsum(padded)
    p_starts = p_ends - padded
    n_rows = t * TOP_K + N_EXPERTS * bm
    n_blk = n_rows // bm
    onehot = idx[:TOP_K, :, None] == jnp.arange(N_EXPERTS, dtype=jnp.int32)[None, None, :]
    dest = jnp.sum(jnp.where(onehot, p_starts[None, None, :], 0), axis=-1) + rank[:TOP_K]
    dest_flat = dest.reshape(TOP_K * t)
    blk_start = jnp.arange(n_blk, dtype=jnp.int32) * bm
    blk_expert = jnp.sum((blk_start[:, None] >= p_ends[None, :]).astype(jnp.int32), axis=-1)
    blk_expert = jnp.minimum(blk_expert, N_EXPERTS - 1).astype(jnp.int32)
    n_used = (p_ends[-1:] // bm).astype(jnp.int32)

    xb = _sc_gather_rows(h2, _sc_row_tokens(dest_flat, n_rows, t))
    f2 = up_w.shape[-1]
    y_rows = _expert_ffn(xb, blk_expert, n_used, up_w, up_b.reshape(up_b.shape[0], N_EXPERTS, 1, f2),
                         down_w, down_b.reshape(down_b.shape[0], N_EXPERTS, 1, d), layer)
    y_kt = _sc_gather_rows(y_rows, dest_flat).reshape(TOP_K, t, d // 2)
    return _combine(y_kt, gates.T, x2, mod4, 5, seq)


QKV_TM = 256


def _rope_tables(pos_row, invf_col):
    tm = pos_row.shape[1]
    half = ROPE_DIM // 2
    ang = invf_col * pos_row
    expand = (lax.broadcasted_iota(jnp.int32, (half, LANES), 1) % half
              == lax.broadcasted_iota(jnp.int32, (half, LANES), 0)).astype(BF16)

    def to_lanes(x):
        hi = x.astype(BF16)
        rest = x - hi.astype(F32)
        mid = rest.astype(BF16)
        parts = (hi, mid, (rest - mid.astype(F32)).astype(BF16))
        return sum(lax.dot_general(p, expand, (((0,), (0,)), ((), ())), preferred_element_type=F32) for p in parts)

    c = to_lanes(jnp.cos(ang))
    s = to_lanes(jnp.sin(ang))
    d = lax.broadcasted_iota(jnp.int32, (tm, LANES), 1) % B_HEAD_DIM
    cos_t = jnp.where(d < ROPE_DIM, c, 1.0)
    sin_lo = jnp.where(d < half, -s, 0.0)
    sin_hi = jnp.where(jnp.logical_and(d >= half, d < ROPE_DIM), s, 0.0)
    return cos_t, sin_lo, sin_hi


def _head_norm_rope(x, gain_row, tables):
    cos_t, sin_lo, sin_hi = tables
    same_head = (lax.broadcasted_iota(jnp.int32, (LANES, LANES), 0) // B_HEAD_DIM
                 == lax.broadcasted_iota(jnp.int32, (LANES, LANES), 1) // B_HEAD_DIM).astype(BF16)
    sq_hi, sq_lo = _split_bf16(x * x)
    ss = (jnp.dot(sq_hi, same_head, preferred_element_type=F32)
          + jnp.dot(sq_lo, same_head, preferred_element_type=F32))
    xn = x * lax.rsqrt(ss * (1.0 / B_HEAD_DIM) + EPS) * gain_row
    half = ROPE_DIM // 2
    return (xn * cos_t + pltpu.roll(xn, LANES - half, 1) * sin_lo + pltpu.roll(xn, half, 1) * sin_hi)


def _qkv_body(x_ref, pos_ref, gq_ref, shq_ref, scq_ref, gkv_ref, shkv_ref, sckv_ref, wq_ref, wkv_ref,
              qg_ref, kg_ref, invf_ref, q_ref, k_ref, v_ref):
    x = x_ref[...]
    y = x * lax.rsqrt(jnp.mean(x * x, axis=-1, keepdims=True) + EPS)
    hq = ((y * gq_ref[...]) * (1.0 + scq_ref[...]) + shq_ref[...]).astype(BF16)
    hkv = ((y * gkv_ref[...]) * (1.0 + sckv_ref[...]) + shkv_ref[...]).astype(BF16)
    tables = _rope_tables(pos_ref[...].astype(F32), invf_ref[...])
    kv = jnp.dot(hkv, wkv_ref[...], preferred_element_type=F32)
    k_ref[...] = _head_norm_rope(kv[:, :LANES], kg_ref[...], tables).astype(BF16)
    v_ref[...] = kv[:, LANES:].astype(BF16)
    q = jnp.dot(hq, wq_ref[...], preferred_element_type=F32)
    scale = B_HEAD_DIM ** -0.5
    for p in range(q.shape[-1] // LANES):
        qp = _head_norm_rope(q[:, p * LANES:(p + 1) * LANES], qg_ref[...], tables)
        q_ref[:, p * LANES:(p + 1) * LANES] = (qp * scale).astype(BF16)


def _qkv(x2, pos2, gq, mod4, gkv, kvmod4, wq_bf, wkv_bf, qg2, kg2, invf, seq):
    t, d = x2.shape
    tm = QKV_TM
    tps = seq // tm
    nq = wq_bf.shape[-1]
    nkv = wkv_bf.shape[-1]
    row = lambda n: pl.BlockSpec((1, n), lambda i: (0, 0))
    modspec = lambda slot: pl.BlockSpec((None, None, 1, d), lambda i, slot=slot: (i // tps, slot, 0, 0))
    return pl.pallas_call(
        _qkv_body,
        out_shape=(jax.ShapeDtypeStruct((t, nq), BF16), jax.ShapeDtypeStruct((t, LANES), BF16),
                   jax.ShapeDtypeStruct((t, LANES), BF16)),
        grid=(t // tm,),
        in_specs=[pl.BlockSpec((tm, d), lambda i: (i, 0)),
                  pl.BlockSpec((None, 1, tm), lambda i: (i, 0, 0)),
                  row(d), modspec(0), modspec(1),
                  row(d), modspec(0), modspec(1),
                  pl.BlockSpec((d, nq), lambda i: (0, 0)),
                  pl.BlockSpec((d, nkv), lambda i: (0, 0)),
                  row(LANES), row(LANES), pl.BlockSpec(invf.shape, lambda i: (0, 0))],
        out_specs=(pl.BlockSpec((tm, nq), lambda i: (i, 0)),
                   pl.BlockSpec((tm, LANES), lambda i: (i, 0)),
                   pl.BlockSpec((tm, LANES), lambda i: (i, 0))),
        compiler_params=_cparams(("parallel",)),
        name="swa_qkv_proj",
    )(x2, pos2.reshape(t // tm, 1, tm), gq, mod4, mod4, gkv, kvmod4, kvmod4, wq_bf, wkv_bf, qg2, kg2, invf)


def _attn_body(sink_ref, q_ref, kc_ref, kp_ref, vc_ref, vp_ref, w_ref, x_ref, g_ref, o_ref):
    n = pl.program_id(1)
    w = WINDOW
    qi = lax.broadcasted_iota(jnp.int32, (w, 2 * w), 0) + w
    kj = lax.broadcasted_iota(jnp.int32, (w, 2 * w), 1)
    band = jnp.logical_and(kj <= qi, qi - kj < w)
    mask = jnp.logical_and(band, jnp.logical_or(n > 0, kj >= w))
    hd = B_HEAD_DIM
    lane = lax.broadcasted_iota(jnp.int32, (2 * w, LANES), 1)
    first = lane < hd
    kfull = jnp.concatenate([kp_ref[...], kc_ref[...]], axis=0).astype(F32)
    vfull = jnp.concatenate([vp_ref[...], vc_ref[...]], axis=0).astype(F32)
    placed = {}
    for g in range(B_KV_HEADS):
        for name, full in (("k", kfull), ("v", vfull)):
            swapped = pltpu.roll(full, hd, 1)
            own_first = full if g == 0 else swapped
            own_second = swapped if g == 0 else full
            placed[name, g, 0] = jnp.where(first, own_first, 0.0).astype(BF16)
            placed[name, g, 1] = jnp.where(first, 0.0, own_second).astype(BF16)
    pairs = B_Q_HEADS // 2
    heads = [(p, side) for p in range(pairs) for side in range(2)]
    scores = []
    for p, side in heads:
        g = (2 * p) // B_GROUP
        q_pair = q_ref[:, p * LANES:(p + 1) * LANES]
        s = lax.dot_general(q_pair, placed["k", g, side], (((1,), (1,)), ((), ())), preferred_element_type=F32)
        scores.append(jnp.where(mask, s, NEG_INF))
    probs, denoms = [], []
    for (p, side), s in zip(heads, scores):
        sink = sink_ref[0, 2 * p + side]
        m = jnp.maximum(jnp.max(jnp.maximum(s[:, :w], s[:, w:]), axis=-1, keepdims=True), sink)
        e = jnp.exp(s - m)
        denoms.append(jnp.sum(e[:, :w] + e[:, w:], axis=-1, keepdims=True) + jnp.exp(sink - m))
        probs.append(e.astype(BF16))
    outs = []
    for (p, side), pr, den in zip(heads, probs, denoms):
        g = (2 * p) // B_GROUP
        outs.append(jnp.dot(pr, placed["v", g, side], preferred_element_type=F32) / den)
    attn = jnp.concatenate([(outs[2 * p] + outs[2 * p + 1]).astype(BF16) for p in range(pairs)], axis=1)
    o_ref[...] = x_ref[...] + g_ref[...] * jnp.dot(attn, w_ref[...], preferred_element_type=F32)


def _attention_residual(q, k, v, sinks2, w_out_bf, x2, mod4, gate_slot, bsz, seq):
    t, nq = q.shape
    d = x2.shape[-1]
    w = WINDOW
    nb = seq // w
    cur = lambda b, n: (b * nb + n, 0)
    prev = lambda b, n: (b * nb + jnp.maximum(n - 1, 0), 0)
    return pl.pallas_call(
        _attn_body,
        out_shape=jax.ShapeDtypeStruct((t, d), F32),
        grid=(bsz, nb),
        in_specs=[pl.BlockSpec(memory_space=pltpu.SMEM),
                  pl.BlockSpec((w, nq), cur),
                  pl.BlockSpec((w, LANES), cur), pl.BlockSpec((w, LANES), prev),
                  pl.BlockSpec((w, LANES), cur), pl.BlockSpec((w, LANES), prev),
                  pl.BlockSpec((nq, d), lambda b, n: (0, 0)),
                  pl.BlockSpec((w, d), cur),
                  pl.BlockSpec((None, None, 1, d), lambda b, n: (b, gate_slot, 0, 0))],
        out_specs=pl.BlockSpec((w, d), cur),
        compiler_params=_cparams(("parallel", "parallel")),
        name="swa_sink_attention",
    )(sinks2, q, k, k, v, v, w_out_bf, x2, mod4)


def kernel(x, c, positions, ada_w, ada_b, norm_gain, a_w_in, a_conv, a_log, a_dt_bias, a_out_gain, a_w_out,
           kv_ada_w, kv_ada_b, kv_norm_gain, kv_w, k_norm_gain, b_w_q, q_norm_gain, b_sinks, b_w_out,
           router_w, router_b, up_w, up_b, down_w, down_b):
    bsz, seq, d = x.shape
    t = bsz * seq
    depth = ada_w.shape[0]
    x2 = x.reshape(t, d)
    ada_b3 = ada_b.reshape(depth, 1, 6 * d)

    for layer in range(depth):
        mod4 = _modulation(c, ada_w, ada_b3, layer).reshape(bsz, 6, 1, d)
        gain1 = norm_gain[layer, 0].reshape(1, d)
        gain2 = norm_gain[layer, 1].reshape(1, d)
        if layer < N_A_LAYERS:
            w_in = jnp.pad(a_w_in[layer].astype(BF16), ((0, 0), (0, A_PROJ_PAD - a_w_in.shape[-1])))
            proj = _in_proj(x2, gain1, mod4, w_in, a_conv[layer], seq)
            o = _gdn(proj, a_log[layer].reshape(1, A_HEADS),
                     a_dt_bias[layer].reshape(1, A_HEADS), a_out_gain[layer].reshape(1, A_HEAD_DIM),
                     bsz, seq)
            x2 = _out_proj_residual(o, a_w_out[layer].astype(BF16), x2, mod4, 2, seq)
        else:
            j = layer - N_A_LAYERS
            kvmod4 = _modulation(c, kv_ada_w.reshape(1, d, 2 * d), kv_ada_b.reshape(1, 1, 2 * d), 0)
            kvmod4 = kvmod4.reshape(bsz, 2, 1, d)
            inv_freq = ROPE_THETA ** (-np.arange(0, ROPE_DIM, 2, dtype=np.float32) / ROPE_DIM)
            invf = jnp.asarray(inv_freq.astype(np.float32).reshape(ROPE_DIM // 2, 1))
            q, k, v = _qkv(x2, positions.reshape(t), gain1, mod4, kv_norm_gain.reshape(1, d), kvmod4,
                           b_w_q[j].astype(BF16), kv_w.astype(BF16),
                           jnp.tile(q_norm_gain[j], 2).reshape(1, LANES),
                           jnp.tile(k_norm_gain, 2).reshape(1, LANES), invf, seq)
            x2 = _attention_residual(q, k, v, b_sinks[j].reshape(1, B_Q_HEADS), b_w_out[j].astype(BF16),
                                     x2, mod4, 2, bsz, seq)
        x2 = _moe_layer(x2, gain2, mod4, router_w, router_b, up_w, up_b, down_w, down_b, layer, seq)
    return x2.reshape(bsz, seq, d)
```

```python
import dataclasses
import functools

import jax
import jax.numpy as jnp
import numpy as np
from jax import lax
from jax.experimental import pallas as pl
from jax.experimental.pallas import tpu as pltpu
from jax.experimental.pallas import tpu_sc as plsc

F32 = jnp.float32
BF16 = jnp.bfloat16
HIGHEST = lax.Precision.HIGHEST

D_MODEL = 1024
N_A_LAYERS = 1

A_HEAD_DIM = 128
A_HEADS = 8
A_WIDTH = 1024
A_CONV = 4
A_CHUNK = 64
A_PROJ_PAD = 4608

B_HEAD_DIM = 64
B_Q_HEADS = 16
B_KV_HEADS = 2
B_GROUP = 8
WINDOW = 128
ROPE_DIM = 16
ROPE_THETA = 500000.0

N_EXPERTS = 32
TOP_K = 4
D_FF = 1024
SWIGLU_LIMIT = 7.0
SWIGLU_ALPHA = 1.702
MOE_ROWS = 512

EPS = 1e-6
LANES = 128
NEG_INF = float("-inf")

VMEM_LIMIT = 56 * 1024 * 1024


def _cparams(sem, vmem=VMEM_LIMIT):
    return pltpu.CompilerParams(dimension_semantics=sem, vmem_limit_bytes=vmem)


def _silu(x):
    return x * jax.nn.sigmoid(x)


HI_HALF = -65536


def _pack_bf16_pairs(x):
    n = x.shape[-1] // 2
    bits = lax.bitcast_convert_type(x.astype(BF16).astype(F32), jnp.int32)
    return jnp.bitwise_or(lax.shift_right_logical(bits[:, :n], 16), jnp.bitwise_and(bits[:, n:], HI_HALF))


def _unpack_bf16_pairs(w):
    lo = lax.bitcast_convert_type(lax.shift_left(w, 16), F32)
    hi = lax.bitcast_convert_type(jnp.bitwise_and(w, HI_HALF), F32)
    return lo, hi


def _ada_norm(x, gain, shift, scale):
    y = x * lax.rsqrt(jnp.mean(x * x, axis=-1, keepdims=True) + EPS)
    return (y * gain) * (1.0 + scale) + shift


def _mod_body(c_ref, w_ref, b_ref, o_ref):
    o_ref[...] = jnp.dot(_silu(c_ref[...]), w_ref[...], preferred_element_type=F32,
                         precision=HIGHEST) + b_ref[...]


def _modulation(c, w3, b3, layer):
    bsz, d = c.shape
    n = w3.shape[-1]
    tn = 1024
    return pl.pallas_call(
        _mod_body,
        out_shape=jax.ShapeDtypeStruct((bsz, n), F32),
        grid=(n // tn,),
        in_specs=[pl.BlockSpec((bsz, d), lambda j: (0, 0)),
                  pl.BlockSpec((None, d, tn), lambda j: (layer, 0, j)),
                  pl.BlockSpec((None, 1, tn), lambda j: (layer, 0, j))],
        out_specs=pl.BlockSpec((bsz, tn), lambda j: (0, j)),
        compiler_params=_cparams(("arbitrary",)),
        name="adaln_mod",
    )(c, w3, b3)


INPROJ_TM = 256
INPROJ_TN = 256
HALO = 8


def _inproj_body(x_ref, gain_ref, sh_ref, sc_ref, w_ref, conv_ref, o_ref, ext, *, tiles_per_seq):
    i = pl.program_id(0)
    tm = x_ref.shape[0]
    hd = A_HEAD_DIM
    n = o_ref.shape[-1]
    tn = INPROJ_TN
    h = _ada_norm(x_ref[...], gain_ref[...], sh_ref[...], sc_ref[...]).astype(BF16)

    @pl.when(i % tiles_per_seq == 0)
    def _():
        ext[0:HALO, :] = jnp.zeros((HALO, ext.shape[-1]), F32)

    def conv_silu_norm(raw, n0):
        cols = slice(n0, n0 + tn)
        ext[HALO:HALO + tm, cols] = raw
        acc = raw * conv_ref[A_CONV - 1:A_CONV, cols]
        for s in range(1, A_CONV):
            acc = acc + ext[HALO - s:HALO - s + tm, cols] * conv_ref[A_CONV - 1 - s:A_CONV - s, cols]
        ext[0:HALO, cols] = raw[tm - HALO:, :]
        y = _silu(acc)
        if n0 >= 2 * A_WIDTH:
            return y
        scale = hd ** -0.5 if n0 < A_WIDTH else 1.0
        heads = []
        for c0 in range(0, tn, hd):
            yh = y[:, c0:c0 + hd]
            heads.append(yh * (lax.rsqrt(jnp.sum(yh * yh, axis=-1, keepdims=True) + EPS) * scale))
        return jnp.concatenate(heads, axis=1)

    matmul = lambda n0: jnp.dot(h, w_ref[:, n0:n0 + tn], preferred_element_type=F32)
    conv_chunks = list(range(0, 3 * A_WIDTH, tn))
    plain_chunks = list(range(3 * A_WIDTH, n, tn))
    every = max(1, len(conv_chunks) // max(1, len(plain_chunks)))
    raw = matmul(conv_chunks[0])
    for idx, n0 in enumerate(conv_chunks):
        nxt = matmul(conv_chunks[idx + 1]) if idx + 1 < len(conv_chunks) else None
        if idx % every == every - 1 and plain_chunks:
            p0 = plain_chunks.pop(0)
            o_ref[:, p0:p0 + tn] = matmul(p0)
        o_ref[:, n0:n0 + tn] = conv_silu_norm(raw, n0)
        raw = nxt
    for p0 in plain_chunks:
        o_ref[:, p0:p0 + tn] = matmul(p0)


def _in_proj(x2, gain, mod4, w_bf, conv_w, seq):
    t, d = x2.shape
    n = w_bf.shape[-1]
    tm = INPROJ_TM
    tps = seq // tm
    return pl.pallas_call(
        functools.partial(_inproj_body, tiles_per_seq=tps),
        out_shape=jax.ShapeDtypeStruct((t, n), F32),
        grid=(t // tm,),
        in_specs=[pl.BlockSpec((tm, d), lambda i: (i, 0)),
                  pl.BlockSpec((1, d), lambda i: (0, 0)),
                  pl.BlockSpec((None, None, 1, d), lambda i: (i // tps, 0, 0, 0)),
                  pl.BlockSpec((None, None, 1, d), lambda i: (i // tps, 1, 0, 0)),
                  pl.BlockSpec((d, n), lambda i: (0, 0)),
                  pl.BlockSpec(conv_w.shape, lambda i: (0, 0))],
        out_specs=pl.BlockSpec((tm, n), lambda i: (i, 0)),
        scratch_shapes=[pltpu.VMEM((HALO + tm, conv_w.shape[-1]), F32)],
        compiler_params=_cparams(("arbitrary",)),
        name="gdn_in_proj",
    )(x2, gain, mod4, mod4, w_bf, conv_w)


GDN_HG = 2
GDN_RB = 256
GDN_SC = 128
GDN_SUB = 4


def _split_bf16(x):
    hi = x.astype(BF16)
    return hi, (x - hi.astype(F32)).astype(BF16)


def _mm3(a_hi, a_lo, b_hi, b_lo, dims=(((1,), (0,)), ((), ()))):
    dg = functools.partial(lax.dot_general, dimension_numbers=dims, preferred_element_type=F32)
    return dg(a_hi, b_hi) + (dg(a_lo, b_hi) + dg(a_hi, b_lo))


def _dot3(a, b, dims):
    return _mm3(*_split_bf16(a), *_split_bf16(b), dims)


def _gdn_body(alog_ref, dtb_ref, q_ref, k_ref, v_ref, z_ref, ab_ref, og_ref,
              o_ref, bs, gcs, mp, bc, qp, op):
    seq = q_ref.shape[0]
    hd = A_HEAD_DIM
    ch = A_CHUNK
    g_idx = pl.program_id(1)

    lane = lax.broadcasted_iota(jnp.int32, (GDN_RB, hd), 1)
    row = lax.broadcasted_iota(jnp.int32, (GDN_RB, hd), 0)
    row_in_chunk = row % ch

    sc_rows = GDN_SC
    per_sc = sc_rows // ch
    rb_per_step = sc_rows * GDN_SUB // GDN_RB
    head_cols = lambda hh: slice(hh * hd, (hh + 1) * hd)

    def prep_stages(j):
        for rbi in range(rb_per_step):
            r = j * rb_per_step + rbi
            start = pl.multiple_of(r * GDN_RB, GDN_RB)
            rows = pl.ds(start, GDN_RB)
            for hh in range(GDN_HG):
                head = g_idx * GDN_HG + hh
                ab = ab_ref[rows, :]
                a_col = jnp.sum(jnp.where(lane == head, ab, 0.0), axis=-1, keepdims=True)
                b_col = jnp.sum(jnp.where(lane == head + A_HEADS, ab, 0.0), axis=-1, keepdims=True)
                bs[hh, rows, :] = jnp.broadcast_to(jax.nn.sigmoid(b_col), (GDN_RB, hd))
                xa = a_col + dtb_ref[0, head]
                softplus = jnp.maximum(xa, 0.0) + jnp.log1p(jnp.exp(-jnp.abs(xa)))
                g = jnp.broadcast_to(-jnp.exp(alog_ref[0, head]) * softplus, (GDN_RB, hd))
                s = 1
                while s < ch:
                    g = g + jnp.where(row_in_chunk >= s, pltpu.roll(g, s, 0), 0.0)
                    s *= 2
                gcs[hh, rows, :] = g
                yield

    ri = lax.broadcasted_iota(jnp.int32, (sc_rows, sc_rows), 0)
    cj = lax.broadcasted_iota(jnp.int32, (sc_rows, sc_rows), 1)
    same_chunk = (ri // ch) == (cj // ch)
    incl = jnp.logical_and(same_chunk, ri >= cj)
    strict = jnp.logical_and(same_chunk, ri > cj)
    eye = jnp.where(ri == cj, 1.0, 0.0)
    chunk_of_col = lax.broadcasted_iota(jnp.int32, (hd, sc_rows), 1) // ch
    lanes_nt = (((1,), (1,)), ((), ()))
    plain = (((1,), (0,)), ((), ()))

    chunks_per_step = GDN_SUB * per_sc
    n_steps = seq // (sc_rows * GDN_SUB)

    def local_stages(i):
        chains = [(i * GDN_SUB + sub, hh) for sub in range(GDN_SUB) for hh in range(GDN_HG)]
        rows_of = lambda blk: pl.ds(pl.multiple_of(blk * sc_rows, sc_rows), sc_rows)

        def start(blk, hh):
            rows = rows_of(blk)
            k = k_ref[rows, head_cols(hh)]
            gc = gcs[hh, rows, :]
            decay = jnp.where(incl, jnp.exp(jnp.minimum(gc - gc.T[0:1, :], 0.0)), 0.0)
            k_bf = k.astype(BF16)
            kk = lax.dot_general((k * bs[hh, rows, :]).astype(BF16), k_bf, lanes_nt, preferred_element_type=F32)
            qk = lax.dot_general(q_ref[rows, head_cols(hh)].astype(BF16), k_bf, lanes_nt, preferred_element_type=F32)
            return jnp.where(strict, -(kk * decay), 0.0), (qk * decay).astype(BF16)

        started = [start(*c) for c in chains]
        yield
        qks = [s[1] for s in started]
        invs = [eye + s[0] for s in started]
        pws = [_dot3(s[0], s[0], plain) for s in started]
        yield
        for step in range(1, 6):
            for n in range(len(chains)):
                pw_hi, pw_lo = _split_bf16(pws[n])
                inv_hi, inv_lo = _split_bf16(invs[n])
                if step < 5:
                    prod = _mm3(jnp.concatenate([pw_hi, inv_hi], axis=0),
                                jnp.concatenate([pw_lo, inv_lo], axis=0), pw_hi, pw_lo)
                    pws[n] = prod[:sc_rows]
                    invs[n] = invs[n] + prod[sc_rows:]
                else:
                    invs[n] = invs[n] + _mm3(inv_hi, inv_lo, pw_hi, pw_lo)
            yield

        def solve(n, blk, hh):
            rows = rows_of(blk)
            beta = bs[hh, rows, :]
            rhs = jnp.concatenate([v_ref[rows, head_cols(hh)] * beta, k_ref[rows, head_cols(hh)] * beta * jnp.exp(gcs[hh, rows, :])], axis=1)
            return _dot3(invs[n], rhs, plain).astype(BF16)

        uws = [solve(n, *c) for n, c in enumerate(chains)]
        yield

        def finish(n, blk, hh):
            rows = rows_of(blk)
            k = k_ref[rows, head_cols(hh)]
            gc = gcs[hh, rows, :]
            res = jnp.dot(qks[n], uws[n], preferred_element_type=F32)
            op[hh, rows, :] = res[:, :hd]
            qp[hh, rows, :] = (q_ref[rows, head_cols(hh)] * jnp.exp(gc) - res[:, hd:]).astype(BF16)
            gl = jnp.concatenate(
                [jnp.broadcast_to(gc[(j + 1) * ch - 1:(j + 1) * ch, :], (ch, hd)) for j in range(per_sc)], axis=0)
            kt_t = (k * jnp.exp(gl - gc)).T
            for j in range(per_sc):
                kt_j = jnp.where(chunk_of_col == j, kt_t, 0.0).astype(BF16)
                bm = jnp.dot(kt_j, uws[n], preferred_element_type=F32)
                bc[hh, blk * per_sc + j] = bm[:, :hd]
                mp[hh, blk * per_sc + j] = bm[:, hd:].astype(BF16)

        for n, c in enumerate(chains):
            finish(n, *c)

    def scan_stages(j, states):
        for cc in range(chunks_per_step):
            c = j * chunks_per_step + cc
            r0 = pl.multiple_of(c * ch, ch)
            rows = pl.ds(r0, ch)
            for hh in range(GDN_HG):
                col0 = hh * hd
                state = states[hh]
                s_bf = state.astype(BF16)
                o = jnp.dot(qp[hh, rows, :], s_bf, preferred_element_type=F32) + op[hh, rows, :]
                g_tot = jnp.exp(gcs[hh, pl.ds(r0 + ch - 1, 1), :])
                states[hh] = state * g_tot - jnp.dot(mp[hh, c], s_bf, preferred_element_type=F32) + bc[hh, c]
                on = o * lax.rsqrt(jnp.mean(o * o, axis=-1, keepdims=True) + EPS) * og_ref[...]
                z = z_ref[rows, col0:col0 + hd]
                o_ref[rows, col0:col0 + hd] = (on * _silu(z)).astype(BF16)
            yield

    def interleave(*gens):
        live = list(gens)
        while live:
            for g in list(live):
                try:
                    next(g)
                except StopIteration:
                    live.remove(g)

    assert n_steps >= 2
    interleave(prep_stages(0))
    interleave(local_stages(0), prep_stages(1))

    def pipelined(i, states):
        states = list(states)
        interleave(local_stages(i), scan_stages(i - 1, states), prep_stages(i + 1))
        return tuple(states)

    states = list(lax.fori_loop(1, n_steps - 1, pipelined,
                                tuple(jnp.zeros((hd, hd), F32) for _ in range(GDN_HG))))
    interleave(local_stages(n_steps - 1), scan_stages(n_steps - 2, states))
    interleave(scan_stages(n_steps - 1, states))


def _gdn(proj, a_log, dt_bias, out_gain, bsz, seq):
    t = proj.shape[0]
    hd = A_HEAD_DIM
    wb = hd * GDN_HG
    ng = A_HEADS // GDN_HG
    per = A_WIDTH // wb
    n_chunks = seq // A_CHUNK
    smem = pl.BlockSpec(memory_space=pltpu.SMEM)
    seq_spec = lambda off: pl.BlockSpec((seq, wb), lambda b, g, off=off: (b, off + g))
    sc = lambda dt=F32: pltpu.VMEM((GDN_HG, seq, hd), dt)
    return pl.pallas_call(
        _gdn_body,
        out_shape=jax.ShapeDtypeStruct((t, A_WIDTH), BF16),
        grid=(bsz, ng),
        in_specs=[smem, smem,
                  seq_spec(0), seq_spec(per), seq_spec(2 * per), seq_spec(3 * per),
                  pl.BlockSpec((seq, LANES), lambda b, g: (b, 4 * A_WIDTH // LANES)),
                  pl.BlockSpec((1, hd), lambda b, g: (0, 0))],
        out_specs=pl.BlockSpec((seq, wb), lambda b, g: (b, g)),
        scratch_shapes=[sc(), sc(),
                        pltpu.VMEM((GDN_HG, n_chunks, hd, hd), BF16),
                        pltpu.VMEM((GDN_HG, n_chunks, hd, hd), F32),
                        sc(BF16), sc()],
        compiler_params=_cparams(("parallel", "parallel")),
        name="gdn_core",
    )(a_log, dt_bias, proj, proj, proj, proj, proj, out_gain)


def _oproj_body(a_ref, w_ref, x_ref, g_ref, o_ref):
    y = jnp.dot(a_ref[...], w_ref[...], preferred_element_type=F32)
    o_ref[...] = x_ref[...] + g_ref[...] * y


def _out_proj_residual(a_bf, w_bf, x2, mod4, gate_slot, seq):
    t, d = x2.shape
    kdim = a_bf.shape[-1]
    tm = 512
    tps = seq // tm
    return pl.pallas_call(
        _oproj_body,
        out_shape=jax.ShapeDtypeStruct((t, d), F32),
        grid=(t // tm,),
        in_specs=[pl.BlockSpec((tm, kdim), lambda i: (i, 0)),
                  pl.BlockSpec((kdim, d), lambda i: (0, 0)),
                  pl.BlockSpec((tm, d), lambda i: (i, 0)),
                  pl.BlockSpec((None, None, 1, d), lambda i: (i // tps, gate_slot, 0, 0))],
        out_specs=pl.BlockSpec((tm, d), lambda i: (i, 0)),
        compiler_params=_cparams(("parallel",)),
        name="out_proj_residual",
    )(a_bf, w_bf, x2, mod4)


ROUTER_TM = 512


K_ROWS = 8


def _router_body(x_ref, gain_ref, sh_ref, sc_ref, rw_ref, rb_ref, h_ref, idx_ref, gate_ref, rank_ref,
                 cnt_ref, carry):
    i = pl.program_id(0)

    @pl.when(i == 0)
    def _():
        carry[...] = jnp.zeros_like(carry)

    tm = x_ref.shape[0]
    h = _ada_norm(x_ref[...], gain_ref[...], sh_ref[...], sc_ref[...])
    h_ref[...] = _pack_bf16_pairs(h)
    logits = _dot3(h, rw_ref[...], (((1,), (0,)), ((), ()))) + rb_ref[...]
    work = logits.T[:N_EXPERTS, :]
    expert = lax.broadcasted_iota(jnp.int32, (N_EXPERTS, tm), 0)
    tops, sels = [], []
    for _ in range(TOP_K):
        m = jnp.max(work, axis=0, keepdims=True)
        sel = jnp.min(jnp.where(work == m, expert, N_EXPERTS), axis=0, keepdims=True)
        work = jnp.where(expert == sel, NEG_INF, work)
        tops.append(m)
        sels.append(sel)
    exps = [jnp.exp(m - tops[0]) for m in tops]
    denom = exps[0] + exps[1] + exps[2] + exps[3]
    onehot = jnp.zeros((N_EXPERTS, tm), F32)
    for sel in sels:
        onehot = onehot + jnp.where(expert == sel, 1.0, 0.0)
    ti = lax.broadcasted_iota(jnp.int32, (tm, tm), 0)
    tj = lax.broadcasted_iota(jnp.int32, (tm, tm), 1)
    earlier = (ti < tj).astype(BF16)
    before = jnp.dot(onehot.astype(BF16), earlier, preferred_element_type=F32) + carry[:, 0:1]
    ranks = [jnp.sum(jnp.where(expert == sel, before, 0.0), axis=0, keepdims=True) for sel in sels]
    pad_i = jnp.zeros((K_ROWS - TOP_K, tm), jnp.int32)
    idx_ref[...] = jnp.concatenate(sels + [pad_i], axis=0)
    gate_ref[...] = jnp.concatenate([e / denom for e in exps] + [pad_i.astype(F32)], axis=0)
    rank_ref[...] = jnp.concatenate([r.astype(jnp.int32) for r in ranks] + [pad_i], axis=0)
    carry[...] = carry[...] + jnp.sum(onehot, axis=1, keepdims=True)
    cnt_ref[...] = carry[...]


def _router(x2, gain, mod4, sh_slot, sc_slot, rw_pad, rb_pad, seq):
    t, d = x2.shape
    tm = ROUTER_TM
    tps = seq // tm
    tok = lambda dt: jax.ShapeDtypeStruct((K_ROWS, t), dt)
    tok_spec = pl.BlockSpec((K_ROWS, tm), lambda i: (0, i))
    cnt_spec = pl.BlockSpec((N_EXPERTS, LANES), lambda i: (0, 0))
    return pl.pallas_call(
        _router_body,
        out_shape=(jax.ShapeDtypeStruct((t, d // 2), jnp.int32), tok(jnp.int32), tok(F32), tok(jnp.int32),
                   jax.ShapeDtypeStruct((N_EXPERTS, LANES), F32)),
        grid=(t // tm,),
        in_specs=[pl.BlockSpec((tm, d), lambda i: (i, 0)),
                  pl.BlockSpec((1, d), lambda i: (0, 0)),
                  pl.BlockSpec((None, None, 1, d), lambda i: (i // tps, sh_slot, 0, 0)),
                  pl.BlockSpec((None, None, 1, d), lambda i: (i // tps, sc_slot, 0, 0)),
                  pl.BlockSpec((d, LANES), lambda i: (0, 0)),
                  pl.BlockSpec((1, LANES), lambda i: (0, 0))],
        out_specs=(pl.BlockSpec((tm, d // 2), lambda i: (i, 0)), tok_spec, tok_spec, tok_spec, cnt_spec),
        scratch_shapes=[pltpu.VMEM((N_EXPERTS, LANES), F32)],
        compiler_params=_cparams(("arbitrary",)),
        name="moe_router",
    )(x2, gain, mod4, mod4, rw_pad, rb_pad)


SC_LANES = 16
SC_INDEX_CHUNK = 8192


def _sc_row_tokens(dest_flat, n_rows, n_tok):
    n_assign = dest_flat.shape[0]
    assert n_rows % SC_LANES == 0 and n_assign % SC_INDEX_CHUNK == 0
    n_cores = plsc.get_sparse_core_info().num_cores
    mesh = plsc.VectorSubcoreMesh(core_axis_name="c", subcore_axis_name="s")

    @functools.partial(
        pl.kernel, mesh=mesh,
        out_type=jax.ShapeDtypeStruct((n_rows,), jnp.int32),
        scratch_types=[pltpu.VMEM((n_rows,), jnp.int32), pltpu.VMEM((SC_INDEX_CHUNK,), jnp.int32)],
        compiler_params=dataclasses.replace(pltpu.CompilerParams(), needs_layout_passes=False),
    )
    def row_token_kernel(dest_hbm, out_hbm, rt_v, d_v):
        wid = lax.axis_index("s") * n_cores + lax.axis_index("c")

        @pl.when(wid == 0)
        def _():
            lanes = lax.iota(jnp.int32, SC_LANES)

            @pl.loop(0, n_rows // SC_LANES)
            def _(i):
                rt_v[pl.ds(i * SC_LANES, SC_LANES)] = lax.rem(lanes + i * SC_LANES, n_tok)

            @pl.loop(0, n_assign // SC_INDEX_CHUNK)
            def _(c):
                pltpu.sync_copy(dest_hbm.at[pl.ds(c * SC_INDEX_CHUNK, SC_INDEX_CHUNK)], d_v)

                @pl.loop(0, SC_INDEX_CHUNK // SC_LANES)
                def _(i):
                    idx = d_v[pl.ds(i * SC_LANES, SC_LANES)]
                    tok = lax.rem(lanes + (c * SC_INDEX_CHUNK + i * SC_LANES), n_tok)
                    plsc.store_scatter(rt_v, [idx], tok)

            pltpu.sync_copy(rt_v, out_hbm)

    return row_token_kernel(dest_flat)


def _ffn_body(be_ref, nu_ref, nxt_ref, slot_ref, x_ref, uw_hbm, ub_ref, dw_hbm, db_ref, y_ref,
              uw_f32, dw_f32, uw_bf, dw_bf, sems, *, layer):
    i = pl.program_id(0)
    first = jnp.logical_and(jnp.logical_or(i == 0, be_ref[i] != be_ref[jnp.maximum(i - 1, 0)]), i < nu_ref[0])
    active = i < nu_ref[0]
    slot = slot_ref[i]

    def weight_copies(expert, s):
        return (pltpu.make_async_copy(uw_hbm.at[layer, expert], uw_f32.at[s], sems.at[s, 0]),
                pltpu.make_async_copy(dw_hbm.at[layer, expert], dw_f32.at[s], sems.at[s, 1]))

    @pl.when(jnp.logical_and(i == 0, active))
    def _():
        for cp in weight_copies(be_ref[0], 0):
            cp.start()

    @pl.when(first)
    def _():
        for cp in weight_copies(be_ref[i], slot):
            cp.wait()

        @pl.when(nxt_ref[i] >= 0)
        def _():
            for cp in weight_copies(nxt_ref[i], 1 - slot):
                cp.start()

        rows = 128
        for r0 in range(0, uw_bf.shape[0], rows):
            uw_bf[r0:r0 + rows, :] = uw_f32[slot, r0:r0 + rows, :].astype(BF16)
        for r0 in range(0, dw_bf.shape[0], rows):
            dw_bf[r0:r0 + rows, :] = dw_f32[slot, r0:r0 + rows, :].astype(BF16)

    @pl.when(active)
    def _():
        x = jnp.concatenate(_unpack_bf16_pairs(x_ref[...]), axis=1).astype(BF16)
        gu = jnp.dot(x, uw_bf[...], preferred_element_type=F32) + ub_ref[...]
        gate = jnp.minimum(gu[:, :D_FF], SWIGLU_LIMIT)
        lin = jnp.clip(gu[:, D_FF:], -SWIGLU_LIMIT, SWIGLU_LIMIT)
        act = gate * jax.nn.sigmoid(SWIGLU_ALPHA * gate) * (lin + 1.0)
        y = jnp.dot(act.astype(BF16), dw_bf[...], preferred_element_type=F32) + db_ref[...]
        y_ref[...] = _pack_bf16_pairs(y)

    @pl.when(jnp.logical_not(active))
    def _():
        y_ref[...] = jnp.zeros_like(y_ref)


def _expert_ffn(xb, blk_expert, n_used, up_w, up_b4, down_w, down_b4, layer):
    n_rows, dp = xb.shape
    d = 2 * dp
    bm = MOE_ROWS
    n_blk = n_rows // bm
    f2 = up_w.shape[-1]
    blk = jnp.arange(n_blk, dtype=jnp.int32)
    first = jnp.logical_and(jnp.concatenate([jnp.ones((1,), bool), blk_expert[1:] != blk_expert[:-1]]),
                            blk < n_used[0])
    slot = ((jnp.cumsum(first.astype(jnp.int32)) - 1) % 2).astype(jnp.int32)
    later_first = jnp.where(first, blk, n_blk)[::-1]
    next_first = jnp.concatenate([lax.cummin(later_first)[::-1][1:], jnp.full((1,), n_blk, jnp.int32)])
    nxt = jnp.where(jnp.logical_and(first, next_first < n_blk),
                    blk_expert[jnp.minimum(next_first, n_blk - 1)], -1).astype(jnp.int32)
    grid_spec = pltpu.PrefetchScalarGridSpec(
        num_scalar_prefetch=4,
        grid=(n_blk,),
        in_specs=[pl.BlockSpec((bm, dp), lambda i, be, nu, nx, sl: (i, 0)),
                  pl.BlockSpec(memory_space=pl.ANY),
                  pl.BlockSpec((None, None, 1, f2), lambda i, be, nu, nx, sl: (layer, be[i], 0, 0)),
                  pl.BlockSpec(memory_space=pl.ANY),
                  pl.BlockSpec((None, None, 1, d), lambda i, be, nu, nx, sl: (layer, be[i], 0, 0))],
        out_specs=pl.BlockSpec((bm, dp), lambda i, be, nu, nx, sl: (i, 0)),
        scratch_shapes=[pltpu.VMEM((2, d, f2), F32), pltpu.VMEM((2, f2 // 2, d), F32),
                        pltpu.VMEM((d, f2), BF16), pltpu.VMEM((f2 // 2, d), BF16),
                        pltpu.SemaphoreType.DMA((2, 2))],
    )
    return pl.pallas_call(
        functools.partial(_ffn_body, layer=layer),
        out_shape=jax.ShapeDtypeStruct((n_rows, dp), jnp.int32),
        grid_spec=grid_spec,
        compiler_params=_cparams(("arbitrary",)),
        name="moe_expert_ffn",
    )(blk_expert, n_used, nxt, slot, xb, up_w, up_b4, down_w, down_b4)


SC_GATHER_ROWS = 64


def _sc_gather_rows(table, idx):
    n, d = idx.shape[0], table.shape[1]
    info = plsc.get_sparse_core_info()
    n_cores, n_sub = info.num_cores, info.num_subcores
    n_workers = n_cores * n_sub
    chunk = SC_GATHER_ROWS
    per_worker = n // n_workers
    n_chunks = per_worker // chunk
    assert n_chunks * chunk * n_workers == n and n_chunks % 2 == 0
    mesh = plsc.VectorSubcoreMesh(core_axis_name="c", subcore_axis_name="s")

    @functools.partial(
        pl.kernel, mesh=mesh,
        out_type=jax.ShapeDtypeStruct((n, d), table.dtype),
        scratch_types=[pltpu.VMEM((n_chunks, chunk), jnp.int32), pltpu.VMEM((2, chunk, d), table.dtype),
                       pltpu.SemaphoreType.DMA((2,))],
    )
    def gather_kernel(table_hbm, idx_hbm, out_hbm, idx_v, rows_v, sems):
        wid = lax.axis_index("s") * n_cores + lax.axis_index("c")
        base = wid * per_worker
        pltpu.sync_copy(idx_hbm.at[wid], idx_v)

        def gather(ci, slot):
            return pltpu.make_async_copy(table_hbm.at[idx_v.at[ci]], rows_v.at[slot], sems.at[slot])

        gather(0, 0).start()

        @pl.loop(0, n_chunks, step=2)
        def _(c0):
            for slot in range(2):
                ci = c0 + slot
                gather(ci, slot).wait()

                @pl.when(ci + 1 < n_chunks)
                def _():
                    gather(ci + 1, 1 - slot).start()

                pltpu.sync_copy(rows_v.at[slot], out_hbm.at[pl.ds(base + ci * chunk, chunk)])

    return gather_kernel(table, idx.reshape(n_workers, n_chunks, chunk))


COMBINE_TM = 256


def _combine_body(gate_ref, x_ref, g_ref, y_ref, o_ref):
    gates = gate_ref[...]
    acc_lo = acc_hi = None
    for k in range(TOP_K):
        lo, hi = _unpack_bf16_pairs(y_ref[k])
        gk = gates[:, k:k + 1]
        acc_lo = gk * lo if acc_lo is None else acc_lo + gk * lo
        acc_hi = gk * hi if acc_hi is None else acc_hi + gk * hi
    o_ref[...] = x_ref[...] + g_ref[...] * jnp.concatenate([acc_lo, acc_hi], axis=1)


def _combine(y_kt, gates, x2, mod4, gate_slot, seq):
    t, d = x2.shape
    tm = COMBINE_TM
    tps = seq // tm
    return pl.pallas_call(
        _combine_body,
        out_shape=jax.ShapeDtypeStruct((t, d), F32),
        grid=(t // tm,),
        in_specs=[pl.BlockSpec((tm, gates.shape[1]), lambda i: (i, 0)),
                  pl.BlockSpec((tm, d), lambda i: (i, 0)),
                  pl.BlockSpec((None, None, 1, d), lambda i: (i // tps, gate_slot, 0, 0)),
                  pl.BlockSpec((TOP_K, tm, d // 2), lambda i: (0, i, 0))],
        out_specs=pl.BlockSpec((tm, d), lambda i: (i, 0)),
        compiler_params=_cparams(("parallel",)),
        name="moe_combine",
    )(gates, x2, mod4, y_kt)


def _moe_layer(x2, gain, mod4, router_w, router_b, up_w, up_b, down_w, down_b, layer, seq):
    t, d = x2.shape
    rw_pad = jnp.pad(router_w[layer], ((0, 0), (0, LANES - N_EXPERTS)))
    rb_pad = jnp.pad(router_b[layer], (0, LANES - N_EXPERTS)).reshape(1, LANES)
    h2, idx, gates, rank, counts = _router(x2, gain, mod4, 3, 4, rw_pad, rb_pad, seq)

    bm = MOE_ROWS
    counts = counts[:, 0].astype(jnp.int32)
    padded = (counts + bm - 1) // bm * bm
    p_ends = jnp.cumsum(padded)
    p_starts = p_ends - padded
    n_rows = t * TOP_K + N_EXPERTS * bm
    n_blk = n_rows // bm
    onehot = idx[:TOP_K, :, None] == jnp.arange(N_EXPERTS, dtype=jnp.int32)[None, None, :]
    dest = jnp.sum(jnp.where(onehot, p_starts[None, None, :], 0), axis=-1) + rank[:TOP_K]
    dest_flat = dest.reshape(TOP_K * t)
    blk_start = jnp.arange(n_blk, dtype=jnp.int32) * bm
    blk_expert = jnp.sum((blk_start[:, None] >= p_ends[None, :]).astype(jnp.int32), axis=-1)
    blk_expert = jnp.minimum(blk_expert, N_EXPERTS - 1).astype(jnp.int32)
    n_used = (p_ends[-1:] // bm).astype(jnp.int32)

    xb = _sc_gather_rows(h2, _sc_row_tokens(dest_flat, n_rows, t))
    f2 = up_w.shape[-1]
    y_rows = _expert_ffn(xb, blk_expert, n_used, up_w, up_b.reshape(up_b.shape[0], N_EXPERTS, 1, f2),
                         down_w, down_b.reshape(down_b.shape[0], N_EXPERTS, 1, d), layer)
    y_kt = _sc_gather_rows(y_rows, dest_flat).reshape(TOP_K, t, d // 2)
    return _combine(y_kt, gates.T, x2, mod4, 5, seq)


QKV_TM = 256


def _rope_tables(pos_row, invf_col):
    tm = pos_row.shape[1]
    half = ROPE_DIM // 2
    ang = invf_col * pos_row
    expand = (lax.broadcasted_iota(jnp.int32, (half, LANES), 1) % half
              == lax.broadcasted_iota(jnp.int32, (half, LANES), 0)).astype(BF16)

    def to_lanes(x):
        hi = x.astype(BF16)
        rest = x - hi.astype(F32)
        mid = rest.astype(BF16)
        parts = (hi, mid, (rest - mid.astype(F32)).astype(BF16))
        return sum(lax.dot_general(p, expand, (((0,), (0,)), ((), ())), preferred_element_type=F32) for p in parts)

    c = to_lanes(jnp.cos(ang))
    s = to_lanes(jnp.sin(ang))
    d = lax.broadcasted_iota(jnp.int32, (tm, LANES), 1) % B_HEAD_DIM
    cos_t = jnp.where(d < ROPE_DIM, c, 1.0)
    sin_lo = jnp.where(d < half, -s, 0.0)
    sin_hi = jnp.where(jnp.logical_and(d >= half, d < ROPE_DIM), s, 0.0)
    return cos_t, sin_lo, sin_hi


def _head_norm_rope(x, gain_row, tables):
    cos_t, sin_lo, sin_hi = tables
    same_head = (lax.broadcasted_iota(jnp.int32, (LANES, LANES), 0) // B_HEAD_DIM
                 == lax.broadcasted_iota(jnp.int32, (LANES, LANES), 1) // B_HEAD_DIM).astype(BF16)
    sq_hi, sq_lo = _split_bf16(x * x)
    ss = (jnp.dot(sq_hi, same_head, preferred_element_type=F32)
          + jnp.dot(sq_lo, same_head, preferred_element_type=F32))
    xn = x * lax.rsqrt(ss * (1.0 / B_HEAD_DIM) + EPS) * gain_row
    half = ROPE_DIM // 2
    return (xn * cos_t + pltpu.roll(xn, LANES - half, 1) * sin_lo + pltpu.roll(xn, half, 1) * sin_hi)


def _qkv_body(x_ref, pos_ref, gq_ref, shq_ref, scq_ref, gkv_ref, shkv_ref, sckv_ref, wq_ref, wkv_ref,
              qg_ref, kg_ref, invf_ref, q_ref, k_ref, v_ref):
    x = x_ref[...]
    y = x * lax.rsqrt(jnp.mean(x * x, axis=-1, keepdims=True) + EPS)
    hq = ((y * gq_ref[...]) * (1.0 + scq_ref[...]) + shq_ref[...]).astype(BF16)
    hkv = ((y * gkv_ref[...]) * (1.0 + sckv_ref[...]) + shkv_ref[...]).astype(BF16)
    tables = _rope_tables(pos_ref[...].astype(F32), invf_ref[...])
    kv = jnp.dot(hkv, wkv_ref[...], preferred_element_type=F32)
    k_ref[...] = _head_norm_rope(kv[:, :LANES], kg_ref[...], tables).astype(BF16)
    v_ref[...] = kv[:, LANES:].astype(BF16)
    q = jnp.dot(hq, wq_ref[...], preferred_element_type=F32)
    scale = B_HEAD_DIM ** -0.5
    for p in range(q.shape[-1] // LANES):
        qp = _head_norm_rope(q[:, p * LANES:(p + 1) * LANES], qg_ref[...], tables)
        q_ref[:, p * LANES:(p + 1) * LANES] = (qp * scale).astype(BF16)


def _qkv(x2, pos2, gq, mod4, gkv, kvmod4, wq_bf, wkv_bf, qg2, kg2, invf, seq):
    t, d = x2.shape
    tm = QKV_TM
    tps = seq // tm
    nq = wq_bf.shape[-1]
    nkv = wkv_bf.shape[-1]
    row = lambda n: pl.BlockSpec((1, n), lambda i: (0, 0))
    modspec = lambda slot: pl.BlockSpec((None, None, 1, d), lambda i, slot=slot: (i // tps, slot, 0, 0))
    return pl.pallas_call(
        _qkv_body,
        out_shape=(jax.ShapeDtypeStruct((t, nq), BF16), jax.ShapeDtypeStruct((t, LANES), BF16),
                   jax.ShapeDtypeStruct((t, LANES), BF16)),
        grid=(t // tm,),
        in_specs=[pl.BlockSpec((tm, d), lambda i: (i, 0)),
                  pl.BlockSpec((None, 1, tm), lambda i: (i, 0, 0)),
                  row(d), modspec(0), modspec(1),
                  row(d), modspec(0), modspec(1),
                  pl.BlockSpec((d, nq), lambda i: (0, 0)),
                  pl.BlockSpec((d, nkv), lambda i: (0, 0)),
                  row(LANES), row(LANES), pl.BlockSpec(invf.shape, lambda i: (0, 0))],
        out_specs=(pl.BlockSpec((tm, nq), lambda i: (i, 0)),
                   pl.BlockSpec((tm, LANES), lambda i: (i, 0)),
                   pl.BlockSpec((tm, LANES), lambda i: (i, 0))),
        compiler_params=_cparams(("parallel",)),
        name="swa_qkv_proj",
    )(x2, pos2.reshape(t // tm, 1, tm), gq, mod4, mod4, gkv, kvmod4, kvmod4, wq_bf, wkv_bf, qg2, kg2, invf)


def _attn_body(sink_ref, q_ref, kc_ref, kp_ref, vc_ref, vp_ref, w_ref, x_ref, g_ref, o_ref):
    n = pl.program_id(1)
    w = WINDOW
    qi = lax.broadcasted_iota(jnp.int32, (w, 2 * w), 0) + w
    kj = lax.broadcasted_iota(jnp.int32, (w, 2 * w), 1)
    band = jnp.logical_and(kj <= qi, qi - kj < w)
    mask = jnp.logical_and(band, jnp.logical_or(n > 0, kj >= w))
    hd = B_HEAD_DIM
    lane = lax.broadcasted_iota(jnp.int32, (2 * w, LANES), 1)
    first = lane < hd
    kfull = jnp.concatenate([kp_ref[...], kc_ref[...]], axis=0).astype(F32)
    vfull = jnp.concatenate([vp_ref[...], vc_ref[...]], axis=0).astype(F32)
    placed = {}
    for g in range(B_KV_HEADS):
        for name, full in (("k", kfull), ("v", vfull)):
            swapped = pltpu.roll(full, hd, 1)
            own_first = full if g == 0 else swapped
            own_second = swapped if g == 0 else full
            placed[name, g, 0] = jnp.where(first, own_first, 0.0).astype(BF16)
            placed[name, g, 1] = jnp.where(first, 0.0, own_second).astype(BF16)
    pairs = B_Q_HEADS // 2
    heads = [(p, side) for p in range(pairs) for side in range(2)]
    scores = []
    for p, side in heads:
        g = (2 * p) // B_GROUP
        q_pair = q_ref[:, p * LANES:(p + 1) * LANES]
        s = lax.dot_general(q_pair, placed["k", g, side], (((1,), (1,)), ((), ())), preferred_element_type=F32)
        scores.append(jnp.where(mask, s, NEG_INF))
    probs, denoms = [], []
    for (p, side), s in zip(heads, scores):
        sink = sink_ref[0, 2 * p + side]
        m = jnp.maximum(jnp.max(jnp.maximum(s[:, :w], s[:, w:]), axis=-1, keepdims=True), sink)
        e = jnp.exp(s - m)
        denoms.append(jnp.sum(e[:, :w] + e[:, w:], axis=-1, keepdims=True) + jnp.exp(sink - m))
        probs.append(e.astype(BF16))
    outs = []
    for (p, side), pr, den in zip(heads, probs, denoms):
        g = (2 * p) // B_GROUP
        outs.append(jnp.dot(pr, placed["v", g, side], preferred_element_type=F32) / den)
    attn = jnp.concatenate([(outs[2 * p] + outs[2 * p + 1]).astype(BF16) for p in range(pairs)], axis=1)
    o_ref[...] = x_ref[...] + g_ref[...] * jnp.dot(attn, w_ref[...], preferred_element_type=F32)


def _attention_residual(q, k, v, sinks2, w_out_bf, x2, mod4, gate_slot, bsz, seq):
    t, nq = q.shape
    d = x2.shape[-1]
    w = WINDOW
    nb = seq // w
    cur = lambda b, n: (b * nb + n, 0)
    prev = lambda b, n: (b * nb + jnp.maximum(n - 1, 0), 0)
    return pl.pallas_call(
        _attn_body,
        out_shape=jax.ShapeDtypeStruct((t, d), F32),
        grid=(bsz, nb),
        in_specs=[pl.BlockSpec(memory_space=pltpu.SMEM),
                  pl.BlockSpec((w, nq), cur),
                  pl.BlockSpec((w, LANES), cur), pl.BlockSpec((w, LANES), prev),
                  pl.BlockSpec((w, LANES), cur), pl.BlockSpec((w, LANES), prev),
                  pl.BlockSpec((nq, d), lambda b, n: (0, 0)),
                  pl.BlockSpec((w, d), cur),
                  pl.BlockSpec((None, None, 1, d), lambda b, n: (b, gate_slot, 0, 0))],
        out_specs=pl.BlockSpec((w, d), cur),
        compiler_params=_cparams(("parallel", "parallel")),
        name="swa_sink_attention",
    )(sinks2, q, k, k, v, v, w_out_bf, x2, mod4)


def kernel(x, c, positions, ada_w, ada_b, norm_gain, a_w_in, a_conv, a_log, a_dt_bias, a_out_gain, a_w_out,
           kv_ada_w, kv_ada_b, kv_norm_gain, kv_w, k_norm_gain, b_w_q, q_norm_gain, b_sinks, b_w_out,
           router_w, router_b, up_w, up_b, down_w, down_b):
    bsz, seq, d = x.shape
    t = bsz * seq
    depth = ada_w.shape[0]
    x2 = x.reshape(t, d)
    ada_b3 = ada_b.reshape(depth, 1, 6 * d)

    for layer in range(depth):
        mod4 = _modulation(c, ada_w, ada_b3, layer).reshape(bsz, 6, 1, d)
        gain1 = norm_gain[layer, 0].reshape(1, d)
        gain2 = norm_gain[layer, 1].reshape(1, d)
        if layer < N_A_LAYERS:
            w_in = jnp.pad(a_w_in[layer].astype(BF16), ((0, 0), (0, A_PROJ_PAD - a_w_in.shape[-1])))
            proj = _in_proj(x2, gain1, mod4, w_in, a_conv[layer], seq)
            o = _gdn(proj, a_log[layer].reshape(1, A_HEADS),
                     a_dt_bias[layer].reshape(1, A_HEADS), a_out_gain[layer].reshape(1, A_HEAD_DIM),
                     bsz, seq)
            x2 = _out_proj_residual(o, a_w_out[layer].astype(BF16), x2, mod4, 2, seq)
        else:
            j = layer - N_A_LAYERS
            kvmod4 = _modulation(c, kv_ada_w.reshape(1, d, 2 * d), kv_ada_b.reshape(1, 1, 2 * d), 0)
            kvmod4 = kvmod4.reshape(bsz, 2, 1, d)
            inv_freq = ROPE_THETA ** (-np.arange(0, ROPE_DIM, 2, dtype=np.float32) / ROPE_DIM)
            invf = jnp.asarray(inv_freq.astype(np.float32).reshape(ROPE_DIM // 2, 1))
            q, k, v = _qkv(x2, positions.reshape(t), gain1, mod4, kv_norm_gain.reshape(1, d), kvmod4,
                           b_w_q[j].astype(BF16), kv_w.astype(BF16),
                           jnp.tile(q_norm_gain[j], 2).reshape(1, LANES),
                           jnp.tile(k_norm_gain, 2).reshape(1, LANES), invf, seq)
            x2 = _attention_residual(q, k, v, b_sinks[j].reshape(1, B_Q_HEADS), b_w_out[j].astype(BF16),
                                     x2, mod4, 2, bsz, seq)
        x2 = _moe_layer(x2, gain2, mod4, router_w, router_b, up_w, up_b, down_w, down_b, layer, seq)
    return x2.reshape(bsz, seq, d)
```

```python
import dataclasses
import functools

import jax
import jax.numpy as jnp
import numpy as np
from jax import lax
from jax.experimental import pallas as pl
from jax.experimental.pallas import tpu as pltpu
from jax.experimental.pallas import tpu_sc as plsc

F32 = jnp.float32
BF16 = jnp.bfloat16
HIGHEST = lax.Precision.HIGHEST

D_MODEL = 1024
N_A_LAYERS = 1

A_HEAD_DIM = 128
A_HEADS = 8
A_WIDTH = 1024
A_CONV = 4
A_CHUNK = 64
A_PROJ_PAD = 4608

B_HEAD_DIM = 64
B_Q_HEADS = 16
B_KV_HEADS = 2
B_GROUP = 8
WINDOW = 128
ROPE_DIM = 16
ROPE_THETA = 500000.0

N_EXPERTS = 32
TOP_K = 4
D_FF = 1024
SWIGLU_LIMIT = 7.0
SWIGLU_ALPHA = 1.702
MOE_ROWS = 512
MOE_PARTS = 2

EPS = 1e-6
LANES = 128
NEG_INF = float("-inf")

VMEM_LIMIT = 56 * 1024 * 1024


def _cparams(sem, vmem=VMEM_LIMIT):
    return pltpu.CompilerParams(dimension_semantics=sem, vmem_limit_bytes=vmem)


def _silu(x):
    return x * jax.nn.sigmoid(x)


HI_HALF = -65536


def _pack_bf16_pairs(x):
    n = x.shape[-1] // 2
    bits = lax.bitcast_convert_type(x.astype(BF16).astype(F32), jnp.int32)
    return jnp.bitwise_or(lax.shift_right_logical(bits[:, :n], 16), jnp.bitwise_and(bits[:, n:], HI_HALF))


def _unpack_bf16_pairs(w):
    lo = lax.bitcast_convert_type(lax.shift_left(w, 16), F32)
    hi = lax.bitcast_convert_type(jnp.bitwise_and(w, HI_HALF), F32)
    return lo, hi


def _ada_norm(x, gain, shift, scale):
    y = x * lax.rsqrt(jnp.mean(x * x, axis=-1, keepdims=True) + EPS)
    return (y * gain) * (1.0 + scale) + shift


def _mod_body(c_ref, w_ref, b_ref, o_ref):
    o_ref[...] = jnp.dot(_silu(c_ref[...]), w_ref[...], preferred_element_type=F32,
                         precision=HIGHEST) + b_ref[...]


def _modulation(c, w3, b3, layer):
    bsz, d = c.shape
    n = w3.shape[-1]
    tn = 1024
    return pl.pallas_call(
        _mod_body,
        out_shape=jax.ShapeDtypeStruct((bsz, n), F32),
        grid=(n // tn,),
        in_specs=[pl.BlockSpec((bsz, d), lambda j: (0, 0)),
                  pl.BlockSpec((None, d, tn), lambda j: (layer, 0, j)),
                  pl.BlockSpec((None, 1, tn), lambda j: (layer, 0, j))],
        out_specs=pl.BlockSpec((bsz, tn), lambda j: (0, j)),
        compiler_params=_cparams(("arbitrary",)),
        name="adaln_mod",
    )(c, w3, b3)


INPROJ_TM = 256
INPROJ_TN = 256
HALO = 8


def _inproj_body(x_ref, gain_ref, sh_ref, sc_ref, w_ref, conv_ref, o_ref, ext, *, tiles_per_seq):
    i = pl.program_id(0)
    tm = x_ref.shape[0]
    hd = A_HEAD_DIM
    n = o_ref.shape[-1]
    tn = INPROJ_TN
    h = _ada_norm(x_ref[...], gain_ref[...], sh_ref[...], sc_ref[...]).astype(BF16)

    @pl.when(i % tiles_per_seq == 0)
    def _():
        ext[0:HALO, :] = jnp.zeros((HALO, ext.shape[-1]), F32)

    def conv_silu_norm(raw, n0):
        cols = slice(n0, n0 + tn)
        ext[HALO:HALO + tm, cols] = raw
        acc = raw * conv_ref[A_CONV - 1:A_CONV, cols]
        for s in range(1, A_CONV):
            acc = acc + ext[HALO - s:HALO - s + tm, cols] * conv_ref[A_CONV - 1 - s:A_CONV - s, cols]
        ext[0:HALO, cols] = raw[tm - HALO:, :]
        y = _silu(acc)
        if n0 >= 2 * A_WIDTH:
            return y
        scale = hd ** -0.5 if n0 < A_WIDTH else 1.0
        heads = []
        for c0 in range(0, tn, hd):
            yh = y[:, c0:c0 + hd]
            heads.append(yh * (lax.rsqrt(jnp.sum(yh * yh, axis=-1, keepdims=True) + EPS) * scale))
        return jnp.concatenate(heads, axis=1)

    matmul = lambda n0: jnp.dot(h, w_ref[:, n0:n0 + tn], preferred_element_type=F32)
    conv_chunks = list(range(0, 3 * A_WIDTH, tn))
    plain_chunks = list(range(3 * A_WIDTH, n, tn))
    every = max(1, len(conv_chunks) // max(1, len(plain_chunks)))
    raw = matmul(conv_chunks[0])
    for idx, n0 in enumerate(conv_chunks):
        nxt = matmul(conv_chunks[idx + 1]) if idx + 1 < len(conv_chunks) else None
        if idx % every == every - 1 and plain_chunks:
            p0 = plain_chunks.pop(0)
            o_ref[:, p0:p0 + tn] = matmul(p0)
        o_ref[:, n0:n0 + tn] = conv_silu_norm(raw, n0)
        raw = nxt
    for p0 in plain_chunks:
        o_ref[:, p0:p0 + tn] = matmul(p0)


def _in_proj(x2, gain, mod4, w_bf, conv_w, seq):
    t, d = x2.shape
    n = w_bf.shape[-1]
    tm = INPROJ_TM
    tps = seq // tm
    return pl.pallas_call(
        functools.partial(_inproj_body, tiles_per_seq=tps),
        out_shape=jax.ShapeDtypeStruct((t, n), F32),
        grid=(t // tm,),
        in_specs=[pl.BlockSpec((tm, d), lambda i: (i, 0)),
                  pl.BlockSpec((1, d), lambda i: (0, 0)),
                  pl.BlockSpec((None, None, 1, d), lambda i: (i // tps, 0, 0, 0)),
                  pl.BlockSpec((None, None, 1, d), lambda i: (i // tps, 1, 0, 0)),
                  pl.BlockSpec((d, n), lambda i: (0, 0)),
                  pl.BlockSpec(conv_w.shape, lambda i: (0, 0))],
        out_specs=pl.BlockSpec((tm, n), lambda i: (i, 0)),
        scratch_shapes=[pltpu.VMEM((HALO + tm, conv_w.shape[-1]), F32)],
        compiler_params=_cparams(("arbitrary",)),
        name="gdn_in_proj",
    )(x2, gain, mod4, mod4, w_bf, conv_w)


GDN_HG = 2
GDN_RB = 256
GDN_SC = 128
GDN_SUB = 4


def _split_bf16(x):
    hi = x.astype(BF16)
    return hi, (x - hi.astype(F32)).astype(BF16)


def _mm3(a_hi, a_lo, b_hi, b_lo, dims=(((1,), (0,)), ((), ()))):
    dg = functools.partial(lax.dot_general, dimension_numbers=dims, preferred_element_type=F32)
    return dg(a_hi, b_hi) + (dg(a_lo, b_hi) + dg(a_hi, b_lo))


def _dot3(a, b, dims):
    return _mm3(*_split_bf16(a), *_split_bf16(b), dims)


def _gdn_body(alog_ref, dtb_ref, q_ref, k_ref, v_ref, z_ref, ab_ref, og_ref,
              o_ref, bs, gcs, mp, bc, qp, op):
    seq = q_ref.shape[0]
    hd = A_HEAD_DIM
    ch = A_CHUNK
    g_idx = pl.program_id(1)

    lane = lax.broadcasted_iota(jnp.int32, (GDN_RB, hd), 1)
    row = lax.broadcasted_iota(jnp.int32, (GDN_RB, hd), 0)
    row_in_chunk = row % ch

    sc_rows = GDN_SC
    per_sc = sc_rows // ch
    rb_per_step = sc_rows * GDN_SUB // GDN_RB
    head_cols = lambda hh: slice(hh * hd, (hh + 1) * hd)

    def prep_stages(j):
        for rbi in range(rb_per_step):
            r = j * rb_per_step + rbi
            start = pl.multiple_of(r * GDN_RB, GDN_RB)
            rows = pl.ds(start, GDN_RB)
            for hh in range(GDN_HG):
                head = g_idx * GDN_HG + hh
                ab = ab_ref[rows, :]
                a_col = jnp.sum(jnp.where(lane == head, ab, 0.0), axis=-1, keepdims=True)
                b_col = jnp.sum(jnp.where(lane == head + A_HEADS, ab, 0.0), axis=-1, keepdims=True)
                bs[hh, rows, :] = jnp.broadcast_to(jax.nn.sigmoid(b_col), (GDN_RB, hd))
                xa = a_col + dtb_ref[0, head]
                softplus = jnp.maximum(xa, 0.0) + jnp.log1p(jnp.exp(-jnp.abs(xa)))
                g = jnp.broadcast_to(-jnp.exp(alog_ref[0, head]) * softplus, (GDN_RB, hd))
                s = 1
                while s < ch:
                    g = g + jnp.where(row_in_chunk >= s, pltpu.roll(g, s, 0), 0.0)
                    s *= 2
                gcs[hh, rows, :] = g
                yield

    ri = lax.broadcasted_iota(jnp.int32, (sc_rows, sc_rows), 0)
    cj = lax.broadcasted_iota(jnp.int32, (sc_rows, sc_rows), 1)
    same_chunk = (ri // ch) == (cj // ch)
    incl = jnp.logical_and(same_chunk, ri >= cj)
    strict = jnp.logical_and(same_chunk, ri > cj)
    eye = jnp.where(ri == cj, 1.0, 0.0)
    chunk_of_col = lax.broadcasted_iota(jnp.int32, (hd, sc_rows), 1) // ch
    lanes_nt = (((1,), (1,)), ((), ()))
    plain = (((1,), (0,)), ((), ()))

    chunks_per_step = GDN_SUB * per_sc
    n_steps = seq // (sc_rows * GDN_SUB)

    def local_stages(i):
        chains = [(i * GDN_SUB + sub, hh) for sub in range(GDN_SUB) for hh in range(GDN_HG)]
        rows_of = lambda blk: pl.ds(pl.multiple_of(blk * sc_rows, sc_rows), sc_rows)

        def start(blk, hh):
            rows = rows_of(blk)
            k = k_ref[rows, head_cols(hh)]
            gc = gcs[hh, rows, :]
            decay = jnp.where(incl, jnp.exp(jnp.minimum(gc - gc.T[0:1, :], 0.0)), 0.0)
            k_bf = k.astype(BF16)
            kk = lax.dot_general((k * bs[hh, rows, :]).astype(BF16), k_bf, lanes_nt, preferred_element_type=F32)
            qk = lax.dot_general(q_ref[rows, head_cols(hh)].astype(BF16), k_bf, lanes_nt, preferred_element_type=F32)
            return jnp.where(strict, -(kk * decay), 0.0), (qk * decay).astype(BF16)

        started = [start(*c) for c in chains]
        yield
        qks = [s[1] for s in started]
        invs = [eye + s[0] for s in started]
        pws = [_dot3(s[0], s[0], plain) for s in started]
        yield
        for step in range(1, 6):
            for n in range(len(chains)):
                pw_hi, pw_lo = _split_bf16(pws[n])
                inv_hi, inv_lo = _split_bf16(invs[n])
                if step < 5:
                    prod = _mm3(jnp.concatenate([pw_hi, inv_hi], axis=0),
                                jnp.concatenate([pw_lo, inv_lo], axis=0), pw_hi, pw_lo)
                    pws[n] = prod[:sc_rows]
                    invs[n] = invs[n] + prod[sc_rows:]
                else:
                    invs[n] = invs[n] + _mm3(inv_hi, inv_lo, pw_hi, pw_lo)
            yield

        def solve(n, blk, hh):
            rows = rows_of(blk)
            beta = bs[hh, rows, :]
            rhs = jnp.concatenate([v_ref[rows, head_cols(hh)] * beta, k_ref[rows, head_cols(hh)] * beta * jnp.exp(gcs[hh, rows, :])], axis=1)
            return _dot3(invs[n], rhs, plain).astype(BF16)

        uws = [solve(n, *c) for n, c in enumerate(chains)]
        yield

        def finish(n, blk, hh):
            rows = rows_of(blk)
            k = k_ref[rows, head_cols(hh)]
            gc = gcs[hh, rows, :]
            res = jnp.dot(qks[n], uws[n], preferred_element_type=F32)
            op[hh, rows, :] = res[:, :hd]
            qp[hh, rows, :] = (q_ref[rows, head_cols(hh)] * jnp.exp(gc) - res[:, hd:]).astype(BF16)
            gl = jnp.concatenate(
                [jnp.broadcast_to(gc[(j + 1) * ch - 1:(j + 1) * ch, :], (ch, hd)) for j in range(per_sc)], axis=0)
            kt_t = (k * jnp.exp(gl - gc)).T
            for j in range(per_sc):
                kt_j = jnp.where(chunk_of_col == j, kt_t, 0.0).astype(BF16)
                bm = jnp.dot(kt_j, uws[n], preferred_element_type=F32)
                bc[hh, blk * per_sc + j] = bm[:, :hd]
                mp[hh, blk * per_sc + j] = bm[:, hd:].astype(BF16)

        for n, c in enumerate(chains):
            finish(n, *c)

    def scan_stages(j, states):
        for cc in range(chunks_per_step):
            c = j * chunks_per_step + cc
            r0 = pl.multiple_of(c * ch, ch)
            rows = pl.ds(r0, ch)
            for hh in range(GDN_HG):
                col0 = hh * hd
                state = states[hh]
                s_bf = state.astype(BF16)
                o = jnp.dot(qp[hh, rows, :], s_bf, preferred_element_type=F32) + op[hh, rows, :]
                g_tot = jnp.exp(gcs[hh, pl.ds(r0 + ch - 1, 1), :])
                states[hh] = state * g_tot - jnp.dot(mp[hh, c], s_bf, preferred_element_type=F32) + bc[hh, c]
                on = o * lax.rsqrt(jnp.mean(o * o, axis=-1, keepdims=True) + EPS) * og_ref[...]
                z = z_ref[rows, col0:col0 + hd]
                o_ref[rows, col0:col0 + hd] = (on * _silu(z)).astype(BF16)
            yield

    def interleave(*gens):
        live = list(gens)
        while live:
            for g in list(live):
                try:
                    next(g)
                except StopIteration:
                    live.remove(g)

    assert n_steps >= 2
    interleave(prep_stages(0))
    interleave(local_stages(0), prep_stages(1))

    def pipelined(i, states):
        states = list(states)
        interleave(local_stages(i), scan_stages(i - 1, states), prep_stages(i + 1))
        return tuple(states)

    states = list(lax.fori_loop(1, n_steps - 1, pipelined,
                                tuple(jnp.zeros((hd, hd), F32) for _ in range(GDN_HG))))
    interleave(local_stages(n_steps - 1), scan_stages(n_steps - 2, states))
    interleave(scan_stages(n_steps - 1, states))


def _gdn(proj, a_log, dt_bias, out_gain, bsz, seq):
    t = proj.shape[0]
    hd = A_HEAD_DIM
    wb = hd * GDN_HG
    ng = A_HEADS // GDN_HG
    per = A_WIDTH // wb
    n_chunks = seq // A_CHUNK
    smem = pl.BlockSpec(memory_space=pltpu.SMEM)
    seq_spec = lambda off: pl.BlockSpec((seq, wb), lambda b, g, off=off: (b, off + g))
    sc = lambda dt=F32: pltpu.VMEM((GDN_HG, seq, hd), dt)
    return pl.pallas_call(
        _gdn_body,
        out_shape=jax.ShapeDtypeStruct((t, A_WIDTH), BF16),
        grid=(bsz, ng),
        in_specs=[smem, smem,
                  seq_spec(0), seq_spec(per), seq_spec(2 * per), seq_spec(3 * per),
                  pl.BlockSpec((seq, LANES), lambda b, g: (b, 4 * A_WIDTH // LANES)),
                  pl.BlockSpec((1, hd), lambda b, g: (0, 0))],
        out_specs=pl.BlockSpec((seq, wb), lambda b, g: (b, g)),
        scratch_shapes=[sc(), sc(),
                        pltpu.VMEM((GDN_HG, n_chunks, hd, hd), BF16),
                        pltpu.VMEM((GDN_HG, n_chunks, hd, hd), F32),
                        sc(BF16), sc()],
        compiler_params=_cparams(("parallel", "parallel")),
        name="gdn_core",
    )(a_log, dt_bias, proj, proj, proj, proj, proj, out_gain)


def _oproj_body(a_ref, w_ref, x_ref, g_ref, o_ref):
    y = jnp.dot(a_ref[...], w_ref[...], preferred_element_type=F32)
    o_ref[...] = x_ref[...] + g_ref[...] * y


def _out_proj_residual(a_bf, w_bf, x2, mod4, gate_slot, seq):
    t, d = x2.shape
    kdim = a_bf.shape[-1]
    tm = 512
    tps = seq // tm
    return pl.pallas_call(
        _oproj_body,
        out_shape=jax.ShapeDtypeStruct((t, d), F32),
        grid=(t // tm,),
        in_specs=[pl.BlockSpec((tm, kdim), lambda i: (i, 0)),
                  pl.BlockSpec((kdim, d), lambda i: (0, 0)),
                  pl.BlockSpec((tm, d), lambda i: (i, 0)),
                  pl.BlockSpec((None, None, 1, d), lambda i: (i // tps, gate_slot, 0, 0))],
        out_specs=pl.BlockSpec((tm, d), lambda i: (i, 0)),
        compiler_params=_cparams(("parallel",)),
        name="out_proj_residual",
    )(a_bf, w_bf, x2, mod4)


ROUTER_TM = 512


K_ROWS = 8


def _router_body(x_ref, gain_ref, sh_ref, sc_ref, rw_ref, rb_ref, h_ref, idx_ref, gate_ref, rank_ref,
                 cnt_ref, carry):
    i = pl.program_id(0)

    @pl.when(i == 0)
    def _():
        carry[...] = jnp.zeros_like(carry)

    tm = x_ref.shape[0]
    h = _ada_norm(x_ref[...], gain_ref[...], sh_ref[...], sc_ref[...])
    h_ref[...] = _pack_bf16_pairs(h)
    logits = _dot3(h, rw_ref[...], (((1,), (0,)), ((), ()))) + rb_ref[...]
    work = logits.T[:N_EXPERTS, :]
    expert = lax.broadcasted_iota(jnp.int32, (N_EXPERTS, tm), 0)
    tops, sels = [], []
    for _ in range(TOP_K):
        m = jnp.max(work, axis=0, keepdims=True)
        sel = jnp.min(jnp.where(work == m, expert, N_EXPERTS), axis=0, keepdims=True)
        work = jnp.where(expert == sel, NEG_INF, work)
        tops.append(m)
        sels.append(sel)
    exps = [jnp.exp(m - tops[0]) for m in tops]
    denom = exps[0] + exps[1] + exps[2] + exps[3]
    onehot = jnp.zeros((N_EXPERTS, tm), F32)
    for sel in sels:
        onehot = onehot + jnp.where(expert == sel, 1.0, 0.0)
    ti = lax.broadcasted_iota(jnp.int32, (tm, tm), 0)
    tj = lax.broadcasted_iota(jnp.int32, (tm, tm), 1)
    earlier = (ti < tj).astype(BF16)
    before = jnp.dot(onehot.astype(BF16), earlier, preferred_element_type=F32) + carry[:, 0:1]
    ranks = [jnp.sum(jnp.where(expert == sel, before, 0.0), axis=0, keepdims=True) for sel in sels]
    pad_i = jnp.zeros((K_ROWS - TOP_K, tm), jnp.int32)
    idx_ref[...] = jnp.concatenate(sels + [pad_i], axis=0)
    gate_ref[...] = jnp.concatenate([e / denom for e in exps] + [pad_i.astype(F32)], axis=0)
    rank_ref[...] = jnp.concatenate([r.astype(jnp.int32) for r in ranks] + [pad_i], axis=0)
    carry[...] = carry[...] + jnp.sum(onehot, axis=1, keepdims=True)
    cnt_ref[...] = carry[...]


def _router(x2, gain, mod4, sh_slot, sc_slot, rw_pad, rb_pad, seq):
    t, d = x2.shape
    tm = ROUTER_TM
    tps = seq // tm
    tok = lambda dt: jax.ShapeDtypeStruct((K_ROWS, t), dt)
    tok_spec = pl.BlockSpec((K_ROWS, tm), lambda i: (0, i))
    cnt_spec = pl.BlockSpec((N_EXPERTS, LANES), lambda i: (0, 0))
    return pl.pallas_call(
        _router_body,
        out_shape=(jax.ShapeDtypeStruct((t, d // 2), jnp.int32), tok(jnp.int32), tok(F32), tok(jnp.int32),
                   jax.ShapeDtypeStruct((N_EXPERTS, LANES), F32)),
        grid=(t // tm,),
        in_specs=[pl.BlockSpec((tm, d), lambda i: (i, 0)),
                  pl.BlockSpec((1, d), lambda i: (0, 0)),
                  pl.BlockSpec((None, None, 1, d), lambda i: (i // tps, sh_slot, 0, 0)),
                  pl.BlockSpec((None, None, 1, d), lambda i: (i // tps, sc_slot, 0, 0)),
                  pl.BlockSpec((d, LANES), lambda i: (0, 0)),
                  pl.BlockSpec((1, LANES), lambda i: (0, 0))],
        out_specs=(pl.BlockSpec((tm, d // 2), lambda i: (i, 0)), tok_spec, tok_spec, tok_spec, cnt_spec),
        scratch_shapes=[pltpu.VMEM((N_EXPERTS, LANES), F32)],
        compiler_params=_cparams(("arbitrary",)),
        name="moe_router",
    )(x2, gain, mod4, mod4, rw_pad, rb_pad)


SC_LANES = 16
SC_INDEX_CHUNK = 8192


def _sc_row_tokens(dest_flat, n_rows, n_tok):
    n_assign = dest_flat.shape[0]
    assert n_rows % SC_LANES == 0 and n_assign % SC_INDEX_CHUNK == 0
    n_cores = plsc.get_sparse_core_info().num_cores
    mesh = plsc.VectorSubcoreMesh(core_axis_name="c", subcore_axis_name="s")

    @functools.partial(
        pl.kernel, mesh=mesh,
        out_type=jax.ShapeDtypeStruct((n_rows,), jnp.int32),
        scratch_types=[pltpu.VMEM((n_rows,), jnp.int32), pltpu.VMEM((SC_INDEX_CHUNK,), jnp.int32)],
        compiler_params=dataclasses.replace(pltpu.CompilerParams(), needs_layout_passes=False),
    )
    def row_token_kernel(dest_hbm, out_hbm, rt_v, d_v):
        wid = lax.axis_index("s") * n_cores + lax.axis_index("c")

        @pl.when(wid == 0)
        def _():
            lanes = lax.iota(jnp.int32, SC_LANES)

            @pl.loop(0, n_rows // SC_LANES)
            def _(i):
                rt_v[pl.ds(i * SC_LANES, SC_LANES)] = lax.rem(lanes + i * SC_LANES, n_tok)

            @pl.loop(0, n_assign // SC_INDEX_CHUNK)
            def _(c):
                pltpu.sync_copy(dest_hbm.at[pl.ds(c * SC_INDEX_CHUNK, SC_INDEX_CHUNK)], d_v)

                @pl.loop(0, SC_INDEX_CHUNK // SC_LANES)
                def _(i):
                    idx = d_v[pl.ds(i * SC_LANES, SC_LANES)]
                    tok = lax.rem(lanes + (c * SC_INDEX_CHUNK + i * SC_LANES), n_tok)
                    plsc.store_scatter(rt_v, [idx], tok)

            pltpu.sync_copy(rt_v, out_hbm)

    return row_token_kernel(dest_flat)


def _ffn_body(be_ref, nu_ref, nxt_ref, slot_ref, x_ref, uw_hbm, ub_ref, dw_hbm, db_ref, *rest, layer):
    y_ref, uw_f32, dw_f32, uw_bf, dw_bf, sems = rest[-6:]
    i = pl.program_id(0)
    first = jnp.logical_and(jnp.logical_or(i == 0, be_ref[i] != be_ref[jnp.maximum(i - 1, 0)]), i < nu_ref[0])
    active = i < nu_ref[0]
    slot = slot_ref[i]

    def weight_copies(expert, s):
        return (pltpu.make_async_copy(uw_hbm.at[layer, expert], uw_f32.at[s], sems.at[s, 0]),
                pltpu.make_async_copy(dw_hbm.at[layer, expert], dw_f32.at[s], sems.at[s, 1]))

    @pl.when(jnp.logical_and(i == 0, active))
    def _():
        for cp in weight_copies(be_ref[0], 0):
            cp.start()

    @pl.when(first)
    def _():
        for cp in weight_copies(be_ref[i], slot):
            cp.wait()

        @pl.when(nxt_ref[i] >= 0)
        def _():
            for cp in weight_copies(nxt_ref[i], 1 - slot):
                cp.start()

        rows = 128
        for r0 in range(0, uw_bf.shape[0], rows):
            uw_bf[r0:r0 + rows, :] = uw_f32[slot, r0:r0 + rows, :].astype(BF16)
        for r0 in range(0, dw_bf.shape[0], rows):
            dw_bf[r0:r0 + rows, :] = dw_f32[slot, r0:r0 + rows, :].astype(BF16)

    @pl.when(active)
    def _():
        x = jnp.concatenate(_unpack_bf16_pairs(x_ref[...]), axis=1).astype(BF16)
        gu = jnp.dot(x, uw_bf[...], preferred_element_type=F32) + ub_ref[...]
        gate = jnp.minimum(gu[:, :D_FF], SWIGLU_LIMIT)
        lin = jnp.clip(gu[:, D_FF:], -SWIGLU_LIMIT, SWIGLU_LIMIT)
        act = gate * jax.nn.sigmoid(SWIGLU_ALPHA * gate) * (lin + 1.0)
        y = jnp.dot(act.astype(BF16), dw_bf[...], preferred_element_type=F32) + db_ref[...]
        y_ref[...] = _pack_bf16_pairs(y)

    @pl.when(jnp.logical_not(active))
    def _():
        y_ref[...] = jnp.zeros_like(y_ref)


def _expert_ffn(xb, blk_expert, n_used, up_w, up_b4, down_w, down_b4, layer, n_rows_total, first_blk, y_prev):
    n_rows, dp = xb.shape
    d = 2 * dp
    bm = MOE_ROWS
    n_blk = n_rows // bm
    f2 = up_w.shape[-1]
    blk = jnp.arange(n_blk, dtype=jnp.int32)
    first = jnp.logical_and(jnp.concatenate([jnp.ones((1,), bool), blk_expert[1:] != blk_expert[:-1]]),
                            blk < n_used[0])
    slot = ((jnp.cumsum(first.astype(jnp.int32)) - 1) % 2).astype(jnp.int32)
    later_first = jnp.where(first, blk, n_blk)[::-1]
    next_first = jnp.concatenate([lax.cummin(later_first)[::-1][1:], jnp.full((1,), n_blk, jnp.int32)])
    nxt = jnp.where(jnp.logical_and(first, next_first < n_blk),
                    blk_expert[jnp.minimum(next_first, n_blk - 1)], -1).astype(jnp.int32)
    grid_spec = pltpu.PrefetchScalarGridSpec(
        num_scalar_prefetch=4,
        grid=(n_blk,),
        in_specs=[pl.BlockSpec((bm, dp), lambda i, be, nu, nx, sl: (i, 0)),
                  pl.BlockSpec(memory_space=pl.ANY),
                  pl.BlockSpec((None, None, 1, f2), lambda i, be, nu, nx, sl: (layer, be[i], 0, 0)),
                  pl.BlockSpec(memory_space=pl.ANY),
                  pl.BlockSpec((None, None, 1, d), lambda i, be, nu, nx, sl: (layer, be[i], 0, 0))]
                 + ([] if y_prev is None else [pl.BlockSpec(memory_space=pl.ANY)]),
        out_specs=pl.BlockSpec((bm, dp), lambda i, be, nu, nx, sl: (i + first_blk, 0)),
        scratch_shapes=[pltpu.VMEM((2, d, f2), F32), pltpu.VMEM((2, f2 // 2, d), F32),
                        pltpu.VMEM((d, f2), BF16), pltpu.VMEM((f2 // 2, d), BF16),
                        pltpu.SemaphoreType.DMA((2, 2))],
    )
    args = (blk_expert, n_used, nxt, slot, xb, up_w, up_b4, down_w, down_b4)
    return pl.pallas_call(
        functools.partial(_ffn_body, layer=layer),
        out_shape=jax.ShapeDtypeStruct((n_rows_total, dp), jnp.int32),
        grid_spec=grid_spec,
        input_output_aliases={} if y_prev is None else {len(args): 0},
        compiler_params=_cparams(("arbitrary",)),
        name="moe_expert_ffn",
    )(*args, *(() if y_prev is None else (y_prev,)))


SC_GATHER_ROWS = 64


def _sc_gather_rows(table, idx):
    n, d = idx.shape[0], table.shape[1]
    info = plsc.get_sparse_core_info()
    n_cores, n_sub = info.num_cores, info.num_subcores
    n_workers = n_cores * n_sub
    chunk = SC_GATHER_ROWS
    per_worker = n // n_workers
    n_chunks = per_worker // chunk
    assert n_chunks * chunk * n_workers == n and n_chunks % 2 == 0
    mesh = plsc.VectorSubcoreMesh(core_axis_name="c", subcore_axis_name="s")

    @functools.partial(
        pl.kernel, mesh=mesh,
        out_type=jax.ShapeDtypeStruct((n, d), table.dtype),
        scratch_types=[pltpu.VMEM((n_chunks, chunk), jnp.int32), pltpu.VMEM((2, chunk, d), table.dtype),
                       pltpu.SemaphoreType.DMA((2,))],
    )
    def gather_kernel(table_hbm, idx_hbm, out_hbm, idx_v, rows_v, sems):
        wid = lax.axis_index("s") * n_cores + lax.axis_index("c")
        base = wid * per_worker
        pltpu.sync_copy(idx_hbm.at[wid], idx_v)

        def gather(ci, slot):
            return pltpu.make_async_copy(table_hbm.at[idx_v.at[ci]], rows_v.at[slot], sems.at[slot])

        gather(0, 0).start()

        @pl.loop(0, n_chunks, step=2)
        def _(c0):
            for slot in range(2):
                ci = c0 + slot
                gather(ci, slot).wait()

                @pl.when(ci + 1 < n_chunks)
                def _():
                    gather(ci + 1, 1 - slot).start()

                pltpu.sync_copy(rows_v.at[slot], out_hbm.at[pl.ds(base + ci * chunk, chunk)])

    return gather_kernel(table, idx.reshape(n_workers, n_chunks, chunk))


COMBINE_TM = 256


def _combine_body(gate_ref, x_ref, g_ref, y_ref, o_ref):
    gates = gate_ref[...]
    acc_lo = acc_hi = None
    for k in range(TOP_K):
        lo, hi = _unpack_bf16_pairs(y_ref[k])
        gk = gates[:, k:k + 1]
        acc_lo = gk * lo if acc_lo is None else acc_lo + gk * lo
        acc_hi = gk * hi if acc_hi is None else acc_hi + gk * hi
    o_ref[...] = x_ref[...] + g_ref[...] * jnp.concatenate([acc_lo, acc_hi], axis=1)


def _combine(y_kt, gates, x2, mod4, gate_slot, seq):
    t, d = x2.shape
    tm = COMBINE_TM
    tps = seq // tm
    return pl.pallas_call(
        _combine_body,
        out_shape=jax.ShapeDtypeStruct((t, d), F32),
        grid=(t // tm,),
        in_specs=[pl.BlockSpec((tm, gates.shape[1]), lambda i: (i, 0)),
                  pl.BlockSpec((tm, d), lambda i: (i, 0)),
                  pl.BlockSpec((None, None, 1, d), lambda i: (i // tps, gate_slot, 0, 0)),
                  pl.BlockSpec((TOP_K, tm, d // 2), lambda i: (0, i, 0))],
        out_specs=pl.BlockSpec((tm, d), lambda i: (i, 0)),
        compiler_params=_cparams(("parallel",)),
        name="moe_combine",
    )(gates, x2, mod4, y_kt)


def _moe_layer(x2, gain, mod4, router_w, router_b, up_w, up_b, down_w, down_b, layer, seq):
    t, d = x2.shape
    rw_pad = jnp.pad(router_w[layer], ((0, 0), (0, LANES - N_EXPERTS)))
    rb_pad = jnp.pad(router_b[layer], (0, LANES - N_EXPERTS)).reshape(1, LANES)
    h2, idx, gates, rank, counts = _router(x2, gain, mod4, 3, 4, rw_pad, rb_pad, seq)

    bm = MOE_ROWS
    counts = counts[:, 0].astype(jnp.int32)
    padded = (counts + bm - 1) // bm * bm
    p_ends = jnp.cumsum(padded)
    p_starts = p_ends - padded
    n_rows = t * TOP_K + N_EXPERTS * bm
    n_blk = n_rows // bm
    onehot = idx[:TOP_K, :, None] == jnp.arange(N_EXPERTS, dtype=jnp.int32)[None, None, :]
    dest = jnp.sum(jnp.where(onehot, p_starts[None, None, :], 0), axis=-1) + rank[:TOP_K]
    dest_flat = dest.reshape(TOP_K * t)
    blk_start = jnp.arange(n_blk, dtype=jnp.int32) * bm
    blk_expert = jnp.sum((blk_start[:, None] >= p_ends[None, :]).astype(jnp.int32), axis=-1)
    blk_expert = jnp.minimum(blk_expert, N_EXPERTS - 1).astype(jnp.int32)
    n_used = (p_ends[-1:] // bm).astype(jnp.int32)

    row_token = _sc_row_tokens(dest_flat, n_rows, t)
    f2 = up_w.shape[-1]
    up_b4 = up_b.reshape(up_b.shape[0], N_EXPERTS, 1, f2)
    down_b4 = down_b.reshape(down_b.shape[0], N_EXPERTS, 1, d)
    part_blk = n_blk // MOE_PARTS
    xb_parts = [_sc_gather_rows(h2, row_token[p * part_blk * bm:(p + 1) * part_blk * bm]) for p in range(MOE_PARTS)]
    y_rows = None
    for p, xb in enumerate(xb_parts):
        y_rows = _expert_ffn(xb, blk_expert[p * part_blk:(p + 1) * part_blk],
                             jnp.clip(n_used - p * part_blk, 0, part_blk), up_w, up_b4, down_w, down_b4,
                             layer, n_rows, p * part_blk, y_rows)
    y_kt = _sc_gather_rows(y_rows, dest_flat).reshape(TOP_K, t, d // 2)
    return _combine(y_kt, gates.T, x2, mod4, 5, seq)


QKV_TM = 256


def _rope_tables(pos_row, invf_col):
    tm = pos_row.shape[1]
    half = ROPE_DIM // 2
    ang = invf_col * pos_row
    expand = (lax.broadcasted_iota(jnp.int32, (half, LANES), 1) % half
              == lax.broadcasted_iota(jnp.int32, (half, LANES), 0)).astype(BF16)

    def to_lanes(x):
        hi = x.astype(BF16)
        rest = x - hi.astype(F32)
        mid = rest.astype(BF16)
        parts = (hi, mid, (rest - mid.astype(F32)).astype(BF16))
        return sum(lax.dot_general(p, expand, (((0,), (0,)), ((), ())), preferred_element_type=F32) for p in parts)

    c = to_lanes(jnp.cos(ang))
    s = to_lanes(jnp.sin(ang))
    d = lax.broadcasted_iota(jnp.int32, (tm, LANES), 1) % B_HEAD_DIM
    cos_t = jnp.where(d < ROPE_DIM, c, 1.0)
    sin_lo = jnp.where(d < half, -s, 0.0)
    sin_hi = jnp.where(jnp.logical_and(d >= half, d < ROPE_DIM), s, 0.0)
    return cos_t, sin_lo, sin_hi


def _head_norm_rope(x, gain_row, tables):
    cos_t, sin_lo, sin_hi = tables
    same_head = (lax.broadcasted_iota(jnp.int32, (LANES, LANES), 0) // B_HEAD_DIM
                 == lax.broadcasted_iota(jnp.int32, (LANES, LANES), 1) // B_HEAD_DIM).astype(BF16)
    sq_hi, sq_lo = _split_bf16(x * x)
    ss = (jnp.dot(sq_hi, same_head, preferred_element_type=F32)
          + jnp.dot(sq_lo, same_head, preferred_element_type=F32))
    xn = x * lax.rsqrt(ss * (1.0 / B_HEAD_DIM) + EPS) * gain_row
    half = ROPE_DIM // 2
    return (xn * cos_t + pltpu.roll(xn, LANES - half, 1) * sin_lo + pltpu.roll(xn, half, 1) * sin_hi)


def _qkv_body(x_ref, pos_ref, gq_ref, shq_ref, scq_ref, gkv_ref, shkv_ref, sckv_ref, wq_ref, wkv_ref,
              qg_ref, kg_ref, invf_ref, q_ref, k_ref, v_ref):
    x = x_ref[...]
    y = x * lax.rsqrt(jnp.mean(x * x, axis=-1, keepdims=True) + EPS)
    hq = ((y * gq_ref[...]) * (1.0 + scq_ref[...]) + shq_ref[...]).astype(BF16)
    hkv = ((y * gkv_ref[...]) * (1.0 + sckv_ref[...]) + shkv_ref[...]).astype(BF16)
    tables = _rope_tables(pos_ref[...].astype(F32), invf_ref[...])
    kv = jnp.dot(hkv, wkv_ref[...], preferred_element_type=F32)
    k_ref[...] = _head_norm_rope(kv[:, :LANES], kg_ref[...], tables).astype(BF16)
    v_ref[...] = kv[:, LANES:].astype(BF16)
    q = jnp.dot(hq, wq_ref[...], preferred_element_type=F32)
    scale = B_HEAD_DIM ** -0.5
    for p in range(q.shape[-1] // LANES):
        qp = _head_norm_rope(q[:, p * LANES:(p + 1) * LANES], qg_ref[...], tables)
        q_ref[:, p * LANES:(p + 1) * LANES] = (qp * scale).astype(BF16)


def _qkv(x2, pos2, gq, mod4, gkv, kvmod4, wq_bf, wkv_bf, qg2, kg2, invf, seq):
    t, d = x2.shape
    tm = QKV_TM
    tps = seq // tm
    nq = wq_bf.shape[-1]
    nkv = wkv_bf.shape[-1]
    row = lambda n: pl.BlockSpec((1, n), lambda i: (0, 0))
    modspec = lambda slot: pl.BlockSpec((None, None, 1, d), lambda i, slot=slot: (i // tps, slot, 0, 0))
    return pl.pallas_call(
        _qkv_body,
        out_shape=(jax.ShapeDtypeStruct((t, nq), BF16), jax.ShapeDtypeStruct((t, LANES), BF16),
                   jax.ShapeDtypeStruct((t, LANES), BF16)),
        grid=(t // tm,),
        in_specs=[pl.BlockSpec((tm, d), lambda i: (i, 0)),
                  pl.BlockSpec((None, 1, tm), lambda i: (i, 0, 0)),
                  row(d), modspec(0), modspec(1),
                  row(d), modspec(0), modspec(1),
                  pl.BlockSpec((d, nq), lambda i: (0, 0)),
                  pl.BlockSpec((d, nkv), lambda i: (0, 0)),
                  row(LANES), row(LANES), pl.BlockSpec(invf.shape, lambda i: (0, 0))],
        out_specs=(pl.BlockSpec((tm, nq), lambda i: (i, 0)),
                   pl.BlockSpec((tm, LANES), lambda i: (i, 0)),
                   pl.BlockSpec((tm, LANES), lambda i: (i, 0))),
        compiler_params=_cparams(("parallel",)),
        name="swa_qkv_proj",
    )(x2, pos2.reshape(t // tm, 1, tm), gq, mod4, mod4, gkv, kvmod4, kvmod4, wq_bf, wkv_bf, qg2, kg2, invf)


def _attn_body(sink_ref, q_ref, kc_ref, kp_ref, vc_ref, vp_ref, w_ref, x_ref, g_ref, o_ref):
    n = pl.program_id(1)
    w = WINDOW
    qi = lax.broadcasted_iota(jnp.int32, (w, 2 * w), 0) + w
    kj = lax.broadcasted_iota(jnp.int32, (w, 2 * w), 1)
    band = jnp.logical_and(kj <= qi, qi - kj < w)
    mask = jnp.logical_and(band, jnp.logical_or(n > 0, kj >= w))
    hd = B_HEAD_DIM
    lane = lax.broadcasted_iota(jnp.int32, (2 * w, LANES), 1)
    first = lane < hd
    kfull = jnp.concatenate([kp_ref[...], kc_ref[...]], axis=0).astype(F32)
    vfull = jnp.concatenate([vp_ref[...], vc_ref[...]], axis=0).astype(F32)
    placed = {}
    for g in range(B_KV_HEADS):
        for name, full in (("k", kfull), ("v", vfull)):
            swapped = pltpu.roll(full, hd, 1)
            own_first = full if g == 0 else swapped
            own_second = swapped if g == 0 else full
            placed[name, g, 0] = jnp.where(first, own_first, 0.0).astype(BF16)
            placed[name, g, 1] = jnp.where(first, 0.0, own_second).astype(BF16)
    pairs = B_Q_HEADS // 2
    heads = [(p, side) for p in range(pairs) for side in range(2)]
    scores = []
    for p, side in heads:
        g = (2 * p) // B_GROUP
        q_pair = q_ref[:, p * LANES:(p + 1) * LANES]
        s = lax.dot_general(q_pair, placed["k", g, side], (((1,), (1,)), ((), ())), preferred_element_type=F32)
        scores.append(jnp.where(mask, s, NEG_INF))
    probs, denoms = [], []
    for (p, side), s in zip(heads, scores):
        sink = sink_ref[0, 2 * p + side]
        m = jnp.maximum(jnp.max(jnp.maximum(s[:, :w], s[:, w:]), axis=-1, keepdims=True), sink)
        e = jnp.exp(s - m)
        denoms.append(jnp.sum(e[:, :w] + e[:, w:], axis=-1, keepdims=True) + jnp.exp(sink - m))
        probs.append(e.astype(BF16))
    outs = []
    for (p, side), pr, den in zip(heads, probs, denoms):
        g = (2 * p) // B_GROUP
        outs.append(jnp.dot(pr, placed["v", g, side], preferred_element_type=F32) / den)
    attn = jnp.concatenate([(outs[2 * p] + outs[2 * p + 1]).astype(BF16) for p in range(pairs)], axis=1)
    o_ref[...] = x_ref[...] + g_ref[...] * jnp.dot(attn, w_ref[...], preferred_element_type=F32)


def _attention_residual(q, k, v, sinks2, w_out_bf, x2, mod4, gate_slot, bsz, seq):
    t, nq = q.shape
    d = x2.shape[-1]
    w = WINDOW
    nb = seq // w
    cur = lambda b, n: (b * nb + n, 0)
    prev = lambda b, n: (b * nb + jnp.maximum(n - 1, 0), 0)
    return pl.pallas_call(
        _attn_body,
        out_shape=jax.ShapeDtypeStruct((t, d), F32),
        grid=(bsz, nb),
        in_specs=[pl.BlockSpec(memory_space=pltpu.SMEM),
                  pl.BlockSpec((w, nq), cur),
                  pl.BlockSpec((w, LANES), cur), pl.BlockSpec((w, LANES), prev),
                  pl.BlockSpec((w, LANES), cur), pl.BlockSpec((w, LANES), prev),
                  pl.BlockSpec((nq, d), lambda b, n: (0, 0)),
                  pl.BlockSpec((w, d), cur),
                  pl.BlockSpec((None, None, 1, d), lambda b, n: (b, gate_slot, 0, 0))],
        out_specs=pl.BlockSpec((w, d), cur),
        compiler_params=_cparams(("parallel", "parallel")),
        name="swa_sink_attention",
    )(sinks2, q, k, k, v, v, w_out_bf, x2, mod4)


def kernel(x, c, positions, ada_w, ada_b, norm_gain, a_w_in, a_conv, a_log, a_dt_bias, a_out_gain, a_w_out,
           kv_ada_w, kv_ada_b, kv_norm_gain, kv_w, k_norm_gain, b_w_q, q_norm_gain, b_sinks, b_w_out,
           router_w, router_b, up_w, up_b, down_w, down_b):
    bsz, seq, d = x.shape
    t = bsz * seq
    depth = ada_w.shape[0]
    x2 = x.reshape(t, d)
    ada_b3 = ada_b.reshape(depth, 1, 6 * d)

    for layer in range(depth):
        mod4 = _modulation(c, ada_w, ada_b3, layer).reshape(bsz, 6, 1, d)
        gain1 = norm_gain[layer, 0].reshape(1, d)
        gain2 = norm_gain[layer, 1].reshape(1, d)
        if layer < N_A_LAYERS:
            w_in = jnp.pad(a_w_in[layer].astype(BF16), ((0, 0), (0, A_PROJ_PAD - a_w_in.shape[-1])))
            proj = _in_proj(x2, gain1, mod4, w_in, a_conv[layer], seq)
            o = _gdn(proj, a_log[layer].reshape(1, A_HEADS),
                     a_dt_bias[layer].reshape(1, A_HEADS), a_out_gain[layer].reshape(1, A_HEAD_DIM),
                     bsz, seq)
            x2 = _out_proj_residual(o, a_w_out[layer].astype(BF16), x2, mod4, 2, seq)
        else:
            j = layer - N_A_LAYERS
            kvmod4 = _modulation(c, kv_ada_w.reshape(1, d, 2 * d), kv_ada_b.reshape(1, 1, 2 * d), 0)
            kvmod4 = kvmod4.reshape(bsz, 2, 1, d)
            inv_freq = ROPE_THETA ** (-np.arange(0, ROPE_DIM, 2, dtype=np.float32) / ROPE_DIM)
            invf = jnp.asarray(inv_freq.astype(np.float32).reshape(ROPE_DIM // 2, 1))
            q, k, v = _qkv(x2, positions.reshape(t), gain1, mod4, kv_norm_gain.reshape(1, d), kvmod4,
                           b_w_q[j].astype(BF16), kv_w.astype(BF16),
                           jnp.tile(q_norm_gain[j], 2).reshape(1, LANES),
                           jnp.tile(k_norm_gain, 2).reshape(1, LANES), invf, seq)
            x2 = _attention_residual(q, k, v, b_sinks[j].reshape(1, B_Q_HEADS), b_w_out[j].astype(BF16),
                                     x2, mod4, 2, bsz, seq)
        x2 = _moe_layer(x2, gain2, mod4, router_w, router_b, up_w, up_b, down_w, down_b, layer, seq)
    return x2.reshape(bsz, seq, d)
```

```python
import dataclasses
import functools

import jax
import jax.numpy as jnp
import numpy as np
from jax import lax
from jax.experimental import pallas as pl
from jax.experimental.pallas import tpu as pltpu
from jax.experimental.pallas import tpu_sc as plsc

F32 = jnp.float32
BF16 = jnp.bfloat16
HIGHEST = lax.Precision.HIGHEST

D_MODEL = 1024
N_A_LAYERS = 1

A_HEAD_DIM = 128
A_HEADS = 8
A_WIDTH = 1024
A_CONV = 4
A_CHUNK = 64
A_PROJ_PAD = 4608

B_HEAD_DIM = 64
B_Q_HEADS = 16
B_KV_HEADS = 2
B_GROUP = 8
WINDOW = 128
ROPE_DIM = 16
ROPE_THETA = 500000.0

N_EXPERTS = 32
TOP_K = 4
D_FF = 1024
SWIGLU_LIMIT = 7.0
SWIGLU_ALPHA = 1.702
MOE_ROWS = 512
MOE_PARTS = 4

EPS = 1e-6
LANES = 128
NEG_INF = float("-inf")

VMEM_LIMIT = 56 * 1024 * 1024


def _cparams(sem, vmem=VMEM_LIMIT):
    return pltpu.CompilerParams(dimension_semantics=sem, vmem_limit_bytes=vmem)


def _silu(x):
    return x * jax.nn.sigmoid(x)


HI_HALF = -65536


def _pack_bf16_pairs(x):
    n = x.shape[-1] // 2
    bits = lax.bitcast_convert_type(x.astype(BF16).astype(F32), jnp.int32)
    return jnp.bitwise_or(lax.shift_right_logical(bits[:, :n], 16), jnp.bitwise_and(bits[:, n:], HI_HALF))


def _unpack_bf16_pairs(w):
    lo = lax.bitcast_convert_type(lax.shift_left(w, 16), F32)
    hi = lax.bitcast_convert_type(jnp.bitwise_and(w, HI_HALF), F32)
    return lo, hi


def _ada_norm(x, gain, shift, scale):
    y = x * lax.rsqrt(jnp.mean(x * x, axis=-1, keepdims=True) + EPS)
    return (y * gain) * (1.0 + scale) + shift


def _mod_body(c_ref, w_ref, b_ref, o_ref):
    o_ref[...] = jnp.dot(_silu(c_ref[...]), w_ref[...], preferred_element_type=F32,
                         precision=HIGHEST) + b_ref[...]


def _modulation(c, w3, b3, layer):
    bsz, d = c.shape
    n = w3.shape[-1]
    tn = 1024
    return pl.pallas_call(
        _mod_body,
        out_shape=jax.ShapeDtypeStruct((bsz, n), F32),
        grid=(n // tn,),
        in_specs=[pl.BlockSpec((bsz, d), lambda j: (0, 0)),
                  pl.BlockSpec((None, d, tn), lambda j: (layer, 0, j)),
                  pl.BlockSpec((None, 1, tn), lambda j: (layer, 0, j))],
        out_specs=pl.BlockSpec((bsz, tn), lambda j: (0, j)),
        compiler_params=_cparams(("arbitrary",)),
        name="adaln_mod",
    )(c, w3, b3)


INPROJ_TM = 256
INPROJ_TN = 256
HALO = 8


def _inproj_body(x_ref, gain_ref, sh_ref, sc_ref, w_ref, conv_ref, o_ref, ext, *, tiles_per_seq):
    i = pl.program_id(0)
    tm = x_ref.shape[0]
    hd = A_HEAD_DIM
    n = o_ref.shape[-1]
    tn = INPROJ_TN
    h = _ada_norm(x_ref[...], gain_ref[...], sh_ref[...], sc_ref[...]).astype(BF16)

    @pl.when(i % tiles_per_seq == 0)
    def _():
        ext[0:HALO, :] = jnp.zeros((HALO, ext.shape[-1]), F32)

    def conv_silu_norm(raw, n0):
        cols = slice(n0, n0 + tn)
        ext[HALO:HALO + tm, cols] = raw
        acc = raw * conv_ref[A_CONV - 1:A_CONV, cols]
        for s in range(1, A_CONV):
            acc = acc + ext[HALO - s:HALO - s + tm, cols] * conv_ref[A_CONV - 1 - s:A_CONV - s, cols]
        ext[0:HALO, cols] = raw[tm - HALO:, :]
        y = _silu(acc)
        if n0 >= 2 * A_WIDTH:
            return y
        scale = hd ** -0.5 if n0 < A_WIDTH else 1.0
        heads = []
        for c0 in range(0, tn, hd):
            yh = y[:, c0:c0 + hd]
            heads.append(yh * (lax.rsqrt(jnp.sum(yh * yh, axis=-1, keepdims=True) + EPS) * scale))
        return jnp.concatenate(heads, axis=1)

    matmul = lambda n0: jnp.dot(h, w_ref[:, n0:n0 + tn], preferred_element_type=F32)
    conv_chunks = list(range(0, 3 * A_WIDTH, tn))
    plain_chunks = list(range(3 * A_WIDTH, n, tn))
    every = max(1, len(conv_chunks) // max(1, len(plain_chunks)))
    raw = matmul(conv_chunks[0])
    for idx, n0 in enumerate(conv_chunks):
        nxt = matmul(conv_chunks[idx + 1]) if idx + 1 < len(conv_chunks) else None
        if idx % every == every - 1 and plain_chunks:
            p0 = plain_chunks.pop(0)
            o_ref[:, p0:p0 + tn] = matmul(p0)
        o_ref[:, n0:n0 + tn] = conv_silu_norm(raw, n0)
        raw = nxt
    for p0 in plain_chunks:
        o_ref[:, p0:p0 + tn] = matmul(p0)


def _in_proj(x2, gain, mod4, w_bf, conv_w, seq):
    t, d = x2.shape
    n = w_bf.shape[-1]
    tm = INPROJ_TM
    tps = seq // tm
    return pl.pallas_call(
        functools.partial(_inproj_body, tiles_per_seq=tps),
        out_shape=jax.ShapeDtypeStruct((t, n), F32),
        grid=(t // tm,),
        in_specs=[pl.BlockSpec((tm, d), lambda i: (i, 0)),
                  pl.BlockSpec((1, d), lambda i: (0, 0)),
                  pl.BlockSpec((None, None, 1, d), lambda i: (i // tps, 0, 0, 0)),
                  pl.BlockSpec((None, None, 1, d), lambda i: (i // tps, 1, 0, 0)),
                  pl.BlockSpec((d, n), lambda i: (0, 0)),
                  pl.BlockSpec(conv_w.shape, lambda i: (0, 0))],
        out_specs=pl.BlockSpec((tm, n), lambda i: (i, 0)),
        scratch_shapes=[pltpu.VMEM((HALO + tm, conv_w.shape[-1]), F32)],
        compiler_params=_cparams(("arbitrary",)),
        name="gdn_in_proj",
    )(x2, gain, mod4, mod4, w_bf, conv_w)


GDN_HG = 2
GDN_RB = 256
GDN_SC = 128
GDN_SUB = 4


def _split_bf16(x):
    hi = x.astype(BF16)
    return hi, (x - hi.astype(F32)).astype(BF16)


def _mm3(a_hi, a_lo, b_hi, b_lo, dims=(((1,), (0,)), ((), ()))):
    dg = functools.partial(lax.dot_general, dimension_numbers=dims, preferred_element_type=F32)
    return dg(a_hi, b_hi) + (dg(a_lo, b_hi) + dg(a_hi, b_lo))


def _dot3(a, b, dims):
    return _mm3(*_split_bf16(a), *_split_bf16(b), dims)


def _gdn_body(alog_ref, dtb_ref, q_ref, k_ref, v_ref, z_ref, ab_ref, og_ref,
              o_ref, bs, gcs, mp, bc, qp, op):
    seq = q_ref.shape[0]
    hd = A_HEAD_DIM
    ch = A_CHUNK
    g_idx = pl.program_id(1)

    lane = lax.broadcasted_iota(jnp.int32, (GDN_RB, hd), 1)
    row = lax.broadcasted_iota(jnp.int32, (GDN_RB, hd), 0)
    row_in_chunk = row % ch

    sc_rows = GDN_SC
    per_sc = sc_rows // ch
    rb_per_step = sc_rows * GDN_SUB // GDN_RB
    head_cols = lambda hh: slice(hh * hd, (hh + 1) * hd)

    def prep_stages(j):
        for rbi in range(rb_per_step):
            r = j * rb_per_step + rbi
            start = pl.multiple_of(r * GDN_RB, GDN_RB)
            rows = pl.ds(start, GDN_RB)
            for hh in range(GDN_HG):
                head = g_idx * GDN_HG + hh
                ab = ab_ref[rows, :]
                a_col = jnp.sum(jnp.where(lane == head, ab, 0.0), axis=-1, keepdims=True)
                b_col = jnp.sum(jnp.where(lane == head + A_HEADS, ab, 0.0), axis=-1, keepdims=True)
                bs[hh, rows, :] = jnp.broadcast_to(jax.nn.sigmoid(b_col), (GDN_RB, hd))
                xa = a_col + dtb_ref[0, head]
                softplus = jnp.maximum(xa, 0.0) + jnp.log1p(jnp.exp(-jnp.abs(xa)))
                g = jnp.broadcast_to(-jnp.exp(alog_ref[0, head]) * softplus, (GDN_RB, hd))
                s = 1
                while s < ch:
                    g = g + jnp.where(row_in_chunk >= s, pltpu.roll(g, s, 0), 0.0)
                    s *= 2
                gcs[hh, rows, :] = g
                yield

    ri = lax.broadcasted_iota(jnp.int32, (sc_rows, sc_rows), 0)
    cj = lax.broadcasted_iota(jnp.int32, (sc_rows, sc_rows), 1)
    same_chunk = (ri // ch) == (cj // ch)
    incl = jnp.logical_and(same_chunk, ri >= cj)
    strict = jnp.logical_and(same_chunk, ri > cj)
    eye = jnp.where(ri == cj, 1.0, 0.0)
    chunk_of_col = lax.broadcasted_iota(jnp.int32, (hd, sc_rows), 1) // ch
    lanes_nt = (((1,), (1,)), ((), ()))
    plain = (((1,), (0,)), ((), ()))

    chunks_per_step = GDN_SUB * per_sc
    n_steps = seq // (sc_rows * GDN_SUB)

    def local_stages(i):
        chains = [(i * GDN_SUB + sub, hh) for sub in range(GDN_SUB) for hh in range(GDN_HG)]
        rows_of = lambda blk: pl.ds(pl.multiple_of(blk * sc_rows, sc_rows), sc_rows)

        def start(blk, hh):
            rows = rows_of(blk)
            k = k_ref[rows, head_cols(hh)]
            gc = gcs[hh, rows, :]
            decay = jnp.where(incl, jnp.exp(jnp.minimum(gc - gc.T[0:1, :], 0.0)), 0.0)
            k_bf = k.astype(BF16)
            kk = lax.dot_general((k * bs[hh, rows, :]).astype(BF16), k_bf, lanes_nt, preferred_element_type=F32)
            qk = lax.dot_general(q_ref[rows, head_cols(hh)].astype(BF16), k_bf, lanes_nt, preferred_element_type=F32)
            return jnp.where(strict, -(kk * decay), 0.0), (qk * decay).astype(BF16)

        started = [start(*c) for c in chains]
        yield
        qks = [s[1] for s in started]
        invs = [eye + s[0] for s in started]
        pws = [_dot3(s[0], s[0], plain) for s in started]
        yield
        for step in range(1, 6):
            for n in range(len(chains)):
                pw_hi, pw_lo = _split_bf16(pws[n])
                inv_hi, inv_lo = _split_bf16(invs[n])
                if step < 5:
                    prod = _mm3(jnp.concatenate([pw_hi, inv_hi], axis=0),
                                jnp.concatenate([pw_lo, inv_lo], axis=0), pw_hi, pw_lo)
                    pws[n] = prod[:sc_rows]
                    invs[n] = invs[n] + prod[sc_rows:]
                else:
                    invs[n] = invs[n] + _mm3(inv_hi, inv_lo, pw_hi, pw_lo)
            yield

        def solve(n, blk, hh):
            rows = rows_of(blk)
            beta = bs[hh, rows, :]
            rhs = jnp.concatenate([v_ref[rows, head_cols(hh)] * beta, k_ref[rows, head_cols(hh)] * beta * jnp.exp(gcs[hh, rows, :])], axis=1)
            return _dot3(invs[n], rhs, plain).astype(BF16)

        uws = [solve(n, *c) for n, c in enumerate(chains)]
        yield

        def finish(n, blk, hh):
            rows = rows_of(blk)
            k = k_ref[rows, head_cols(hh)]
            gc = gcs[hh, rows, :]
            res = jnp.dot(qks[n], uws[n], preferred_element_type=F32)
            op[hh, rows, :] = res[:, :hd]
            qp[hh, rows, :] = (q_ref[rows, head_cols(hh)] * jnp.exp(gc) - res[:, hd:]).astype(BF16)
            gl = jnp.concatenate(
                [jnp.broadcast_to(gc[(j + 1) * ch - 1:(j + 1) * ch, :], (ch, hd)) for j in range(per_sc)], axis=0)
            kt_t = (k * jnp.exp(gl - gc)).T
            for j in range(per_sc):
                kt_j = jnp.where(chunk_of_col == j, kt_t, 0.0).astype(BF16)
                bm = jnp.dot(kt_j, uws[n], preferred_element_type=F32)
                bc[hh, blk * per_sc + j] = bm[:, :hd]
                mp[hh, blk * per_sc + j] = bm[:, hd:].astype(BF16)

        for n, c in enumerate(chains):
            finish(n, *c)

    def scan_stages(j, states):
        for cc in range(chunks_per_step):
            c = j * chunks_per_step + cc
            r0 = pl.multiple_of(c * ch, ch)
            rows = pl.ds(r0, ch)
            for hh in range(GDN_HG):
                col0 = hh * hd
                state = states[hh]
                s_bf = state.astype(BF16)
                o = jnp.dot(qp[hh, rows, :], s_bf, preferred_element_type=F32) + op[hh, rows, :]
                g_tot = jnp.exp(gcs[hh, pl.ds(r0 + ch - 1, 1), :])
                states[hh] = state * g_tot - jnp.dot(mp[hh, c], s_bf, preferred_element_type=F32) + bc[hh, c]
                on = o * lax.rsqrt(jnp.mean(o * o, axis=-1, keepdims=True) + EPS) * og_ref[...]
                z = z_ref[rows, col0:col0 + hd]
                o_ref[rows, col0:col0 + hd] = (on * _silu(z)).astype(BF16)
            yield

    def interleave(*gens):
        live = list(gens)
        while live:
            for g in list(live):
                try:
                    next(g)
                except StopIteration:
                    live.remove(g)

    assert n_steps >= 2
    interleave(prep_stages(0))
    interleave(local_stages(0), prep_stages(1))

    def pipelined(i, states):
        states = list(states)
        interleave(local_stages(i), scan_stages(i - 1, states), prep_stages(i + 1))
        return tuple(states)

    states = list(lax.fori_loop(1, n_steps - 1, pipelined,
                                tuple(jnp.zeros((hd, hd), F32) for _ in range(GDN_HG))))
    interleave(local_stages(n_steps - 1), scan_stages(n_steps - 2, states))
    interleave(scan_stages(n_steps - 1, states))


def _gdn(proj, a_log, dt_bias, out_gain, bsz, seq):
    t = proj.shape[0]
    hd = A_HEAD_DIM
    wb = hd * GDN_HG
    ng = A_HEADS // GDN_HG
    per = A_WIDTH // wb
    n_chunks = seq // A_CHUNK
    smem = pl.BlockSpec(memory_space=pltpu.SMEM)
    seq_spec = lambda off: pl.BlockSpec((seq, wb), lambda b, g, off=off: (b, off + g))
    sc = lambda dt=F32: pltpu.VMEM((GDN_HG, seq, hd), dt)
    return pl.pallas_call(
        _gdn_body,
        out_shape=jax.ShapeDtypeStruct((t, A_WIDTH), BF16),
        grid=(bsz, ng),
        in_specs=[smem, smem,
                  seq_spec(0), seq_spec(per), seq_spec(2 * per), seq_spec(3 * per),
                  pl.BlockSpec((seq, LANES), lambda b, g: (b, 4 * A_WIDTH // LANES)),
                  pl.BlockSpec((1, hd), lambda b, g: (0, 0))],
        out_specs=pl.BlockSpec((seq, wb), lambda b, g: (b, g)),
        scratch_shapes=[sc(), sc(),
                        pltpu.VMEM((GDN_HG, n_chunks, hd, hd), BF16),
                        pltpu.VMEM((GDN_HG, n_chunks, hd, hd), F32),
                        sc(BF16), sc()],
        compiler_params=_cparams(("parallel", "parallel")),
        name="gdn_core",
    )(a_log, dt_bias, proj, proj, proj, proj, proj, out_gain)


def _oproj_body(a_ref, w_ref, x_ref, g_ref, o_ref):
    y = jnp.dot(a_ref[...], w_ref[...], preferred_element_type=F32)
    o_ref[...] = x_ref[...] + g_ref[...] * y


def _out_proj_residual(a_bf, w_bf, x2, mod4, gate_slot, seq):
    t, d = x2.shape
    kdim = a_bf.shape[-1]
    tm = 512
    tps = seq // tm
    return pl.pallas_call(
        _oproj_body,
        out_shape=jax.ShapeDtypeStruct((t, d), F32),
        grid=(t // tm,),
        in_specs=[pl.BlockSpec((tm, kdim), lambda i: (i, 0)),
                  pl.BlockSpec((kdim, d), lambda i: (0, 0)),
                  pl.BlockSpec((tm, d), lambda i: (i, 0)),
                  pl.BlockSpec((None, None, 1, d), lambda i: (i // tps, gate_slot, 0, 0))],
        out_specs=pl.BlockSpec((tm, d), lambda i: (i, 0)),
        compiler_params=_cparams(("parallel",)),
        name="out_proj_residual",
    )(a_bf, w_bf, x2, mod4)


ROUTER_TM = 512


K_ROWS = 8


def _router_body(x_ref, gain_ref, sh_ref, sc_ref, rw_ref, rb_ref, h_ref, idx_ref, gate_ref, rank_ref,
                 cnt_ref, carry):
    i = pl.program_id(0)

    @pl.when(i == 0)
    def _():
        carry[...] = jnp.zeros_like(carry)

    tm = x_ref.shape[0]
    h = _ada_norm(x_ref[...], gain_ref[...], sh_ref[...], sc_ref[...])
    h_ref[...] = _pack_bf16_pairs(h)
    logits = _dot3(h, rw_ref[...], (((1,), (0,)), ((), ()))) + rb_ref[...]
    work = logits.T[:N_EXPERTS, :]
    expert = lax.broadcasted_iota(jnp.int32, (N_EXPERTS, tm), 0)
    tops, sels = [], []
    for _ in range(TOP_K):
        m = jnp.max(work, axis=0, keepdims=True)
        sel = jnp.min(jnp.where(work == m, expert, N_EXPERTS), axis=0, keepdims=True)
        work = jnp.where(expert == sel, NEG_INF, work)
        tops.append(m)
        sels.append(sel)
    exps = [jnp.exp(m - tops[0]) for m in tops]
    denom = exps[0] + exps[1] + exps[2] + exps[3]
    onehot = jnp.zeros((N_EXPERTS, tm), F32)
    for sel in sels:
        onehot = onehot + jnp.where(expert == sel, 1.0, 0.0)
    ti = lax.broadcasted_iota(jnp.int32, (tm, tm), 0)
    tj = lax.broadcasted_iota(jnp.int32, (tm, tm), 1)
    earlier = (ti < tj).astype(BF16)
    before = jnp.dot(onehot.astype(BF16), earlier, preferred_element_type=F32) + carry[:, 0:1]
    ranks = [jnp.sum(jnp.where(expert == sel, before, 0.0), axis=0, keepdims=True) for sel in sels]
    pad_i = jnp.zeros((K_ROWS - TOP_K, tm), jnp.int32)
    idx_ref[...] = jnp.concatenate(sels + [pad_i], axis=0)
    gate_ref[...] = jnp.concatenate([e / denom for e in exps] + [pad_i.astype(F32)], axis=0)
    rank_ref[...] = jnp.concatenate([r.astype(jnp.int32) for r in ranks] + [pad_i], axis=0)
    carry[...] = carry[...] + jnp.sum(onehot, axis=1, keepdims=True)
    cnt_ref[...] = carry[...]


def _router(x2, gain, mod4, sh_slot, sc_slot, rw_pad, rb_pad, seq):
    t, d = x2.shape
    tm = ROUTER_TM
    tps = seq // tm
    tok = lambda dt: jax.ShapeDtypeStruct((K_ROWS, t), dt)
    tok_spec = pl.BlockSpec((K_ROWS, tm), lambda i: (0, i))
    cnt_spec = pl.BlockSpec((N_EXPERTS, LANES), lambda i: (0, 0))
    return pl.pallas_call(
        _router_body,
        out_shape=(jax.ShapeDtypeStruct((t, d // 2), jnp.int32), tok(jnp.int32), tok(F32), tok(jnp.int32),
                   jax.ShapeDtypeStruct((N_EXPERTS, LANES), F32)),
        grid=(t // tm,),
        in_specs=[pl.BlockSpec((tm, d), lambda i: (i, 0)),
                  pl.BlockSpec((1, d), lambda i: (0, 0)),
                  pl.BlockSpec((None, None, 1, d), lambda i: (i // tps, sh_slot, 0, 0)),
                  pl.BlockSpec((None, None, 1, d), lambda i: (i // tps, sc_slot, 0, 0)),
                  pl.BlockSpec((d, LANES), lambda i: (0, 0)),
                  pl.BlockSpec((1, LANES), lambda i: (0, 0))],
        out_specs=(pl.BlockSpec((tm, d // 2), lambda i: (i, 0)), tok_spec, tok_spec, tok_spec, cnt_spec),
        scratch_shapes=[pltpu.VMEM((N_EXPERTS, LANES), F32)],
        compiler_params=_cparams(("arbitrary",)),
        name="moe_router",
    )(x2, gain, mod4, mod4, rw_pad, rb_pad)


SC_LANES = 16
SC_INDEX_CHUNK = 8192


def _sc_row_tokens(dest_flat, n_rows, n_tok):
    n_assign = dest_flat.shape[0]
    assert n_rows % SC_LANES == 0 and n_assign % SC_INDEX_CHUNK == 0
    n_cores = plsc.get_sparse_core_info().num_cores
    mesh = plsc.VectorSubcoreMesh(core_axis_name="c", subcore_axis_name="s")

    @functools.partial(
        pl.kernel, mesh=mesh,
        out_type=jax.ShapeDtypeStruct((n_rows,), jnp.int32),
        scratch_types=[pltpu.VMEM((n_rows,), jnp.int32), pltpu.VMEM((SC_INDEX_CHUNK,), jnp.int32)],
        compiler_params=dataclasses.replace(pltpu.CompilerParams(), needs_layout_passes=False),
    )
    def row_token_kernel(dest_hbm, out_hbm, rt_v, d_v):
        wid = lax.axis_index("s") * n_cores + lax.axis_index("c")

        @pl.when(wid == 0)
        def _():
            lanes = lax.iota(jnp.int32, SC_LANES)

            @pl.loop(0, n_rows // SC_LANES)
            def _(i):
                rt_v[pl.ds(i * SC_LANES, SC_LANES)] = lax.rem(lanes + i * SC_LANES, n_tok)

            @pl.loop(0, n_assign // SC_INDEX_CHUNK)
            def _(c):
                pltpu.sync_copy(dest_hbm.at[pl.ds(c * SC_INDEX_CHUNK, SC_INDEX_CHUNK)], d_v)

                @pl.loop(0, SC_INDEX_CHUNK // SC_LANES)
                def _(i):
                    idx = d_v[pl.ds(i * SC_LANES, SC_LANES)]
                    tok = lax.rem(lanes + (c * SC_INDEX_CHUNK + i * SC_LANES), n_tok)
                    plsc.store_scatter(rt_v, [idx], tok)

            pltpu.sync_copy(rt_v, out_hbm)

    return row_token_kernel(dest_flat)


def _ffn_body(be_ref, nu_ref, nxt_ref, slot_ref, x_ref, uw_hbm, ub_ref, dw_hbm, db_ref, *rest, layer):
    y_ref, uw_f32, dw_f32, uw_bf, dw_bf, sems = rest[-6:]
    i = pl.program_id(0)
    first = jnp.logical_and(jnp.logical_or(i == 0, be_ref[i] != be_ref[jnp.maximum(i - 1, 0)]), i < nu_ref[0])
    active = i < nu_ref[0]
    slot = slot_ref[i]

    def weight_copies(expert, s):
        return (pltpu.make_async_copy(uw_hbm.at[layer, expert], uw_f32.at[s], sems.at[s, 0]),
                pltpu.make_async_copy(dw_hbm.at[layer, expert], dw_f32.at[s], sems.at[s, 1]))

    @pl.when(jnp.logical_and(i == 0, active))
    def _():
        for cp in weight_copies(be_ref[0], 0):
            cp.start()

    @pl.when(first)
    def _():
        for cp in weight_copies(be_ref[i], slot):
            cp.wait()

        @pl.when(nxt_ref[i] >= 0)
        def _():
            for cp in weight_copies(nxt_ref[i], 1 - slot):
                cp.start()

        rows = 128
        for r0 in range(0, uw_bf.shape[0], rows):
            uw_bf[r0:r0 + rows, :] = uw_f32[slot, r0:r0 + rows, :].astype(BF16)
        for r0 in range(0, dw_bf.shape[0], rows):
            dw_bf[r0:r0 + rows, :] = dw_f32[slot, r0:r0 + rows, :].astype(BF16)

    @pl.when(active)
    def _():
        x = jnp.concatenate(_unpack_bf16_pairs(x_ref[...]), axis=1).astype(BF16)
        gu = jnp.dot(x, uw_bf[...], preferred_element_type=F32) + ub_ref[...]
        gate = jnp.minimum(gu[:, :D_FF], SWIGLU_LIMIT)
        lin = jnp.clip(gu[:, D_FF:], -SWIGLU_LIMIT, SWIGLU_LIMIT)
        act = gate * jax.nn.sigmoid(SWIGLU_ALPHA * gate) * (lin + 1.0)
        y = jnp.dot(act.astype(BF16), dw_bf[...], preferred_element_type=F32) + db_ref[...]
        y_ref[...] = _pack_bf16_pairs(y)

    @pl.when(jnp.logical_not(active))
    def _():
        y_ref[...] = jnp.zeros_like(y_ref)


def _expert_ffn(xb, blk_expert, n_used, up_w, up_b4, down_w, down_b4, layer, n_rows_total, first_blk, y_prev):
    n_rows, dp = xb.shape
    d = 2 * dp
    bm = MOE_ROWS
    n_blk = n_rows // bm
    f2 = up_w.shape[-1]
    blk = jnp.arange(n_blk, dtype=jnp.int32)
    first = jnp.logical_and(jnp.concatenate([jnp.ones((1,), bool), blk_expert[1:] != blk_expert[:-1]]),
                            blk < n_used[0])
    slot = ((jnp.cumsum(first.astype(jnp.int32)) - 1) % 2).astype(jnp.int32)
    later_first = jnp.where(first, blk, n_blk)[::-1]
    next_first = jnp.concatenate([lax.cummin(later_first)[::-1][1:], jnp.full((1,), n_blk, jnp.int32)])
    nxt = jnp.where(jnp.logical_and(first, next_first < n_blk),
                    blk_expert[jnp.minimum(next_first, n_blk - 1)], -1).astype(jnp.int32)
    grid_spec = pltpu.PrefetchScalarGridSpec(
        num_scalar_prefetch=4,
        grid=(n_blk,),
        in_specs=[pl.BlockSpec((bm, dp), lambda i, be, nu, nx, sl: (i, 0)),
                  pl.BlockSpec(memory_space=pl.ANY),
                  pl.BlockSpec((None, None, 1, f2), lambda i, be, nu, nx, sl: (layer, be[i], 0, 0)),
                  pl.BlockSpec(memory_space=pl.ANY),
                  pl.BlockSpec((None, None, 1, d), lambda i, be, nu, nx, sl: (layer, be[i], 0, 0))]
                 + ([] if y_prev is None else [pl.BlockSpec(memory_space=pl.ANY)]),
        out_specs=pl.BlockSpec((bm, dp), lambda i, be, nu, nx, sl: (i + first_blk, 0)),
        scratch_shapes=[pltpu.VMEM((2, d, f2), F32), pltpu.VMEM((2, f2 // 2, d), F32),
                        pltpu.VMEM((d, f2), BF16), pltpu.VMEM((f2 // 2, d), BF16),
                        pltpu.SemaphoreType.DMA((2, 2))],
    )
    args = (blk_expert, n_used, nxt, slot, xb, up_w, up_b4, down_w, down_b4)
    return pl.pallas_call(
        functools.partial(_ffn_body, layer=layer),
        out_shape=jax.ShapeDtypeStruct((n_rows_total, dp), jnp.int32),
        grid_spec=grid_spec,
        input_output_aliases={} if y_prev is None else {len(args): 0},
        compiler_params=_cparams(("arbitrary",)),
        name="moe_expert_ffn",
    )(*args, *(() if y_prev is None else (y_prev,)))


SC_GATHER_ROWS = 64


def _sc_gather_rows(table, idx):
    n, d = idx.shape[0], table.shape[1]
    info = plsc.get_sparse_core_info()
    n_cores, n_sub = info.num_cores, info.num_subcores
    n_workers = n_cores * n_sub
    chunk = SC_GATHER_ROWS
    per_worker = n // n_workers
    n_chunks = per_worker // chunk
    assert n_chunks * chunk * n_workers == n and n_chunks % 2 == 0
    mesh = plsc.VectorSubcoreMesh(core_axis_name="c", subcore_axis_name="s")

    @functools.partial(
        pl.kernel, mesh=mesh,
        out_type=jax.ShapeDtypeStruct((n, d), table.dtype),
        scratch_types=[pltpu.VMEM((n_chunks, chunk), jnp.int32), pltpu.VMEM((2, chunk, d), table.dtype),
                       pltpu.SemaphoreType.DMA((2,))],
    )
    def gather_kernel(table_hbm, idx_hbm, out_hbm, idx_v, rows_v, sems):
        wid = lax.axis_index("s") * n_cores + lax.axis_index("c")
        base = wid * per_worker
        pltpu.sync_copy(idx_hbm.at[wid], idx_v)

        def gather(ci, slot):
            return pltpu.make_async_copy(table_hbm.at[idx_v.at[ci]], rows_v.at[slot], sems.at[slot])

        gather(0, 0).start()

        @pl.loop(0, n_chunks, step=2)
        def _(c0):
            for slot in range(2):
                ci = c0 + slot
                gather(ci, slot).wait()

                @pl.when(ci + 1 < n_chunks)
                def _():
                    gather(ci + 1, 1 - slot).start()

                pltpu.sync_copy(rows_v.at[slot], out_hbm.at[pl.ds(base + ci * chunk, chunk)])

    return gather_kernel(table, idx.reshape(n_workers, n_chunks, chunk))


COMBINE_TM = 256


def _combine_body(gate_ref, x_ref, g_ref, y_ref, o_ref):
    gates = gate_ref[...]
    acc_lo = acc_hi = None
    for k in range(TOP_K):
        lo, hi = _unpack_bf16_pairs(y_ref[k])
        gk = gates[:, k:k + 1]
        acc_lo = gk * lo if acc_lo is None else acc_lo + gk * lo
        acc_hi = gk * hi if acc_hi is None else acc_hi + gk * hi
    o_ref[...] = x_ref[...] + g_ref[...] * jnp.concatenate([acc_lo, acc_hi], axis=1)


def _combine(y_kt, gates, x2, mod4, gate_slot, seq):
    t, d = x2.shape
    tm = COMBINE_TM
    tps = seq // tm
    return pl.pallas_call(
        _combine_body,
        out_shape=jax.ShapeDtypeStruct((t, d), F32),
        grid=(t // tm,),
        in_specs=[pl.BlockSpec((tm, gates.shape[1]), lambda i: (i, 0)),
                  pl.BlockSpec((tm, d), lambda i: (i, 0)),
                  pl.BlockSpec((None, None, 1, d), lambda i: (i // tps, gate_slot, 0, 0)),
                  pl.BlockSpec((TOP_K, tm, d // 2), lambda i: (0, i, 0))],
        out_specs=pl.BlockSpec((tm, d), lambda i: (i, 0)),
        compiler_params=_cparams(("parallel",)),
        name="moe_combine",
    )(gates, x2, mod4, y_kt)


def _moe_layer(x2, gain, mod4, router_w, router_b, up_w, up_b, down_w, down_b, layer, seq):
    t, d = x2.shape
    rw_pad = jnp.pad(router_w[layer], ((0, 0), (0, LANES - N_EXPERTS)))
    rb_pad = jnp.pad(router_b[layer], (0, LANES - N_EXPERTS)).reshape(1, LANES)
    h2, idx, gates, rank, counts = _router(x2, gain, mod4, 3, 4, rw_pad, rb_pad, seq)

    bm = MOE_ROWS
    counts = counts[:, 0].astype(jnp.int32)
    padded = (counts + bm - 1) // bm * bm
    p_ends = jnp.cumsum(padded)
    p_starts = p_ends - padded
    n_rows = t * TOP_K + N_EXPERTS * bm
    n_blk = n_rows // bm
    onehot = idx[:TOP_K, :, None] == jnp.arange(N_EXPERTS, dtype=jnp.int32)[None, None, :]
    dest = jnp.sum(jnp.where(onehot, p_starts[None, None, :], 0), axis=-1) + rank[:TOP_K]
    dest_flat = dest.reshape(TOP_K * t)
    blk_start = jnp.arange(n_blk, dtype=jnp.int32) * bm
    blk_expert = jnp.sum((blk_start[:, None] >= p_ends[None, :]).astype(jnp.int32), axis=-1)
    blk_expert = jnp.minimum(blk_expert, N_EXPERTS - 1).astype(jnp.int32)
    n_used = (p_ends[-1:] // bm).astype(jnp.int32)

    row_token = _sc_row_tokens(dest_flat, n_rows, t)
    f2 = up_w.shape[-1]
    up_b4 = up_b.reshape(up_b.shape[0], N_EXPERTS, 1, f2)
    down_b4 = down_b.reshape(down_b.shape[0], N_EXPERTS, 1, d)
    part_blk = n_blk // MOE_PARTS
    xb_parts = [_sc_gather_rows(h2, row_token[p * part_blk * bm:(p + 1) * part_blk * bm]) for p in range(MOE_PARTS)]
    y_rows = None
    for p, xb in enumerate(xb_parts):
        y_rows = _expert_ffn(xb, blk_expert[p * part_blk:(p + 1) * part_blk],
                             jnp.clip(n_used - p * part_blk, 0, part_blk), up_w, up_b4, down_w, down_b4,
                             layer, n_rows, p * part_blk, y_rows)
    y_kt = _sc_gather_rows(y_rows, dest_flat).reshape(TOP_K, t, d // 2)
    return _combine(y_kt, gates.T, x2, mod4, 5, seq)


QKV_TM = 256


def _rope_tables(pos_row, invf_col):
    tm = pos_row.shape[1]
    half = ROPE_DIM // 2
    ang = invf_col * pos_row
    expand = (lax.broadcasted_iota(jnp.int32, (half, LANES), 1) % half
              == lax.broadcasted_iota(jnp.int32, (half, LANES), 0)).astype(BF16)

    def to_lanes(x):
        hi = x.astype(BF16)
        rest = x - hi.astype(F32)
        mid = rest.astype(BF16)
        parts = (hi, mid, (rest - mid.astype(F32)).astype(BF16))
        return sum(lax.dot_general(p, expand, (((0,), (0,)), ((), ())), preferred_element_type=F32) for p in parts)

    c = to_lanes(jnp.cos(ang))
    s = to_lanes(jnp.sin(ang))
    d = lax.broadcasted_iota(jnp.int32, (tm, LANES), 1) % B_HEAD_DIM
    cos_t = jnp.where(d < ROPE_DIM, c, 1.0)
    sin_lo = jnp.where(d < half, -s, 0.0)
    sin_hi = jnp.where(jnp.logical_and(d >= half, d < ROPE_DIM), s, 0.0)
    return cos_t, sin_lo, sin_hi


def _head_norm_rope(x, gain_row, tables):
    cos_t, sin_lo, sin_hi = tables
    same_head = (lax.broadcasted_iota(jnp.int32, (LANES, LANES), 0) // B_HEAD_DIM
                 == lax.broadcasted_iota(jnp.int32, (LANES, LANES), 1) // B_HEAD_DIM).astype(BF16)
    sq_hi, sq_lo = _split_bf16(x * x)
    ss = (jnp.dot(sq_hi, same_head, preferred_element_type=F32)
          + jnp.dot(sq_lo, same_head, preferred_element_type=F32))
    xn = x * lax.rsqrt(ss * (1.0 / B_HEAD_DIM) + EPS) * gain_row
    half = ROPE_DIM // 2
    return (xn * cos_t + pltpu.roll(xn, LANES - half, 1) * sin_lo + pltpu.roll(xn, half, 1) * sin_hi)


def _qkv_body(x_ref, pos_ref, gq_ref, shq_ref, scq_ref, gkv_ref, shkv_ref, sckv_ref, wq_ref, wkv_ref,
              qg_ref, kg_ref, invf_ref, q_ref, k_ref, v_ref):
    x = x_ref[...]
    y = x * lax.rsqrt(jnp.mean(x * x, axis=-1, keepdims=True) + EPS)
    hq = ((y * gq_ref[...]) * (1.0 + scq_ref[...]) + shq_ref[...]).astype(BF16)
    hkv = ((y * gkv_ref[...]) * (1.0 + sckv_ref[...]) + shkv_ref[...]).astype(BF16)
    tables = _rope_tables(pos_ref[...].astype(F32), invf_ref[...])
    kv = jnp.dot(hkv, wkv_ref[...], preferred_element_type=F32)
    k_ref[...] = _head_norm_rope(kv[:, :LANES], kg_ref[...], tables).astype(BF16)
    v_ref[...] = kv[:, LANES:].astype(BF16)
    q = jnp.dot(hq, wq_ref[...], preferred_element_type=F32)
    scale = B_HEAD_DIM ** -0.5
    for p in range(q.shape[-1] // LANES):
        qp = _head_norm_rope(q[:, p * LANES:(p + 1) * LANES], qg_ref[...], tables)
        q_ref[:, p * LANES:(p + 1) * LANES] = (qp * scale).astype(BF16)


def _qkv(x2, pos2, gq, mod4, gkv, kvmod4, wq_bf, wkv_bf, qg2, kg2, invf, seq):
    t, d = x2.shape
    tm = QKV_TM
    tps = seq // tm
    nq = wq_bf.shape[-1]
    nkv = wkv_bf.shape[-1]
    row = lambda n: pl.BlockSpec((1, n), lambda i: (0, 0))
    modspec = lambda slot: pl.BlockSpec((None, None, 1, d), lambda i, slot=slot: (i // tps, slot, 0, 0))
    return pl.pallas_call(
        _qkv_body,
        out_shape=(jax.ShapeDtypeStruct((t, nq), BF16), jax.ShapeDtypeStruct((t, LANES), BF16),
                   jax.ShapeDtypeStruct((t, LANES), BF16)),
        grid=(t // tm,),
        in_specs=[pl.BlockSpec((tm, d), lambda i: (i, 0)),
                  pl.BlockSpec((None, 1, tm), lambda i: (i, 0, 0)),
                  row(d), modspec(0), modspec(1),
                  row(d), modspec(0), modspec(1),
                  pl.BlockSpec((d, nq), lambda i: (0, 0)),
                  pl.BlockSpec((d, nkv), lambda i: (0, 0)),
                  row(LANES), row(LANES), pl.BlockSpec(invf.shape, lambda i: (0, 0))],
        out_specs=(pl.BlockSpec((tm, nq), lambda i: (i, 0)),
                   pl.BlockSpec((tm, LANES), lambda i: (i, 0)),
                   pl.BlockSpec((tm, LANES), lambda i: (i, 0))),
        compiler_params=_cparams(("parallel",)),
        name="swa_qkv_proj",
    )(x2, pos2.reshape(t // tm, 1, tm), gq, mod4, mod4, gkv, kvmod4, kvmod4, wq_bf, wkv_bf, qg2, kg2, invf)


def _attn_body(sink_ref, q_ref, kc_ref, kp_ref, vc_ref, vp_ref, w_ref, x_ref, g_ref, o_ref):
    n = pl.program_id(1)
    w = WINDOW
    qi = lax.broadcasted_iota(jnp.int32, (w, 2 * w), 0) + w
    kj = lax.broadcasted_iota(jnp.int32, (w, 2 * w), 1)
    band = jnp.logical_and(kj <= qi, qi - kj < w)
    mask = jnp.logical_and(band, jnp.logical_or(n > 0, kj >= w))
    hd = B_HEAD_DIM
    lane = lax.broadcasted_iota(jnp.int32, (2 * w, LANES), 1)
    first = lane < hd
    kfull = jnp.concatenate([kp_ref[...], kc_ref[...]], axis=0).astype(F32)
    vfull = jnp.concatenate([vp_ref[...], vc_ref[...]], axis=0).astype(F32)
    placed = {}
    for g in range(B_KV_HEADS):
        for name, full in (("k", kfull), ("v", vfull)):
            swapped = pltpu.roll(full, hd, 1)
            own_first = full if g == 0 else swapped
            own_second = swapped if g == 0 else full
            placed[name, g, 0] = jnp.where(first, own_first, 0.0).astype(BF16)
            placed[name, g, 1] = jnp.where(first, 0.0, own_second).astype(BF16)
    pairs = B_Q_HEADS // 2
    heads = [(p, side) for p in range(pairs) for side in range(2)]
    scores = []
    for p, side in heads:
        g = (2 * p) // B_GROUP
        q_pair = q_ref[:, p * LANES:(p + 1) * LANES]
        s = lax.dot_general(q_pair, placed["k", g, side], (((1,), (1,)), ((), ())), preferred_element_type=F32)
        scores.append(jnp.where(mask, s, NEG_INF))
    probs, denoms = [], []
    for (p, side), s in zip(heads, scores):
        sink = sink_ref[0, 2 * p + side]
        m = jnp.maximum(jnp.max(jnp.maximum(s[:, :w], s[:, w:]), axis=-1, keepdims=True), sink)
        e = jnp.exp(s - m)
        denoms.append(jnp.sum(e[:, :w] + e[:, w:], axis=-1, keepdims=True) + jnp.exp(sink - m))
        probs.append(e.astype(BF16))
    outs = []
    for (p, side), pr, den in zip(heads, probs, denoms):
        g = (2 * p) // B_GROUP
        outs.append(jnp.dot(pr, placed["v", g, side], preferred_element_type=F32) / den)
    attn = jnp.concatenate([(outs[2 * p] + outs[2 * p + 1]).astype(BF16) for p in range(pairs)], axis=1)
    o_ref[...] = x_ref[...] + g_ref[...] * jnp.dot(attn, w_ref[...], preferred_element_type=F32)


def _attention_residual(q, k, v, sinks2, w_out_bf, x2, mod4, gate_slot, bsz, seq):
    t, nq = q.shape
    d = x2.shape[-1]
    w = WINDOW
    nb = seq // w
    cur = lambda b, n: (b * nb + n, 0)
    prev = lambda b, n: (b * nb + jnp.maximum(n - 1, 0), 0)
    return pl.pallas_call(
        _attn_body,
        out_shape=jax.ShapeDtypeStruct((t, d), F32),
        grid=(bsz, nb),
        in_specs=[pl.BlockSpec(memory_space=pltpu.SMEM),
                  pl.BlockSpec((w, nq), cur),
                  pl.BlockSpec((w, LANES), cur), pl.BlockSpec((w, LANES), prev),
                  pl.BlockSpec((w, LANES), cur), pl.BlockSpec((w, LANES), prev),
                  pl.BlockSpec((nq, d), lambda b, n: (0, 0)),
                  pl.BlockSpec((w, d), cur),
                  pl.BlockSpec((None, None, 1, d), lambda b, n: (b, gate_slot, 0, 0))],
        out_specs=pl.BlockSpec((w, d), cur),
        compiler_params=_cparams(("parallel", "parallel")),
        name="swa_sink_attention",
    )(sinks2, q, k, k, v, v, w_out_bf, x2, mod4)


def kernel(x, c, positions, ada_w, ada_b, norm_gain, a_w_in, a_conv, a_log, a_dt_bias, a_out_gain, a_w_out,
           kv_ada_w, kv_ada_b, kv_norm_gain, kv_w, k_norm_gain, b_w_q, q_norm_gain, b_sinks, b_w_out,
           router_w, router_b, up_w, up_b, down_w, down_b):
    bsz, seq, d = x.shape
    t = bsz * seq
    depth = ada_w.shape[0]
    x2 = x.reshape(t, d)
    ada_b3 = ada_b.reshape(depth, 1, 6 * d)

    for layer in range(depth):
        mod4 = _modulation(c, ada_w, ada_b3, layer).reshape(bsz, 6, 1, d)
        gain1 = norm_gain[layer, 0].reshape(1, d)
        gain2 = norm_gain[layer, 1].reshape(1, d)
        if layer < N_A_LAYERS:
            w_in = jnp.pad(a_w_in[layer].astype(BF16), ((0, 0), (0, A_PROJ_PAD - a_w_in.shape[-1])))
            proj = _in_proj(x2, gain1, mod4, w_in, a_conv[layer], seq)
            o = _gdn(proj, a_log[layer].reshape(1, A_HEADS),
                     a_dt_bias[layer].reshape(1, A_HEADS), a_out_gain[layer].reshape(1, A_HEAD_DIM),
                     bsz, seq)
            x2 = _out_proj_residual(o, a_w_out[layer].astype(BF16), x2, mod4, 2, seq)
        else:
            j = layer - N_A_LAYERS
            kvmod4 = _modulation(c, kv_ada_w.reshape(1, d, 2 * d), kv_ada_b.reshape(1, 1, 2 * d), 0)
            kvmod4 = kvmod4.reshape(bsz, 2, 1, d)
            inv_freq = ROPE_THETA ** (-np.arange(0, ROPE_DIM, 2, dtype=np.float32) / ROPE_DIM)
            invf = jnp.asarray(inv_freq.astype(np.float32).reshape(ROPE_DIM // 2, 1))
            q, k, v = _qkv(x2, positions.reshape(t), gain1, mod4, kv_norm_gain.reshape(1, d), kvmod4,
                           b_w_q[j].astype(BF16), kv_w.astype(BF16),
                           jnp.tile(q_norm_gain[j], 2).reshape(1, LANES),
                           jnp.tile(k_norm_gain, 2).reshape(1, LANES), invf, seq)
            x2 = _attention_residual(q, k, v, b_sinks[j].reshape(1, B_Q_HEADS), b_w_out[j].astype(BF16),
                                     x2, mod4, 2, bsz, seq)
        x2 = _moe_layer(x2, gain2, mod4, router_w, router_b, up_w, up_b, down_w, down_b, layer, seq)
    return x2.reshape(bsz, seq, d)
```

```python
import dataclasses
import functools

import jax
import jax.numpy as jnp
import numpy as np
from jax import lax
from jax.experimental import pallas as pl
from jax.experimental.pallas import tpu as pltpu
from jax.experimental.pallas import tpu_sc as plsc

F32 = jnp.float32
BF16 = jnp.bfloat16
HIGHEST = lax.Precision.HIGHEST

N_A_LAYERS = 1

A_HEAD_DIM = 128
A_HEADS = 8
A_WIDTH = 1024
A_CONV = 4
A_CHUNK = 64
A_PROJ_PAD = 4608

B_HEAD_DIM = 64
B_Q_HEADS = 16
B_KV_HEADS = 2
B_GROUP = 8
WINDOW = 128
ROPE_DIM = 16
ROPE_THETA = 500000.0

N_EXPERTS = 32
TOP_K = 4
D_FF = 1024
SWIGLU_LIMIT = 7.0
SWIGLU_ALPHA = 1.702
MOE_ROWS = 512
MOE_PARTS = 2

EPS = 1e-6
LANES = 128
NEG_INF = float("-inf")

VMEM_LIMIT = 56 * 1024 * 1024


def _cparams(sem, vmem=VMEM_LIMIT):
    return pltpu.CompilerParams(dimension_semantics=sem, vmem_limit_bytes=vmem)


def _silu(x):
    return x * jax.nn.sigmoid(x)


HI_HALF = -65536


def _pack_bf16_pairs(x):
    n = x.shape[-1] // 2
    bits = lax.bitcast_convert_type(x.astype(BF16).astype(F32), jnp.int32)
    return jnp.bitwise_or(lax.shift_right_logical(bits[:, :n], 16), jnp.bitwise_and(bits[:, n:], HI_HALF))


def _unpack_bf16_pairs(w):
    lo = lax.bitcast_convert_type(lax.shift_left(w, 16), F32)
    hi = lax.bitcast_convert_type(jnp.bitwise_and(w, HI_HALF), F32)
    return lo, hi


def _ada_norm(x, gain, shift, scale):
    y = x * lax.rsqrt(jnp.mean(x * x, axis=-1, keepdims=True) + EPS)
    return (y * gain) * (1.0 + scale) + shift


def _mod_body(c_ref, w_ref, b_ref, o_ref):
    o_ref[...] = jnp.dot(_silu(c_ref[...]), w_ref[...], preferred_element_type=F32,
                         precision=HIGHEST) + b_ref[...]


def _modulation(c, w3, b3, layer):
    bsz, d = c.shape
    n = w3.shape[-1]
    tn = 1024
    return pl.pallas_call(
        _mod_body,
        out_shape=jax.ShapeDtypeStruct((bsz, n), F32),
        grid=(n // tn,),
        in_specs=[pl.BlockSpec((bsz, d), lambda j: (0, 0)),
                  pl.BlockSpec((None, d, tn), lambda j: (layer, 0, j)),
                  pl.BlockSpec((None, 1, tn), lambda j: (layer, 0, j))],
        out_specs=pl.BlockSpec((bsz, tn), lambda j: (0, j)),
        compiler_params=_cparams(("arbitrary",)),
        name="adaln_mod",
    )(c, w3, b3)


INPROJ_TM = 256
INPROJ_TN = 256
HALO = 8


def _inproj_body(x_ref, gain_ref, sh_ref, sc_ref, w_ref, conv_ref, o_ref, ext, *, tiles_per_seq):
    i = pl.program_id(0)
    tm = x_ref.shape[0]
    hd = A_HEAD_DIM
    n = o_ref.shape[-1]
    tn = INPROJ_TN
    h = _ada_norm(x_ref[...], gain_ref[...], sh_ref[...], sc_ref[...]).astype(BF16)

    @pl.when(i % tiles_per_seq == 0)
    def _():
        ext[0:HALO, :] = jnp.zeros((HALO, ext.shape[-1]), F32)

    def conv_silu_norm(raw, n0):
        cols = slice(n0, n0 + tn)
        ext[HALO:HALO + tm, cols] = raw
        acc = raw * conv_ref[A_CONV - 1:A_CONV, cols]
        for s in range(1, A_CONV):
            acc = acc + ext[HALO - s:HALO - s + tm, cols] * conv_ref[A_CONV - 1 - s:A_CONV - s, cols]
        ext[0:HALO, cols] = raw[tm - HALO:, :]
        y = _silu(acc)
        if n0 >= 2 * A_WIDTH:
            return y
        scale = hd ** -0.5 if n0 < A_WIDTH else 1.0
        heads = []
        for c0 in range(0, tn, hd):
            yh = y[:, c0:c0 + hd]
            heads.append(yh * (lax.rsqrt(jnp.sum(yh * yh, axis=-1, keepdims=True) + EPS) * scale))
        return jnp.concatenate(heads, axis=1)

    matmul = lambda n0: jnp.dot(h, w_ref[:, n0:n0 + tn], preferred_element_type=F32)
    conv_chunks = list(range(0, 3 * A_WIDTH, tn))
    plain_chunks = list(range(3 * A_WIDTH, n, tn))
    every = max(1, len(conv_chunks) // max(1, len(plain_chunks)))
    raw = matmul(conv_chunks[0])
    for idx, n0 in enumerate(conv_chunks):
        nxt = matmul(conv_chunks[idx + 1]) if idx + 1 < len(conv_chunks) else None
        if idx % every == every - 1 and plain_chunks:
            p0 = plain_chunks.pop(0)
            o_ref[:, p0:p0 + tn] = matmul(p0)
        o_ref[:, n0:n0 + tn] = conv_silu_norm(raw, n0)
        raw = nxt
    for p0 in plain_chunks:
        o_ref[:, p0:p0 + tn] = matmul(p0)


def _in_proj(x2, gain, mod4, w_bf, conv_w, seq):
    t, d = x2.shape
    n = w_bf.shape[-1]
    tm = INPROJ_TM
    tps = seq // tm
    return pl.pallas_call(
        functools.partial(_inproj_body, tiles_per_seq=tps),
        out_shape=jax.ShapeDtypeStruct((t, n), F32),
        grid=(t // tm,),
        in_specs=[pl.BlockSpec((tm, d), lambda i: (i, 0)),
                  pl.BlockSpec((1, d), lambda i: (0, 0)),
                  pl.BlockSpec((None, None, 1, d), lambda i: (i // tps, 0, 0, 0)),
                  pl.BlockSpec((None, None, 1, d), lambda i: (i // tps, 1, 0, 0)),
                  pl.BlockSpec((d, n), lambda i: (0, 0)),
                  pl.BlockSpec(conv_w.shape, lambda i: (0, 0))],
        out_specs=pl.BlockSpec((tm, n), lambda i: (i, 0)),
        scratch_shapes=[pltpu.VMEM((HALO + tm, conv_w.shape[-1]), F32)],
        compiler_params=_cparams(("arbitrary",)),
        name="gdn_in_proj",
    )(x2, gain, mod4, mod4, w_bf, conv_w)


GDN_HG = 2
GDN_RB = 256
GDN_SC = 128
GDN_SUB = 4


def _split_bf16(x):
    hi = x.astype(BF16)
    return hi, (x - hi.astype(F32)).astype(BF16)


def _mm3(a_hi, a_lo, b_hi, b_lo, dims=(((1,), (0,)), ((), ()))):
    dg = functools.partial(lax.dot_general, dimension_numbers=dims, preferred_element_type=F32)
    return dg(a_hi, b_hi) + (dg(a_lo, b_hi) + dg(a_hi, b_lo))


def _dot3(a, b, dims):
    return _mm3(*_split_bf16(a), *_split_bf16(b), dims)


def _gdn_body(alog_ref, dtb_ref, q_ref, k_ref, v_ref, z_ref, ab_ref, og_ref,
              o_ref, bs, gcs, mp, bc, qp, op):
    seq = q_ref.shape[0]
    hd = A_HEAD_DIM
    ch = A_CHUNK
    g_idx = pl.program_id(1)

    lane = lax.broadcasted_iota(jnp.int32, (GDN_RB, hd), 1)
    row = lax.broadcasted_iota(jnp.int32, (GDN_RB, hd), 0)
    row_in_chunk = row % ch

    sc_rows = GDN_SC
    per_sc = sc_rows // ch
    rb_per_step = sc_rows * GDN_SUB // GDN_RB
    head_cols = lambda hh: slice(hh * hd, (hh + 1) * hd)

    def prep_stages(j):
        for rbi in range(rb_per_step):
            r = j * rb_per_step + rbi
            start = pl.multiple_of(r * GDN_RB, GDN_RB)
            rows = pl.ds(start, GDN_RB)
            for hh in range(GDN_HG):
                head = g_idx * GDN_HG + hh
                ab = ab_ref[rows, :]
                a_col = jnp.sum(jnp.where(lane == head, ab, 0.0), axis=-1, keepdims=True)
                b_col = jnp.sum(jnp.where(lane == head + A_HEADS, ab, 0.0), axis=-1, keepdims=True)
                bs[hh, rows, :] = jnp.broadcast_to(jax.nn.sigmoid(b_col), (GDN_RB, hd))
                xa = a_col + dtb_ref[0, head]
                softplus = jnp.maximum(xa, 0.0) + jnp.log1p(jnp.exp(-jnp.abs(xa)))
                g = jnp.broadcast_to(-jnp.exp(alog_ref[0, head]) * softplus, (GDN_RB, hd))
                s = 1
                while s < ch:
                    g = g + jnp.where(row_in_chunk >= s, pltpu.roll(g, s, 0), 0.0)
                    s *= 2
                gcs[hh, rows, :] = g
                yield

    ri = lax.broadcasted_iota(jnp.int32, (sc_rows, sc_rows), 0)
    cj = lax.broadcasted_iota(jnp.int32, (sc_rows, sc_rows), 1)
    same_chunk = (ri // ch) == (cj // ch)
    incl = jnp.logical_and(same_chunk, ri >= cj)
    strict = jnp.logical_and(same_chunk, ri > cj)
    eye = jnp.where(ri == cj, 1.0, 0.0)
    chunk_of_col = lax.broadcasted_iota(jnp.int32, (hd, sc_rows), 1) // ch
    lanes_nt = (((1,), (1,)), ((), ()))
    plain = (((1,), (0,)), ((), ()))

    chunks_per_step = GDN_SUB * per_sc
    n_steps = seq // (sc_rows * GDN_SUB)

    def local_stages(i):
        chains = [(i * GDN_SUB + sub, hh) for sub in range(GDN_SUB) for hh in range(GDN_HG)]
        rows_of = lambda blk: pl.ds(pl.multiple_of(blk * sc_rows, sc_rows), sc_rows)

        def start(blk, hh):
            rows = rows_of(blk)
            k = k_ref[rows, head_cols(hh)]
            gc = gcs[hh, rows, :]
            decay = jnp.where(incl, jnp.exp(jnp.minimum(gc - gc.T[0:1, :], 0.0)), 0.0)
            k_bf = k.astype(BF16)
            kk = lax.dot_general((k * bs[hh, rows, :]).astype(BF16), k_bf, lanes_nt, preferred_element_type=F32)
            qk = lax.dot_general(q_ref[rows, head_cols(hh)].astype(BF16), k_bf, lanes_nt, preferred_element_type=F32)
            return jnp.where(strict, -(kk * decay), 0.0), (qk * decay).astype(BF16)

        started = [start(*c) for c in chains]
        yield
        qks = [s[1] for s in started]
        invs = [eye + s[0] for s in started]
        pws = [_dot3(s[0], s[0], plain) for s in started]
        yield
        for step in range(1, 6):
            for n in range(len(chains)):
                pw_hi, pw_lo = _split_bf16(pws[n])
                inv_hi, inv_lo = _split_bf16(invs[n])
                if step < 5:
                    prod = _mm3(jnp.concatenate([pw_hi, inv_hi], axis=0),
                                jnp.concatenate([pw_lo, inv_lo], axis=0), pw_hi, pw_lo)
                    pws[n] = prod[:sc_rows]
                    invs[n] = invs[n] + prod[sc_rows:]
                else:
                    invs[n] = invs[n] + _mm3(inv_hi, inv_lo, pw_hi, pw_lo)
            yield

        def solve(n, blk, hh):
            rows = rows_of(blk)
            beta = bs[hh, rows, :]
            rhs = jnp.concatenate([v_ref[rows, head_cols(hh)] * beta, k_ref[rows, head_cols(hh)] * beta * jnp.exp(gcs[hh, rows, :])], axis=1)
            return _dot3(invs[n], rhs, plain).astype(BF16)

        uws = [solve(n, *c) for n, c in enumerate(chains)]
        yield

        def finish(n, blk, hh):
            rows = rows_of(blk)
            k = k_ref[rows, head_cols(hh)]
            gc = gcs[hh, rows, :]
            res = jnp.dot(qks[n], uws[n], preferred_element_type=F32)
            op[hh, rows, :] = res[:, :hd]
            qp[hh, rows, :] = (q_ref[rows, head_cols(hh)] * jnp.exp(gc) - res[:, hd:]).astype(BF16)
            gl = jnp.concatenate(
                [jnp.broadcast_to(gc[(j + 1) * ch - 1:(j + 1) * ch, :], (ch, hd)) for j in range(per_sc)], axis=0)
            kt_t = (k * jnp.exp(gl - gc)).T
            for j in range(per_sc):
                kt_j = jnp.where(chunk_of_col == j, kt_t, 0.0).astype(BF16)
                bm = jnp.dot(kt_j, uws[n], preferred_element_type=F32)
                bc[hh, blk * per_sc + j] = bm[:, :hd]
                mp[hh, blk * per_sc + j] = bm[:, hd:].astype(BF16)

        for n, c in enumerate(chains):
            finish(n, *c)

    def scan_stages(j, states):
        for cc in range(chunks_per_step):
            c = j * chunks_per_step + cc
            r0 = pl.multiple_of(c * ch, ch)
            rows = pl.ds(r0, ch)
            for hh in range(GDN_HG):
                col0 = hh * hd
                state = states[hh]
                s_bf = state.astype(BF16)
                o = jnp.dot(qp[hh, rows, :], s_bf, preferred_element_type=F32) + op[hh, rows, :]
                g_tot = jnp.exp(gcs[hh, pl.ds(r0 + ch - 1, 1), :])
                states[hh] = state * g_tot - jnp.dot(mp[hh, c], s_bf, preferred_element_type=F32) + bc[hh, c]
                on = o * lax.rsqrt(jnp.mean(o * o, axis=-1, keepdims=True) + EPS) * og_ref[...]
                z = z_ref[rows, col0:col0 + hd]
                o_ref[rows, col0:col0 + hd] = (on * _silu(z)).astype(BF16)
            yield

    def interleave(*gens):
        live = list(gens)
        while live:
            for g in list(live):
                try:
                    next(g)
                except StopIteration:
                    live.remove(g)

    assert n_steps >= 2
    interleave(prep_stages(0))
    interleave(local_stages(0), prep_stages(1))

    def pipelined(i, states):
        states = list(states)
        interleave(local_stages(i), scan_stages(i - 1, states), prep_stages(i + 1))
        return tuple(states)

    states = list(lax.fori_loop(1, n_steps - 1, pipelined,
                                tuple(jnp.zeros((hd, hd), F32) for _ in range(GDN_HG))))
    interleave(local_stages(n_steps - 1), scan_stages(n_steps - 2, states))
    interleave(scan_stages(n_steps - 1, states))


def _gdn(proj, a_log, dt_bias, out_gain, bsz, seq):
    t = proj.shape[0]
    hd = A_HEAD_DIM
    wb = hd * GDN_HG
    ng = A_HEADS // GDN_HG
    per = A_WIDTH // wb
    n_chunks = seq // A_CHUNK
    smem = pl.BlockSpec(memory_space=pltpu.SMEM)
    seq_spec = lambda off: pl.BlockSpec((seq, wb), lambda b, g, off=off: (b, off + g))
    sc = lambda dt=F32: pltpu.VMEM((GDN_HG, seq, hd), dt)
    return pl.pallas_call(
        _gdn_body,
        out_shape=jax.ShapeDtypeStruct((t, A_WIDTH), BF16),
        grid=(bsz, ng),
        in_specs=[smem, smem,
                  seq_spec(0), seq_spec(per), seq_spec(2 * per), seq_spec(3 * per),
                  pl.BlockSpec((seq, LANES), lambda b, g: (b, 4 * A_WIDTH // LANES)),
                  pl.BlockSpec((1, hd), lambda b, g: (0, 0))],
        out_specs=pl.BlockSpec((seq, wb), lambda b, g: (b, g)),
        scratch_shapes=[sc(), sc(),
                        pltpu.VMEM((GDN_HG, n_chunks, hd, hd), BF16),
                        pltpu.VMEM((GDN_HG, n_chunks, hd, hd), F32),
                        sc(BF16), sc()],
        compiler_params=_cparams(("parallel", "parallel")),
        name="gdn_core",
    )(a_log, dt_bias, proj, proj, proj, proj, proj, out_gain)


def _oproj_body(a_ref, w_ref, x_ref, g_ref, o_ref):
    y = jnp.dot(a_ref[...], w_ref[...], preferred_element_type=F32)
    o_ref[...] = x_ref[...] + g_ref[...] * y


def _out_proj_residual(a_bf, w_bf, x2, mod4, gate_slot, seq):
    t, d = x2.shape
    kdim = a_bf.shape[-1]
    tm = 512
    tps = seq // tm
    return pl.pallas_call(
        _oproj_body,
        out_shape=jax.ShapeDtypeStruct((t, d), F32),
        grid=(t // tm,),
        in_specs=[pl.BlockSpec((tm, kdim), lambda i: (i, 0)),
                  pl.BlockSpec((kdim, d), lambda i: (0, 0)),
                  pl.BlockSpec((tm, d), lambda i: (i, 0)),
                  pl.BlockSpec((None, None, 1, d), lambda i: (i // tps, gate_slot, 0, 0))],
        out_specs=pl.BlockSpec((tm, d), lambda i: (i, 0)),
        compiler_params=_cparams(("parallel",)),
        name="out_proj_residual",
    )(a_bf, w_bf, x2, mod4)


ROUTER_TM = 512


K_ROWS = 8


def _router_body(x_ref, gain_ref, sh_ref, sc_ref, rw_ref, rb_ref, h_ref, idx_ref, gate_ref, rank_ref,
                 cnt_ref, carry):
    i = pl.program_id(0)

    @pl.when(i == 0)
    def _():
        carry[...] = jnp.zeros_like(carry)

    tm = x_ref.shape[0]
    h = _ada_norm(x_ref[...], gain_ref[...], sh_ref[...], sc_ref[...])
    h_ref[...] = _pack_bf16_pairs(h)
    logits = _dot3(h, rw_ref[...], (((1,), (0,)), ((), ()))) + rb_ref[...]
    work = logits.T[:N_EXPERTS, :]
    expert = lax.broadcasted_iota(jnp.int32, (N_EXPERTS, tm), 0)
    tops, sels = [], []
    for _ in range(TOP_K):
        m = jnp.max(work, axis=0, keepdims=True)
        sel = jnp.min(jnp.where(work == m, expert, N_EXPERTS), axis=0, keepdims=True)
        work = jnp.where(expert == sel, NEG_INF, work)
        tops.append(m)
        sels.append(sel)
    exps = [jnp.exp(m - tops[0]) for m in tops]
    denom = exps[0] + exps[1] + exps[2] + exps[3]
    onehot = jnp.zeros((N_EXPERTS, tm), F32)
    for sel in sels:
        onehot = onehot + jnp.where(expert == sel, 1.0, 0.0)
    ti = lax.broadcasted_iota(jnp.int32, (tm, tm), 0)
    tj = lax.broadcasted_iota(jnp.int32, (tm, tm), 1)
    earlier = (ti < tj).astype(BF16)
    before = jnp.dot(onehot.astype(BF16), earlier, preferred_element_type=F32) + carry[:, 0:1]
    ranks = [jnp.sum(jnp.where(expert == sel, before, 0.0), axis=0, keepdims=True) for sel in sels]
    pad_i = jnp.zeros((K_ROWS - TOP_K, tm), jnp.int32)
    idx_ref[...] = jnp.concatenate(sels + [pad_i], axis=0)
    gate_ref[...] = jnp.concatenate([e / denom for e in exps] + [pad_i.astype(F32)], axis=0)
    rank_ref[...] = jnp.concatenate([r.astype(jnp.int32) for r in ranks] + [pad_i], axis=0)
    carry[...] = carry[...] + jnp.sum(onehot, axis=1, keepdims=True)
    cnt_ref[...] = carry[...]


def _router(x2, gain, mod4, sh_slot, sc_slot, rw_pad, rb_pad, seq):
    t, d = x2.shape
    tm = ROUTER_TM
    tps = seq // tm
    tok = lambda dt: jax.ShapeDtypeStruct((K_ROWS, t), dt)
    tok_spec = pl.BlockSpec((K_ROWS, tm), lambda i: (0, i))
    cnt_spec = pl.BlockSpec((N_EXPERTS, LANES), lambda i: (0, 0))
    return pl.pallas_call(
        _router_body,
        out_shape=(jax.ShapeDtypeStruct((t, d // 2), jnp.int32), tok(jnp.int32), tok(F32), tok(jnp.int32),
                   jax.ShapeDtypeStruct((N_EXPERTS, LANES), F32)),
        grid=(t // tm,),
        in_specs=[pl.BlockSpec((tm, d), lambda i: (i, 0)),
                  pl.BlockSpec((1, d), lambda i: (0, 0)),
                  pl.BlockSpec((None, None, 1, d), lambda i: (i // tps, sh_slot, 0, 0)),
                  pl.BlockSpec((None, None, 1, d), lambda i: (i // tps, sc_slot, 0, 0)),
                  pl.BlockSpec((d, LANES), lambda i: (0, 0)),
                  pl.BlockSpec((1, LANES), lambda i: (0, 0))],
        out_specs=(pl.BlockSpec((tm, d // 2), lambda i: (i, 0)), tok_spec, tok_spec, tok_spec, cnt_spec),
        scratch_shapes=[pltpu.VMEM((N_EXPERTS, LANES), F32)],
        compiler_params=_cparams(("arbitrary",)),
        name="moe_router",
    )(x2, gain, mod4, mod4, rw_pad, rb_pad)


SC_LANES = 16
SC_INDEX_CHUNK = 8192


def _sc_row_tokens(dest_flat, n_rows, n_tok):
    n_assign = dest_flat.shape[0]
    assert n_rows % SC_LANES == 0 and n_assign % SC_INDEX_CHUNK == 0
    n_cores = plsc.get_sparse_core_info().num_cores
    mesh = plsc.VectorSubcoreMesh(core_axis_name="c", subcore_axis_name="s")

    @functools.partial(
        pl.kernel, mesh=mesh,
        out_type=jax.ShapeDtypeStruct((n_rows,), jnp.int32),
        scratch_types=[pltpu.VMEM((n_rows,), jnp.int32), pltpu.VMEM((SC_INDEX_CHUNK,), jnp.int32)],
        compiler_params=dataclasses.replace(pltpu.CompilerParams(), needs_layout_passes=False),
    )
    def row_token_kernel(dest_hbm, out_hbm, rt_v, d_v):
        wid = lax.axis_index("s") * n_cores + lax.axis_index("c")

        @pl.when(wid == 0)
        def _():
            lanes = lax.iota(jnp.int32, SC_LANES)

            @pl.loop(0, n_rows // SC_LANES)
            def _(i):
                rt_v[pl.ds(i * SC_LANES, SC_LANES)] = lax.rem(lanes + i * SC_LANES, n_tok)

            @pl.loop(0, n_assign // SC_INDEX_CHUNK)
            def _(c):
                pltpu.sync_copy(dest_hbm.at[pl.ds(c * SC_INDEX_CHUNK, SC_INDEX_CHUNK)], d_v)

                @pl.loop(0, SC_INDEX_CHUNK // SC_LANES)
                def _(i):
                    idx = d_v[pl.ds(i * SC_LANES, SC_LANES)]
                    tok = lax.rem(lanes + (c * SC_INDEX_CHUNK + i * SC_LANES), n_tok)
                    plsc.store_scatter(rt_v, [idx], tok)

            pltpu.sync_copy(rt_v, out_hbm)

    return row_token_kernel(dest_flat)


def _ffn_body(be_ref, nu_ref, nxt_ref, slot_ref, x_ref, uw_hbm, ub_ref, dw_hbm, db_ref, *rest, layer):
    y_ref, uw_f32, dw_f32, uw_bf, dw_bf, sems = rest[-6:]
    i = pl.program_id(0)
    first = jnp.logical_and(jnp.logical_or(i == 0, be_ref[i] != be_ref[jnp.maximum(i - 1, 0)]), i < nu_ref[0])
    active = i < nu_ref[0]
    slot = slot_ref[i]

    def weight_copies(expert, s):
        return (pltpu.make_async_copy(uw_hbm.at[layer, expert], uw_f32.at[s], sems.at[s, 0]),
                pltpu.make_async_copy(dw_hbm.at[layer, expert], dw_f32.at[s], sems.at[s, 1]))

    @pl.when(jnp.logical_and(i == 0, active))
    def _():
        for cp in weight_copies(be_ref[0], 0):
            cp.start()

    @pl.when(first)
    def _():
        for cp in weight_copies(be_ref[i], slot):
            cp.wait()

        @pl.when(nxt_ref[i] >= 0)
        def _():
            for cp in weight_copies(nxt_ref[i], 1 - slot):
                cp.start()

        rows = 128
        for r0 in range(0, uw_bf.shape[0], rows):
            uw_bf[r0:r0 + rows, :] = uw_f32[slot, r0:r0 + rows, :].astype(BF16)
        for r0 in range(0, dw_bf.shape[0], rows):
            dw_bf[r0:r0 + rows, :] = dw_f32[slot, r0:r0 + rows, :].astype(BF16)

    @pl.when(active)
    def _():
        x = jnp.concatenate(_unpack_bf16_pairs(x_ref[...]), axis=1).astype(BF16)
        gu = jnp.dot(x, uw_bf[...], preferred_element_type=F32) + ub_ref[...]
        gate = jnp.minimum(gu[:, :D_FF], SWIGLU_LIMIT)
        lin = jnp.clip(gu[:, D_FF:], -SWIGLU_LIMIT, SWIGLU_LIMIT)
        act = gate * jax.nn.sigmoid(SWIGLU_ALPHA * gate) * (lin + 1.0)
        y = jnp.dot(act.astype(BF16), dw_bf[...], preferred_element_type=F32) + db_ref[...]
        y_ref[...] = _pack_bf16_pairs(y)

    @pl.when(jnp.logical_not(active))
    def _():
        y_ref[...] = jnp.zeros_like(y_ref)


def _expert_ffn(xb, blk_expert, n_used, up_w, up_b4, down_w, down_b4, layer, n_rows_total, first_blk, y_prev):
    n_rows, dp = xb.shape
    d = 2 * dp
    bm = MOE_ROWS
    n_blk = n_rows // bm
    f2 = up_w.shape[-1]
    blk = jnp.arange(n_blk, dtype=jnp.int32)
    first = jnp.logical_and(jnp.concatenate([jnp.ones((1,), bool), blk_expert[1:] != blk_expert[:-1]]),
                            blk < n_used[0])
    slot = ((jnp.cumsum(first.astype(jnp.int32)) - 1) % 2).astype(jnp.int32)
    later_first = jnp.where(first, blk, n_blk)[::-1]
    next_first = jnp.concatenate([lax.cummin(later_first)[::-1][1:], jnp.full((1,), n_blk, jnp.int32)])
    nxt = jnp.where(jnp.logical_and(first, next_first < n_blk),
                    blk_expert[jnp.minimum(next_first, n_blk - 1)], -1).astype(jnp.int32)
    grid_spec = pltpu.PrefetchScalarGridSpec(
        num_scalar_prefetch=4,
        grid=(n_blk,),
        in_specs=[pl.BlockSpec((bm, dp), lambda i, be, nu, nx, sl: (i, 0)),
                  pl.BlockSpec(memory_space=pl.ANY),
                  pl.BlockSpec((None, None, 1, f2), lambda i, be, nu, nx, sl: (layer, be[i], 0, 0)),
                  pl.BlockSpec(memory_space=pl.ANY),
                  pl.BlockSpec((None, None, 1, d), lambda i, be, nu, nx, sl: (layer, be[i], 0, 0))]
                 + ([] if y_prev is None else [pl.BlockSpec(memory_space=pl.ANY)]),
        out_specs=pl.BlockSpec((bm, dp), lambda i, be, nu, nx, sl: (i + first_blk, 0)),
        scratch_shapes=[pltpu.VMEM((2, d, f2), F32), pltpu.VMEM((2, f2 // 2, d), F32),
                        pltpu.VMEM((d, f2), BF16), pltpu.VMEM((f2 // 2, d), BF16),
                        pltpu.SemaphoreType.DMA((2, 2))],
    )
    args = (blk_expert, n_used, nxt, slot, xb, up_w, up_b4, down_w, down_b4)
    return pl.pallas_call(
        functools.partial(_ffn_body, layer=layer),
        out_shape=jax.ShapeDtypeStruct((n_rows_total, dp), jnp.int32),
        grid_spec=grid_spec,
        input_output_aliases={} if y_prev is None else {len(args): 0},
        compiler_params=_cparams(("arbitrary",)),
        name="moe_expert_ffn",
    )(*args, *(() if y_prev is None else (y_prev,)))


SC_GATHER_ROWS = 64


def _sc_gather_rows(table, idx):
    n, d = idx.shape[0], table.shape[1]
    info = plsc.get_sparse_core_info()
    n_cores, n_sub = info.num_cores, info.num_subcores
    n_workers = n_cores * n_sub
    chunk = SC_GATHER_ROWS
    per_worker = n // n_workers
    n_chunks = per_worker // chunk
    assert n_chunks * chunk * n_workers == n and n_chunks % 2 == 0
    mesh = plsc.VectorSubcoreMesh(core_axis_name="c", subcore_axis_name="s")

    @functools.partial(
        pl.kernel, mesh=mesh,
        out_type=jax.ShapeDtypeStruct((n, d), table.dtype),
        scratch_types=[pltpu.VMEM((n_chunks, chunk), jnp.int32), pltpu.VMEM((2, chunk, d), table.dtype),
                       pltpu.SemaphoreType.DMA((2,))],
    )
    def gather_kernel(table_hbm, idx_hbm, out_hbm, idx_v, rows_v, sems):
        wid = lax.axis_index("s") * n_cores + lax.axis_index("c")
        base = wid * per_worker
        pltpu.sync_copy(idx_hbm.at[wid], idx_v)

        def gather(ci, slot):
            return pltpu.make_async_copy(table_hbm.at[idx_v.at[ci]], rows_v.at[slot], sems.at[slot])

        gather(0, 0).start()

        @pl.loop(0, n_chunks, step=2)
        def _(c0):
            for slot in range(2):
                ci = c0 + slot
                gather(ci, slot).wait()

                @pl.when(ci + 1 < n_chunks)
                def _():
                    gather(ci + 1, 1 - slot).start()

                pltpu.sync_copy(rows_v.at[slot], out_hbm.at[pl.ds(base + ci * chunk, chunk)])

    return gather_kernel(table, idx.reshape(n_workers, n_chunks, chunk))


COMBINE_TM = 256


def _combine_body(gate_ref, x_ref, g_ref, y_ref, o_ref):
    gates = gate_ref[...]
    acc_lo = acc_hi = None
    for k in range(TOP_K):
        lo, hi = _unpack_bf16_pairs(y_ref[k])
        gk = gates[:, k:k + 1]
        acc_lo = gk * lo if acc_lo is None else acc_lo + gk * lo
        acc_hi = gk * hi if acc_hi is None else acc_hi + gk * hi
    o_ref[...] = x_ref[...] + g_ref[...] * jnp.concatenate([acc_lo, acc_hi], axis=1)


def _combine(y_kt, gates, x2, mod4, gate_slot, seq):
    t, d = x2.shape
    tm = COMBINE_TM
    tps = seq // tm
    return pl.pallas_call(
        _combine_body,
        out_shape=jax.ShapeDtypeStruct((t, d), F32),
        grid=(t // tm,),
        in_specs=[pl.BlockSpec((tm, gates.shape[1]), lambda i: (i, 0)),
                  pl.BlockSpec((tm, d), lambda i: (i, 0)),
                  pl.BlockSpec((None, None, 1, d), lambda i: (i // tps, gate_slot, 0, 0)),
                  pl.BlockSpec((TOP_K, tm, d // 2), lambda i: (0, i, 0))],
        out_specs=pl.BlockSpec((tm, d), lambda i: (i, 0)),
        compiler_params=_cparams(("parallel",)),
        name="moe_combine",
    )(gates, x2, mod4, y_kt)


def _moe_layer(x2, gain, mod4, router_w, router_b, up_w, up_b, down_w, down_b, layer, seq):
    t, d = x2.shape
    rw_pad = jnp.pad(router_w[layer], ((0, 0), (0, LANES - N_EXPERTS)))
    rb_pad = jnp.pad(router_b[layer], (0, LANES - N_EXPERTS)).reshape(1, LANES)
    h2, idx, gates, rank, counts = _router(x2, gain, mod4, 3, 4, rw_pad, rb_pad, seq)

    bm = MOE_ROWS
    counts = counts[:, 0].astype(jnp.int32)
    padded = (counts + bm - 1) // bm * bm
    p_ends = jnp.cumsum(padded)
    p_starts = p_ends - padded
    n_rows = t * TOP_K + N_EXPERTS * bm
    n_blk = n_rows // bm
    onehot = idx[:TOP_K, :, None] == jnp.arange(N_EXPERTS, dtype=jnp.int32)[None, None, :]
    dest = jnp.sum(jnp.where(onehot, p_starts[None, None, :], 0), axis=-1) + rank[:TOP_K]
    dest_flat = dest.reshape(TOP_K * t)
    blk_start = jnp.arange(n_blk, dtype=jnp.int32) * bm
    blk_expert = jnp.sum((blk_start[:, None] >= p_ends[None, :]).astype(jnp.int32), axis=-1)
    blk_expert = jnp.minimum(blk_expert, N_EXPERTS - 1).astype(jnp.int32)
    n_used = (p_ends[-1:] // bm).astype(jnp.int32)

    row_token = _sc_row_tokens(dest_flat, n_rows, t)
    f2 = up_w.shape[-1]
    up_b4 = up_b.reshape(up_b.shape[0], N_EXPERTS, 1, f2)
    down_b4 = down_b.reshape(down_b.shape[0], N_EXPERTS, 1, d)
    part_blk = n_blk // MOE_PARTS
    xb_parts = [_sc_gather_rows(h2, row_token[p * part_blk * bm:(p + 1) * part_blk * bm]) for p in range(MOE_PARTS)]
    y_rows = None
    for p, xb in enumerate(xb_parts):
        y_rows = _expert_ffn(xb, blk_expert[p * part_blk:(p + 1) * part_blk],
                             jnp.clip(n_used - p * part_blk, 0, part_blk), up_w, up_b4, down_w, down_b4,
                             layer, n_rows, p * part_blk, y_rows)
    y_kt = _sc_gather_rows(y_rows, dest_flat).reshape(TOP_K, t, d // 2)
    return _combine(y_kt, gates.T, x2, mod4, 5, seq)


QKV_TM = 256


def _rope_tables(pos_row, invf_col):
    tm = pos_row.shape[1]
    half = ROPE_DIM // 2
    ang = invf_col * pos_row
    expand = (lax.broadcasted_iota(jnp.int32, (half, LANES), 1) % half
              == lax.broadcasted_iota(jnp.int32, (half, LANES), 0)).astype(BF16)

    def to_lanes(x):
        hi = x.astype(BF16)
        rest = x - hi.astype(F32)
        mid = rest.astype(BF16)
        parts = (hi, mid, (rest - mid.astype(F32)).astype(BF16))
        return sum(lax.dot_general(p, expand, (((0,), (0,)), ((), ())), preferred_element_type=F32) for p in parts)

    c = to_lanes(jnp.cos(ang))
    s = to_lanes(jnp.sin(ang))
    d = lax.broadcasted_iota(jnp.int32, (tm, LANES), 1) % B_HEAD_DIM
    cos_t = jnp.where(d < ROPE_DIM, c, 1.0)
    sin_lo = jnp.where(d < half, -s, 0.0)
    sin_hi = jnp.where(jnp.logical_and(d >= half, d < ROPE_DIM), s, 0.0)
    return cos_t, sin_lo, sin_hi


def _head_norm_rope(x, gain_row, tables):
    cos_t, sin_lo, sin_hi = tables
    same_head = (lax.broadcasted_iota(jnp.int32, (LANES, LANES), 0) // B_HEAD_DIM
                 == lax.broadcasted_iota(jnp.int32, (LANES, LANES), 1) // B_HEAD_DIM).astype(BF16)
    sq_hi, sq_lo = _split_bf16(x * x)
    ss = (jnp.dot(sq_hi, same_head, preferred_element_type=F32)
          + jnp.dot(sq_lo, same_head, preferred_element_type=F32))
    xn = x * lax.rsqrt(ss * (1.0 / B_HEAD_DIM) + EPS) * gain_row
    half = ROPE_DIM // 2
    return (xn * cos_t + pltpu.roll(xn, LANES - half, 1) * sin_lo + pltpu.roll(xn, half, 1) * sin_hi)


def _qkv_body(x_ref, pos_ref, gq_ref, shq_ref, scq_ref, gkv_ref, shkv_ref, sckv_ref, wq_ref, wkv_ref,
              qg_ref, kg_ref, invf_ref, q_ref, k_ref, v_ref):
    x = x_ref[...]
    y = x * lax.rsqrt(jnp.mean(x * x, axis=-1, keepdims=True) + EPS)
    hq = ((y * gq_ref[...]) * (1.0 + scq_ref[...]) + shq_ref[...]).astype(BF16)
    hkv = ((y * gkv_ref[...]) * (1.0 + sckv_ref[...]) + shkv_ref[...]).astype(BF16)
    tables = _rope_tables(pos_ref[...].astype(F32), invf_ref[...])
    kv = jnp.dot(hkv, wkv_ref[...], preferred_element_type=F32)
    k_ref[...] = _head_norm_rope(kv[:, :LANES], kg_ref[...], tables).astype(BF16)
    v_ref[...] = kv[:, LANES:].astype(BF16)
    q = jnp.dot(hq, wq_ref[...], preferred_element_type=F32)
    scale = B_HEAD_DIM ** -0.5
    for p in range(q.shape[-1] // LANES):
        qp = _head_norm_rope(q[:, p * LANES:(p + 1) * LANES], qg_ref[...], tables)
        q_ref[:, p * LANES:(p + 1) * LANES] = (qp * scale).astype(BF16)


def _qkv(x2, pos2, gq, mod4, gkv, kvmod4, wq_bf, wkv_bf, qg2, kg2, invf, seq):
    t, d = x2.shape
    tm = QKV_TM
    tps = seq // tm
    nq = wq_bf.shape[-1]
    nkv = wkv_bf.shape[-1]
    row = lambda n: pl.BlockSpec((1, n), lambda i: (0, 0))
    modspec = lambda slot: pl.BlockSpec((None, None, 1, d), lambda i, slot=slot: (i // tps, slot, 0, 0))
    return pl.pallas_call(
        _qkv_body,
        out_shape=(jax.ShapeDtypeStruct((t, nq), BF16), jax.ShapeDtypeStruct((t, LANES), BF16),
                   jax.ShapeDtypeStruct((t, LANES), BF16)),
        grid=(t // tm,),
        in_specs=[pl.BlockSpec((tm, d), lambda i: (i, 0)),
                  pl.BlockSpec((None, 1, tm), lambda i: (i, 0, 0)),
                  row(d), modspec(0), modspec(1),
                  row(d), modspec(0), modspec(1),
                  pl.BlockSpec((d, nq), lambda i: (0, 0)),
                  pl.BlockSpec((d, nkv), lambda i: (0, 0)),
                  row(LANES), row(LANES), pl.BlockSpec(invf.shape, lambda i: (0, 0))],
        out_specs=(pl.BlockSpec((tm, nq), lambda i: (i, 0)),
                   pl.BlockSpec((tm, LANES), lambda i: (i, 0)),
                   pl.BlockSpec((tm, LANES), lambda i: (i, 0))),
        compiler_params=_cparams(("parallel",)),
        name="swa_qkv_proj",
    )(x2, pos2.reshape(t // tm, 1, tm), gq, mod4, mod4, gkv, kvmod4, kvmod4, wq_bf, wkv_bf, qg2, kg2, invf)


def _attn_body(sink_ref, q_ref, kc_ref, kp_ref, vc_ref, vp_ref, w_ref, x_ref, g_ref, o_ref):
    n = pl.program_id(1)
    w = WINDOW
    qi = lax.broadcasted_iota(jnp.int32, (w, 2 * w), 0) + w
    kj = lax.broadcasted_iota(jnp.int32, (w, 2 * w), 1)
    band = jnp.logical_and(kj <= qi, qi - kj < w)
    mask = jnp.logical_and(band, jnp.logical_or(n > 0, kj >= w))
    hd = B_HEAD_DIM
    lane = lax.broadcasted_iota(jnp.int32, (2 * w, LANES), 1)
    first = lane < hd
    kfull = jnp.concatenate([kp_ref[...], kc_ref[...]], axis=0).astype(F32)
    vfull = jnp.concatenate([vp_ref[...], vc_ref[...]], axis=0).astype(F32)
    placed = {}
    for g in range(B_KV_HEADS):
        for name, full in (("k", kfull), ("v", vfull)):
            swapped = pltpu.roll(full, hd, 1)
            own_first = full if g == 0 else swapped
            own_second = swapped if g == 0 else full
            placed[name, g, 0] = jnp.where(first, own_first, 0.0).astype(BF16)
            placed[name, g, 1] = jnp.where(first, 0.0, own_second).astype(BF16)
    pairs = B_Q_HEADS // 2
    heads = [(p, side) for p in range(pairs) for side in range(2)]
    scores = []
    for p, side in heads:
        g = (2 * p) // B_GROUP
        q_pair = q_ref[:, p * LANES:(p + 1) * LANES]
        s = lax.dot_general(q_pair, placed["k", g, side], (((1,), (1,)), ((), ())), preferred_element_type=F32)
        scores.append(jnp.where(mask, s, NEG_INF))
    probs, denoms = [], []
    for (p, side), s in zip(heads, scores):
        sink = sink_ref[0, 2 * p + side]
        m = jnp.maximum(jnp.max(jnp.maximum(s[:, :w], s[:, w:]), axis=-1, keepdims=True), sink)
        e = jnp.exp(s - m)
        denoms.append(jnp.sum(e[:, :w] + e[:, w:], axis=-1, keepdims=True) + jnp.exp(sink - m))
        probs.append(e.astype(BF16))
    outs = []
    for (p, side), pr, den in zip(heads, probs, denoms):
        g = (2 * p) // B_GROUP
        outs.append(jnp.dot(pr, placed["v", g, side], preferred_element_type=F32) / den)
    attn = jnp.concatenate([(outs[2 * p] + outs[2 * p + 1]).astype(BF16) for p in range(pairs)], axis=1)
    o_ref[...] = x_ref[...] + g_ref[...] * jnp.dot(attn, w_ref[...], preferred_element_type=F32)


def _attention_residual(q, k, v, sinks2, w_out_bf, x2, mod4, gate_slot, bsz, seq):
    t, nq = q.shape
    d = x2.shape[-1]
    w = WINDOW
    nb = seq // w
    cur = lambda b, n: (b * nb + n, 0)
    prev = lambda b, n: (b * nb + jnp.maximum(n - 1, 0), 0)
    return pl.pallas_call(
        _attn_body,
        out_shape=jax.ShapeDtypeStruct((t, d), F32),
        grid=(bsz, nb),
        in_specs=[pl.BlockSpec(memory_space=pltpu.SMEM),
                  pl.BlockSpec((w, nq), cur),
                  pl.BlockSpec((w, LANES), cur), pl.BlockSpec((w, LANES), prev),
                  pl.BlockSpec((w, LANES), cur), pl.BlockSpec((w, LANES), prev),
                  pl.BlockSpec((nq, d), lambda b, n: (0, 0)),
                  pl.BlockSpec((w, d), cur),
                  pl.BlockSpec((None, None, 1, d), lambda b, n: (b, gate_slot, 0, 0))],
        out_specs=pl.BlockSpec((w, d), cur),
        compiler_params=_cparams(("parallel", "parallel")),
        name="swa_sink_attention",
    )(sinks2, q, k, k, v, v, w_out_bf, x2, mod4)


def kernel(x, c, positions, ada_w, ada_b, norm_gain, a_w_in, a_conv, a_log, a_dt_bias, a_out_gain, a_w_out,
           kv_ada_w, kv_ada_b, kv_norm_gain, kv_w, k_norm_gain, b_w_q, q_norm_gain, b_sinks, b_w_out,
           router_w, router_b, up_w, up_b, down_w, down_b):
    bsz, seq, d = x.shape
    t = bsz * seq
    depth = ada_w.shape[0]
    x2 = x.reshape(t, d)
    ada_b3 = ada_b.reshape(depth, 1, 6 * d)

    for layer in range(depth):
        mod4 = _modulation(c, ada_w, ada_b3, layer).reshape(bsz, 6, 1, d)
        gain1 = norm_gain[layer, 0].reshape(1, d)
        gain2 = norm_gain[layer, 1].reshape(1, d)
        if layer < N_A_LAYERS:
            w_in = jnp.pad(a_w_in[layer].astype(BF16), ((0, 0), (0, A_PROJ_PAD - a_w_in.shape[-1])))
            proj = _in_proj(x2, gain1, mod4, w_in, a_conv[layer], seq)
            o = _gdn(proj, a_log[layer].reshape(1, A_HEADS),
                     a_dt_bias[layer].reshape(1, A_HEADS), a_out_gain[layer].reshape(1, A_HEAD_DIM),
                     bsz, seq)
            x2 = _out_proj_residual(o, a_w_out[layer].astype(BF16), x2, mod4, 2, seq)
        else:
            j = layer - N_A_LAYERS
            kvmod4 = _modulation(c, kv_ada_w.reshape(1, d, 2 * d), kv_ada_b.reshape(1, 1, 2 * d), 0)
            kvmod4 = kvmod4.reshape(bsz, 2, 1, d)
            inv_freq = ROPE_THETA ** (-np.arange(0, ROPE_DIM, 2, dtype=np.float32) / ROPE_DIM)
            invf = jnp.asarray(inv_freq.astype(np.float32).reshape(ROPE_DIM // 2, 1))
            q, k, v = _qkv(x2, positions.reshape(t), gain1, mod4, kv_norm_gain.reshape(1, d), kvmod4,
                           b_w_q[j].astype(BF16), kv_w.astype(BF16),
                           jnp.tile(q_norm_gain[j], 2).reshape(1, LANES),
                           jnp.tile(k_norm_gain, 2).reshape(1, LANES), invf, seq)
            x2 = _attention_residual(q, k, v, b_sinks[j].reshape(1, B_Q_HEADS), b_w_out[j].astype(BF16),
                                     x2, mod4, 2, bsz, seq)
        x2 = _moe_layer(x2, gain2, mod4, router_w, router_b, up_w, up_b, down_w, down_b, layer, seq)
    return x2.reshape(bsz, seq, d)
```

```python
import dataclasses
import functools

import jax
import jax.numpy as jnp
import numpy as np
from jax import lax
from jax.experimental import pallas as pl
from jax.experimental.pallas import tpu as pltpu
from jax.experimental.pallas import tpu_sc as plsc

F32 = jnp.float32
BF16 = jnp.bfloat16
HIGHEST = lax.Precision.HIGHEST

N_A_LAYERS = 1

A_HEAD_DIM = 128
A_HEADS = 8
A_WIDTH = 1024
A_CONV = 4
A_CHUNK = 64
A_PROJ_PAD = 4608

B_HEAD_DIM = 64
B_Q_HEADS = 16
B_KV_HEADS = 2
B_GROUP = 8
WINDOW = 128
ROPE_DIM = 16
ROPE_THETA = 500000.0

N_EXPERTS = 32
TOP_K = 4
D_FF = 1024
SWIGLU_LIMIT = 7.0
SWIGLU_ALPHA = 1.702
MOE_ROWS = 512
MOE_PARTS = 2

EPS = 1e-6
LANES = 128
NEG_INF = float("-inf")

VMEM_LIMIT = 56 * 1024 * 1024


def _cparams(sem, vmem=VMEM_LIMIT):
    return pltpu.CompilerParams(dimension_semantics=sem, vmem_limit_bytes=vmem)


def _silu(x):
    return x * jax.nn.sigmoid(x)


HI_HALF = -65536


def _pack_bf16_pairs(x):
    n = x.shape[-1] // 2
    bits = lax.bitcast_convert_type(x.astype(BF16).astype(F32), jnp.int32)
    return jnp.bitwise_or(lax.shift_right_logical(bits[:, :n], 16), jnp.bitwise_and(bits[:, n:], HI_HALF))


def _unpack_bf16_pairs(w):
    lo = lax.bitcast_convert_type(lax.shift_left(w, 16), F32)
    hi = lax.bitcast_convert_type(jnp.bitwise_and(w, HI_HALF), F32)
    return lo, hi


def _ada_norm(x, gain, shift, scale):
    y = x * lax.rsqrt(jnp.mean(x * x, axis=-1, keepdims=True) + EPS)
    return (y * gain) * (1.0 + scale) + shift


def _mod_body(c_ref, w_ref, b_ref, o_ref):
    o_ref[...] = jnp.dot(_silu(c_ref[...]), w_ref[...], preferred_element_type=F32,
                         precision=HIGHEST) + b_ref[...]


def _modulation(c, w3, b3, layer):
    bsz, d = c.shape
    n = w3.shape[-1]
    tn = 1024
    return pl.pallas_call(
        _mod_body,
        out_shape=jax.ShapeDtypeStruct((bsz, n), F32),
        grid=(n // tn,),
        in_specs=[pl.BlockSpec((bsz, d), lambda j: (0, 0)),
                  pl.BlockSpec((None, d, tn), lambda j: (layer, 0, j)),
                  pl.BlockSpec((None, 1, tn), lambda j: (layer, 0, j))],
        out_specs=pl.BlockSpec((bsz, tn), lambda j: (0, j)),
        compiler_params=_cparams(("arbitrary",)),
        name="adaln_mod",
    )(c, w3, b3)


INPROJ_TM = 256
INPROJ_TN = 256
HALO = 8


def _inproj_body(x_ref, gain_ref, sh_ref, sc_ref, w_ref, conv_ref, o_ref, ext, *, tiles_per_seq):
    i = pl.program_id(0)
    tm = x_ref.shape[0]
    hd = A_HEAD_DIM
    n = o_ref.shape[-1]
    tn = INPROJ_TN
    h = _ada_norm(x_ref[...], gain_ref[...], sh_ref[...], sc_ref[...]).astype(BF16)

    @pl.when(i % tiles_per_seq == 0)
    def _():
        ext[0:HALO, :] = jnp.zeros((HALO, ext.shape[-1]), F32)

    def conv_silu_norm(raw, n0):
        cols = slice(n0, n0 + tn)
        ext[HALO:HALO + tm, cols] = raw
        acc = raw * conv_ref[A_CONV - 1:A_CONV, cols]
        for s in range(1, A_CONV):
            acc = acc + ext[HALO - s:HALO - s + tm, cols] * conv_ref[A_CONV - 1 - s:A_CONV - s, cols]
        ext[0:HALO, cols] = raw[tm - HALO:, :]
        y = _silu(acc)
        if n0 >= 2 * A_WIDTH:
            return y
        scale = hd ** -0.5 if n0 < A_WIDTH else 1.0
        heads = []
        for c0 in range(0, tn, hd):
            yh = y[:, c0:c0 + hd]
            heads.append(yh * (lax.rsqrt(jnp.sum(yh * yh, axis=-1, keepdims=True) + EPS) * scale))
        return jnp.concatenate(heads, axis=1)

    matmul = lambda n0: jnp.dot(h, w_ref[:, n0:n0 + tn], preferred_element_type=F32)
    conv_chunks = list(range(0, 3 * A_WIDTH, tn))
    plain_chunks = list(range(3 * A_WIDTH, n, tn))
    every = max(1, len(conv_chunks) // max(1, len(plain_chunks)))
    raw = matmul(conv_chunks[0])
    for idx, n0 in enumerate(conv_chunks):
        nxt = matmul(conv_chunks[idx + 1]) if idx + 1 < len(conv_chunks) else None
        if idx % every == every - 1 and plain_chunks:
            p0 = plain_chunks.pop(0)
            o_ref[:, p0:p0 + tn] = matmul(p0)
        o_ref[:, n0:n0 + tn] = conv_silu_norm(raw, n0)
        raw = nxt
    for p0 in plain_chunks:
        o_ref[:, p0:p0 + tn] = matmul(p0)


def _in_proj(x2, gain, mod4, w_bf, conv_w, seq):
    t, d = x2.shape
    n = w_bf.shape[-1]
    tm = INPROJ_TM
    tps = seq // tm
    return pl.pallas_call(
        functools.partial(_inproj_body, tiles_per_seq=tps),
        out_shape=jax.ShapeDtypeStruct((t, n), F32),
        grid=(t // tm,),
        in_specs=[pl.BlockSpec((tm, d), lambda i: (i, 0)),
                  pl.BlockSpec((1, d), lambda i: (0, 0)),
                  pl.BlockSpec((None, None, 1, d), lambda i: (i // tps, 0, 0, 0)),
                  pl.BlockSpec((None, None, 1, d), lambda i: (i // tps, 1, 0, 0)),
                  pl.BlockSpec((d, n), lambda i: (0, 0)),
                  pl.BlockSpec(conv_w.shape, lambda i: (0, 0))],
        out_specs=pl.BlockSpec((tm, n), lambda i: (i, 0)),
        scratch_shapes=[pltpu.VMEM((HALO + tm, conv_w.shape[-1]), F32)],
        compiler_params=_cparams(("arbitrary",)),
        name="gdn_in_proj",
    )(x2, gain, mod4, mod4, w_bf, conv_w)


GDN_HG = 2
GDN_RB = 256
GDN_SC = 128
GDN_SUB = 4


def _split_bf16(x):
    hi = x.astype(BF16)
    return hi, (x - hi.astype(F32)).astype(BF16)


def _mm3(a_hi, a_lo, b_hi, b_lo, dims=(((1,), (0,)), ((), ()))):
    dg = functools.partial(lax.dot_general, dimension_numbers=dims, preferred_element_type=F32)
    return dg(a_hi, b_hi) + (dg(a_lo, b_hi) + dg(a_hi, b_lo))


def _dot3(a, b, dims):
    return _mm3(*_split_bf16(a), *_split_bf16(b), dims)


def _gdn_body(alog_ref, dtb_ref, q_ref, k_ref, v_ref, z_ref, ab_ref, og_ref,
              o_ref, bs, gcs, mp, bc, qp, op):
    seq = q_ref.shape[0]
    hd = A_HEAD_DIM
    ch = A_CHUNK
    g_idx = pl.program_id(1)

    lane = lax.broadcasted_iota(jnp.int32, (GDN_RB, hd), 1)
    row = lax.broadcasted_iota(jnp.int32, (GDN_RB, hd), 0)
    row_in_chunk = row % ch

    sc_rows = GDN_SC
    per_sc = sc_rows // ch
    rb_per_step = sc_rows * GDN_SUB // GDN_RB
    head_cols = lambda hh: slice(hh * hd, (hh + 1) * hd)

    def prep_stages(j):
        for rbi in range(rb_per_step):
            r = j * rb_per_step + rbi
            start = pl.multiple_of(r * GDN_RB, GDN_RB)
            rows = pl.ds(start, GDN_RB)
            for hh in range(GDN_HG):
                head = g_idx * GDN_HG + hh
                ab = ab_ref[rows, :]
                a_col = jnp.sum(jnp.where(lane == head, ab, 0.0), axis=-1, keepdims=True)
                b_col = jnp.sum(jnp.where(lane == head + A_HEADS, ab, 0.0), axis=-1, keepdims=True)
                bs[hh, rows, :] = jnp.broadcast_to(jax.nn.sigmoid(b_col), (GDN_RB, hd))
                xa = a_col + dtb_ref[0, head]
                softplus = jnp.maximum(xa, 0.0) + jnp.log1p(jnp.exp(-jnp.abs(xa)))
                g = jnp.broadcast_to(-jnp.exp(alog_ref[0, head]) * softplus, (GDN_RB, hd))
                s = 1
                while s < ch:
                    g = g + jnp.where(row_in_chunk >= s, pltpu.roll(g, s, 0), 0.0)
                    s *= 2
                gcs[hh, rows, :] = g
                yield

    ri = lax.broadcasted_iota(jnp.int32, (sc_rows, sc_rows), 0)
    cj = lax.broadcasted_iota(jnp.int32, (sc_rows, sc_rows), 1)
    same_chunk = (ri // ch) == (cj // ch)
    incl = jnp.logical_and(same_chunk, ri >= cj)
    strict = jnp.logical_and(same_chunk, ri > cj)
    eye = jnp.where(ri == cj, 1.0, 0.0)
    chunk_of_col = lax.broadcasted_iota(jnp.int32, (hd, sc_rows), 1) // ch
    lanes_nt = (((1,), (1,)), ((), ()))
    plain = (((1,), (0,)), ((), ()))

    chunks_per_step = GDN_SUB * per_sc
    n_steps = seq // (sc_rows * GDN_SUB)

    def local_stages(i):
        chains = [(i * GDN_SUB + sub, hh) for sub in range(GDN_SUB) for hh in range(GDN_HG)]
        rows_of = lambda blk: pl.ds(pl.multiple_of(blk * sc_rows, sc_rows), sc_rows)

        def start(blk, hh):
            rows = rows_of(blk)
            k = k_ref[rows, head_cols(hh)]
            gc = gcs[hh, rows, :]
            decay = jnp.where(incl, jnp.exp(jnp.minimum(gc - gc.T[0:1, :], 0.0)), 0.0)
            k_bf = k.astype(BF16)
            kk = lax.dot_general((k * bs[hh, rows, :]).astype(BF16), k_bf, lanes_nt, preferred_element_type=F32)
            qk = lax.dot_general(q_ref[rows, head_cols(hh)].astype(BF16), k_bf, lanes_nt, preferred_element_type=F32)
            return jnp.where(strict, -(kk * decay), 0.0), (qk * decay).astype(BF16)

        started = [start(*c) for c in chains]
        yield
        qks = [s[1] for s in started]
        invs = [eye + s[0] for s in started]
        pws = [_dot3(s[0], s[0], plain) for s in started]
        yield
        for step in range(1, 6):
            for n in range(len(chains)):
                pw_hi, pw_lo = _split_bf16(pws[n])
                inv_hi, inv_lo = _split_bf16(invs[n])
                if step < 5:
                    prod = _mm3(jnp.concatenate([pw_hi, inv_hi], axis=0),
                                jnp.concatenate([pw_lo, inv_lo], axis=0), pw_hi, pw_lo)
                    pws[n] = prod[:sc_rows]
                    invs[n] = invs[n] + prod[sc_rows:]
                else:
                    invs[n] = invs[n] + _mm3(inv_hi, inv_lo, pw_hi, pw_lo)
            yield

        def solve(n, blk, hh):
            rows = rows_of(blk)
            beta = bs[hh, rows, :]
            rhs = jnp.concatenate([v_ref[rows, head_cols(hh)] * beta, k_ref[rows, head_cols(hh)] * beta * jnp.exp(gcs[hh, rows, :])], axis=1)
            return _dot3(invs[n], rhs, plain).astype(BF16)

        uws = [solve(n, *c) for n, c in enumerate(chains)]
        yield

        def finish(n, blk, hh):
            rows = rows_of(blk)
            k = k_ref[rows, head_cols(hh)]
            gc = gcs[hh, rows, :]
            res = jnp.dot(qks[n], uws[n], preferred_element_type=F32)
            op[hh, rows, :] = res[:, :hd]
            qp[hh, rows, :] = (q_ref[rows, head_cols(hh)] * jnp.exp(gc) - res[:, hd:]).astype(BF16)
            gl = jnp.concatenate(
                [jnp.broadcast_to(gc[(j + 1) * ch - 1:(j + 1) * ch, :], (ch, hd)) for j in range(per_sc)], axis=0)
            kt_t = (k * jnp.exp(gl - gc)).T
            for j in range(per_sc):
                kt_j = jnp.where(chunk_of_col == j, kt_t, 0.0).astype(BF16)
                bm = jnp.dot(kt_j, uws[n], preferred_element_type=F32)
                bc[hh, blk * per_sc + j] = bm[:, :hd]
                mp[hh, blk * per_sc + j] = bm[:, hd:].astype(BF16)

        for n, c in enumerate(chains):
            finish(n, *c)

    def scan_stages(j, states):
        for cc in range(chunks_per_step):
            c = j * chunks_per_step + cc
            r0 = pl.multiple_of(c * ch, ch)
            rows = pl.ds(r0, ch)
            for hh in range(GDN_HG):
                col0 = hh * hd
                state = states[hh]
                s_bf = state.astype(BF16)
                o = jnp.dot(qp[hh, rows, :], s_bf, preferred_element_type=F32) + op[hh, rows, :]
                g_tot = jnp.exp(gcs[hh, pl.ds(r0 + ch - 1, 1), :])
                states[hh] = state * g_tot - jnp.dot(mp[hh, c], s_bf, preferred_element_type=F32) + bc[hh, c]
                on = o * lax.rsqrt(jnp.mean(o * o, axis=-1, keepdims=True) + EPS) * og_ref[...]
                z = z_ref[rows, col0:col0 + hd]
                o_ref[rows, col0:col0 + hd] = (on * _silu(z)).astype(BF16)
            yield

    def interleave(*gens):
        live = list(gens)
        while live:
            for g in list(live):
                try:
                    next(g)
                except StopIteration:
                    live.remove(g)

    assert n_steps >= 2
    interleave(prep_stages(0))
    interleave(local_stages(0), prep_stages(1))

    def pipelined(i, states):
        states = list(states)
        interleave(local_stages(i), scan_stages(i - 1, states), prep_stages(i + 1))
        return tuple(states)

    states = list(lax.fori_loop(1, n_steps - 1, pipelined,
                                tuple(jnp.zeros((hd, hd), F32) for _ in range(GDN_HG))))
    interleave(local_stages(n_steps - 1), scan_stages(n_steps - 2, states))
    interleave(scan_stages(n_steps - 1, states))


def _gdn(proj, a_log, dt_bias, out_gain, bsz, seq):
    t = proj.shape[0]
    hd = A_HEAD_DIM
    wb = hd * GDN_HG
    ng = A_HEADS // GDN_HG
    per = A_WIDTH // wb
    n_chunks = seq // A_CHUNK
    smem = pl.BlockSpec(memory_space=pltpu.SMEM)
    seq_spec = lambda off: pl.BlockSpec((seq, wb), lambda b, g, off=off: (b, off + g))
    sc = lambda dt=F32: pltpu.VMEM((GDN_HG, seq, hd), dt)
    return pl.pallas_call(
        _gdn_body,
        out_shape=jax.ShapeDtypeStruct((t, A_WIDTH), BF16),
        grid=(bsz, ng),
        in_specs=[smem, smem,
                  seq_spec(0), seq_spec(per), seq_spec(2 * per), seq_spec(3 * per),
                  pl.BlockSpec((seq, LANES), lambda b, g: (b, 4 * A_WIDTH // LANES)),
                  pl.BlockSpec((1, hd), lambda b, g: (0, 0))],
        out_specs=pl.BlockSpec((seq, wb), lambda b, g: (b, g)),
        scratch_shapes=[sc(), sc(),
                        pltpu.VMEM((GDN_HG, n_chunks, hd, hd), BF16),
                        pltpu.VMEM((GDN_HG, n_chunks, hd, hd), F32),
                        sc(BF16), sc()],
        compiler_params=_cparams(("parallel", "parallel")),
        name="gdn_core",
    )(a_log, dt_bias, proj, proj, proj, proj, proj, out_gain)


def _oproj_body(a_ref, w_ref, x_ref, g_ref, o_ref):
    y = jnp.dot(a_ref[...], w_ref[...], preferred_element_type=F32)
    o_ref[...] = x_ref[...] + g_ref[...] * y


def _out_proj_residual(a_bf, w_bf, x2, mod4, gate_slot, seq):
    t, d = x2.shape
    kdim = a_bf.shape[-1]
    tm = 512
    tps = seq // tm
    return pl.pallas_call(
        _oproj_body,
        out_shape=jax.ShapeDtypeStruct((t, d), F32),
        grid=(t // tm,),
        in_specs=[pl.BlockSpec((tm, kdim), lambda i: (i, 0)),
                  pl.BlockSpec((kdim, d), lambda i: (0, 0)),
                  pl.BlockSpec((tm, d), lambda i: (i, 0)),
                  pl.BlockSpec((None, None, 1, d), lambda i: (i // tps, gate_slot, 0, 0))],
        out_specs=pl.BlockSpec((tm, d), lambda i: (i, 0)),
        compiler_params=_cparams(("parallel",)),
        name="out_proj_residual",
    )(a_bf, w_bf, x2, mod4)


ROUTER_TM = 512


K_ROWS = 8


def _router_body(x_ref, gain_ref, sh_ref, sc_ref, rw_ref, rb_ref, h_ref, idx_ref, gate_ref, rank_ref,
                 cnt_ref, carry):
    i = pl.program_id(0)

    @pl.when(i == 0)
    def _():
        carry[...] = jnp.zeros_like(carry)

    tm = x_ref.shape[0]
    h = _ada_norm(x_ref[...], gain_ref[...], sh_ref[...], sc_ref[...])
    h_ref[...] = _pack_bf16_pairs(h)
    logits = _dot3(h, rw_ref[...], (((1,), (0,)), ((), ()))) + rb_ref[...]
    work = logits.T[:N_EXPERTS, :]
    expert = lax.broadcasted_iota(jnp.int32, (N_EXPERTS, tm), 0)
    tops, sels = [], []
    for _ in range(TOP_K):
        m = jnp.max(work, axis=0, keepdims=True)
        sel = jnp.min(jnp.where(work == m, expert, N_EXPERTS), axis=0, keepdims=True)
        work = jnp.where(expert == sel, NEG_INF, work)
        tops.append(m)
        sels.append(sel)
    exps = [jnp.exp(m - tops[0]) for m in tops]
    denom = exps[0] + exps[1] + exps[2] + exps[3]
    onehot = jnp.zeros((N_EXPERTS, tm), F32)
    for sel in sels:
        onehot = onehot + jnp.where(expert == sel, 1.0, 0.0)
    ti = lax.broadcasted_iota(jnp.int32, (tm, tm), 0)
    tj = lax.broadcasted_iota(jnp.int32, (tm, tm), 1)
    earlier = (ti < tj).astype(BF16)
    before = jnp.dot(onehot.astype(BF16), earlier, preferred_element_type=F32) + carry[:, 0:1]
    ranks = [jnp.sum(jnp.where(expert == sel, before, 0.0), axis=0, keepdims=True) for sel in sels]
    pad_i = jnp.zeros((K_ROWS - TOP_K, tm), jnp.int32)
    idx_ref[...] = jnp.concatenate(sels + [pad_i], axis=0)
    gate_ref[...] = jnp.concatenate([e / denom for e in exps] + [pad_i.astype(F32)], axis=0)
    rank_ref[...] = jnp.concatenate([r.astype(jnp.int32) for r in ranks] + [pad_i], axis=0)
    carry[...] = carry[...] + jnp.sum(onehot, axis=1, keepdims=True)
    cnt_ref[...] = carry[...]


def _router(x2, gain, mod4, sh_slot, sc_slot, rw_pad, rb_pad, seq):
    t, d = x2.shape
    tm = ROUTER_TM
    tps = seq // tm
    tok = lambda dt: jax.ShapeDtypeStruct((K_ROWS, t), dt)
    tok_spec = pl.BlockSpec((K_ROWS, tm), lambda i: (0, i))
    cnt_spec = pl.BlockSpec((N_EXPERTS, LANES), lambda i: (0, 0))
    return pl.pallas_call(
        _router_body,
        out_shape=(jax.ShapeDtypeStruct((t, d // 2), jnp.int32), tok(jnp.int32), tok(F32), tok(jnp.int32),
                   jax.ShapeDtypeStruct((N_EXPERTS, LANES), F32)),
        grid=(t // tm,),
        in_specs=[pl.BlockSpec((tm, d), lambda i: (i, 0)),
                  pl.BlockSpec((1, d), lambda i: (0, 0)),
                  pl.BlockSpec((None, None, 1, d), lambda i: (i // tps, sh_slot, 0, 0)),
                  pl.BlockSpec((None, None, 1, d), lambda i: (i // tps, sc_slot, 0, 0)),
                  pl.BlockSpec((d, LANES), lambda i: (0, 0)),
                  pl.BlockSpec((1, LANES), lambda i: (0, 0))],
        out_specs=(pl.BlockSpec((tm, d // 2), lambda i: (i, 0)), tok_spec, tok_spec, tok_spec, cnt_spec),
        scratch_shapes=[pltpu.VMEM((N_EXPERTS, LANES), F32)],
        compiler_params=_cparams(("arbitrary",)),
        name="moe_router",
    )(x2, gain, mod4, mod4, rw_pad, rb_pad)


SC_LANES = 16
SC_INDEX_CHUNK = 8192


def _sc_row_tokens(dest_flat, n_rows, n_tok):
    n_assign = dest_flat.shape[0]
    assert n_rows % SC_LANES == 0 and n_assign % SC_INDEX_CHUNK == 0
    n_cores = plsc.get_sparse_core_info().num_cores
    mesh = plsc.VectorSubcoreMesh(core_axis_name="c", subcore_axis_name="s")

    @functools.partial(
        pl.kernel, mesh=mesh,
        out_type=jax.ShapeDtypeStruct((n_rows,), jnp.int32),
        scratch_types=[pltpu.VMEM((n_rows,), jnp.int32), pltpu.VMEM((SC_INDEX_CHUNK,), jnp.int32)],
        compiler_params=dataclasses.replace(pltpu.CompilerParams(), needs_layout_passes=False),
    )
    def row_token_kernel(dest_hbm, out_hbm, rt_v, d_v):
        wid = lax.axis_index("s") * n_cores + lax.axis_index("c")

        @pl.when(wid == 0)
        def _():
            lanes = lax.iota(jnp.int32, SC_LANES)

            @pl.loop(0, n_rows // SC_LANES)
            def _(i):
                rt_v[pl.ds(i * SC_LANES, SC_LANES)] = lax.rem(lanes + i * SC_LANES, n_tok)

            @pl.loop(0, n_assign // SC_INDEX_CHUNK)
            def _(c):
                pltpu.sync_copy(dest_hbm.at[pl.ds(c * SC_INDEX_CHUNK, SC_INDEX_CHUNK)], d_v)

                @pl.loop(0, SC_INDEX_CHUNK // SC_LANES)
                def _(i):
                    idx = d_v[pl.ds(i * SC_LANES, SC_LANES)]
                    tok = lax.rem(lanes + (c * SC_INDEX_CHUNK + i * SC_LANES), n_tok)
                    plsc.store_scatter(rt_v, [idx], tok)

            pltpu.sync_copy(rt_v, out_hbm)

    return row_token_kernel(dest_flat)


def _ffn_body(be_ref, nu_ref, nxt_ref, slot_ref, x_ref, uw_hbm, ub_ref, dw_hbm, db_ref, *rest, layer):
    y_ref, uw_f32, dw_f32, uw_bf, dw_bf, sems = rest[-6:]
    i = pl.program_id(0)
    first = jnp.logical_and(jnp.logical_or(i == 0, be_ref[i] != be_ref[jnp.maximum(i - 1, 0)]), i < nu_ref[0])
    active = i < nu_ref[0]
    slot = slot_ref[i]

    def weight_copies(expert, s):
        return (pltpu.make_async_copy(uw_hbm.at[layer, expert], uw_f32.at[s], sems.at[s, 0]),
                pltpu.make_async_copy(dw_hbm.at[layer, expert], dw_f32.at[s], sems.at[s, 1]))

    @pl.when(jnp.logical_and(i == 0, active))
    def _():
        for cp in weight_copies(be_ref[0], 0):
            cp.start()

    @pl.when(first)
    def _():
        for cp in weight_copies(be_ref[i], slot):
            cp.wait()

        @pl.when(nxt_ref[i] >= 0)
        def _():
            for cp in weight_copies(nxt_ref[i], 1 - slot):
                cp.start()

        rows = 128
        for r0 in range(0, uw_bf.shape[0], rows):
            uw_bf[r0:r0 + rows, :] = uw_f32[slot, r0:r0 + rows, :].astype(BF16)
        for r0 in range(0, dw_bf.shape[0], rows):
            dw_bf[r0:r0 + rows, :] = dw_f32[slot, r0:r0 + rows, :].astype(BF16)

    @pl.when(active)
    def _():
        x = jnp.concatenate(_unpack_bf16_pairs(x_ref[...]), axis=1).astype(BF16)
        gu = jnp.dot(x, uw_bf[...], preferred_element_type=F32) + ub_ref[...]
        gate = jnp.minimum(gu[:, :D_FF], SWIGLU_LIMIT)
        lin = jnp.clip(gu[:, D_FF:], -SWIGLU_LIMIT, SWIGLU_LIMIT)
        act = gate * jax.nn.sigmoid(SWIGLU_ALPHA * gate) * (lin + 1.0)
        y = jnp.dot(act.astype(BF16), dw_bf[...], preferred_element_type=F32) + db_ref[...]
        y_ref[...] = _pack_bf16_pairs(y)

    @pl.when(jnp.logical_not(active))
    def _():
        y_ref[...] = jnp.zeros_like(y_ref)


def _expert_ffn(xb, blk_expert, n_used, up_w, up_b4, down_w, down_b4, layer, n_rows_total, first_blk, y_prev):
    n_rows, dp = xb.shape
    d = 2 * dp
    bm = MOE_ROWS
    n_blk = n_rows // bm
    f2 = up_w.shape[-1]
    blk = jnp.arange(n_blk, dtype=jnp.int32)
    first = jnp.logical_and(jnp.concatenate([jnp.ones((1,), bool), blk_expert[1:] != blk_expert[:-1]]),
                            blk < n_used[0])
    slot = ((jnp.cumsum(first.astype(jnp.int32)) - 1) % 2).astype(jnp.int32)
    later_first = jnp.where(first, blk, n_blk)[::-1]
    next_first = jnp.concatenate([lax.cummin(later_first)[::-1][1:], jnp.full((1,), n_blk, jnp.int32)])
    nxt = jnp.where(jnp.logical_and(first, next_first < n_blk),
                    blk_expert[jnp.minimum(next_first, n_blk - 1)], -1).astype(jnp.int32)
    grid_spec = pltpu.PrefetchScalarGridSpec(
        num_scalar_prefetch=4,
        grid=(n_blk,),
        in_specs=[pl.BlockSpec((bm, dp), lambda i, be, nu, nx, sl: (i, 0)),
                  pl.BlockSpec(memory_space=pl.ANY),
                  pl.BlockSpec((None, None, 1, f2), lambda i, be, nu, nx, sl: (layer, be[i], 0, 0)),
                  pl.BlockSpec(memory_space=pl.ANY),
                  pl.BlockSpec((None, None, 1, d), lambda i, be, nu, nx, sl: (layer, be[i], 0, 0))]
                 + ([] if y_prev is None else [pl.BlockSpec(memory_space=pl.ANY)]),
        out_specs=pl.BlockSpec((bm, dp), lambda i, be, nu, nx, sl: (i + first_blk, 0)),
        scratch_shapes=[pltpu.VMEM((2, d, f2), F32), pltpu.VMEM((2, f2 // 2, d), F32),
                        pltpu.VMEM((d, f2), BF16), pltpu.VMEM((f2 // 2, d), BF16),
                        pltpu.SemaphoreType.DMA((2, 2))],
    )
    args = (blk_expert, n_used, nxt, slot, xb, up_w, up_b4, down_w, down_b4)
    return pl.pallas_call(
        functools.partial(_ffn_body, layer=layer),
        out_shape=jax.ShapeDtypeStruct((n_rows_total, dp), jnp.int32),
        grid_spec=grid_spec,
        input_output_aliases={} if y_prev is None else {len(args): 0},
        compiler_params=_cparams(("arbitrary",)),
        name="moe_expert_ffn",
    )(*args, *(() if y_prev is None else (y_prev,)))


SC_GATHER_ROWS = 64


def _sc_gather_rows(table, idx):
    n, d = idx.shape[0], table.shape[1]
    info = plsc.get_sparse_core_info()
    n_cores, n_sub = info.num_cores, info.num_subcores
    n_workers = n_cores * n_sub
    chunk = SC_GATHER_ROWS
    per_worker = n // n_workers
    n_chunks = per_worker // chunk
    assert n_chunks * chunk * n_workers == n and n_chunks % 2 == 0
    mesh = plsc.VectorSubcoreMesh(core_axis_name="c", subcore_axis_name="s")

    @functools.partial(
        pl.kernel, mesh=mesh,
        out_type=jax.ShapeDtypeStruct((n, d), table.dtype),
        scratch_types=[pltpu.VMEM((n_chunks, chunk), jnp.int32), pltpu.VMEM((2, chunk, d), table.dtype),
                       pltpu.SemaphoreType.DMA((2,))],
    )
    def gather_kernel(table_hbm, idx_hbm, out_hbm, idx_v, rows_v, sems):
        wid = lax.axis_index("s") * n_cores + lax.axis_index("c")
        base = wid * per_worker
        pltpu.sync_copy(idx_hbm.at[wid], idx_v)

        def gather(ci, slot):
            return pltpu.make_async_copy(table_hbm.at[idx_v.at[ci]], rows_v.at[slot], sems.at[slot])

        gather(0, 0).start()

        @pl.loop(0, n_chunks, step=2)
        def _(c0):
            for slot in range(2):
                ci = c0 + slot
                gather(ci, slot).wait()

                @pl.when(ci + 1 < n_chunks)
                def _():
                    gather(ci + 1, 1 - slot).start()

                pltpu.sync_copy(rows_v.at[slot], out_hbm.at[pl.ds(base + ci * chunk, chunk)])

    return gather_kernel(table, idx.reshape(n_workers, n_chunks, chunk))


COMBINE_TM = 256


def _combine_body(gate_ref, x_ref, g_ref, y_ref, o_ref):
    gates = gate_ref[...]
    acc_lo = acc_hi = None
    for k in range(TOP_K):
        lo, hi = _unpack_bf16_pairs(y_ref[k])
        gk = gates[:, k:k + 1]
        acc_lo = gk * lo if acc_lo is None else acc_lo + gk * lo
        acc_hi = gk * hi if acc_hi is None else acc_hi + gk * hi
    o_ref[...] = x_ref[...] + g_ref[...] * jnp.concatenate([acc_lo, acc_hi], axis=1)


def _combine(y_kt, gates, x2, mod4, gate_slot, seq):
    t, d = x2.shape
    tm = COMBINE_TM
    tps = seq // tm
    return pl.pallas_call(
        _combine_body,
        out_shape=jax.ShapeDtypeStruct((t, d), F32),
        grid=(t // tm,),
        in_specs=[pl.BlockSpec((tm, gates.shape[1]), lambda i: (i, 0)),
                  pl.BlockSpec((tm, d), lambda i: (i, 0)),
                  pl.BlockSpec((None, None, 1, d), lambda i: (i // tps, gate_slot, 0, 0)),
                  pl.BlockSpec((TOP_K, tm, d // 2), lambda i: (0, i, 0))],
        out_specs=pl.BlockSpec((tm, d), lambda i: (i, 0)),
        compiler_params=_cparams(("parallel",)),
        name="moe_combine",
    )(gates, x2, mod4, y_kt)


def _moe_layer(x2, gain, mod4, router_w, router_b, up_w, up_b, down_w, down_b, layer, seq):
    t, d = x2.shape
    rw_pad = jnp.pad(router_w[layer], ((0, 0), (0, LANES - N_EXPERTS)))
    rb_pad = jnp.pad(router_b[layer], (0, LANES - N_EXPERTS)).reshape(1, LANES)
    h2, idx, gates, rank, counts = _router(x2, gain, mod4, 3, 4, rw_pad, rb_pad, seq)

    bm = MOE_ROWS
    counts = counts[:, 0].astype(jnp.int32)
    padded = (counts + bm - 1) // bm * bm
    p_ends = jnp.cumsum(padded)
    p_starts = p_ends - padded
    n_rows = t * TOP_K + N_EXPERTS * bm
    n_blk = n_rows // bm
    onehot = idx[:TOP_K, :, None] == jnp.arange(N_EXPERTS, dtype=jnp.int32)[None, None, :]
    dest = jnp.sum(jnp.where(onehot, p_starts[None, None, :], 0), axis=-1) + rank[:TOP_K]
    dest_flat = dest.reshape(TOP_K * t)
    blk_start = jnp.arange(n_blk, dtype=jnp.int32) * bm
    blk_expert = jnp.sum((blk_start[:, None] >= p_ends[None, :]).astype(jnp.int32), axis=-1)
    blk_expert = jnp.minimum(blk_expert, N_EXPERTS - 1).astype(jnp.int32)
    n_used = (p_ends[-1:] // bm).astype(jnp.int32)

    row_token = _sc_row_tokens(dest_flat, n_rows, t)
    f2 = up_w.shape[-1]
    up_b4 = up_b.reshape(up_b.shape[0], N_EXPERTS, 1, f2)
    down_b4 = down_b.reshape(down_b.shape[0], N_EXPERTS, 1, d)
    part_blk = n_blk // MOE_PARTS
    xb_parts = [_sc_gather_rows(h2, row_token[p * part_blk * bm:(p + 1) * part_blk * bm]) for p in range(MOE_PARTS)]
    y_rows = None
    for p, xb in enumerate(xb_parts):
        y_rows = _expert_ffn(xb, blk_expert[p * part_blk:(p + 1) * part_blk],
                             jnp.clip(n_used - p * part_blk, 0, part_blk), up_w, up_b4, down_w, down_b4,
                             layer, n_rows, p * part_blk, y_rows)
    y_kt = _sc_gather_rows(y_rows, dest_flat).reshape(TOP_K, t, d // 2)
    return _combine(y_kt, gates.T, x2, mod4, 5, seq)


QKV_TM = 256


def _rope_tables(pos_row, invf_col):
    tm = pos_row.shape[1]
    half = ROPE_DIM // 2
    ang = invf_col * pos_row
    expand = (lax.broadcasted_iota(jnp.int32, (half, LANES), 1) % half
              == lax.broadcasted_iota(jnp.int32, (half, LANES), 0)).astype(BF16)

    def to_lanes(x):
        hi = x.astype(BF16)
        rest = x - hi.astype(F32)
        mid = rest.astype(BF16)
        parts = (hi, mid, (rest - mid.astype(F32)).astype(BF16))
        return sum(lax.dot_general(p, expand, (((0,), (0,)), ((), ())), preferred_element_type=F32) for p in parts)

    c = to_lanes(jnp.cos(ang))
    s = to_lanes(jnp.sin(ang))
    d = lax.broadcasted_iota(jnp.int32, (tm, LANES), 1) % B_HEAD_DIM
    cos_t = jnp.where(d < ROPE_DIM, c, 1.0)
    sin_lo = jnp.where(d < half, -s, 0.0)
    sin_hi = jnp.where(jnp.logical_and(d >= half, d < ROPE_DIM), s, 0.0)
    return cos_t, sin_lo, sin_hi


def _head_norm_rope(x, gain_row, tables):
    cos_t, sin_lo, sin_hi = tables
    same_head = (lax.broadcasted_iota(jnp.int32, (LANES, LANES), 0) // B_HEAD_DIM
                 == lax.broadcasted_iota(jnp.int32, (LANES, LANES), 1) // B_HEAD_DIM).astype(BF16)
    sq_hi, sq_lo = _split_bf16(x * x)
    ss = (jnp.dot(sq_hi, same_head, preferred_element_type=F32)
          + jnp.dot(sq_lo, same_head, preferred_element_type=F32))
    xn = x * lax.rsqrt(ss * (1.0 / B_HEAD_DIM) + EPS) * gain_row
    half = ROPE_DIM // 2
    return (xn * cos_t + pltpu.roll(xn, LANES - half, 1) * sin_lo + pltpu.roll(xn, half, 1) * sin_hi)


def _qkv_body(x_ref, pos_ref, gq_ref, shq_ref, scq_ref, gkv_ref, shkv_ref, sckv_ref, wq_ref, wkv_ref,
              qg_ref, kg_ref, invf_ref, q_ref, k_ref, v_ref):
    x = x_ref[...]
    y = x * lax.rsqrt(jnp.mean(x * x, axis=-1, keepdims=True) + EPS)
    hq = ((y * gq_ref[...]) * (1.0 + scq_ref[...]) + shq_ref[...]).astype(BF16)
    hkv = ((y * gkv_ref[...]) * (1.0 + sckv_ref[...]) + shkv_ref[...]).astype(BF16)
    tables = _rope_tables(pos_ref[...].astype(F32), invf_ref[...])
    kv = jnp.dot(hkv, wkv_ref[...], preferred_element_type=F32)
    k_ref[...] = _head_norm_rope(kv[:, :LANES], kg_ref[...], tables).astype(BF16)
    v_ref[...] = kv[:, LANES:].astype(BF16)
    q = jnp.dot(hq, wq_ref[...], preferred_element_type=F32)
    scale = B_HEAD_DIM ** -0.5
    for p in range(q.shape[-1] // LANES):
        qp = _head_norm_rope(q[:, p * LANES:(p + 1) * LANES], qg_ref[...], tables)
        q_ref[:, p * LANES:(p + 1) * LANES] = (qp * scale).astype(BF16)


def _qkv(x2, pos2, gq, mod4, gkv, kvmod4, wq_bf, wkv_bf, qg2, kg2, invf, seq):
    t, d = x2.shape
    tm = QKV_TM
    tps = seq // tm
    nq = wq_bf.shape[-1]
    nkv = wkv_bf.shape[-1]
    row = lambda n: pl.BlockSpec((1, n), lambda i: (0, 0))
    modspec = lambda slot: pl.BlockSpec((None, None, 1, d), lambda i, slot=slot: (i // tps, slot, 0, 0))
    return pl.pallas_call(
        _qkv_body,
        out_shape=(jax.ShapeDtypeStruct((t, nq), BF16), jax.ShapeDtypeStruct((t, LANES), BF16),
                   jax.ShapeDtypeStruct((t, LANES), BF16)),
        grid=(t // tm,),
        in_specs=[pl.BlockSpec((tm, d), lambda i: (i, 0)),
                  pl.BlockSpec((None, 1, tm), lambda i: (i, 0, 0)),
                  row(d), modspec(0), modspec(1),
                  row(d), modspec(0), modspec(1),
                  pl.BlockSpec((d, nq), lambda i: (0, 0)),
                  pl.BlockSpec((d, nkv), lambda i: (0, 0)),
                  row(LANES), row(LANES), pl.BlockSpec(invf.shape, lambda i: (0, 0))],
        out_specs=(pl.BlockSpec((tm, nq), lambda i: (i, 0)),
                   pl.BlockSpec((tm, LANES), lambda i: (i, 0)),
                   pl.BlockSpec((tm, LANES), lambda i: (i, 0))),
        compiler_params=_cparams(("parallel",)),
        name="swa_qkv_proj",
    )(x2, pos2.reshape(t // tm, 1, tm), gq, mod4, mod4, gkv, kvmod4, kvmod4, wq_bf, wkv_bf, qg2, kg2, invf)


def _attn_body(sink_ref, q_ref, kc_ref, kp_ref, vc_ref, vp_ref, w_ref, x_ref, g_ref, o_ref):
    n = pl.program_id(1)
    w = WINDOW
    from_prev = (lax.broadcasted_iota(jnp.int32, (w, w), 1) > lax.broadcasted_iota(jnp.int32, (w, w), 0))
    prev_fill = jnp.where(n > 0, 0.0, NEG_INF)
    hd = B_HEAD_DIM
    lane = lax.broadcasted_iota(jnp.int32, (2 * w, LANES), 1)
    first = lane < hd
    kfull = jnp.concatenate([kp_ref[...], kc_ref[...]], axis=0).astype(F32)
    vfull = jnp.concatenate([vp_ref[...], vc_ref[...]], axis=0).astype(F32)
    placed = {}
    for g in range(B_KV_HEADS):
        for name, full in (("k", kfull), ("v", vfull)):
            swapped = pltpu.roll(full, hd, 1)
            own_first = full if g == 0 else swapped
            own_second = swapped if g == 0 else full
            placed[name, g, 0] = jnp.where(first, own_first, 0.0).astype(BF16)
            placed[name, g, 1] = jnp.where(first, 0.0, own_second).astype(BF16)
    pairs = B_Q_HEADS // 2
    heads = [(p, side) for p in range(pairs) for side in range(2)]
    scores = []
    for p, side in heads:
        g = (2 * p) // B_GROUP
        q_pair = q_ref[:, p * LANES:(p + 1) * LANES]
        s = lax.dot_general(q_pair, placed["k", g, side], (((1,), (1,)), ((), ())), preferred_element_type=F32)
        scores.append(jnp.where(from_prev, s[:, :w] + prev_fill, s[:, w:]))
    probs, denoms = [], []
    for (p, side), s in zip(heads, scores):
        sink = sink_ref[0, 2 * p + side]
        m = jnp.maximum(jnp.max(s, axis=-1, keepdims=True), sink)
        e = jnp.exp(s - m)
        denoms.append(jnp.sum(e, axis=-1, keepdims=True) + jnp.exp(sink - m))
        probs.append(jnp.concatenate([jnp.where(from_prev, e, 0.0), jnp.where(from_prev, 0.0, e)],
                                     axis=1).astype(BF16))
    outs = []
    for (p, side), pr, den in zip(heads, probs, denoms):
        g = (2 * p) // B_GROUP
        outs.append(jnp.dot(pr, placed["v", g, side], preferred_element_type=F32) / den)
    attn = jnp.concatenate([(outs[2 * p] + outs[2 * p + 1]).astype(BF16) for p in range(pairs)], axis=1)
    o_ref[...] = x_ref[...] + g_ref[...] * jnp.dot(attn, w_ref[...], preferred_element_type=F32)


def _attention_residual(q, k, v, sinks2, w_out_bf, x2, mod4, gate_slot, bsz, seq):
    t, nq = q.shape
    d = x2.shape[-1]
    w = WINDOW
    nb = seq // w
    cur = lambda b, n: (b * nb + n, 0)
    prev = lambda b, n: (b * nb + jnp.maximum(n - 1, 0), 0)
    return pl.pallas_call(
        _attn_body,
        out_shape=jax.ShapeDtypeStruct((t, d), F32),
        grid=(bsz, nb),
        in_specs=[pl.BlockSpec(memory_space=pltpu.SMEM),
                  pl.BlockSpec((w, nq), cur),
                  pl.BlockSpec((w, LANES), cur), pl.BlockSpec((w, LANES), prev),
                  pl.BlockSpec((w, LANES), cur), pl.BlockSpec((w, LANES), prev),
                  pl.BlockSpec((nq, d), lambda b, n: (0, 0)),
                  pl.BlockSpec((w, d), cur),
                  pl.BlockSpec((None, None, 1, d), lambda b, n: (b, gate_slot, 0, 0))],
        out_specs=pl.BlockSpec((w, d), cur),
        compiler_params=_cparams(("parallel", "parallel")),
        name="swa_sink_attention",
    )(sinks2, q, k, k, v, v, w_out_bf, x2, mod4)


def kernel(x, c, positions, ada_w, ada_b, norm_gain, a_w_in, a_conv, a_log, a_dt_bias, a_out_gain, a_w_out,
           kv_ada_w, kv_ada_b, kv_norm_gain, kv_w, k_norm_gain, b_w_q, q_norm_gain, b_sinks, b_w_out,
           router_w, router_b, up_w, up_b, down_w, down_b):
    bsz, seq, d = x.shape
    t = bsz * seq
    depth = ada_w.shape[0]
    x2 = x.reshape(t, d)
    ada_b3 = ada_b.reshape(depth, 1, 6 * d)

    for layer in range(depth):
        mod4 = _modulation(c, ada_w, ada_b3, layer).reshape(bsz, 6, 1, d)
        gain1 = norm_gain[layer, 0].reshape(1, d)
        gain2 = norm_gain[layer, 1].reshape(1, d)
        if layer < N_A_LAYERS:
            w_in = jnp.pad(a_w_in[layer].astype(BF16), ((0, 0), (0, A_PROJ_PAD - a_w_in.shape[-1])))
            proj = _in_proj(x2, gain1, mod4, w_in, a_conv[layer], seq)
            o = _gdn(proj, a_log[layer].reshape(1, A_HEADS),
                     a_dt_bias[layer].reshape(1, A_HEADS), a_out_gain[layer].reshape(1, A_HEAD_DIM),
                     bsz, seq)
            x2 = _out_proj_residual(o, a_w_out[layer].astype(BF16), x2, mod4, 2, seq)
        else:
            j = layer - N_A_LAYERS
            kvmod4 = _modulation(c, kv_ada_w.reshape(1, d, 2 * d), kv_ada_b.reshape(1, 1, 2 * d), 0)
            kvmod4 = kvmod4.reshape(bsz, 2, 1, d)
            inv_freq = ROPE_THETA ** (-np.arange(0, ROPE_DIM, 2, dtype=np.float32) / ROPE_DIM)
            invf = jnp.asarray(inv_freq.astype(np.float32).reshape(ROPE_DIM // 2, 1))
            q, k, v = _qkv(x2, positions.reshape(t), gain1, mod4, kv_norm_gain.reshape(1, d), kvmod4,
                           b_w_q[j].astype(BF16), kv_w.astype(BF16),
                           jnp.tile(q_norm_gain[j], 2).reshape(1, LANES),
                           jnp.tile(k_norm_gain, 2).reshape(1, LANES), invf, seq)
            x2 = _attention_residual(q, k, v, b_sinks[j].reshape(1, B_Q_HEADS), b_w_out[j].astype(BF16),
                                     x2, mod4, 2, bsz, seq)
        x2 = _moe_layer(x2, gain2, mod4, router_w, router_b, up_w, up_b, down_w, down_b, layer, seq)
    return x2.reshape(bsz, seq, d)
```

```python
import dataclasses
import functools

import jax
import jax.numpy as jnp
import numpy as np
from jax import lax
from jax.experimental import pallas as pl
from jax.experimental.pallas import tpu as pltpu
from jax.experimental.pallas import tpu_sc as plsc

F32 = jnp.float32
BF16 = jnp.bfloat16
HIGHEST = lax.Precision.HIGHEST

N_A_LAYERS = 1

A_HEAD_DIM = 128
A_HEADS = 8
A_WIDTH = 1024
A_CONV = 4
A_CHUNK = 64
A_PROJ_PAD = 4608

B_HEAD_DIM = 64
B_Q_HEADS = 16
B_KV_HEADS = 2
B_GROUP = 8
WINDOW = 128
ROPE_DIM = 16
ROPE_THETA = 500000.0

N_EXPERTS = 32
TOP_K = 4
D_FF = 1024
SWIGLU_LIMIT = 7.0
SWIGLU_ALPHA = 1.702
MOE_ROWS = 512
MOE_FIRST_PART = 4

EPS = 1e-6
LANES = 128
NEG_INF = float("-inf")

VMEM_LIMIT = 56 * 1024 * 1024


def _cparams(sem, vmem=VMEM_LIMIT):
    return pltpu.CompilerParams(dimension_semantics=sem, vmem_limit_bytes=vmem)


def _silu(x):
    return x * jax.nn.sigmoid(x)


HI_HALF = -65536


def _pack_bf16_pairs(x):
    n = x.shape[-1] // 2
    bits = lax.bitcast_convert_type(x.astype(BF16).astype(F32), jnp.int32)
    return jnp.bitwise_or(lax.shift_right_logical(bits[:, :n], 16), jnp.bitwise_and(bits[:, n:], HI_HALF))


def _unpack_bf16_pairs(w):
    lo = lax.bitcast_convert_type(lax.shift_left(w, 16), F32)
    hi = lax.bitcast_convert_type(jnp.bitwise_and(w, HI_HALF), F32)
    return lo, hi


def _ada_norm(x, gain, shift, scale):
    y = x * lax.rsqrt(jnp.mean(x * x, axis=-1, keepdims=True) + EPS)
    return (y * gain) * (1.0 + scale) + shift


def _mod_body(c_ref, w_ref, b_ref, o_ref):
    o_ref[...] = jnp.dot(_silu(c_ref[...]), w_ref[...], preferred_element_type=F32,
                         precision=HIGHEST) + b_ref[...]


def _modulation(c, w3, b3, layer):
    bsz, d = c.shape
    n = w3.shape[-1]
    tn = 1024
    return pl.pallas_call(
        _mod_body,
        out_shape=jax.ShapeDtypeStruct((bsz, n), F32),
        grid=(n // tn,),
        in_specs=[pl.BlockSpec((bsz, d), lambda j: (0, 0)),
                  pl.BlockSpec((None, d, tn), lambda j: (layer, 0, j)),
                  pl.BlockSpec((None, 1, tn), lambda j: (layer, 0, j))],
        out_specs=pl.BlockSpec((bsz, tn), lambda j: (0, j)),
        compiler_params=_cparams(("arbitrary",)),
        name="adaln_mod",
    )(c, w3, b3)


INPROJ_TM = 256
INPROJ_TN = 256
HALO = 8


def _inproj_body(x_ref, gain_ref, sh_ref, sc_ref, w_ref, conv_ref, o_ref, ext, *, tiles_per_seq):
    i = pl.program_id(0)
    tm = x_ref.shape[0]
    hd = A_HEAD_DIM
    n = o_ref.shape[-1]
    tn = INPROJ_TN
    h = _ada_norm(x_ref[...], gain_ref[...], sh_ref[...], sc_ref[...]).astype(BF16)

    @pl.when(i % tiles_per_seq == 0)
    def _():
        ext[0:HALO, :] = jnp.zeros((HALO, ext.shape[-1]), F32)

    def conv_silu_norm(raw, n0):
        cols = slice(n0, n0 + tn)
        ext[HALO:HALO + tm, cols] = raw
        acc = raw * conv_ref[A_CONV - 1:A_CONV, cols]
        for s in range(1, A_CONV):
            acc = acc + ext[HALO - s:HALO - s + tm, cols] * conv_ref[A_CONV - 1 - s:A_CONV - s, cols]
        ext[0:HALO, cols] = raw[tm - HALO:, :]
        y = _silu(acc)
        if n0 >= 2 * A_WIDTH:
            return y
        scale = hd ** -0.5 if n0 < A_WIDTH else 1.0
        heads = []
        for c0 in range(0, tn, hd):
            yh = y[:, c0:c0 + hd]
            heads.append(yh * (lax.rsqrt(jnp.sum(yh * yh, axis=-1, keepdims=True) + EPS) * scale))
        return jnp.concatenate(heads, axis=1)

    matmul = lambda n0: jnp.dot(h, w_ref[:, n0:n0 + tn], preferred_element_type=F32)
    conv_chunks = list(range(0, 3 * A_WIDTH, tn))
    plain_chunks = list(range(3 * A_WIDTH, n, tn))
    every = max(1, len(conv_chunks) // max(1, len(plain_chunks)))
    raw = matmul(conv_chunks[0])
    for idx, n0 in enumerate(conv_chunks):
        nxt = matmul(conv_chunks[idx + 1]) if idx + 1 < len(conv_chunks) else None
        if idx % every == every - 1 and plain_chunks:
            p0 = plain_chunks.pop(0)
            o_ref[:, p0:p0 + tn] = matmul(p0)
        o_ref[:, n0:n0 + tn] = conv_silu_norm(raw, n0)
        raw = nxt
    for p0 in plain_chunks:
        o_ref[:, p0:p0 + tn] = matmul(p0)


def _in_proj(x2, gain, mod4, w_bf, conv_w, seq):
    t, d = x2.shape
    n = w_bf.shape[-1]
    tm = INPROJ_TM
    tps = seq // tm
    return pl.pallas_call(
        functools.partial(_inproj_body, tiles_per_seq=tps),
        out_shape=jax.ShapeDtypeStruct((t, n), F32),
        grid=(t // tm,),
        in_specs=[pl.BlockSpec((tm, d), lambda i: (i, 0)),
                  pl.BlockSpec((1, d), lambda i: (0, 0)),
                  pl.BlockSpec((None, None, 1, d), lambda i: (i // tps, 0, 0, 0)),
                  pl.BlockSpec((None, None, 1, d), lambda i: (i // tps, 1, 0, 0)),
                  pl.BlockSpec((d, n), lambda i: (0, 0)),
                  pl.BlockSpec(conv_w.shape, lambda i: (0, 0))],
        out_specs=pl.BlockSpec((tm, n), lambda i: (i, 0)),
        scratch_shapes=[pltpu.VMEM((HALO + tm, conv_w.shape[-1]), F32)],
        compiler_params=_cparams(("arbitrary",)),
        name="gdn_in_proj",
    )(x2, gain, mod4, mod4, w_bf, conv_w)


GDN_HG = 2
GDN_RB = 256
GDN_SC = 128
GDN_SUB = 4


def _split_bf16(x):
    hi = x.astype(BF16)
    return hi, (x - hi.astype(F32)).astype(BF16)


def _mm3(a_hi, a_lo, b_hi, b_lo, dims=(((1,), (0,)), ((), ()))):
    dg = functools.partial(lax.dot_general, dimension_numbers=dims, preferred_element_type=F32)
    return dg(a_hi, b_hi) + (dg(a_lo, b_hi) + dg(a_hi, b_lo))


def _dot3(a, b, dims):
    return _mm3(*_split_bf16(a), *_split_bf16(b), dims)


def _gdn_body(alog_ref, dtb_ref, q_ref, k_ref, v_ref, z_ref, ab_ref, og_ref,
              o_ref, bs, gcs, mp, bc, qp, op):
    seq = q_ref.shape[0]
    hd = A_HEAD_DIM
    ch = A_CHUNK
    g_idx = pl.program_id(1)

    lane = lax.broadcasted_iota(jnp.int32, (GDN_RB, hd), 1)
    row = lax.broadcasted_iota(jnp.int32, (GDN_RB, hd), 0)
    row_in_chunk = row % ch

    sc_rows = GDN_SC
    per_sc = sc_rows // ch
    rb_per_step = sc_rows * GDN_SUB // GDN_RB
    head_cols = lambda hh: slice(hh * hd, (hh + 1) * hd)

    def prep_stages(j):
        for rbi in range(rb_per_step):
            r = j * rb_per_step + rbi
            start = pl.multiple_of(r * GDN_RB, GDN_RB)
            rows = pl.ds(start, GDN_RB)
            for hh in range(GDN_HG):
                head = g_idx * GDN_HG + hh
                ab = ab_ref[rows, :]
                a_col = jnp.sum(jnp.where(lane == head, ab, 0.0), axis=-1, keepdims=True)
                b_col = jnp.sum(jnp.where(lane == head + A_HEADS, ab, 0.0), axis=-1, keepdims=True)
                bs[hh, rows, :] = jnp.broadcast_to(jax.nn.sigmoid(b_col), (GDN_RB, hd))
                xa = a_col + dtb_ref[0, head]
                softplus = jnp.maximum(xa, 0.0) + jnp.log1p(jnp.exp(-jnp.abs(xa)))
                g = jnp.broadcast_to(-jnp.exp(alog_ref[0, head]) * softplus, (GDN_RB, hd))
                s = 1
                while s < ch:
                    g = g + jnp.where(row_in_chunk >= s, pltpu.roll(g, s, 0), 0.0)
                    s *= 2
                gcs[hh, rows, :] = g
                yield

    ri = lax.broadcasted_iota(jnp.int32, (sc_rows, sc_rows), 0)
    cj = lax.broadcasted_iota(jnp.int32, (sc_rows, sc_rows), 1)
    same_chunk = (ri // ch) == (cj // ch)
    incl = jnp.logical_and(same_chunk, ri >= cj)
    strict = jnp.logical_and(same_chunk, ri > cj)
    eye = jnp.where(ri == cj, 1.0, 0.0)
    chunk_of_col = lax.broadcasted_iota(jnp.int32, (hd, sc_rows), 1) // ch
    lanes_nt = (((1,), (1,)), ((), ()))
    plain = (((1,), (0,)), ((), ()))

    chunks_per_step = GDN_SUB * per_sc
    n_steps = seq // (sc_rows * GDN_SUB)

    def local_stages(i):
        chains = [(i * GDN_SUB + sub, hh) for sub in range(GDN_SUB) for hh in range(GDN_HG)]
        rows_of = lambda blk: pl.ds(pl.multiple_of(blk * sc_rows, sc_rows), sc_rows)

        def start(blk, hh):
            rows = rows_of(blk)
            k = k_ref[rows, head_cols(hh)]
            gc = gcs[hh, rows, :]
            decay = jnp.where(incl, jnp.exp(jnp.minimum(gc - gc.T[0:1, :], 0.0)), 0.0)
            k_bf = k.astype(BF16)
            kk = lax.dot_general((k * bs[hh, rows, :]).astype(BF16), k_bf, lanes_nt, preferred_element_type=F32)
            qk = lax.dot_general(q_ref[rows, head_cols(hh)].astype(BF16), k_bf, lanes_nt, preferred_element_type=F32)
            return jnp.where(strict, -(kk * decay), 0.0), (qk * decay).astype(BF16)

        started = [start(*c) for c in chains]
        yield
        qks = [s[1] for s in started]
        invs = [eye + s[0] for s in started]
        pws = [_dot3(s[0], s[0], plain) for s in started]
        yield
        for step in range(1, 6):
            for n in range(len(chains)):
                pw_hi, pw_lo = _split_bf16(pws[n])
                inv_hi, inv_lo = _split_bf16(invs[n])
                if step < 5:
                    prod = _mm3(jnp.concatenate([pw_hi, inv_hi], axis=0),
                                jnp.concatenate([pw_lo, inv_lo], axis=0), pw_hi, pw_lo)
                    pws[n] = prod[:sc_rows]
                    invs[n] = invs[n] + prod[sc_rows:]
                else:
                    invs[n] = invs[n] + _mm3(inv_hi, inv_lo, pw_hi, pw_lo)
            yield

        def solve(n, blk, hh):
            rows = rows_of(blk)
            beta = bs[hh, rows, :]
            rhs = jnp.concatenate([v_ref[rows, head_cols(hh)] * beta, k_ref[rows, head_cols(hh)] * beta * jnp.exp(gcs[hh, rows, :])], axis=1)
            return _dot3(invs[n], rhs, plain).astype(BF16)

        uws = [solve(n, *c) for n, c in enumerate(chains)]
        yield

        def finish(n, blk, hh):
            rows = rows_of(blk)
            k = k_ref[rows, head_cols(hh)]
            gc = gcs[hh, rows, :]
            res = jnp.dot(qks[n], uws[n], preferred_element_type=F32)
            op[hh, rows, :] = res[:, :hd]
            qp[hh, rows, :] = (q_ref[rows, head_cols(hh)] * jnp.exp(gc) - res[:, hd:]).astype(BF16)
            gl = jnp.concatenate(
                [jnp.broadcast_to(gc[(j + 1) * ch - 1:(j + 1) * ch, :], (ch, hd)) for j in range(per_sc)], axis=0)
            kt_t = (k * jnp.exp(gl - gc)).T
            for j in range(per_sc):
                kt_j = jnp.where(chunk_of_col == j, kt_t, 0.0).astype(BF16)
                bm = jnp.dot(kt_j, uws[n], preferred_element_type=F32)
                bc[hh, blk * per_sc + j] = bm[:, :hd]
                mp[hh, blk * per_sc + j] = bm[:, hd:].astype(BF16)

        for n, c in enumerate(chains):
            finish(n, *c)

    def scan_stages(j, states):
        for cc in range(chunks_per_step):
            c = j * chunks_per_step + cc
            r0 = pl.multiple_of(c * ch, ch)
            rows = pl.ds(r0, ch)
            for hh in range(GDN_HG):
                col0 = hh * hd
                state = states[hh]
                s_bf = state.astype(BF16)
                o = jnp.dot(qp[hh, rows, :], s_bf, preferred_element_type=F32) + op[hh, rows, :]
                g_tot = jnp.exp(gcs[hh, pl.ds(r0 + ch - 1, 1), :])
                states[hh] = state * g_tot - jnp.dot(mp[hh, c], s_bf, preferred_element_type=F32) + bc[hh, c]
                on = o * lax.rsqrt(jnp.mean(o * o, axis=-1, keepdims=True) + EPS) * og_ref[...]
                z = z_ref[rows, col0:col0 + hd]
                o_ref[rows, col0:col0 + hd] = (on * _silu(z)).astype(BF16)
            yield

    def interleave(*gens):
        live = list(gens)
        while live:
            for g in list(live):
                try:
                    next(g)
                except StopIteration:
                    live.remove(g)

    assert n_steps >= 2
    interleave(prep_stages(0))
    interleave(local_stages(0), prep_stages(1))

    def pipelined(i, states):
        states = list(states)
        interleave(local_stages(i), scan_stages(i - 1, states), prep_stages(i + 1))
        return tuple(states)

    states = list(lax.fori_loop(1, n_steps - 1, pipelined,
                                tuple(jnp.zeros((hd, hd), F32) for _ in range(GDN_HG))))
    interleave(local_stages(n_steps - 1), scan_stages(n_steps - 2, states))
    interleave(scan_stages(n_steps - 1, states))


def _gdn(proj, a_log, dt_bias, out_gain, bsz, seq):
    t = proj.shape[0]
    hd = A_HEAD_DIM
    wb = hd * GDN_HG
    ng = A_HEADS // GDN_HG
    per = A_WIDTH // wb
    n_chunks = seq // A_CHUNK
    smem = pl.BlockSpec(memory_space=pltpu.SMEM)
    seq_spec = lambda off: pl.BlockSpec((seq, wb), lambda b, g, off=off: (b, off + g))
    sc = lambda dt=F32: pltpu.VMEM((GDN_HG, seq, hd), dt)
    return pl.pallas_call(
        _gdn_body,
        out_shape=jax.ShapeDtypeStruct((t, A_WIDTH), BF16),
        grid=(bsz, ng),
        in_specs=[smem, smem,
                  seq_spec(0), seq_spec(per), seq_spec(2 * per), seq_spec(3 * per),
                  pl.BlockSpec((seq, LANES), lambda b, g: (b, 4 * A_WIDTH // LANES)),
                  pl.BlockSpec((1, hd), lambda b, g: (0, 0))],
        out_specs=pl.BlockSpec((seq, wb), lambda b, g: (b, g)),
        scratch_shapes=[sc(), sc(),
                        pltpu.VMEM((GDN_HG, n_chunks, hd, hd), BF16),
                        pltpu.VMEM((GDN_HG, n_chunks, hd, hd), F32),
                        sc(BF16), sc()],
        compiler_params=_cparams(("parallel", "parallel")),
        name="gdn_core",
    )(a_log, dt_bias, proj, proj, proj, proj, proj, out_gain)


def _oproj_body(a_ref, w_ref, x_ref, g_ref, o_ref):
    y = jnp.dot(a_ref[...], w_ref[...], preferred_element_type=F32)
    o_ref[...] = x_ref[...] + g_ref[...] * y


def _out_proj_residual(a_bf, w_bf, x2, mod4, gate_slot, seq):
    t, d = x2.shape
    kdim = a_bf.shape[-1]
    tm = 512
    tps = seq // tm
    return pl.pallas_call(
        _oproj_body,
        out_shape=jax.ShapeDtypeStruct((t, d), F32),
        grid=(t // tm,),
        in_specs=[pl.BlockSpec((tm, kdim), lambda i: (i, 0)),
                  pl.BlockSpec((kdim, d), lambda i: (0, 0)),
                  pl.BlockSpec((tm, d), lambda i: (i, 0)),
                  pl.BlockSpec((None, None, 1, d), lambda i: (i // tps, gate_slot, 0, 0))],
        out_specs=pl.BlockSpec((tm, d), lambda i: (i, 0)),
        compiler_params=_cparams(("parallel",)),
        name="out_proj_residual",
    )(a_bf, w_bf, x2, mod4)


ROUTER_TM = 512


K_ROWS = 8


def _router_body(x_ref, gain_ref, sh_ref, sc_ref, rw_ref, rb_ref, h_ref, idx_ref, gate_ref, rank_ref,
                 cnt_ref, carry):
    i = pl.program_id(0)

    @pl.when(i == 0)
    def _():
        carry[...] = jnp.zeros_like(carry)

    tm = x_ref.shape[0]
    h = _ada_norm(x_ref[...], gain_ref[...], sh_ref[...], sc_ref[...])
    h_ref[...] = _pack_bf16_pairs(h)
    logits = _dot3(h, rw_ref[...], (((1,), (0,)), ((), ()))) + rb_ref[...]
    work = logits.T[:N_EXPERTS, :]
    expert = lax.broadcasted_iota(jnp.int32, (N_EXPERTS, tm), 0)
    tops, sels = [], []
    for _ in range(TOP_K):
        m = jnp.max(work, axis=0, keepdims=True)
        sel = jnp.min(jnp.where(work == m, expert, N_EXPERTS), axis=0, keepdims=True)
        work = jnp.where(expert == sel, NEG_INF, work)
        tops.append(m)
        sels.append(sel)
    exps = [jnp.exp(m - tops[0]) for m in tops]
    denom = exps[0] + exps[1] + exps[2] + exps[3]
    onehot = jnp.zeros((N_EXPERTS, tm), F32)
    for sel in sels:
        onehot = onehot + jnp.where(expert == sel, 1.0, 0.0)
    ti = lax.broadcasted_iota(jnp.int32, (tm, tm), 0)
    tj = lax.broadcasted_iota(jnp.int32, (tm, tm), 1)
    earlier = (ti < tj).astype(BF16)
    before = jnp.dot(onehot.astype(BF16), earlier, preferred_element_type=F32) + carry[:, 0:1]
    ranks = [jnp.sum(jnp.where(expert == sel, before, 0.0), axis=0, keepdims=True) for sel in sels]
    pad_i = jnp.zeros((K_ROWS - TOP_K, tm), jnp.int32)
    idx_ref[...] = jnp.concatenate(sels + [pad_i], axis=0)
    gate_ref[...] = jnp.concatenate([e / denom for e in exps] + [pad_i.astype(F32)], axis=0)
    rank_ref[...] = jnp.concatenate([r.astype(jnp.int32) for r in ranks] + [pad_i], axis=0)
    carry[...] = carry[...] + jnp.sum(onehot, axis=1, keepdims=True)
    cnt_ref[...] = carry[...]


def _router(x2, gain, mod4, sh_slot, sc_slot, rw_pad, rb_pad, seq):
    t, d = x2.shape
    tm = ROUTER_TM
    tps = seq // tm
    tok = lambda dt: jax.ShapeDtypeStruct((K_ROWS, t), dt)
    tok_spec = pl.BlockSpec((K_ROWS, tm), lambda i: (0, i))
    cnt_spec = pl.BlockSpec((N_EXPERTS, LANES), lambda i: (0, 0))
    return pl.pallas_call(
        _router_body,
        out_shape=(jax.ShapeDtypeStruct((t, d // 2), jnp.int32), tok(jnp.int32), tok(F32), tok(jnp.int32),
                   jax.ShapeDtypeStruct((N_EXPERTS, LANES), F32)),
        grid=(t // tm,),
        in_specs=[pl.BlockSpec((tm, d), lambda i: (i, 0)),
                  pl.BlockSpec((1, d), lambda i: (0, 0)),
                  pl.BlockSpec((None, None, 1, d), lambda i: (i // tps, sh_slot, 0, 0)),
                  pl.BlockSpec((None, None, 1, d), lambda i: (i // tps, sc_slot, 0, 0)),
                  pl.BlockSpec((d, LANES), lambda i: (0, 0)),
                  pl.BlockSpec((1, LANES), lambda i: (0, 0))],
        out_specs=(pl.BlockSpec((tm, d // 2), lambda i: (i, 0)), tok_spec, tok_spec, tok_spec, cnt_spec),
        scratch_shapes=[pltpu.VMEM((N_EXPERTS, LANES), F32)],
        compiler_params=_cparams(("arbitrary",)),
        name="moe_router",
    )(x2, gain, mod4, mod4, rw_pad, rb_pad)


SC_LANES = 16
SC_INDEX_CHUNK = 8192


def _sc_row_tokens(dest_flat, n_rows, n_tok):
    n_assign = dest_flat.shape[0]
    assert n_rows % SC_LANES == 0 and n_assign % SC_INDEX_CHUNK == 0
    n_cores = plsc.get_sparse_core_info().num_cores
    mesh = plsc.VectorSubcoreMesh(core_axis_name="c", subcore_axis_name="s")

    @functools.partial(
        pl.kernel, mesh=mesh,
        out_type=jax.ShapeDtypeStruct((n_rows,), jnp.int32),
        scratch_types=[pltpu.VMEM((n_rows,), jnp.int32), pltpu.VMEM((SC_INDEX_CHUNK,), jnp.int32)],
        compiler_params=dataclasses.replace(pltpu.CompilerParams(), needs_layout_passes=False),
    )
    def row_token_kernel(dest_hbm, out_hbm, rt_v, d_v):
        wid = lax.axis_index("s") * n_cores + lax.axis_index("c")

        @pl.when(wid == 0)
        def _():
            lanes = lax.iota(jnp.int32, SC_LANES)

            @pl.loop(0, n_rows // SC_LANES)
            def _(i):
                rt_v[pl.ds(i * SC_LANES, SC_LANES)] = lax.rem(lanes + i * SC_LANES, n_tok)

            @pl.loop(0, n_assign // SC_INDEX_CHUNK)
            def _(c):
                pltpu.sync_copy(dest_hbm.at[pl.ds(c * SC_INDEX_CHUNK, SC_INDEX_CHUNK)], d_v)

                @pl.loop(0, SC_INDEX_CHUNK // SC_LANES)
                def _(i):
                    idx = d_v[pl.ds(i * SC_LANES, SC_LANES)]
                    tok = lax.rem(lanes + (c * SC_INDEX_CHUNK + i * SC_LANES), n_tok)
                    plsc.store_scatter(rt_v, [idx], tok)

            pltpu.sync_copy(rt_v, out_hbm)

    return row_token_kernel(dest_flat)


def _ffn_body(be_ref, nu_ref, nxt_ref, slot_ref, x_ref, uw_hbm, ub_ref, dw_hbm, db_ref, *rest, layer):
    y_ref, uw_f32, dw_f32, uw_bf, dw_bf, sems = rest[-6:]
    i = pl.program_id(0)
    first = jnp.logical_and(jnp.logical_or(i == 0, be_ref[i] != be_ref[jnp.maximum(i - 1, 0)]), i < nu_ref[0])
    active = i < nu_ref[0]
    slot = slot_ref[i]

    def weight_copies(expert, s):
        return (pltpu.make_async_copy(uw_hbm.at[layer, expert], uw_f32.at[s], sems.at[s, 0]),
                pltpu.make_async_copy(dw_hbm.at[layer, expert], dw_f32.at[s], sems.at[s, 1]))

    @pl.when(jnp.logical_and(i == 0, active))
    def _():
        for cp in weight_copies(be_ref[0], 0):
            cp.start()

    @pl.when(first)
    def _():
        for cp in weight_copies(be_ref[i], slot):
            cp.wait()

        @pl.when(nxt_ref[i] >= 0)
        def _():
            for cp in weight_copies(nxt_ref[i], 1 - slot):
                cp.start()

        rows = 128
        for r0 in range(0, uw_bf.shape[0], rows):
            uw_bf[r0:r0 + rows, :] = uw_f32[slot, r0:r0 + rows, :].astype(BF16)
        for r0 in range(0, dw_bf.shape[0], rows):
            dw_bf[r0:r0 + rows, :] = dw_f32[slot, r0:r0 + rows, :].astype(BF16)

    @pl.when(active)
    def _():
        x = jnp.concatenate(_unpack_bf16_pairs(x_ref[...]), axis=1).astype(BF16)
        gu = jnp.dot(x, uw_bf[...], preferred_element_type=F32) + ub_ref[...]
        gate = jnp.minimum(gu[:, :D_FF], SWIGLU_LIMIT)
        lin = jnp.clip(gu[:, D_FF:], -SWIGLU_LIMIT, SWIGLU_LIMIT)
        act = gate * jax.nn.sigmoid(SWIGLU_ALPHA * gate) * (lin + 1.0)
        y = jnp.dot(act.astype(BF16), dw_bf[...], preferred_element_type=F32) + db_ref[...]
        y_ref[...] = _pack_bf16_pairs(y)

    @pl.when(jnp.logical_not(active))
    def _():
        y_ref[...] = jnp.zeros_like(y_ref)


def _expert_ffn(xb, blk_expert, n_used, up_w, up_b4, down_w, down_b4, layer, n_rows_total, first_blk, y_prev):
    n_rows, dp = xb.shape
    d = 2 * dp
    bm = MOE_ROWS
    n_blk = n_rows // bm
    f2 = up_w.shape[-1]
    blk = jnp.arange(n_blk, dtype=jnp.int32)
    first = jnp.logical_and(jnp.concatenate([jnp.ones((1,), bool), blk_expert[1:] != blk_expert[:-1]]),
                            blk < n_used[0])
    slot = ((jnp.cumsum(first.astype(jnp.int32)) - 1) % 2).astype(jnp.int32)
    later_first = jnp.where(first, blk, n_blk)[::-1]
    next_first = jnp.concatenate([lax.cummin(later_first)[::-1][1:], jnp.full((1,), n_blk, jnp.int32)])
    nxt = jnp.where(jnp.logical_and(first, next_first < n_blk),
                    blk_expert[jnp.minimum(next_first, n_blk - 1)], -1).astype(jnp.int32)
    grid_spec = pltpu.PrefetchScalarGridSpec(
        num_scalar_prefetch=4,
        grid=(n_blk,),
        in_specs=[pl.BlockSpec((bm, dp), lambda i, be, nu, nx, sl: (i, 0)),
                  pl.BlockSpec(memory_space=pl.ANY),
                  pl.BlockSpec((None, None, 1, f2), lambda i, be, nu, nx, sl: (layer, be[i], 0, 0)),
                  pl.BlockSpec(memory_space=pl.ANY),
                  pl.BlockSpec((None, None, 1, d), lambda i, be, nu, nx, sl: (layer, be[i], 0, 0))]
                 + ([] if y_prev is None else [pl.BlockSpec(memory_space=pl.ANY)]),
        out_specs=pl.BlockSpec((bm, dp), lambda i, be, nu, nx, sl: (i + first_blk, 0)),
        scratch_shapes=[pltpu.VMEM((2, d, f2), F32), pltpu.VMEM((2, f2 // 2, d), F32),
                        pltpu.VMEM((d, f2), BF16), pltpu.VMEM((f2 // 2, d), BF16),
                        pltpu.SemaphoreType.DMA((2, 2))],
    )
    args = (blk_expert, n_used, nxt, slot, xb, up_w, up_b4, down_w, down_b4)
    return pl.pallas_call(
        functools.partial(_ffn_body, layer=layer),
        out_shape=jax.ShapeDtypeStruct((n_rows_total, dp), jnp.int32),
        grid_spec=grid_spec,
        input_output_aliases={} if y_prev is None else {len(args): 0},
        compiler_params=_cparams(("arbitrary",)),
        name="moe_expert_ffn",
    )(*args, *(() if y_prev is None else (y_prev,)))


SC_GATHER_ROWS = 64


def _sc_gather_rows(table, idx):
    n, d = idx.shape[0], table.shape[1]
    info = plsc.get_sparse_core_info()
    n_cores, n_sub = info.num_cores, info.num_subcores
    n_workers = n_cores * n_sub
    chunk = SC_GATHER_ROWS
    per_worker = n // n_workers
    n_chunks = per_worker // chunk
    assert n_chunks * chunk * n_workers == n and n_chunks % 2 == 0
    mesh = plsc.VectorSubcoreMesh(core_axis_name="c", subcore_axis_name="s")

    @functools.partial(
        pl.kernel, mesh=mesh,
        out_type=jax.ShapeDtypeStruct((n, d), table.dtype),
        scratch_types=[pltpu.VMEM((n_chunks, chunk), jnp.int32), pltpu.VMEM((2, chunk, d), table.dtype),
                       pltpu.SemaphoreType.DMA((2,))],
    )
    def gather_kernel(table_hbm, idx_hbm, out_hbm, idx_v, rows_v, sems):
        wid = lax.axis_index("s") * n_cores + lax.axis_index("c")
        base = wid * per_worker
        pltpu.sync_copy(idx_hbm.at[wid], idx_v)

        def gather(ci, slot):
            return pltpu.make_async_copy(table_hbm.at[idx_v.at[ci]], rows_v.at[slot], sems.at[slot])

        gather(0, 0).start()

        @pl.loop(0, n_chunks, step=2)
        def _(c0):
            for slot in range(2):
                ci = c0 + slot
                gather(ci, slot).wait()

                @pl.when(ci + 1 < n_chunks)
                def _():
                    gather(ci + 1, 1 - slot).start()

                pltpu.sync_copy(rows_v.at[slot], out_hbm.at[pl.ds(base + ci * chunk, chunk)])

    return gather_kernel(table, idx.reshape(n_workers, n_chunks, chunk))


COMBINE_TM = 256


def _combine_body(gate_ref, x_ref, g_ref, y_ref, o_ref):
    gates = gate_ref[...]
    acc_lo = acc_hi = None
    for k in range(TOP_K):
        lo, hi = _unpack_bf16_pairs(y_ref[k])
        gk = gates[:, k:k + 1]
        acc_lo = gk * lo if acc_lo is None else acc_lo + gk * lo
        acc_hi = gk * hi if acc_hi is None else acc_hi + gk * hi
    o_ref[...] = x_ref[...] + g_ref[...] * jnp.concatenate([acc_lo, acc_hi], axis=1)


def _combine(y_kt, gates, x2, mod4, gate_slot, seq):
    t, d = x2.shape
    tm = COMBINE_TM
    tps = seq // tm
    return pl.pallas_call(
        _combine_body,
        out_shape=jax.ShapeDtypeStruct((t, d), F32),
        grid=(t // tm,),
        in_specs=[pl.BlockSpec((tm, gates.shape[1]), lambda i: (i, 0)),
                  pl.BlockSpec((tm, d), lambda i: (i, 0)),
                  pl.BlockSpec((None, None, 1, d), lambda i: (i // tps, gate_slot, 0, 0)),
                  pl.BlockSpec((TOP_K, tm, d // 2), lambda i: (0, i, 0))],
        out_specs=pl.BlockSpec((tm, d), lambda i: (i, 0)),
        compiler_params=_cparams(("parallel",)),
        name="moe_combine",
    )(gates, x2, mod4, y_kt)


def _moe_layer(x2, gain, mod4, router_w, router_b, up_w, up_b, down_w, down_b, layer, seq):
    t, d = x2.shape
    rw_pad = jnp.pad(router_w[layer], ((0, 0), (0, LANES - N_EXPERTS)))
    rb_pad = jnp.pad(router_b[layer], (0, LANES - N_EXPERTS)).reshape(1, LANES)
    h2, idx, gates, rank, counts = _router(x2, gain, mod4, 3, 4, rw_pad, rb_pad, seq)

    bm = MOE_ROWS
    counts = counts[:, 0].astype(jnp.int32)
    padded = (counts + bm - 1) // bm * bm
    p_ends = jnp.cumsum(padded)
    p_starts = p_ends - padded
    n_rows = t * TOP_K + N_EXPERTS * bm
    n_blk = n_rows // bm
    onehot = idx[:TOP_K, :, None] == jnp.arange(N_EXPERTS, dtype=jnp.int32)[None, None, :]
    dest = jnp.sum(jnp.where(onehot, p_starts[None, None, :], 0), axis=-1) + rank[:TOP_K]
    dest_flat = dest.reshape(TOP_K * t)
    blk_start = jnp.arange(n_blk, dtype=jnp.int32) * bm
    blk_expert = jnp.sum((blk_start[:, None] >= p_ends[None, :]).astype(jnp.int32), axis=-1)
    blk_expert = jnp.minimum(blk_expert, N_EXPERTS - 1).astype(jnp.int32)
    n_used = (p_ends[-1:] // bm).astype(jnp.int32)

    row_token = _sc_row_tokens(dest_flat, n_rows, t)
    f2 = up_w.shape[-1]
    up_b4 = up_b.reshape(up_b.shape[0], N_EXPERTS, 1, f2)
    down_b4 = down_b.reshape(down_b.shape[0], N_EXPERTS, 1, d)
    bounds = (0, n_blk // MOE_FIRST_PART, n_blk)
    xb_parts = [_sc_gather_rows(h2, row_token[lo * bm:hi * bm]) for lo, hi in zip(bounds[:-1], bounds[1:])]
    y_rows = None
    for lo, hi, xb in zip(bounds[:-1], bounds[1:], xb_parts):
        y_rows = _expert_ffn(xb, blk_expert[lo:hi], jnp.clip(n_used - lo, 0, hi - lo), up_w, up_b4, down_w,
                             down_b4, layer, n_rows, lo, y_rows)
    y_kt = _sc_gather_rows(y_rows, dest_flat).reshape(TOP_K, t, d // 2)
    return _combine(y_kt, gates.T, x2, mod4, 5, seq)


QKV_TM = 256


def _rope_tables(pos_row, invf_col):
    tm = pos_row.shape[1]
    half = ROPE_DIM // 2
    ang = invf_col * pos_row
    expand = (lax.broadcasted_iota(jnp.int32, (half, LANES), 1) % half
              == lax.broadcasted_iota(jnp.int32, (half, LANES), 0)).astype(BF16)

    def to_lanes(x):
        hi = x.astype(BF16)
        rest = x - hi.astype(F32)
        mid = rest.astype(BF16)
        parts = (hi, mid, (rest - mid.astype(F32)).astype(BF16))
        return sum(lax.dot_general(p, expand, (((0,), (0,)), ((), ())), preferred_element_type=F32) for p in parts)

    c = to_lanes(jnp.cos(ang))
    s = to_lanes(jnp.sin(ang))
    d = lax.broadcasted_iota(jnp.int32, (tm, LANES), 1) % B_HEAD_DIM
    cos_t = jnp.where(d < ROPE_DIM, c, 1.0)
    sin_lo = jnp.where(d < half, -s, 0.0)
    sin_hi = jnp.where(jnp.logical_and(d >= half, d < ROPE_DIM), s, 0.0)
    return cos_t, sin_lo, sin_hi


def _head_norm_rope(x, gain_row, tables):
    cos_t, sin_lo, sin_hi = tables
    same_head = (lax.broadcasted_iota(jnp.int32, (LANES, LANES), 0) // B_HEAD_DIM
                 == lax.broadcasted_iota(jnp.int32, (LANES, LANES), 1) // B_HEAD_DIM).astype(BF16)
    sq_hi, sq_lo = _split_bf16(x * x)
    ss = (jnp.dot(sq_hi, same_head, preferred_element_type=F32)
          + jnp.dot(sq_lo, same_head, preferred_element_type=F32))
    xn = x * lax.rsqrt(ss * (1.0 / B_HEAD_DIM) + EPS) * gain_row
    half = ROPE_DIM // 2
    return (xn * cos_t + pltpu.roll(xn, LANES - half, 1) * sin_lo + pltpu.roll(xn, half, 1) * sin_hi)


def _qkv_body(x_ref, pos_ref, gq_ref, shq_ref, scq_ref, gkv_ref, shkv_ref, sckv_ref, wq_ref, wkv_ref,
              qg_ref, kg_ref, invf_ref, q_ref, k_ref, v_ref):
    x = x_ref[...]
    y = x * lax.rsqrt(jnp.mean(x * x, axis=-1, keepdims=True) + EPS)
    hq = ((y * gq_ref[...]) * (1.0 + scq_ref[...]) + shq_ref[...]).astype(BF16)
    hkv = ((y * gkv_ref[...]) * (1.0 + sckv_ref[...]) + shkv_ref[...]).astype(BF16)
    tables = _rope_tables(pos_ref[...].astype(F32), invf_ref[...])
    kv = jnp.dot(hkv, wkv_ref[...], preferred_element_type=F32)
    k_ref[...] = _head_norm_rope(kv[:, :LANES], kg_ref[...], tables).astype(BF16)
    v_ref[...] = kv[:, LANES:].astype(BF16)
    q = jnp.dot(hq, wq_ref[...], preferred_element_type=F32)
    scale = B_HEAD_DIM ** -0.5
    for p in range(q.shape[-1] // LANES):
        qp = _head_norm_rope(q[:, p * LANES:(p + 1) * LANES], qg_ref[...], tables)
        q_ref[:, p * LANES:(p + 1) * LANES] = (qp * scale).astype(BF16)


def _qkv(x2, pos2, gq, mod4, gkv, kvmod4, wq_bf, wkv_bf, qg2, kg2, invf, seq):
    t, d = x2.shape
    tm = QKV_TM
    tps = seq // tm
    nq = wq_bf.shape[-1]
    nkv = wkv_bf.shape[-1]
    row = lambda n: pl.BlockSpec((1, n), lambda i: (0, 0))
    modspec = lambda slot: pl.BlockSpec((None, None, 1, d), lambda i, slot=slot: (i // tps, slot, 0, 0))
    return pl.pallas_call(
        _qkv_body,
        out_shape=(jax.ShapeDtypeStruct((t, nq), BF16), jax.ShapeDtypeStruct((t, LANES), BF16),
                   jax.ShapeDtypeStruct((t, LANES), BF16)),
        grid=(t // tm,),
        in_specs=[pl.BlockSpec((tm, d), lambda i: (i, 0)),
                  pl.BlockSpec((None, 1, tm), lambda i: (i, 0, 0)),
                  row(d), modspec(0), modspec(1),
                  row(d), modspec(0), modspec(1),
                  pl.BlockSpec((d, nq), lambda i: (0, 0)),
                  pl.BlockSpec((d, nkv), lambda i: (0, 0)),
                  row(LANES), row(LANES), pl.BlockSpec(invf.shape, lambda i: (0, 0))],
        out_specs=(pl.BlockSpec((tm, nq), lambda i: (i, 0)),
                   pl.BlockSpec((tm, LANES), lambda i: (i, 0)),
                   pl.BlockSpec((tm, LANES), lambda i: (i, 0))),
        compiler_params=_cparams(("parallel",)),
        name="swa_qkv_proj",
    )(x2, pos2.reshape(t // tm, 1, tm), gq, mod4, mod4, gkv, kvmod4, kvmod4, wq_bf, wkv_bf, qg2, kg2, invf)


def _attn_body(sink_ref, q_ref, kc_ref, kp_ref, vc_ref, vp_ref, w_ref, x_ref, g_ref, o_ref):
    n = pl.program_id(1)
    w = WINDOW
    from_prev = (lax.broadcasted_iota(jnp.int32, (w, w), 1) > lax.broadcasted_iota(jnp.int32, (w, w), 0))
    prev_fill = jnp.where(n > 0, 0.0, NEG_INF)
    hd = B_HEAD_DIM
    lane = lax.broadcasted_iota(jnp.int32, (2 * w, LANES), 1)
    first = lane < hd
    kfull = jnp.concatenate([kp_ref[...], kc_ref[...]], axis=0).astype(F32)
    vfull = jnp.concatenate([vp_ref[...], vc_ref[...]], axis=0).astype(F32)
    placed = {}
    for g in range(B_KV_HEADS):
        for name, full in (("k", kfull), ("v", vfull)):
            swapped = pltpu.roll(full, hd, 1)
            own_first = full if g == 0 else swapped
            own_second = swapped if g == 0 else full
            placed[name, g, 0] = jnp.where(first, own_first, 0.0).astype(BF16)
            placed[name, g, 1] = jnp.where(first, 0.0, own_second).astype(BF16)
    pairs = B_Q_HEADS // 2
    heads = [(p, side) for p in range(pairs) for side in range(2)]
    scores = []
    for p, side in heads:
        g = (2 * p) // B_GROUP
        q_pair = q_ref[:, p * LANES:(p + 1) * LANES]
        s = lax.dot_general(q_pair, placed["k", g, side], (((1,), (1,)), ((), ())), preferred_element_type=F32)
        scores.append(jnp.where(from_prev, s[:, :w] + prev_fill, s[:, w:]))
    probs, denoms = [], []
    for (p, side), s in zip(heads, scores):
        sink = sink_ref[0, 2 * p + side]
        m = jnp.maximum(jnp.max(s, axis=-1, keepdims=True), sink)
        e = jnp.exp(s - m)
        denoms.append(jnp.sum(e, axis=-1, keepdims=True) + jnp.exp(sink - m))
        probs.append(jnp.concatenate([jnp.where(from_prev, e, 0.0), jnp.where(from_prev, 0.0, e)],
                                     axis=1).astype(BF16))
    outs = []
    for (p, side), pr, den in zip(heads, probs, denoms):
        g = (2 * p) // B_GROUP
        outs.append(jnp.dot(pr, placed["v", g, side], preferred_element_type=F32) / den)
    attn = jnp.concatenate([(outs[2 * p] + outs[2 * p + 1]).astype(BF16) for p in range(pairs)], axis=1)
    o_ref[...] = x_ref[...] + g_ref[...] * jnp.dot(attn, w_ref[...], preferred_element_type=F32)


def _attention_residual(q, k, v, sinks2, w_out_bf, x2, mod4, gate_slot, bsz, seq):
    t, nq = q.shape
    d = x2.shape[-1]
    w = WINDOW
    nb = seq // w
    cur = lambda b, n: (b * nb + n, 0)
    prev = lambda b, n: (b * nb + jnp.maximum(n - 1, 0), 0)
    return pl.pallas_call(
        _attn_body,
        out_shape=jax.ShapeDtypeStruct((t, d), F32),
        grid=(bsz, nb),
        in_specs=[pl.BlockSpec(memory_space=pltpu.SMEM),
                  pl.BlockSpec((w, nq), cur),
                  pl.BlockSpec((w, LANES), cur), pl.BlockSpec((w, LANES), prev),
                  pl.BlockSpec((w, LANES), cur), pl.BlockSpec((w, LANES), prev),
                  pl.BlockSpec((nq, d), lambda b, n: (0, 0)),
                  pl.BlockSpec((w, d), cur),
                  pl.BlockSpec((None, None, 1, d), lambda b, n: (b, gate_slot, 0, 0))],
        out_specs=pl.BlockSpec((w, d), cur),
        compiler_params=_cparams(("parallel", "parallel")),
        name="swa_sink_attention",
    )(sinks2, q, k, k, v, v, w_out_bf, x2, mod4)


def kernel(x, c, positions, ada_w, ada_b, norm_gain, a_w_in, a_conv, a_log, a_dt_bias, a_out_gain, a_w_out,
           kv_ada_w, kv_ada_b, kv_norm_gain, kv_w, k_norm_gain, b_w_q, q_norm_gain, b_sinks, b_w_out,
           router_w, router_b, up_w, up_b, down_w, down_b):
    bsz, seq, d = x.shape
    t = bsz * seq
    depth = ada_w.shape[0]
    x2 = x.reshape(t, d)
    ada_b3 = ada_b.reshape(depth, 1, 6 * d)

    for layer in range(depth):
        mod4 = _modulation(c, ada_w, ada_b3, layer).reshape(bsz, 6, 1, d)
        gain1 = norm_gain[layer, 0].reshape(1, d)
        gain2 = norm_gain[layer, 1].reshape(1, d)
        if layer < N_A_LAYERS:
            w_in = jnp.pad(a_w_in[layer].astype(BF16), ((0, 0), (0, A_PROJ_PAD - a_w_in.shape[-1])))
            proj = _in_proj(x2, gain1, mod4, w_in, a_conv[layer], seq)
            o = _gdn(proj, a_log[layer].reshape(1, A_HEADS),
                     a_dt_bias[layer].reshape(1, A_HEADS), a_out_gain[layer].reshape(1, A_HEAD_DIM),
                     bsz, seq)
            x2 = _out_proj_residual(o, a_w_out[layer].astype(BF16), x2, mod4, 2, seq)
        else:
            j = layer - N_A_LAYERS
            kvmod4 = _modulation(c, kv_ada_w.reshape(1, d, 2 * d), kv_ada_b.reshape(1, 1, 2 * d), 0)
            kvmod4 = kvmod4.reshape(bsz, 2, 1, d)
            inv_freq = ROPE_THETA ** (-np.arange(0, ROPE_DIM, 2, dtype=np.float32) / ROPE_DIM)
            invf = jnp.asarray(inv_freq.astype(np.float32).reshape(ROPE_DIM // 2, 1))
            q, k, v = _qkv(x2, positions.reshape(t), gain1, mod4, kv_norm_gain.reshape(1, d), kvmod4,
                           b_w_q[j].astype(BF16), kv_w.astype(BF16),
                           jnp.tile(q_norm_gain[j], 2).reshape(1, LANES),
                           jnp.tile(k_norm_gain, 2).reshape(1, LANES), invf, seq)
            x2 = _attention_residual(q, k, v, b_sinks[j].reshape(1, B_Q_HEADS), b_w_out[j].astype(BF16),
                                     x2, mod4, 2, bsz, seq)
        x2 = _moe_layer(x2, gain2, mod4, router_w, router_b, up_w, up_b, down_w, down_b, layer, seq)
    return x2.reshape(bsz, seq, d)
```

```python
import dataclasses
import functools

import jax
import jax.numpy as jnp
import numpy as np
from jax import lax
from jax.experimental import pallas as pl
from jax.experimental.pallas import tpu as pltpu
from jax.experimental.pallas import tpu_sc as plsc

F32 = jnp.float32
BF16 = jnp.bfloat16
HIGHEST = lax.Precision.HIGHEST

N_A_LAYERS = 1

A_HEAD_DIM = 128
A_HEADS = 8
A_WIDTH = 1024
A_CONV = 4
A_CHUNK = 64
A_PROJ_PAD = 4608

B_HEAD_DIM = 64
B_Q_HEADS = 16
B_KV_HEADS = 2
B_GROUP = 8
WINDOW = 128
ROPE_DIM = 16
ROPE_THETA = 500000.0

N_EXPERTS = 32
TOP_K = 4
D_FF = 1024
SWIGLU_LIMIT = 7.0
SWIGLU_ALPHA = 1.702
MOE_ROWS = 512
MOE_FIRST_PART = 4

EPS = 1e-6
LANES = 128
NEG_INF = float("-inf")

VMEM_LIMIT = 56 * 1024 * 1024


def _cparams(sem, vmem=VMEM_LIMIT):
    return pltpu.CompilerParams(dimension_semantics=sem, vmem_limit_bytes=vmem)


def _silu(x):
    return x * jax.nn.sigmoid(x)


HI_HALF = -65536


def _pack_bf16_pairs(x):
    n = x.shape[-1] // 2
    bits = lax.bitcast_convert_type(x.astype(BF16).astype(F32), jnp.int32)
    return jnp.bitwise_or(lax.shift_right_logical(bits[:, :n], 16), jnp.bitwise_and(bits[:, n:], HI_HALF))


def _unpack_bf16_pairs(w):
    lo = lax.bitcast_convert_type(lax.shift_left(w, 16), F32)
    hi = lax.bitcast_convert_type(jnp.bitwise_and(w, HI_HALF), F32)
    return lo, hi


def _ada_norm(x, gain, shift, scale):
    y = x * lax.rsqrt(jnp.mean(x * x, axis=-1, keepdims=True) + EPS)
    return (y * gain) * (1.0 + scale) + shift


def _mod_body(c_ref, w_ref, b_ref, o_ref):
    o_ref[...] = jnp.dot(_silu(c_ref[...]), w_ref[...], preferred_element_type=F32,
                         precision=HIGHEST) + b_ref[...]


def _modulation(c, w3, b3, layer):
    bsz, d = c.shape
    n = w3.shape[-1]
    tn = 1024
    return pl.pallas_call(
        _mod_body,
        out_shape=jax.ShapeDtypeStruct((bsz, n), F32),
        grid=(n // tn,),
        in_specs=[pl.BlockSpec((bsz, d), lambda j: (0, 0)),
                  pl.BlockSpec((None, d, tn), lambda j: (layer, 0, j)),
                  pl.BlockSpec((None, 1, tn), lambda j: (layer, 0, j))],
        out_specs=pl.BlockSpec((bsz, tn), lambda j: (0, j)),
        compiler_params=_cparams(("arbitrary",)),
        name="adaln_mod",
    )(c, w3, b3)


INPROJ_TM = 256
INPROJ_TN = 256
HALO = 8


def _inproj_body(x_ref, gain_ref, sh_ref, sc_ref, w_ref, conv_ref, o_ref, ext, *, tiles_per_seq):
    i = pl.program_id(0)
    tm = x_ref.shape[0]
    hd = A_HEAD_DIM
    n = o_ref.shape[-1]
    tn = INPROJ_TN
    h = _ada_norm(x_ref[...], gain_ref[...], sh_ref[...], sc_ref[...]).astype(BF16)

    @pl.when(i % tiles_per_seq == 0)
    def _():
        ext[0:HALO, :] = jnp.zeros((HALO, ext.shape[-1]), F32)

    def conv_silu_norm(raw, n0):
        cols = slice(n0, n0 + tn)
        ext[HALO:HALO + tm, cols] = raw
        acc = raw * conv_ref[A_CONV - 1:A_CONV, cols]
        for s in range(1, A_CONV):
            acc = acc + ext[HALO - s:HALO - s + tm, cols] * conv_ref[A_CONV - 1 - s:A_CONV - s, cols]
        ext[0:HALO, cols] = raw[tm - HALO:, :]
        y = _silu(acc)
        if n0 >= 2 * A_WIDTH:
            return y
        scale = hd ** -0.5 if n0 < A_WIDTH else 1.0
        heads = []
        for c0 in range(0, tn, hd):
            yh = y[:, c0:c0 + hd]
            heads.append(yh * (lax.rsqrt(jnp.sum(yh * yh, axis=-1, keepdims=True) + EPS) * scale))
        return jnp.concatenate(heads, axis=1)

    matmul = lambda n0: jnp.dot(h, w_ref[:, n0:n0 + tn], preferred_element_type=F32)
    conv_chunks = list(range(0, 3 * A_WIDTH, tn))
    plain_chunks = list(range(3 * A_WIDTH, n, tn))
    every = max(1, len(conv_chunks) // max(1, len(plain_chunks)))
    raw = matmul(conv_chunks[0])
    for idx, n0 in enumerate(conv_chunks):
        nxt = matmul(conv_chunks[idx + 1]) if idx + 1 < len(conv_chunks) else None
        if idx % every == every - 1 and plain_chunks:
            p0 = plain_chunks.pop(0)
            o_ref[:, p0:p0 + tn] = matmul(p0)
        o_ref[:, n0:n0 + tn] = conv_silu_norm(raw, n0)
        raw = nxt
    for p0 in plain_chunks:
        o_ref[:, p0:p0 + tn] = matmul(p0)


def _in_proj(x2, gain, mod4, w_bf, conv_w, seq):
    t, d = x2.shape
    n = w_bf.shape[-1]
    tm = INPROJ_TM
    tps = seq // tm
    return pl.pallas_call(
        functools.partial(_inproj_body, tiles_per_seq=tps),
        out_shape=jax.ShapeDtypeStruct((t, n), F32),
        grid=(t // tm,),
        in_specs=[pl.BlockSpec((tm, d), lambda i: (i, 0)),
                  pl.BlockSpec((1, d), lambda i: (0, 0)),
                  pl.BlockSpec((None, None, 1, d), lambda i: (i // tps, 0, 0, 0)),
                  pl.BlockSpec((None, None, 1, d), lambda i: (i // tps, 1, 0, 0)),
                  pl.BlockSpec((d, n), lambda i: (0, 0)),
                  pl.BlockSpec(conv_w.shape, lambda i: (0, 0))],
        out_specs=pl.BlockSpec((tm, n), lambda i: (i, 0)),
        scratch_shapes=[pltpu.VMEM((HALO + tm, conv_w.shape[-1]), F32)],
        compiler_params=_cparams(("arbitrary",)),
        name="gdn_in_proj",
    )(x2, gain, mod4, mod4, w_bf, conv_w)


GDN_HG = 2
GDN_RB = 256
GDN_SC = 128
GDN_SUB = 4


def _split_bf16(x):
    hi = x.astype(BF16)
    return hi, (x - hi.astype(F32)).astype(BF16)


def _mm3(a_hi, a_lo, b_hi, b_lo, dims=(((1,), (0,)), ((), ()))):
    dg = functools.partial(lax.dot_general, dimension_numbers=dims, preferred_element_type=F32)
    return dg(a_hi, b_hi) + (dg(a_lo, b_hi) + dg(a_hi, b_lo))


def _dot3(a, b, dims):
    return _mm3(*_split_bf16(a), *_split_bf16(b), dims)


def _gdn_body(alog_ref, dtb_ref, q_ref, k_ref, v_ref, z_ref, ab_ref, og_ref,
              o_ref, bs, gcs, mp, bc, qp, op):
    seq = q_ref.shape[0]
    hd = A_HEAD_DIM
    ch = A_CHUNK
    g_idx = pl.program_id(1)

    lane = lax.broadcasted_iota(jnp.int32, (GDN_RB, hd), 1)
    row = lax.broadcasted_iota(jnp.int32, (GDN_RB, hd), 0)
    row_in_chunk = row % ch

    sc_rows = GDN_SC
    per_sc = sc_rows // ch
    rb_per_step = sc_rows * GDN_SUB // GDN_RB
    head_cols = lambda hh: slice(hh * hd, (hh + 1) * hd)

    def prep_stages(j):
        for rbi in range(rb_per_step):
            r = j * rb_per_step + rbi
            start = pl.multiple_of(r * GDN_RB, GDN_RB)
            rows = pl.ds(start, GDN_RB)
            for hh in range(GDN_HG):
                head = g_idx * GDN_HG + hh
                ab = ab_ref[rows, :]
                a_col = jnp.sum(jnp.where(lane == head, ab, 0.0), axis=-1, keepdims=True)
                b_col = jnp.sum(jnp.where(lane == head + A_HEADS, ab, 0.0), axis=-1, keepdims=True)
                bs[hh, rows, :] = jnp.broadcast_to(jax.nn.sigmoid(b_col), (GDN_RB, hd))
                xa = a_col + dtb_ref[0, head]
                softplus = jnp.maximum(xa, 0.0) + jnp.log1p(jnp.exp(-jnp.abs(xa)))
                g = jnp.broadcast_to(-jnp.exp(alog_ref[0, head]) * softplus, (GDN_RB, hd))
                s = 1
                while s < ch:
                    g = g + jnp.where(row_in_chunk >= s, pltpu.roll(g, s, 0), 0.0)
                    s *= 2
                gcs[hh, rows, :] = g
                yield

    ri = lax.broadcasted_iota(jnp.int32, (sc_rows, sc_rows), 0)
    cj = lax.broadcasted_iota(jnp.int32, (sc_rows, sc_rows), 1)
    same_chunk = (ri // ch) == (cj // ch)
    incl = jnp.logical_and(same_chunk, ri >= cj)
    strict = jnp.logical_and(same_chunk, ri > cj)
    eye = jnp.where(ri == cj, 1.0, 0.0)
    chunk_of_col = lax.broadcasted_iota(jnp.int32, (hd, sc_rows), 1) // ch
    lanes_nt = (((1,), (1,)), ((), ()))
    plain = (((1,), (0,)), ((), ()))

    chunks_per_step = GDN_SUB * per_sc
    n_steps = seq // (sc_rows * GDN_SUB)

    def local_stages(i):
        chains = [(i * GDN_SUB + sub, hh) for sub in range(GDN_SUB) for hh in range(GDN_HG)]
        rows_of = lambda blk: pl.ds(pl.multiple_of(blk * sc_rows, sc_rows), sc_rows)

        def start(blk, hh):
            rows = rows_of(blk)
            k = k_ref[rows, head_cols(hh)]
            gc = gcs[hh, rows, :]
            decay = jnp.where(incl, jnp.exp(jnp.minimum(gc - gc.T[0:1, :], 0.0)), 0.0)
            k_bf = k.astype(BF16)
            kk = lax.dot_general((k * bs[hh, rows, :]).astype(BF16), k_bf, lanes_nt, preferred_element_type=F32)
            qk = lax.dot_general(q_ref[rows, head_cols(hh)].astype(BF16), k_bf, lanes_nt, preferred_element_type=F32)
            return jnp.where(strict, -(kk * decay), 0.0), (qk * decay).astype(BF16)

        started = [start(*c) for c in chains]
        yield
        qks = [s[1] for s in started]
        invs = [eye + s[0] for s in started]
        pws = [_dot3(s[0], s[0], plain) for s in started]
        yield
        for step in range(1, 6):
            for n in range(len(chains)):
                pw_hi, pw_lo = _split_bf16(pws[n])
                inv_hi, inv_lo = _split_bf16(invs[n])
                if step < 5:
                    prod = _mm3(jnp.concatenate([pw_hi, inv_hi], axis=0),
                                jnp.concatenate([pw_lo, inv_lo], axis=0), pw_hi, pw_lo)
                    pws[n] = prod[:sc_rows]
                    invs[n] = invs[n] + prod[sc_rows:]
                else:
                    invs[n] = invs[n] + _mm3(inv_hi, inv_lo, pw_hi, pw_lo)
            yield

        def solve(n, blk, hh):
            rows = rows_of(blk)
            beta = bs[hh, rows, :]
            rhs = jnp.concatenate([v_ref[rows, head_cols(hh)] * beta, k_ref[rows, head_cols(hh)] * beta * jnp.exp(gcs[hh, rows, :])], axis=1)
            return _dot3(invs[n], rhs, plain).astype(BF16)

        uws = [solve(n, *c) for n, c in enumerate(chains)]
        yield

        def finish(n, blk, hh):
            rows = rows_of(blk)
            k = k_ref[rows, head_cols(hh)]
            gc = gcs[hh, rows, :]
            res = jnp.dot(qks[n], uws[n], preferred_element_type=F32)
            op[hh, rows, :] = res[:, :hd]
            qp[hh, rows, :] = (q_ref[rows, head_cols(hh)] * jnp.exp(gc) - res[:, hd:]).astype(BF16)
            gl = jnp.concatenate(
                [jnp.broadcast_to(gc[(j + 1) * ch - 1:(j + 1) * ch, :], (ch, hd)) for j in range(per_sc)], axis=0)
            kt_t = (k * jnp.exp(gl - gc)).T
            for j in range(per_sc):
                kt_j = jnp.where(chunk_of_col == j, kt_t, 0.0).astype(BF16)
                bm = jnp.dot(kt_j, uws[n], preferred_element_type=F32)
                bc[hh, blk * per_sc + j] = bm[:, :hd]
                mp[hh, blk * per_sc + j] = bm[:, hd:].astype(BF16)

        for n, c in enumerate(chains):
            finish(n, *c)

    def scan_stages(j, states):
        for cc in range(chunks_per_step):
            c = j * chunks_per_step + cc
            r0 = pl.multiple_of(c * ch, ch)
            rows = pl.ds(r0, ch)
            for hh in range(GDN_HG):
                col0 = hh * hd
                state = states[hh]
                s_bf = state.astype(BF16)
                o = jnp.dot(qp[hh, rows, :], s_bf, preferred_element_type=F32) + op[hh, rows, :]
                g_tot = jnp.exp(gcs[hh, pl.ds(r0 + ch - 1, 1), :])
                states[hh] = state * g_tot - jnp.dot(mp[hh, c], s_bf, preferred_element_type=F32) + bc[hh, c]
                on = o * lax.rsqrt(jnp.mean(o * o, axis=-1, keepdims=True) + EPS) * og_ref[...]
                z = z_ref[rows, col0:col0 + hd]
                o_ref[rows, col0:col0 + hd] = (on * _silu(z)).astype(BF16)
            yield

    def interleave(*gens):
        live = list(gens)
        while live:
            for g in list(live):
                try:
                    next(g)
                except StopIteration:
                    live.remove(g)

    assert n_steps >= 2
    interleave(prep_stages(0))
    interleave(local_stages(0), prep_stages(1))

    def pipelined(i, states):
        states = list(states)
        interleave(local_stages(i), scan_stages(i - 1, states), prep_stages(i + 1))
        return tuple(states)

    states = list(lax.fori_loop(1, n_steps - 1, pipelined,
                                tuple(jnp.zeros((hd, hd), F32) for _ in range(GDN_HG))))
    interleave(local_stages(n_steps - 1), scan_stages(n_steps - 2, states))
    interleave(scan_stages(n_steps - 1, states))


def _gdn(proj, a_log, dt_bias, out_gain, bsz, seq):
    t = proj.shape[0]
    hd = A_HEAD_DIM
    wb = hd * GDN_HG
    ng = A_HEADS // GDN_HG
    per = A_WIDTH // wb
    n_chunks = seq // A_CHUNK
    smem = pl.BlockSpec(memory_space=pltpu.SMEM)
    seq_spec = lambda off: pl.BlockSpec((seq, wb), lambda b, g, off=off: (b, off + g))
    sc = lambda dt=F32: pltpu.VMEM((GDN_HG, seq, hd), dt)
    return pl.pallas_call(
        _gdn_body,
        out_shape=jax.ShapeDtypeStruct((t, A_WIDTH), BF16),
        grid=(bsz, ng),
        in_specs=[smem, smem,
                  seq_spec(0), seq_spec(per), seq_spec(2 * per), seq_spec(3 * per),
                  pl.BlockSpec((seq, LANES), lambda b, g: (b, 4 * A_WIDTH // LANES)),
                  pl.BlockSpec((1, hd), lambda b, g: (0, 0))],
        out_specs=pl.BlockSpec((seq, wb), lambda b, g: (b, g)),
        scratch_shapes=[sc(), sc(),
                        pltpu.VMEM((GDN_HG, n_chunks, hd, hd), BF16),
                        pltpu.VMEM((GDN_HG, n_chunks, hd, hd), F32),
                        sc(BF16), sc()],
        compiler_params=_cparams(("parallel", "parallel")),
        name="gdn_core",
    )(a_log, dt_bias, proj, proj, proj, proj, proj, out_gain)


def _oproj_body(a_ref, w_ref, x_ref, g_ref, o_ref):
    y = jnp.dot(a_ref[...], w_ref[...], preferred_element_type=F32)
    o_ref[...] = x_ref[...] + g_ref[...] * y


def _out_proj_residual(a_bf, w_bf, x2, mod4, gate_slot, seq):
    t, d = x2.shape
    kdim = a_bf.shape[-1]
    tm = 512
    tps = seq // tm
    return pl.pallas_call(
        _oproj_body,
        out_shape=jax.ShapeDtypeStruct((t, d), F32),
        grid=(t // tm,),
        in_specs=[pl.BlockSpec((tm, kdim), lambda i: (i, 0)),
                  pl.BlockSpec((kdim, d), lambda i: (0, 0)),
                  pl.BlockSpec((tm, d), lambda i: (i, 0)),
                  pl.BlockSpec((None, None, 1, d), lambda i: (i // tps, gate_slot, 0, 0))],
        out_specs=pl.BlockSpec((tm, d), lambda i: (i, 0)),
        compiler_params=_cparams(("parallel",)),
        name="out_proj_residual",
    )(a_bf, w_bf, x2, mod4)


ROUTER_TM = 1024


K_ROWS = 8


def _router_body(x_ref, gain_ref, sh_ref, sc_ref, rw_ref, rb_ref, h_ref, idx_ref, gate_ref, rank_ref,
                 cnt_ref, carry):
    i = pl.program_id(0)

    @pl.when(i == 0)
    def _():
        carry[...] = jnp.zeros_like(carry)

    tm = x_ref.shape[0]
    h = _ada_norm(x_ref[...], gain_ref[...], sh_ref[...], sc_ref[...])
    h_ref[...] = _pack_bf16_pairs(h)
    logits = _dot3(h, rw_ref[...], (((1,), (0,)), ((), ()))) + rb_ref[...]
    work = logits.T[:N_EXPERTS, :]
    expert = lax.broadcasted_iota(jnp.int32, (N_EXPERTS, tm), 0)
    tops, sels = [], []
    for _ in range(TOP_K):
        m = jnp.max(work, axis=0, keepdims=True)
        sel = jnp.min(jnp.where(work == m, expert, N_EXPERTS), axis=0, keepdims=True)
        work = jnp.where(expert == sel, NEG_INF, work)
        tops.append(m)
        sels.append(sel)
    exps = [jnp.exp(m - tops[0]) for m in tops]
    denom = exps[0] + exps[1] + exps[2] + exps[3]
    onehot = jnp.zeros((N_EXPERTS, tm), F32)
    for sel in sels:
        onehot = onehot + jnp.where(expert == sel, 1.0, 0.0)
    ti = lax.broadcasted_iota(jnp.int32, (tm, tm), 0)
    tj = lax.broadcasted_iota(jnp.int32, (tm, tm), 1)
    earlier = (ti < tj).astype(BF16)
    before = jnp.dot(onehot.astype(BF16), earlier, preferred_element_type=F32) + carry[:, 0:1]
    ranks = [jnp.sum(jnp.where(expert == sel, before, 0.0), axis=0, keepdims=True) for sel in sels]
    pad_i = jnp.zeros((K_ROWS - TOP_K, tm), jnp.int32)
    idx_ref[...] = jnp.concatenate(sels + [pad_i], axis=0)
    gate_ref[...] = jnp.concatenate([e / denom for e in exps] + [pad_i.astype(F32)], axis=0)
    rank_ref[...] = jnp.concatenate([r.astype(jnp.int32) for r in ranks] + [pad_i], axis=0)
    carry[...] = carry[...] + jnp.sum(onehot, axis=1, keepdims=True)
    cnt_ref[...] = carry[...]


def _router(x2, gain, mod4, sh_slot, sc_slot, rw_pad, rb_pad, seq):
    t, d = x2.shape
    tm = ROUTER_TM
    tps = seq // tm
    tok = lambda dt: jax.ShapeDtypeStruct((K_ROWS, t), dt)
    tok_spec = pl.BlockSpec((K_ROWS, tm), lambda i: (0, i))
    cnt_spec = pl.BlockSpec((N_EXPERTS, LANES), lambda i: (0, 0))
    return pl.pallas_call(
        _router_body,
        out_shape=(jax.ShapeDtypeStruct((t, d // 2), jnp.int32), tok(jnp.int32), tok(F32), tok(jnp.int32),
                   jax.ShapeDtypeStruct((N_EXPERTS, LANES), F32)),
        grid=(t // tm,),
        in_specs=[pl.BlockSpec((tm, d), lambda i: (i, 0)),
                  pl.BlockSpec((1, d), lambda i: (0, 0)),
                  pl.BlockSpec((None, None, 1, d), lambda i: (i // tps, sh_slot, 0, 0)),
                  pl.BlockSpec((None, None, 1, d), lambda i: (i // tps, sc_slot, 0, 0)),
                  pl.BlockSpec((d, LANES), lambda i: (0, 0)),
                  pl.BlockSpec((1, LANES), lambda i: (0, 0))],
        out_specs=(pl.BlockSpec((tm, d // 2), lambda i: (i, 0)), tok_spec, tok_spec, tok_spec, cnt_spec),
        scratch_shapes=[pltpu.VMEM((N_EXPERTS, LANES), F32)],
        compiler_params=_cparams(("arbitrary",)),
        name="moe_router",
    )(x2, gain, mod4, mod4, rw_pad, rb_pad)


SC_LANES = 16
SC_INDEX_CHUNK = 8192


def _sc_row_tokens(dest_flat, n_rows, n_tok):
    n_assign = dest_flat.shape[0]
    assert n_rows % SC_LANES == 0 and n_assign % SC_INDEX_CHUNK == 0
    n_cores = plsc.get_sparse_core_info().num_cores
    mesh = plsc.VectorSubcoreMesh(core_axis_name="c", subcore_axis_name="s")

    @functools.partial(
        pl.kernel, mesh=mesh,
        out_type=jax.ShapeDtypeStruct((n_rows,), jnp.int32),
        scratch_types=[pltpu.VMEM((n_rows,), jnp.int32), pltpu.VMEM((SC_INDEX_CHUNK,), jnp.int32)],
        compiler_params=dataclasses.replace(pltpu.CompilerParams(), needs_layout_passes=False),
    )
    def row_token_kernel(dest_hbm, out_hbm, rt_v, d_v):
        wid = lax.axis_index("s") * n_cores + lax.axis_index("c")

        @pl.when(wid == 0)
        def _():
            lanes = lax.iota(jnp.int32, SC_LANES)

            @pl.loop(0, n_rows // SC_LANES)
            def _(i):
                rt_v[pl.ds(i * SC_LANES, SC_LANES)] = lax.rem(lanes + i * SC_LANES, n_tok)

            @pl.loop(0, n_assign // SC_INDEX_CHUNK)
            def _(c):
                pltpu.sync_copy(dest_hbm.at[pl.ds(c * SC_INDEX_CHUNK, SC_INDEX_CHUNK)], d_v)

                @pl.loop(0, SC_INDEX_CHUNK // SC_LANES)
                def _(i):
                    idx = d_v[pl.ds(i * SC_LANES, SC_LANES)]
                    tok = lax.rem(lanes + (c * SC_INDEX_CHUNK + i * SC_LANES), n_tok)
                    plsc.store_scatter(rt_v, [idx], tok)

            pltpu.sync_copy(rt_v, out_hbm)

    return row_token_kernel(dest_flat)


def _ffn_body(be_ref, nu_ref, nxt_ref, slot_ref, x_ref, uw_hbm, ub_ref, dw_hbm, db_ref, *rest, layer):
    y_ref, uw_f32, dw_f32, uw_bf, dw_bf, sems = rest[-6:]
    i = pl.program_id(0)
    first = jnp.logical_and(jnp.logical_or(i == 0, be_ref[i] != be_ref[jnp.maximum(i - 1, 0)]), i < nu_ref[0])
    active = i < nu_ref[0]
    slot = slot_ref[i]

    def weight_copies(expert, s):
        return (pltpu.make_async_copy(uw_hbm.at[layer, expert], uw_f32.at[s], sems.at[s, 0]),
                pltpu.make_async_copy(dw_hbm.at[layer, expert], dw_f32.at[s], sems.at[s, 1]))

    @pl.when(jnp.logical_and(i == 0, active))
    def _():
        for cp in weight_copies(be_ref[0], 0):
            cp.start()

    @pl.when(first)
    def _():
        for cp in weight_copies(be_ref[i], slot):
            cp.wait()

        @pl.when(nxt_ref[i] >= 0)
        def _():
            for cp in weight_copies(nxt_ref[i], 1 - slot):
                cp.start()

        rows = 128
        for r0 in range(0, uw_bf.shape[0], rows):
            uw_bf[r0:r0 + rows, :] = uw_f32[slot, r0:r0 + rows, :].astype(BF16)
        for r0 in range(0, dw_bf.shape[0], rows):
            dw_bf[r0:r0 + rows, :] = dw_f32[slot, r0:r0 + rows, :].astype(BF16)

    @pl.when(active)
    def _():
        x = jnp.concatenate(_unpack_bf16_pairs(x_ref[...]), axis=1).astype(BF16)
        gu = jnp.dot(x, uw_bf[...], preferred_element_type=F32) + ub_ref[...]
        gate = jnp.minimum(gu[:, :D_FF], SWIGLU_LIMIT)
        lin = jnp.clip(gu[:, D_FF:], -SWIGLU_LIMIT, SWIGLU_LIMIT)
        act = gate * jax.nn.sigmoid(SWIGLU_ALPHA * gate) * (lin + 1.0)
        y = jnp.dot(act.astype(BF16), dw_bf[...], preferred_element_type=F32) + db_ref[...]
        y_ref[...] = _pack_bf16_pairs(y)

    @pl.when(jnp.logical_not(active))
    def _():
        y_ref[...] = jnp.zeros_like(y_ref)


def _expert_ffn(xb, blk_expert, n_used, up_w, up_b4, down_w, down_b4, layer, n_rows_total, first_blk, y_prev):
    n_rows, dp = xb.shape
    d = 2 * dp
    bm = MOE_ROWS
    n_blk = n_rows // bm
    f2 = up_w.shape[-1]
    blk = jnp.arange(n_blk, dtype=jnp.int32)
    first = jnp.logical_and(jnp.concatenate([jnp.ones((1,), bool), blk_expert[1:] != blk_expert[:-1]]),
                            blk < n_used[0])
    slot = ((jnp.cumsum(first.astype(jnp.int32)) - 1) % 2).astype(jnp.int32)
    later_first = jnp.where(first, blk, n_blk)[::-1]
    next_first = jnp.concatenate([lax.cummin(later_first)[::-1][1:], jnp.full((1,), n_blk, jnp.int32)])
    nxt = jnp.where(jnp.logical_and(first, next_first < n_blk),
                    blk_expert[jnp.minimum(next_first, n_blk - 1)], -1).astype(jnp.int32)
    grid_spec = pltpu.PrefetchScalarGridSpec(
        num_scalar_prefetch=4,
        grid=(n_blk,),
        in_specs=[pl.BlockSpec((bm, dp), lambda i, be, nu, nx, sl: (i, 0)),
                  pl.BlockSpec(memory_space=pl.ANY),
                  pl.BlockSpec((None, None, 1, f2), lambda i, be, nu, nx, sl: (layer, be[i], 0, 0)),
                  pl.BlockSpec(memory_space=pl.ANY),
                  pl.BlockSpec((None, None, 1, d), lambda i, be, nu, nx, sl: (layer, be[i], 0, 0))]
                 + ([] if y_prev is None else [pl.BlockSpec(memory_space=pl.ANY)]),
        out_specs=pl.BlockSpec((bm, dp), lambda i, be, nu, nx, sl: (i + first_blk, 0)),
        scratch_shapes=[pltpu.VMEM((2, d, f2), F32), pltpu.VMEM((2, f2 // 2, d), F32),
                        pltpu.VMEM((d, f2), BF16), pltpu.VMEM((f2 // 2, d), BF16),
                        pltpu.SemaphoreType.DMA((2, 2))],
    )
    args = (blk_expert, n_used, nxt, slot, xb, up_w, up_b4, down_w, down_b4)
    return pl.pallas_call(
        functools.partial(_ffn_body, layer=layer),
        out_shape=jax.ShapeDtypeStruct((n_rows_total, dp), jnp.int32),
        grid_spec=grid_spec,
        input_output_aliases={} if y_prev is None else {len(args): 0},
        compiler_params=_cparams(("arbitrary",)),
        name="moe_expert_ffn",
    )(*args, *(() if y_prev is None else (y_prev,)))


SC_GATHER_ROWS = 64


def _sc_gather_rows(table, idx):
    n, d = idx.shape[0], table.shape[1]
    info = plsc.get_sparse_core_info()
    n_cores, n_sub = info.num_cores, info.num_subcores
    n_workers = n_cores * n_sub
    chunk = SC_GATHER_ROWS
    per_worker = n // n_workers
    n_chunks = per_worker // chunk
    assert n_chunks * chunk * n_workers == n and n_chunks % 2 == 0
    mesh = plsc.VectorSubcoreMesh(core_axis_name="c", subcore_axis_name="s")

    @functools.partial(
        pl.kernel, mesh=mesh,
        out_type=jax.ShapeDtypeStruct((n, d), table.dtype),
        scratch_types=[pltpu.VMEM((n_chunks, chunk), jnp.int32), pltpu.VMEM((2, chunk, d), table.dtype),
                       pltpu.SemaphoreType.DMA((2,))],
    )
    def gather_kernel(table_hbm, idx_hbm, out_hbm, idx_v, rows_v, sems):
        wid = lax.axis_index("s") * n_cores + lax.axis_index("c")
        base = wid * per_worker
        pltpu.sync_copy(idx_hbm.at[wid], idx_v)

        def gather(ci, slot):
            return pltpu.make_async_copy(table_hbm.at[idx_v.at[ci]], rows_v.at[slot], sems.at[slot])

        gather(0, 0).start()

        @pl.loop(0, n_chunks, step=2)
        def _(c0):
            for slot in range(2):
                ci = c0 + slot
                gather(ci, slot).wait()

                @pl.when(ci + 1 < n_chunks)
                def _():
                    gather(ci + 1, 1 - slot).start()

                pltpu.sync_copy(rows_v.at[slot], out_hbm.at[pl.ds(base + ci * chunk, chunk)])

    return gather_kernel(table, idx.reshape(n_workers, n_chunks, chunk))


COMBINE_TM = 512


def _combine_body(gate_ref, x_ref, g_ref, y_ref, o_ref):
    gates = gate_ref[...]
    acc_lo = acc_hi = None
    for k in range(TOP_K):
        lo, hi = _unpack_bf16_pairs(y_ref[k])
        gk = gates[:, k:k + 1]
        acc_lo = gk * lo if acc_lo is None else acc_lo + gk * lo
        acc_hi = gk * hi if acc_hi is None else acc_hi + gk * hi
    o_ref[...] = x_ref[...] + g_ref[...] * jnp.concatenate([acc_lo, acc_hi], axis=1)


def _combine(y_kt, gates, x2, mod4, gate_slot, seq):
    t, d = x2.shape
    tm = COMBINE_TM
    tps = seq // tm
    return pl.pallas_call(
        _combine_body,
        out_shape=jax.ShapeDtypeStruct((t, d), F32),
        grid=(t // tm,),
        in_specs=[pl.BlockSpec((tm, gates.shape[1]), lambda i: (i, 0)),
                  pl.BlockSpec((tm, d), lambda i: (i, 0)),
                  pl.BlockSpec((None, None, 1, d), lambda i: (i // tps, gate_slot, 0, 0)),
                  pl.BlockSpec((TOP_K, tm, d // 2), lambda i: (0, i, 0))],
        out_specs=pl.BlockSpec((tm, d), lambda i: (i, 0)),
        compiler_params=_cparams(("parallel",)),
        name="moe_combine",
    )(gates, x2, mod4, y_kt)


def _moe_layer(x2, gain, mod4, router_w, router_b, up_w, up_b, down_w, down_b, layer, seq):
    t, d = x2.shape
    rw_pad = jnp.pad(router_w[layer], ((0, 0), (0, LANES - N_EXPERTS)))
    rb_pad = jnp.pad(router_b[layer], (0, LANES - N_EXPERTS)).reshape(1, LANES)
    h2, idx, gates, rank, counts = _router(x2, gain, mod4, 3, 4, rw_pad, rb_pad, seq)

    bm = MOE_ROWS
    counts = counts[:, 0].astype(jnp.int32)
    padded = (counts + bm - 1) // bm * bm
    p_ends = jnp.cumsum(padded)
    p_starts = p_ends - padded
    n_rows = t * TOP_K + N_EXPERTS * bm
    n_blk = n_rows // bm
    onehot = idx[:TOP_K, :, None] == jnp.arange(N_EXPERTS, dtype=jnp.int32)[None, None, :]
    dest = jnp.sum(jnp.where(onehot, p_starts[None, None, :], 0), axis=-1) + rank[:TOP_K]
    dest_flat = dest.reshape(TOP_K * t)
    blk_start = jnp.arange(n_blk, dtype=jnp.int32) * bm
    blk_expert = jnp.sum((blk_start[:, None] >= p_ends[None, :]).astype(jnp.int32), axis=-1)
    blk_expert = jnp.minimum(blk_expert, N_EXPERTS - 1).astype(jnp.int32)
    n_used = (p_ends[-1:] // bm).astype(jnp.int32)

    row_token = _sc_row_tokens(dest_flat, n_rows, t)
    f2 = up_w.shape[-1]
    up_b4 = up_b.reshape(up_b.shape[0], N_EXPERTS, 1, f2)
    down_b4 = down_b.reshape(down_b.shape[0], N_EXPERTS, 1, d)
    bounds = (0, n_blk // MOE_FIRST_PART, n_blk)
    xb_parts = [_sc_gather_rows(h2, row_token[lo * bm:hi * bm]) for lo, hi in zip(bounds[:-1], bounds[1:])]
    y_rows = None
    for lo, hi, xb in zip(bounds[:-1], bounds[1:], xb_parts):
        y_rows = _expert_ffn(xb, blk_expert[lo:hi], jnp.clip(n_used - lo, 0, hi - lo), up_w, up_b4, down_w,
                             down_b4, layer, n_rows, lo, y_rows)
    y_kt = _sc_gather_rows(y_rows, dest_flat).reshape(TOP_K, t, d // 2)
    return _combine(y_kt, gates.T, x2, mod4, 5, seq)


QKV_TM = 256


def _rope_tables(pos_row, invf_col):
    tm = pos_row.shape[1]
    half = ROPE_DIM // 2
    ang = invf_col * pos_row
    expand = (lax.broadcasted_iota(jnp.int32, (half, LANES), 1) % half
              == lax.broadcasted_iota(jnp.int32, (half, LANES), 0)).astype(BF16)

    def to_lanes(x):
        hi = x.astype(BF16)
        rest = x - hi.astype(F32)
        mid = rest.astype(BF16)
        parts = (hi, mid, (rest - mid.astype(F32)).astype(BF16))
        return sum(lax.dot_general(p, expand, (((0,), (0,)), ((), ())), preferred_element_type=F32) for p in parts)

    c = to_lanes(jnp.cos(ang))
    s = to_lanes(jnp.sin(ang))
    d = lax.broadcasted_iota(jnp.int32, (tm, LANES), 1) % B_HEAD_DIM
    cos_t = jnp.where(d < ROPE_DIM, c, 1.0)
    sin_lo = jnp.where(d < half, -s, 0.0)
    sin_hi = jnp.where(jnp.logical_and(d >= half, d < ROPE_DIM), s, 0.0)
    return cos_t, sin_lo, sin_hi


def _head_norm_rope(x, gain_row, tables):
    cos_t, sin_lo, sin_hi = tables
    same_head = (lax.broadcasted_iota(jnp.int32, (LANES, LANES), 0) // B_HEAD_DIM
                 == lax.broadcasted_iota(jnp.int32, (LANES, LANES), 1) // B_HEAD_DIM).astype(BF16)
    sq_hi, sq_lo = _split_bf16(x * x)
    ss = (jnp.dot(sq_hi, same_head, preferred_element_type=F32)
          + jnp.dot(sq_lo, same_head, preferred_element_type=F32))
    xn = x * lax.rsqrt(ss * (1.0 / B_HEAD_DIM) + EPS) * gain_row
    half = ROPE_DIM // 2
    return (xn * cos_t + pltpu.roll(xn, LANES - half, 1) * sin_lo + pltpu.roll(xn, half, 1) * sin_hi)


def _qkv_body(x_ref, pos_ref, gq_ref, shq_ref, scq_ref, gkv_ref, shkv_ref, sckv_ref, wq_ref, wkv_ref,
              qg_ref, kg_ref, invf_ref, q_ref, k_ref, v_ref):
    x = x_ref[...]
    y = x * lax.rsqrt(jnp.mean(x * x, axis=-1, keepdims=True) + EPS)
    hq = ((y * gq_ref[...]) * (1.0 + scq_ref[...]) + shq_ref[...]).astype(BF16)
    hkv = ((y * gkv_ref[...]) * (1.0 + sckv_ref[...]) + shkv_ref[...]).astype(BF16)
    tables = _rope_tables(pos_ref[...].astype(F32), invf_ref[...])
    kv = jnp.dot(hkv, wkv_ref[...], preferred_element_type=F32)
    k_ref[...] = _head_norm_rope(kv[:, :LANES], kg_ref[...], tables).astype(BF16)
    v_ref[...] = kv[:, LANES:].astype(BF16)
    q = jnp.dot(hq, wq_ref[...], preferred_element_type=F32)
    scale = B_HEAD_DIM ** -0.5
    for p in range(q.shape[-1] // LANES):
        qp = _head_norm_rope(q[:, p * LANES:(p + 1) * LANES], qg_ref[...], tables)
        q_ref[:, p * LANES:(p + 1) * LANES] = (qp * scale).astype(BF16)


def _qkv(x2, pos2, gq, mod4, gkv, kvmod4, wq_bf, wkv_bf, qg2, kg2, invf, seq):
    t, d = x2.shape
    tm = QKV_TM
    tps = seq // tm
    nq = wq_bf.shape[-1]
    nkv = wkv_bf.shape[-1]
    row = lambda n: pl.BlockSpec((1, n), lambda i: (0, 0))
    modspec = lambda slot: pl.BlockSpec((None, None, 1, d), lambda i, slot=slot: (i // tps, slot, 0, 0))
    return pl.pallas_call(
        _qkv_body,
        out_shape=(jax.ShapeDtypeStruct((t, nq), BF16), jax.ShapeDtypeStruct((t, LANES), BF16),
                   jax.ShapeDtypeStruct((t, LANES), BF16)),
        grid=(t // tm,),
        in_specs=[pl.BlockSpec((tm, d), lambda i: (i, 0)),
                  pl.BlockSpec((None, 1, tm), lambda i: (i, 0, 0)),
                  row(d), modspec(0), modspec(1),
                  row(d), modspec(0), modspec(1),
                  pl.BlockSpec((d, nq), lambda i: (0, 0)),
                  pl.BlockSpec((d, nkv), lambda i: (0, 0)),
                  row(LANES), row(LANES), pl.BlockSpec(invf.shape, lambda i: (0, 0))],
        out_specs=(pl.BlockSpec((tm, nq), lambda i: (i, 0)),
                   pl.BlockSpec((tm, LANES), lambda i: (i, 0)),
                   pl.BlockSpec((tm, LANES), lambda i: (i, 0))),
        compiler_params=_cparams(("parallel",)),
        name="swa_qkv_proj",
    )(x2, pos2.reshape(t // tm, 1, tm), gq, mod4, mod4, gkv, kvmod4, kvmod4, wq_bf, wkv_bf, qg2, kg2, invf)


def _attn_body(sink_ref, q_ref, kc_ref, kp_ref, vc_ref, vp_ref, w_ref, x_ref, g_ref, o_ref):
    n = pl.program_id(1)
    w = WINDOW
    from_prev = (lax.broadcasted_iota(jnp.int32, (w, w), 1) > lax.broadcasted_iota(jnp.int32, (w, w), 0))
    prev_fill = jnp.where(n > 0, 0.0, NEG_INF)
    hd = B_HEAD_DIM
    lane = lax.broadcasted_iota(jnp.int32, (2 * w, LANES), 1)
    first = lane < hd
    kfull = jnp.concatenate([kp_ref[...], kc_ref[...]], axis=0).astype(F32)
    vfull = jnp.concatenate([vp_ref[...], vc_ref[...]], axis=0).astype(F32)
    placed = {}
    for g in range(B_KV_HEADS):
        for name, full in (("k", kfull), ("v", vfull)):
            swapped = pltpu.roll(full, hd, 1)
            own_first = full if g == 0 else swapped
            own_second = swapped if g == 0 else full
            placed[name, g, 0] = jnp.where(first, own_first, 0.0).astype(BF16)
            placed[name, g, 1] = jnp.where(first, 0.0, own_second).astype(BF16)
    pairs = B_Q_HEADS // 2
    heads = [(p, side) for p in range(pairs) for side in range(2)]
    scores = []
    for p, side in heads:
        g = (2 * p) // B_GROUP
        q_pair = q_ref[:, p * LANES:(p + 1) * LANES]
        s = lax.dot_general(q_pair, placed["k", g, side], (((1,), (1,)), ((), ())), preferred_element_type=F32)
        scores.append(jnp.where(from_prev, s[:, :w] + prev_fill, s[:, w:]))
    probs, denoms = [], []
    for (p, side), s in zip(heads, scores):
        sink = sink_ref[0, 2 * p + side]
        m = jnp.maximum(jnp.max(s, axis=-1, keepdims=True), sink)
        e = jnp.exp(s - m)
        denoms.append(jnp.sum(e, axis=-1, keepdims=True) + jnp.exp(sink - m))
        probs.append(jnp.concatenate([jnp.where(from_prev, e, 0.0), jnp.where(from_prev, 0.0, e)],
                                     axis=1).astype(BF16))
    outs = []
    for (p, side), pr, den in zip(heads, probs, denoms):
        g = (2 * p) // B_GROUP
        outs.append(jnp.dot(pr, placed["v", g, side], preferred_element_type=F32) / den)
    attn = jnp.concatenate([(outs[2 * p] + outs[2 * p + 1]).astype(BF16) for p in range(pairs)], axis=1)
    o_ref[...] = x_ref[...] + g_ref[...] * jnp.dot(attn, w_ref[...], preferred_element_type=F32)


def _attention_residual(q, k, v, sinks2, w_out_bf, x2, mod4, gate_slot, bsz, seq):
    t, nq = q.shape
    d = x2.shape[-1]
    w = WINDOW
    nb = seq // w
    cur = lambda b, n: (b * nb + n, 0)
    prev = lambda b, n: (b * nb + jnp.maximum(n - 1, 0), 0)
    return pl.pallas_call(
        _attn_body,
        out_shape=jax.ShapeDtypeStruct((t, d), F32),
        grid=(bsz, nb),
        in_specs=[pl.BlockSpec(memory_space=pltpu.SMEM),
                  pl.BlockSpec((w, nq), cur),
                  pl.BlockSpec((w, LANES), cur), pl.BlockSpec((w, LANES), prev),
                  pl.BlockSpec((w, LANES), cur), pl.BlockSpec((w, LANES), prev),
                  pl.BlockSpec((nq, d), lambda b, n: (0, 0)),
                  pl.BlockSpec((w, d), cur),
                  pl.BlockSpec((None, None, 1, d), lambda b, n: (b, gate_slot, 0, 0))],
        out_specs=pl.BlockSpec((w, d), cur),
        compiler_params=_cparams(("parallel", "parallel")),
        name="swa_sink_attention",
    )(sinks2, q, k, k, v, v, w_out_bf, x2, mod4)


def kernel(x, c, positions, ada_w, ada_b, norm_gain, a_w_in, a_conv, a_log, a_dt_bias, a_out_gain, a_w_out,
           kv_ada_w, kv_ada_b, kv_norm_gain, kv_w, k_norm_gain, b_w_q, q_norm_gain, b_sinks, b_w_out,
           router_w, router_b, up_w, up_b, down_w, down_b):
    bsz, seq, d = x.shape
    t = bsz * seq
    depth = ada_w.shape[0]
    x2 = x.reshape(t, d)
    ada_b3 = ada_b.reshape(depth, 1, 6 * d)

    for layer in range(depth):
        mod4 = _modulation(c, ada_w, ada_b3, layer).reshape(bsz, 6, 1, d)
        gain1 = norm_gain[layer, 0].reshape(1, d)
        gain2 = norm_gain[layer, 1].reshape(1, d)
        if layer < N_A_LAYERS:
            w_in = jnp.pad(a_w_in[layer].astype(BF16), ((0, 0), (0, A_PROJ_PAD - a_w_in.shape[-1])))
            proj = _in_proj(x2, gain1, mod4, w_in, a_conv[layer], seq)
            o = _gdn(proj, a_log[layer].reshape(1, A_HEADS),
                     a_dt_bias[layer].reshape(1, A_HEADS), a_out_gain[layer].reshape(1, A_HEAD_DIM),
                     bsz, seq)
            x2 = _out_proj_residual(o, a_w_out[layer].astype(BF16), x2, mod4, 2, seq)
        else:
            j = layer - N_A_LAYERS
            kvmod4 = _modulation(c, kv_ada_w.reshape(1, d, 2 * d), kv_ada_b.reshape(1, 1, 2 * d), 0)
            kvmod4 = kvmod4.reshape(bsz, 2, 1, d)
            inv_freq = ROPE_THETA ** (-np.arange(0, ROPE_DIM, 2, dtype=np.float32) / ROPE_DIM)
            invf = jnp.asarray(inv_freq.astype(np.float32).reshape(ROPE_DIM // 2, 1))
            q, k, v = _qkv(x2, positions.reshape(t), gain1, mod4, kv_norm_gain.reshape(1, d), kvmod4,
                           b_w_q[j].astype(BF16), kv_w.astype(BF16),
                           jnp.tile(q_norm_gain[j], 2).reshape(1, LANES),
                           jnp.tile(k_norm_gain, 2).reshape(1, LANES), invf, seq)
            x2 = _attention_residual(q, k, v, b_sinks[j].reshape(1, B_Q_HEADS), b_w_out[j].astype(BF16),
                                     x2, mod4, 2, bsz, seq)
        x2 = _moe_layer(x2, gain2, mod4, router_w, router_b, up_w, up_b, down_w, down_b, layer, seq)
    return x2.reshape(bsz, seq, d)
```

```python
import dataclasses
import functools

import jax
import jax.numpy as jnp
import numpy as np
from jax import lax
from jax.experimental import pallas as pl
from jax.experimental.pallas import tpu as pltpu
from jax.experimental.pallas import tpu_sc as plsc

F32 = jnp.float32
BF16 = jnp.bfloat16
HIGHEST = lax.Precision.HIGHEST

N_A_LAYERS = 1

A_HEAD_DIM = 128
A_HEADS = 8
A_WIDTH = 1024
A_CONV = 4
A_CHUNK = 64
A_PROJ_PAD = 4608

B_HEAD_DIM = 64
B_Q_HEADS = 16
B_KV_HEADS = 2
B_GROUP = 8
WINDOW = 128
ROPE_DIM = 16
ROPE_THETA = 500000.0

N_EXPERTS = 32
TOP_K = 4
D_FF = 1024
SWIGLU_LIMIT = 7.0
SWIGLU_ALPHA = 1.702
MOE_ROWS = 512
MOE_FIRST_PART = 4

EPS = 1e-6
LANES = 128
NEG_INF = float("-inf")

VMEM_LIMIT = 56 * 1024 * 1024


def _cparams(sem, vmem=VMEM_LIMIT):
    return pltpu.CompilerParams(dimension_semantics=sem, vmem_limit_bytes=vmem)


def _silu(x):
    return x * jax.nn.sigmoid(x)


HI_HALF = -65536


def _pack_bf16_pairs(x):
    n = x.shape[-1] // 2
    bits = lax.bitcast_convert_type(x.astype(BF16).astype(F32), jnp.int32)
    return jnp.bitwise_or(lax.shift_right_logical(bits[:, :n], 16), jnp.bitwise_and(bits[:, n:], HI_HALF))


def _unpack_bf16_pairs(w):
    lo = lax.bitcast_convert_type(lax.shift_left(w, 16), F32)
    hi = lax.bitcast_convert_type(jnp.bitwise_and(w, HI_HALF), F32)
    return lo, hi


def _ada_norm(x, gain, shift, scale):
    y = x * lax.rsqrt(jnp.mean(x * x, axis=-1, keepdims=True) + EPS)
    return (y * gain) * (1.0 + scale) + shift


def _mod_body(c_ref, w_ref, b_ref, o_ref):
    o_ref[...] = jnp.dot(_silu(c_ref[...]), w_ref[...], preferred_element_type=F32,
                         precision=HIGHEST) + b_ref[...]


def _modulation(c, w3, b3, layer):
    bsz, d = c.shape
    n = w3.shape[-1]
    tn = 1024
    return pl.pallas_call(
        _mod_body,
        out_shape=jax.ShapeDtypeStruct((bsz, n), F32),
        grid=(n // tn,),
        in_specs=[pl.BlockSpec((bsz, d), lambda j: (0, 0)),
                  pl.BlockSpec((None, d, tn), lambda j: (layer, 0, j)),
                  pl.BlockSpec((None, 1, tn), lambda j: (layer, 0, j))],
        out_specs=pl.BlockSpec((bsz, tn), lambda j: (0, j)),
        compiler_params=_cparams(("arbitrary",)),
        name="adaln_mod",
    )(c, w3, b3)


INPROJ_TM = 256
INPROJ_TN = 256
HALO = 8


def _inproj_body(x_ref, gain_ref, sh_ref, sc_ref, w_ref, conv_ref, o_ref, ext, *, tiles_per_seq):
    i = pl.program_id(0)
    tm = x_ref.shape[0]
    hd = A_HEAD_DIM
    n = o_ref.shape[-1]
    tn = INPROJ_TN
    h = _ada_norm(x_ref[...], gain_ref[...], sh_ref[...], sc_ref[...]).astype(BF16)

    @pl.when(i % tiles_per_seq == 0)
    def _():
        ext[0:HALO, :] = jnp.zeros((HALO, ext.shape[-1]), F32)

    def conv_silu_norm(raw, n0):
        cols = slice(n0, n0 + tn)
        ext[HALO:HALO + tm, cols] = raw
        acc = raw * conv_ref[A_CONV - 1:A_CONV, cols]
        for s in range(1, A_CONV):
            acc = acc + ext[HALO - s:HALO - s + tm, cols] * conv_ref[A_CONV - 1 - s:A_CONV - s, cols]
        ext[0:HALO, cols] = raw[tm - HALO:, :]
        y = _silu(acc)
        if n0 >= 2 * A_WIDTH:
            return y
        scale = hd ** -0.5 if n0 < A_WIDTH else 1.0
        heads = []
        for c0 in range(0, tn, hd):
            yh = y[:, c0:c0 + hd]
            heads.append(yh * (lax.rsqrt(jnp.sum(yh * yh, axis=-1, keepdims=True) + EPS) * scale))
        return jnp.concatenate(heads, axis=1)

    matmul = lambda n0: jnp.dot(h, w_ref[:, n0:n0 + tn], preferred_element_type=F32)
    conv_chunks = list(range(0, 3 * A_WIDTH, tn))
    plain_chunks = list(range(3 * A_WIDTH, n, tn))
    every = max(1, len(conv_chunks) // max(1, len(plain_chunks)))
    raw = matmul(conv_chunks[0])
    for idx, n0 in enumerate(conv_chunks):
        nxt = matmul(conv_chunks[idx + 1]) if idx + 1 < len(conv_chunks) else None
        if idx % every == every - 1 and plain_chunks:
            p0 = plain_chunks.pop(0)
            o_ref[:, p0:p0 + tn] = matmul(p0)
        o_ref[:, n0:n0 + tn] = conv_silu_norm(raw, n0)
        raw = nxt
    for p0 in plain_chunks:
        o_ref[:, p0:p0 + tn] = matmul(p0)


def _in_proj(x2, gain, mod4, w_bf, conv_w, seq):
    t, d = x2.shape
    n = w_bf.shape[-1]
    tm = INPROJ_TM
    tps = seq // tm
    return pl.pallas_call(
        functools.partial(_inproj_body, tiles_per_seq=tps),
        out_shape=jax.ShapeDtypeStruct((t, n), F32),
        grid=(t // tm,),
        in_specs=[pl.BlockSpec((tm, d), lambda i: (i, 0)),
                  pl.BlockSpec((1, d), lambda i: (0, 0)),
                  pl.BlockSpec((None, None, 1, d), lambda i: (i // tps, 0, 0, 0)),
                  pl.BlockSpec((None, None, 1, d), lambda i: (i // tps, 1, 0, 0)),
                  pl.BlockSpec((d, n), lambda i: (0, 0)),
                  pl.BlockSpec(conv_w.shape, lambda i: (0, 0))],
        out_specs=pl.BlockSpec((tm, n), lambda i: (i, 0)),
        scratch_shapes=[pltpu.VMEM((HALO + tm, conv_w.shape[-1]), F32)],
        compiler_params=_cparams(("arbitrary",)),
        name="gdn_in_proj",
    )(x2, gain, mod4, mod4, w_bf, conv_w)


GDN_HG = 2
GDN_RB = 256
GDN_SC = 128
GDN_SUB = 4


def _split_bf16(x):
    hi = x.astype(BF16)
    return hi, (x - hi.astype(F32)).astype(BF16)


def _mm3(a_hi, a_lo, b_hi, b_lo, dims=(((1,), (0,)), ((), ()))):
    dg = functools.partial(lax.dot_general, dimension_numbers=dims, preferred_element_type=F32)
    return dg(a_hi, b_hi) + (dg(a_lo, b_hi) + dg(a_hi, b_lo))


def _dot3(a, b, dims):
    return _mm3(*_split_bf16(a), *_split_bf16(b), dims)


def _gdn_body(alog_ref, dtb_ref, q_ref, k_ref, v_ref, z_ref, ab_ref, og_ref,
              o_ref, bs, gcs, mp, bc, qp, op):
    seq = q_ref.shape[0]
    hd = A_HEAD_DIM
    ch = A_CHUNK
    g_idx = pl.program_id(1)

    lane = lax.broadcasted_iota(jnp.int32, (GDN_RB, hd), 1)
    row = lax.broadcasted_iota(jnp.int32, (GDN_RB, hd), 0)
    row_in_chunk = row % ch

    sc_rows = GDN_SC
    per_sc = sc_rows // ch
    rb_per_step = sc_rows * GDN_SUB // GDN_RB
    head_cols = lambda hh: slice(hh * hd, (hh + 1) * hd)

    def prep_stages(j):
        for rbi in range(rb_per_step):
            r = j * rb_per_step + rbi
            start = pl.multiple_of(r * GDN_RB, GDN_RB)
            rows = pl.ds(start, GDN_RB)
            for hh in range(GDN_HG):
                head = g_idx * GDN_HG + hh
                ab = ab_ref[rows, :]
                a_col = jnp.sum(jnp.where(lane == head, ab, 0.0), axis=-1, keepdims=True)
                b_col = jnp.sum(jnp.where(lane == head + A_HEADS, ab, 0.0), axis=-1, keepdims=True)
                bs[hh, rows, :] = jnp.broadcast_to(jax.nn.sigmoid(b_col), (GDN_RB, hd))
                xa = a_col + dtb_ref[0, head]
                softplus = jnp.maximum(xa, 0.0) + jnp.log1p(jnp.exp(-jnp.abs(xa)))
                g = jnp.broadcast_to(-jnp.exp(alog_ref[0, head]) * softplus, (GDN_RB, hd))
                s = 1
                while s < ch:
                    g = g + jnp.where(row_in_chunk >= s, pltpu.roll(g, s, 0), 0.0)
                    s *= 2
                gcs[hh, rows, :] = g
                yield

    ri = lax.broadcasted_iota(jnp.int32, (sc_rows, sc_rows), 0)
    cj = lax.broadcasted_iota(jnp.int32, (sc_rows, sc_rows), 1)
    same_chunk = (ri // ch) == (cj // ch)
    incl = jnp.logical_and(same_chunk, ri >= cj)
    strict = jnp.logical_and(same_chunk, ri > cj)
    eye = jnp.where(ri == cj, 1.0, 0.0)
    chunk_of_col = lax.broadcasted_iota(jnp.int32, (hd, sc_rows), 1) // ch
    lanes_nt = (((1,), (1,)), ((), ()))
    plain = (((1,), (0,)), ((), ()))

    chunks_per_step = GDN_SUB * per_sc
    n_steps = seq // (sc_rows * GDN_SUB)

    def local_stages(i):
        chains = [(i * GDN_SUB + sub, hh) for sub in range(GDN_SUB) for hh in range(GDN_HG)]
        rows_of = lambda blk: pl.ds(pl.multiple_of(blk * sc_rows, sc_rows), sc_rows)

        def start(blk, hh):
            rows = rows_of(blk)
            k = k_ref[rows, head_cols(hh)]
            gc = gcs[hh, rows, :]
            decay = jnp.where(incl, jnp.exp(jnp.minimum(gc - gc.T[0:1, :], 0.0)), 0.0)
            k_bf = k.astype(BF16)
            kk = lax.dot_general((k * bs[hh, rows, :]).astype(BF16), k_bf, lanes_nt, preferred_element_type=F32)
            qk = lax.dot_general(q_ref[rows, head_cols(hh)].astype(BF16), k_bf, lanes_nt, preferred_element_type=F32)
            return jnp.where(strict, -(kk * decay), 0.0), (qk * decay).astype(BF16)

        started = [start(*c) for c in chains]
        yield
        qks = [s[1] for s in started]
        invs = [eye + s[0] for s in started]
        pws = [_dot3(s[0], s[0], plain) for s in started]
        yield
        for step in range(1, 6):
            for n in range(len(chains)):
                pw_hi, pw_lo = _split_bf16(pws[n])
                inv_hi, inv_lo = _split_bf16(invs[n])
                if step < 5:
                    prod = _mm3(jnp.concatenate([pw_hi, inv_hi], axis=0),
                                jnp.concatenate([pw_lo, inv_lo], axis=0), pw_hi, pw_lo)
                    pws[n] = prod[:sc_rows]
                    invs[n] = invs[n] + prod[sc_rows:]
                else:
                    invs[n] = invs[n] + _mm3(inv_hi, inv_lo, pw_hi, pw_lo)
            yield

        def solve(n, blk, hh):
            rows = rows_of(blk)
            beta = bs[hh, rows, :]
            rhs = jnp.concatenate([v_ref[rows, head_cols(hh)] * beta, k_ref[rows, head_cols(hh)] * beta * jnp.exp(gcs[hh, rows, :])], axis=1)
            return _dot3(invs[n], rhs, plain).astype(BF16)

        uws = [solve(n, *c) for n, c in enumerate(chains)]
        yield

        def finish(n, blk, hh):
            rows = rows_of(blk)
            k = k_ref[rows, head_cols(hh)]
            gc = gcs[hh, rows, :]
            res = jnp.dot(qks[n], uws[n], preferred_element_type=F32)
            op[hh, rows, :] = res[:, :hd]
            qp[hh, rows, :] = (q_ref[rows, head_cols(hh)] * jnp.exp(gc) - res[:, hd:]).astype(BF16)
            gl = jnp.concatenate(
                [jnp.broadcast_to(gc[(j + 1) * ch - 1:(j + 1) * ch, :], (ch, hd)) for j in range(per_sc)], axis=0)
            kt_t = (k * jnp.exp(gl - gc)).T
            for j in range(per_sc):
                kt_j = jnp.where(chunk_of_col == j, kt_t, 0.0).astype(BF16)
                bm = jnp.dot(kt_j, uws[n], preferred_element_type=F32)
                bc[hh, blk * per_sc + j] = bm[:, :hd]
                mp[hh, blk * per_sc + j] = bm[:, hd:].astype(BF16)

        for n, c in enumerate(chains):
            finish(n, *c)

    def scan_stages(j, states):
        for cc in range(chunks_per_step):
            c = j * chunks_per_step + cc
            r0 = pl.multiple_of(c * ch, ch)
            rows = pl.ds(r0, ch)
            for hh in range(GDN_HG):
                col0 = hh * hd
                state = states[hh]
                s_bf = state.astype(BF16)
                o = jnp.dot(qp[hh, rows, :], s_bf, preferred_element_type=F32) + op[hh, rows, :]
                g_tot = jnp.exp(gcs[hh, pl.ds(r0 + ch - 1, 1), :])
                states[hh] = state * g_tot - jnp.dot(mp[hh, c], s_bf, preferred_element_type=F32) + bc[hh, c]
                on = o * lax.rsqrt(jnp.mean(o * o, axis=-1, keepdims=True) + EPS) * og_ref[...]
                z = z_ref[rows, col0:col0 + hd]
                o_ref[rows, col0:col0 + hd] = (on * _silu(z)).astype(BF16)
            yield

    def interleave(*gens):
        live = list(gens)
        while live:
            for g in list(live):
                try:
                    next(g)
                except StopIteration:
                    live.remove(g)

    assert n_steps >= 2
    interleave(prep_stages(0))
    interleave(local_stages(0), prep_stages(1))

    def pipelined(i, states):
        states = list(states)
        interleave(local_stages(i), scan_stages(i - 1, states), prep_stages(i + 1))
        return tuple(states)

    states = list(lax.fori_loop(1, n_steps - 1, pipelined,
                                tuple(jnp.zeros((hd, hd), F32) for _ in range(GDN_HG))))
    interleave(local_stages(n_steps - 1), scan_stages(n_steps - 2, states))
    interleave(scan_stages(n_steps - 1, states))


def _gdn(proj, a_log, dt_bias, out_gain, bsz, seq):
    t = proj.shape[0]
    hd = A_HEAD_DIM
    wb = hd * GDN_HG
    ng = A_HEADS // GDN_HG
    per = A_WIDTH // wb
    n_chunks = seq // A_CHUNK
    smem = pl.BlockSpec(memory_space=pltpu.SMEM)
    seq_spec = lambda off: pl.BlockSpec((seq, wb), lambda b, g, off=off: (b, off + g))
    sc = lambda dt=F32: pltpu.VMEM((GDN_HG, seq, hd), dt)
    return pl.pallas_call(
        _gdn_body,
        out_shape=jax.ShapeDtypeStruct((t, A_WIDTH), BF16),
        grid=(bsz, ng),
        in_specs=[smem, smem,
                  seq_spec(0), seq_spec(per), seq_spec(2 * per), seq_spec(3 * per),
                  pl.BlockSpec((seq, LANES), lambda b, g: (b, 4 * A_WIDTH // LANES)),
                  pl.BlockSpec((1, hd), lambda b, g: (0, 0))],
        out_specs=pl.BlockSpec((seq, wb), lambda b, g: (b, g)),
        scratch_shapes=[sc(), sc(),
                        pltpu.VMEM((GDN_HG, n_chunks, hd, hd), BF16),
                        pltpu.VMEM((GDN_HG, n_chunks, hd, hd), F32),
                        sc(BF16), sc()],
        compiler_params=_cparams(("parallel", "parallel")),
        name="gdn_core",
    )(a_log, dt_bias, proj, proj, proj, proj, proj, out_gain)


def _oproj_body(a_ref, w_ref, x_ref, g_ref, o_ref):
    y = jnp.dot(a_ref[...], w_ref[...], preferred_element_type=F32)
    o_ref[...] = x_ref[...] + g_ref[...] * y


def _out_proj_residual(a_bf, w_bf, x2, mod4, gate_slot, seq):
    t, d = x2.shape
    kdim = a_bf.shape[-1]
    tm = 512
    tps = seq // tm
    return pl.pallas_call(
        _oproj_body,
        out_shape=jax.ShapeDtypeStruct((t, d), F32),
        grid=(t // tm,),
        in_specs=[pl.BlockSpec((tm, kdim), lambda i: (i, 0)),
                  pl.BlockSpec((kdim, d), lambda i: (0, 0)),
                  pl.BlockSpec((tm, d), lambda i: (i, 0)),
                  pl.BlockSpec((None, None, 1, d), lambda i: (i // tps, gate_slot, 0, 0))],
        out_specs=pl.BlockSpec((tm, d), lambda i: (i, 0)),
        compiler_params=_cparams(("parallel",)),
        name="out_proj_residual",
    )(a_bf, w_bf, x2, mod4)


ROUTER_TM = 1024


K_ROWS = 8


def _router_body(x_ref, gain_ref, sh_ref, sc_ref, rw_ref, rb_ref, h_ref, idx_ref, gate_ref, rank_ref,
                 cnt_ref, carry):
    i = pl.program_id(0)

    @pl.when(i == 0)
    def _():
        carry[...] = jnp.zeros_like(carry)

    tm = x_ref.shape[0]
    h = _ada_norm(x_ref[...], gain_ref[...], sh_ref[...], sc_ref[...])
    h_ref[...] = _pack_bf16_pairs(h)
    logits = _dot3(h, rw_ref[...], (((1,), (0,)), ((), ()))) + rb_ref[...]
    work = logits.T[:N_EXPERTS, :]
    expert = lax.broadcasted_iota(jnp.int32, (N_EXPERTS, tm), 0)
    tops, sels = [], []
    for _ in range(TOP_K):
        m = jnp.max(work, axis=0, keepdims=True)
        sel = jnp.min(jnp.where(work == m, expert, N_EXPERTS), axis=0, keepdims=True)
        work = jnp.where(expert == sel, NEG_INF, work)
        tops.append(m)
        sels.append(sel)
    exps = [jnp.exp(m - tops[0]) for m in tops]
    denom = exps[0] + exps[1] + exps[2] + exps[3]
    onehot = jnp.zeros((N_EXPERTS, tm), F32)
    for sel in sels:
        onehot = onehot + jnp.where(expert == sel, 1.0, 0.0)
    ti = lax.broadcasted_iota(jnp.int32, (tm, tm), 0)
    tj = lax.broadcasted_iota(jnp.int32, (tm, tm), 1)
    earlier = (ti < tj).astype(BF16)
    before = jnp.dot(onehot.astype(BF16), earlier, preferred_element_type=F32) + carry[:, 0:1]
    ranks = [jnp.sum(jnp.where(expert == sel, before, 0.0), axis=0, keepdims=True) for sel in sels]
    pad_i = jnp.zeros((K_ROWS - TOP_K, tm), jnp.int32)
    idx_ref[...] = jnp.concatenate(sels + [pad_i], axis=0)
    gate_ref[...] = jnp.concatenate([e / denom for e in exps] + [pad_i.astype(F32)], axis=0)
    rank_ref[...] = jnp.concatenate([r.astype(jnp.int32) for r in ranks] + [pad_i], axis=0)
    carry[...] = carry[...] + jnp.sum(onehot, axis=1, keepdims=True)
    cnt_ref[...] = carry[...]


def _router(x2, gain, mod4, sh_slot, sc_slot, rw_pad, rb_pad, seq):
    t, d = x2.shape
    tm = ROUTER_TM
    tps = seq // tm
    tok = lambda dt: jax.ShapeDtypeStruct((K_ROWS, t), dt)
    tok_spec = pl.BlockSpec((K_ROWS, tm), lambda i: (0, i))
    cnt_spec = pl.BlockSpec((N_EXPERTS, LANES), lambda i: (0, 0))
    return pl.pallas_call(
        _router_body,
        out_shape=(jax.ShapeDtypeStruct((t, d // 2), jnp.int32), tok(jnp.int32), tok(F32), tok(jnp.int32),
                   jax.ShapeDtypeStruct((N_EXPERTS, LANES), F32)),
        grid=(t // tm,),
        in_specs=[pl.BlockSpec((tm, d), lambda i: (i, 0)),
                  pl.BlockSpec((1, d), lambda i: (0, 0)),
                  pl.BlockSpec((None, None, 1, d), lambda i: (i // tps, sh_slot, 0, 0)),
                  pl.BlockSpec((None, None, 1, d), lambda i: (i // tps, sc_slot, 0, 0)),
                  pl.BlockSpec((d, LANES), lambda i: (0, 0)),
                  pl.BlockSpec((1, LANES), lambda i: (0, 0))],
        out_specs=(pl.BlockSpec((tm, d // 2), lambda i: (i, 0)), tok_spec, tok_spec, tok_spec, cnt_spec),
        scratch_shapes=[pltpu.VMEM((N_EXPERTS, LANES), F32)],
        compiler_params=_cparams(("arbitrary",)),
        name="moe_router",
    )(x2, gain, mod4, mod4, rw_pad, rb_pad)


SC_LANES = 16
SC_INDEX_CHUNK = 8192


def _sc_row_tokens(dest_flat, n_rows, n_tok):
    n_assign = dest_flat.shape[0]
    assert n_rows % SC_LANES == 0 and n_assign % SC_INDEX_CHUNK == 0
    n_cores = plsc.get_sparse_core_info().num_cores
    mesh = plsc.VectorSubcoreMesh(core_axis_name="c", subcore_axis_name="s")

    @functools.partial(
        pl.kernel, mesh=mesh,
        out_type=jax.ShapeDtypeStruct((n_rows,), jnp.int32),
        scratch_types=[pltpu.VMEM((n_rows,), jnp.int32), pltpu.VMEM((SC_INDEX_CHUNK,), jnp.int32)],
        compiler_params=dataclasses.replace(pltpu.CompilerParams(), needs_layout_passes=False),
    )
    def row_token_kernel(dest_hbm, out_hbm, rt_v, d_v):
        wid = lax.axis_index("s") * n_cores + lax.axis_index("c")

        @pl.when(wid == 0)
        def _():
            lanes = lax.iota(jnp.int32, SC_LANES)

            @pl.loop(0, n_rows // SC_LANES)
            def _(i):
                rt_v[pl.ds(i * SC_LANES, SC_LANES)] = lax.rem(lanes + i * SC_LANES, n_tok)

            @pl.loop(0, n_assign // SC_INDEX_CHUNK)
            def _(c):
                pltpu.sync_copy(dest_hbm.at[pl.ds(c * SC_INDEX_CHUNK, SC_INDEX_CHUNK)], d_v)

                @pl.loop(0, SC_INDEX_CHUNK // SC_LANES)
                def _(i):
                    idx = d_v[pl.ds(i * SC_LANES, SC_LANES)]
                    tok = lax.rem(lanes + (c * SC_INDEX_CHUNK + i * SC_LANES), n_tok)
                    plsc.store_scatter(rt_v, [idx], tok)

            pltpu.sync_copy(rt_v, out_hbm)

    return row_token_kernel(dest_flat)


def _ffn_body(be_ref, nu_ref, nxt_ref, slot_ref, x_ref, uw_hbm, ub_ref, dw_hbm, db_ref, *rest, layer):
    y_ref, uw_f32, dw_f32, uw_bf, dw_bf, sems = rest[-6:]
    i = pl.program_id(0)
    first = jnp.logical_and(jnp.logical_or(i == 0, be_ref[i] != be_ref[jnp.maximum(i - 1, 0)]), i < nu_ref[0])
    active = i < nu_ref[0]
    slot = slot_ref[i]

    def weight_copies(expert, s):
        return (pltpu.make_async_copy(uw_hbm.at[layer, expert], uw_f32.at[s], sems.at[s, 0]),
                pltpu.make_async_copy(dw_hbm.at[layer, expert], dw_f32.at[s], sems.at[s, 1]))

    @pl.when(jnp.logical_and(i == 0, active))
    def _():
        for cp in weight_copies(be_ref[0], 0):
            cp.start()

    @pl.when(first)
    def _():
        for cp in weight_copies(be_ref[i], slot):
            cp.wait()

        @pl.when(nxt_ref[i] >= 0)
        def _():
            for cp in weight_copies(nxt_ref[i], 1 - slot):
                cp.start()

        rows = 128
        for r0 in range(0, uw_bf.shape[0], rows):
            uw_bf[r0:r0 + rows, :] = uw_f32[slot, r0:r0 + rows, :].astype(BF16)
        for r0 in range(0, dw_bf.shape[0], rows):
            dw_bf[r0:r0 + rows, :] = dw_f32[slot, r0:r0 + rows, :].astype(BF16)

    @pl.when(active)
    def _():
        x = jnp.concatenate(_unpack_bf16_pairs(x_ref[...]), axis=1).astype(BF16)
        gu = jnp.dot(x, uw_bf[...], preferred_element_type=F32) + ub_ref[...]
        gate = jnp.minimum(gu[:, :D_FF], SWIGLU_LIMIT)
        lin = jnp.clip(gu[:, D_FF:], -SWIGLU_LIMIT, SWIGLU_LIMIT)
        act = gate * jax.nn.sigmoid(SWIGLU_ALPHA * gate) * (lin + 1.0)
        y = jnp.dot(act.astype(BF16), dw_bf[...], preferred_element_type=F32) + db_ref[...]
        y_ref[...] = _pack_bf16_pairs(y)

    @pl.when(jnp.logical_not(active))
    def _():
        y_ref[...] = jnp.zeros_like(y_ref)


def _expert_ffn(xb, blk_expert, n_used, up_w, up_b4, down_w, down_b4, layer, n_rows_total, first_blk, y_prev):
    n_rows, dp = xb.shape
    d = 2 * dp
    bm = MOE_ROWS
    n_blk = n_rows // bm
    f2 = up_w.shape[-1]
    blk = jnp.arange(n_blk, dtype=jnp.int32)
    first = jnp.logical_and(jnp.concatenate([jnp.ones((1,), bool), blk_expert[1:] != blk_expert[:-1]]),
                            blk < n_used[0])
    slot = ((jnp.cumsum(first.astype(jnp.int32)) - 1) % 2).astype(jnp.int32)
    later_first = jnp.where(first, blk, n_blk)[::-1]
    next_first = jnp.concatenate([lax.cummin(later_first)[::-1][1:], jnp.full((1,), n_blk, jnp.int32)])
    nxt = jnp.where(jnp.logical_and(first, next_first < n_blk),
                    blk_expert[jnp.minimum(next_first, n_blk - 1)], -1).astype(jnp.int32)
    grid_spec = pltpu.PrefetchScalarGridSpec(
        num_scalar_prefetch=4,
        grid=(n_blk,),
        in_specs=[pl.BlockSpec((bm, dp), lambda i, be, nu, nx, sl: (i, 0)),
                  pl.BlockSpec(memory_space=pl.ANY),
                  pl.BlockSpec((None, None, 1, f2), lambda i, be, nu, nx, sl: (layer, be[i], 0, 0)),
                  pl.BlockSpec(memory_space=pl.ANY),
                  pl.BlockSpec((None, None, 1, d), lambda i, be, nu, nx, sl: (layer, be[i], 0, 0))]
                 + ([] if y_prev is None else [pl.BlockSpec(memory_space=pl.ANY)]),
        out_specs=pl.BlockSpec((bm, dp), lambda i, be, nu, nx, sl: (i + first_blk, 0)),
        scratch_shapes=[pltpu.VMEM((2, d, f2), F32), pltpu.VMEM((2, f2 // 2, d), F32),
                        pltpu.VMEM((d, f2), BF16), pltpu.VMEM((f2 // 2, d), BF16),
                        pltpu.SemaphoreType.DMA((2, 2))],
    )
    args = (blk_expert, n_used, nxt, slot, xb, up_w, up_b4, down_w, down_b4)
    return pl.pallas_call(
        functools.partial(_ffn_body, layer=layer),
        out_shape=jax.ShapeDtypeStruct((n_rows_total, dp), jnp.int32),
        grid_spec=grid_spec,
        input_output_aliases={} if y_prev is None else {len(args): 0},
        compiler_params=_cparams(("arbitrary",)),
        name="moe_expert_ffn",
    )(*args, *(() if y_prev is None else (y_prev,)))


SC_GATHER_ROWS = 64


def _sc_gather_rows(table, idx):
    n, d = idx.shape[0], table.shape[1]
    info = plsc.get_sparse_core_info()
    n_cores, n_sub = info.num_cores, info.num_subcores
    n_workers = n_cores * n_sub
    chunk = SC_GATHER_ROWS
    per_worker = n // n_workers
    n_chunks = per_worker // chunk
    assert n_chunks * chunk * n_workers == n and n_chunks % 2 == 0
    mesh = plsc.VectorSubcoreMesh(core_axis_name="c", subcore_axis_name="s")

    @functools.partial(
        pl.kernel, mesh=mesh,
        out_type=jax.ShapeDtypeStruct((n, d), table.dtype),
        scratch_types=[pltpu.VMEM((n_chunks, chunk), jnp.int32), pltpu.VMEM((2, chunk, d), table.dtype),
                       pltpu.SemaphoreType.DMA((2,))],
    )
    def gather_kernel(table_hbm, idx_hbm, out_hbm, idx_v, rows_v, sems):
        wid = lax.axis_index("s") * n_cores + lax.axis_index("c")
        base = wid * per_worker
        pltpu.sync_copy(idx_hbm.at[wid], idx_v)

        def gather(ci, slot):
            return pltpu.make_async_copy(table_hbm.at[idx_v.at[ci]], rows_v.at[slot], sems.at[slot])

        gather(0, 0).start()

        @pl.loop(0, n_chunks, step=2)
        def _(c0):
            for slot in range(2):
                ci = c0 + slot
                gather(ci, slot).wait()

                @pl.when(ci + 1 < n_chunks)
                def _():
                    gather(ci + 1, 1 - slot).start()

                pltpu.sync_copy(rows_v.at[slot], out_hbm.at[pl.ds(base + ci * chunk, chunk)])

    return gather_kernel(table, idx.reshape(n_workers, n_chunks, chunk))


COMBINE_TM = 1024


def _combine_body(gate_ref, x_ref, g_ref, y_ref, o_ref):
    gates = gate_ref[...]
    acc_lo = acc_hi = None
    for k in range(TOP_K):
        lo, hi = _unpack_bf16_pairs(y_ref[k])
        gk = gates[:, k:k + 1]
        acc_lo = gk * lo if acc_lo is None else acc_lo + gk * lo
        acc_hi = gk * hi if acc_hi is None else acc_hi + gk * hi
    o_ref[...] = x_ref[...] + g_ref[...] * jnp.concatenate([acc_lo, acc_hi], axis=1)


def _combine(y_kt, gates, x2, mod4, gate_slot, seq):
    t, d = x2.shape
    tm = COMBINE_TM
    tps = seq // tm
    return pl.pallas_call(
        _combine_body,
        out_shape=jax.ShapeDtypeStruct((t, d), F32),
        grid=(t // tm,),
        in_specs=[pl.BlockSpec((tm, gates.shape[1]), lambda i: (i, 0)),
                  pl.BlockSpec((tm, d), lambda i: (i, 0)),
                  pl.BlockSpec((None, None, 1, d), lambda i: (i // tps, gate_slot, 0, 0)),
                  pl.BlockSpec((TOP_K, tm, d // 2), lambda i: (0, i, 0))],
        out_specs=pl.BlockSpec((tm, d), lambda i: (i, 0)),
        compiler_params=_cparams(("parallel",)),
        name="moe_combine",
    )(gates, x2, mod4, y_kt)


def _moe_layer(x2, gain, mod4, router_w, router_b, up_w, up_b, down_w, down_b, layer, seq):
    t, d = x2.shape
    rw_pad = jnp.pad(router_w[layer], ((0, 0), (0, LANES - N_EXPERTS)))
    rb_pad = jnp.pad(router_b[layer], (0, LANES - N_EXPERTS)).reshape(1, LANES)
    h2, idx, gates, rank, counts = _router(x2, gain, mod4, 3, 4, rw_pad, rb_pad, seq)

    bm = MOE_ROWS
    counts = counts[:, 0].astype(jnp.int32)
    padded = (counts + bm - 1) // bm * bm
    p_ends = jnp.cumsum(padded)
    p_starts = p_ends - padded
    n_rows = t * TOP_K + N_EXPERTS * bm
    n_blk = n_rows // bm
    onehot = idx[:TOP_K, :, None] == jnp.arange(N_EXPERTS, dtype=jnp.int32)[None, None, :]
    dest = jnp.sum(jnp.where(onehot, p_starts[None, None, :], 0), axis=-1) + rank[:TOP_K]
    dest_flat = dest.reshape(TOP_K * t)
    blk_start = jnp.arange(n_blk, dtype=jnp.int32) * bm
    blk_expert = jnp.sum((blk_start[:, None] >= p_ends[None, :]).astype(jnp.int32), axis=-1)
    blk_expert = jnp.minimum(blk_expert, N_EXPERTS - 1).astype(jnp.int32)
    n_used = (p_ends[-1:] // bm).astype(jnp.int32)

    row_token = _sc_row_tokens(dest_flat, n_rows, t)
    f2 = up_w.shape[-1]
    up_b4 = up_b.reshape(up_b.shape[0], N_EXPERTS, 1, f2)
    down_b4 = down_b.reshape(down_b.shape[0], N_EXPERTS, 1, d)
    bounds = (0, n_blk // MOE_FIRST_PART, n_blk)
    xb_parts = [_sc_gather_rows(h2, row_token[lo * bm:hi * bm]) for lo, hi in zip(bounds[:-1], bounds[1:])]
    y_rows = None
    for lo, hi, xb in zip(bounds[:-1], bounds[1:], xb_parts):
        y_rows = _expert_ffn(xb, blk_expert[lo:hi], jnp.clip(n_used - lo, 0, hi - lo), up_w, up_b4, down_w,
                             down_b4, layer, n_rows, lo, y_rows)
    y_kt = _sc_gather_rows(y_rows, dest_flat).reshape(TOP_K, t, d // 2)
    return _combine(y_kt, gates.T, x2, mod4, 5, seq)


QKV_TM = 512


def _rope_tables(pos_row, invf_col):
    tm = pos_row.shape[1]
    half = ROPE_DIM // 2
    ang = invf_col * pos_row
    expand = (lax.broadcasted_iota(jnp.int32, (half, LANES), 1) % half
              == lax.broadcasted_iota(jnp.int32, (half, LANES), 0)).astype(BF16)

    def to_lanes(x):
        hi = x.astype(BF16)
        rest = x - hi.astype(F32)
        mid = rest.astype(BF16)
        parts = (hi, mid, (rest - mid.astype(F32)).astype(BF16))
        return sum(lax.dot_general(p, expand, (((0,), (0,)), ((), ())), preferred_element_type=F32) for p in parts)

    c = to_lanes(jnp.cos(ang))
    s = to_lanes(jnp.sin(ang))
    d = lax.broadcasted_iota(jnp.int32, (tm, LANES), 1) % B_HEAD_DIM
    cos_t = jnp.where(d < ROPE_DIM, c, 1.0)
    sin_lo = jnp.where(d < half, -s, 0.0)
    sin_hi = jnp.where(jnp.logical_and(d >= half, d < ROPE_DIM), s, 0.0)
    return cos_t, sin_lo, sin_hi


def _head_norm_rope(x, gain_row, tables):
    cos_t, sin_lo, sin_hi = tables
    same_head = (lax.broadcasted_iota(jnp.int32, (LANES, LANES), 0) // B_HEAD_DIM
                 == lax.broadcasted_iota(jnp.int32, (LANES, LANES), 1) // B_HEAD_DIM).astype(BF16)
    sq_hi, sq_lo = _split_bf16(x * x)
    ss = (jnp.dot(sq_hi, same_head, preferred_element_type=F32)
          + jnp.dot(sq_lo, same_head, preferred_element_type=F32))
    xn = x * lax.rsqrt(ss * (1.0 / B_HEAD_DIM) + EPS) * gain_row
    half = ROPE_DIM // 2
    return (xn * cos_t + pltpu.roll(xn, LANES - half, 1) * sin_lo + pltpu.roll(xn, half, 1) * sin_hi)


def _qkv_body(x_ref, pos_ref, gq_ref, shq_ref, scq_ref, gkv_ref, shkv_ref, sckv_ref, wq_ref, wkv_ref,
              qg_ref, kg_ref, invf_ref, q_ref, k_ref, v_ref):
    x = x_ref[...]
    y = x * lax.rsqrt(jnp.mean(x * x, axis=-1, keepdims=True) + EPS)
    hq = ((y * gq_ref[...]) * (1.0 + scq_ref[...]) + shq_ref[...]).astype(BF16)
    hkv = ((y * gkv_ref[...]) * (1.0 + sckv_ref[...]) + shkv_ref[...]).astype(BF16)
    tables = _rope_tables(pos_ref[...].astype(F32), invf_ref[...])
    kv = jnp.dot(hkv, wkv_ref[...], preferred_element_type=F32)
    k_ref[...] = _head_norm_rope(kv[:, :LANES], kg_ref[...], tables).astype(BF16)
    v_ref[...] = kv[:, LANES:].astype(BF16)
    q = jnp.dot(hq, wq_ref[...], preferred_element_type=F32)
    scale = B_HEAD_DIM ** -0.5
    for p in range(q.shape[-1] // LANES):
        qp = _head_norm_rope(q[:, p * LANES:(p + 1) * LANES], qg_ref[...], tables)
        q_ref[:, p * LANES:(p + 1) * LANES] = (qp * scale).astype(BF16)


def _qkv(x2, pos2, gq, mod4, gkv, kvmod4, wq_bf, wkv_bf, qg2, kg2, invf, seq):
    t, d = x2.shape
    tm = QKV_TM
    tps = seq // tm
    nq = wq_bf.shape[-1]
    nkv = wkv_bf.shape[-1]
    row = lambda n: pl.BlockSpec((1, n), lambda i: (0, 0))
    modspec = lambda slot: pl.BlockSpec((None, None, 1, d), lambda i, slot=slot: (i // tps, slot, 0, 0))
    return pl.pallas_call(
        _qkv_body,
        out_shape=(jax.ShapeDtypeStruct((t, nq), BF16), jax.ShapeDtypeStruct((t, LANES), BF16),
                   jax.ShapeDtypeStruct((t, LANES), BF16)),
        grid=(t // tm,),
        in_specs=[pl.BlockSpec((tm, d), lambda i: (i, 0)),
                  pl.BlockSpec((None, 1, tm), lambda i: (i, 0, 0)),
                  row(d), modspec(0), modspec(1),
                  row(d), modspec(0), modspec(1),
                  pl.BlockSpec((d, nq), lambda i: (0, 0)),
                  pl.BlockSpec((d, nkv), lambda i: (0, 0)),
                  row(LANES), row(LANES), pl.BlockSpec(invf.shape, lambda i: (0, 0))],
        out_specs=(pl.BlockSpec((tm, nq), lambda i: (i, 0)),
                   pl.BlockSpec((tm, LANES), lambda i: (i, 0)),
                   pl.BlockSpec((tm, LANES), lambda i: (i, 0))),
        compiler_params=_cparams(("parallel",)),
        name="swa_qkv_proj",
    )(x2, pos2.reshape(t // tm, 1, tm), gq, mod4, mod4, gkv, kvmod4, kvmod4, wq_bf, wkv_bf, qg2, kg2, invf)


def _attn_body(sink_ref, q_ref, kc_ref, kp_ref, vc_ref, vp_ref, w_ref, x_ref, g_ref, o_ref):
    n = pl.program_id(1)
    w = WINDOW
    from_prev = (lax.broadcasted_iota(jnp.int32, (w, w), 1) > lax.broadcasted_iota(jnp.int32, (w, w), 0))
    prev_fill = jnp.where(n > 0, 0.0, NEG_INF)
    hd = B_HEAD_DIM
    lane = lax.broadcasted_iota(jnp.int32, (2 * w, LANES), 1)
    first = lane < hd
    kfull = jnp.concatenate([kp_ref[...], kc_ref[...]], axis=0).astype(F32)
    vfull = jnp.concatenate([vp_ref[...], vc_ref[...]], axis=0).astype(F32)
    placed = {}
    for g in range(B_KV_HEADS):
        for name, full in (("k", kfull), ("v", vfull)):
            swapped = pltpu.roll(full, hd, 1)
            own_first = full if g == 0 else swapped
            own_second = swapped if g == 0 else full
            placed[name, g, 0] = jnp.where(first, own_first, 0.0).astype(BF16)
            placed[name, g, 1] = jnp.where(first, 0.0, own_second).astype(BF16)
    pairs = B_Q_HEADS // 2
    heads = [(p, side) for p in range(pairs) for side in range(2)]
    scores = []
    for p, side in heads:
        g = (2 * p) // B_GROUP
        q_pair = q_ref[:, p * LANES:(p + 1) * LANES]
        s = lax.dot_general(q_pair, placed["k", g, side], (((1,), (1,)), ((), ())), preferred_element_type=F32)
        scores.append(jnp.where(from_prev, s[:, :w] + prev_fill, s[:, w:]))
    probs, denoms = [], []
    for (p, side), s in zip(heads, scores):
        sink = sink_ref[0, 2 * p + side]
        m = jnp.maximum(jnp.max(s, axis=-1, keepdims=True), sink)
        e = jnp.exp(s - m)
        denoms.append(jnp.sum(e, axis=-1, keepdims=True) + jnp.exp(sink - m))
        probs.append(jnp.concatenate([jnp.where(from_prev, e, 0.0), jnp.where(from_prev, 0.0, e)],
                                     axis=1).astype(BF16))
    outs = []
    for (p, side), pr, den in zip(heads, probs, denoms):
        g = (2 * p) // B_GROUP
        outs.append(jnp.dot(pr, placed["v", g, side], preferred_element_type=F32) / den)
    attn = jnp.concatenate([(outs[2 * p] + outs[2 * p + 1]).astype(BF16) for p in range(pairs)], axis=1)
    o_ref[...] = x_ref[...] + g_ref[...] * jnp.dot(attn, w_ref[...], preferred_element_type=F32)


def _attention_residual(q, k, v, sinks2, w_out_bf, x2, mod4, gate_slot, bsz, seq):
    t, nq = q.shape
    d = x2.shape[-1]
    w = WINDOW
    nb = seq // w
    cur = lambda b, n: (b * nb + n, 0)
    prev = lambda b, n: (b * nb + jnp.maximum(n - 1, 0), 0)
    return pl.pallas_call(
        _attn_body,
        out_shape=jax.ShapeDtypeStruct((t, d), F32),
        grid=(bsz, nb),
        in_specs=[pl.BlockSpec(memory_space=pltpu.SMEM),
                  pl.BlockSpec((w, nq), cur),
                  pl.BlockSpec((w, LANES), cur), pl.BlockSpec((w, LANES), prev),
                  pl.BlockSpec((w, LANES), cur), pl.BlockSpec((w, LANES), prev),
                  pl.BlockSpec((nq, d), lambda b, n: (0, 0)),
                  pl.BlockSpec((w, d), cur),
                  pl.BlockSpec((None, None, 1, d), lambda b, n: (b, gate_slot, 0, 0))],
        out_specs=pl.BlockSpec((w, d), cur),
        compiler_params=_cparams(("parallel", "parallel")),
        name="swa_sink_attention",
    )(sinks2, q, k, k, v, v, w_out_bf, x2, mod4)


def kernel(x, c, positions, ada_w, ada_b, norm_gain, a_w_in, a_conv, a_log, a_dt_bias, a_out_gain, a_w_out,
           kv_ada_w, kv_ada_b, kv_norm_gain, kv_w, k_norm_gain, b_w_q, q_norm_gain, b_sinks, b_w_out,
           router_w, router_b, up_w, up_b, down_w, down_b):
    bsz, seq, d = x.shape
    t = bsz * seq
    depth = ada_w.shape[0]
    x2 = x.reshape(t, d)
    ada_b3 = ada_b.reshape(depth, 1, 6 * d)

    for layer in range(depth):
        mod4 = _modulation(c, ada_w, ada_b3, layer).reshape(bsz, 6, 1, d)
        gain1 = norm_gain[layer, 0].reshape(1, d)
        gain2 = norm_gain[layer, 1].reshape(1, d)
        if layer < N_A_LAYERS:
            w_in = jnp.pad(a_w_in[layer].astype(BF16), ((0, 0), (0, A_PROJ_PAD - a_w_in.shape[-1])))
            proj = _in_proj(x2, gain1, mod4, w_in, a_conv[layer], seq)
            o = _gdn(proj, a_log[layer].reshape(1, A_HEADS),
                     a_dt_bias[layer].reshape(1, A_HEADS), a_out_gain[layer].reshape(1, A_HEAD_DIM),
                     bsz, seq)
            x2 = _out_proj_residual(o, a_w_out[layer].astype(BF16), x2, mod4, 2, seq)
        else:
            j = layer - N_A_LAYERS
            kvmod4 = _modulation(c, kv_ada_w.reshape(1, d, 2 * d), kv_ada_b.reshape(1, 1, 2 * d), 0)
            kvmod4 = kvmod4.reshape(bsz, 2, 1, d)
            inv_freq = ROPE_THETA ** (-np.arange(0, ROPE_DIM, 2, dtype=np.float32) / ROPE_DIM)
            invf = jnp.asarray(inv_freq.astype(np.float32).reshape(ROPE_DIM // 2, 1))
            q, k, v = _qkv(x2, positions.reshape(t), gain1, mod4, kv_norm_gain.reshape(1, d), kvmod4,
                           b_w_q[j].astype(BF16), kv_w.astype(BF16),
                           jnp.tile(q_norm_gain[j], 2).reshape(1, LANES),
                           jnp.tile(k_norm_gain, 2).reshape(1, LANES), invf, seq)
            x2 = _attention_residual(q, k, v, b_sinks[j].reshape(1, B_Q_HEADS), b_w_out[j].astype(BF16),
                                     x2, mod4, 2, bsz, seq)
        x2 = _moe_layer(x2, gain2, mod4, router_w, router_b, up_w, up_b, down_w, down_b, layer, seq)
    return x2.reshape(bsz, seq, d)
```

```python
import dataclasses
import functools

import jax
import jax.numpy as jnp
import numpy as np
from jax import lax
from jax.experimental import pallas as pl
from jax.experimental.pallas import tpu as pltpu
from jax.experimental.pallas import tpu_sc as plsc

F32 = jnp.float32
BF16 = jnp.bfloat16
HIGHEST = lax.Precision.HIGHEST

N_A_LAYERS = 1

A_HEAD_DIM = 128
A_HEADS = 8
A_WIDTH = 1024
A_CONV = 4
A_CHUNK = 64
A_PROJ_PAD = 4608

B_HEAD_DIM = 64
B_Q_HEADS = 16
B_KV_HEADS = 2
B_GROUP = 8
WINDOW = 128
ROPE_DIM = 16
ROPE_THETA = 500000.0

N_EXPERTS = 32
TOP_K = 4
D_FF = 1024
SWIGLU_LIMIT = 7.0
SWIGLU_ALPHA = 1.702
MOE_ROWS = 512
MOE_FIRST_PART = 4

EPS = 1e-6
LANES = 128
NEG_INF = float("-inf")

VMEM_LIMIT = 56 * 1024 * 1024


def _cparams(sem, vmem=VMEM_LIMIT):
    return pltpu.CompilerParams(dimension_semantics=sem, vmem_limit_bytes=vmem)


def _silu(x):
    return x * jax.nn.sigmoid(x)


HI_HALF = -65536


def _pack_bf16_pairs(x):
    n = x.shape[-1] // 2
    bits = lax.bitcast_convert_type(x.astype(BF16).astype(F32), jnp.int32)
    return jnp.bitwise_or(lax.shift_right_logical(bits[:, :n], 16), jnp.bitwise_and(bits[:, n:], HI_HALF))


def _unpack_bf16_pairs(w):
    lo = lax.bitcast_convert_type(lax.shift_left(w, 16), F32)
    hi = lax.bitcast_convert_type(jnp.bitwise_and(w, HI_HALF), F32)
    return lo, hi


def _ada_norm(x, gain, shift, scale):
    y = x * lax.rsqrt(jnp.mean(x * x, axis=-1, keepdims=True) + EPS)
    return (y * gain) * (1.0 + scale) + shift


def _mod_body(c_ref, w_ref, b_ref, o_ref):
    o_ref[...] = jnp.dot(_silu(c_ref[...]), w_ref[...], preferred_element_type=F32,
                         precision=HIGHEST) + b_ref[...]


def _modulation(c, w3, b3, layer):
    bsz, d = c.shape
    n = w3.shape[-1]
    tn = 1024
    return pl.pallas_call(
        _mod_body,
        out_shape=jax.ShapeDtypeStruct((bsz, n), F32),
        grid=(n // tn,),
        in_specs=[pl.BlockSpec((bsz, d), lambda j: (0, 0)),
                  pl.BlockSpec((None, d, tn), lambda j: (layer, 0, j)),
                  pl.BlockSpec((None, 1, tn), lambda j: (layer, 0, j))],
        out_specs=pl.BlockSpec((bsz, tn), lambda j: (0, j)),
        compiler_params=_cparams(("arbitrary",)),
        name="adaln_mod",
    )(c, w3, b3)


INPROJ_TM = 256
INPROJ_TN = 256
HALO = 8


def _inproj_body(x_ref, gain_ref, sh_ref, sc_ref, w_ref, conv_ref, o_ref, ext, *, tiles_per_seq):
    i = pl.program_id(0)
    tm = x_ref.shape[0]
    hd = A_HEAD_DIM
    n = o_ref.shape[-1]
    tn = INPROJ_TN
    h = _ada_norm(x_ref[...], gain_ref[...], sh_ref[...], sc_ref[...]).astype(BF16)

    @pl.when(i % tiles_per_seq == 0)
    def _():
        ext[0:HALO, :] = jnp.zeros((HALO, ext.shape[-1]), F32)

    def conv_silu_norm(raw, n0):
        cols = slice(n0, n0 + tn)
        ext[HALO:HALO + tm, cols] = raw
        acc = raw * conv_ref[A_CONV - 1:A_CONV, cols]
        for s in range(1, A_CONV):
            acc = acc + ext[HALO - s:HALO - s + tm, cols] * conv_ref[A_CONV - 1 - s:A_CONV - s, cols]
        ext[0:HALO, cols] = raw[tm - HALO:, :]
        y = _silu(acc)
        if n0 >= 2 * A_WIDTH:
            return y
        scale = hd ** -0.5 if n0 < A_WIDTH else 1.0
        heads = []
        for c0 in range(0, tn, hd):
            yh = y[:, c0:c0 + hd]
            heads.append(yh * (lax.rsqrt(jnp.sum(yh * yh, axis=-1, keepdims=True) + EPS) * scale))
        return jnp.concatenate(heads, axis=1)

    matmul = lambda n0: jnp.dot(h, w_ref[:, n0:n0 + tn], preferred_element_type=F32)
    conv_chunks = list(range(0, 3 * A_WIDTH, tn))
    plain_chunks = list(range(3 * A_WIDTH, n, tn))
    every = max(1, len(conv_chunks) // max(1, len(plain_chunks)))
    raw = matmul(conv_chunks[0])
    for idx, n0 in enumerate(conv_chunks):
        nxt = matmul(conv_chunks[idx + 1]) if idx + 1 < len(conv_chunks) else None
        if idx % every == every - 1 and plain_chunks:
            p0 = plain_chunks.pop(0)
            o_ref[:, p0:p0 + tn] = matmul(p0)
        o_ref[:, n0:n0 + tn] = conv_silu_norm(raw, n0)
        raw = nxt
    for p0 in plain_chunks:
        o_ref[:, p0:p0 + tn] = matmul(p0)


def _in_proj(x2, gain, mod4, w_bf, conv_w, seq):
    t, d = x2.shape
    n = w_bf.shape[-1]
    tm = INPROJ_TM
    tps = seq // tm
    return pl.pallas_call(
        functools.partial(_inproj_body, tiles_per_seq=tps),
        out_shape=jax.ShapeDtypeStruct((t, n), F32),
        grid=(t // tm,),
        in_specs=[pl.BlockSpec((tm, d), lambda i: (i, 0)),
                  pl.BlockSpec((1, d), lambda i: (0, 0)),
                  pl.BlockSpec((None, None, 1, d), lambda i: (i // tps, 0, 0, 0)),
                  pl.BlockSpec((None, None, 1, d), lambda i: (i // tps, 1, 0, 0)),
                  pl.BlockSpec((d, n), lambda i: (0, 0)),
                  pl.BlockSpec(conv_w.shape, lambda i: (0, 0))],
        out_specs=pl.BlockSpec((tm, n), lambda i: (i, 0)),
        scratch_shapes=[pltpu.VMEM((HALO + tm, conv_w.shape[-1]), F32)],
        compiler_params=_cparams(("arbitrary",)),
        name="gdn_in_proj",
    )(x2, gain, mod4, mod4, w_bf, conv_w)


GDN_HG = 2
GDN_RB = 256
GDN_SC = 128
GDN_SUB = 4


def _split_bf16(x):
    hi = x.astype(BF16)
    return hi, (x - hi.astype(F32)).astype(BF16)


def _mm3(a_hi, a_lo, b_hi, b_lo, dims=(((1,), (0,)), ((), ()))):
    dg = functools.partial(lax.dot_general, dimension_numbers=dims, preferred_element_type=F32)
    return dg(a_hi, b_hi) + (dg(a_lo, b_hi) + dg(a_hi, b_lo))


def _dot3(a, b, dims):
    return _mm3(*_split_bf16(a), *_split_bf16(b), dims)


def _gdn_body(alog_ref, dtb_ref, q_ref, k_ref, v_ref, z_ref, ab_ref, og_ref,
              o_ref, bs, gcs, mp, bc, qp, op):
    seq = q_ref.shape[0]
    hd = A_HEAD_DIM
    ch = A_CHUNK
    g_idx = pl.program_id(1)

    lane = lax.broadcasted_iota(jnp.int32, (GDN_RB, hd), 1)
    row = lax.broadcasted_iota(jnp.int32, (GDN_RB, hd), 0)
    row_in_chunk = row % ch

    sc_rows = GDN_SC
    per_sc = sc_rows // ch
    rb_per_step = sc_rows * GDN_SUB // GDN_RB
    head_cols = lambda hh: slice(hh * hd, (hh + 1) * hd)

    def prep_stages(j):
        for rbi in range(rb_per_step):
            r = j * rb_per_step + rbi
            start = pl.multiple_of(r * GDN_RB, GDN_RB)
            rows = pl.ds(start, GDN_RB)
            for hh in range(GDN_HG):
                head = g_idx * GDN_HG + hh
                ab = ab_ref[rows, :]
                a_col = jnp.sum(jnp.where(lane == head, ab, 0.0), axis=-1, keepdims=True)
                b_col = jnp.sum(jnp.where(lane == head + A_HEADS, ab, 0.0), axis=-1, keepdims=True)
                bs[hh, rows, :] = jnp.broadcast_to(jax.nn.sigmoid(b_col), (GDN_RB, hd))
                xa = a_col + dtb_ref[0, head]
                softplus = jnp.maximum(xa, 0.0) + jnp.log1p(jnp.exp(-jnp.abs(xa)))
                g = jnp.broadcast_to(-jnp.exp(alog_ref[0, head]) * softplus, (GDN_RB, hd))
                s = 1
                while s < ch:
                    g = g + jnp.where(row_in_chunk >= s, pltpu.roll(g, s, 0), 0.0)
                    s *= 2
                gcs[hh, rows, :] = g
                yield

    ri = lax.broadcasted_iota(jnp.int32, (sc_rows, sc_rows), 0)
    cj = lax.broadcasted_iota(jnp.int32, (sc_rows, sc_rows), 1)
    same_chunk = (ri // ch) == (cj // ch)
    incl = jnp.logical_and(same_chunk, ri >= cj)
    strict = jnp.logical_and(same_chunk, ri > cj)
    eye = jnp.where(ri == cj, 1.0, 0.0)
    chunk_of_col = lax.broadcasted_iota(jnp.int32, (hd, sc_rows), 1) // ch
    lanes_nt = (((1,), (1,)), ((), ()))
    plain = (((1,), (0,)), ((), ()))

    chunks_per_step = GDN_SUB * per_sc
    n_steps = seq // (sc_rows * GDN_SUB)

    def local_stages(i):
        chains = [(i * GDN_SUB + sub, hh) for sub in range(GDN_SUB) for hh in range(GDN_HG)]
        rows_of = lambda blk: pl.ds(pl.multiple_of(blk * sc_rows, sc_rows), sc_rows)

        def start(blk, hh):
            rows = rows_of(blk)
            k = k_ref[rows, head_cols(hh)]
            gc = gcs[hh, rows, :]
            decay = jnp.where(incl, jnp.exp(jnp.minimum(gc - gc.T[0:1, :], 0.0)), 0.0)
            k_bf = k.astype(BF16)
            kk = lax.dot_general((k * bs[hh, rows, :]).astype(BF16), k_bf, lanes_nt, preferred_element_type=F32)
            qk = lax.dot_general(q_ref[rows, head_cols(hh)].astype(BF16), k_bf, lanes_nt, preferred_element_type=F32)
            return jnp.where(strict, -(kk * decay), 0.0), (qk * decay).astype(BF16)

        started = [start(*c) for c in chains]
        yield
        qks = [s[1] for s in started]
        invs = [eye + s[0] for s in started]
        pws = [_dot3(s[0], s[0], plain) for s in started]
        yield
        for step in range(1, 6):
            for n in range(len(chains)):
                pw_hi, pw_lo = _split_bf16(pws[n])
                inv_hi, inv_lo = _split_bf16(invs[n])
                if step < 5:
                    prod = _mm3(jnp.concatenate([pw_hi, inv_hi], axis=0),
                                jnp.concatenate([pw_lo, inv_lo], axis=0), pw_hi, pw_lo)
                    pws[n] = prod[:sc_rows]
                    invs[n] = invs[n] + prod[sc_rows:]
                else:
                    invs[n] = invs[n] + _mm3(inv_hi, inv_lo, pw_hi, pw_lo)
            yield

        def solve(n, blk, hh):
            rows = rows_of(blk)
            beta = bs[hh, rows, :]
            rhs = jnp.concatenate([v_ref[rows, head_cols(hh)] * beta, k_ref[rows, head_cols(hh)] * beta * jnp.exp(gcs[hh, rows, :])], axis=1)
            return _dot3(invs[n], rhs, plain).astype(BF16)

        uws = [solve(n, *c) for n, c in enumerate(chains)]
        yield

        def finish(n, blk, hh):
            rows = rows_of(blk)
            k = k_ref[rows, head_cols(hh)]
            gc = gcs[hh, rows, :]
            res = jnp.dot(qks[n], uws[n], preferred_element_type=F32)
            op[hh, rows, :] = res[:, :hd]
            qp[hh, rows, :] = (q_ref[rows, head_cols(hh)] * jnp.exp(gc) - res[:, hd:]).astype(BF16)
            gl = jnp.concatenate(
                [jnp.broadcast_to(gc[(j + 1) * ch - 1:(j + 1) * ch, :], (ch, hd)) for j in range(per_sc)], axis=0)
            kt_t = (k * jnp.exp(gl - gc)).T
            for j in range(per_sc):
                kt_j = jnp.where(chunk_of_col == j, kt_t, 0.0).astype(BF16)
                bm = jnp.dot(kt_j, uws[n], preferred_element_type=F32)
                bc[hh, blk * per_sc + j] = bm[:, :hd]
                mp[hh, blk * per_sc + j] = bm[:, hd:].astype(BF16)

        for n, c in enumerate(chains):
            finish(n, *c)

    def scan_stages(j, states):
        for cc in range(chunks_per_step):
            c = j * chunks_per_step + cc
            r0 = pl.multiple_of(c * ch, ch)
            rows = pl.ds(r0, ch)
            for hh in range(GDN_HG):
                col0 = hh * hd
                state = states[hh]
                s_bf = state.astype(BF16)
                o = jnp.dot(qp[hh, rows, :], s_bf, preferred_element_type=F32) + op[hh, rows, :]
                g_tot = jnp.exp(gcs[hh, pl.ds(r0 + ch - 1, 1), :])
                states[hh] = state * g_tot - jnp.dot(mp[hh, c], s_bf, preferred_element_type=F32) + bc[hh, c]
                on = o * lax.rsqrt(jnp.mean(o * o, axis=-1, keepdims=True) + EPS) * og_ref[...]
                z = z_ref[rows, col0:col0 + hd]
                o_ref[rows, col0:col0 + hd] = (on * _silu(z)).astype(BF16)
            yield

    def interleave(*gens):
        live = list(gens)
        while live:
            for g in list(live):
                try:
                    next(g)
                except StopIteration:
                    live.remove(g)

    assert n_steps >= 2
    interleave(prep_stages(0))
    interleave(local_stages(0), prep_stages(1))

    def pipelined(i, states):
        states = list(states)
        interleave(local_stages(i), scan_stages(i - 1, states), prep_stages(i + 1))
        return tuple(states)

    states = list(lax.fori_loop(1, n_steps - 1, pipelined,
                                tuple(jnp.zeros((hd, hd), F32) for _ in range(GDN_HG))))
    interleave(local_stages(n_steps - 1), scan_stages(n_steps - 2, states))
    interleave(scan_stages(n_steps - 1, states))


def _gdn(proj, a_log, dt_bias, out_gain, bsz, seq):
    t = proj.shape[0]
    hd = A_HEAD_DIM
    wb = hd * GDN_HG
    ng = A_HEADS // GDN_HG
    per = A_WIDTH // wb
    n_chunks = seq // A_CHUNK
    smem = pl.BlockSpec(memory_space=pltpu.SMEM)
    seq_spec = lambda off: pl.BlockSpec((seq, wb), lambda b, g, off=off: (b, off + g))
    sc = lambda dt=F32: pltpu.VMEM((GDN_HG, seq, hd), dt)
    return pl.pallas_call(
        _gdn_body,
        out_shape=jax.ShapeDtypeStruct((t, A_WIDTH), BF16),
        grid=(bsz, ng),
        in_specs=[smem, smem,
                  seq_spec(0), seq_spec(per), seq_spec(2 * per), seq_spec(3 * per),
                  pl.BlockSpec((seq, LANES), lambda b, g: (b, 4 * A_WIDTH // LANES)),
                  pl.BlockSpec((1, hd), lambda b, g: (0, 0))],
        out_specs=pl.BlockSpec((seq, wb), lambda b, g: (b, g)),
        scratch_shapes=[sc(), sc(),
                        pltpu.VMEM((GDN_HG, n_chunks, hd, hd), BF16),
                        pltpu.VMEM((GDN_HG, n_chunks, hd, hd), F32),
                        sc(BF16), sc()],
        compiler_params=_cparams(("parallel", "parallel")),
        name="gdn_core",
    )(a_log, dt_bias, proj, proj, proj, proj, proj, out_gain)


def _oproj_body(a_ref, w_ref, x_ref, g_ref, o_ref):
    y = jnp.dot(a_ref[...], w_ref[...], preferred_element_type=F32)
    o_ref[...] = x_ref[...] + g_ref[...] * y


def _out_proj_residual(a_bf, w_bf, x2, mod4, gate_slot, seq):
    t, d = x2.shape
    kdim = a_bf.shape[-1]
    tm = 1024
    tps = seq // tm
    return pl.pallas_call(
        _oproj_body,
        out_shape=jax.ShapeDtypeStruct((t, d), F32),
        grid=(t // tm,),
        in_specs=[pl.BlockSpec((tm, kdim), lambda i: (i, 0)),
                  pl.BlockSpec((kdim, d), lambda i: (0, 0)),
                  pl.BlockSpec((tm, d), lambda i: (i, 0)),
                  pl.BlockSpec((None, None, 1, d), lambda i: (i // tps, gate_slot, 0, 0))],
        out_specs=pl.BlockSpec((tm, d), lambda i: (i, 0)),
        compiler_params=_cparams(("parallel",)),
        name="out_proj_residual",
    )(a_bf, w_bf, x2, mod4)


ROUTER_TM = 1024


K_ROWS = 8


def _router_body(x_ref, gain_ref, sh_ref, sc_ref, rw_ref, rb_ref, h_ref, idx_ref, gate_ref, rank_ref,
                 cnt_ref, carry):
    i = pl.program_id(0)

    @pl.when(i == 0)
    def _():
        carry[...] = jnp.zeros_like(carry)

    tm = x_ref.shape[0]
    h = _ada_norm(x_ref[...], gain_ref[...], sh_ref[...], sc_ref[...])
    h_ref[...] = _pack_bf16_pairs(h)
    logits = _dot3(h, rw_ref[...], (((1,), (0,)), ((), ()))) + rb_ref[...]
    work = logits.T[:N_EXPERTS, :]
    expert = lax.broadcasted_iota(jnp.int32, (N_EXPERTS, tm), 0)
    tops, sels = [], []
    for _ in range(TOP_K):
        m = jnp.max(work, axis=0, keepdims=True)
        sel = jnp.min(jnp.where(work == m, expert, N_EXPERTS), axis=0, keepdims=True)
        work = jnp.where(expert == sel, NEG_INF, work)
        tops.append(m)
        sels.append(sel)
    exps = [jnp.exp(m - tops[0]) for m in tops]
    denom = exps[0] + exps[1] + exps[2] + exps[3]
    onehot = jnp.zeros((N_EXPERTS, tm), F32)
    for sel in sels:
        onehot = onehot + jnp.where(expert == sel, 1.0, 0.0)
    ti = lax.broadcasted_iota(jnp.int32, (tm, tm), 0)
    tj = lax.broadcasted_iota(jnp.int32, (tm, tm), 1)
    earlier = (ti < tj).astype(BF16)
    before = jnp.dot(onehot.astype(BF16), earlier, preferred_element_type=F32) + carry[:, 0:1]
    ranks = [jnp.sum(jnp.where(expert == sel, before, 0.0), axis=0, keepdims=True) for sel in sels]
    pad_i = jnp.zeros((K_ROWS - TOP_K, tm), jnp.int32)
    idx_ref[...] = jnp.concatenate(sels + [pad_i], axis=0)
    gate_ref[...] = jnp.concatenate([e / denom for e in exps] + [pad_i.astype(F32)], axis=0)
    rank_ref[...] = jnp.concatenate([r.astype(jnp.int32) for r in ranks] + [pad_i], axis=0)
    carry[...] = carry[...] + jnp.sum(onehot, axis=1, keepdims=True)
    cnt_ref[...] = carry[...]


def _router(x2, gain, mod4, sh_slot, sc_slot, rw_pad, rb_pad, seq):
    t, d = x2.shape
    tm = ROUTER_TM
    tps = seq // tm
    tok = lambda dt: jax.ShapeDtypeStruct((K_ROWS, t), dt)
    tok_spec = pl.BlockSpec((K_ROWS, tm), lambda i: (0, i))
    cnt_spec = pl.BlockSpec((N_EXPERTS, LANES), lambda i: (0, 0))
    return pl.pallas_call(
        _router_body,
        out_shape=(jax.ShapeDtypeStruct((t, d // 2), jnp.int32), tok(jnp.int32), tok(F32), tok(jnp.int32),
                   jax.ShapeDtypeStruct((N_EXPERTS, LANES), F32)),
        grid=(t // tm,),
        in_specs=[pl.BlockSpec((tm, d), lambda i: (i, 0)),
                  pl.BlockSpec((1, d), lambda i: (0, 0)),
                  pl.BlockSpec((None, None, 1, d), lambda i: (i // tps, sh_slot, 0, 0)),
                  pl.BlockSpec((None, None, 1, d), lambda i: (i // tps, sc_slot, 0, 0)),
                  pl.BlockSpec((d, LANES), lambda i: (0, 0)),
                  pl.BlockSpec((1, LANES), lambda i: (0, 0))],
        out_specs=(pl.BlockSpec((tm, d // 2), lambda i: (i, 0)), tok_spec, tok_spec, tok_spec, cnt_spec),
        scratch_shapes=[pltpu.VMEM((N_EXPERTS, LANES), F32)],
        compiler_params=_cparams(("arbitrary",)),
        name="moe_router",
    )(x2, gain, mod4, mod4, rw_pad, rb_pad)


SC_LANES = 16
SC_INDEX_CHUNK = 8192


def _sc_row_tokens(dest_flat, n_rows, n_tok):
    n_assign = dest_flat.shape[0]
    assert n_rows % SC_LANES == 0 and n_assign % SC_INDEX_CHUNK == 0
    n_cores = plsc.get_sparse_core_info().num_cores
    mesh = plsc.VectorSubcoreMesh(core_axis_name="c", subcore_axis_name="s")

    @functools.partial(
        pl.kernel, mesh=mesh,
        out_type=jax.ShapeDtypeStruct((n_rows,), jnp.int32),
        scratch_types=[pltpu.VMEM((n_rows,), jnp.int32), pltpu.VMEM((SC_INDEX_CHUNK,), jnp.int32)],
        compiler_params=dataclasses.replace(pltpu.CompilerParams(), needs_layout_passes=False),
    )
    def row_token_kernel(dest_hbm, out_hbm, rt_v, d_v):
        wid = lax.axis_index("s") * n_cores + lax.axis_index("c")

        @pl.when(wid == 0)
        def _():
            lanes = lax.iota(jnp.int32, SC_LANES)

            @pl.loop(0, n_rows // SC_LANES)
            def _(i):
                rt_v[pl.ds(i * SC_LANES, SC_LANES)] = lax.rem(lanes + i * SC_LANES, n_tok)

            @pl.loop(0, n_assign // SC_INDEX_CHUNK)
            def _(c):
                pltpu.sync_copy(dest_hbm.at[pl.ds(c * SC_INDEX_CHUNK, SC_INDEX_CHUNK)], d_v)

                @pl.loop(0, SC_INDEX_CHUNK // SC_LANES)
                def _(i):
                    idx = d_v[pl.ds(i * SC_LANES, SC_LANES)]
                    tok = lax.rem(lanes + (c * SC_INDEX_CHUNK + i * SC_LANES), n_tok)
                    plsc.store_scatter(rt_v, [idx], tok)

            pltpu.sync_copy(rt_v, out_hbm)

    return row_token_kernel(dest_flat)


def _ffn_body(be_ref, nu_ref, nxt_ref, slot_ref, x_ref, uw_hbm, ub_ref, dw_hbm, db_ref, *rest, layer):
    y_ref, uw_f32, dw_f32, uw_bf, dw_bf, sems = rest[-6:]
    i = pl.program_id(0)
    first = jnp.logical_and(jnp.logical_or(i == 0, be_ref[i] != be_ref[jnp.maximum(i - 1, 0)]), i < nu_ref[0])
    active = i < nu_ref[0]
    slot = slot_ref[i]

    def weight_copies(expert, s):
        return (pltpu.make_async_copy(uw_hbm.at[layer, expert], uw_f32.at[s], sems.at[s, 0]),
                pltpu.make_async_copy(dw_hbm.at[layer, expert], dw_f32.at[s], sems.at[s, 1]))

    @pl.when(jnp.logical_and(i == 0, active))
    def _():
        for cp in weight_copies(be_ref[0], 0):
            cp.start()

    @pl.when(first)
    def _():
        for cp in weight_copies(be_ref[i], slot):
            cp.wait()

        @pl.when(nxt_ref[i] >= 0)
        def _():
            for cp in weight_copies(nxt_ref[i], 1 - slot):
                cp.start()

        rows = 128
        for r0 in range(0, uw_bf.shape[0], rows):
            uw_bf[r0:r0 + rows, :] = uw_f32[slot, r0:r0 + rows, :].astype(BF16)
        for r0 in range(0, dw_bf.shape[0], rows):
            dw_bf[r0:r0 + rows, :] = dw_f32[slot, r0:r0 + rows, :].astype(BF16)

    @pl.when(active)
    def _():
        x = jnp.concatenate(_unpack_bf16_pairs(x_ref[...]), axis=1).astype(BF16)
        gu = jnp.dot(x, uw_bf[...], preferred_element_type=F32) + ub_ref[...]
        gate = jnp.minimum(gu[:, :D_FF], SWIGLU_LIMIT)
        lin = jnp.clip(gu[:, D_FF:], -SWIGLU_LIMIT, SWIGLU_LIMIT)
        act = gate * jax.nn.sigmoid(SWIGLU_ALPHA * gate) * (lin + 1.0)
        y = jnp.dot(act.astype(BF16), dw_bf[...], preferred_element_type=F32) + db_ref[...]
        y_ref[...] = _pack_bf16_pairs(y)

    @pl.when(jnp.logical_not(active))
    def _():
        y_ref[...] = jnp.zeros_like(y_ref)


def _expert_ffn(xb, blk_expert, n_used, up_w, up_b4, down_w, down_b4, layer, n_rows_total, first_blk, y_prev):
    n_rows, dp = xb.shape
    d = 2 * dp
    bm = MOE_ROWS
    n_blk = n_rows // bm
    f2 = up_w.shape[-1]
    blk = jnp.arange(n_blk, dtype=jnp.int32)
    first = jnp.logical_and(jnp.concatenate([jnp.ones((1,), bool), blk_expert[1:] != blk_expert[:-1]]),
                            blk < n_used[0])
    slot = ((jnp.cumsum(first.astype(jnp.int32)) - 1) % 2).astype(jnp.int32)
    later_first = jnp.where(first, blk, n_blk)[::-1]
    next_first = jnp.concatenate([lax.cummin(later_first)[::-1][1:], jnp.full((1,), n_blk, jnp.int32)])
    nxt = jnp.where(jnp.logical_and(first, next_first < n_blk),
                    blk_expert[jnp.minimum(next_first, n_blk - 1)], -1).astype(jnp.int32)
    grid_spec = pltpu.PrefetchScalarGridSpec(
        num_scalar_prefetch=4,
        grid=(n_blk,),
        in_specs=[pl.BlockSpec((bm, dp), lambda i, be, nu, nx, sl: (i, 0)),
                  pl.BlockSpec(memory_space=pl.ANY),
                  pl.BlockSpec((None, None, 1, f2), lambda i, be, nu, nx, sl: (layer, be[i], 0, 0)),
                  pl.BlockSpec(memory_space=pl.ANY),
                  pl.BlockSpec((None, None, 1, d), lambda i, be, nu, nx, sl: (layer, be[i], 0, 0))]
                 + ([] if y_prev is None else [pl.BlockSpec(memory_space=pl.ANY)]),
        out_specs=pl.BlockSpec((bm, dp), lambda i, be, nu, nx, sl: (i + first_blk, 0)),
        scratch_shapes=[pltpu.VMEM((2, d, f2), F32), pltpu.VMEM((2, f2 // 2, d), F32),
                        pltpu.VMEM((d, f2), BF16), pltpu.VMEM((f2 // 2, d), BF16),
                        pltpu.SemaphoreType.DMA((2, 2))],
    )
    args = (blk_expert, n_used, nxt, slot, xb, up_w, up_b4, down_w, down_b4)
    return pl.pallas_call(
        functools.partial(_ffn_body, layer=layer),
        out_shape=jax.ShapeDtypeStruct((n_rows_total, dp), jnp.int32),
        grid_spec=grid_spec,
        input_output_aliases={} if y_prev is None else {len(args): 0},
        compiler_params=_cparams(("arbitrary",)),
        name="moe_expert_ffn",
    )(*args, *(() if y_prev is None else (y_prev,)))


SC_GATHER_ROWS = 64


def _sc_gather_rows(table, idx):
    n, d = idx.shape[0], table.shape[1]
    info = plsc.get_sparse_core_info()
    n_cores, n_sub = info.num_cores, info.num_subcores
    n_workers = n_cores * n_sub
    chunk = SC_GATHER_ROWS
    per_worker = n // n_workers
    n_chunks = per_worker // chunk
    assert n_chunks * chunk * n_workers == n and n_chunks % 2 == 0
    mesh = plsc.VectorSubcoreMesh(core_axis_name="c", subcore_axis_name="s")

    @functools.partial(
        pl.kernel, mesh=mesh,
        out_type=jax.ShapeDtypeStruct((n, d), table.dtype),
        scratch_types=[pltpu.VMEM((n_chunks, chunk), jnp.int32), pltpu.VMEM((2, chunk, d), table.dtype),
                       pltpu.SemaphoreType.DMA((2,))],
    )
    def gather_kernel(table_hbm, idx_hbm, out_hbm, idx_v, rows_v, sems):
        wid = lax.axis_index("s") * n_cores + lax.axis_index("c")
        base = wid * per_worker
        pltpu.sync_copy(idx_hbm.at[wid], idx_v)

        def gather(ci, slot):
            return pltpu.make_async_copy(table_hbm.at[idx_v.at[ci]], rows_v.at[slot], sems.at[slot])

        gather(0, 0).start()

        @pl.loop(0, n_chunks, step=2)
        def _(c0):
            for slot in range(2):
                ci = c0 + slot
                gather(ci, slot).wait()

                @pl.when(ci + 1 < n_chunks)
                def _():
                    gather(ci + 1, 1 - slot).start()

                pltpu.sync_copy(rows_v.at[slot], out_hbm.at[pl.ds(base + ci * chunk, chunk)])

    return gather_kernel(table, idx.reshape(n_workers, n_chunks, chunk))


COMBINE_TM = 1024


def _combine_body(gate_ref, x_ref, g_ref, y_ref, o_ref):
    gates = gate_ref[...]
    acc_lo = acc_hi = None
    for k in range(TOP_K):
        lo, hi = _unpack_bf16_pairs(y_ref[k])
        gk = gates[:, k:k + 1]
        acc_lo = gk * lo if acc_lo is None else acc_lo + gk * lo
        acc_hi = gk * hi if acc_hi is None else acc_hi + gk * hi
    o_ref[...] = x_ref[...] + g_ref[...] * jnp.concatenate([acc_lo, acc_hi], axis=1)


def _combine(y_kt, gates, x2, mod4, gate_slot, seq):
    t, d = x2.shape
    tm = COMBINE_TM
    tps = seq // tm
    return pl.pallas_call(
        _combine_body,
        out_shape=jax.ShapeDtypeStruct((t, d), F32),
        grid=(t // tm,),
        in_specs=[pl.BlockSpec((tm, gates.shape[1]), lambda i: (i, 0)),
                  pl.BlockSpec((tm, d), lambda i: (i, 0)),
                  pl.BlockSpec((None, None, 1, d), lambda i: (i // tps, gate_slot, 0, 0)),
                  pl.BlockSpec((TOP_K, tm, d // 2), lambda i: (0, i, 0))],
        out_specs=pl.BlockSpec((tm, d), lambda i: (i, 0)),
        compiler_params=_cparams(("parallel",)),
        name="moe_combine",
    )(gates, x2, mod4, y_kt)


def _moe_layer(x2, gain, mod4, router_w, router_b, up_w, up_b, down_w, down_b, layer, seq):
    t, d = x2.shape
    rw_pad = jnp.pad(router_w[layer], ((0, 0), (0, LANES - N_EXPERTS)))
    rb_pad = jnp.pad(router_b[layer], (0, LANES - N_EXPERTS)).reshape(1, LANES)
    h2, idx, gates, rank, counts = _router(x2, gain, mod4, 3, 4, rw_pad, rb_pad, seq)

    bm = MOE_ROWS
    counts = counts[:, 0].astype(jnp.int32)
    padded = (counts + bm - 1) // bm * bm
    p_ends = jnp.cumsum(padded)
    p_starts = p_ends - padded
    n_rows = t * TOP_K + N_EXPERTS * bm
    n_blk = n_rows // bm
    onehot = idx[:TOP_K, :, None] == jnp.arange(N_EXPERTS, dtype=jnp.int32)[None, None, :]
    dest = jnp.sum(jnp.where(onehot, p_starts[None, None, :], 0), axis=-1) + rank[:TOP_K]
    dest_flat = dest.reshape(TOP_K * t)
    blk_start = jnp.arange(n_blk, dtype=jnp.int32) * bm
    blk_expert = jnp.sum((blk_start[:, None] >= p_ends[None, :]).astype(jnp.int32), axis=-1)
    blk_expert = jnp.minimum(blk_expert, N_EXPERTS - 1).astype(jnp.int32)
    n_used = (p_ends[-1:] // bm).astype(jnp.int32)

    row_token = _sc_row_tokens(dest_flat, n_rows, t)
    f2 = up_w.shape[-1]
    up_b4 = up_b.reshape(up_b.shape[0], N_EXPERTS, 1, f2)
    down_b4 = down_b.reshape(down_b.shape[0], N_EXPERTS, 1, d)
    bounds = (0, n_blk // MOE_FIRST_PART, n_blk)
    xb_parts = [_sc_gather_rows(h2, row_token[lo * bm:hi * bm]) for lo, hi in zip(bounds[:-1], bounds[1:])]
    y_rows = None
    for lo, hi, xb in zip(bounds[:-1], bounds[1:], xb_parts):
        y_rows = _expert_ffn(xb, blk_expert[lo:hi], jnp.clip(n_used - lo, 0, hi - lo), up_w, up_b4, down_w,
                             down_b4, layer, n_rows, lo, y_rows)
    y_kt = _sc_gather_rows(y_rows, dest_flat).reshape(TOP_K, t, d // 2)
    return _combine(y_kt, gates.T, x2, mod4, 5, seq)


QKV_TM = 1024


def _rope_tables(pos_row, invf_col):
    tm = pos_row.shape[1]
    half = ROPE_DIM // 2
    ang = invf_col * pos_row
    expand = (lax.broadcasted_iota(jnp.int32, (half, LANES), 1) % half
              == lax.broadcasted_iota(jnp.int32, (half, LANES), 0)).astype(BF16)

    def to_lanes(x):
        hi = x.astype(BF16)
        rest = x - hi.astype(F32)
        mid = rest.astype(BF16)
        parts = (hi, mid, (rest - mid.astype(F32)).astype(BF16))
        return sum(lax.dot_general(p, expand, (((0,), (0,)), ((), ())), preferred_element_type=F32) for p in parts)

    c = to_lanes(jnp.cos(ang))
    s = to_lanes(jnp.sin(ang))
    d = lax.broadcasted_iota(jnp.int32, (tm, LANES), 1) % B_HEAD_DIM
    cos_t = jnp.where(d < ROPE_DIM, c, 1.0)
    sin_lo = jnp.where(d < half, -s, 0.0)
    sin_hi = jnp.where(jnp.logical_and(d >= half, d < ROPE_DIM), s, 0.0)
    return cos_t, sin_lo, sin_hi


def _head_norm_rope(x, gain_row, tables):
    cos_t, sin_lo, sin_hi = tables
    same_head = (lax.broadcasted_iota(jnp.int32, (LANES, LANES), 0) // B_HEAD_DIM
                 == lax.broadcasted_iota(jnp.int32, (LANES, LANES), 1) // B_HEAD_DIM).astype(BF16)
    sq_hi, sq_lo = _split_bf16(x * x)
    ss = (jnp.dot(sq_hi, same_head, preferred_element_type=F32)
          + jnp.dot(sq_lo, same_head, preferred_element_type=F32))
    xn = x * lax.rsqrt(ss * (1.0 / B_HEAD_DIM) + EPS) * gain_row
    half = ROPE_DIM // 2
    return (xn * cos_t + pltpu.roll(xn, LANES - half, 1) * sin_lo + pltpu.roll(xn, half, 1) * sin_hi)


def _qkv_body(x_ref, pos_ref, gq_ref, shq_ref, scq_ref, gkv_ref, shkv_ref, sckv_ref, wq_ref, wkv_ref,
              qg_ref, kg_ref, invf_ref, q_ref, k_ref, v_ref):
    x = x_ref[...]
    y = x * lax.rsqrt(jnp.mean(x * x, axis=-1, keepdims=True) + EPS)
    hq = ((y * gq_ref[...]) * (1.0 + scq_ref[...]) + shq_ref[...]).astype(BF16)
    hkv = ((y * gkv_ref[...]) * (1.0 + sckv_ref[...]) + shkv_ref[...]).astype(BF16)
    tables = _rope_tables(pos_ref[...].astype(F32), invf_ref[...])
    kv = jnp.dot(hkv, wkv_ref[...], preferred_element_type=F32)
    k_ref[...] = _head_norm_rope(kv[:, :LANES], kg_ref[...], tables).astype(BF16)
    v_ref[...] = kv[:, LANES:].astype(BF16)
    q = jnp.dot(hq, wq_ref[...], preferred_element_type=F32)
    scale = B_HEAD_DIM ** -0.5
    for p in range(q.shape[-1] // LANES):
        qp = _head_norm_rope(q[:, p * LANES:(p + 1) * LANES], qg_ref[...], tables)
        q_ref[:, p * LANES:(p + 1) * LANES] = (qp * scale).astype(BF16)


def _qkv(x2, pos2, gq, mod4, gkv, kvmod4, wq_bf, wkv_bf, qg2, kg2, invf, seq):
    t, d = x2.shape
    tm = QKV_TM
    tps = seq // tm
    nq = wq_bf.shape[-1]
    nkv = wkv_bf.shape[-1]
    row = lambda n: pl.BlockSpec((1, n), lambda i: (0, 0))
    modspec = lambda slot: pl.BlockSpec((None, None, 1, d), lambda i, slot=slot: (i // tps, slot, 0, 0))
    return pl.pallas_call(
        _qkv_body,
        out_shape=(jax.ShapeDtypeStruct((t, nq), BF16), jax.ShapeDtypeStruct((t, LANES), BF16),
                   jax.ShapeDtypeStruct((t, LANES), BF16)),
        grid=(t // tm,),
        in_specs=[pl.BlockSpec((tm, d), lambda i: (i, 0)),
                  pl.BlockSpec((None, 1, tm), lambda i: (i, 0, 0)),
                  row(d), modspec(0), modspec(1),
                  row(d), modspec(0), modspec(1),
                  pl.BlockSpec((d, nq), lambda i: (0, 0)),
                  pl.BlockSpec((d, nkv), lambda i: (0, 0)),
                  row(LANES), row(LANES), pl.BlockSpec(invf.shape, lambda i: (0, 0))],
        out_specs=(pl.BlockSpec((tm, nq), lambda i: (i, 0)),
                   pl.BlockSpec((tm, LANES), lambda i: (i, 0)),
                   pl.BlockSpec((tm, LANES), lambda i: (i, 0))),
        compiler_params=_cparams(("parallel",)),
        name="swa_qkv_proj",
    )(x2, pos2.reshape(t // tm, 1, tm), gq, mod4, mod4, gkv, kvmod4, kvmod4, wq_bf, wkv_bf, qg2, kg2, invf)


def _attn_body(sink_ref, q_ref, kc_ref, kp_ref, vc_ref, vp_ref, w_ref, x_ref, g_ref, o_ref):
    n = pl.program_id(1)
    w = WINDOW
    from_prev = (lax.broadcasted_iota(jnp.int32, (w, w), 1) > lax.broadcasted_iota(jnp.int32, (w, w), 0))
    prev_fill = jnp.where(n > 0, 0.0, NEG_INF)
    hd = B_HEAD_DIM
    lane = lax.broadcasted_iota(jnp.int32, (2 * w, LANES), 1)
    first = lane < hd
    kfull = jnp.concatenate([kp_ref[...], kc_ref[...]], axis=0).astype(F32)
    vfull = jnp.concatenate([vp_ref[...], vc_ref[...]], axis=0).astype(F32)
    placed = {}
    for g in range(B_KV_HEADS):
        for name, full in (("k", kfull), ("v", vfull)):
            swapped = pltpu.roll(full, hd, 1)
            own_first = full if g == 0 else swapped
            own_second = swapped if g == 0 else full
            placed[name, g, 0] = jnp.where(first, own_first, 0.0).astype(BF16)
            placed[name, g, 1] = jnp.where(first, 0.0, own_second).astype(BF16)
    pairs = B_Q_HEADS // 2
    heads = [(p, side) for p in range(pairs) for side in range(2)]
    scores = []
    for p, side in heads:
        g = (2 * p) // B_GROUP
        q_pair = q_ref[:, p * LANES:(p + 1) * LANES]
        s = lax.dot_general(q_pair, placed["k", g, side], (((1,), (1,)), ((), ())), preferred_element_type=F32)
        scores.append(jnp.where(from_prev, s[:, :w] + prev_fill, s[:, w:]))
    probs, denoms = [], []
    for (p, side), s in zip(heads, scores):
        sink = sink_ref[0, 2 * p + side]
        m = jnp.maximum(jnp.max(s, axis=-1, keepdims=True), sink)
        e = jnp.exp(s - m)
        denoms.append(jnp.sum(e, axis=-1, keepdims=True) + jnp.exp(sink - m))
        probs.append(jnp.concatenate([jnp.where(from_prev, e, 0.0), jnp.where(from_prev, 0.0, e)],
                                     axis=1).astype(BF16))
    outs = []
    for (p, side), pr, den in zip(heads, probs, denoms):
        g = (2 * p) // B_GROUP
        outs.append(jnp.dot(pr, placed["v", g, side], preferred_element_type=F32) / den)
    attn = jnp.concatenate([(outs[2 * p] + outs[2 * p + 1]).astype(BF16) for p in range(pairs)], axis=1)
    o_ref[...] = x_ref[...] + g_ref[...] * jnp.dot(attn, w_ref[...], preferred_element_type=F32)


def _attention_residual(q, k, v, sinks2, w_out_bf, x2, mod4, gate_slot, bsz, seq):
    t, nq = q.shape
    d = x2.shape[-1]
    w = WINDOW
    nb = seq // w
    cur = lambda b, n: (b * nb + n, 0)
    prev = lambda b, n: (b * nb + jnp.maximum(n - 1, 0), 0)
    return pl.pallas_call(
        _attn_body,
        out_shape=jax.ShapeDtypeStruct((t, d), F32),
        grid=(bsz, nb),
        in_specs=[pl.BlockSpec(memory_space=pltpu.SMEM),
                  pl.BlockSpec((w, nq), cur),
                  pl.BlockSpec((w, LANES), cur), pl.BlockSpec((w, LANES), prev),
                  pl.BlockSpec((w, LANES), cur), pl.BlockSpec((w, LANES), prev),
                  pl.BlockSpec((nq, d), lambda b, n: (0, 0)),
                  pl.BlockSpec((w, d), cur),
                  pl.BlockSpec((None, None, 1, d), lambda b, n: (b, gate_slot, 0, 0))],
        out_specs=pl.BlockSpec((w, d), cur),
        compiler_params=_cparams(("parallel", "parallel")),
        name="swa_sink_attention",
    )(sinks2, q, k, k, v, v, w_out_bf, x2, mod4)


def kernel(x, c, positions, ada_w, ada_b, norm_gain, a_w_in, a_conv, a_log, a_dt_bias, a_out_gain, a_w_out,
           kv_ada_w, kv_ada_b, kv_norm_gain, kv_w, k_norm_gain, b_w_q, q_norm_gain, b_sinks, b_w_out,
           router_w, router_b, up_w, up_b, down_w, down_b):
    bsz, seq, d = x.shape
    t = bsz * seq
    depth = ada_w.shape[0]
    x2 = x.reshape(t, d)
    ada_b3 = ada_b.reshape(depth, 1, 6 * d)

    for layer in range(depth):
        mod4 = _modulation(c, ada_w, ada_b3, layer).reshape(bsz, 6, 1, d)
        gain1 = norm_gain[layer, 0].reshape(1, d)
        gain2 = norm_gain[layer, 1].reshape(1, d)
        if layer < N_A_LAYERS:
            w_in = jnp.pad(a_w_in[layer].astype(BF16), ((0, 0), (0, A_PROJ_PAD - a_w_in.shape[-1])))
            proj = _in_proj(x2, gain1, mod4, w_in, a_conv[layer], seq)
            o = _gdn(proj, a_log[layer].reshape(1, A_HEADS),
                     a_dt_bias[layer].reshape(1, A_HEADS), a_out_gain[layer].reshape(1, A_HEAD_DIM),
                     bsz, seq)
            x2 = _out_proj_residual(o, a_w_out[layer].astype(BF16), x2, mod4, 2, seq)
        else:
            j = layer - N_A_LAYERS
            kvmod4 = _modulation(c, kv_ada_w.reshape(1, d, 2 * d), kv_ada_b.reshape(1, 1, 2 * d), 0)
            kvmod4 = kvmod4.reshape(bsz, 2, 1, d)
            inv_freq = ROPE_THETA ** (-np.arange(0, ROPE_DIM, 2, dtype=np.float32) / ROPE_DIM)
            invf = jnp.asarray(inv_freq.astype(np.float32).reshape(ROPE_DIM // 2, 1))
            q, k, v = _qkv(x2, positions.reshape(t), gain1, mod4, kv_norm_gain.reshape(1, d), kvmod4,
                           b_w_q[j].astype(BF16), kv_w.astype(BF16),
                           jnp.tile(q_norm_gain[j], 2).reshape(1, LANES),
                           jnp.tile(k_norm_gain, 2).reshape(1, LANES), invf, seq)
            x2 = _attention_residual(q, k, v, b_sinks[j].reshape(1, B_Q_HEADS), b_w_out[j].astype(BF16),
                                     x2, mod4, 2, bsz, seq)
        x2 = _moe_layer(x2, gain2, mod4, router_w, router_b, up_w, up_b, down_w, down_b, layer, seq)
    return x2.reshape(bsz, seq, d)
```
